```python
import math
import jax, jax.numpy as jnp
from jax import lax
import numpy as np


D_MODEL = 1024
BATCH = 8
SEQ = 2048
DEPTH = 2

N_A_LAYERS = DEPTH // 2
N_B_LAYERS = DEPTH - N_A_LAYERS

A_CHUNK = 128
A_GROUPS = 8
A_WIDTH = 2 * D_MODEL
A_GROUP_DIM = A_WIDTH // A_GROUPS

B_HEADS = 16
B_HEAD_DIM = D_MODEL // B_HEADS
B_BLOCK = 256
B_TOPK = 3
B_QCHUNK = 32

N_EXPERTS = 64
TOP_K = 8
N_GROUPS = 8
TOPK_GROUPS = 4
EXPERT_DIM = 256
SHARED_DIM = 256
ROUTED_SCALE = 2.5
MOE_ROW_BLOCK = 128

RMS_EPS = 1e-6
LN_EPS = 1e-5
NEG_INF = -1e30

kernel_name = 'hybrid_gmlp_moba_moe_yoco'


def rmsnorm(x, g):
    xf = x.astype(jnp.float32)
    y = xf * lax.rsqrt(jnp.mean(xf * xf, axis=-1, keepdims=True) + RMS_EPS)
    return y.astype(x.dtype) * g


def layernorm(x, g, b):
    xf = x.astype(jnp.float32)
    mu = jnp.mean(xf, axis=-1, keepdims=True)
    var = jnp.mean(jnp.square(xf - mu), axis=-1, keepdims=True)
    y = (xf - mu) * lax.rsqrt(var + LN_EPS)
    return y.astype(x.dtype) * g + b


def ada(c, w, b, n):
    m = jax.nn.silu(c) @ w + b
    return [t[:, None, :] for t in jnp.split(m, n, axis=-1)]


def modulate(h, shift, scale):
    return h * (1 + scale) + shift


def swiglu(x, wg, wu, wd):
    return (jax.nn.silu(x @ wg) * (x @ wu)) @ wd


def gmlp_mixer(h, w_in, b_in, ln_g, ln_b, w_s, b_s, w_out):
    B, S, _ = h.shape
    z = jax.nn.gelu(h @ w_in + b_in)
    u, v = jnp.split(z, 2, axis=-1)
    v = layernorm(v, ln_g, ln_b)
    nc = S // A_CHUNK
    v = v.reshape(B, nc, A_CHUNK, A_GROUPS, A_GROUP_DIM)
    causal = jnp.tril(jnp.ones((A_CHUNK, A_CHUNK), dtype=bool))
    w = jnp.where(causal[None], w_s, 0)
    sv = jnp.einsum('gts,bcsgd->bctgd', w, v) + b_s.T[None, None, :, :, None]
    y = u * sv.reshape(B, S, A_WIDTH)
    return y @ w_out


def shared_kv(x, c, g, w_ada, b_ada, w_k, w_v):
    shift, scale = ada(c, w_ada, b_ada, 2)
    h = modulate(rmsnorm(x, g), shift, scale)
    B, S, _ = h.shape
    k = (h @ w_k).reshape(B, S, B_HEADS, B_HEAD_DIM).transpose(0, 2, 1, 3)
    v = (h @ w_v).reshape(B, S, B_HEADS, B_HEAD_DIM).transpose(0, 2, 1, 3)
    nb = -(-S // B_BLOCK)
    pad = nb * B_BLOCK - S
    k = jnp.pad(k, ((0, 0), (0, 0), (0, pad), (0, 0)))
    v = jnp.pad(v, ((0, 0), (0, 0), (0, pad), (0, 0)))
    kb = k.reshape(B, B_HEADS, nb, B_BLOCK, B_HEAD_DIM)
    vb = v.reshape(B, B_HEADS, nb, B_BLOCK, B_HEAD_DIM)
    kmean = jnp.mean(kb, axis=3)
    return kb, vb, kmean


def moba_attention(q, kb, vb, kmean):
    B, H, S, dh = q.shape
    nb = kb.shape[2]
    n_sel = min(B_TOPK, nb - 1)
    scale = dh ** -0.5
    nqc = S // B_QCHUNK
    qc = q.reshape(B, H, nqc, B_QCHUNK, dh).transpose(0, 2, 1, 3, 4)
    head_idx = jnp.arange(H)[:, None, None]

    def one_seq(args):
        q_s, kb_s, vb_s, km_s = args

        def one_chunk(args2):
            ci, q_c = args2
            start = ci * B_QCHUNK
            own = start // B_BLOCK
            qpos = start + jnp.arange(B_QCHUNK)
            kpos = own * B_BLOCK + jnp.arange(B_BLOCK)
            k_own = lax.dynamic_index_in_dim(kb_s, own, axis=1, keepdims=False)
            v_own = lax.dynamic_index_in_dim(vb_s, own, axis=1, keepdims=False)
            s_own = jnp.einsum('hqd,hjd->hqj', q_c, k_own).astype(jnp.float32) * scale
            s_own = jnp.where((kpos[None, :] <= qpos[:, None])[None], s_own, NEG_INF)
            if n_sel > 0:
                gate = jnp.einsum('hqd,hnd->hqn', q_c, km_s).astype(jnp.float32)
                past = jnp.arange(nb) < own
                gate = jnp.where(past[None, None, :], gate, NEG_INF)
                _, idx = lax.top_k(gate, n_sel)
                valid = idx < own
                k_sel = kb_s[head_idx, idx]
                v_sel = vb_s[head_idx, idx]
                s_sel = jnp.einsum('hqd,hqnjd->hqnj', q_c, k_sel).astype(jnp.float32) * scale
                s_sel = jnp.where(valid[..., None], s_sel, NEG_INF)
                s = jnp.concatenate([s_sel.reshape(H, B_QCHUNK, n_sel * B_BLOCK), s_own], axis=-1)
                p = jax.nn.softmax(s, axis=-1).astype(v_own.dtype)
                p_sel = p[..., :n_sel * B_BLOCK].reshape(H, B_QCHUNK, n_sel, B_BLOCK)
                p_own = p[..., n_sel * B_BLOCK:]
                o = (jnp.einsum('hqnj,hqnjd->hqd', p_sel, v_sel)
                     + jnp.einsum('hqj,hjd->hqd', p_own, v_own))
            else:
                p = jax.nn.softmax(s_own, axis=-1).astype(v_own.dtype)
                o = jnp.einsum('hqj,hjd->hqd', p, v_own)
            return o

        return lax.map(one_chunk, (jnp.arange(nqc), q_s))

    o = lax.map(one_seq, (qc, kb, vb, kmean))
    return o.transpose(0, 2, 1, 3, 4).reshape(B, H, S, dh)


def moba_mixer(h, w_q, w_o, kb, vb, kmean):
    B, S, D = h.shape
    q = (h @ w_q).reshape(B, S, B_HEADS, B_HEAD_DIM).transpose(0, 2, 1, 3)
    o = moba_attention(q, kb, vb, kmean)
    return o.transpose(0, 2, 1, 3).reshape(B, S, D) @ w_o


def grouped_experts(xt, eidx, w, w_gate, w_up, w_down):
    T, D = xt.shape
    K = eidx.shape[1]
    E, M = N_EXPERTS, MOE_ROW_BLOCK
    TK = T * K
    e_flat = eidx.reshape(TK)
    tok_flat = jnp.arange(TK, dtype=jnp.int32) // K
    w_flat = w.reshape(TK)
    order = jnp.argsort(e_flat)
    e_sorted = e_flat[order]
    tok_sorted = tok_flat[order]
    w_sorted = w_flat[order]
    counts = jnp.zeros((E,), jnp.int32).at[e_flat].add(1)
    starts = jnp.cumsum(counts) - counts
    padded = ((counts + M - 1) // M) * M
    pad_ends = jnp.cumsum(padded)
    pad_starts = pad_ends - padded
    rank = jnp.arange(TK, dtype=jnp.int32) - starts[e_sorted]
    dest = pad_starts[e_sorted] + rank
    n_blocks = -(-(TK + E * (M - 1)) // M)
    R = n_blocks * M
    x_pad = jnp.zeros((R, D), xt.dtype).at[dest].set(xt[tok_sorted])
    block_start = jnp.arange(n_blocks, dtype=jnp.int32) * M
    block_expert = jnp.minimum(jnp.sum(pad_ends[None, :] <= block_start[:, None], axis=1), E - 1)

    def run_block(args):
        xb, e = args
        return swiglu(xb, w_gate[e], w_up[e], w_down[e])

    y_pad = lax.map(run_block, (x_pad.reshape(n_blocks, M, D), block_expert)).reshape(R, D)
    y = y_pad[dest] * w_sorted[:, None].astype(xt.dtype)
    return jax.ops.segment_sum(y, tok_sorted, num_segments=T)


def moe(h, w_router, e_bias, w_gate, w_up, w_down, ws_gate, ws_up, ws_down):
    B, S, D = h.shape
    T = B * S
    xt = h.reshape(T, D)
    scores = jax.nn.sigmoid((xt @ w_router).astype(jnp.float32))
    choice = scores + e_bias.astype(jnp.float32)
    grp = choice.reshape(T, N_GROUPS, N_EXPERTS // N_GROUPS)
    grp_score = jnp.sum(lax.top_k(grp, 2)[0], axis=-1)
    _, gidx = lax.top_k(grp_score, TOPK_GROUPS)
    gmask = jnp.sum(jax.nn.one_hot(gidx, N_GROUPS, dtype=jnp.float32), axis=1) > 0
    emask = jnp.repeat(gmask, N_EXPERTS // N_GROUPS, axis=1)
    choice = jnp.where(emask, choice, NEG_INF)
    _, eidx = lax.top_k(choice, TOP_K)
    w = jnp.take_along_axis(scores, eidx, axis=1)
    w = w / jnp.sum(w, axis=-1, keepdims=True) * ROUTED_SCALE
    routed = grouped_experts(xt, eidx, w, w_gate, w_up, w_down)
    shared = swiglu(xt, ws_gate, ws_up, ws_down)
    return (routed + shared).reshape(B, S, D)


def setup_inputs(seed: int = 0) -> dict:
    key = jax.random.key(seed)
    ks = jax.random.split(key, 32)
    f32 = jnp.float32

    def nrm(k, shape, s):
        return jax.random.normal(k, shape, f32) * s

    D = D_MODEL
    return {
        'x': nrm(ks[0], (BATCH, SEQ, D), 1.0),
        'c': nrm(ks[1], (BATCH, D), 1.0),
        'ada_w': nrm(ks[2], (DEPTH, D, 6 * D), 0.2 * D ** -0.5),
        'ada_b': nrm(ks[3], (DEPTH, 6 * D), 0.02),
        'norm_mix': 1.0 + nrm(ks[4], (DEPTH, D), 0.02),
        'norm_ffn': 1.0 + nrm(ks[5], (DEPTH, D), 0.02),
        'a_w_in': nrm(ks[6], (N_A_LAYERS, D, 2 * A_WIDTH), D ** -0.5),
        'a_b_in': nrm(ks[7], (N_A_LAYERS, 2 * A_WIDTH), 0.02),
        'a_ln_g': 1.0 + nrm(ks[8], (N_A_LAYERS, A_WIDTH), 0.02),
        'a_ln_b': nrm(ks[9], (N_A_LAYERS, A_WIDTH), 0.02),
        'a_w_s': nrm(ks[10], (N_A_LAYERS, A_GROUPS, A_CHUNK, A_CHUNK), 0.5 * A_CHUNK ** -0.5),
        'a_b_s': 1.0 + nrm(ks[11], (N_A_LAYERS, A_GROUPS, A_CHUNK), 0.02),
        'a_w_out': nrm(ks[12], (N_A_LAYERS, A_WIDTH, D), A_WIDTH ** -0.5),
        'kv_norm': 1.0 + nrm(ks[13], (D,), 0.02),
        'kv_ada_w': nrm(ks[14], (D, 2 * D), 0.2 * D ** -0.5),
        'kv_ada_b': nrm(ks[15], (2 * D,), 0.02),
        'kv_w_k': nrm(ks[16], (D, D), D ** -0.5),
        'kv_w_v': nrm(ks[17], (D, D), D ** -0.5),
        'b_w_q': nrm(ks[18], (N_B_LAYERS, D, D), D ** -0.5),
        'b_w_o': nrm(ks[19], (N_B_LAYERS, D, D), D ** -0.5),
        'moe_router': nrm(ks[20], (DEPTH, D, N_EXPERTS), D ** -0.5),
        'moe_bias': nrm(ks[21], (DEPTH, N_EXPERTS), 0.01),
        'moe_w_gate': nrm(ks[22], (DEPTH, N_EXPERTS, D, EXPERT_DIM), D ** -0.5),
        'moe_w_up': nrm(ks[23], (DEPTH, N_EXPERTS, D, EXPERT_DIM), D ** -0.5),
        'moe_w_down': nrm(ks[24], (DEPTH, N_EXPERTS, EXPERT_DIM, D), EXPERT_DIM ** -0.5),
        'sh_w_gate': nrm(ks[25], (DEPTH, D, SHARED_DIM), D ** -0.5),
        'sh_w_up': nrm(ks[26], (DEPTH, D, SHARED_DIM), D ** -0.5),
        'sh_w_down': nrm(ks[27], (DEPTH, SHARED_DIM, D), SHARED_DIM ** -0.5),
        'final_norm': 1.0 + nrm(ks[28], (D,), 0.02),
    }


def reference(x, c, ada_w, ada_b, norm_mix, norm_ffn, a_w_in, a_b_in, a_ln_g, a_ln_b,
              a_w_s, a_b_s, a_w_out, kv_norm, kv_ada_w, kv_ada_b, kv_w_k, kv_w_v,
              b_w_q, b_w_o, moe_router, moe_bias, moe_w_gate, moe_w_up, moe_w_down,
              sh_w_gate, sh_w_up, sh_w_down, final_norm):
    kv = None
    for i in range(DEPTH):
        sh1, sc1, g1, sh2, sc2, g2 = ada(c, ada_w[i], ada_b[i], 6)
        h = modulate(rmsnorm(x, norm_mix[i]), sh1, sc1)
        if i < N_A_LAYERS:
            y = gmlp_mixer(h, a_w_in[i], a_b_in[i], a_ln_g[i], a_ln_b[i],
                           a_w_s[i], a_b_s[i], a_w_out[i])
        else:
            if kv is None:
                kv = shared_kv(x, c, kv_norm, kv_ada_w, kv_ada_b, kv_w_k, kv_w_v)
            j = i - N_A_LAYERS
            y = moba_mixer(h, b_w_q[j], b_w_o[j], kv[0], kv[1], kv[2])
        x = x + g1 * y
        h = modulate(rmsnorm(x, norm_ffn[i]), sh2, sc2)
        x = x + g2 * moe(h, moe_router[i], moe_bias[i], moe_w_gate[i], moe_w_up[i],
                         moe_w_down[i], sh_w_gate[i], sh_w_up[i], sh_w_down[i])
    return rmsnorm(x, final_norm)
```

```python
import functools

import jax
import jax.numpy as jnp
from jax import lax
from jax.experimental import pallas as pl
from jax.experimental.pallas import tpu as pltpu

F32 = jnp.float32
BF16 = jnp.bfloat16
I32 = jnp.int32

RMS_EPS = 1e-6
LN_EPS = 1e-5
NEG_INF = -1e30

A_CHUNK = 128
A_GROUPS = 8
B_HEADS = 16
B_BLOCK = 256
B_TOPK = 3
N_EXPERTS = 64
TOP_K = 8
N_GROUPS = 8
TOPK_GROUPS = 4
ROUTED_SCALE = 2.5

LANES = 128
VMEM_LIMIT = 56 * 1024 * 1024

TM_GMLP = 256
TM_ROUTER = 256
TM_DEST = 2048
TM_DISPATCH = 256
TM_COMBINE = 128
BM_EXPERT = 256


def _cparams(sem):
    return pltpu.CompilerParams(dimension_semantics=sem, vmem_limit_bytes=VMEM_LIMIT)


def _sigmoid(x):
    return 1.0 / (1.0 + jnp.exp(-x))


def _silu(x):
    return x * _sigmoid(x)


def _gelu_tanh(x):
    return 0.5 * x * (1.0 + jnp.tanh(0.7978845608028654 * (x + 0.044715 * (x * x * x))))


def _rms(x, g):
    return x * lax.rsqrt(jnp.mean(x * x, axis=-1, keepdims=True) + RMS_EPS) * g


def _bdot(a, b):
    return jnp.dot(a, b, preferred_element_type=F32)


def _bdot_nt(a, b):
    return lax.dot_general(a, b, (((1,), (1,)), ((), ())), preferred_element_type=F32)


def _split(a):
    hi = a.astype(BF16)
    lo = (a - hi.astype(F32)).astype(BF16)
    return hi, lo


def _dot3_nt(a, b):
    ah, al = _split(a)
    bh, bl = _split(b)
    return _bdot_nt(ah, bh) + (_bdot_nt(ah, bl) + _bdot_nt(al, bh))


def _ada_kernel(c_ref, w_ref, b_ref, o_ref):
    a = _silu(c_ref[...]).astype(BF16)
    o_ref[...] = _bdot(a, w_ref[...].astype(BF16)) + b_ref[...]


def _ada(c, w, b):
    bsz, d = c.shape
    n = w.shape[1]
    tn = 1024
    return pl.pallas_call(
        _ada_kernel,
        grid=(n // tn,),
        in_specs=[pl.BlockSpec((bsz, d), lambda j: (0, 0)),
                  pl.BlockSpec((d, tn), lambda j: (0, j)),
                  pl.BlockSpec((1, tn), lambda j: (0, j))],
        out_specs=pl.BlockSpec((bsz, tn), lambda j: (0, j)),
        out_shape=jax.ShapeDtypeStruct((bsz, n), F32),
        compiler_params=_cparams(("arbitrary",)),
    )(c, w, b.reshape(1, n))


def _gmlp_kernel(x_ref, sh_ref, sc_ref, g_ref, nw_ref, win_ref, bin_ref, lng_ref, lnb_ref,
                 ws_ref, bst_ref, wout_ref, o_ref, y_sc):
    x = x_ref[0]
    tm = x.shape[0]
    h = _rms(x, nw_ref[...]) * (1.0 + sc_ref[0]) + sh_ref[0]
    z = _gelu_tanh(_bdot(h.astype(BF16), win_ref[...]) + bin_ref[...])
    aw = z.shape[1] // 2
    gd = aw // A_GROUPS
    u = z[:, :aw]
    v = z[:, aw:]
    mu = jnp.mean(v, axis=-1, keepdims=True)
    dv = v - mu
    var = jnp.mean(dv * dv, axis=-1, keepdims=True)
    vn = (dv * lax.rsqrt(var + LN_EPS) * lng_ref[...] + lnb_ref[...]).astype(BF16)
    row = lax.broadcasted_iota(I32, (A_CHUNK, A_CHUNK), 0)
    col = lax.broadcasted_iota(I32, (A_CHUNK, A_CHUNK), 1)
    causal = col <= row
    for g in range(A_GROUPS):
        wg = jnp.where(causal, ws_ref[g], 0.0).astype(BF16)
        bcol = bst_ref[:, g:g + 1]
        for ci in range(tm // A_CHUNK):
            rs = slice(ci * A_CHUNK, (ci + 1) * A_CHUNK)
            cs = slice(g * gd, (g + 1) * gd)
            sv = _bdot(wg, vn[rs, cs]) + bcol
            y_sc[rs, cs] = (u[rs, cs] * sv).astype(BF16)
    o_ref[0] = x + g_ref[0] * _bdot(y_sc[...], wout_ref[...])


def _gmlp(x, sh, sc, gt, nw, w_in, b_in, ln_g, ln_b, w_s, b_s, w_out):
    bsz, s, d = x.shape
    tm = TM_GMLP
    n_in = w_in.shape[1]
    aw = n_in // 2
    vec = pl.BlockSpec((1, 1, d), lambda b, j: (b, 0, 0))
    full2 = lambda shape: pl.BlockSpec(shape, lambda b, j: (0, 0))
    return pl.pallas_call(
        _gmlp_kernel,
        grid=(bsz, s // tm),
        in_specs=[pl.BlockSpec((1, tm, d), lambda b, j: (b, j, 0)), vec, vec, vec,
                  full2((1, d)), full2((d, n_in)), full2((1, n_in)), full2((1, aw)), full2((1, aw)),
                  pl.BlockSpec((A_GROUPS, A_CHUNK, A_CHUNK), lambda b, j: (0, 0, 0)),
                  full2((A_CHUNK, A_GROUPS)), full2((aw, d))],
        out_specs=pl.BlockSpec((1, tm, d), lambda b, j: (b, j, 0)),
        out_shape=jax.ShapeDtypeStruct((bsz, s, d), F32),
        scratch_shapes=[pltpu.VMEM((tm, aw), BF16)],
        compiler_params=_cparams(("arbitrary", "arbitrary")),
    )(x, sh, sc, gt, nw.reshape(1, d), w_in.astype(BF16), b_in.reshape(1, n_in),
      ln_g.reshape(1, aw), ln_b.reshape(1, aw), w_s, b_s.T, w_out.astype(BF16))


def _router_kernel(x_ref, sh_ref, sc_ref, g_ref, nw_ref, wrt_ref, bias_ref, wsg_ref, wsu_ref, wsd_ref,
                   h_ref, base_ref, eidx_ref, rank_ref, w_ref, cnt_ref, carry_sc):
    first = jnp.logical_and(pl.program_id(0) == 0, pl.program_id(1) == 0)

    @pl.when(first)
    def _():
        carry_sc[...] = jnp.zeros_like(carry_sc)

    x = x_ref[0]
    tm = x.shape[0]
    h = _rms(x, nw_ref[...]) * (1.0 + sc_ref[0]) + sh_ref[0]
    h_ref[...] = h
    hb = h.astype(BF16)
    act = (_silu(_bdot(hb, wsg_ref[...])) * _bdot(hb, wsu_ref[...])).astype(BF16)
    base_ref[...] = x + g_ref[0] * _bdot(act, wsd_ref[...])

    scores = _sigmoid(_dot3_nt(wrt_ref[...], h))
    choice = scores + bias_ref[...]
    gsz = N_EXPERTS // N_GROUPS
    sub = lax.broadcasted_iota(I32, (gsz, tm), 0)
    blocks = [choice[g * gsz:(g + 1) * gsz] for g in range(N_GROUPS)]
    gscore = []
    for blk in blocks:
        m1 = jnp.max(blk, axis=0, keepdims=True)
        i1 = jnp.min(jnp.where(blk == m1, sub, gsz), axis=0, keepdims=True)
        m2 = jnp.max(jnp.where(sub == i1, -jnp.inf, blk), axis=0, keepdims=True)
        gscore.append(m1 + m2)
    masked = []
    for g in range(N_GROUPS):
        beats = jnp.zeros((1, tm), F32)
        for m in range(N_GROUPS):
            if m == g:
                continue
            b = gscore[m] > gscore[g]
            if m < g:
                b = jnp.logical_or(b, gscore[m] == gscore[g])
            beats = beats + b.astype(F32)
        masked.append(jnp.where(beats < TOPK_GROUPS, blocks[g], NEG_INF))
    cur = jnp.concatenate(masked, axis=0)
    eio = lax.broadcasted_iota(I32, (N_EXPERTS, tm), 0)
    sels, eids, ws = [], [], []
    for _ in range(TOP_K):
        m = jnp.max(cur, axis=0, keepdims=True)
        idx = jnp.min(jnp.where(cur == m, eio, N_EXPERTS), axis=0, keepdims=True)
        sel = eio == idx
        sels.append(sel)
        eids.append(idx)
        ws.append(jnp.sum(jnp.where(sel, scores, 0.0), axis=0, keepdims=True))
        cur = jnp.where(sel, -jnp.inf, cur)
    wsum = ws[0]
    for k in range(1, TOP_K):
        wsum = wsum + ws[k]
    w_ref[...] = jnp.concatenate(ws, axis=0) / wsum * ROUTED_SCALE
    eidx_ref[...] = jnp.concatenate(eids, axis=0)

    onehot = jnp.zeros((N_EXPERTS, tm), F32)
    for sel in sels:
        onehot = onehot + sel.astype(F32)
    r_i = lax.broadcasted_iota(I32, (tm, tm), 0)
    c_i = lax.broadcasted_iota(I32, (tm, tm), 1)
    before = (r_i < c_i).astype(BF16)
    prior = _bdot(onehot.astype(BF16), before) + carry_sc[...]
    ranks = [jnp.sum(jnp.where(sel, prior, 0.0), axis=0, keepdims=True) for sel in sels]
    rank_ref[...] = jnp.concatenate(ranks, axis=0).astype(I32)
    total = carry_sc[...] + jnp.sum(onehot, axis=1, keepdims=True)
    carry_sc[...] = total
    cnt_ref[...] = total


def _router(x, sh, sc, gt, nw, w_router, e_bias, wsg, wsu, wsd):
    bsz, s, d = x.shape
    t = bsz * s
    tm = TM_ROUTER
    nt = s // tm
    sd = wsg.shape[1]
    vec = pl.BlockSpec((1, 1, d), lambda b, j: (b, 0, 0))
    full2 = lambda shape: pl.BlockSpec(shape, lambda b, j: (0, 0))
    tok = pl.BlockSpec((tm, d), lambda b, j: (b * nt + j, 0))
    slot = pl.BlockSpec((TOP_K, tm), lambda b, j: (0, b * nt + j))
    return pl.pallas_call(
        _router_kernel,
        grid=(bsz, nt),
        in_specs=[pl.BlockSpec((1, tm, d), lambda b, j: (b, j, 0)), vec, vec, vec,
                  full2((1, d)), full2((N_EXPERTS, d)), full2((N_EXPERTS, 1)),
                  full2((d, sd)), full2((d, sd)), full2((sd, d))],
        out_specs=[tok, tok, slot, slot, slot, full2((N_EXPERTS, 1))],
        out_shape=[jax.ShapeDtypeStruct((t, d), F32), jax.ShapeDtypeStruct((t, d), F32),
                   jax.ShapeDtypeStruct((TOP_K, t), I32), jax.ShapeDtypeStruct((TOP_K, t), I32),
                   jax.ShapeDtypeStruct((TOP_K, t), F32), jax.ShapeDtypeStruct((N_EXPERTS, 1), F32)],
        scratch_shapes=[pltpu.VMEM((N_EXPERTS, 1), F32)],
        compiler_params=_cparams(("arbitrary", "arbitrary")),
    )(x, sh, sc, gt, nw.reshape(1, d), w_router.T, e_bias.reshape(N_EXPERTS, 1),
      wsg.astype(BF16), wsu.astype(BF16), wsd.astype(BF16))


def _dest_kernel(starts_ref, e_ref, r_ref, o_ref):
    e = e_ref[...]
    acc = r_ref[...]
    for j in range(N_EXPERTS):
        acc = acc + jnp.where(e == j, starts_ref[j], 0)
    o_ref[...] = acc


def _dest(starts, eidx, rank):
    t = eidx.shape[1]
    tm = min(TM_DEST, t)
    blk = pl.BlockSpec((TOP_K, tm), lambda i, s: (0, i))
    return pl.pallas_call(
        _dest_kernel,
        grid_spec=pltpu.PrefetchScalarGridSpec(
            num_scalar_prefetch=1, grid=(t // tm,), in_specs=[blk, blk], out_specs=blk),
        out_shape=jax.ShapeDtypeStruct((TOP_K, t), I32),
        compiler_params=_cparams(("arbitrary",)),
    )(starts, eidx, rank)


def _dispatch_kernel(dest_ref, h_ref, xs_ref, sem):
    tm = dest_ref.shape[1]
    base = pl.program_id(0) * tm

    def issue(t, carry):
        for k in range(TOP_K):
            pltpu.make_async_copy(h_ref.at[pl.ds(base + t, 1)],
                                  xs_ref.at[pl.ds(dest_ref[k, t], 1)], sem).start()
        return carry

    lax.fori_loop(0, tm, issue, 0)

    def drain(t, carry):
        for k in range(TOP_K):
            pltpu.make_async_copy(h_ref.at[pl.ds(0, 1)], xs_ref.at[pl.ds(0, 1)], sem).wait()
        return carry

    lax.fori_loop(0, tm, drain, 0)


def _dispatch(dest, h):
    t, d = h.shape
    tm = TM_DISPATCH
    return pl.pallas_call(
        _dispatch_kernel,
        grid=(t // tm,),
        in_specs=[pl.BlockSpec((TOP_K, tm), lambda i: (0, i), memory_space=pltpu.SMEM),
                  pl.BlockSpec(memory_space=pl.ANY)],
        out_specs=pl.BlockSpec(memory_space=pl.ANY),
        out_shape=jax.ShapeDtypeStruct((t * TOP_K, d), F32),
        scratch_shapes=[pltpu.SemaphoreType.DMA(())],
        compiler_params=_cparams(("arbitrary",)),
    )(dest, h)


def _expert_kernel(vb_ref, ve_ref, vlo_ref, vhi_ref, vfirst_ref, vnew_ref,
                   x_ref, wg_ref, wu_ref, wd_ref, o_ref, wg_sc, wu_sc, wd_sc):
    v = pl.program_id(0)

    @pl.when(vnew_ref[v] == 1)
    def _():
        wg_sc[...] = wg_ref[0].astype(BF16)
        wu_sc[...] = wu_ref[0].astype(BF16)
        wd_sc[...] = wd_ref[0].astype(BF16)

    xb = x_ref[...].astype(BF16)
    act = (_silu(_bdot(xb, wg_sc[...])) * _bdot(xb, wu_sc[...])).astype(BF16)
    y = _bdot(act, wd_sc[...])
    rows = lax.broadcasted_iota(I32, (y.shape[0], 1), 0)
    y = jnp.where(jnp.logical_and(rows >= vlo_ref[v], rows < vhi_ref[v]), y, 0.0)

    @pl.when(vfirst_ref[v] == 1)
    def _():
        o_ref[...] = y

    @pl.when(vfirst_ref[v] == 0)
    def _():
        o_ref[...] += y


def _experts(meta, xs, w_gate, w_up, w_down):
    r, d = xs.shape
    _, _, ed = w_gate.shape
    bm = BM_EXPERT
    nv = meta[0].shape[0]
    xmap = lambda v, vb, ve, vlo, vhi, vf, vn: (vb[v], 0)
    wmap = lambda v, vb, ve, vlo, vhi, vf, vn: (ve[v], 0, 0)
    return pl.pallas_call(
        _expert_kernel,
        grid_spec=pltpu.PrefetchScalarGridSpec(
            num_scalar_prefetch=6, grid=(nv,),
            in_specs=[pl.BlockSpec((bm, d), xmap), pl.BlockSpec((1, d, ed), wmap),
                      pl.BlockSpec((1, d, ed), wmap), pl.BlockSpec((1, ed, d), wmap)],
            out_specs=pl.BlockSpec((bm, d), xmap),
            scratch_shapes=[pltpu.VMEM((d, ed), BF16), pltpu.VMEM((d, ed), BF16),
                            pltpu.VMEM((ed, d), BF16)]),
        out_shape=jax.ShapeDtypeStruct((r, d), F32),
        compiler_params=_cparams(("arbitrary",)),
    )(*meta, xs, w_gate, w_up, w_down)


def _visit_meta(counts, n_rows):
    bm = BM_EXPERT
    nb = n_rows // bm
    nv = nb + N_EXPERTS - 1
    ends = jnp.cumsum(counts)
    starts = ends - counts
    blo = jnp.arange(nb, dtype=I32) * bm
    first_e = jnp.searchsorted(ends, blo, side="right").astype(I32)
    last_e = (jnp.searchsorted(starts, blo + bm, side="left") - 1).astype(I32)
    per = last_e - first_e + 1
    off_incl = jnp.cumsum(per)
    off = off_incl - per
    total = off_incl[-1]
    vi = jnp.arange(nv, dtype=I32)
    real = vi < total
    vb = jnp.minimum(jnp.searchsorted(off_incl, vi, side="right"), nb - 1).astype(I32)
    ve_real = first_e[vb] + (vi - off[vb])
    ve_last = last_e[nb - 1]
    ve = jnp.where(real, ve_real, ve_last).astype(I32)
    lo = jnp.clip(starts[ve] - vb * bm, 0, bm)
    hi = jnp.clip(ends[ve] - vb * bm, 0, bm)
    vlo = jnp.where(real, lo, 0).astype(I32)
    vhi = jnp.where(real, hi, 0).astype(I32)
    vfirst = jnp.logical_and(real, vi == off[vb]).astype(I32)
    prev = jnp.concatenate([jnp.full((1,), -1, I32), ve[:-1]])
    vnew = (ve != prev).astype(I32)
    return starts.astype(I32), (vb, ve, vlo, vhi, vfirst, vnew)


def _combine_kernel(dest_ref, ys_ref, w_ref, base_ref, g_ref, fn_ref, o_ref, buf, sem, *, final):
    tm = base_ref.shape[0]

    def issue(t, carry):
        for k in range(TOP_K):
            pltpu.make_async_copy(ys_ref.at[pl.ds(dest_ref[k, t], 1)],
                                  buf.at[k, pl.ds(t, 1)], sem).start()
        return carry

    lax.fori_loop(0, tm, issue, 0)

    def drain(t, carry):
        for k in range(TOP_K):
            pltpu.make_async_copy(ys_ref.at[pl.ds(0, 1)], buf.at[0, pl.ds(0, 1)], sem).wait()
        return carry

    lax.fori_loop(0, tm, drain, 0)

    w = w_ref[...]
    acc = w[:, 0:1] * buf[0]
    for k in range(1, TOP_K):
        acc = acc + w[:, k:k + 1] * buf[k]
    out = base_ref[...] + g_ref[0] * acc
    if final:
        out = _rms(out, fn_ref[...])
    o_ref[...] = out


def _combine(dest, ys, w_tok, base, gt, fnorm, seq, final):
    t, d = base.shape
    tm = TM_COMBINE
    nt = seq // tm
    return pl.pallas_call(
        functools.partial(_combine_kernel, final=final),
        grid=(t // tm,),
        in_specs=[pl.BlockSpec((TOP_K, tm), lambda i: (0, i), memory_space=pltpu.SMEM),
                  pl.BlockSpec(memory_space=pl.ANY),
                  pl.BlockSpec((tm, TOP_K), lambda i: (i, 0)),
                  pl.BlockSpec((tm, d), lambda i: (i, 0)),
                  pl.BlockSpec((1, 1, d), lambda i: (i // nt, 0, 0)),
                  pl.BlockSpec((1, d), lambda i: (0, 0))],
        out_specs=pl.BlockSpec((tm, d), lambda i: (i, 0)),
        out_shape=jax.ShapeDtypeStruct((t, d), F32),
        scratch_shapes=[pltpu.VMEM((TOP_K, tm, d), F32), pltpu.SemaphoreType.DMA(())],
        compiler_params=_cparams(("arbitrary",)),
    )(dest, ys, w_tok, base, gt, fnorm.reshape(1, d))


def _moe(x, sh, sc, gt, nw, w_router, e_bias, w_gate, w_up, w_down, wsg, wsu, wsd, fnorm, final):
    bsz, s, d = x.shape
    t = bsz * s
    h, base, eidx, rank, w, cnt = _router(x, sh, sc, gt, nw, w_router, e_bias, wsg, wsu, wsd)
    counts = cnt.reshape(N_EXPERTS).astype(I32)
    starts, meta = _visit_meta(counts, t * TOP_K)
    dest = _dest(starts, eidx, rank)
    xs = _dispatch(dest, h)
    ys = _experts(meta, xs, w_gate, w_up, w_down)
    out = _combine(dest, ys, w.T, base, gt, fnorm, s, final)
    return out.reshape(bsz, s, d)


def _kv_kernel(x_ref, sh_ref, sc_ref, nw_ref, wk_ref, wv_ref, k_ref, v_ref, km_ref):
    x = x_ref[0]
    hb = (_rms(x, nw_ref[...]) * (1.0 + sc_ref[0]) + sh_ref[0]).astype(BF16)
    k = _bdot(hb, wk_ref[...])
    v = _bdot(hb, wv_ref[...])
    for p in range(k.shape[1] // LANES):
        k_ref[0, p] = k[:, p * LANES:(p + 1) * LANES].astype(BF16)
        v_ref[0, p] = v[:, p * LANES:(p + 1) * LANES].astype(BF16)
    km_ref[0, 0] = jnp.mean(k, axis=0, keepdims=True)


def _kv(x, sh, sc, nw, w_k, w_v):
    bsz, s, d = x.shape
    nb = s // B_BLOCK
    npair = d // LANES
    vec = pl.BlockSpec((1, 1, d), lambda b, j: (b, 0, 0))
    full2 = lambda shape: pl.BlockSpec(shape, lambda b, j: (0, 0))
    kvspec = pl.BlockSpec((1, npair, B_BLOCK, LANES), lambda b, j: (b, 0, j, 0))
    return pl.pallas_call(
        _kv_kernel,
        grid=(bsz, nb),
        in_specs=[pl.BlockSpec((1, B_BLOCK, d), lambda b, j: (b, j, 0)), vec, vec,
                  full2((1, d)), full2((d, d)), full2((d, d))],
        out_specs=[kvspec, kvspec, pl.BlockSpec((1, 1, 1, d), lambda b, j: (b, j, 0, 0))],
        out_shape=[jax.ShapeDtypeStruct((bsz, npair, s, LANES), BF16),
                   jax.ShapeDtypeStruct((bsz, npair, s, LANES), BF16),
                   jax.ShapeDtypeStruct((bsz, nb, 1, d), F32)],
        compiler_params=_cparams(("arbitrary", "arbitrary")),
    )(x, sh, sc, nw.reshape(1, d), w_k.astype(BF16), w_v.astype(BF16))


def _attn_kernel(x_ref, sh_ref, sc_ref, g_ref, nw_ref, wq_ref, wo_ref, k_ref, v_ref, km_ref,
                 o_ref, q_sc, oh_sc, *, nb, n_sel):
    qb = pl.program_id(1)
    x = x_ref[0]
    bq = x.shape[0]
    npair = q_sc.shape[0]
    hd = LANES // 2
    scale = float(hd) ** -0.5
    h = _rms(x, nw_ref[...]) * (1.0 + sc_ref[0]) + sh_ref[0]
    q = _bdot(h.astype(BF16), wq_ref[...])
    for p in range(npair):
        q_sc[p] = q[:, p * LANES:(p + 1) * LANES]

    lane = lax.broadcasted_iota(I32, (1, LANES), 1)
    r_i = lax.broadcasted_iota(I32, (bq, B_BLOCK), 0)
    c_i = lax.broadcasted_iota(I32, (bq, B_BLOCK), 1)
    causal = c_i <= r_i
    past_lane = lane < qb
    own0 = pl.multiple_of(qb * B_BLOCK, B_BLOCK)

    def pair_body(p, carry):
        q2 = q_sc[p]
        km2 = km_ref[0, p]
        outs = []
        for e in range(2):
            hm = (lane >= hd) if e == 1 else (lane < hd)
            qe = jnp.where(hm, q2, 0.0)
            gate = _dot3_nt(qe, km2)
            selmat = jnp.zeros((bq, LANES), F32)
            for n in range(nb):
                gn = gate[:, n:n + 1]
                beats = jnp.logical_or(gate > gn, jnp.logical_and(gate == gn, lane < n))
                beats = jnp.logical_and(beats, past_lane)
                cnt = jnp.sum(beats.astype(F32), axis=1, keepdims=True)
                selmat = jnp.where(lane == n, (cnt < n_sel).astype(F32), selmat)
            qs = (qe * scale).astype(BF16)

            s_own = jnp.where(causal, _bdot_nt(qs, k_ref[0, p, pl.ds(own0, B_BLOCK), :]), NEG_INF)
            m0 = jnp.max(s_own, axis=1, keepdims=True)
            p0 = jnp.exp(s_own - m0)
            l0 = jnp.sum(p0, axis=1, keepdims=True)
            a0 = _bdot(p0.astype(BF16), v_ref[0, p, pl.ds(own0, B_BLOCK), :])

            def kb_body(kb, st):
                m, l, acc = st
                k0 = pl.multiple_of(kb * B_BLOCK, B_BLOCK)
                s = _bdot_nt(qs, k_ref[0, p, pl.ds(k0, B_BLOCK), :])
                colsel = jnp.sum(jnp.where(lane == kb, selmat, 0.0), axis=1, keepdims=True)
                s = jnp.where(colsel > 0.5, s, NEG_INF)
                m_new = jnp.maximum(m, jnp.max(s, axis=1, keepdims=True))
                alpha = jnp.exp(m - m_new)
                pe = jnp.exp(s - m_new)
                l = alpha * l + jnp.sum(pe, axis=1, keepdims=True)
                acc = alpha * acc + _bdot(pe.astype(BF16), v_ref[0, p, pl.ds(k0, B_BLOCK), :])
                return m_new, l, acc

            _, l, acc = lax.fori_loop(0, qb, kb_body, (m0, l0, a0))
            outs.append(acc / l)
        oh_sc[p] = jnp.where(lane < hd, outs[0], outs[1]).astype(BF16)
        return carry

    lax.fori_loop(0, npair, pair_body, 0)
    o = jnp.concatenate([oh_sc[p] for p in range(npair)], axis=1)
    o_ref[0] = x + g_ref[0] * _bdot(o, wo_ref[...])


def _attn(x, sh, sc, gt, nw, w_q, w_o, k2, v2, km2):
    bsz, s, d = x.shape
    nb = s // B_BLOCK
    npair = d // LANES
    n_sel = min(B_TOPK, nb - 1)
    vec = pl.BlockSpec((1, 1, d), lambda b, j: (b, 0, 0))
    full2 = lambda shape: pl.BlockSpec(shape, lambda b, j: (0, 0))
    kvspec = pl.BlockSpec((1, npair, s, LANES), lambda b, j: (b, 0, 0, 0))
    return pl.pallas_call(
        functools.partial(_attn_kernel, nb=nb, n_sel=n_sel),
        grid=(bsz, nb),
        in_specs=[pl.BlockSpec((1, B_BLOCK, d), lambda b, j: (b, j, 0)), vec, vec, vec,
                  full2((1, d)), full2((d, d)), full2((d, d)), kvspec, kvspec,
                  pl.BlockSpec((1, npair, LANES, LANES), lambda b, j: (b, 0, 0, 0))],
        out_specs=pl.BlockSpec((1, B_BLOCK, d), lambda b, j: (b, j, 0)),
        out_shape=jax.ShapeDtypeStruct((bsz, s, d), F32),
        scratch_shapes=[pltpu.VMEM((npair, B_BLOCK, LANES), F32),
                        pltpu.VMEM((npair, B_BLOCK, LANES), BF16)],
        compiler_params=_cparams(("arbitrary", "arbitrary")),
    )(x, sh, sc, gt, nw.reshape(1, d), w_q.astype(BF16), w_o.astype(BF16), k2, v2, km2)


def kernel(x, c, ada_w, ada_b, norm_mix, norm_ffn, a_w_in, a_b_in, a_ln_g, a_ln_b, a_w_s, a_b_s,
           a_w_out, kv_norm, kv_ada_w, kv_ada_b, kv_w_k, kv_w_v, b_w_q, b_w_o, moe_router, moe_bias,
           moe_w_gate, moe_w_up, moe_w_down, sh_w_gate, sh_w_up, sh_w_down, final_norm):
    bsz, s, d = x.shape
    depth = ada_w.shape[0]
    n_a = a_w_in.shape[0]
    assert s % B_BLOCK == 0 and s % TM_GMLP == 0 and d % LANES == 0 and s // B_BLOCK <= LANES
    nb = s // B_BLOCK
    npair = d // LANES

    def mods(w, b, n):
        m = _ada(c, w, b)
        return [m[:, i * d:(i + 1) * d].reshape(bsz, 1, d) for i in range(n)]

    k2 = v2 = km2 = None
    for i in range(depth):
        sh1, sc1, g1, sh2, sc2, g2 = mods(ada_w[i], ada_b[i], 6)
        if i < n_a:
            x = _gmlp(x, sh1, sc1, g1, norm_mix[i], a_w_in[i], a_b_in[i], a_ln_g[i], a_ln_b[i],
                      a_w_s[i], a_b_s[i], a_w_out[i])
        else:
            if k2 is None:
                ksh, ksc = mods(kv_ada_w, kv_ada_b, 2)
                k2, v2, km = _kv(x, ksh, ksc, kv_norm, kv_w_k, kv_w_v)
                km = km.reshape(bsz, nb, npair, LANES).transpose(0, 2, 1, 3)
                km2 = jnp.pad(km, ((0, 0), (0, 0), (0, LANES - nb), (0, 0)))
            j = i - n_a
            x = _attn(x, sh1, sc1, g1, norm_mix[i], b_w_q[j], b_w_o[j], k2, v2, km2)
        x = _moe(x, sh2, sc2, g2, norm_ffn[i], moe_router[i], moe_bias[i], moe_w_gate[i],
                 moe_w_up[i], moe_w_down[i], sh_w_gate[i], sh_w_up[i], sh_w_down[i],
                 final_norm, i == depth - 1)
    return x
```

```python
import functools

import jax
import jax.numpy as jnp
from jax import lax
from jax.experimental import pallas as pl
from jax.experimental.pallas import tpu as pltpu

F32 = jnp.float32
BF16 = jnp.bfloat16
I32 = jnp.int32

RMS_EPS = 1e-6
LN_EPS = 1e-5
NEG_INF = -1e30

A_CHUNK = 128
A_GROUPS = 8
B_HEADS = 16
B_BLOCK = 256
B_TOPK = 3
N_EXPERTS = 64
TOP_K = 8
N_GROUPS = 8
TOPK_GROUPS = 4
ROUTED_SCALE = 2.5

LANES = 128
VMEM_LIMIT = 56 * 1024 * 1024

TM_GMLP = 256
TM_ROUTER = 256
TM_DEST = 2048
TM_DISPATCH = 256
TM_COMBINE = 128
BM_EXPERT = 256


def _cparams(sem):
    return pltpu.CompilerParams(dimension_semantics=sem, vmem_limit_bytes=VMEM_LIMIT)


def _sigmoid(x):
    return 1.0 / (1.0 + jnp.exp(-x))


def _silu(x):
    return x * _sigmoid(x)


def _gelu_tanh(x):
    return 0.5 * x * (1.0 + jnp.tanh(0.7978845608028654 * (x + 0.044715 * (x * x * x))))


def _rms(x, g):
    return x * lax.rsqrt(jnp.mean(x * x, axis=-1, keepdims=True) + RMS_EPS) * g


def _bdot(a, b):
    return jnp.dot(a, b, preferred_element_type=F32)


def _bdot_nt(a, b):
    return lax.dot_general(a, b, (((1,), (1,)), ((), ())), preferred_element_type=F32)


def _split(a):
    hi = a.astype(BF16)
    lo = (a - hi.astype(F32)).astype(BF16)
    return hi, lo


def _dot3_nt(a, b):
    ah, al = _split(a)
    bh, bl = _split(b)
    return _bdot_nt(ah, bh) + (_bdot_nt(ah, bl) + _bdot_nt(al, bh))


def _ada_kernel(c_ref, w_ref, b_ref, o_ref):
    a = _silu(c_ref[...]).astype(BF16)
    o_ref[...] = _bdot(a, w_ref[...].astype(BF16)) + b_ref[...]


def _ada(c, w, b):
    bsz, d = c.shape
    n = w.shape[1]
    tn = 1024
    return pl.pallas_call(
        _ada_kernel,
        grid=(n // tn,),
        in_specs=[pl.BlockSpec((bsz, d), lambda j: (0, 0)),
                  pl.BlockSpec((d, tn), lambda j: (0, j)),
                  pl.BlockSpec((1, tn), lambda j: (0, j))],
        out_specs=pl.BlockSpec((bsz, tn), lambda j: (0, j)),
        out_shape=jax.ShapeDtypeStruct((bsz, n), F32),
        compiler_params=_cparams(("arbitrary",)),
    )(c, w, b.reshape(1, n))


def _gmlp_kernel(x_ref, sh_ref, sc_ref, g_ref, nw_ref, win_ref, bin_ref, lng_ref, lnb_ref,
                 ws_ref, bst_ref, wout_ref, o_ref, y_sc):
    x = x_ref[0]
    tm = x.shape[0]
    h = _rms(x, nw_ref[...]) * (1.0 + sc_ref[0]) + sh_ref[0]
    z = _gelu_tanh(_bdot(h.astype(BF16), win_ref[...]) + bin_ref[...])
    aw = z.shape[1] // 2
    gd = aw // A_GROUPS
    u = z[:, :aw]
    v = z[:, aw:]
    mu = jnp.mean(v, axis=-1, keepdims=True)
    dv = v - mu
    var = jnp.mean(dv * dv, axis=-1, keepdims=True)
    vn = (dv * lax.rsqrt(var + LN_EPS) * lng_ref[...] + lnb_ref[...]).astype(BF16)
    row = lax.broadcasted_iota(I32, (A_CHUNK, A_CHUNK), 0)
    col = lax.broadcasted_iota(I32, (A_CHUNK, A_CHUNK), 1)
    causal = col <= row
    for g in range(A_GROUPS):
        wg = jnp.where(causal, ws_ref[g], 0.0).astype(BF16)
        bcol = bst_ref[:, g:g + 1]
        for ci in range(tm // A_CHUNK):
            rs = slice(ci * A_CHUNK, (ci + 1) * A_CHUNK)
            cs = slice(g * gd, (g + 1) * gd)
            sv = _bdot(wg, vn[rs, cs]) + bcol
            y_sc[rs, cs] = (u[rs, cs] * sv).astype(BF16)
    o_ref[0] = x + g_ref[0] * _bdot(y_sc[...], wout_ref[...])


def _gmlp(x, sh, sc, gt, nw, w_in, b_in, ln_g, ln_b, w_s, b_s, w_out):
    bsz, s, d = x.shape
    tm = TM_GMLP
    n_in = w_in.shape[1]
    aw = n_in // 2
    vec = pl.BlockSpec((1, 1, d), lambda b, j: (b, 0, 0))
    full2 = lambda shape: pl.BlockSpec(shape, lambda b, j: (0, 0))
    return pl.pallas_call(
        _gmlp_kernel,
        grid=(bsz, s // tm),
        in_specs=[pl.BlockSpec((1, tm, d), lambda b, j: (b, j, 0)), vec, vec, vec,
                  full2((1, d)), full2((d, n_in)), full2((1, n_in)), full2((1, aw)), full2((1, aw)),
                  pl.BlockSpec((A_GROUPS, A_CHUNK, A_CHUNK), lambda b, j: (0, 0, 0)),
                  full2((A_CHUNK, A_GROUPS)), full2((aw, d))],
        out_specs=pl.BlockSpec((1, tm, d), lambda b, j: (b, j, 0)),
        out_shape=jax.ShapeDtypeStruct((bsz, s, d), F32),
        scratch_shapes=[pltpu.VMEM((tm, aw), BF16)],
        compiler_params=_cparams(("arbitrary", "arbitrary")),
    )(x, sh, sc, gt, nw.reshape(1, d), w_in.astype(BF16), b_in.reshape(1, n_in),
      ln_g.reshape(1, aw), ln_b.reshape(1, aw), w_s, b_s.T, w_out.astype(BF16))


def _router_kernel(x_ref, sh_ref, sc_ref, g_ref, nw_ref, wrt_ref, bias_ref, wsg_ref, wsu_ref, wsd_ref,
                   h_ref, base_ref, eidx_ref, rank_ref, w_ref, cnt_ref, carry_sc):
    first = jnp.logical_and(pl.program_id(0) == 0, pl.program_id(1) == 0)

    @pl.when(first)
    def _():
        carry_sc[...] = jnp.zeros_like(carry_sc)

    x = x_ref[0]
    tm = x.shape[0]
    h = _rms(x, nw_ref[...]) * (1.0 + sc_ref[0]) + sh_ref[0]
    h_ref[...] = h
    hb = h.astype(BF16)
    act = (_silu(_bdot(hb, wsg_ref[...])) * _bdot(hb, wsu_ref[...])).astype(BF16)
    base_ref[...] = x + g_ref[0] * _bdot(act, wsd_ref[...])

    scores = _sigmoid(_dot3_nt(wrt_ref[...], h))
    choice = scores + bias_ref[...]
    gsz = N_EXPERTS // N_GROUPS
    sub = lax.broadcasted_iota(I32, (gsz, tm), 0)
    blocks = [choice[g * gsz:(g + 1) * gsz] for g in range(N_GROUPS)]
    gscore = []
    for blk in blocks:
        m1 = jnp.max(blk, axis=0, keepdims=True)
        i1 = jnp.min(jnp.where(blk == m1, sub, gsz), axis=0, keepdims=True)
        m2 = jnp.max(jnp.where(sub == i1, -jnp.inf, blk), axis=0, keepdims=True)
        gscore.append(m1 + m2)
    masked = []
    for g in range(N_GROUPS):
        beats = jnp.zeros((1, tm), F32)
        for m in range(N_GROUPS):
            if m == g:
                continue
            b = gscore[m] > gscore[g]
            if m < g:
                b = jnp.logical_or(b, gscore[m] == gscore[g])
            beats = beats + b.astype(F32)
        masked.append(jnp.where(beats < TOPK_GROUPS, blocks[g], NEG_INF))
    cur = jnp.concatenate(masked, axis=0)
    eio = lax.broadcasted_iota(I32, (N_EXPERTS, tm), 0)
    sels, eids, ws = [], [], []
    for _ in range(TOP_K):
        m = jnp.max(cur, axis=0, keepdims=True)
        idx = jnp.min(jnp.where(cur == m, eio, N_EXPERTS), axis=0, keepdims=True)
        sel = eio == idx
        sels.append(sel)
        eids.append(idx)
        ws.append(jnp.sum(jnp.where(sel, scores, 0.0), axis=0, keepdims=True))
        cur = jnp.where(sel, -jnp.inf, cur)
    wsum = ws[0]
    for k in range(1, TOP_K):
        wsum = wsum + ws[k]
    w_ref[...] = jnp.concatenate(ws, axis=0) / wsum * ROUTED_SCALE
    eidx_ref[...] = jnp.concatenate(eids, axis=0)

    onehot = jnp.zeros((N_EXPERTS, tm), F32)
    for sel in sels:
        onehot = onehot + sel.astype(F32)
    r_i = lax.broadcasted_iota(I32, (tm, tm), 0)
    c_i = lax.broadcasted_iota(I32, (tm, tm), 1)
    before = (r_i < c_i).astype(BF16)
    prior = _bdot(onehot.astype(BF16), before) + carry_sc[...]
    ranks = [jnp.sum(jnp.where(sel, prior, 0.0), axis=0, keepdims=True) for sel in sels]
    rank_ref[...] = jnp.concatenate(ranks, axis=0).astype(I32)
    total = carry_sc[...] + jnp.sum(onehot, axis=1, keepdims=True)
    carry_sc[...] = total
    cnt_ref[...] = total


def _router(x, sh, sc, gt, nw, w_router, e_bias, wsg, wsu, wsd):
    bsz, s, d = x.shape
    t = bsz * s
    tm = TM_ROUTER
    nt = s // tm
    sd = wsg.shape[1]
    vec = pl.BlockSpec((1, 1, d), lambda b, j: (b, 0, 0))
    full2 = lambda shape: pl.BlockSpec(shape, lambda b, j: (0, 0))
    tok = pl.BlockSpec((tm, d), lambda b, j: (b * nt + j, 0))
    slot = pl.BlockSpec((TOP_K, tm), lambda b, j: (0, b * nt + j))
    return pl.pallas_call(
        _router_kernel,
        grid=(bsz, nt),
        in_specs=[pl.BlockSpec((1, tm, d), lambda b, j: (b, j, 0)), vec, vec, vec,
                  full2((1, d)), full2((N_EXPERTS, d)), full2((N_EXPERTS, 1)),
                  full2((d, sd)), full2((d, sd)), full2((sd, d))],
        out_specs=[tok, tok, slot, slot, slot, full2((N_EXPERTS, 1))],
        out_shape=[jax.ShapeDtypeStruct((t, d), F32), jax.ShapeDtypeStruct((t, d), F32),
                   jax.ShapeDtypeStruct((TOP_K, t), I32), jax.ShapeDtypeStruct((TOP_K, t), I32),
                   jax.ShapeDtypeStruct((TOP_K, t), F32), jax.ShapeDtypeStruct((N_EXPERTS, 1), F32)],
        scratch_shapes=[pltpu.VMEM((N_EXPERTS, 1), F32)],
        compiler_params=_cparams(("arbitrary", "arbitrary")),
    )(x, sh, sc, gt, nw.reshape(1, d), w_router.T, e_bias.reshape(N_EXPERTS, 1),
      wsg.astype(BF16), wsu.astype(BF16), wsd.astype(BF16))


def _dest_kernel(starts_ref, e_ref, r_ref, o_ref):
    e = e_ref[...]
    acc = r_ref[...]
    for j in range(N_EXPERTS):
        acc = acc + jnp.where(e == j, starts_ref[j], 0)
    o_ref[...] = acc


def _dest(starts, eidx, rank):
    t = eidx.shape[1]
    tm = min(TM_DEST, t)
    blk = pl.BlockSpec((TOP_K, tm), lambda i, s: (0, i))
    return pl.pallas_call(
        _dest_kernel,
        grid_spec=pltpu.PrefetchScalarGridSpec(
            num_scalar_prefetch=1, grid=(t // tm,), in_specs=[blk, blk], out_specs=blk),
        out_shape=jax.ShapeDtypeStruct((TOP_K, t), I32),
        compiler_params=_cparams(("arbitrary",)),
    )(starts, eidx, rank)


def _dispatch_kernel(dest_ref, h_ref, xs_ref, sem):
    tm = dest_ref.shape[1]

    def issue(t, carry):
        for k in range(TOP_K):
            pltpu.make_async_copy(h_ref.at[pl.ds(t, 1)],
                                  xs_ref.at[pl.ds(dest_ref[k, t], 1)], sem).start()
        return carry

    lax.fori_loop(0, tm, issue, 0)

    def drain(t, carry):
        for k in range(TOP_K):
            pltpu.make_async_copy(h_ref.at[pl.ds(0, 1)], xs_ref.at[pl.ds(0, 1)], sem).wait()
        return carry

    lax.fori_loop(0, tm, drain, 0)


def _dispatch(dest, h):
    t, d = h.shape
    tm = TM_DISPATCH
    return pl.pallas_call(
        _dispatch_kernel,
        grid=(t // tm,),
        in_specs=[pl.BlockSpec((TOP_K, tm), lambda i: (0, i), memory_space=pltpu.SMEM),
                  pl.BlockSpec((tm, d), lambda i: (i, 0))],
        out_specs=pl.BlockSpec(memory_space=pl.ANY),
        out_shape=jax.ShapeDtypeStruct((t * TOP_K, d), F32),
        scratch_shapes=[pltpu.SemaphoreType.DMA(())],
        compiler_params=_cparams(("arbitrary",)),
    )(dest, h)


def _expert_kernel(vb_ref, ve_ref, vlo_ref, vhi_ref, vfirst_ref, vnew_ref,
                   x_ref, wg_ref, wu_ref, wd_ref, o_ref, wg_sc, wu_sc, wd_sc):
    v = pl.program_id(0)

    @pl.when(vnew_ref[v] == 1)
    def _():
        wg_sc[...] = wg_ref[0].astype(BF16)
        wu_sc[...] = wu_ref[0].astype(BF16)
        wd_sc[...] = wd_ref[0].astype(BF16)

    xb = x_ref[...].astype(BF16)
    act = (_silu(_bdot(xb, wg_sc[...])) * _bdot(xb, wu_sc[...])).astype(BF16)
    y = _bdot(act, wd_sc[...])
    rows = lax.broadcasted_iota(I32, (y.shape[0], 1), 0)
    y = jnp.where(jnp.logical_and(rows >= vlo_ref[v], rows < vhi_ref[v]), y, 0.0)

    @pl.when(vfirst_ref[v] == 1)
    def _():
        o_ref[...] = y

    @pl.when(vfirst_ref[v] == 0)
    def _():
        o_ref[...] += y


def _experts(meta, xs, w_gate, w_up, w_down):
    r, d = xs.shape
    _, _, ed = w_gate.shape
    bm = BM_EXPERT
    nv = meta[0].shape[0]
    xmap = lambda v, vb, ve, vlo, vhi, vf, vn: (vb[v], 0)
    wmap = lambda v, vb, ve, vlo, vhi, vf, vn: (ve[v], 0, 0)
    return pl.pallas_call(
        _expert_kernel,
        grid_spec=pltpu.PrefetchScalarGridSpec(
            num_scalar_prefetch=6, grid=(nv,),
            in_specs=[pl.BlockSpec((bm, d), xmap), pl.BlockSpec((1, d, ed), wmap),
                      pl.BlockSpec((1, d, ed), wmap), pl.BlockSpec((1, ed, d), wmap)],
            out_specs=pl.BlockSpec((bm, d), xmap),
            scratch_shapes=[pltpu.VMEM((d, ed), BF16), pltpu.VMEM((d, ed), BF16),
                            pltpu.VMEM((ed, d), BF16)]),
        out_shape=jax.ShapeDtypeStruct((r, d), F32),
        compiler_params=_cparams(("arbitrary",)),
    )(*meta, xs, w_gate, w_up, w_down)


def _visit_meta(counts, n_rows):
    bm = BM_EXPERT
    nb = n_rows // bm
    nv = nb + N_EXPERTS - 1
    ends = jnp.cumsum(counts)
    starts = ends - counts
    blo = jnp.arange(nb, dtype=I32) * bm
    first_e = jnp.sum((ends[None, :] <= blo[:, None]).astype(I32), axis=1)
    last_e = jnp.sum((starts[None, :] < (blo + bm)[:, None]).astype(I32), axis=1) - 1
    per = last_e - first_e + 1
    off_incl = jnp.cumsum(per)
    off = off_incl - per
    total = off_incl[-1]
    vi = jnp.arange(nv, dtype=I32)
    real = vi < total
    vb = jnp.minimum(jnp.sum((off_incl[None, :] <= vi[:, None]).astype(I32), axis=1), nb - 1)
    ve_real = first_e[vb] + (vi - off[vb])
    ve_last = last_e[nb - 1]
    ve = jnp.where(real, ve_real, ve_last).astype(I32)
    lo = jnp.clip(starts[ve] - vb * bm, 0, bm)
    hi = jnp.clip(ends[ve] - vb * bm, 0, bm)
    vlo = jnp.where(real, lo, 0).astype(I32)
    vhi = jnp.where(real, hi, 0).astype(I32)
    vfirst = jnp.logical_and(real, vi == off[vb]).astype(I32)
    prev = jnp.concatenate([jnp.full((1,), -1, I32), ve[:-1]])
    vnew = (ve != prev).astype(I32)
    return starts.astype(I32), (vb, ve, vlo, vhi, vfirst, vnew)


def _combine_kernel(dest_ref, ys_ref, w_ref, base_ref, g_ref, fn_ref, o_ref, buf, sem, *, final):
    tm = base_ref.shape[0]

    def issue(t, carry):
        for k in range(TOP_K):
            pltpu.make_async_copy(ys_ref.at[pl.ds(dest_ref[k, t], 1)],
                                  buf.at[k, pl.ds(t, 1)], sem).start()
        return carry

    lax.fori_loop(0, tm, issue, 0)

    def drain(t, carry):
        for k in range(TOP_K):
            pltpu.make_async_copy(ys_ref.at[pl.ds(0, 1)], buf.at[0, pl.ds(0, 1)], sem).wait()
        return carry

    lax.fori_loop(0, tm, drain, 0)

    w = w_ref[...]
    acc = w[:, 0:1] * buf[0]
    for k in range(1, TOP_K):
        acc = acc + w[:, k:k + 1] * buf[k]
    out = base_ref[...] + g_ref[0] * acc
    if final:
        out = _rms(out, fn_ref[...])
    o_ref[...] = out


def _combine(dest, ys, w_tok, base, gt, fnorm, seq, final):
    t, d = base.shape
    tm = TM_COMBINE
    nt = seq // tm
    return pl.pallas_call(
        functools.partial(_combine_kernel, final=final),
        grid=(t // tm,),
        in_specs=[pl.BlockSpec((TOP_K, tm), lambda i: (0, i), memory_space=pltpu.SMEM),
                  pl.BlockSpec(memory_space=pl.ANY),
                  pl.BlockSpec((tm, TOP_K), lambda i: (i, 0)),
                  pl.BlockSpec((tm, d), lambda i: (i, 0)),
                  pl.BlockSpec((1, 1, d), lambda i: (i // nt, 0, 0)),
                  pl.BlockSpec((1, d), lambda i: (0, 0))],
        out_specs=pl.BlockSpec((tm, d), lambda i: (i, 0)),
        out_shape=jax.ShapeDtypeStruct((t, d), F32),
        scratch_shapes=[pltpu.VMEM((TOP_K, tm, d), F32), pltpu.SemaphoreType.DMA(())],
        compiler_params=_cparams(("arbitrary",)),
    )(dest, ys, w_tok, base, gt, fnorm.reshape(1, d))


def _moe(x, sh, sc, gt, nw, w_router, e_bias, w_gate, w_up, w_down, wsg, wsu, wsd, fnorm, final):
    bsz, s, d = x.shape
    t = bsz * s
    h, base, eidx, rank, w, cnt = _router(x, sh, sc, gt, nw, w_router, e_bias, wsg, wsu, wsd)
    counts = cnt.reshape(N_EXPERTS).astype(I32)
    starts, meta = _visit_meta(counts, t * TOP_K)
    dest = _dest(starts, eidx, rank)
    xs = _dispatch(dest, h)
    ys = _experts(meta, xs, w_gate, w_up, w_down)
    out = _combine(dest, ys, w.T, base, gt, fnorm, s, final)
    return out.reshape(bsz, s, d)


def _kv_kernel(x_ref, sh_ref, sc_ref, nw_ref, wk_ref, wv_ref, k_ref, v_ref, km_ref):
    x = x_ref[0]
    hb = (_rms(x, nw_ref[...]) * (1.0 + sc_ref[0]) + sh_ref[0]).astype(BF16)
    k = _bdot(hb, wk_ref[...])
    v = _bdot(hb, wv_ref[...])
    for p in range(k.shape[1] // LANES):
        k_ref[0, p] = k[:, p * LANES:(p + 1) * LANES].astype(BF16)
        v_ref[0, p] = v[:, p * LANES:(p + 1) * LANES].astype(BF16)
    km_ref[0, 0] = jnp.mean(k, axis=0, keepdims=True)


def _kv(x, sh, sc, nw, w_k, w_v):
    bsz, s, d = x.shape
    nb = s // B_BLOCK
    npair = d // LANES
    vec = pl.BlockSpec((1, 1, d), lambda b, j: (b, 0, 0))
    full2 = lambda shape: pl.BlockSpec(shape, lambda b, j: (0, 0))
    kvspec = pl.BlockSpec((1, npair, B_BLOCK, LANES), lambda b, j: (b, 0, j, 0))
    return pl.pallas_call(
        _kv_kernel,
        grid=(bsz, nb),
        in_specs=[pl.BlockSpec((1, B_BLOCK, d), lambda b, j: (b, j, 0)), vec, vec,
                  full2((1, d)), full2((d, d)), full2((d, d))],
        out_specs=[kvspec, kvspec, pl.BlockSpec((1, 1, 1, d), lambda b, j: (b, j, 0, 0))],
        out_shape=[jax.ShapeDtypeStruct((bsz, npair, s, LANES), BF16),
                   jax.ShapeDtypeStruct((bsz, npair, s, LANES), BF16),
                   jax.ShapeDtypeStruct((bsz, nb, 1, d), F32)],
        compiler_params=_cparams(("arbitrary", "arbitrary")),
    )(x, sh, sc, nw.reshape(1, d), w_k.astype(BF16), w_v.astype(BF16))


def _attn_kernel(x_ref, sh_ref, sc_ref, g_ref, nw_ref, wq_ref, wo_ref, k_ref, v_ref, km_ref,
                 o_ref, q_sc, oh_sc, *, nb, n_sel):
    qb = pl.program_id(1)
    x = x_ref[0]
    bq = x.shape[0]
    npair = q_sc.shape[0]
    hd = LANES // 2
    scale = float(hd) ** -0.5
    h = _rms(x, nw_ref[...]) * (1.0 + sc_ref[0]) + sh_ref[0]
    q = _bdot(h.astype(BF16), wq_ref[...])
    for p in range(npair):
        q_sc[p] = q[:, p * LANES:(p + 1) * LANES]

    lane = lax.broadcasted_iota(I32, (1, LANES), 1)
    r_i = lax.broadcasted_iota(I32, (bq, B_BLOCK), 0)
    c_i = lax.broadcasted_iota(I32, (bq, B_BLOCK), 1)
    causal = c_i <= r_i
    past_lane = lane < qb
    own0 = pl.multiple_of(qb * B_BLOCK, B_BLOCK)

    def pair_body(p, carry):
        q2 = q_sc[p]
        km2 = km_ref[0, p]
        outs = []
        for e in range(2):
            hm = (lane >= hd) if e == 1 else (lane < hd)
            qe = jnp.where(hm, q2, 0.0)
            gate = _dot3_nt(qe, km2)
            selmat = jnp.zeros((bq, LANES), F32)
            for n in range(nb):
                gn = gate[:, n:n + 1]
                beats = jnp.logical_or(gate > gn, jnp.logical_and(gate == gn, lane < n))
                beats = jnp.logical_and(beats, past_lane)
                cnt = jnp.sum(beats.astype(F32), axis=1, keepdims=True)
                selmat = jnp.where(lane == n, (cnt < n_sel).astype(F32), selmat)
            qs = (qe * scale).astype(BF16)

            s_own = jnp.where(causal, _bdot_nt(qs, k_ref[0, p, pl.ds(own0, B_BLOCK), :]), NEG_INF)
            m0 = jnp.max(s_own, axis=1, keepdims=True)
            p0 = jnp.exp(s_own - m0)
            l0 = jnp.sum(p0, axis=1, keepdims=True)
            a0 = _bdot(p0.astype(BF16), v_ref[0, p, pl.ds(own0, B_BLOCK), :])

            def kb_body(kb, st):
                m, l, acc = st
                k0 = pl.multiple_of(kb * B_BLOCK, B_BLOCK)
                s = _bdot_nt(qs, k_ref[0, p, pl.ds(k0, B_BLOCK), :])
                colsel = jnp.sum(jnp.where(lane == kb, selmat, 0.0), axis=1, keepdims=True)
                s = jnp.where(colsel > 0.5, s, NEG_INF)
                m_new = jnp.maximum(m, jnp.max(s, axis=1, keepdims=True))
                alpha = jnp.exp(m - m_new)
                pe = jnp.exp(s - m_new)
                l = alpha * l + jnp.sum(pe, axis=1, keepdims=True)
                acc = alpha * acc + _bdot(pe.astype(BF16), v_ref[0, p, pl.ds(k0, B_BLOCK), :])
                return m_new, l, acc

            _, l, acc = lax.fori_loop(0, qb, kb_body, (m0, l0, a0))
            outs.append(acc / l)
        oh_sc[p] = jnp.where(lane < hd, outs[0], outs[1]).astype(BF16)
        return carry

    lax.fori_loop(0, npair, pair_body, 0)
    o = jnp.concatenate([oh_sc[p] for p in range(npair)], axis=1)
    o_ref[0] = x + g_ref[0] * _bdot(o, wo_ref[...])


def _attn(x, sh, sc, gt, nw, w_q, w_o, k2, v2, km2):
    bsz, s, d = x.shape
    nb = s // B_BLOCK
    npair = d // LANES
    n_sel = min(B_TOPK, nb - 1)
    vec = pl.BlockSpec((1, 1, d), lambda b, j: (b, 0, 0))
    full2 = lambda shape: pl.BlockSpec(shape, lambda b, j: (0, 0))
    kvspec = pl.BlockSpec((1, npair, s, LANES), lambda b, j: (b, 0, 0, 0))
    return pl.pallas_call(
        functools.partial(_attn_kernel, nb=nb, n_sel=n_sel),
        grid=(bsz, nb),
        in_specs=[pl.BlockSpec((1, B_BLOCK, d), lambda b, j: (b, j, 0)), vec, vec, vec,
                  full2((1, d)), full2((d, d)), full2((d, d)), kvspec, kvspec,
                  pl.BlockSpec((1, npair, LANES, LANES), lambda b, j: (b, 0, 0, 0))],
        out_specs=pl.BlockSpec((1, B_BLOCK, d), lambda b, j: (b, j, 0)),
        out_shape=jax.ShapeDtypeStruct((bsz, s, d), F32),
        scratch_shapes=[pltpu.VMEM((npair, B_BLOCK, LANES), F32),
                        pltpu.VMEM((npair, B_BLOCK, LANES), BF16)],
        compiler_params=_cparams(("arbitrary", "arbitrary")),
    )(x, sh, sc, gt, nw.reshape(1, d), w_q.astype(BF16), w_o.astype(BF16), k2, v2, km2)


def kernel(x, c, ada_w, ada_b, norm_mix, norm_ffn, a_w_in, a_b_in, a_ln_g, a_ln_b, a_w_s, a_b_s,
           a_w_out, kv_norm, kv_ada_w, kv_ada_b, kv_w_k, kv_w_v, b_w_q, b_w_o, moe_router, moe_bias,
           moe_w_gate, moe_w_up, moe_w_down, sh_w_gate, sh_w_up, sh_w_down, final_norm):
    bsz, s, d = x.shape
    depth = ada_w.shape[0]
    n_a = a_w_in.shape[0]
    assert s % B_BLOCK == 0 and s % TM_GMLP == 0 and d % LANES == 0 and s // B_BLOCK <= LANES
    nb = s // B_BLOCK
    npair = d // LANES

    def mods(w, b, n):
        m = _ada(c, w, b)
        return [m[:, i * d:(i + 1) * d].reshape(bsz, 1, d) for i in range(n)]

    k2 = v2 = km2 = None
    for i in range(depth):
        sh1, sc1, g1, sh2, sc2, g2 = mods(ada_w[i], ada_b[i], 6)
        if i < n_a:
            x = _gmlp(x, sh1, sc1, g1, norm_mix[i], a_w_in[i], a_b_in[i], a_ln_g[i], a_ln_b[i],
                      a_w_s[i], a_b_s[i], a_w_out[i])
        else:
            if k2 is None:
                ksh, ksc = mods(kv_ada_w, kv_ada_b, 2)
                k2, v2, km = _kv(x, ksh, ksc, kv_norm, kv_w_k, kv_w_v)
                km = km.reshape(bsz, nb, npair, LANES).transpose(0, 2, 1, 3)
                km2 = jnp.pad(km, ((0, 0), (0, 0), (0, LANES - nb), (0, 0)))
            j = i - n_a
            x = _attn(x, sh1, sc1, g1, norm_mix[i], b_w_q[j], b_w_o[j], k2, v2, km2)
        x = _moe(x, sh2, sc2, g2, norm_ffn[i], moe_router[i], moe_bias[i], moe_w_gate[i],
                 moe_w_up[i], moe_w_down[i], sh_w_gate[i], sh_w_up[i], sh_w_down[i],
                 final_norm, i == depth - 1)
    return x
```

```python
import functools

import jax
import jax.numpy as jnp
from jax import lax
from jax.experimental import pallas as pl
from jax.experimental.pallas import tpu as pltpu

F32 = jnp.float32
BF16 = jnp.bfloat16
I32 = jnp.int32

RMS_EPS = 1e-6
LN_EPS = 1e-5
NEG_INF = -1e30

A_CHUNK = 128
A_GROUPS = 8
B_HEADS = 16
B_BLOCK = 256
B_TOPK = 3
N_EXPERTS = 64
TOP_K = 8
N_GROUPS = 8
TOPK_GROUPS = 4
ROUTED_SCALE = 2.5

LANES = 128
VMEM_LIMIT = 56 * 1024 * 1024

TM_GMLP = 256
TM_ROUTER = 256
TM_DEST = 2048
TM_DISPATCH = 256
TM_COMBINE = 128
BM_EXPERT = 256
ATTN_PAIR_UNROLL = 4


def _cparams(sem):
    return pltpu.CompilerParams(dimension_semantics=sem, vmem_limit_bytes=VMEM_LIMIT)


def _sigmoid(x):
    return 1.0 / (1.0 + jnp.exp(-x))


def _silu(x):
    return x * _sigmoid(x)


def _gelu_tanh(x):
    return 0.5 * x * (1.0 + jnp.tanh(0.7978845608028654 * (x + 0.044715 * (x * x * x))))


def _rms(x, g):
    return x * lax.rsqrt(jnp.mean(x * x, axis=-1, keepdims=True) + RMS_EPS) * g


def _bdot(a, b):
    return jnp.dot(a, b, preferred_element_type=F32)


def _bdot_nt(a, b):
    return lax.dot_general(a, b, (((1,), (1,)), ((), ())), preferred_element_type=F32)


def _split(a):
    hi = a.astype(BF16)
    lo = (a - hi.astype(F32)).astype(BF16)
    return hi, lo


def _dot3_nt(a, b):
    ah, al = _split(a)
    bh, bl = _split(b)
    return _bdot_nt(ah, bh) + (_bdot_nt(ah, bl) + _bdot_nt(al, bh))


def _dot3(a, b):
    ah, al = _split(a)
    bh, bl = _split(b)
    return _bdot(ah, bh) + (_bdot(ah, bl) + _bdot(al, bh))


def _ada_kernel(c_ref, w_ref, b_ref, o_ref):
    a = _silu(c_ref[...]).astype(BF16)
    o_ref[0] = _bdot(a, w_ref[0].astype(BF16)) + b_ref[0]


def _ada(c, w, b):
    bsz, d = c.shape
    nl, _, n = w.shape
    tn = 1024
    return pl.pallas_call(
        _ada_kernel,
        grid=(nl, n // tn),
        in_specs=[pl.BlockSpec((bsz, d), lambda l, j: (0, 0)),
                  pl.BlockSpec((1, d, tn), lambda l, j: (l, 0, j)),
                  pl.BlockSpec((1, 1, tn), lambda l, j: (l, 0, j))],
        out_specs=pl.BlockSpec((1, bsz, tn), lambda l, j: (l, 0, j)),
        out_shape=jax.ShapeDtypeStruct((nl, bsz, n), F32),
        compiler_params=_cparams(("arbitrary", "arbitrary")),
    )(c, w, b.reshape(nl, 1, n))


def _gmlp_kernel(x_ref, sh_ref, sc_ref, g_ref, nw_ref, win_ref, bin_ref, lng_ref, lnb_ref,
                 ws_ref, bst_ref, wout_ref, o_ref, y_sc):
    x = x_ref[0]
    tm = x.shape[0]
    h = _rms(x, nw_ref[...]) * (1.0 + sc_ref[0]) + sh_ref[0]
    z = _gelu_tanh(_bdot(h.astype(BF16), win_ref[...]) + bin_ref[...])
    aw = z.shape[1] // 2
    gd = aw // A_GROUPS
    u = z[:, :aw]
    v = z[:, aw:]
    mu = jnp.mean(v, axis=-1, keepdims=True)
    dv = v - mu
    var = jnp.mean(dv * dv, axis=-1, keepdims=True)
    vn = (dv * lax.rsqrt(var + LN_EPS) * lng_ref[...] + lnb_ref[...]).astype(BF16)
    row = lax.broadcasted_iota(I32, (A_CHUNK, A_CHUNK), 0)
    col = lax.broadcasted_iota(I32, (A_CHUNK, A_CHUNK), 1)
    causal = col <= row
    for g in range(A_GROUPS):
        wg = jnp.where(causal, ws_ref[g], 0.0).astype(BF16)
        bcol = bst_ref[:, g:g + 1]
        for ci in range(tm // A_CHUNK):
            rs = slice(ci * A_CHUNK, (ci + 1) * A_CHUNK)
            cs = slice(g * gd, (g + 1) * gd)
            sv = _bdot(wg, vn[rs, cs]) + bcol
            y_sc[rs, cs] = (u[rs, cs] * sv).astype(BF16)
    o_ref[0] = x + g_ref[0] * _bdot(y_sc[...], wout_ref[...])


def _gmlp(x, sh, sc, gt, nw, w_in, b_in, ln_g, ln_b, w_s, b_s, w_out):
    bsz, s, d = x.shape
    tm = TM_GMLP
    n_in = w_in.shape[1]
    aw = n_in // 2
    vec = pl.BlockSpec((1, 1, d), lambda b, j: (b, 0, 0))
    full2 = lambda shape: pl.BlockSpec(shape, lambda b, j: (0, 0))
    return pl.pallas_call(
        _gmlp_kernel,
        grid=(bsz, s // tm),
        in_specs=[pl.BlockSpec((1, tm, d), lambda b, j: (b, j, 0)), vec, vec, vec,
                  full2((1, d)), full2((d, n_in)), full2((1, n_in)), full2((1, aw)), full2((1, aw)),
                  pl.BlockSpec((A_GROUPS, A_CHUNK, A_CHUNK), lambda b, j: (0, 0, 0)),
                  full2((A_CHUNK, A_GROUPS)), full2((aw, d))],
        out_specs=pl.BlockSpec((1, tm, d), lambda b, j: (b, j, 0)),
        out_shape=jax.ShapeDtypeStruct((bsz, s, d), F32),
        scratch_shapes=[pltpu.VMEM((tm, aw), BF16)],
        compiler_params=_cparams(("arbitrary", "arbitrary")),
    )(x, sh, sc, gt, nw.reshape(1, d), w_in.astype(BF16), b_in.reshape(1, n_in),
      ln_g.reshape(1, aw), ln_b.reshape(1, aw), w_s, b_s.T, w_out.astype(BF16))


def _router_kernel(x_ref, sh_ref, sc_ref, g_ref, nw_ref, wrt_ref, bias_ref, wsg_ref, wsu_ref, wsd_ref,
                   h_ref, base_ref, eidx_ref, rank_ref, w_ref, cnt_ref, carry_sc):
    first = jnp.logical_and(pl.program_id(0) == 0, pl.program_id(1) == 0)

    @pl.when(first)
    def _():
        carry_sc[...] = jnp.zeros_like(carry_sc)

    x = x_ref[0]
    tm = x.shape[0]
    h = _rms(x, nw_ref[...]) * (1.0 + sc_ref[0]) + sh_ref[0]
    h_ref[...] = h
    hb = h.astype(BF16)
    act = (_silu(_bdot(hb, wsg_ref[...])) * _bdot(hb, wsu_ref[...])).astype(BF16)
    base_ref[...] = x + g_ref[0] * _bdot(act, wsd_ref[...])

    scores = _sigmoid(_dot3_nt(wrt_ref[...], h))
    choice = scores + bias_ref[...]
    gsz = N_EXPERTS // N_GROUPS
    sub = lax.broadcasted_iota(I32, (gsz, tm), 0)
    blocks = [choice[g * gsz:(g + 1) * gsz] for g in range(N_GROUPS)]
    gscore = []
    for blk in blocks:
        m1 = jnp.max(blk, axis=0, keepdims=True)
        i1 = jnp.min(jnp.where(blk == m1, sub, gsz), axis=0, keepdims=True)
        m2 = jnp.max(jnp.where(sub == i1, -jnp.inf, blk), axis=0, keepdims=True)
        gscore.append(m1 + m2)
    masked = []
    for g in range(N_GROUPS):
        beats = jnp.zeros((1, tm), F32)
        for m in range(N_GROUPS):
            if m == g:
                continue
            b = gscore[m] > gscore[g]
            if m < g:
                b = jnp.logical_or(b, gscore[m] == gscore[g])
            beats = beats + b.astype(F32)
        masked.append(jnp.where(beats < TOPK_GROUPS, blocks[g], NEG_INF))
    cur = jnp.concatenate(masked, axis=0)
    eio = lax.broadcasted_iota(I32, (N_EXPERTS, tm), 0)
    sels, eids, ws = [], [], []
    for _ in range(TOP_K):
        m = jnp.max(cur, axis=0, keepdims=True)
        idx = jnp.min(jnp.where(cur == m, eio, N_EXPERTS), axis=0, keepdims=True)
        sel = eio == idx
        sels.append(sel)
        eids.append(idx)
        ws.append(jnp.sum(jnp.where(sel, scores, 0.0), axis=0, keepdims=True))
        cur = jnp.where(sel, -jnp.inf, cur)
    wsum = ws[0]
    for k in range(1, TOP_K):
        wsum = wsum + ws[k]
    w_ref[...] = jnp.concatenate(ws, axis=0) / wsum * ROUTED_SCALE
    eidx_ref[...] = jnp.concatenate(eids, axis=0)

    onehot = jnp.zeros((N_EXPERTS, tm), F32)
    for sel in sels:
        onehot = onehot + sel.astype(F32)
    r_i = lax.broadcasted_iota(I32, (tm, tm), 0)
    c_i = lax.broadcasted_iota(I32, (tm, tm), 1)
    before = (r_i < c_i).astype(BF16)
    prior = _bdot(onehot.astype(BF16), before) + carry_sc[...]
    ranks = [jnp.sum(jnp.where(sel, prior, 0.0), axis=0, keepdims=True) for sel in sels]
    rank_ref[...] = jnp.concatenate(ranks, axis=0).astype(I32)
    total = carry_sc[...] + jnp.sum(onehot, axis=1, keepdims=True)
    carry_sc[...] = total
    cnt_ref[...] = total


def _router(x, sh, sc, gt, nw, w_router, e_bias, wsg, wsu, wsd):
    bsz, s, d = x.shape
    t = bsz * s
    tm = TM_ROUTER
    nt = s // tm
    sd = wsg.shape[1]
    vec = pl.BlockSpec((1, 1, d), lambda b, j: (b, 0, 0))
    full2 = lambda shape: pl.BlockSpec(shape, lambda b, j: (0, 0))
    tok = pl.BlockSpec((tm, d), lambda b, j: (b * nt + j, 0))
    slot = pl.BlockSpec((TOP_K, tm), lambda b, j: (0, b * nt + j))
    return pl.pallas_call(
        _router_kernel,
        grid=(bsz, nt),
        in_specs=[pl.BlockSpec((1, tm, d), lambda b, j: (b, j, 0)), vec, vec, vec,
                  full2((1, d)), full2((N_EXPERTS, d)), full2((N_EXPERTS, 1)),
                  full2((d, sd)), full2((d, sd)), full2((sd, d))],
        out_specs=[tok, tok, slot, slot, slot, full2((N_EXPERTS, 1))],
        out_shape=[jax.ShapeDtypeStruct((t, d), F32), jax.ShapeDtypeStruct((t, d), F32),
                   jax.ShapeDtypeStruct((TOP_K, t), I32), jax.ShapeDtypeStruct((TOP_K, t), I32),
                   jax.ShapeDtypeStruct((TOP_K, t), F32), jax.ShapeDtypeStruct((N_EXPERTS, 1), F32)],
        scratch_shapes=[pltpu.VMEM((N_EXPERTS, 1), F32)],
        compiler_params=_cparams(("arbitrary", "arbitrary")),
    )(x, sh, sc, gt, nw.reshape(1, d), w_router.T, e_bias.reshape(N_EXPERTS, 1),
      wsg.astype(BF16), wsu.astype(BF16), wsd.astype(BF16))


def _dest_kernel(starts_ref, e_ref, r_ref, o_ref):
    e = e_ref[...]
    acc = r_ref[...]
    for j in range(N_EXPERTS):
        acc = acc + jnp.where(e == j, starts_ref[j], 0)
    o_ref[...] = acc


def _dest(starts, eidx, rank):
    t = eidx.shape[1]
    tm = min(TM_DEST, t)
    blk = pl.BlockSpec((TOP_K, tm), lambda i, s: (0, i))
    return pl.pallas_call(
        _dest_kernel,
        grid_spec=pltpu.PrefetchScalarGridSpec(
            num_scalar_prefetch=1, grid=(t // tm,), in_specs=[blk, blk], out_specs=blk),
        out_shape=jax.ShapeDtypeStruct((TOP_K, t), I32),
        compiler_params=_cparams(("arbitrary",)),
    )(starts, eidx, rank)


def _dispatch_kernel(dest_ref, h_ref, xs_ref, sem):
    tm = dest_ref.shape[1]

    def issue(t, carry):
        for k in range(TOP_K):
            pltpu.make_async_copy(h_ref.at[pl.ds(t, 1)],
                                  xs_ref.at[pl.ds(dest_ref[k, t], 1)], sem).start()
        return carry

    lax.fori_loop(0, tm, issue, 0)

    def drain(t, carry):
        for k in range(TOP_K):
            pltpu.make_async_copy(h_ref.at[pl.ds(0, 1)], xs_ref.at[pl.ds(0, 1)], sem).wait()
        return carry

    lax.fori_loop(0, tm, drain, 0)


def _dispatch(dest, h):
    t, d = h.shape
    tm = TM_DISPATCH
    return pl.pallas_call(
        _dispatch_kernel,
        grid=(t // tm,),
        in_specs=[pl.BlockSpec((TOP_K, tm), lambda i: (0, i), memory_space=pltpu.SMEM),
                  pl.BlockSpec((tm, d), lambda i: (i, 0))],
        out_specs=pl.BlockSpec(memory_space=pl.ANY),
        out_shape=jax.ShapeDtypeStruct((t * TOP_K, d), F32),
        scratch_shapes=[pltpu.SemaphoreType.DMA(())],
        compiler_params=_cparams(("arbitrary",)),
    )(dest, h)


def _expert_kernel(vb_ref, ve_ref, vlo_ref, vhi_ref, vfirst_ref, vnew_ref,
                   x_ref, wg_ref, wu_ref, wd_ref, o_ref, wg_sc, wu_sc, wd_sc):
    v = pl.program_id(0)

    @pl.when(vnew_ref[v] == 1)
    def _():
        wg_sc[...] = wg_ref[0, 0].astype(BF16)
        wu_sc[...] = wu_ref[0, 0].astype(BF16)
        wd_sc[...] = wd_ref[0, 0].astype(BF16)

    xb = x_ref[...].astype(BF16)
    act = (_silu(_bdot(xb, wg_sc[...])) * _bdot(xb, wu_sc[...])).astype(BF16)
    y = _bdot(act, wd_sc[...])
    rows = lax.broadcasted_iota(I32, (y.shape[0], 1), 0)
    y = jnp.where(jnp.logical_and(rows >= vlo_ref[v], rows < vhi_ref[v]), y, 0.0)

    @pl.when(vfirst_ref[v] == 1)
    def _():
        o_ref[...] = y

    @pl.when(vfirst_ref[v] == 0)
    def _():
        o_ref[...] += y


def _experts(meta, xs, layer, w_gate, w_up, w_down):
    r, d = xs.shape
    ed = w_gate.shape[-1]
    bm = BM_EXPERT
    nv = meta[0].shape[0]
    xmap = lambda v, vb, ve, vlo, vhi, vf, vn: (vb[v], 0)
    wmap = lambda v, vb, ve, vlo, vhi, vf, vn: (layer, ve[v], 0, 0)
    return pl.pallas_call(
        _expert_kernel,
        grid_spec=pltpu.PrefetchScalarGridSpec(
            num_scalar_prefetch=6, grid=(nv,),
            in_specs=[pl.BlockSpec((bm, d), xmap), pl.BlockSpec((1, 1, d, ed), wmap),
                      pl.BlockSpec((1, 1, d, ed), wmap), pl.BlockSpec((1, 1, ed, d), wmap)],
            out_specs=pl.BlockSpec((bm, d), xmap),
            scratch_shapes=[pltpu.VMEM((d, ed), BF16), pltpu.VMEM((d, ed), BF16),
                            pltpu.VMEM((ed, d), BF16)]),
        out_shape=jax.ShapeDtypeStruct((r, d), F32),
        compiler_params=_cparams(("arbitrary",)),
    )(*meta, xs, w_gate, w_up, w_down)


def _visit_meta(counts, n_rows):
    bm = BM_EXPERT
    nb = n_rows // bm
    nv = nb + N_EXPERTS - 1
    ends = jnp.cumsum(counts)
    starts = ends - counts
    blo = jnp.arange(nb, dtype=I32) * bm
    first_e = jnp.sum((ends[None, :] <= blo[:, None]).astype(I32), axis=1)
    last_e = jnp.sum((starts[None, :] < (blo + bm)[:, None]).astype(I32), axis=1) - 1
    per = last_e - first_e + 1
    off_incl = jnp.cumsum(per)
    off = off_incl - per
    total = off_incl[-1]
    vi = jnp.arange(nv, dtype=I32)
    real = vi < total
    vb = jnp.minimum(jnp.sum((off_incl[None, :] <= vi[:, None]).astype(I32), axis=1), nb - 1)
    ve_real = first_e[vb] + (vi - off[vb])
    ve_last = last_e[nb - 1]
    ve = jnp.where(real, ve_real, ve_last).astype(I32)
    lo = jnp.clip(starts[ve] - vb * bm, 0, bm)
    hi = jnp.clip(ends[ve] - vb * bm, 0, bm)
    vlo = jnp.where(real, lo, 0).astype(I32)
    vhi = jnp.where(real, hi, 0).astype(I32)
    vfirst = jnp.logical_and(real, vi == off[vb]).astype(I32)
    prev = jnp.concatenate([jnp.full((1,), -1, I32), ve[:-1]])
    vnew = (ve != prev).astype(I32)
    return starts.astype(I32), (vb, ve, vlo, vhi, vfirst, vnew)


def _combine_kernel(dest_ref, ys_ref, w_ref, base_ref, g_ref, fn_ref, o_ref, buf, sem, *, final):
    tm = base_ref.shape[0]

    def issue(t, carry):
        for k in range(TOP_K):
            pltpu.make_async_copy(ys_ref.at[pl.ds(dest_ref[k, t], 1)],
                                  buf.at[k, pl.ds(t, 1)], sem).start()
        return carry

    lax.fori_loop(0, tm, issue, 0)

    def drain(t, carry):
        for k in range(TOP_K):
            pltpu.make_async_copy(ys_ref.at[pl.ds(0, 1)], buf.at[0, pl.ds(0, 1)], sem).wait()
        return carry

    lax.fori_loop(0, tm, drain, 0)

    w = w_ref[...]
    acc = w[:, 0:1] * buf[0]
    for k in range(1, TOP_K):
        acc = acc + w[:, k:k + 1] * buf[k]
    out = base_ref[...] + g_ref[0] * acc
    if final:
        out = _rms(out, fn_ref[...])
    o_ref[...] = out


def _combine(dest, ys, w_tok, base, gt, fnorm, seq, final):
    t, d = base.shape
    tm = TM_COMBINE
    nt = seq // tm
    return pl.pallas_call(
        functools.partial(_combine_kernel, final=final),
        grid=(t // tm,),
        in_specs=[pl.BlockSpec((TOP_K, tm), lambda i: (0, i), memory_space=pltpu.SMEM),
                  pl.BlockSpec(memory_space=pl.ANY),
                  pl.BlockSpec((tm, TOP_K), lambda i: (i, 0)),
                  pl.BlockSpec((tm, d), lambda i: (i, 0)),
                  pl.BlockSpec((1, 1, d), lambda i: (i // nt, 0, 0)),
                  pl.BlockSpec((1, d), lambda i: (0, 0))],
        out_specs=pl.BlockSpec((tm, d), lambda i: (i, 0)),
        out_shape=jax.ShapeDtypeStruct((t, d), F32),
        scratch_shapes=[pltpu.VMEM((TOP_K, tm, d), F32), pltpu.SemaphoreType.DMA(())],
        compiler_params=_cparams(("arbitrary",)),
    )(dest, ys, w_tok, base, gt, fnorm.reshape(1, d))


def _moe(x, sh, sc, gt, nw, w_router, e_bias, layer, w_gate, w_up, w_down, wsg, wsu, wsd, fnorm, final):
    bsz, s, d = x.shape
    t = bsz * s
    h, base, eidx, rank, w, cnt = _router(x, sh, sc, gt, nw, w_router, e_bias, wsg, wsu, wsd)
    counts = cnt.reshape(N_EXPERTS).astype(I32)
    starts, meta = _visit_meta(counts, t * TOP_K)
    dest = _dest(starts, eidx, rank)
    xs = _dispatch(dest, h)
    ys = _experts(meta, xs, layer, w_gate, w_up, w_down)
    out = _combine(dest, ys, w.T, base, gt, fnorm, s, final)
    return out.reshape(bsz, s, d)


def _kv_kernel(x_ref, sh_ref, sc_ref, nw_ref, wk_ref, wvt_ref, k_ref, vt_ref, km_ref):
    x = x_ref[0]
    hb = (_rms(x, nw_ref[...]) * (1.0 + sc_ref[0]) + sh_ref[0]).astype(BF16)
    k = _bdot(hb, wk_ref[...])
    vt = _bdot_nt(wvt_ref[...], hb)
    for p in range(k.shape[1] // LANES):
        k_ref[0, p, 0] = k[:, p * LANES:(p + 1) * LANES].astype(BF16)
        vt_ref[0, p, 0] = vt[p * LANES:(p + 1) * LANES, :].astype(BF16)
    km_ref[0, 0] = jnp.mean(k, axis=0, keepdims=True)


def _kv(x, sh, sc, nw, w_k, w_v):
    bsz, s, d = x.shape
    nb = s // B_BLOCK
    npair = d // LANES
    vec = pl.BlockSpec((1, 1, d), lambda b, j: (b, 0, 0))
    full2 = lambda shape: pl.BlockSpec(shape, lambda b, j: (0, 0))
    return pl.pallas_call(
        _kv_kernel,
        grid=(bsz, nb),
        in_specs=[pl.BlockSpec((1, B_BLOCK, d), lambda b, j: (b, j, 0)), vec, vec,
                  full2((1, d)), full2((d, d)), full2((d, d))],
        out_specs=[pl.BlockSpec((1, npair, 1, B_BLOCK, LANES), lambda b, j: (b, 0, j, 0, 0)),
                   pl.BlockSpec((1, npair, 1, LANES, B_BLOCK), lambda b, j: (b, 0, j, 0, 0)),
                   pl.BlockSpec((1, 1, 1, d), lambda b, j: (b, j, 0, 0))],
        out_shape=[jax.ShapeDtypeStruct((bsz, npair, nb, B_BLOCK, LANES), BF16),
                   jax.ShapeDtypeStruct((bsz, npair, nb, LANES, B_BLOCK), BF16),
                   jax.ShapeDtypeStruct((bsz, nb, 1, d), F32)],
        compiler_params=_cparams(("arbitrary", "arbitrary")),
    )(x, sh, sc, nw.reshape(1, d), w_k.astype(BF16), w_v.T.astype(BF16))


def _attn_kernel(x_ref, sh_ref, sc_ref, g_ref, nw_ref, wqt_ref, wo_ref, k_ref, vt_ref, km_ref,
                 o_ref, qt_sc, qs_sc, acc_sc, sel_sc, m_sc, l_sc, *, nb, n_sel):
    qb = pl.program_id(1)
    x = x_ref[0]
    bq = x.shape[0]
    npair = qt_sc.shape[0]
    nbp = km_ref.shape[2]
    hd = LANES // 2
    scale = float(hd) ** -0.5
    h = _rms(x, nw_ref[...]) * (1.0 + sc_ref[0]) + sh_ref[0]
    qt = _bdot_nt(wqt_ref[...], h.astype(BF16))
    for p in range(npair):
        qt_sc[p] = qt[p * LANES:(p + 1) * LANES, :]

    subn = lax.broadcasted_iota(I32, (nbp, bq), 0)
    past = subn < qb
    krow = lax.broadcasted_iota(I32, (B_BLOCK, bq), 0)
    qcol = lax.broadcasted_iota(I32, (B_BLOCK, bq), 1)
    causal = krow <= qcol
    rowh = lax.broadcasted_iota(I32, (LANES, 1), 0)

    grp = ATTN_PAIR_UNROLL
    heads = [(u, e) for u in range(grp) for e in range(2)]

    def own_body(gi, carry):
        ps = [gi * grp + u for u in range(grp)]
        q2ts = [qt_sc[p] for p in ps]
        kms = [km_ref[0, p] for p in ps]
        kown = [k_ref[0, p, qb] for p in ps]
        vown = [vt_ref[0, p, qb] for p in ps]
        qets = [jnp.where((rowh >= hd) if e == 1 else (rowh < hd), q2ts[u], 0.0) for u, e in heads]
        qsts = [(q * scale).astype(BF16) for q in qets]
        ss = [jnp.where(causal, _bdot(kown[u], qsts[i]), NEG_INF) for i, (u, e) in enumerate(heads)]
        gates = [_dot3(kms[u], qets[i]) for i, (u, e) in enumerate(heads)]
        ms = [jnp.max(s, axis=0, keepdims=True) for s in ss]
        pes = [jnp.exp(s - m) for s, m in zip(ss, ms)]
        ls = [jnp.sum(pe, axis=0, keepdims=True) for pe in pes]
        accs = [_bdot(vown[u][e * hd:(e + 1) * hd, :], pes[i].astype(BF16))
                for i, (u, e) in enumerate(heads)]
        sels = []
        for gate in gates:
            selt = jnp.zeros((nbp, bq), F32)
            for n in range(nb):
                gn = gate[n:n + 1, :]
                beats = jnp.logical_or(gate > gn, jnp.logical_and(gate == gn, subn < n))
                beats = jnp.logical_and(beats, past)
                cnt = jnp.sum(beats.astype(F32), axis=0, keepdims=True)
                selt = jnp.where(subn == n, (cnt < n_sel).astype(F32), selt)
            sels.append(selt)
        for i, (u, e) in enumerate(heads):
            p = ps[u]
            sel_sc[p, e] = sels[i]
            qs_sc[p, e] = qsts[i]
            m_sc[p, e] = ms[i]
            l_sc[p, e] = ls[i]
            acc_sc[p, e * hd:(e + 1) * hd, :] = accs[i]
        return carry

    lax.fori_loop(0, npair // grp, own_body, 0)

    def kb_body(kb, carry):
        def group_body(gi, c2):
            ps = [gi * grp + u for u in range(grp)]
            kbl = [k_ref[0, p, kb] for p in ps]
            vbl = [vt_ref[0, p, kb] for p in ps]
            qsts = [qs_sc[ps[u], e] for u, e in heads]
            rows = [sel_sc[ps[u], e, pl.ds(kb, 1), :] for u, e in heads]
            m_old = [m_sc[ps[u], e] for u, e in heads]
            l_old = [l_sc[ps[u], e] for u, e in heads]
            a_old = [acc_sc[ps[u], e * hd:(e + 1) * hd, :] for u, e in heads]
            ss = [jnp.where(rows[i] > 0.5, _bdot(kbl[u], qsts[i]), NEG_INF)
                  for i, (u, e) in enumerate(heads)]
            m_new = [jnp.maximum(m, jnp.max(s, axis=0, keepdims=True)) for m, s in zip(m_old, ss)]
            alphas = [jnp.exp(m - mn) for m, mn in zip(m_old, m_new)]
            pes = [jnp.exp(s - mn) for s, mn in zip(ss, m_new)]
            l_new = [a * l + jnp.sum(pe, axis=0, keepdims=True) for a, l, pe in zip(alphas, l_old, pes)]
            a_new = [alphas[i] * a_old[i] + _bdot(vbl[u][e * hd:(e + 1) * hd, :], pes[i].astype(BF16))
                     for i, (u, e) in enumerate(heads)]
            for i, (u, e) in enumerate(heads):
                p = ps[u]
                m_sc[p, e] = m_new[i]
                l_sc[p, e] = l_new[i]
                acc_sc[p, e * hd:(e + 1) * hd, :] = a_new[i]
            return c2

        lax.fori_loop(0, npair // grp, group_body, 0)
        return carry

    lax.fori_loop(0, qb, kb_body, 0)

    parts = []
    for p in range(npair):
        for e in range(2):
            parts.append(acc_sc[p, e * hd:(e + 1) * hd, :] / l_sc[p, e])
    ot = jnp.concatenate(parts, axis=0)
    o_ref[0] = x + g_ref[0] * _bdot(ot.T.astype(BF16), wo_ref[...])


def _attn(x, sh, sc, gt, nw, w_q, w_o, k5, vt5, km2):
    bsz, s, d = x.shape
    nb = s // B_BLOCK
    npair = d // LANES
    nbp = km2.shape[2]
    n_sel = min(B_TOPK, nb - 1)
    vec = pl.BlockSpec((1, 1, d), lambda b, j: (b, 0, 0))
    full2 = lambda shape: pl.BlockSpec(shape, lambda b, j: (0, 0))
    return pl.pallas_call(
        functools.partial(_attn_kernel, nb=nb, n_sel=n_sel),
        grid=(bsz, nb),
        in_specs=[pl.BlockSpec((1, B_BLOCK, d), lambda b, j: (b, j, 0)), vec, vec, vec,
                  full2((1, d)), full2((d, d)), full2((d, d)),
                  pl.BlockSpec((1, npair, nb, B_BLOCK, LANES), lambda b, j: (b, 0, 0, 0, 0)),
                  pl.BlockSpec((1, npair, nb, LANES, B_BLOCK), lambda b, j: (b, 0, 0, 0, 0)),
                  pl.BlockSpec((1, npair, nbp, LANES), lambda b, j: (b, 0, 0, 0))],
        out_specs=pl.BlockSpec((1, B_BLOCK, d), lambda b, j: (b, j, 0)),
        out_shape=jax.ShapeDtypeStruct((bsz, s, d), F32),
        scratch_shapes=[pltpu.VMEM((npair, LANES, B_BLOCK), F32),
                        pltpu.VMEM((npair, 2, LANES, B_BLOCK), BF16),
                        pltpu.VMEM((npair, LANES, B_BLOCK), F32),
                        pltpu.VMEM((npair, 2, nbp, B_BLOCK), F32),
                        pltpu.VMEM((npair, 2, 1, B_BLOCK), F32),
                        pltpu.VMEM((npair, 2, 1, B_BLOCK), F32)],
        compiler_params=_cparams(("arbitrary", "arbitrary")),
    )(x, sh, sc, gt, nw.reshape(1, d), w_q.T.astype(BF16), w_o.astype(BF16), k5, vt5, km2)


def kernel(x, c, ada_w, ada_b, norm_mix, norm_ffn, a_w_in, a_b_in, a_ln_g, a_ln_b, a_w_s, a_b_s,
           a_w_out, kv_norm, kv_ada_w, kv_ada_b, kv_w_k, kv_w_v, b_w_q, b_w_o, moe_router, moe_bias,
           moe_w_gate, moe_w_up, moe_w_down, sh_w_gate, sh_w_up, sh_w_down, final_norm):
    bsz, s, d = x.shape
    depth = ada_w.shape[0]
    n_a = a_w_in.shape[0]
    assert s % B_BLOCK == 0 and s % TM_GMLP == 0 and d % LANES == 0
    nb = s // B_BLOCK
    npair = d // LANES
    nbp = -(-nb // 8) * 8

    def split(m, n):
        return [m[:, i * d:(i + 1) * d].reshape(bsz, 1, d) for i in range(n)]

    layer_mods = _ada(c, ada_w, ada_b)
    k5 = vt5 = km2 = None
    for i in range(depth):
        sh1, sc1, g1, sh2, sc2, g2 = split(layer_mods[i], 6)
        if i < n_a:
            x = _gmlp(x, sh1, sc1, g1, norm_mix[i], a_w_in[i], a_b_in[i], a_ln_g[i], a_ln_b[i],
                      a_w_s[i], a_b_s[i], a_w_out[i])
        else:
            if k5 is None:
                ksh, ksc = split(_ada(c, kv_ada_w[None], kv_ada_b[None])[0], 2)
                k5, vt5, km = _kv(x, ksh, ksc, kv_norm, kv_w_k, kv_w_v)
                km = km.reshape(bsz, nb, npair, LANES).transpose(0, 2, 1, 3)
                km2 = jnp.pad(km, ((0, 0), (0, 0), (0, nbp - nb), (0, 0)))
            j = i - n_a
            x = _attn(x, sh1, sc1, g1, norm_mix[i], b_w_q[j], b_w_o[j], k5, vt5, km2)
        x = _moe(x, sh2, sc2, g2, norm_ffn[i], moe_router[i], moe_bias[i], i, moe_w_gate,
                 moe_w_up, moe_w_down, sh_w_gate[i], sh_w_up[i], sh_w_down[i],
                 final_norm, i == depth - 1)
    return x
```

```python
import functools

import jax
import jax.numpy as jnp
from jax import lax
from jax.experimental import pallas as pl
from jax.experimental.pallas import tpu as pltpu

F32 = jnp.float32
BF16 = jnp.bfloat16
I32 = jnp.int32
U32 = jnp.uint32

RMS_EPS = 1e-6
LN_EPS = 1e-5
NEG_INF = -1e30

A_CHUNK = 128
A_GROUPS = 8
B_HEADS = 16
B_BLOCK = 256
B_TOPK = 3
N_EXPERTS = 64
TOP_K = 8
N_GROUPS = 8
TOPK_GROUPS = 4
ROUTED_SCALE = 2.5

LANES = 128
VMEM_LIMIT = 56 * 1024 * 1024

TM_GMLP = 256
TM_ROUTER = 256
TM_DEST = 2048
TM_DISPATCH = 256
TM_COMBINE = 128
BM_EXPERT = 256
ATTN_PAIR_UNROLL = 4


def _cparams(sem):
    return pltpu.CompilerParams(dimension_semantics=sem, vmem_limit_bytes=VMEM_LIMIT)


def _sigmoid(x):
    return 1.0 / (1.0 + jnp.exp(-x))


def _silu(x):
    return x * _sigmoid(x)


def _gelu_tanh(x):
    return 0.5 * x * (1.0 + jnp.tanh(0.7978845608028654 * (x + 0.044715 * (x * x * x))))


def _rms(x, g):
    return x * lax.rsqrt(jnp.mean(x * x, axis=-1, keepdims=True) + RMS_EPS) * g


def _bdot(a, b):
    return jnp.dot(a, b, preferred_element_type=F32)


def _bdot_nt(a, b):
    return lax.dot_general(a, b, (((1,), (1,)), ((), ())), preferred_element_type=F32)


def _split(a):
    hi = a.astype(BF16)
    lo = (a - hi.astype(F32)).astype(BF16)
    return hi, lo


def _dot3_nt(a, b):
    ah, al = _split(a)
    bh, bl = _split(b)
    return _bdot_nt(ah, bh) + (_bdot_nt(ah, bl) + _bdot_nt(al, bh))


def _pack_pairs(x):
    bits = lax.bitcast_convert_type(x, U32)
    r = (bits + (jnp.uint32(0x7FFF) + ((bits >> 16) & jnp.uint32(1)))) >> 16
    n = x.shape[1] // 2
    return (r[:, :n] << 16) | r[:, n:]


def _unpack_pairs(u):
    hi = lax.bitcast_convert_type(u & jnp.uint32(0xFFFF0000), F32)
    lo = lax.bitcast_convert_type(u << 16, F32)
    return hi, lo


def _dot3(a, b):
    ah, al = _split(a)
    bh, bl = _split(b)
    return _bdot(ah, bh) + (_bdot(ah, bl) + _bdot(al, bh))


def _ada_kernel(c_ref, w_ref, b_ref, o_ref):
    a = _silu(c_ref[...]).astype(BF16)
    o_ref[0] = _bdot(a, w_ref[0].astype(BF16)) + b_ref[0]


def _ada(c, w, b):
    bsz, d = c.shape
    nl, _, n = w.shape
    tn = 1024
    return pl.pallas_call(
        _ada_kernel,
        grid=(nl, n // tn),
        in_specs=[pl.BlockSpec((bsz, d), lambda l, j: (0, 0)),
                  pl.BlockSpec((1, d, tn), lambda l, j: (l, 0, j)),
                  pl.BlockSpec((1, 1, tn), lambda l, j: (l, 0, j))],
        out_specs=pl.BlockSpec((1, bsz, tn), lambda l, j: (l, 0, j)),
        out_shape=jax.ShapeDtypeStruct((nl, bsz, n), F32),
        compiler_params=_cparams(("arbitrary", "arbitrary")),
    )(c, w, b.reshape(nl, 1, n))


def _gmlp_kernel(x_ref, sh_ref, sc_ref, g_ref, nw_ref, win_ref, bin_ref, lng_ref, lnb_ref,
                 ws_ref, bst_ref, wout_ref, o_ref, y_sc):
    x = x_ref[0]
    tm = x.shape[0]
    h = _rms(x, nw_ref[...]) * (1.0 + sc_ref[0]) + sh_ref[0]
    z = _gelu_tanh(_bdot(h.astype(BF16), win_ref[...]) + bin_ref[...])
    aw = z.shape[1] // 2
    gd = aw // A_GROUPS
    u = z[:, :aw]
    v = z[:, aw:]
    mu = jnp.mean(v, axis=-1, keepdims=True)
    dv = v - mu
    var = jnp.mean(dv * dv, axis=-1, keepdims=True)
    vn = (dv * lax.rsqrt(var + LN_EPS) * lng_ref[...] + lnb_ref[...]).astype(BF16)
    row = lax.broadcasted_iota(I32, (A_CHUNK, A_CHUNK), 0)
    col = lax.broadcasted_iota(I32, (A_CHUNK, A_CHUNK), 1)
    causal = col <= row
    for g in range(A_GROUPS):
        wg = jnp.where(causal, ws_ref[g], 0.0).astype(BF16)
        bcol = bst_ref[:, g:g + 1]
        for ci in range(tm // A_CHUNK):
            rs = slice(ci * A_CHUNK, (ci + 1) * A_CHUNK)
            cs = slice(g * gd, (g + 1) * gd)
            sv = _bdot(wg, vn[rs, cs]) + bcol
            y_sc[rs, cs] = (u[rs, cs] * sv).astype(BF16)
    o_ref[0] = x + g_ref[0] * _bdot(y_sc[...], wout_ref[...])


def _gmlp(x, sh, sc, gt, nw, w_in, b_in, ln_g, ln_b, w_s, b_s, w_out):
    bsz, s, d = x.shape
    tm = TM_GMLP
    n_in = w_in.shape[1]
    aw = n_in // 2
    vec = pl.BlockSpec((1, 1, d), lambda b, j: (b, 0, 0))
    full2 = lambda shape: pl.BlockSpec(shape, lambda b, j: (0, 0))
    return pl.pallas_call(
        _gmlp_kernel,
        grid=(bsz, s // tm),
        in_specs=[pl.BlockSpec((1, tm, d), lambda b, j: (b, j, 0)), vec, vec, vec,
                  full2((1, d)), full2((d, n_in)), full2((1, n_in)), full2((1, aw)), full2((1, aw)),
                  pl.BlockSpec((A_GROUPS, A_CHUNK, A_CHUNK), lambda b, j: (0, 0, 0)),
                  full2((A_CHUNK, A_GROUPS)), full2((aw, d))],
        out_specs=pl.BlockSpec((1, tm, d), lambda b, j: (b, j, 0)),
        out_shape=jax.ShapeDtypeStruct((bsz, s, d), F32),
        scratch_shapes=[pltpu.VMEM((tm, aw), BF16)],
        compiler_params=_cparams(("arbitrary", "arbitrary")),
    )(x, sh, sc, gt, nw.reshape(1, d), w_in.astype(BF16), b_in.reshape(1, n_in),
      ln_g.reshape(1, aw), ln_b.reshape(1, aw), w_s, b_s.T, w_out.astype(BF16))


def _router_kernel(x_ref, sh_ref, sc_ref, g_ref, nw_ref, wrt_ref, bias_ref, wsg_ref, wsu_ref, wsd_ref,
                   h_ref, base_ref, eidx_ref, rank_ref, w_ref, cnt_ref, carry_sc):
    first = jnp.logical_and(pl.program_id(0) == 0, pl.program_id(1) == 0)

    @pl.when(first)
    def _():
        carry_sc[...] = jnp.zeros_like(carry_sc)

    x = x_ref[0]
    tm = x.shape[0]
    h = _rms(x, nw_ref[...]) * (1.0 + sc_ref[0]) + sh_ref[0]
    h_ref[...] = _pack_pairs(h)
    hb = h.astype(BF16)
    act = (_silu(_bdot(hb, wsg_ref[...])) * _bdot(hb, wsu_ref[...])).astype(BF16)
    base_ref[...] = x + g_ref[0] * _bdot(act, wsd_ref[...])

    scores = _sigmoid(_dot3_nt(wrt_ref[...], h))
    choice = scores + bias_ref[...]
    gsz = N_EXPERTS // N_GROUPS
    sub = lax.broadcasted_iota(I32, (gsz, tm), 0)
    blocks = [choice[g * gsz:(g + 1) * gsz] for g in range(N_GROUPS)]
    gscore = []
    for blk in blocks:
        m1 = jnp.max(blk, axis=0, keepdims=True)
        i1 = jnp.min(jnp.where(blk == m1, sub, gsz), axis=0, keepdims=True)
        m2 = jnp.max(jnp.where(sub == i1, -jnp.inf, blk), axis=0, keepdims=True)
        gscore.append(m1 + m2)
    masked = []
    for g in range(N_GROUPS):
        beats = jnp.zeros((1, tm), F32)
        for m in range(N_GROUPS):
            if m == g:
                continue
            b = gscore[m] > gscore[g]
            if m < g:
                b = jnp.logical_or(b, gscore[m] == gscore[g])
            beats = beats + b.astype(F32)
        masked.append(jnp.where(beats < TOPK_GROUPS, blocks[g], NEG_INF))
    cur = jnp.concatenate(masked, axis=0)
    eio = lax.broadcasted_iota(I32, (N_EXPERTS, tm), 0)
    sels, eids, ws = [], [], []
    for _ in range(TOP_K):
        m = jnp.max(cur, axis=0, keepdims=True)
        idx = jnp.min(jnp.where(cur == m, eio, N_EXPERTS), axis=0, keepdims=True)
        sel = eio == idx
        sels.append(sel)
        eids.append(idx)
        ws.append(jnp.sum(jnp.where(sel, scores, 0.0), axis=0, keepdims=True))
        cur = jnp.where(sel, -jnp.inf, cur)
    wsum = ws[0]
    for k in range(1, TOP_K):
        wsum = wsum + ws[k]
    w_ref[...] = jnp.concatenate(ws, axis=0) / wsum * ROUTED_SCALE
    eidx_ref[...] = jnp.concatenate(eids, axis=0)

    onehot = jnp.zeros((N_EXPERTS, tm), F32)
    for sel in sels:
        onehot = onehot + sel.astype(F32)
    r_i = lax.broadcasted_iota(I32, (tm, tm), 0)
    c_i = lax.broadcasted_iota(I32, (tm, tm), 1)
    before = (r_i < c_i).astype(BF16)
    prior = _bdot(onehot.astype(BF16), before) + carry_sc[...]
    ranks = [jnp.sum(jnp.where(sel, prior, 0.0), axis=0, keepdims=True) for sel in sels]
    rank_ref[...] = jnp.concatenate(ranks, axis=0).astype(I32)
    total = carry_sc[...] + jnp.sum(onehot, axis=1, keepdims=True)
    carry_sc[...] = total
    cnt_ref[...] = total


def _router(x, sh, sc, gt, nw, w_router, e_bias, wsg, wsu, wsd):
    bsz, s, d = x.shape
    t = bsz * s
    tm = TM_ROUTER
    nt = s // tm
    sd = wsg.shape[1]
    vec = pl.BlockSpec((1, 1, d), lambda b, j: (b, 0, 0))
    full2 = lambda shape: pl.BlockSpec(shape, lambda b, j: (0, 0))
    tok = pl.BlockSpec((tm, d), lambda b, j: (b * nt + j, 0))
    tokp = pl.BlockSpec((tm, d // 2), lambda b, j: (b * nt + j, 0))
    slot = pl.BlockSpec((TOP_K, tm), lambda b, j: (0, b * nt + j))
    return pl.pallas_call(
        _router_kernel,
        grid=(bsz, nt),
        in_specs=[pl.BlockSpec((1, tm, d), lambda b, j: (b, j, 0)), vec, vec, vec,
                  full2((1, d)), full2((N_EXPERTS, d)), full2((N_EXPERTS, 1)),
                  full2((d, sd)), full2((d, sd)), full2((sd, d))],
        out_specs=[tokp, tok, slot, slot, slot, full2((N_EXPERTS, 1))],
        out_shape=[jax.ShapeDtypeStruct((t, d // 2), U32), jax.ShapeDtypeStruct((t, d), F32),
                   jax.ShapeDtypeStruct((TOP_K, t), I32), jax.ShapeDtypeStruct((TOP_K, t), I32),
                   jax.ShapeDtypeStruct((TOP_K, t), F32), jax.ShapeDtypeStruct((N_EXPERTS, 1), F32)],
        scratch_shapes=[pltpu.VMEM((N_EXPERTS, 1), F32)],
        compiler_params=_cparams(("arbitrary", "arbitrary")),
    )(x, sh, sc, gt, nw.reshape(1, d), w_router.T, e_bias.reshape(N_EXPERTS, 1),
      wsg.astype(BF16), wsu.astype(BF16), wsd.astype(BF16))


def _dest_kernel(starts_ref, e_ref, r_ref, o_ref):
    e = e_ref[...]
    acc = r_ref[...]
    for j in range(N_EXPERTS):
        acc = acc + jnp.where(e == j, starts_ref[j], 0)
    o_ref[...] = acc


def _dest(starts, eidx, rank):
    t = eidx.shape[1]
    tm = min(TM_DEST, t)
    blk = pl.BlockSpec((TOP_K, tm), lambda i, s: (0, i))
    return pl.pallas_call(
        _dest_kernel,
        grid_spec=pltpu.PrefetchScalarGridSpec(
            num_scalar_prefetch=1, grid=(t // tm,), in_specs=[blk, blk], out_specs=blk),
        out_shape=jax.ShapeDtypeStruct((TOP_K, t), I32),
        compiler_params=_cparams(("arbitrary",)),
    )(starts, eidx, rank)


def _dispatch_kernel(dest_ref, h_ref, xs_ref, sem):
    tm = dest_ref.shape[1]

    def issue(t, carry):
        for k in range(TOP_K):
            pltpu.make_async_copy(h_ref.at[pl.ds(t, 1)],
                                  xs_ref.at[pl.ds(dest_ref[k, t], 1)], sem).start()
        return carry

    lax.fori_loop(0, tm, issue, 0)

    def drain(t, carry):
        for k in range(TOP_K):
            pltpu.make_async_copy(h_ref.at[pl.ds(0, 1)], xs_ref.at[pl.ds(0, 1)], sem).wait()
        return carry

    lax.fori_loop(0, tm, drain, 0)


def _dispatch(dest, h):
    t, d = h.shape
    tm = TM_DISPATCH
    return pl.pallas_call(
        _dispatch_kernel,
        grid=(t // tm,),
        in_specs=[pl.BlockSpec((TOP_K, tm), lambda i: (0, i), memory_space=pltpu.SMEM),
                  pl.BlockSpec((tm, d), lambda i: (i, 0))],
        out_specs=pl.BlockSpec(memory_space=pl.ANY),
        out_shape=jax.ShapeDtypeStruct((t * TOP_K, d), U32),
        scratch_shapes=[pltpu.SemaphoreType.DMA(())],
        compiler_params=_cparams(("arbitrary",)),
    )(dest, h)


def _expert_kernel(vb_ref, ve_ref, vlo_ref, vhi_ref, vfirst_ref, vnew_ref,
                   x_ref, wg_ref, wu_ref, wd_ref, o_ref, wg_sc, wu_sc, wd_sc):
    v = pl.program_id(0)

    @pl.when(vnew_ref[v] == 1)
    def _():
        wg_sc[...] = wg_ref[0, 0].astype(BF16)
        wu_sc[...] = wu_ref[0, 0].astype(BF16)
        wd_sc[...] = wd_ref[0, 0].astype(BF16)

    x_hi, x_lo = _unpack_pairs(x_ref[...])
    xb = jnp.concatenate([x_hi.astype(BF16), x_lo.astype(BF16)], axis=1)
    act = (_silu(_bdot(xb, wg_sc[...])) * _bdot(xb, wu_sc[...])).astype(BF16)
    y = _bdot(act, wd_sc[...])
    rows = lax.broadcasted_iota(I32, (y.shape[0], 1), 0)
    y = jnp.where(jnp.logical_and(rows >= vlo_ref[v], rows < vhi_ref[v]), y, 0.0)
    yp = _pack_pairs(y)

    @pl.when(vfirst_ref[v] == 1)
    def _():
        o_ref[...] = yp

    @pl.when(vfirst_ref[v] == 0)
    def _():
        o_ref[...] = o_ref[...] | yp


def _experts(meta, xs, layer, w_gate, w_up, w_down):
    r, dp = xs.shape
    d = 2 * dp
    ed = w_gate.shape[-1]
    bm = BM_EXPERT
    nv = meta[0].shape[0]
    xmap = lambda v, vb, ve, vlo, vhi, vf, vn: (vb[v], 0)
    wmap = lambda v, vb, ve, vlo, vhi, vf, vn: (layer, ve[v], 0, 0)
    return pl.pallas_call(
        _expert_kernel,
        grid_spec=pltpu.PrefetchScalarGridSpec(
            num_scalar_prefetch=6, grid=(nv,),
            in_specs=[pl.BlockSpec((bm, dp), xmap), pl.BlockSpec((1, 1, d, ed), wmap),
                      pl.BlockSpec((1, 1, d, ed), wmap), pl.BlockSpec((1, 1, ed, d), wmap)],
            out_specs=pl.BlockSpec((bm, dp), xmap),
            scratch_shapes=[pltpu.VMEM((d, ed), BF16), pltpu.VMEM((d, ed), BF16),
                            pltpu.VMEM((ed, d), BF16)]),
        out_shape=jax.ShapeDtypeStruct((r, dp), U32),
        compiler_params=_cparams(("arbitrary",)),
    )(*meta, xs, w_gate, w_up, w_down)


def _visit_meta(counts, n_rows):
    bm = BM_EXPERT
    nb = n_rows // bm
    nv = nb + N_EXPERTS - 1
    ends = jnp.cumsum(counts)
    starts = ends - counts
    blo = jnp.arange(nb, dtype=I32) * bm
    first_e = jnp.sum((ends[None, :] <= blo[:, None]).astype(I32), axis=1)
    last_e = jnp.sum((starts[None, :] < (blo + bm)[:, None]).astype(I32), axis=1) - 1
    per = last_e - first_e + 1
    off_incl = jnp.cumsum(per)
    off = off_incl - per
    total = off_incl[-1]
    vi = jnp.arange(nv, dtype=I32)
    real = vi < total
    vb = jnp.minimum(jnp.sum((off_incl[None, :] <= vi[:, None]).astype(I32), axis=1), nb - 1)
    ve_real = first_e[vb] + (vi - off[vb])
    ve_last = last_e[nb - 1]
    ve = jnp.where(real, ve_real, ve_last).astype(I32)
    lo = jnp.clip(starts[ve] - vb * bm, 0, bm)
    hi = jnp.clip(ends[ve] - vb * bm, 0, bm)
    vlo = jnp.where(real, lo, 0).astype(I32)
    vhi = jnp.where(real, hi, 0).astype(I32)
    vfirst = jnp.logical_and(real, vi == off[vb]).astype(I32)
    prev = jnp.concatenate([jnp.full((1,), -1, I32), ve[:-1]])
    vnew = (ve != prev).astype(I32)
    return starts.astype(I32), (vb, ve, vlo, vhi, vfirst, vnew)


def _combine_kernel(dest_ref, ys_ref, w_ref, base_ref, g_ref, fn_ref, o_ref, buf, sem, *, final):
    tm = base_ref.shape[0]

    def issue(t, carry):
        for k in range(TOP_K):
            pltpu.make_async_copy(ys_ref.at[pl.ds(dest_ref[k, t], 1)],
                                  buf.at[k, pl.ds(t, 1)], sem).start()
        return carry

    lax.fori_loop(0, tm, issue, 0)

    def drain(t, carry):
        for k in range(TOP_K):
            pltpu.make_async_copy(ys_ref.at[pl.ds(0, 1)], buf.at[0, pl.ds(0, 1)], sem).wait()
        return carry

    lax.fori_loop(0, tm, drain, 0)

    w = w_ref[...]
    acc_hi = acc_lo = None
    for k in range(TOP_K):
        y_hi, y_lo = _unpack_pairs(buf[k])
        wk = w[:, k:k + 1]
        acc_hi = wk * y_hi if k == 0 else acc_hi + wk * y_hi
        acc_lo = wk * y_lo if k == 0 else acc_lo + wk * y_lo
    out = base_ref[...] + g_ref[0] * jnp.concatenate([acc_hi, acc_lo], axis=1)
    if final:
        out = _rms(out, fn_ref[...])
    o_ref[...] = out


def _combine(dest, ys, w_tok, base, gt, fnorm, seq, final):
    t, d = base.shape
    tm = TM_COMBINE
    nt = seq // tm
    return pl.pallas_call(
        functools.partial(_combine_kernel, final=final),
        grid=(t // tm,),
        in_specs=[pl.BlockSpec((TOP_K, tm), lambda i: (0, i), memory_space=pltpu.SMEM),
                  pl.BlockSpec(memory_space=pl.ANY),
                  pl.BlockSpec((tm, TOP_K), lambda i: (i, 0)),
                  pl.BlockSpec((tm, d), lambda i: (i, 0)),
                  pl.BlockSpec((1, 1, d), lambda i: (i // nt, 0, 0)),
                  pl.BlockSpec((1, d), lambda i: (0, 0))],
        out_specs=pl.BlockSpec((tm, d), lambda i: (i, 0)),
        out_shape=jax.ShapeDtypeStruct((t, d), F32),
        scratch_shapes=[pltpu.VMEM((TOP_K, tm, d // 2), U32), pltpu.SemaphoreType.DMA(())],
        compiler_params=_cparams(("arbitrary",)),
    )(dest, ys, w_tok, base, gt, fnorm.reshape(1, d))


def _moe(x, sh, sc, gt, nw, w_router, e_bias, layer, w_gate, w_up, w_down, wsg, wsu, wsd, fnorm, final):
    bsz, s, d = x.shape
    t = bsz * s
    h, base, eidx, rank, w, cnt = _router(x, sh, sc, gt, nw, w_router, e_bias, wsg, wsu, wsd)
    counts = cnt.reshape(N_EXPERTS).astype(I32)
    starts, meta = _visit_meta(counts, t * TOP_K)
    dest = _dest(starts, eidx, rank)
    xs = _dispatch(dest, h)
    ys = _experts(meta, xs, layer, w_gate, w_up, w_down)
    out = _combine(dest, ys, w.T, base, gt, fnorm, s, final)
    return out.reshape(bsz, s, d)


def _kv_kernel(x_ref, sh_ref, sc_ref, nw_ref, wk_ref, wvt_ref, k_ref, vt_ref, km_ref):
    x = x_ref[0]
    hb = (_rms(x, nw_ref[...]) * (1.0 + sc_ref[0]) + sh_ref[0]).astype(BF16)
    k = _bdot(hb, wk_ref[...])
    vt = _bdot_nt(wvt_ref[...], hb)
    for p in range(k.shape[1] // LANES):
        k_ref[0, p, 0] = k[:, p * LANES:(p + 1) * LANES].astype(BF16)
        vt_ref[0, p, 0] = vt[p * LANES:(p + 1) * LANES, :].astype(BF16)
    km_ref[0, 0] = jnp.mean(k, axis=0, keepdims=True)


def _kv(x, sh, sc, nw, w_k, w_v):
    bsz, s, d = x.shape
    nb = s // B_BLOCK
    npair = d // LANES
    vec = pl.BlockSpec((1, 1, d), lambda b, j: (b, 0, 0))
    full2 = lambda shape: pl.BlockSpec(shape, lambda b, j: (0, 0))
    return pl.pallas_call(
        _kv_kernel,
        grid=(bsz, nb),
        in_specs=[pl.BlockSpec((1, B_BLOCK, d), lambda b, j: (b, j, 0)), vec, vec,
                  full2((1, d)), full2((d, d)), full2((d, d))],
        out_specs=[pl.BlockSpec((1, npair, 1, B_BLOCK, LANES), lambda b, j: (b, 0, j, 0, 0)),
                   pl.BlockSpec((1, npair, 1, LANES, B_BLOCK), lambda b, j: (b, 0, j, 0, 0)),
                   pl.BlockSpec((1, 1, 1, d), lambda b, j: (b, j, 0, 0))],
        out_shape=[jax.ShapeDtypeStruct((bsz, npair, nb, B_BLOCK, LANES), BF16),
                   jax.ShapeDtypeStruct((bsz, npair, nb, LANES, B_BLOCK), BF16),
                   jax.ShapeDtypeStruct((bsz, nb, 1, d), F32)],
        compiler_params=_cparams(("arbitrary", "arbitrary")),
    )(x, sh, sc, nw.reshape(1, d), w_k.astype(BF16), w_v.T.astype(BF16))


def _attn_kernel(x_ref, sh_ref, sc_ref, g_ref, nw_ref, wqt_ref, wo_ref, k_ref, vt_ref, km_ref,
                 o_ref, qt_sc, qs_sc, acc_sc, sel_sc, m_sc, l_sc, *, nb, n_sel):
    qb = pl.program_id(1)
    x = x_ref[0]
    bq = x.shape[0]
    npair = qt_sc.shape[0]
    nbp = km_ref.shape[2]
    hd = LANES // 2
    scale = float(hd) ** -0.5
    h = _rms(x, nw_ref[...]) * (1.0 + sc_ref[0]) + sh_ref[0]
    qt = _bdot_nt(wqt_ref[...], h.astype(BF16))
    for p in range(npair):
        qt_sc[p] = qt[p * LANES:(p + 1) * LANES, :]

    subn = lax.broadcasted_iota(I32, (nbp, bq), 0)
    past = subn < qb
    krow = lax.broadcasted_iota(I32, (B_BLOCK, bq), 0)
    qcol = lax.broadcasted_iota(I32, (B_BLOCK, bq), 1)
    causal = krow <= qcol
    rowh = lax.broadcasted_iota(I32, (LANES, 1), 0)

    grp = ATTN_PAIR_UNROLL
    heads = [(u, e) for u in range(grp) for e in range(2)]

    def own_body(gi, carry):
        ps = [gi * grp + u for u in range(grp)]
        q2ts = [qt_sc[p] for p in ps]
        kms = [km_ref[0, p] for p in ps]
        kown = [k_ref[0, p, qb] for p in ps]
        vown = [vt_ref[0, p, qb] for p in ps]
        qets = [jnp.where((rowh >= hd) if e == 1 else (rowh < hd), q2ts[u], 0.0) for u, e in heads]
        qsts = [(q * scale).astype(BF16) for q in qets]
        ss = [jnp.where(causal, _bdot(kown[u], qsts[i]), NEG_INF) for i, (u, e) in enumerate(heads)]
        gates = [_dot3(kms[u], qets[i]) for i, (u, e) in enumerate(heads)]
        ms = [jnp.max(s, axis=0, keepdims=True) for s in ss]
        pes = [jnp.exp(s - m) for s, m in zip(ss, ms)]
        ls = [jnp.sum(pe, axis=0, keepdims=True) for pe in pes]
        accs = [_bdot(vown[u][e * hd:(e + 1) * hd, :], pes[i].astype(BF16))
                for i, (u, e) in enumerate(heads)]
        sels = []
        for gate in gates:
            selt = jnp.zeros((nbp, bq), F32)
            for n in range(nb):
                gn = gate[n:n + 1, :]
                beats = jnp.logical_or(gate > gn, jnp.logical_and(gate == gn, subn < n))
                beats = jnp.logical_and(beats, past)
                cnt = jnp.sum(beats.astype(F32), axis=0, keepdims=True)
                selt = jnp.where(subn == n, (cnt < n_sel).astype(F32), selt)
            sels.append(selt)
        for i, (u, e) in enumerate(heads):
            p = ps[u]
            sel_sc[p, e] = sels[i]
            qs_sc[p, e] = qsts[i]
            m_sc[p, e] = ms[i]
            l_sc[p, e] = ls[i]
            acc_sc[p, e * hd:(e + 1) * hd, :] = accs[i]
        return carry

    lax.fori_loop(0, npair // grp, own_body, 0)

    def kb_body(kb, carry):
        def group_body(gi, c2):
            ps = [gi * grp + u for u in range(grp)]
            kbl = [k_ref[0, p, kb] for p in ps]
            vbl = [vt_ref[0, p, kb] for p in ps]
            qsts = [qs_sc[ps[u], e] for u, e in heads]
            rows = [sel_sc[ps[u], e, pl.ds(kb, 1), :] for u, e in heads]
            m_old = [m_sc[ps[u], e] for u, e in heads]
            l_old = [l_sc[ps[u], e] for u, e in heads]
            a_old = [acc_sc[ps[u], e * hd:(e + 1) * hd, :] for u, e in heads]
            ss = [jnp.where(rows[i] > 0.5, _bdot(kbl[u], qsts[i]), NEG_INF)
                  for i, (u, e) in enumerate(heads)]
            m_new = [jnp.maximum(m, jnp.max(s, axis=0, keepdims=True)) for m, s in zip(m_old, ss)]
            alphas = [jnp.exp(m - mn) for m, mn in zip(m_old, m_new)]
            pes = [jnp.exp(s - mn) for s, mn in zip(ss, m_new)]
            l_new = [a * l + jnp.sum(pe, axis=0, keepdims=True) for a, l, pe in zip(alphas, l_old, pes)]
            a_new = [alphas[i] * a_old[i] + _bdot(vbl[u][e * hd:(e + 1) * hd, :], pes[i].astype(BF16))
                     for i, (u, e) in enumerate(heads)]
            for i, (u, e) in enumerate(heads):
                p = ps[u]
                m_sc[p, e] = m_new[i]
                l_sc[p, e] = l_new[i]
                acc_sc[p, e * hd:(e + 1) * hd, :] = a_new[i]
            return c2

        lax.fori_loop(0, npair // grp, group_body, 0)
        return carry

    lax.fori_loop(0, qb, kb_body, 0)

    parts = []
    for p in range(npair):
        for e in range(2):
            parts.append(acc_sc[p, e * hd:(e + 1) * hd, :] / l_sc[p, e])
    ot = jnp.concatenate(parts, axis=0)
    o_ref[0] = x + g_ref[0] * _bdot(ot.T.astype(BF16), wo_ref[...])


def _attn(x, sh, sc, gt, nw, w_q, w_o, k5, vt5, km2):
    bsz, s, d = x.shape
    nb = s // B_BLOCK
    npair = d // LANES
    nbp = km2.shape[2]
    n_sel = min(B_TOPK, nb - 1)
    vec = pl.BlockSpec((1, 1, d), lambda b, j: (b, 0, 0))
    full2 = lambda shape: pl.BlockSpec(shape, lambda b, j: (0, 0))
    return pl.pallas_call(
        functools.partial(_attn_kernel, nb=nb, n_sel=n_sel),
        grid=(bsz, nb),
        in_specs=[pl.BlockSpec((1, B_BLOCK, d), lambda b, j: (b, j, 0)), vec, vec, vec,
                  full2((1, d)), full2((d, d)), full2((d, d)),
                  pl.BlockSpec((1, npair, nb, B_BLOCK, LANES), lambda b, j: (b, 0, 0, 0, 0)),
                  pl.BlockSpec((1, npair, nb, LANES, B_BLOCK), lambda b, j: (b, 0, 0, 0, 0)),
                  pl.BlockSpec((1, npair, nbp, LANES), lambda b, j: (b, 0, 0, 0))],
        out_specs=pl.BlockSpec((1, B_BLOCK, d), lambda b, j: (b, j, 0)),
        out_shape=jax.ShapeDtypeStruct((bsz, s, d), F32),
        scratch_shapes=[pltpu.VMEM((npair, LANES, B_BLOCK), F32),
                        pltpu.VMEM((npair, 2, LANES, B_BLOCK), BF16),
                        pltpu.VMEM((npair, LANES, B_BLOCK), F32),
                        pltpu.VMEM((npair, 2, nbp, B_BLOCK), F32),
                        pltpu.VMEM((npair, 2, 1, B_BLOCK), F32),
                        pltpu.VMEM((npair, 2, 1, B_BLOCK), F32)],
        compiler_params=_cparams(("arbitrary", "arbitrary")),
    )(x, sh, sc, gt, nw.reshape(1, d), w_q.T.astype(BF16), w_o.astype(BF16), k5, vt5, km2)


def kernel(x, c, ada_w, ada_b, norm_mix, norm_ffn, a_w_in, a_b_in, a_ln_g, a_ln_b, a_w_s, a_b_s,
           a_w_out, kv_norm, kv_ada_w, kv_ada_b, kv_w_k, kv_w_v, b_w_q, b_w_o, moe_router, moe_bias,
           moe_w_gate, moe_w_up, moe_w_down, sh_w_gate, sh_w_up, sh_w_down, final_norm):
    bsz, s, d = x.shape
    depth = ada_w.shape[0]
    n_a = a_w_in.shape[0]
    assert s % B_BLOCK == 0 and s % TM_GMLP == 0 and d % LANES == 0
    nb = s // B_BLOCK
    npair = d // LANES
    nbp = -(-nb // 8) * 8

    def split(m, n):
        return [m[:, i * d:(i + 1) * d].reshape(bsz, 1, d) for i in range(n)]

    layer_mods = _ada(c, ada_w, ada_b)
    k5 = vt5 = km2 = None
    for i in range(depth):
        sh1, sc1, g1, sh2, sc2, g2 = split(layer_mods[i], 6)
        if i < n_a:
            x = _gmlp(x, sh1, sc1, g1, norm_mix[i], a_w_in[i], a_b_in[i], a_ln_g[i], a_ln_b[i],
                      a_w_s[i], a_b_s[i], a_w_out[i])
        else:
            if k5 is None:
                ksh, ksc = split(_ada(c, kv_ada_w[None], kv_ada_b[None])[0], 2)
                k5, vt5, km = _kv(x, ksh, ksc, kv_norm, kv_w_k, kv_w_v)
                km = km.reshape(bsz, nb, npair, LANES).transpose(0, 2, 1, 3)
                km2 = jnp.pad(km, ((0, 0), (0, 0), (0, nbp - nb), (0, 0)))
            j = i - n_a
            x = _attn(x, sh1, sc1, g1, norm_mix[i], b_w_q[j], b_w_o[j], k5, vt5, km2)
        x = _moe(x, sh2, sc2, g2, norm_ffn[i], moe_router[i], moe_bias[i], i, moe_w_gate,
                 moe_w_up, moe_w_down, sh_w_gate[i], sh_w_up[i], sh_w_down[i],
                 final_norm, i == depth - 1)
    return x
```

```python
import functools

import jax
import jax.numpy as jnp
from jax import lax
from jax.experimental import pallas as pl
from jax.experimental.pallas import tpu as pltpu

F32 = jnp.float32
BF16 = jnp.bfloat16
I32 = jnp.int32

RMS_EPS = 1e-6
LN_EPS = 1e-5
NEG_INF = -1e30

A_CHUNK = 128
A_GROUPS = 8
B_HEADS = 16
B_BLOCK = 256
B_TOPK = 3
N_EXPERTS = 64
TOP_K = 8
N_GROUPS = 8
TOPK_GROUPS = 4
ROUTED_SCALE = 2.5

LANES = 128
VMEM_LIMIT = 56 * 1024 * 1024

TM_GMLP = 256
TM_SORT = 512
RUN_ALIGN = 16
SORT_CHUNK = 256
ROWS_TILE_USED = TM_SORT * TOP_K + N_EXPERTS * (RUN_ALIGN - 1)
ROWS_TILE = -(-ROWS_TILE_USED // SORT_CHUNK) * SORT_CHUNK
PIECES_TILE = ROWS_TILE // RUN_ALIGN
BM_EXPERT = 256
ATTN_PAIR_UNROLL = 4


def _cparams(sem):
    return pltpu.CompilerParams(dimension_semantics=sem, vmem_limit_bytes=VMEM_LIMIT)


def _sigmoid(x):
    return 1.0 / (1.0 + jnp.exp(-x))


def _silu(x):
    return x * _sigmoid(x)


def _gelu_tanh(x):
    return 0.5 * x * (1.0 + jnp.tanh(0.7978845608028654 * (x + 0.044715 * (x * x * x))))


def _rms(x, g):
    return x * lax.rsqrt(jnp.mean(x * x, axis=-1, keepdims=True) + RMS_EPS) * g


def _bdot(a, b):
    return jnp.dot(a, b, preferred_element_type=F32)


def _bdot_nt(a, b):
    return lax.dot_general(a, b, (((1,), (1,)), ((), ())), preferred_element_type=F32)


def _split(a):
    hi = a.astype(BF16)
    lo = (a - hi.astype(F32)).astype(BF16)
    return hi, lo


def _dot3_nt(a, b):
    ah, al = _split(a)
    bh, bl = _split(b)
    return _bdot_nt(ah, bh) + (_bdot_nt(ah, bl) + _bdot_nt(al, bh))


def _dot3(a, b):
    ah, al = _split(a)
    bh, bl = _split(b)
    return _bdot(ah, bh) + (_bdot(ah, bl) + _bdot(al, bh))


def _ada_kernel(c_ref, w_ref, b_ref, o_ref):
    a = _silu(c_ref[...]).astype(BF16)
    o_ref[0] = _bdot(a, w_ref[0].astype(BF16)) + b_ref[0]


def _ada(c, w, b):
    bsz, d = c.shape
    nl, _, n = w.shape
    tn = 1024
    return pl.pallas_call(
        _ada_kernel,
        grid=(nl, n // tn),
        in_specs=[pl.BlockSpec((bsz, d), lambda l, j: (0, 0)),
                  pl.BlockSpec((1, d, tn), lambda l, j: (l, 0, j)),
                  pl.BlockSpec((1, 1, tn), lambda l, j: (l, 0, j))],
        out_specs=pl.BlockSpec((1, bsz, tn), lambda l, j: (l, 0, j)),
        out_shape=jax.ShapeDtypeStruct((nl, bsz, n), F32),
        compiler_params=_cparams(("arbitrary", "arbitrary")),
    )(c, w, b.reshape(nl, 1, n))


def _gmlp_kernel(x_ref, sh_ref, sc_ref, g_ref, nw_ref, win_ref, bin_ref, lng_ref, lnb_ref,
                 ws_ref, bst_ref, wout_ref, o_ref, y_sc):
    x = x_ref[0]
    tm = x.shape[0]
    h = _rms(x, nw_ref[...]) * (1.0 + sc_ref[0]) + sh_ref[0]
    z = _gelu_tanh(_bdot(h.astype(BF16), win_ref[...]) + bin_ref[...])
    aw = z.shape[1] // 2
    gd = aw // A_GROUPS
    u = z[:, :aw]
    v = z[:, aw:]
    mu = jnp.mean(v, axis=-1, keepdims=True)
    dv = v - mu
    var = jnp.mean(dv * dv, axis=-1, keepdims=True)
    vn = (dv * lax.rsqrt(var + LN_EPS) * lng_ref[...] + lnb_ref[...]).astype(BF16)
    row = lax.broadcasted_iota(I32, (A_CHUNK, A_CHUNK), 0)
    col = lax.broadcasted_iota(I32, (A_CHUNK, A_CHUNK), 1)
    causal = col <= row
    for g in range(A_GROUPS):
        wg = jnp.where(causal, ws_ref[g], 0.0).astype(BF16)
        bcol = bst_ref[:, g:g + 1]
        for ci in range(tm // A_CHUNK):
            rs = slice(ci * A_CHUNK, (ci + 1) * A_CHUNK)
            cs = slice(g * gd, (g + 1) * gd)
            sv = _bdot(wg, vn[rs, cs]) + bcol
            y_sc[rs, cs] = (u[rs, cs] * sv).astype(BF16)
    o_ref[0] = x + g_ref[0] * _bdot(y_sc[...], wout_ref[...])


def _gmlp(x, sh, sc, gt, nw, w_in, b_in, ln_g, ln_b, w_s, b_s, w_out):
    bsz, s, d = x.shape
    tm = TM_GMLP
    n_in = w_in.shape[1]
    aw = n_in // 2
    vec = pl.BlockSpec((1, 1, d), lambda b, j: (b, 0, 0))
    full2 = lambda shape: pl.BlockSpec(shape, lambda b, j: (0, 0))
    return pl.pallas_call(
        _gmlp_kernel,
        grid=(bsz, s // tm),
        in_specs=[pl.BlockSpec((1, tm, d), lambda b, j: (b, j, 0)), vec, vec, vec,
                  full2((1, d)), full2((d, n_in)), full2((1, n_in)), full2((1, aw)), full2((1, aw)),
                  pl.BlockSpec((A_GROUPS, A_CHUNK, A_CHUNK), lambda b, j: (0, 0, 0)),
                  full2((A_CHUNK, A_GROUPS)), full2((aw, d))],
        out_specs=pl.BlockSpec((1, tm, d), lambda b, j: (b, j, 0)),
        out_shape=jax.ShapeDtypeStruct((bsz, s, d), F32),
        scratch_shapes=[pltpu.VMEM((tm, aw), BF16)],
        compiler_params=_cparams(("arbitrary", "arbitrary")),
    )(x, sh, sc, gt, nw.reshape(1, d), w_in.astype(BF16), b_in.reshape(1, n_in),
      ln_g.reshape(1, aw), ln_b.reshape(1, aw), w_s, b_s.T, w_out.astype(BF16))


def _router_kernel(x_ref, sh_ref, sc_ref, g_ref, nw_ref, wrt_ref, bias_ref, wsg_ref, wsu_ref, wsd_ref,
                   h_ref, base_ref, key_ref, wkey_ref, cnt_ref):
    x = x_ref[0]
    tm = x.shape[0]
    h = _rms(x, nw_ref[...]) * (1.0 + sc_ref[0]) + sh_ref[0]
    hb = h.astype(BF16)
    h_ref[...] = hb
    act = (_silu(_bdot(hb, wsg_ref[...])) * _bdot(hb, wsu_ref[...])).astype(BF16)
    base_ref[...] = x + g_ref[0] * _bdot(act, wsd_ref[...])

    scores = _sigmoid(_dot3_nt(wrt_ref[...], h))
    choice = scores + bias_ref[...]
    gsz = N_EXPERTS // N_GROUPS
    sub = lax.broadcasted_iota(I32, (gsz, tm), 0)
    blocks = [choice[g * gsz:(g + 1) * gsz] for g in range(N_GROUPS)]
    gscore = []
    for blk in blocks:
        m1 = jnp.max(blk, axis=0, keepdims=True)
        i1 = jnp.min(jnp.where(blk == m1, sub, gsz), axis=0, keepdims=True)
        m2 = jnp.max(jnp.where(sub == i1, -jnp.inf, blk), axis=0, keepdims=True)
        gscore.append(m1 + m2)
    masked = []
    for g in range(N_GROUPS):
        beats = jnp.zeros((1, tm), F32)
        for m in range(N_GROUPS):
            if m == g:
                continue
            b = gscore[m] > gscore[g]
            if m < g:
                b = jnp.logical_or(b, gscore[m] == gscore[g])
            beats = beats + b.astype(F32)
        masked.append(jnp.where(beats < TOPK_GROUPS, blocks[g], NEG_INF))
    cur = jnp.concatenate(masked, axis=0)
    eio = lax.broadcasted_iota(I32, (N_EXPERTS, tm), 0)
    chosen = jnp.zeros((N_EXPERTS, tm), jnp.bool_)
    wsum = jnp.zeros((1, tm), F32)
    for _ in range(TOP_K):
        m = jnp.max(cur, axis=0, keepdims=True)
        idx = jnp.min(jnp.where(cur == m, eio, N_EXPERTS), axis=0, keepdims=True)
        sel = eio == idx
        chosen = jnp.logical_or(chosen, sel)
        wsum = wsum + jnp.sum(jnp.where(sel, scores, 0.0), axis=0, keepdims=True)
        cur = jnp.where(sel, -jnp.inf, cur)
    wkey_ref[...] = jnp.where(chosen, scores / wsum * ROUTED_SCALE, 0.0)

    onehot = chosen.astype(BF16)
    r_i = lax.broadcasted_iota(I32, (tm, tm), 0)
    c_i = lax.broadcasted_iota(I32, (tm, tm), 1)
    before = (r_i < c_i).astype(BF16)
    prior = _bdot(onehot, before)
    key_ref[...] = jnp.where(chosen, prior, -1.0).astype(I32)
    cnt_ref[0] = jnp.sum(chosen.astype(F32), axis=1, keepdims=True)


def _router(x, sh, sc, gt, nw, w_router, e_bias, wsg, wsu, wsd):
    bsz, s, d = x.shape
    t = bsz * s
    tm = TM_SORT
    nt = s // tm
    sd = wsg.shape[1]
    vec = pl.BlockSpec((1, 1, d), lambda b, j: (b, 0, 0))
    full2 = lambda shape: pl.BlockSpec(shape, lambda b, j: (0, 0))
    tok = pl.BlockSpec((tm, d), lambda b, j: (b * nt + j, 0))
    etok = pl.BlockSpec((N_EXPERTS, tm), lambda b, j: (0, b * nt + j))
    return pl.pallas_call(
        _router_kernel,
        grid=(bsz, nt),
        in_specs=[pl.BlockSpec((1, tm, d), lambda b, j: (b, j, 0)), vec, vec, vec,
                  full2((1, d)), full2((N_EXPERTS, d)), full2((N_EXPERTS, 1)),
                  full2((d, sd)), full2((d, sd)), full2((sd, d))],
        out_specs=[tok, tok, etok, etok,
                   pl.BlockSpec((1, N_EXPERTS, 1), lambda b, j: (b * nt + j, 0, 0))],
        out_shape=[jax.ShapeDtypeStruct((t, d), BF16), jax.ShapeDtypeStruct((t, d), F32),
                   jax.ShapeDtypeStruct((N_EXPERTS, t), I32), jax.ShapeDtypeStruct((N_EXPERTS, t), F32),
                   jax.ShapeDtypeStruct((t // tm, N_EXPERTS, 1), F32)],
        compiler_params=_cparams(("arbitrary", "arbitrary")),
    )(x, sh, sc, gt, nw.reshape(1, d), w_router.T, e_bias.reshape(N_EXPERTS, 1),
      wsg.astype(BF16), wsu.astype(BF16), wsd.astype(BF16))


def _sort_meta(cnt, n_rows_max):
    nts = cnt.shape[0]
    c = cnt.reshape(nts, N_EXPERTS).astype(I32)
    run = (c + (RUN_ALIGN - 1)) // RUN_ALIGN * RUN_ALIGN
    lend = jnp.cumsum(run, axis=1)
    loff = lend - run
    used = lend[:, -1]
    per_e = jnp.sum(run, axis=0)
    ends = jnp.cumsum(per_e)
    starts = ends - per_e
    goff = starts[None, :] + jnp.cumsum(run, axis=0) - run
    total = ends[-1]

    row0 = jnp.arange(PIECES_TILE, dtype=I32) * RUN_ALIGN
    pe = jnp.sum((lend[:, None, :] <= row0[None, :, None]).astype(I32), axis=2)
    pe = jnp.minimum(pe, N_EXPERTS - 1)
    po = row0[None, :] - jnp.take_along_axis(loff, pe, axis=1)
    pg = jnp.take_along_axis(goff, pe, axis=1) + po

    bm = BM_EXPERT
    nb = n_rows_max // bm
    nv = nb + N_EXPERTS - 1
    last = N_EXPERTS - 1
    blo = jnp.arange(nb, dtype=I32) * bm
    first_e = jnp.minimum(jnp.sum((ends[None, :] <= blo[:, None]).astype(I32), axis=1), last)
    last_e = jnp.maximum(jnp.sum((starts[None, :] < (blo + bm)[:, None]).astype(I32), axis=1) - 1, first_e)
    per = last_e - first_e + 1
    off_incl = jnp.cumsum(per)
    off = off_incl - per
    nreal = off_incl[-1]
    vi = jnp.arange(nv, dtype=I32)
    real = vi < nreal
    vb = jnp.minimum(jnp.sum((off_incl[None, :] <= vi[:, None]).astype(I32), axis=1), nb - 1)
    ve = jnp.where(real, first_e[vb] + (vi - off[vb]), last_e[nb - 1]).astype(I32)
    vlo = jnp.where(real, jnp.clip(starts[ve] - vb * bm, 0, bm), 0).astype(I32)
    vhi = jnp.where(real, jnp.clip(ends[ve] - vb * bm, 0, bm), 0).astype(I32)
    vhi = jnp.maximum(vhi, vlo)
    vfirst = jnp.logical_and(real, vi == off[vb]).astype(I32)
    prev = jnp.concatenate([jnp.full((1,), -1, I32), ve[:-1]])
    vnew = (ve != prev).astype(I32)
    vbx = jnp.where(vb * bm < total, vb, 0).astype(I32)
    flat = lambda a: a.reshape(-1).astype(I32)
    return (flat(pe), flat(po), flat(pg), used.astype(I32)), (vb.astype(I32), vbx, ve, vlo, vhi, vfirst, vnew)


def _piece_copies(pg_ref, used_ref, tile, local, remote, sem, to_remote):
    def copy(j):
        l0 = pl.multiple_of(j * RUN_ALIGN, RUN_ALIGN)
        g0 = pl.multiple_of(pg_ref[tile * PIECES_TILE + j], RUN_ALIGN)
        lref = local.at[pl.ds(l0, RUN_ALIGN)]
        gref = remote.at[pl.ds(g0, RUN_ALIGN)]
        return pltpu.make_async_copy(lref, gref, sem) if to_remote else pltpu.make_async_copy(gref, lref, sem)

    def apply(act):
        def body(j, carry):
            getattr(copy(j), act)()
            return carry

        lax.fori_loop(0, used_ref[tile] // RUN_ALIGN, body, 0)

    return apply


def _piece_rows(pe_ref, po_ref, key_ref, first_piece, npieces, val_ref=None):
    tm = key_ref.shape[1]
    sub = lax.broadcasted_iota(I32, (RUN_ALIGN, tm), 0)
    out = []
    for jj in range(npieces):
        j = first_piece + jj
        e = pe_ref[j]
        hit = (key_ref[pl.ds(e, 1), :] - po_ref[j]) == sub
        val = 1.0 if val_ref is None else val_ref[pl.ds(e, 1), :]
        out.append(jnp.where(hit, val, 0.0))
    return out


def _dispatch_kernel(pe_ref, po_ref, pg_ref, used_ref, key_ref, h_ref, xs_ref, xbuf, p_sc, sem):
    i = pl.program_id(0)
    hb = h_ref[...]
    ch = SORT_CHUNK
    ppc = ch // RUN_ALIGN

    def chunk(ci, carry):
        r0 = pl.multiple_of(ci * ch, ch)

        @pl.when(r0 < used_ref[i])
        def _():
            rows = _piece_rows(pe_ref, po_ref, key_ref, i * PIECES_TILE + ci * ppc, ppc)
            for jj, p in enumerate(rows):
                p_sc[jj * RUN_ALIGN:(jj + 1) * RUN_ALIGN, :] = p.astype(BF16)
            xbuf[pl.ds(r0, ch), :] = _bdot(p_sc[...], hb).astype(BF16)
        return carry

    lax.fori_loop(0, xbuf.shape[0] // ch, chunk, 0)
    copies = _piece_copies(pg_ref, used_ref, i, xbuf, xs_ref, sem, True)
    copies("start")
    copies("wait")


def _dispatch(meta, key, hb, n_rows_max):
    t, d = hb.shape
    tm = TM_SORT
    return pl.pallas_call(
        _dispatch_kernel,
        grid_spec=pltpu.PrefetchScalarGridSpec(
            num_scalar_prefetch=4, grid=(t // tm,),
            in_specs=[pl.BlockSpec((N_EXPERTS, tm), lambda i, *_: (0, i)),
                      pl.BlockSpec((tm, d), lambda i, *_: (i, 0))],
            out_specs=pl.BlockSpec(memory_space=pl.ANY),
            scratch_shapes=[pltpu.VMEM((ROWS_TILE, d), BF16), pltpu.VMEM((SORT_CHUNK, tm), BF16),
                            pltpu.SemaphoreType.DMA(())]),
        out_shape=jax.ShapeDtypeStruct((n_rows_max, d), BF16),
        compiler_params=_cparams(("arbitrary",)),
    )(*meta, key, hb)


def _expert_kernel(vb_ref, vbx_ref, ve_ref, vlo_ref, vhi_ref, vfirst_ref, vnew_ref,
                   x_ref, wg_ref, wu_ref, wd_ref, o_ref, wg_sc, wu_sc, wd_sc):
    v = pl.program_id(0)
    lo = vlo_ref[v]
    hi = vhi_ref[v]

    @pl.when(vnew_ref[v] == 1)
    def _():
        wg_sc[...] = wg_ref[0, 0].astype(BF16)
        wu_sc[...] = wu_ref[0, 0].astype(BF16)
        wd_sc[...] = wd_ref[0, 0].astype(BF16)

    @pl.when(jnp.logical_and(vfirst_ref[v] == 1, hi <= lo))
    def _():
        o_ref[...] = jnp.zeros_like(o_ref)

    @pl.when(hi > lo)
    def _():
        xb = x_ref[...]
        act = (_silu(_bdot(xb, wg_sc[...])) * _bdot(xb, wu_sc[...])).astype(BF16)
        y = _bdot(act, wd_sc[...]).astype(BF16)
        rows = lax.broadcasted_iota(I32, (y.shape[0], 1), 0)
        mine = jnp.logical_and(rows >= lo, rows < hi)

        @pl.when(vfirst_ref[v] == 1)
        def _():
            o_ref[...] = jnp.where(mine, y, jnp.zeros_like(y))

        @pl.when(vfirst_ref[v] == 0)
        def _():
            o_ref[...] = jnp.where(mine, y, o_ref[...])


def _experts(meta, xs, layer, w_gate, w_up, w_down):
    r, d = xs.shape
    ed = w_gate.shape[-1]
    bm = BM_EXPERT
    nv = meta[0].shape[0]
    omap = lambda v, vb, vbx, ve, vlo, vhi, vf, vn: (vb[v], 0)
    xmap = lambda v, vb, vbx, ve, vlo, vhi, vf, vn: (vbx[v], 0)
    wmap = lambda v, vb, vbx, ve, vlo, vhi, vf, vn: (layer, ve[v], 0, 0)
    return pl.pallas_call(
        _expert_kernel,
        grid_spec=pltpu.PrefetchScalarGridSpec(
            num_scalar_prefetch=7, grid=(nv,),
            in_specs=[pl.BlockSpec((bm, d), xmap), pl.BlockSpec((1, 1, d, ed), wmap),
                      pl.BlockSpec((1, 1, d, ed), wmap), pl.BlockSpec((1, 1, ed, d), wmap)],
            out_specs=pl.BlockSpec((bm, d), omap),
            scratch_shapes=[pltpu.VMEM((d, ed), BF16), pltpu.VMEM((d, ed), BF16),
                            pltpu.VMEM((ed, d), BF16)]),
        out_shape=jax.ShapeDtypeStruct((r, d), BF16),
        compiler_params=_cparams(("arbitrary",)),
    )(*meta, xs, w_gate, w_up, w_down)


def _combine_kernel(pe_ref, po_ref, pg_ref, used_ref, key_ref, wkey_ref, ys_ref, base_ref, g_ref,
                    fn_ref, o_ref, ybuf, q_sc, sem, *, final):
    i = pl.program_id(0)
    tm = base_ref.shape[0]
    ch = SORT_CHUNK
    ppc = ch // RUN_ALIGN

    @pl.when(i == 0)
    def _():
        ybuf[...] = jnp.zeros_like(ybuf)

    copies = _piece_copies(pg_ref, used_ref, i, ybuf, ys_ref, sem, False)
    copies("start")

    for ci in range(ybuf.shape[0] // ch):
        cs = slice(ci * ch, (ci + 1) * ch)

        @pl.when(ci * ch < used_ref[i])
        def _():
            rows = _piece_rows(pe_ref, po_ref, key_ref, i * PIECES_TILE + ci * ppc, ppc, wkey_ref)
            q_sc[:, cs] = jnp.concatenate(rows, axis=0).T.astype(BF16)

        @pl.when(ci * ch >= used_ref[i])
        def _():
            q_sc[:, cs] = jnp.zeros((tm, ch), BF16)

    copies("wait")
    out = base_ref[...] + g_ref[0] * _bdot(q_sc[...], ybuf[...])
    if final:
        out = _rms(out, fn_ref[...])
    o_ref[...] = out


def _combine(meta, key, wkey, ys, base, gt, fnorm, seq, final):
    t, d = base.shape
    tm = TM_SORT
    nt = seq // tm
    return pl.pallas_call(
        functools.partial(_combine_kernel, final=final),
        grid_spec=pltpu.PrefetchScalarGridSpec(
            num_scalar_prefetch=4, grid=(t // tm,),
            in_specs=[pl.BlockSpec((N_EXPERTS, tm), lambda i, *_: (0, i)),
                      pl.BlockSpec((N_EXPERTS, tm), lambda i, *_: (0, i)),
                      pl.BlockSpec(memory_space=pl.ANY),
                      pl.BlockSpec((tm, d), lambda i, *_: (i, 0)),
                      pl.BlockSpec((1, 1, d), lambda i, *_: (i // nt, 0, 0)),
                      pl.BlockSpec((1, d), lambda i, *_: (0, 0))],
            out_specs=pl.BlockSpec((tm, d), lambda i, *_: (i, 0)),
            scratch_shapes=[pltpu.VMEM((ROWS_TILE, d), BF16), pltpu.VMEM((tm, ROWS_TILE), BF16),
                            pltpu.SemaphoreType.DMA(())]),
        out_shape=jax.ShapeDtypeStruct((t, d), F32),
        compiler_params=_cparams(("arbitrary",)),
    )(*meta, key, wkey, ys, base, gt, fnorm.reshape(1, d))


def _moe(x, sh, sc, gt, nw, w_router, e_bias, layer, w_gate, w_up, w_down, wsg, wsu, wsd, fnorm, final):
    bsz, s, d = x.shape
    t = bsz * s
    n_rows_max = (t // TM_SORT) * ROWS_TILE_USED
    n_rows_max = -(-n_rows_max // BM_EXPERT) * BM_EXPERT
    hb, base, key, wkey, cnt = _router(x, sh, sc, gt, nw, w_router, e_bias, wsg, wsu, wsd)
    piece_meta, visit_meta = _sort_meta(cnt, n_rows_max)
    xs = _dispatch(piece_meta, key, hb, n_rows_max)
    ys = _experts(visit_meta, xs, layer, w_gate, w_up, w_down)
    out = _combine(piece_meta, key, wkey, ys, base, gt, fnorm, s, final)
    return out.reshape(bsz, s, d)


def _kv_kernel(x_ref, sh_ref, sc_ref, nw_ref, wk_ref, wvt_ref, k_ref, vt_ref, km_ref):
    x = x_ref[0]
    hb = (_rms(x, nw_ref[...]) * (1.0 + sc_ref[0]) + sh_ref[0]).astype(BF16)
    k = _bdot(hb, wk_ref[...])
    vt = _bdot_nt(wvt_ref[...], hb)
    for p in range(k.shape[1] // LANES):
        k_ref[0, p, 0] = k[:, p * LANES:(p + 1) * LANES].astype(BF16)
        vt_ref[0, p, 0] = vt[p * LANES:(p + 1) * LANES, :].astype(BF16)
    km_ref[0, 0] = jnp.mean(k, axis=0, keepdims=True)


def _kv(x, sh, sc, nw, w_k, w_v):
    bsz, s, d = x.shape
    nb = s // B_BLOCK
    npair = d // LANES
    vec = pl.BlockSpec((1, 1, d), lambda b, j: (b, 0, 0))
    full2 = lambda shape: pl.BlockSpec(shape, lambda b, j: (0, 0))
    return pl.pallas_call(
        _kv_kernel,
        grid=(bsz, nb),
        in_specs=[pl.BlockSpec((1, B_BLOCK, d), lambda b, j: (b, j, 0)), vec, vec,
                  full2((1, d)), full2((d, d)), full2((d, d))],
        out_specs=[pl.BlockSpec((1, npair, 1, B_BLOCK, LANES), lambda b, j: (b, 0, j, 0, 0)),
                   pl.BlockSpec((1, npair, 1, LANES, B_BLOCK), lambda b, j: (b, 0, j, 0, 0)),
                   pl.BlockSpec((1, 1, 1, d), lambda b, j: (b, j, 0, 0))],
        out_shape=[jax.ShapeDtypeStruct((bsz, npair, nb, B_BLOCK, LANES), BF16),
                   jax.ShapeDtypeStruct((bsz, npair, nb, LANES, B_BLOCK), BF16),
                   jax.ShapeDtypeStruct((bsz, nb, 1, d), F32)],
        compiler_params=_cparams(("arbitrary", "arbitrary")),
    )(x, sh, sc, nw.reshape(1, d), w_k.astype(BF16), w_v.T.astype(BF16))


def _attn_kernel(x_ref, sh_ref, sc_ref, g_ref, nw_ref, wqt_ref, wo_ref, k_ref, vt_ref, km_ref,
                 o_ref, qt_sc, qs_sc, acc_sc, sel_sc, m_sc, l_sc, *, nb, n_sel):
    qb = pl.program_id(1)
    x = x_ref[0]
    bq = x.shape[0]
    npair = qt_sc.shape[0]
    nbp = km_ref.shape[2]
    hd = LANES // 2
    scale = float(hd) ** -0.5
    h = _rms(x, nw_ref[...]) * (1.0 + sc_ref[0]) + sh_ref[0]
    qt = _bdot_nt(wqt_ref[...], h.astype(BF16))
    for p in range(npair):
        qt_sc[p] = qt[p * LANES:(p + 1) * LANES, :]

    subn = lax.broadcasted_iota(I32, (nbp, bq), 0)
    past = subn < qb
    krow = lax.broadcasted_iota(I32, (B_BLOCK, bq), 0)
    qcol = lax.broadcasted_iota(I32, (B_BLOCK, bq), 1)
    causal = krow <= qcol
    rowh = lax.broadcasted_iota(I32, (LANES, 1), 0)

    grp = ATTN_PAIR_UNROLL
    heads = [(u, e) for u in range(grp) for e in range(2)]

    def own_body(gi, carry):
        ps = [gi * grp + u for u in range(grp)]
        q2ts = [qt_sc[p] for p in ps]
        kms = [km_ref[0, p] for p in ps]
        kown = [k_ref[0, p, qb] for p in ps]
        vown = [vt_ref[0, p, qb] for p in ps]
        qets = [jnp.where((rowh >= hd) if e == 1 else (rowh < hd), q2ts[u], 0.0) for u, e in heads]
        qsts = [(q * scale).astype(BF16) for q in qets]
        ss = [jnp.where(causal, _bdot(kown[u], qsts[i]), NEG_INF) for i, (u, e) in enumerate(heads)]
        gates = [_dot3(kms[u], qets[i]) for i, (u, e) in enumerate(heads)]
        ms = [jnp.max(s, axis=0, keepdims=True) for s in ss]
        pes = [jnp.exp(s - m) for s, m in zip(ss, ms)]
        ls = [jnp.sum(pe, axis=0, keepdims=True) for pe in pes]
        accs = [_bdot(vown[u][e * hd:(e + 1) * hd, :], pes[i].astype(BF16))
                for i, (u, e) in enumerate(heads)]
        sels = []
        for gate in gates:
            selt = jnp.zeros((nbp, bq), F32)
            for n in range(nb):
                gn = gate[n:n + 1, :]
                beats = jnp.logical_or(gate > gn, jnp.logical_and(gate == gn, subn < n))
                beats = jnp.logical_and(beats, past)
                cnt = jnp.sum(beats.astype(F32), axis=0, keepdims=True)
                selt = jnp.where(subn == n, (cnt < n_sel).astype(F32), selt)
            sels.append(selt)
        for i, (u, e) in enumerate(heads):
            p = ps[u]
            sel_sc[p, e] = sels[i]
            qs_sc[p, e] = qsts[i]
            m_sc[p, e] = ms[i]
            l_sc[p, e] = ls[i]
            acc_sc[p, e * hd:(e + 1) * hd, :] = accs[i]
        return carry

    lax.fori_loop(0, npair // grp, own_body, 0)

    def kb_body(kb, carry):
        def group_body(gi, c2):
            ps = [gi * grp + u for u in range(grp)]
            kbl = [k_ref[0, p, kb] for p in ps]
            vbl = [vt_ref[0, p, kb] for p in ps]
            qsts = [qs_sc[ps[u], e] for u, e in heads]
            rows = [sel_sc[ps[u], e, pl.ds(kb, 1), :] for u, e in heads]
            m_old = [m_sc[ps[u], e] for u, e in heads]
            l_old = [l_sc[ps[u], e] for u, e in heads]
            a_old = [acc_sc[ps[u], e * hd:(e + 1) * hd, :] for u, e in heads]
            ss = [jnp.where(rows[i] > 0.5, _bdot(kbl[u], qsts[i]), NEG_INF)
                  for i, (u, e) in enumerate(heads)]
            m_new = [jnp.maximum(m, jnp.max(s, axis=0, keepdims=True)) for m, s in zip(m_old, ss)]
            alphas = [jnp.exp(m - mn) for m, mn in zip(m_old, m_new)]
            pes = [jnp.exp(s - mn) for s, mn in zip(ss, m_new)]
            l_new = [a * l + jnp.sum(pe, axis=0, keepdims=True) for a, l, pe in zip(alphas, l_old, pes)]
            a_new = [alphas[i] * a_old[i] + _bdot(vbl[u][e * hd:(e + 1) * hd, :], pes[i].astype(BF16))
                     for i, (u, e) in enumerate(heads)]
            for i, (u, e) in enumerate(heads):
                p = ps[u]
                m_sc[p, e] = m_new[i]
                l_sc[p, e] = l_new[i]
                acc_sc[p, e * hd:(e + 1) * hd, :] = a_new[i]
            return c2

        lax.fori_loop(0, npair // grp, group_body, 0)
        return carry

    lax.fori_loop(0, qb, kb_body, 0)

    parts = []
    for p in range(npair):
        for e in range(2):
            parts.append(acc_sc[p, e * hd:(e + 1) * hd, :] / l_sc[p, e])
    ot = jnp.concatenate(parts, axis=0)
    o_ref[0] = x + g_ref[0] * _bdot(ot.T.astype(BF16), wo_ref[...])


def _attn(x, sh, sc, gt, nw, w_q, w_o, k5, vt5, km2):
    bsz, s, d = x.shape
    nb = s // B_BLOCK
    npair = d // LANES
    nbp = km2.shape[2]
    n_sel = min(B_TOPK, nb - 1)
    vec = pl.BlockSpec((1, 1, d), lambda b, j: (b, 0, 0))
    full2 = lambda shape: pl.BlockSpec(shape, lambda b, j: (0, 0))
    return pl.pallas_call(
        functools.partial(_attn_kernel, nb=nb, n_sel=n_sel),
        grid=(bsz, nb),
        in_specs=[pl.BlockSpec((1, B_BLOCK, d), lambda b, j: (b, j, 0)), vec, vec, vec,
                  full2((1, d)), full2((d, d)), full2((d, d)),
                  pl.BlockSpec((1, npair, nb, B_BLOCK, LANES), lambda b, j: (b, 0, 0, 0, 0)),
                  pl.BlockSpec((1, npair, nb, LANES, B_BLOCK), lambda b, j: (b, 0, 0, 0, 0)),
                  pl.BlockSpec((1, npair, nbp, LANES), lambda b, j: (b, 0, 0, 0))],
        out_specs=pl.BlockSpec((1, B_BLOCK, d), lambda b, j: (b, j, 0)),
        out_shape=jax.ShapeDtypeStruct((bsz, s, d), F32),
        scratch_shapes=[pltpu.VMEM((npair, LANES, B_BLOCK), F32),
                        pltpu.VMEM((npair, 2, LANES, B_BLOCK), BF16),
                        pltpu.VMEM((npair, LANES, B_BLOCK), F32),
                        pltpu.VMEM((npair, 2, nbp, B_BLOCK), F32),
                        pltpu.VMEM((npair, 2, 1, B_BLOCK), F32),
                        pltpu.VMEM((npair, 2, 1, B_BLOCK), F32)],
        compiler_params=_cparams(("arbitrary", "arbitrary")),
    )(x, sh, sc, gt, nw.reshape(1, d), w_q.T.astype(BF16), w_o.astype(BF16), k5, vt5, km2)


def kernel(x, c, ada_w, ada_b, norm_mix, norm_ffn, a_w_in, a_b_in, a_ln_g, a_ln_b, a_w_s, a_b_s,
           a_w_out, kv_norm, kv_ada_w, kv_ada_b, kv_w_k, kv_w_v, b_w_q, b_w_o, moe_router, moe_bias,
           moe_w_gate, moe_w_up, moe_w_down, sh_w_gate, sh_w_up, sh_w_down, final_norm):
    bsz, s, d = x.shape
    depth = ada_w.shape[0]
    n_a = a_w_in.shape[0]
    assert s % B_BLOCK == 0 and s % TM_GMLP == 0 and d % LANES == 0
    nb = s // B_BLOCK
    npair = d // LANES
    nbp = -(-nb // 8) * 8

    def split(m, n):
        return [m[:, i * d:(i + 1) * d].reshape(bsz, 1, d) for i in range(n)]

    layer_mods = _ada(c, ada_w, ada_b)
    k5 = vt5 = km2 = None
    for i in range(depth):
        sh1, sc1, g1, sh2, sc2, g2 = split(layer_mods[i], 6)
        if i < n_a:
            x = _gmlp(x, sh1, sc1, g1, norm_mix[i], a_w_in[i], a_b_in[i], a_ln_g[i], a_ln_b[i],
                      a_w_s[i], a_b_s[i], a_w_out[i])
        else:
            if k5 is None:
                ksh, ksc = split(_ada(c, kv_ada_w[None], kv_ada_b[None])[0], 2)
                k5, vt5, km = _kv(x, ksh, ksc, kv_norm, kv_w_k, kv_w_v)
                km = km.reshape(bsz, nb, npair, LANES).transpose(0, 2, 1, 3)
                km2 = jnp.pad(km, ((0, 0), (0, 0), (0, nbp - nb), (0, 0)))
            j = i - n_a
            x = _attn(x, sh1, sc1, g1, norm_mix[i], b_w_q[j], b_w_o[j], k5, vt5, km2)
        x = _moe(x, sh2, sc2, g2, norm_ffn[i], moe_router[i], moe_bias[i], i, moe_w_gate,
                 moe_w_up, moe_w_down, sh_w_gate[i], sh_w_up[i], sh_w_down[i],
                 final_norm, i == depth - 1)
    return x
```

```python
import functools

import jax
import jax.numpy as jnp
from jax import lax
from jax.experimental import pallas as pl
from jax.experimental.pallas import tpu as pltpu

F32 = jnp.float32
BF16 = jnp.bfloat16
I32 = jnp.int32

RMS_EPS = 1e-6
LN_EPS = 1e-5
NEG_INF = -1e30

A_CHUNK = 128
A_GROUPS = 8
B_HEADS = 16
B_BLOCK = 256
B_TOPK = 3
N_EXPERTS = 64
TOP_K = 8
N_GROUPS = 8
TOPK_GROUPS = 4
ROUTED_SCALE = 2.5

LANES = 128
VMEM_LIMIT = 56 * 1024 * 1024

TM_GMLP = 256
TM_SORT = 512
RUN_ALIGN = 16
SORT_CHUNK = 256
ROWS_TILE_USED = TM_SORT * TOP_K + N_EXPERTS * (RUN_ALIGN - 1)
ROWS_TILE = -(-ROWS_TILE_USED // SORT_CHUNK) * SORT_CHUNK
PIECES_TILE = ROWS_TILE // RUN_ALIGN
PIECES_PAD = -(-PIECES_TILE // LANES) * LANES
BM_EXPERT = 512
BM_CHAIN = 256
ATTN_PAIR_UNROLL = 4


def _cparams(sem):
    return pltpu.CompilerParams(dimension_semantics=sem, vmem_limit_bytes=VMEM_LIMIT)


def _sigmoid(x):
    return 1.0 / (1.0 + jnp.exp(-x))


def _silu(x):
    return x * _sigmoid(x)


def _gelu_tanh(x):
    return 0.5 * x * (1.0 + jnp.tanh(0.7978845608028654 * (x + 0.044715 * (x * x * x))))


def _rms(x, g):
    return x * lax.rsqrt(jnp.mean(x * x, axis=-1, keepdims=True) + RMS_EPS) * g


def _bdot(a, b):
    return jnp.dot(a, b, preferred_element_type=F32)


def _bdot_nt(a, b):
    return lax.dot_general(a, b, (((1,), (1,)), ((), ())), preferred_element_type=F32)


def _split(a):
    hi = a.astype(BF16)
    lo = (a - hi.astype(F32)).astype(BF16)
    return hi, lo


def _dot3_nt(a, b):
    ah, al = _split(a)
    bh, bl = _split(b)
    return _bdot_nt(ah, bh) + (_bdot_nt(ah, bl) + _bdot_nt(al, bh))


def _dot3(a, b):
    ah, al = _split(a)
    bh, bl = _split(b)
    return _bdot(ah, bh) + (_bdot(ah, bl) + _bdot(al, bh))


def _ada_kernel(c_ref, w_ref, b_ref, o_ref):
    a = _silu(c_ref[...]).astype(BF16)
    o_ref[0] = _bdot(a, w_ref[0].astype(BF16)) + b_ref[0]


def _ada(c, w, b):
    bsz, d = c.shape
    nl, _, n = w.shape
    tn = 1024
    return pl.pallas_call(
        _ada_kernel,
        grid=(nl, n // tn),
        in_specs=[pl.BlockSpec((bsz, d), lambda l, j: (0, 0)),
                  pl.BlockSpec((1, d, tn), lambda l, j: (l, 0, j)),
                  pl.BlockSpec((1, 1, tn), lambda l, j: (l, 0, j))],
        out_specs=pl.BlockSpec((1, bsz, tn), lambda l, j: (l, 0, j)),
        out_shape=jax.ShapeDtypeStruct((nl, bsz, n), F32),
        compiler_params=_cparams(("arbitrary", "arbitrary")),
    )(c, w, b.reshape(nl, 1, n))


def _gmlp_kernel(x_ref, sh_ref, sc_ref, g_ref, nw_ref, win_ref, bin_ref, lng_ref, lnb_ref,
                 ws_ref, bst_ref, wout_ref, o_ref, y_sc):
    x = x_ref[0]
    tm = x.shape[0]
    h = _rms(x, nw_ref[...]) * (1.0 + sc_ref[0]) + sh_ref[0]
    z = _gelu_tanh(_bdot(h.astype(BF16), win_ref[...]) + bin_ref[...])
    aw = z.shape[1] // 2
    gd = aw // A_GROUPS
    u = z[:, :aw]
    v = z[:, aw:]
    mu = jnp.mean(v, axis=-1, keepdims=True)
    dv = v - mu
    var = jnp.mean(dv * dv, axis=-1, keepdims=True)
    vn = (dv * lax.rsqrt(var + LN_EPS) * lng_ref[...] + lnb_ref[...]).astype(BF16)
    row = lax.broadcasted_iota(I32, (A_CHUNK, A_CHUNK), 0)
    col = lax.broadcasted_iota(I32, (A_CHUNK, A_CHUNK), 1)
    causal = col <= row
    for g in range(A_GROUPS):
        wg = jnp.where(causal, ws_ref[g], 0.0).astype(BF16)
        bcol = bst_ref[:, g:g + 1]
        for ci in range(tm // A_CHUNK):
            rs = slice(ci * A_CHUNK, (ci + 1) * A_CHUNK)
            cs = slice(g * gd, (g + 1) * gd)
            sv = _bdot(wg, vn[rs, cs]) + bcol
            y_sc[rs, cs] = (u[rs, cs] * sv).astype(BF16)
    o_ref[0] = x + g_ref[0] * _bdot(y_sc[...], wout_ref[...])


def _gmlp(x, sh, sc, gt, nw, w_in, b_in, ln_g, ln_b, w_s, b_s, w_out):
    bsz, s, d = x.shape
    tm = TM_GMLP
    n_in = w_in.shape[1]
    aw = n_in // 2
    vec = pl.BlockSpec((1, 1, d), lambda b, j: (b, 0, 0))
    full2 = lambda shape: pl.BlockSpec(shape, lambda b, j: (0, 0))
    return pl.pallas_call(
        _gmlp_kernel,
        grid=(bsz, s // tm),
        in_specs=[pl.BlockSpec((1, tm, d), lambda b, j: (b, j, 0)), vec, vec, vec,
                  full2((1, d)), full2((d, n_in)), full2((1, n_in)), full2((1, aw)), full2((1, aw)),
                  pl.BlockSpec((A_GROUPS, A_CHUNK, A_CHUNK), lambda b, j: (0, 0, 0)),
                  full2((A_CHUNK, A_GROUPS)), full2((aw, d))],
        out_specs=pl.BlockSpec((1, tm, d), lambda b, j: (b, j, 0)),
        out_shape=jax.ShapeDtypeStruct((bsz, s, d), F32),
        scratch_shapes=[pltpu.VMEM((tm, aw), BF16)],
        compiler_params=_cparams(("arbitrary", "arbitrary")),
    )(x, sh, sc, gt, nw.reshape(1, d), w_in.astype(BF16), b_in.reshape(1, n_in),
      ln_g.reshape(1, aw), ln_b.reshape(1, aw), w_s, b_s.T, w_out.astype(BF16))


def _router_kernel(x_ref, sh_ref, sc_ref, g_ref, nw_ref, wrt_ref, bias_ref, wsg_ref, wsu_ref, wsd_ref,
                   h_ref, base_ref, key_ref, wkey_ref, cnt_ref, pe_ref, po_ref):
    x = x_ref[0]
    tm = x.shape[0]
    h = _rms(x, nw_ref[...]) * (1.0 + sc_ref[0]) + sh_ref[0]
    hb = h.astype(BF16)
    h_ref[...] = hb
    act = (_silu(_bdot(hb, wsg_ref[...])) * _bdot(hb, wsu_ref[...])).astype(BF16)
    base_ref[...] = x + g_ref[0] * _bdot(act, wsd_ref[...])

    scores = _sigmoid(_dot3_nt(wrt_ref[...], h))
    choice = scores + bias_ref[...]
    gsz = N_EXPERTS // N_GROUPS
    sub = lax.broadcasted_iota(I32, (gsz, tm), 0)
    blocks = [choice[g * gsz:(g + 1) * gsz] for g in range(N_GROUPS)]
    gscore = []
    for blk in blocks:
        m1 = jnp.max(blk, axis=0, keepdims=True)
        i1 = jnp.min(jnp.where(blk == m1, sub, gsz), axis=0, keepdims=True)
        m2 = jnp.max(jnp.where(sub == i1, -jnp.inf, blk), axis=0, keepdims=True)
        gscore.append(m1 + m2)
    masked = []
    for g in range(N_GROUPS):
        beats = jnp.zeros((1, tm), F32)
        for m in range(N_GROUPS):
            if m == g:
                continue
            b = gscore[m] > gscore[g]
            if m < g:
                b = jnp.logical_or(b, gscore[m] == gscore[g])
            beats = beats + b.astype(F32)
        masked.append(jnp.where(beats < TOPK_GROUPS, blocks[g], NEG_INF))
    cur = jnp.concatenate(masked, axis=0)
    eio = lax.broadcasted_iota(I32, (N_EXPERTS, tm), 0)
    chosen = jnp.zeros((N_EXPERTS, tm), jnp.bool_)
    wsum = jnp.zeros((1, tm), F32)
    for _ in range(TOP_K):
        m = jnp.max(cur, axis=0, keepdims=True)
        idx = jnp.min(jnp.where(cur == m, eio, N_EXPERTS), axis=0, keepdims=True)
        sel = eio == idx
        chosen = jnp.logical_or(chosen, sel)
        wsum = wsum + jnp.sum(jnp.where(sel, scores, 0.0), axis=0, keepdims=True)
        cur = jnp.where(sel, -jnp.inf, cur)
    wkey_ref[...] = jnp.where(chosen, scores / wsum * ROUTED_SCALE, 0.0)

    onehot = chosen.astype(BF16)
    r_i = lax.broadcasted_iota(I32, (tm, tm), 0)
    c_i = lax.broadcasted_iota(I32, (tm, tm), 1)
    before = (r_i < c_i).astype(BF16)
    prior = _bdot(onehot, before)
    key_ref[...] = jnp.where(chosen, prior, -1.0).astype(I32)
    cnt = jnp.sum(chosen.astype(F32), axis=1, keepdims=True)
    cnt_ref[0] = cnt

    run_p = jnp.floor((cnt + (RUN_ALIGN - 1)) / RUN_ALIGN)
    e_r = lax.broadcasted_iota(I32, (N_EXPERTS, N_EXPERTS), 0)
    e_c = lax.broadcasted_iota(I32, (N_EXPERTS, N_EXPERTS), 1)
    incl = (e_c <= e_r).astype(BF16)
    lend = _bdot(incl, jnp.broadcast_to(run_p, (N_EXPERTS, LANES)).astype(BF16))[:, 0:1]
    loff = lend - run_p
    pj = lax.broadcasted_iota(I32, (N_EXPERTS, PIECES_PAD), 1).astype(F32)
    er = lax.broadcasted_iota(I32, (N_EXPERTS, PIECES_PAD), 0).astype(F32)
    pe = jnp.minimum(jnp.sum((lend <= pj).astype(F32), axis=0, keepdims=True), N_EXPERTS - 1.0)
    lo = jnp.sum(jnp.where(er == pe, loff, 0.0), axis=0, keepdims=True)
    pe_ref[0] = pe.astype(I32)
    po_ref[0] = ((pj[0:1, :] - lo) * RUN_ALIGN).astype(I32)


def _router(x, sh, sc, gt, nw, w_router, e_bias, wsg, wsu, wsd):
    bsz, s, d = x.shape
    t = bsz * s
    tm = TM_SORT
    nt = s // tm
    sd = wsg.shape[1]
    vec = pl.BlockSpec((1, 1, d), lambda b, j: (b, 0, 0))
    full2 = lambda shape: pl.BlockSpec(shape, lambda b, j: (0, 0))
    tok = pl.BlockSpec((tm, d), lambda b, j: (b * nt + j, 0))
    etok = pl.BlockSpec((N_EXPERTS, tm), lambda b, j: (0, b * nt + j))
    ptab = pl.BlockSpec((1, 1, PIECES_PAD), lambda b, j: (b * nt + j, 0, 0))
    return pl.pallas_call(
        _router_kernel,
        grid=(bsz, nt),
        in_specs=[pl.BlockSpec((1, tm, d), lambda b, j: (b, j, 0)), vec, vec, vec,
                  full2((1, d)), full2((N_EXPERTS, d)), full2((N_EXPERTS, 1)),
                  full2((d, sd)), full2((d, sd)), full2((sd, d))],
        out_specs=[tok, tok, etok, etok,
                   pl.BlockSpec((1, N_EXPERTS, 1), lambda b, j: (b * nt + j, 0, 0)), ptab, ptab],
        out_shape=[jax.ShapeDtypeStruct((t, d), BF16), jax.ShapeDtypeStruct((t, d), F32),
                   jax.ShapeDtypeStruct((N_EXPERTS, t), I32), jax.ShapeDtypeStruct((N_EXPERTS, t), F32),
                   jax.ShapeDtypeStruct((t // tm, N_EXPERTS, 1), F32),
                   jax.ShapeDtypeStruct((t // tm, 1, PIECES_PAD), I32),
                   jax.ShapeDtypeStruct((t // tm, 1, PIECES_PAD), I32)],
        compiler_params=_cparams(("arbitrary", "arbitrary")),
    )(x, sh, sc, gt, nw.reshape(1, d), w_router.T, e_bias.reshape(N_EXPERTS, 1),
      wsg.astype(BF16), wsu.astype(BF16), wsd.astype(BF16))


def _rows_max(t):
    rows = (t // TM_SORT) * ROWS_TILE_USED + N_EXPERTS * (BM_EXPERT - RUN_ALIGN)
    return -(-rows // BM_EXPERT) * BM_EXPERT


def _sort_meta(cnt, n_rows_max):
    nts = cnt.shape[0]
    c = cnt.reshape(nts, N_EXPERTS).astype(I32)
    run = (c + (RUN_ALIGN - 1)) // RUN_ALIGN * RUN_ALIGN
    used = jnp.sum(run, axis=1)
    per_e = jnp.sum(run, axis=0)
    seg = (per_e + (BM_EXPERT - 1)) // BM_EXPERT * BM_EXPERT
    ends = jnp.cumsum(seg)
    starts = ends - seg
    goff = starts[None, :] + jnp.cumsum(run, axis=0) - run
    pad_start = starts + per_e
    pad_cnt = (seg - per_e) // RUN_ALIGN

    nb = n_rows_max // BM_EXPERT
    blo = jnp.arange(nb, dtype=I32) * BM_EXPERT
    be = jnp.sum((ends[None, :] <= blo[:, None]).astype(I32), axis=1)
    breal = (be < N_EXPERTS).astype(I32)
    be = jnp.minimum(be, N_EXPERTS - 1).astype(I32)
    prev = jnp.concatenate([jnp.full((1,), -1, I32), be[:-1]])
    bnew = (be != prev).astype(I32)
    bx = jnp.where(breal == 1, jnp.arange(nb, dtype=I32), 0)
    i32 = lambda a: a.reshape(-1).astype(I32)
    return (i32(goff), i32(used), i32(pad_start), i32(pad_cnt)), (bx, be, bnew, breal)


def _piece_copies(pe_ref, po_ref, goff_ref, used_ref, tile, local, remote, sem, to_remote):
    def copy(j):
        idx = tile * PIECES_PAD + j
        l0 = pl.multiple_of(j * RUN_ALIGN, RUN_ALIGN)
        g0 = pl.multiple_of(goff_ref[tile * N_EXPERTS + pe_ref[idx]] + po_ref[idx], RUN_ALIGN)
        lref = local.at[pl.ds(l0, RUN_ALIGN)]
        gref = remote.at[pl.ds(g0, RUN_ALIGN)]
        return pltpu.make_async_copy(lref, gref, sem) if to_remote else pltpu.make_async_copy(gref, lref, sem)

    def apply(act):
        def body(j, carry):
            getattr(copy(j), act)()
            return carry

        lax.fori_loop(0, used_ref[tile] // RUN_ALIGN, body, 0)

    return apply


def _pad_copies(pad_start_ref, pad_cnt_ref, tile, ntiles, zeros, remote, sem):
    share = -(-N_EXPERTS // ntiles)

    def apply(act):
        for q in range(share):
            e = tile * share + q
            ec = jnp.minimum(e, N_EXPERTS - 1)
            n = jnp.where(e < N_EXPERTS, pad_cnt_ref[ec], 0)
            base = pad_start_ref[ec]

            def body(j, carry):
                g0 = pl.multiple_of(base + j * RUN_ALIGN, RUN_ALIGN)
                getattr(pltpu.make_async_copy(zeros, remote.at[pl.ds(g0, RUN_ALIGN)], sem), act)()
                return carry

            lax.fori_loop(0, n, body, 0)

    return apply


def _piece_rows(pe_ref, po_ref, key_ref, first_piece, npieces, val_ref=None):
    tm = key_ref.shape[1]
    sub = lax.broadcasted_iota(I32, (RUN_ALIGN, tm), 0)
    out = []
    for jj in range(npieces):
        j = first_piece + jj
        e = pe_ref[j]
        hit = (key_ref[pl.ds(e, 1), :] - po_ref[j]) == sub
        val = 1.0 if val_ref is None else val_ref[pl.ds(e, 1), :]
        out.append(jnp.where(hit, val, 0.0))
    return out


def _dispatch_kernel(pe_ref, po_ref, goff_ref, used_ref, pad_start_ref, pad_cnt_ref,
                     key_ref, h_ref, xs_ref, xbuf, p_sc, zbuf, sem, zsem, *, ntiles):
    i = pl.program_id(0)
    hb = h_ref[...]
    ch = SORT_CHUNK
    ppc = ch // RUN_ALIGN
    zbuf[...] = jnp.zeros_like(zbuf)
    pads = _pad_copies(pad_start_ref, pad_cnt_ref, i, ntiles, zbuf, xs_ref, zsem)
    pads("start")

    def chunk(ci, carry):
        r0 = pl.multiple_of(ci * ch, ch)

        @pl.when(r0 < used_ref[i])
        def _():
            rows = _piece_rows(pe_ref, po_ref, key_ref, i * PIECES_PAD + ci * ppc, ppc)
            for jj, p in enumerate(rows):
                p_sc[jj * RUN_ALIGN:(jj + 1) * RUN_ALIGN, :] = p.astype(BF16)
            xbuf[pl.ds(r0, ch), :] = _bdot(p_sc[...], hb).astype(BF16)
        return carry

    lax.fori_loop(0, xbuf.shape[0] // ch, chunk, 0)
    copies = _piece_copies(pe_ref, po_ref, goff_ref, used_ref, i, xbuf, xs_ref, sem, True)
    copies("start")
    pads("wait")
    copies("wait")


def _dispatch(meta, pe, po, key, hb, n_rows_max):
    t, d = hb.shape
    tm = TM_SORT
    goff, used, pad_start, pad_cnt = meta
    return pl.pallas_call(
        functools.partial(_dispatch_kernel, ntiles=t // tm),
        grid_spec=pltpu.PrefetchScalarGridSpec(
            num_scalar_prefetch=6, grid=(t // tm,),
            in_specs=[pl.BlockSpec((N_EXPERTS, tm), lambda i, *_: (0, i)),
                      pl.BlockSpec((tm, d), lambda i, *_: (i, 0))],
            out_specs=pl.BlockSpec(memory_space=pl.ANY),
            scratch_shapes=[pltpu.VMEM((ROWS_TILE, d), BF16), pltpu.VMEM((SORT_CHUNK, tm), BF16),
                            pltpu.VMEM((RUN_ALIGN, d), BF16),
                            pltpu.SemaphoreType.DMA(()), pltpu.SemaphoreType.DMA(())]),
        out_shape=jax.ShapeDtypeStruct((n_rows_max, d), BF16),
        compiler_params=_cparams(("arbitrary",)),
    )(pe, po, goff, used, pad_start, pad_cnt, key, hb)


def _expert_kernel(bx_ref, be_ref, bnew_ref, breal_ref, x_ref, wg_ref, wu_ref, wd_ref, o_ref,
                   wg_sc, wu_sc, wd_sc):
    b = pl.program_id(0)

    @pl.when(bnew_ref[b] == 1)
    def _():
        wg_sc[...] = wg_ref[0, 0].astype(BF16)
        wu_sc[...] = wu_ref[0, 0].astype(BF16)
        wd_sc[...] = wd_ref[0, 0].astype(BF16)

    @pl.when(breal_ref[b] == 0)
    def _():
        o_ref[...] = jnp.zeros_like(o_ref)

    @pl.when(breal_ref[b] == 1)
    def _():
        parts = [slice(r, r + BM_CHAIN) for r in range(0, BM_EXPERT, BM_CHAIN)]
        xs = [x_ref[rs, :] for rs in parts]
        gs = [_bdot(x, wg_sc[...]) for x in xs]
        us = [_bdot(x, wu_sc[...]) for x in xs]
        acts = [(_silu(g) * u).astype(BF16) for g, u in zip(gs, us)]
        for rs, act in zip(parts, acts):
            o_ref[rs, :] = _bdot(act, wd_sc[...]).astype(BF16)


def _experts(meta, xs, layer, w_gate, w_up, w_down):
    r, d = xs.shape
    ed = w_gate.shape[-1]
    bm = BM_EXPERT
    omap = lambda b, bx, be, bnew, breal: (b, 0)
    xmap = lambda b, bx, be, bnew, breal: (bx[b], 0)
    wmap = lambda b, bx, be, bnew, breal: (layer, be[b], 0, 0)
    return pl.pallas_call(
        _expert_kernel,
        grid_spec=pltpu.PrefetchScalarGridSpec(
            num_scalar_prefetch=4, grid=(r // bm,),
            in_specs=[pl.BlockSpec((bm, d), xmap), pl.BlockSpec((1, 1, d, ed), wmap),
                      pl.BlockSpec((1, 1, d, ed), wmap), pl.BlockSpec((1, 1, ed, d), wmap)],
            out_specs=pl.BlockSpec((bm, d), omap),
            scratch_shapes=[pltpu.VMEM((d, ed), BF16), pltpu.VMEM((d, ed), BF16),
                            pltpu.VMEM((ed, d), BF16)]),
        out_shape=jax.ShapeDtypeStruct((r, d), BF16),
        compiler_params=_cparams(("arbitrary",)),
    )(*meta, xs, w_gate, w_up, w_down)


def _combine_kernel(pe_ref, po_ref, goff_ref, used_ref, key_ref, wkey_ref, ys_ref, base_ref, g_ref,
                    fn_ref, o_ref, ybuf, q_sc, sem, *, final):
    i = pl.program_id(0)
    tm = base_ref.shape[0]
    ch = SORT_CHUNK
    ppc = ch // RUN_ALIGN

    @pl.when(i == 0)
    def _():
        ybuf[...] = jnp.zeros_like(ybuf)

    copies = _piece_copies(pe_ref, po_ref, goff_ref, used_ref, i, ybuf, ys_ref, sem, False)
    copies("start")

    for ci in range(ybuf.shape[0] // ch):
        cs = slice(ci * ch, (ci + 1) * ch)

        @pl.when(ci * ch < used_ref[i])
        def _():
            rows = _piece_rows(pe_ref, po_ref, key_ref, i * PIECES_PAD + ci * ppc, ppc, wkey_ref)
            q_sc[:, cs] = jnp.concatenate(rows, axis=0).T.astype(BF16)

        @pl.when(ci * ch >= used_ref[i])
        def _():
            q_sc[:, cs] = jnp.zeros((tm, ch), BF16)

    copies("wait")
    out = base_ref[...] + g_ref[0] * _bdot(q_sc[...], ybuf[...])
    if final:
        out = _rms(out, fn_ref[...])
    o_ref[...] = out


def _combine(meta, pe, po, key, wkey, ys, base, gt, fnorm, seq, final):
    t, d = base.shape
    tm = TM_SORT
    nt = seq // tm
    return pl.pallas_call(
        functools.partial(_combine_kernel, final=final),
        grid_spec=pltpu.PrefetchScalarGridSpec(
            num_scalar_prefetch=4, grid=(t // tm,),
            in_specs=[pl.BlockSpec((N_EXPERTS, tm), lambda i, *_: (0, i)),
                      pl.BlockSpec((N_EXPERTS, tm), lambda i, *_: (0, i)),
                      pl.BlockSpec(memory_space=pl.ANY),
                      pl.BlockSpec((tm, d), lambda i, *_: (i, 0)),
                      pl.BlockSpec((1, 1, d), lambda i, *_: (i // nt, 0, 0)),
                      pl.BlockSpec((1, d), lambda i, *_: (0, 0))],
            out_specs=pl.BlockSpec((tm, d), lambda i, *_: (i, 0)),
            scratch_shapes=[pltpu.VMEM((ROWS_TILE, d), BF16), pltpu.VMEM((tm, ROWS_TILE), BF16),
                            pltpu.SemaphoreType.DMA(())]),
        out_shape=jax.ShapeDtypeStruct((t, d), F32),
        compiler_params=_cparams(("arbitrary",)),
    )(pe, po, meta[0], meta[1], key, wkey, ys, base, gt, fnorm.reshape(1, d))


def _moe(x, sh, sc, gt, nw, w_router, e_bias, layer, w_gate, w_up, w_down, wsg, wsu, wsd, fnorm, final):
    bsz, s, d = x.shape
    t = bsz * s
    n_rows_max = _rows_max(t)
    hb, base, key, wkey, cnt, pe, po = _router(x, sh, sc, gt, nw, w_router, e_bias, wsg, wsu, wsd)
    pe = pe.reshape(-1)
    po = po.reshape(-1)
    layout_meta, block_meta = _sort_meta(cnt, n_rows_max)
    xs = _dispatch(layout_meta, pe, po, key, hb, n_rows_max)
    ys = _experts(block_meta, xs, layer, w_gate, w_up, w_down)
    out = _combine(layout_meta, pe, po, key, wkey, ys, base, gt, fnorm, s, final)
    return out.reshape(bsz, s, d)


def _kv_kernel(x_ref, sh_ref, sc_ref, nw_ref, wk_ref, wvt_ref, k_ref, vt_ref, km_ref):
    x = x_ref[0]
    hb = (_rms(x, nw_ref[...]) * (1.0 + sc_ref[0]) + sh_ref[0]).astype(BF16)
    k = _bdot(hb, wk_ref[...])
    vt = _bdot_nt(wvt_ref[...], hb)
    for p in range(k.shape[1] // LANES):
        k_ref[0, p, 0] = k[:, p * LANES:(p + 1) * LANES].astype(BF16)
        vt_ref[0, p, 0] = vt[p * LANES:(p + 1) * LANES, :].astype(BF16)
    km_ref[0, 0] = jnp.mean(k, axis=0, keepdims=True)


def _kv(x, sh, sc, nw, w_k, w_v):
    bsz, s, d = x.shape
    nb = s // B_BLOCK
    npair = d // LANES
    vec = pl.BlockSpec((1, 1, d), lambda b, j: (b, 0, 0))
    full2 = lambda shape: pl.BlockSpec(shape, lambda b, j: (0, 0))
    return pl.pallas_call(
        _kv_kernel,
        grid=(bsz, nb),
        in_specs=[pl.BlockSpec((1, B_BLOCK, d), lambda b, j: (b, j, 0)), vec, vec,
                  full2((1, d)), full2((d, d)), full2((d, d))],
        out_specs=[pl.BlockSpec((1, npair, 1, B_BLOCK, LANES), lambda b, j: (b, 0, j, 0, 0)),
                   pl.BlockSpec((1, npair, 1, LANES, B_BLOCK), lambda b, j: (b, 0, j, 0, 0)),
                   pl.BlockSpec((1, 1, 1, d), lambda b, j: (b, j, 0, 0))],
        out_shape=[jax.ShapeDtypeStruct((bsz, npair, nb, B_BLOCK, LANES), BF16),
                   jax.ShapeDtypeStruct((bsz, npair, nb, LANES, B_BLOCK), BF16),
                   jax.ShapeDtypeStruct((bsz, nb, 1, d), F32)],
        compiler_params=_cparams(("arbitrary", "arbitrary")),
    )(x, sh, sc, nw.reshape(1, d), w_k.astype(BF16), w_v.T.astype(BF16))


def _attn_kernel(x_ref, sh_ref, sc_ref, g_ref, nw_ref, wqt_ref, wo_ref, k_ref, vt_ref, km_ref,
                 o_ref, qt_sc, qs_sc, acc_sc, sel_sc, m_sc, l_sc, *, nb, n_sel):
    qb = pl.program_id(1)
    x = x_ref[0]
    bq = x.shape[0]
    npair = qt_sc.shape[0]
    nbp = km_ref.shape[2]
    hd = LANES // 2
    scale = float(hd) ** -0.5
    h = _rms(x, nw_ref[...]) * (1.0 + sc_ref[0]) + sh_ref[0]
    qt = _bdot_nt(wqt_ref[...], h.astype(BF16))
    for p in range(npair):
        qt_sc[p] = qt[p * LANES:(p + 1) * LANES, :]

    subn = lax.broadcasted_iota(I32, (nbp, bq), 0)
    past = subn < qb
    krow = lax.broadcasted_iota(I32, (B_BLOCK, bq), 0)
    qcol = lax.broadcasted_iota(I32, (B_BLOCK, bq), 1)
    causal = krow <= qcol
    rowh = lax.broadcasted_iota(I32, (LANES, 1), 0)

    grp = ATTN_PAIR_UNROLL
    heads = [(u, e) for u in range(grp) for e in range(2)]

    def own_body(gi, carry):
        ps = [gi * grp + u for u in range(grp)]
        q2ts = [qt_sc[p] for p in ps]
        kms = [km_ref[0, p] for p in ps]
        kown = [k_ref[0, p, qb] for p in ps]
        vown = [vt_ref[0, p, qb] for p in ps]
        qets = [jnp.where((rowh >= hd) if e == 1 else (rowh < hd), q2ts[u], 0.0) for u, e in heads]
        qsts = [(q * scale).astype(BF16) for q in qets]
        ss = [jnp.where(causal, _bdot(kown[u], qsts[i]), NEG_INF) for i, (u, e) in enumerate(heads)]
        gates = [_dot3(kms[u], qets[i]) for i, (u, e) in enumerate(heads)]
        ms = [jnp.max(s, axis=0, keepdims=True) for s in ss]
        pes = [jnp.exp(s - m) for s, m in zip(ss, ms)]
        ls = [jnp.sum(pe, axis=0, keepdims=True) for pe in pes]
        accs = [_bdot(vown[u][e * hd:(e + 1) * hd, :], pes[i].astype(BF16))
                for i, (u, e) in enumerate(heads)]
        sels = []
        for gate in gates:
            selt = jnp.zeros((nbp, bq), F32)
            for n in range(nb):
                gn = gate[n:n + 1, :]
                beats = jnp.logical_or(gate > gn, jnp.logical_and(gate == gn, subn < n))
                beats = jnp.logical_and(beats, past)
                cnt = jnp.sum(beats.astype(F32), axis=0, keepdims=True)
                selt = jnp.where(subn == n, (cnt < n_sel).astype(F32), selt)
            sels.append(selt)
        for i, (u, e) in enumerate(heads):
            p = ps[u]
            sel_sc[p, e] = sels[i]
            qs_sc[p, e] = qsts[i]
            m_sc[p, e] = ms[i]
            l_sc[p, e] = ls[i]
            acc_sc[p, e * hd:(e + 1) * hd, :] = accs[i]
        return carry

    lax.fori_loop(0, npair // grp, own_body, 0)

    def kb_body(kb, carry):
        def group_body(gi, c2):
            ps = [gi * grp + u for u in range(grp)]
            kbl = [k_ref[0, p, kb] for p in ps]
            vbl = [vt_ref[0, p, kb] for p in ps]
            qsts = [qs_sc[ps[u], e] for u, e in heads]
            rows = [sel_sc[ps[u], e, pl.ds(kb, 1), :] for u, e in heads]
            m_old = [m_sc[ps[u], e] for u, e in heads]
            l_old = [l_sc[ps[u], e] for u, e in heads]
            a_old = [acc_sc[ps[u], e * hd:(e + 1) * hd, :] for u, e in heads]
            ss = [jnp.where(rows[i] > 0.5, _bdot(kbl[u], qsts[i]), NEG_INF)
                  for i, (u, e) in enumerate(heads)]
            m_new = [jnp.maximum(m, jnp.max(s, axis=0, keepdims=True)) for m, s in zip(m_old, ss)]
            alphas = [jnp.exp(m - mn) for m, mn in zip(m_old, m_new)]
            pes = [jnp.exp(s - mn) for s, mn in zip(ss, m_new)]
            l_new = [a * l + jnp.sum(pe, axis=0, keepdims=True) for a, l, pe in zip(alphas, l_old, pes)]
            a_new = [alphas[i] * a_old[i] + _bdot(vbl[u][e * hd:(e + 1) * hd, :], pes[i].astype(BF16))
                     for i, (u, e) in enumerate(heads)]
            for i, (u, e) in enumerate(heads):
                p = ps[u]
                m_sc[p, e] = m_new[i]
                l_sc[p, e] = l_new[i]
                acc_sc[p, e * hd:(e + 1) * hd, :] = a_new[i]
            return c2

        lax.fori_loop(0, npair // grp, group_body, 0)
        return carry

    lax.fori_loop(0, qb, kb_body, 0)

    parts = []
    for p in range(npair):
        for e in range(2):
            parts.append(acc_sc[p, e * hd:(e + 1) * hd, :] / l_sc[p, e])
    ot = jnp.concatenate(parts, axis=0)
    o_ref[0] = x + g_ref[0] * _bdot(ot.T.astype(BF16), wo_ref[...])


def _attn(x, sh, sc, gt, nw, w_q, w_o, k5, vt5, km2):
    bsz, s, d = x.shape
    nb = s // B_BLOCK
    npair = d // LANES
    nbp = km2.shape[2]
    n_sel = min(B_TOPK, nb - 1)
    vec = pl.BlockSpec((1, 1, d), lambda b, j: (b, 0, 0))
    full2 = lambda shape: pl.BlockSpec(shape, lambda b, j: (0, 0))
    return pl.pallas_call(
        functools.partial(_attn_kernel, nb=nb, n_sel=n_sel),
        grid=(bsz, nb),
        in_specs=[pl.BlockSpec((1, B_BLOCK, d), lambda b, j: (b, j, 0)), vec, vec, vec,
                  full2((1, d)), full2((d, d)), full2((d, d)),
                  pl.BlockSpec((1, npair, nb, B_BLOCK, LANES), lambda b, j: (b, 0, 0, 0, 0)),
                  pl.BlockSpec((1, npair, nb, LANES, B_BLOCK), lambda b, j: (b, 0, 0, 0, 0)),
                  pl.BlockSpec((1, npair, nbp, LANES), lambda b, j: (b, 0, 0, 0))],
        out_specs=pl.BlockSpec((1, B_BLOCK, d), lambda b, j: (b, j, 0)),
        out_shape=jax.ShapeDtypeStruct((bsz, s, d), F32),
        scratch_shapes=[pltpu.VMEM((npair, LANES, B_BLOCK), F32),
                        pltpu.VMEM((npair, 2, LANES, B_BLOCK), BF16),
                        pltpu.VMEM((npair, LANES, B_BLOCK), F32),
                        pltpu.VMEM((npair, 2, nbp, B_BLOCK), F32),
                        pltpu.VMEM((npair, 2, 1, B_BLOCK), F32),
                        pltpu.VMEM((npair, 2, 1, B_BLOCK), F32)],
        compiler_params=_cparams(("arbitrary", "arbitrary")),
    )(x, sh, sc, gt, nw.reshape(1, d), w_q.T.astype(BF16), w_o.astype(BF16), k5, vt5, km2)


def kernel(x, c, ada_w, ada_b, norm_mix, norm_ffn, a_w_in, a_b_in, a_ln_g, a_ln_b, a_w_s, a_b_s,
           a_w_out, kv_norm, kv_ada_w, kv_ada_b, kv_w_k, kv_w_v, b_w_q, b_w_o, moe_router, moe_bias,
           moe_w_gate, moe_w_up, moe_w_down, sh_w_gate, sh_w_up, sh_w_down, final_norm):
    bsz, s, d = x.shape
    depth = ada_w.shape[0]
    n_a = a_w_in.shape[0]
    assert s % B_BLOCK == 0 and s % TM_GMLP == 0 and d % LANES == 0
    nb = s // B_BLOCK
    npair = d // LANES
    nbp = -(-nb // 8) * 8

    def split(m, n):
        return [m[:, i * d:(i + 1) * d].reshape(bsz, 1, d) for i in range(n)]

    layer_mods = _ada(c, ada_w, ada_b)
    k5 = vt5 = km2 = None
    for i in range(depth):
        sh1, sc1, g1, sh2, sc2, g2 = split(layer_mods[i], 6)
        if i < n_a:
            x = _gmlp(x, sh1, sc1, g1, norm_mix[i], a_w_in[i], a_b_in[i], a_ln_g[i], a_ln_b[i],
                      a_w_s[i], a_b_s[i], a_w_out[i])
        else:
            if k5 is None:
                ksh, ksc = split(_ada(c, kv_ada_w[None], kv_ada_b[None])[0], 2)
                k5, vt5, km = _kv(x, ksh, ksc, kv_norm, kv_w_k, kv_w_v)
                km = km.reshape(bsz, nb, npair, LANES).transpose(0, 2, 1, 3)
                km2 = jnp.pad(km, ((0, 0), (0, 0), (0, nbp - nb), (0, 0)))
            j = i - n_a
            x = _attn(x, sh1, sc1, g1, norm_mix[i], b_w_q[j], b_w_o[j], k5, vt5, km2)
        x = _moe(x, sh2, sc2, g2, norm_ffn[i], moe_router[i], moe_bias[i], i, moe_w_gate,
                 moe_w_up, moe_w_down, sh_w_gate[i], sh_w_up[i], sh_w_down[i],
                 final_norm, i == depth - 1)
    return x
```

```python
import functools

import jax
import jax.numpy as jnp
from jax import lax
from jax.experimental import pallas as pl
from jax.experimental.pallas import tpu as pltpu

F32 = jnp.float32
BF16 = jnp.bfloat16
I32 = jnp.int32

RMS_EPS = 1e-6
LN_EPS = 1e-5
NEG_INF = -1e30

A_CHUNK = 128
A_GROUPS = 8
B_HEADS = 16
B_BLOCK = 256
B_TOPK = 3
N_EXPERTS = 64
TOP_K = 8
N_GROUPS = 8
TOPK_GROUPS = 4
ROUTED_SCALE = 2.5

LANES = 128
VMEM_LIMIT = 56 * 1024 * 1024

TM_GMLP = 256
TM_SORT = 512
RUN_ALIGN = 16
SORT_CHUNK = 256
ROWS_TILE_USED = TM_SORT * TOP_K + N_EXPERTS * (RUN_ALIGN - 1)
ROWS_TILE = -(-ROWS_TILE_USED // SORT_CHUNK) * SORT_CHUNK
PIECES_TILE = ROWS_TILE // RUN_ALIGN
PIECES_PAD = -(-PIECES_TILE // LANES) * LANES
BM_EXPERT = 512
BM_CHAIN = 256
ATTN_PAIR_UNROLL = 4
SUM_ROWS = 16


def _cparams(sem):
    return pltpu.CompilerParams(dimension_semantics=sem, vmem_limit_bytes=VMEM_LIMIT)


def _sigmoid(x):
    return 1.0 / (1.0 + jnp.exp(-x))


def _silu(x):
    return x * _sigmoid(x)


def _gelu_tanh(x):
    return 0.5 * x * (1.0 + jnp.tanh(0.7978845608028654 * (x + 0.044715 * (x * x * x))))


def _rms(x, g):
    return x * lax.rsqrt(jnp.mean(x * x, axis=-1, keepdims=True) + RMS_EPS) * g


def _bdot(a, b):
    return jnp.dot(a, b, preferred_element_type=F32)


def _bdot_nt(a, b):
    return lax.dot_general(a, b, (((1,), (1,)), ((), ())), preferred_element_type=F32)


def _split(a):
    hi = a.astype(BF16)
    lo = (a - hi.astype(F32)).astype(BF16)
    return hi, lo


def _dot3_nt(a, b):
    ah, al = _split(a)
    bh, bl = _split(b)
    return _bdot_nt(ah, bh) + (_bdot_nt(ah, bl) + _bdot_nt(al, bh))


def _dot3(a, b):
    ah, al = _split(a)
    bh, bl = _split(b)
    return _bdot(ah, bh) + (_bdot(ah, bl) + _bdot(al, bh))


def _ada_kernel(c_ref, w_ref, b_ref, o_ref):
    a = _silu(c_ref[...]).astype(BF16)
    o_ref[0] = _bdot(a, w_ref[0].astype(BF16)) + b_ref[0]


def _ada(c, w, b):
    bsz, d = c.shape
    nl, _, n = w.shape
    tn = 1024
    return pl.pallas_call(
        _ada_kernel,
        grid=(nl, n // tn),
        in_specs=[pl.BlockSpec((bsz, d), lambda l, j: (0, 0)),
                  pl.BlockSpec((1, d, tn), lambda l, j: (l, 0, j)),
                  pl.BlockSpec((1, 1, tn), lambda l, j: (l, 0, j))],
        out_specs=pl.BlockSpec((1, bsz, tn), lambda l, j: (l, 0, j)),
        out_shape=jax.ShapeDtypeStruct((nl, bsz, n), F32),
        compiler_params=_cparams(("arbitrary", "arbitrary")),
    )(c, w, b.reshape(nl, 1, n))


def _gmlp_kernel(x_ref, sh_ref, sc_ref, g_ref, nw_ref, win_ref, bin_ref, lng_ref, lnb_ref,
                 ws_ref, bst_ref, wout_ref, o_ref, y_sc):
    x = x_ref[0]
    tm = x.shape[0]
    h = _rms(x, nw_ref[...]) * (1.0 + sc_ref[0]) + sh_ref[0]
    z = _gelu_tanh(_bdot(h.astype(BF16), win_ref[...]) + bin_ref[...])
    aw = z.shape[1] // 2
    gd = aw // A_GROUPS
    u = z[:, :aw]
    v = z[:, aw:]
    mu = jnp.mean(v, axis=-1, keepdims=True)
    dv = v - mu
    var = jnp.mean(dv * dv, axis=-1, keepdims=True)
    vn = (dv * lax.rsqrt(var + LN_EPS) * lng_ref[...] + lnb_ref[...]).astype(BF16)
    row = lax.broadcasted_iota(I32, (A_CHUNK, A_CHUNK), 0)
    col = lax.broadcasted_iota(I32, (A_CHUNK, A_CHUNK), 1)
    causal = col <= row
    for g in range(A_GROUPS):
        wg = jnp.where(causal, ws_ref[g], 0.0).astype(BF16)
        bcol = bst_ref[:, g:g + 1]
        for ci in range(tm // A_CHUNK):
            rs = slice(ci * A_CHUNK, (ci + 1) * A_CHUNK)
            cs = slice(g * gd, (g + 1) * gd)
            sv = _bdot(wg, vn[rs, cs]) + bcol
            y_sc[rs, cs] = (u[rs, cs] * sv).astype(BF16)
    o_ref[0] = x + g_ref[0] * _bdot(y_sc[...], wout_ref[...])


def _gmlp(x, sh, sc, gt, nw, w_in, b_in, ln_g, ln_b, w_s, b_s, w_out):
    bsz, s, d = x.shape
    tm = TM_GMLP
    n_in = w_in.shape[1]
    aw = n_in // 2
    vec = pl.BlockSpec((1, 1, d), lambda b, j: (b, 0, 0))
    full2 = lambda shape: pl.BlockSpec(shape, lambda b, j: (0, 0))
    return pl.pallas_call(
        _gmlp_kernel,
        grid=(bsz, s // tm),
        in_specs=[pl.BlockSpec((1, tm, d), lambda b, j: (b, j, 0)), vec, vec, vec,
                  full2((1, d)), full2((d, n_in)), full2((1, n_in)), full2((1, aw)), full2((1, aw)),
                  pl.BlockSpec((A_GROUPS, A_CHUNK, A_CHUNK), lambda b, j: (0, 0, 0)),
                  full2((A_CHUNK, A_GROUPS)), full2((aw, d))],
        out_specs=pl.BlockSpec((1, tm, d), lambda b, j: (b, j, 0)),
        out_shape=jax.ShapeDtypeStruct((bsz, s, d), F32),
        scratch_shapes=[pltpu.VMEM((tm, aw), BF16)],
        compiler_params=_cparams(("arbitrary", "arbitrary")),
    )(x, sh, sc, gt, nw.reshape(1, d), w_in.astype(BF16), b_in.reshape(1, n_in),
      ln_g.reshape(1, aw), ln_b.reshape(1, aw), w_s, b_s.T, w_out.astype(BF16))


def _router_kernel(x_ref, sh_ref, sc_ref, g_ref, nw_ref, wrt_ref, bias_ref, wsg_ref, wsu_ref, wsd_ref,
                   h_ref, base_ref, key_ref, wkey_ref, cnt_ref, pe_ref, po_ref):
    x = x_ref[0]
    tm = x.shape[0]
    h = _rms(x, nw_ref[...]) * (1.0 + sc_ref[0]) + sh_ref[0]
    hb = h.astype(BF16)
    h_ref[...] = hb
    act = (_silu(_bdot(hb, wsg_ref[...])) * _bdot(hb, wsu_ref[...])).astype(BF16)
    base_ref[...] = x + g_ref[0] * _bdot(act, wsd_ref[...])

    scores = _sigmoid(_dot3_nt(wrt_ref[...], h))
    choice = scores + bias_ref[...]
    gsz = N_EXPERTS // N_GROUPS
    sub = lax.broadcasted_iota(I32, (gsz, tm), 0)
    blocks = [choice[g * gsz:(g + 1) * gsz] for g in range(N_GROUPS)]
    gscore = []
    for blk in blocks:
        m1 = jnp.max(blk, axis=0, keepdims=True)
        i1 = jnp.min(jnp.where(blk == m1, sub, gsz), axis=0, keepdims=True)
        m2 = jnp.max(jnp.where(sub == i1, -jnp.inf, blk), axis=0, keepdims=True)
        gscore.append(m1 + m2)
    masked = []
    for g in range(N_GROUPS):
        beats = jnp.zeros((1, tm), F32)
        for m in range(N_GROUPS):
            if m == g:
                continue
            b = gscore[m] > gscore[g]
            if m < g:
                b = jnp.logical_or(b, gscore[m] == gscore[g])
            beats = beats + b.astype(F32)
        masked.append(jnp.where(beats < TOPK_GROUPS, blocks[g], NEG_INF))
    cur = jnp.concatenate(masked, axis=0)
    eio = lax.broadcasted_iota(I32, (N_EXPERTS, tm), 0)
    chosen = jnp.zeros((N_EXPERTS, tm), jnp.bool_)
    wsum = jnp.zeros((1, tm), F32)
    for _ in range(TOP_K):
        m = jnp.max(cur, axis=0, keepdims=True)
        idx = jnp.min(jnp.where(cur == m, eio, N_EXPERTS), axis=0, keepdims=True)
        sel = eio == idx
        chosen = jnp.logical_or(chosen, sel)
        wsum = wsum + jnp.sum(jnp.where(sel, scores, 0.0), axis=0, keepdims=True)
        cur = jnp.where(sel, -jnp.inf, cur)
    wkey_ref[...] = jnp.where(chosen, scores / wsum * ROUTED_SCALE, 0.0)

    onehot = chosen.astype(BF16)
    r_i = lax.broadcasted_iota(I32, (tm, tm), 0)
    c_i = lax.broadcasted_iota(I32, (tm, tm), 1)
    before = (r_i < c_i).astype(BF16)
    prior = _bdot(onehot, before)
    key_ref[...] = jnp.where(chosen, prior, -1.0).astype(I32)
    cnt = jnp.sum(chosen.astype(F32), axis=1, keepdims=True)
    cnt_ref[0] = cnt

    run_p = jnp.floor((cnt + (RUN_ALIGN - 1)) / RUN_ALIGN)
    ppc = SORT_CHUNK // RUN_ALIGN
    tot = jnp.sum(run_p, axis=0, keepdims=True)
    fill = jnp.ceil(tot / ppc) * ppc - tot
    run_p = run_p + jnp.where(lax.broadcasted_iota(I32, (N_EXPERTS, 1), 0) == N_EXPERTS - 1, fill, 0.0)
    e_r = lax.broadcasted_iota(I32, (N_EXPERTS, N_EXPERTS), 0)
    e_c = lax.broadcasted_iota(I32, (N_EXPERTS, N_EXPERTS), 1)
    incl = (e_c <= e_r).astype(BF16)
    lend = _bdot(incl, jnp.broadcast_to(run_p, (N_EXPERTS, LANES)).astype(BF16))[:, 0:1]
    loff = lend - run_p
    pj = lax.broadcasted_iota(I32, (N_EXPERTS, PIECES_PAD), 1).astype(F32)
    er = lax.broadcasted_iota(I32, (N_EXPERTS, PIECES_PAD), 0).astype(F32)
    pe = jnp.minimum(jnp.sum((lend <= pj).astype(F32), axis=0, keepdims=True), N_EXPERTS - 1.0)
    lo = jnp.sum(jnp.where(er == pe, loff, 0.0), axis=0, keepdims=True)
    pe_ref[0] = pe.astype(I32)
    po_ref[0] = ((pj[0:1, :] - lo) * RUN_ALIGN).astype(I32)


def _router(x, sh, sc, gt, nw, w_router, e_bias, wsg, wsu, wsd):
    bsz, s, d = x.shape
    t = bsz * s
    tm = TM_SORT
    nt = s // tm
    sd = wsg.shape[1]
    vec = pl.BlockSpec((1, 1, d), lambda b, j: (b, 0, 0))
    full2 = lambda shape: pl.BlockSpec(shape, lambda b, j: (0, 0))
    tok = pl.BlockSpec((tm, d), lambda b, j: (b * nt + j, 0))
    etok = pl.BlockSpec((N_EXPERTS, tm), lambda b, j: (0, b * nt + j))
    ptab = pl.BlockSpec((1, 1, PIECES_PAD), lambda b, j: (b * nt + j, 0, 0))
    return pl.pallas_call(
        _router_kernel,
        grid=(bsz, nt),
        in_specs=[pl.BlockSpec((1, tm, d), lambda b, j: (b, j, 0)), vec, vec, vec,
                  full2((1, d)), full2((N_EXPERTS, d)), full2((N_EXPERTS, 1)),
                  full2((d, sd)), full2((d, sd)), full2((sd, d))],
        out_specs=[tok, tok, etok, etok,
                   pl.BlockSpec((1, N_EXPERTS, 1), lambda b, j: (b * nt + j, 0, 0)), ptab, ptab],
        out_shape=[jax.ShapeDtypeStruct((t, d), BF16), jax.ShapeDtypeStruct((t, d), F32),
                   jax.ShapeDtypeStruct((N_EXPERTS, t), I32), jax.ShapeDtypeStruct((N_EXPERTS, t), F32),
                   jax.ShapeDtypeStruct((t // tm, N_EXPERTS, 1), F32),
                   jax.ShapeDtypeStruct((t // tm, 1, PIECES_PAD), I32),
                   jax.ShapeDtypeStruct((t // tm, 1, PIECES_PAD), I32)],
        compiler_params=_cparams(("arbitrary", "arbitrary")),
    )(x, sh, sc, gt, nw.reshape(1, d), w_router.T, e_bias.reshape(N_EXPERTS, 1),
      wsg.astype(BF16), wsu.astype(BF16), wsd.astype(BF16))


def _rows_max(t):
    rows = (t // TM_SORT) * ROWS_TILE + N_EXPERTS * (BM_EXPERT - RUN_ALIGN)
    return -(-rows // BM_EXPERT) * BM_EXPERT


def _sort_meta(cnt, n_rows_max):
    nts = cnt.shape[0]
    c = cnt.reshape(nts, N_EXPERTS).astype(I32)
    run = (c + (RUN_ALIGN - 1)) // RUN_ALIGN * RUN_ALIGN
    fill = (-jnp.sum(run, axis=1, keepdims=True)) % SORT_CHUNK
    run = jnp.concatenate([run[:, :-1], run[:, -1:] + fill], axis=1)
    used = jnp.sum(run, axis=1)
    per_e = jnp.sum(run, axis=0)
    seg = (per_e + (BM_EXPERT - 1)) // BM_EXPERT * BM_EXPERT
    ends = jnp.cumsum(seg)
    starts = ends - seg
    goff = starts[None, :] + jnp.cumsum(run, axis=0) - run
    pad_start = starts + per_e
    pad_cnt = (seg - per_e) // RUN_ALIGN

    nb = n_rows_max // BM_EXPERT
    blo = jnp.arange(nb, dtype=I32) * BM_EXPERT
    be = jnp.sum((ends[None, :] <= blo[:, None]).astype(I32), axis=1)
    breal = (be < N_EXPERTS).astype(I32)
    be = jnp.minimum(be, N_EXPERTS - 1).astype(I32)
    prev = jnp.concatenate([jnp.full((1,), -1, I32), be[:-1]])
    bnew = (be != prev).astype(I32)
    bx = jnp.where(breal == 1, jnp.arange(nb, dtype=I32), 0)
    i32 = lambda a: a.reshape(-1).astype(I32)
    return (i32(goff), i32(used), i32(pad_start), i32(pad_cnt)), (bx, be, bnew, breal)


def _piece_copies(pe_ref, po_ref, goff_ref, used_ref, tile, local, remote, sem, to_remote):
    def copy(j):
        idx = tile * PIECES_PAD + j
        l0 = pl.multiple_of(j * RUN_ALIGN, RUN_ALIGN)
        g0 = pl.multiple_of(goff_ref[tile * N_EXPERTS + pe_ref[idx]] + po_ref[idx], RUN_ALIGN)
        lref = local.at[pl.ds(l0, RUN_ALIGN)]
        gref = remote.at[pl.ds(g0, RUN_ALIGN)]
        return pltpu.make_async_copy(lref, gref, sem) if to_remote else pltpu.make_async_copy(gref, lref, sem)

    def wait_all():
        def body(j, carry):
            copy(j).wait()
            return carry

        lax.fori_loop(0, used_ref[tile] // RUN_ALIGN, body, 0)

    return copy, wait_all


def _pad_copies(pad_start_ref, pad_cnt_ref, tile, ntiles, zeros, remote, sem):
    share = -(-N_EXPERTS // ntiles)

    def apply(act):
        for q in range(share):
            e = tile * share + q
            ec = jnp.minimum(e, N_EXPERTS - 1)
            n = jnp.where(e < N_EXPERTS, pad_cnt_ref[ec], 0)
            base = pad_start_ref[ec]

            def body(j, carry):
                g0 = pl.multiple_of(base + j * RUN_ALIGN, RUN_ALIGN)
                getattr(pltpu.make_async_copy(zeros, remote.at[pl.ds(g0, RUN_ALIGN)], sem), act)()
                return carry

            lax.fori_loop(0, n, body, 0)

    return apply


def _piece_rows(pe_ref, po_ref, key_ref, first_piece, npieces, val_ref=None):
    tm = key_ref.shape[1]
    sub = lax.broadcasted_iota(I32, (RUN_ALIGN, tm), 0)
    out = []
    for jj in range(npieces):
        j = first_piece + jj
        e = pe_ref[j]
        hit = (key_ref[pl.ds(e, 1), :] - po_ref[j]) == sub
        val = 1.0 if val_ref is None else val_ref[pl.ds(e, 1), :]
        out.append(jnp.where(hit, val, 0.0))
    return out


def _dispatch_kernel(pe_ref, po_ref, goff_ref, used_ref, pad_start_ref, pad_cnt_ref,
                     key_ref, h_ref, xs_ref, xbuf, p_sc, zbuf, sem, zsem, *, ntiles):
    i = pl.program_id(0)
    hb = h_ref[...]
    ch = SORT_CHUNK
    ppc = ch // RUN_ALIGN
    zbuf[...] = jnp.zeros_like(zbuf)
    pads = _pad_copies(pad_start_ref, pad_cnt_ref, i, ntiles, zbuf, xs_ref, zsem)
    pads("start")

    copy, wait_all = _piece_copies(pe_ref, po_ref, goff_ref, used_ref, i, xbuf, xs_ref, sem, True)

    def build(ci):
        rows = _piece_rows(pe_ref, po_ref, key_ref, i * PIECES_PAD + ci * ppc, ppc)
        for jj, p in enumerate(rows):
            p_sc[jj * RUN_ALIGN:(jj + 1) * RUN_ALIGN, :] = p.astype(BF16)
        xbuf[pl.ds(pl.multiple_of(ci * ch, ch), ch), :] = _bdot(p_sc[...], hb).astype(BF16)

    def send(ci):
        for jj in range(ppc):
            copy(ci * ppc + jj).start()

    def step(ci, carry):
        send(ci - 1)
        build(ci)
        return carry

    nchunk = used_ref[i] // ch
    build(0)
    lax.fori_loop(1, nchunk, step, 0)
    send(nchunk - 1)
    pads("wait")
    wait_all()


def _dispatch(meta, pe, po, key, hb, n_rows_max):
    t, d = hb.shape
    tm = TM_SORT
    goff, used, pad_start, pad_cnt = meta
    return pl.pallas_call(
        functools.partial(_dispatch_kernel, ntiles=t // tm),
        grid_spec=pltpu.PrefetchScalarGridSpec(
            num_scalar_prefetch=6, grid=(t // tm,),
            in_specs=[pl.BlockSpec((N_EXPERTS, tm), lambda i, *_: (0, i)),
                      pl.BlockSpec((tm, d), lambda i, *_: (i, 0))],
            out_specs=pl.BlockSpec(memory_space=pl.ANY),
            scratch_shapes=[pltpu.VMEM((ROWS_TILE, d), BF16), pltpu.VMEM((SORT_CHUNK, tm), BF16),
                            pltpu.VMEM((RUN_ALIGN, d), BF16),
                            pltpu.SemaphoreType.DMA(()), pltpu.SemaphoreType.DMA(())]),
        out_shape=jax.ShapeDtypeStruct((n_rows_max, d), BF16),
        compiler_params=_cparams(("arbitrary",)),
    )(pe, po, goff, used, pad_start, pad_cnt, key, hb)


def _expert_kernel(bx_ref, be_ref, bnew_ref, breal_ref, x_ref, wg_ref, wu_ref, wd_ref, o_ref,
                   wg_sc, wu_sc, wd_sc):
    b = pl.program_id(0)

    @pl.when(bnew_ref[b] == 1)
    def _():
        wg_sc[...] = wg_ref[0, 0].astype(BF16)
        wu_sc[...] = wu_ref[0, 0].astype(BF16)
        wd_sc[...] = wd_ref[0, 0].astype(BF16)

    @pl.when(breal_ref[b] == 0)
    def _():
        o_ref[...] = jnp.zeros_like(o_ref)

    @pl.when(breal_ref[b] == 1)
    def _():
        parts = [slice(r, r + BM_CHAIN) for r in range(0, BM_EXPERT, BM_CHAIN)]
        xs = [x_ref[rs, :] for rs in parts]
        gs = [_bdot(x, wg_sc[...]) for x in xs]
        us = [_bdot(x, wu_sc[...]) for x in xs]
        acts = [(_silu(g) * u).astype(BF16) for g, u in zip(gs, us)]
        for rs, act in zip(parts, acts):
            o_ref[rs, :] = _bdot(act, wd_sc[...]).astype(BF16)


def _experts(meta, xs, layer, w_gate, w_up, w_down):
    r, d = xs.shape
    ed = w_gate.shape[-1]
    bm = BM_EXPERT
    omap = lambda b, bx, be, bnew, breal: (b, 0)
    xmap = lambda b, bx, be, bnew, breal: (bx[b], 0)
    wmap = lambda b, bx, be, bnew, breal: (layer, be[b], 0, 0)
    return pl.pallas_call(
        _expert_kernel,
        grid_spec=pltpu.PrefetchScalarGridSpec(
            num_scalar_prefetch=4, grid=(r // bm,),
            in_specs=[pl.BlockSpec((bm, d), xmap), pl.BlockSpec((1, 1, d, ed), wmap),
                      pl.BlockSpec((1, 1, d, ed), wmap), pl.BlockSpec((1, 1, ed, d), wmap)],
            out_specs=pl.BlockSpec((bm, d), omap),
            scratch_shapes=[pltpu.VMEM((d, ed), BF16), pltpu.VMEM((d, ed), BF16),
                            pltpu.VMEM((ed, d), BF16)]),
        out_shape=jax.ShapeDtypeStruct((r, d), BF16),
        compiler_params=_cparams(("arbitrary",)),
    )(*meta, xs, w_gate, w_up, w_down)


def _combine_kernel(pe_ref, po_ref, goff_ref, used_ref, key_ref, wkey_ref, ys_ref, base_ref, g_ref,
                    fn_ref, o_ref, ybuf, q_sc, sem, *, final):
    i = pl.program_id(0)
    tm = base_ref.shape[0]
    ch = SORT_CHUNK
    ppc = ch // RUN_ALIGN

    @pl.when(i == 0)
    def _():
        ybuf[...] = jnp.zeros_like(ybuf)

    copy, wait_all = _piece_copies(pe_ref, po_ref, goff_ref, used_ref, i, ybuf, ys_ref, sem, False)

    for ci in range(ybuf.shape[0] // ch):
        cs = slice(ci * ch, (ci + 1) * ch)

        @pl.when(ci * ch < used_ref[i])
        def _():
            for jj in range(ppc):
                copy(ci * ppc + jj).start()
            rows = _piece_rows(pe_ref, po_ref, key_ref, i * PIECES_PAD + ci * ppc, ppc, wkey_ref)
            q_sc[:, cs] = jnp.concatenate(rows, axis=0).T.astype(BF16)

        @pl.when(ci * ch >= used_ref[i])
        def _():
            q_sc[:, cs] = jnp.zeros((tm, ch), BF16)

    wait_all()
    out = base_ref[...] + g_ref[0] * _bdot(q_sc[...], ybuf[...])
    if final:
        out = _rms(out, fn_ref[...])
    o_ref[...] = out


def _combine(meta, pe, po, key, wkey, ys, base, gt, fnorm, seq, final):
    t, d = base.shape
    tm = TM_SORT
    nt = seq // tm
    return pl.pallas_call(
        functools.partial(_combine_kernel, final=final),
        grid_spec=pltpu.PrefetchScalarGridSpec(
            num_scalar_prefetch=4, grid=(t // tm,),
            in_specs=[pl.BlockSpec((N_EXPERTS, tm), lambda i, *_: (0, i)),
                      pl.BlockSpec((N_EXPERTS, tm), lambda i, *_: (0, i)),
                      pl.BlockSpec(memory_space=pl.ANY),
                      pl.BlockSpec((tm, d), lambda i, *_: (i, 0)),
                      pl.BlockSpec((1, 1, d), lambda i, *_: (i // nt, 0, 0)),
                      pl.BlockSpec((1, d), lambda i, *_: (0, 0))],
            out_specs=pl.BlockSpec((tm, d), lambda i, *_: (i, 0)),
            scratch_shapes=[pltpu.VMEM((ROWS_TILE, d), BF16), pltpu.VMEM((tm, ROWS_TILE), BF16),
                            pltpu.SemaphoreType.DMA(())]),
        out_shape=jax.ShapeDtypeStruct((t, d), F32),
        compiler_params=_cparams(("arbitrary",)),
    )(pe, po, meta[0], meta[1], key, wkey, ys, base, gt, fnorm.reshape(1, d))


def _moe(x, sh, sc, gt, nw, w_router, e_bias, layer, w_gate, w_up, w_down, wsg, wsu, wsd, fnorm, final):
    bsz, s, d = x.shape
    t = bsz * s
    n_rows_max = _rows_max(t)
    hb, base, key, wkey, cnt, pe, po = _router(x, sh, sc, gt, nw, w_router, e_bias, wsg, wsu, wsd)
    pe = pe.reshape(-1)
    po = po.reshape(-1)
    layout_meta, block_meta = _sort_meta(cnt, n_rows_max)
    xs = _dispatch(layout_meta, pe, po, key, hb, n_rows_max)
    ys = _experts(block_meta, xs, layer, w_gate, w_up, w_down)
    out = _combine(layout_meta, pe, po, key, wkey, ys, base, gt, fnorm, s, final)
    return out.reshape(bsz, s, d)


def _kv_kernel(x_ref, sh_ref, sc_ref, nw_ref, wk_ref, wvt_ref, k_ref, vt_ref, km_ref):
    x = x_ref[0]
    hb = (_rms(x, nw_ref[...]) * (1.0 + sc_ref[0]) + sh_ref[0]).astype(BF16)
    k = _bdot(hb, wk_ref[...])
    vt = _bdot_nt(wvt_ref[...], hb)
    for p in range(k.shape[1] // LANES):
        k_ref[0, p, 0] = k[:, p * LANES:(p + 1) * LANES].astype(BF16)
        vt_ref[0, p, 0] = vt[p * LANES:(p + 1) * LANES, :].astype(BF16)
    km_ref[0, 0] = jnp.mean(k, axis=0, keepdims=True)


def _kv(x, sh, sc, nw, w_k, w_v):
    bsz, s, d = x.shape
    nb = s // B_BLOCK
    npair = d // LANES
    vec = pl.BlockSpec((1, 1, d), lambda b, j: (b, 0, 0))
    full2 = lambda shape: pl.BlockSpec(shape, lambda b, j: (0, 0))
    return pl.pallas_call(
        _kv_kernel,
        grid=(bsz, nb),
        in_specs=[pl.BlockSpec((1, B_BLOCK, d), lambda b, j: (b, j, 0)), vec, vec,
                  full2((1, d)), full2((d, d)), full2((d, d))],
        out_specs=[pl.BlockSpec((1, npair, 1, B_BLOCK, LANES), lambda b, j: (b, 0, j, 0, 0)),
                   pl.BlockSpec((1, npair, 1, LANES, B_BLOCK), lambda b, j: (b, 0, j, 0, 0)),
                   pl.BlockSpec((1, 1, 1, d), lambda b, j: (b, j, 0, 0))],
        out_shape=[jax.ShapeDtypeStruct((bsz, npair, nb, B_BLOCK, LANES), BF16),
                   jax.ShapeDtypeStruct((bsz, npair, nb, LANES, B_BLOCK), BF16),
                   jax.ShapeDtypeStruct((bsz, nb, 1, d), F32)],
        compiler_params=_cparams(("arbitrary", "arbitrary")),
    )(x, sh, sc, nw.reshape(1, d), w_k.astype(BF16), w_v.T.astype(BF16))


def _attn_kernel(x_ref, sh_ref, sc_ref, g_ref, nw_ref, wqt_ref, wo_ref, k_ref, vt_ref, km_ref,
                 o_ref, qt_sc, qs_sc, acc_sc, sel_sc, m_sc, l_sc, *, nb, n_sel):
    qb = pl.program_id(1)
    x = x_ref[0]
    bq = x.shape[0]
    npair = qt_sc.shape[0]
    nbp = km_ref.shape[2]
    hd = LANES // 2
    scale = float(hd) ** -0.5 * 1.4426950408889634
    h = _rms(x, nw_ref[...]) * (1.0 + sc_ref[0]) + sh_ref[0]
    qt = _bdot_nt(wqt_ref[...], h.astype(BF16))
    for p in range(npair):
        qt_sc[p] = qt[p * LANES:(p + 1) * LANES, :]

    subn = lax.broadcasted_iota(I32, (nbp, bq), 0)
    past = subn < qb
    krow = lax.broadcasted_iota(I32, (B_BLOCK, bq), 0)
    qcol = lax.broadcasted_iota(I32, (B_BLOCK, bq), 1)
    causal = krow <= qcol
    rowh = lax.broadcasted_iota(I32, (LANES, 1), 0)

    grp = ATTN_PAIR_UNROLL
    heads = [(u, e) for u in range(grp) for e in range(2)]
    ones_rows = jnp.ones((SUM_ROWS, B_BLOCK), BF16)

    def vsum(vt2, e):
        return jnp.concatenate([vt2[e * hd:(e + 1) * hd, :], ones_rows], axis=0)

    def own_body(gi, carry):
        ps = [gi * grp + u for u in range(grp)]
        q2ts = [qt_sc[p] for p in ps]
        kms = [km_ref[0, p] for p in ps]
        kown = [k_ref[0, p, qb] for p in ps]
        vown = [vt_ref[0, p, qb] for p in ps]
        qets = [jnp.where((rowh >= hd) if e == 1 else (rowh < hd), q2ts[u], 0.0) for u, e in heads]
        qsts = [(q * scale).astype(BF16) for q in qets]
        ss = [jnp.where(causal, _bdot(kown[u], qsts[i]), NEG_INF) for i, (u, e) in enumerate(heads)]
        gates = [_dot3(kms[u], qets[i]) for i, (u, e) in enumerate(heads)]
        ms = [jnp.max(s, axis=0, keepdims=True) for s in ss]
        pes = [jnp.exp2(s - m) for s, m in zip(ss, ms)]
        pvs = [_bdot(vsum(vown[u], e), pes[i].astype(BF16)) for i, (u, e) in enumerate(heads)]
        accs = [pv[:hd] for pv in pvs]
        ls = [pv[hd:hd + 1] for pv in pvs]
        sels = []
        for gate in gates:
            selt = jnp.zeros((nbp, bq), F32)
            for n in range(nb):
                gn = gate[n:n + 1, :]
                beats = jnp.logical_or(gate > gn, jnp.logical_and(gate == gn, subn < n))
                beats = jnp.logical_and(beats, past)
                cnt = jnp.sum(beats.astype(F32), axis=0, keepdims=True)
                selt = jnp.where(subn == n, (cnt < n_sel).astype(F32), selt)
            sels.append(selt)
        for i, (u, e) in enumerate(heads):
            p = ps[u]
            sel_sc[p, e] = sels[i]
            qs_sc[p, e] = qsts[i]
            m_sc[p, e] = ms[i]
            l_sc[p, e] = ls[i]
            acc_sc[p, e * hd:(e + 1) * hd, :] = accs[i]
        return carry

    lax.fori_loop(0, npair // grp, own_body, 0)

    def kb_body(kb, carry):
        def group_body(gi, c2):
            ps = [gi * grp + u for u in range(grp)]
            kbl = [k_ref[0, p, kb] for p in ps]
            vbl = [vt_ref[0, p, kb] for p in ps]
            qsts = [qs_sc[ps[u], e] for u, e in heads]
            rows = [sel_sc[ps[u], e, pl.ds(kb, 1), :] for u, e in heads]
            m_old = [m_sc[ps[u], e] for u, e in heads]
            l_old = [l_sc[ps[u], e] for u, e in heads]
            a_old = [acc_sc[ps[u], e * hd:(e + 1) * hd, :] for u, e in heads]
            ss = [jnp.where(rows[i] > 0.5, _bdot(kbl[u], qsts[i]), NEG_INF)
                  for i, (u, e) in enumerate(heads)]
            m_new = [jnp.maximum(m, jnp.max(s, axis=0, keepdims=True)) for m, s in zip(m_old, ss)]
            alphas = [jnp.exp2(m - mn) for m, mn in zip(m_old, m_new)]
            pes = [jnp.exp2(s - mn) for s, mn in zip(ss, m_new)]
            pvs = [_bdot(vsum(vbl[u], e), pes[i].astype(BF16)) for i, (u, e) in enumerate(heads)]
            l_new = [a * l + pv[hd:hd + 1] for a, l, pv in zip(alphas, l_old, pvs)]
            a_new = [a * ao + pv[:hd] for a, ao, pv in zip(alphas, a_old, pvs)]
            for i, (u, e) in enumerate(heads):
                p = ps[u]
                m_sc[p, e] = m_new[i]
                l_sc[p, e] = l_new[i]
                acc_sc[p, e * hd:(e + 1) * hd, :] = a_new[i]
            return c2

        lax.fori_loop(0, npair // grp, group_body, 0)
        return carry

    lax.fori_loop(0, qb, kb_body, 0)

    parts = []
    for p in range(npair):
        for e in range(2):
            parts.append(acc_sc[p, e * hd:(e + 1) * hd, :] / l_sc[p, e])
    ot = jnp.concatenate(parts, axis=0)
    o_ref[0] = x + g_ref[0] * _bdot(ot.T.astype(BF16), wo_ref[...])


def _attn(x, sh, sc, gt, nw, w_q, w_o, k5, vt5, km2):
    bsz, s, d = x.shape
    nb = s // B_BLOCK
    npair = d // LANES
    nbp = km2.shape[2]
    n_sel = min(B_TOPK, nb - 1)
    vec = pl.BlockSpec((1, 1, d), lambda b, j: (b, 0, 0))
    full2 = lambda shape: pl.BlockSpec(shape, lambda b, j: (0, 0))
    return pl.pallas_call(
        functools.partial(_attn_kernel, nb=nb, n_sel=n_sel),
        grid=(bsz, nb),
        in_specs=[pl.BlockSpec((1, B_BLOCK, d), lambda b, j: (b, j, 0)), vec, vec, vec,
                  full2((1, d)), full2((d, d)), full2((d, d)),
                  pl.BlockSpec((1, npair, nb, B_BLOCK, LANES), lambda b, j: (b, 0, 0, 0, 0)),
                  pl.BlockSpec((1, npair, nb, LANES, B_BLOCK), lambda b, j: (b, 0, 0, 0, 0)),
                  pl.BlockSpec((1, npair, nbp, LANES), lambda b, j: (b, 0, 0, 0))],
        out_specs=pl.BlockSpec((1, B_BLOCK, d), lambda b, j: (b, j, 0)),
        out_shape=jax.ShapeDtypeStruct((bsz, s, d), F32),
        scratch_shapes=[pltpu.VMEM((npair, LANES, B_BLOCK), F32),
                        pltpu.VMEM((npair, 2, LANES, B_BLOCK), BF16),
                        pltpu.VMEM((npair, LANES, B_BLOCK), F32),
                        pltpu.VMEM((npair, 2, nbp, B_BLOCK), F32),
                        pltpu.VMEM((npair, 2, 1, B_BLOCK), F32),
                        pltpu.VMEM((npair, 2, 1, B_BLOCK), F32)],
        compiler_params=_cparams(("arbitrary", "arbitrary")),
    )(x, sh, sc, gt, nw.reshape(1, d), w_q.T.astype(BF16), w_o.astype(BF16), k5, vt5, km2)


def kernel(x, c, ada_w, ada_b, norm_mix, norm_ffn, a_w_in, a_b_in, a_ln_g, a_ln_b, a_w_s, a_b_s,
           a_w_out, kv_norm, kv_ada_w, kv_ada_b, kv_w_k, kv_w_v, b_w_q, b_w_o, moe_router, moe_bias,
           moe_w_gate, moe_w_up, moe_w_down, sh_w_gate, sh_w_up, sh_w_down, final_norm):
    bsz, s, d = x.shape
    depth = ada_w.shape[0]
    n_a = a_w_in.shape[0]
    assert s % B_BLOCK == 0 and s % TM_GMLP == 0 and d % LANES == 0
    nb = s // B_BLOCK
    npair = d // LANES
    nbp = -(-nb // 8) * 8

    def split(m, n):
        return [m[:, i * d:(i + 1) * d].reshape(bsz, 1, d) for i in range(n)]

    layer_mods = _ada(c, ada_w, ada_b)
    k5 = vt5 = km2 = None
    for i in range(depth):
        sh1, sc1, g1, sh2, sc2, g2 = split(layer_mods[i], 6)
        if i < n_a:
            x = _gmlp(x, sh1, sc1, g1, norm_mix[i], a_w_in[i], a_b_in[i], a_ln_g[i], a_ln_b[i],
                      a_w_s[i], a_b_s[i], a_w_out[i])
        else:
            if k5 is None:
                ksh, ksc = split(_ada(c, kv_ada_w[None], kv_ada_b[None])[0], 2)
                k5, vt5, km = _kv(x, ksh, ksc, kv_norm, kv_w_k, kv_w_v)
                km = km.reshape(bsz, nb, npair, LANES).transpose(0, 2, 1, 3)
                km2 = jnp.pad(km, ((0, 0), (0, 0), (0, nbp - nb), (0, 0)))
            j = i - n_a
            x = _attn(x, sh1, sc1, g1, norm_mix[i], b_w_q[j], b_w_o[j], k5, vt5, km2)
        x = _moe(x, sh2, sc2, g2, norm_ffn[i], moe_router[i], moe_bias[i], i, moe_w_gate,
                 moe_w_up, moe_w_down, sh_w_gate[i], sh_w_up[i], sh_w_down[i],
                 final_norm, i == depth - 1)
    return x
```

```python
import functools

import jax
import jax.numpy as jnp
from jax import lax
from jax.experimental import pallas as pl
from jax.experimental.pallas import tpu as pltpu

F32 = jnp.float32
BF16 = jnp.bfloat16
I32 = jnp.int32

RMS_EPS = 1e-6
LN_EPS = 1e-5
NEG_INF = -1e30

A_CHUNK = 128
A_GROUPS = 8
B_HEADS = 16
B_BLOCK = 256
B_TOPK = 3
N_EXPERTS = 64
TOP_K = 8
N_GROUPS = 8
TOPK_GROUPS = 4
ROUTED_SCALE = 2.5

LANES = 128
VMEM_LIMIT = 56 * 1024 * 1024

TM_GMLP = 256
TM_SORT = 512
RUN_ALIGN = 16
SORT_CHUNK = 256
ROWS_TILE_USED = TM_SORT * TOP_K + N_EXPERTS * (RUN_ALIGN - 1)
ROWS_TILE = -(-ROWS_TILE_USED // SORT_CHUNK) * SORT_CHUNK
PIECES_TILE = ROWS_TILE // RUN_ALIGN
PIECES_PAD = -(-PIECES_TILE // LANES) * LANES
BM_EXPERT = 512
BM_CHAIN = 256
ATTN_PAIR_UNROLL = 4
SUM_ROWS = 16


def _cparams(sem):
    return pltpu.CompilerParams(dimension_semantics=sem, vmem_limit_bytes=VMEM_LIMIT)


def _sigmoid(x):
    return 1.0 / (1.0 + jnp.exp(-x))


def _silu(x):
    return x * _sigmoid(x)


def _gelu_tanh(x):
    return 0.5 * x * (1.0 + jnp.tanh(0.7978845608028654 * (x + 0.044715 * (x * x * x))))


def _rms(x, g):
    return x * lax.rsqrt(jnp.mean(x * x, axis=-1, keepdims=True) + RMS_EPS) * g


def _bdot(a, b):
    return jnp.dot(a, b, preferred_element_type=F32)


def _bdot_nt(a, b):
    return lax.dot_general(a, b, (((1,), (1,)), ((), ())), preferred_element_type=F32)


def _split(a):
    hi = a.astype(BF16)
    lo = (a - hi.astype(F32)).astype(BF16)
    return hi, lo


def _dot3_nt(a, b):
    ah, al = _split(a)
    bh, bl = _split(b)
    return _bdot_nt(ah, bh) + (_bdot_nt(ah, bl) + _bdot_nt(al, bh))


def _dot3(a, b):
    ah, al = _split(a)
    bh, bl = _split(b)
    return _bdot(ah, bh) + (_bdot(ah, bl) + _bdot(al, bh))


def _ada_kernel(c_ref, w_ref, b_ref, o_ref):
    a = _silu(c_ref[...]).astype(BF16)
    o_ref[0] = _bdot(a, w_ref[0].astype(BF16)) + b_ref[0]


def _ada(c, w, b):
    bsz, d = c.shape
    nl, _, n = w.shape
    tn = 1024
    return pl.pallas_call(
        _ada_kernel,
        grid=(nl, n // tn),
        in_specs=[pl.BlockSpec((bsz, d), lambda l, j: (0, 0)),
                  pl.BlockSpec((1, d, tn), lambda l, j: (l, 0, j)),
                  pl.BlockSpec((1, 1, tn), lambda l, j: (l, 0, j))],
        out_specs=pl.BlockSpec((1, bsz, tn), lambda l, j: (l, 0, j)),
        out_shape=jax.ShapeDtypeStruct((nl, bsz, n), F32),
        compiler_params=_cparams(("arbitrary", "arbitrary")),
    )(c, w, b.reshape(nl, 1, n))


def _gmlp_kernel(x_ref, sh_ref, sc_ref, g_ref, nw_ref, win_ref, bin_ref, lng_ref, lnb_ref,
                 ws_ref, bst_ref, wout_ref, o_ref, y_sc):
    x = x_ref[0]
    tm = x.shape[0]
    h = _rms(x, nw_ref[...]) * (1.0 + sc_ref[0]) + sh_ref[0]
    z = _gelu_tanh(_bdot(h.astype(BF16), win_ref[...]) + bin_ref[...])
    aw = z.shape[1] // 2
    gd = aw // A_GROUPS
    u = z[:, :aw]
    v = z[:, aw:]
    mu = jnp.mean(v, axis=-1, keepdims=True)
    dv = v - mu
    var = jnp.mean(dv * dv, axis=-1, keepdims=True)
    vn = (dv * lax.rsqrt(var + LN_EPS) * lng_ref[...] + lnb_ref[...]).astype(BF16)
    row = lax.broadcasted_iota(I32, (A_CHUNK, A_CHUNK), 0)
    col = lax.broadcasted_iota(I32, (A_CHUNK, A_CHUNK), 1)
    causal = col <= row
    for g in range(A_GROUPS):
        wg = jnp.where(causal, ws_ref[g], 0.0).astype(BF16)
        bcol = bst_ref[:, g:g + 1]
        for ci in range(tm // A_CHUNK):
            rs = slice(ci * A_CHUNK, (ci + 1) * A_CHUNK)
            cs = slice(g * gd, (g + 1) * gd)
            sv = _bdot(wg, vn[rs, cs]) + bcol
            y_sc[rs, cs] = (u[rs, cs] * sv).astype(BF16)
    o_ref[0] = x + g_ref[0] * _bdot(y_sc[...], wout_ref[...])


def _gmlp(x, sh, sc, gt, nw, w_in, b_in, ln_g, ln_b, w_s, b_s, w_out):
    bsz, s, d = x.shape
    tm = TM_GMLP
    n_in = w_in.shape[1]
    aw = n_in // 2
    vec = pl.BlockSpec((1, 1, d), lambda b, j: (b, 0, 0))
    full2 = lambda shape: pl.BlockSpec(shape, lambda b, j: (0, 0))
    return pl.pallas_call(
        _gmlp_kernel,
        grid=(bsz, s // tm),
        in_specs=[pl.BlockSpec((1, tm, d), lambda b, j: (b, j, 0)), vec, vec, vec,
                  full2((1, d)), full2((d, n_in)), full2((1, n_in)), full2((1, aw)), full2((1, aw)),
                  pl.BlockSpec((A_GROUPS, A_CHUNK, A_CHUNK), lambda b, j: (0, 0, 0)),
                  full2((A_CHUNK, A_GROUPS)), full2((aw, d))],
        out_specs=pl.BlockSpec((1, tm, d), lambda b, j: (b, j, 0)),
        out_shape=jax.ShapeDtypeStruct((bsz, s, d), F32),
        scratch_shapes=[pltpu.VMEM((tm, aw), BF16)],
        compiler_params=_cparams(("arbitrary", "arbitrary")),
    )(x, sh, sc, gt, nw.reshape(1, d), w_in.astype(BF16), b_in.reshape(1, n_in),
      ln_g.reshape(1, aw), ln_b.reshape(1, aw), w_s, b_s.T, w_out.astype(BF16))


def _router_kernel(x_ref, sh_ref, sc_ref, g_ref, nw_ref, wrt_ref, bias_ref, wsg_ref, wsu_ref, wsd_ref,
                   h_ref, base_ref, key_ref, wkey_ref, cnt_ref, pe_ref, po_ref):
    x = x_ref[0]
    tm = x.shape[0]
    h = _rms(x, nw_ref[...]) * (1.0 + sc_ref[0]) + sh_ref[0]
    hb = h.astype(BF16)
    h_ref[...] = hb
    act = (_silu(_bdot(hb, wsg_ref[...])) * _bdot(hb, wsu_ref[...])).astype(BF16)
    base_ref[...] = x + g_ref[0] * _bdot(act, wsd_ref[...])

    scores = _sigmoid(_dot3_nt(wrt_ref[...], h))
    choice = scores + bias_ref[...]
    gsz = N_EXPERTS // N_GROUPS
    sub = lax.broadcasted_iota(I32, (gsz, tm), 0)
    blocks = [choice[g * gsz:(g + 1) * gsz] for g in range(N_GROUPS)]
    gscore = []
    for blk in blocks:
        m1 = jnp.max(blk, axis=0, keepdims=True)
        i1 = jnp.min(jnp.where(blk == m1, sub, gsz), axis=0, keepdims=True)
        m2 = jnp.max(jnp.where(sub == i1, -jnp.inf, blk), axis=0, keepdims=True)
        gscore.append(m1 + m2)
    masked = []
    for g in range(N_GROUPS):
        beats = jnp.zeros((1, tm), F32)
        for m in range(N_GROUPS):
            if m == g:
                continue
            b = gscore[m] > gscore[g]
            if m < g:
                b = jnp.logical_or(b, gscore[m] == gscore[g])
            beats = beats + b.astype(F32)
        masked.append(jnp.where(beats < TOPK_GROUPS, blocks[g], NEG_INF))
    cur = jnp.concatenate(masked, axis=0)
    eio = lax.broadcasted_iota(I32, (N_EXPERTS, tm), 0)
    chosen = jnp.zeros((N_EXPERTS, tm), jnp.bool_)
    wsum = jnp.zeros((1, tm), F32)
    for _ in range(TOP_K):
        m = jnp.max(cur, axis=0, keepdims=True)
        idx = jnp.min(jnp.where(cur == m, eio, N_EXPERTS), axis=0, keepdims=True)
        sel = eio == idx
        chosen = jnp.logical_or(chosen, sel)
        wsum = wsum + jnp.sum(jnp.where(sel, scores, 0.0), axis=0, keepdims=True)
        cur = jnp.where(sel, -jnp.inf, cur)
    wkey_ref[...] = jnp.where(chosen, scores / wsum * ROUTED_SCALE, 0.0)

    onehot = chosen.astype(BF16)
    r_i = lax.broadcasted_iota(I32, (tm, tm), 0)
    c_i = lax.broadcasted_iota(I32, (tm, tm), 1)
    before = (r_i < c_i).astype(BF16)
    prior = _bdot(onehot, before)
    key_ref[...] = jnp.where(chosen, prior, -1.0).astype(I32)
    cnt = jnp.sum(chosen.astype(F32), axis=1, keepdims=True)
    cnt_ref[0] = cnt

    run_p = jnp.floor((cnt + (RUN_ALIGN - 1)) / RUN_ALIGN)
    ppc = SORT_CHUNK // RUN_ALIGN
    tot = jnp.sum(run_p, axis=0, keepdims=True)
    fill = jnp.ceil(tot / ppc) * ppc - tot
    run_p = run_p + jnp.where(lax.broadcasted_iota(I32, (N_EXPERTS, 1), 0) == N_EXPERTS - 1, fill, 0.0)
    e_r = lax.broadcasted_iota(I32, (N_EXPERTS, N_EXPERTS), 0)
    e_c = lax.broadcasted_iota(I32, (N_EXPERTS, N_EXPERTS), 1)
    incl = (e_c <= e_r).astype(BF16)
    lend = _bdot(incl, jnp.broadcast_to(run_p, (N_EXPERTS, LANES)).astype(BF16))[:, 0:1]
    loff = lend - run_p
    pj = lax.broadcasted_iota(I32, (N_EXPERTS, PIECES_PAD), 1).astype(F32)
    er = lax.broadcasted_iota(I32, (N_EXPERTS, PIECES_PAD), 0).astype(F32)
    pe = jnp.minimum(jnp.sum((lend <= pj).astype(F32), axis=0, keepdims=True), N_EXPERTS - 1.0)
    lo = jnp.sum(jnp.where(er == pe, loff, 0.0), axis=0, keepdims=True)
    pe_ref[0] = pe.astype(I32)
    po_ref[0] = ((pj[0:1, :] - lo) * RUN_ALIGN).astype(I32)


def _router(x, sh, sc, gt, nw, w_router, e_bias, wsg, wsu, wsd):
    bsz, s, d = x.shape
    t = bsz * s
    tm = TM_SORT
    nt = s // tm
    sd = wsg.shape[1]
    vec = pl.BlockSpec((1, 1, d), lambda b, j: (b, 0, 0))
    full2 = lambda shape: pl.BlockSpec(shape, lambda b, j: (0, 0))
    tok = pl.BlockSpec((tm, d), lambda b, j: (b * nt + j, 0))
    etok = pl.BlockSpec((N_EXPERTS, tm), lambda b, j: (0, b * nt + j))
    ptab = pl.BlockSpec((1, 1, PIECES_PAD), lambda b, j: (b * nt + j, 0, 0))
    return pl.pallas_call(
        _router_kernel,
        grid=(bsz, nt),
        in_specs=[pl.BlockSpec((1, tm, d), lambda b, j: (b, j, 0)), vec, vec, vec,
                  full2((1, d)), full2((N_EXPERTS, d)), full2((N_EXPERTS, 1)),
                  full2((d, sd)), full2((d, sd)), full2((sd, d))],
        out_specs=[tok, tok, etok, etok,
                   pl.BlockSpec((1, N_EXPERTS, 1), lambda b, j: (b * nt + j, 0, 0)), ptab, ptab],
        out_shape=[jax.ShapeDtypeStruct((t, d), BF16), jax.ShapeDtypeStruct((t, d), F32),
                   jax.ShapeDtypeStruct((N_EXPERTS, t), I32), jax.ShapeDtypeStruct((N_EXPERTS, t), F32),
                   jax.ShapeDtypeStruct((t // tm, N_EXPERTS, 1), F32),
                   jax.ShapeDtypeStruct((t // tm, 1, PIECES_PAD), I32),
                   jax.ShapeDtypeStruct((t // tm, 1, PIECES_PAD), I32)],
        compiler_params=_cparams(("arbitrary", "arbitrary")),
    )(x, sh, sc, gt, nw.reshape(1, d), w_router.T, e_bias.reshape(N_EXPERTS, 1),
      wsg.astype(BF16), wsu.astype(BF16), wsd.astype(BF16))


def _rows_max(t):
    rows = (t // TM_SORT) * ROWS_TILE + N_EXPERTS * (BM_EXPERT - RUN_ALIGN)
    return -(-rows // BM_EXPERT) * BM_EXPERT


def _sort_meta(cnt, n_rows_max):
    nts = cnt.shape[0]
    c = cnt.reshape(nts, N_EXPERTS).astype(I32)
    run = (c + (RUN_ALIGN - 1)) // RUN_ALIGN * RUN_ALIGN
    fill = (-jnp.sum(run, axis=1, keepdims=True)) % SORT_CHUNK
    run = jnp.concatenate([run[:, :-1], run[:, -1:] + fill], axis=1)
    used = jnp.sum(run, axis=1)
    per_e = jnp.sum(run, axis=0)
    seg = (per_e + (BM_EXPERT - 1)) // BM_EXPERT * BM_EXPERT
    ends = jnp.cumsum(seg)
    starts = ends - seg
    goff = starts[None, :] + jnp.cumsum(run, axis=0) - run
    pad_start = starts + per_e
    pad_cnt = (seg - per_e) // RUN_ALIGN

    nb = n_rows_max // BM_EXPERT
    blo = jnp.arange(nb, dtype=I32) * BM_EXPERT
    be = jnp.sum((ends[None, :] <= blo[:, None]).astype(I32), axis=1)
    breal = (be < N_EXPERTS).astype(I32)
    be = jnp.minimum(be, N_EXPERTS - 1).astype(I32)
    prev = jnp.concatenate([jnp.full((1,), -1, I32), be[:-1]])
    bnew = (be != prev).astype(I32)
    bx = jnp.where(breal == 1, jnp.arange(nb, dtype=I32), 0)
    i32 = lambda a: a.reshape(-1).astype(I32)
    return (i32(goff), i32(used), i32(pad_start), i32(pad_cnt)), (bx, be, bnew, breal)


def _piece_copies(pe_ref, po_ref, goff_ref, used_ref, tile, local, remote, sem, to_remote):
    def copy(j):
        idx = tile * PIECES_PAD + j
        l0 = pl.multiple_of(j * RUN_ALIGN, RUN_ALIGN)
        g0 = pl.multiple_of(goff_ref[tile * N_EXPERTS + pe_ref[idx]] + po_ref[idx], RUN_ALIGN)
        lref = local.at[pl.ds(l0, RUN_ALIGN)]
        gref = remote.at[pl.ds(g0, RUN_ALIGN)]
        return pltpu.make_async_copy(lref, gref, sem) if to_remote else pltpu.make_async_copy(gref, lref, sem)

    def wait_all():
        def body(j, carry):
            copy(j).wait()
            return carry

        lax.fori_loop(0, used_ref[tile] // RUN_ALIGN, body, 0)

    return copy, wait_all


def _pad_copies(pad_start_ref, pad_cnt_ref, tile, ntiles, zeros, remote, sem):
    share = -(-N_EXPERTS // ntiles)

    def apply(act):
        for q in range(share):
            e = tile * share + q
            ec = jnp.minimum(e, N_EXPERTS - 1)
            n = jnp.where(e < N_EXPERTS, pad_cnt_ref[ec], 0)
            base = pad_start_ref[ec]

            def body(j, carry):
                g0 = pl.multiple_of(base + j * RUN_ALIGN, RUN_ALIGN)
                getattr(pltpu.make_async_copy(zeros, remote.at[pl.ds(g0, RUN_ALIGN)], sem), act)()
                return carry

            lax.fori_loop(0, n, body, 0)

    return apply


def _piece_rows(pe_ref, po_ref, key_ref, first_piece, npieces, val_ref=None):
    tm = key_ref.shape[1]
    sub = lax.broadcasted_iota(I32, (RUN_ALIGN, tm), 0)
    out = []
    for jj in range(npieces):
        j = first_piece + jj
        e = pe_ref[j]
        hit = (key_ref[pl.ds(e, 1), :] - po_ref[j]) == sub
        val = 1.0 if val_ref is None else val_ref[pl.ds(e, 1), :]
        out.append(jnp.where(hit, val, 0.0))
    return out


def _dispatch_kernel(pe_ref, po_ref, goff_ref, used_ref, pad_start_ref, pad_cnt_ref,
                     key_ref, h_ref, xs_ref, xbuf, p_sc, zbuf, sem, zsem, *, ntiles):
    i = pl.program_id(0)
    hb = h_ref[...]
    ch = SORT_CHUNK
    ppc = ch // RUN_ALIGN
    zbuf[...] = jnp.zeros_like(zbuf)
    pads = _pad_copies(pad_start_ref, pad_cnt_ref, i, ntiles, zbuf, xs_ref, zsem)
    pads("start")

    copy, wait_all = _piece_copies(pe_ref, po_ref, goff_ref, used_ref, i, xbuf, xs_ref, sem, True)

    def build(ci):
        rows = _piece_rows(pe_ref, po_ref, key_ref, i * PIECES_PAD + ci * ppc, ppc)
        for jj, p in enumerate(rows):
            p_sc[jj * RUN_ALIGN:(jj + 1) * RUN_ALIGN, :] = p.astype(BF16)
        xbuf[pl.ds(pl.multiple_of(ci * ch, ch), ch), :] = _bdot(p_sc[...], hb).astype(BF16)

    def send(ci):
        for jj in range(ppc):
            copy(ci * ppc + jj).start()

    def step(ci, carry):
        send(ci - 1)
        build(ci)
        return carry

    nchunk = used_ref[i] // ch
    build(0)
    lax.fori_loop(1, nchunk, step, 0)
    send(nchunk - 1)
    pads("wait")
    wait_all()


def _dispatch(meta, pe, po, key, hb, n_rows_max):
    t, d = hb.shape
    tm = TM_SORT
    goff, used, pad_start, pad_cnt = meta
    return pl.pallas_call(
        functools.partial(_dispatch_kernel, ntiles=t // tm),
        grid_spec=pltpu.PrefetchScalarGridSpec(
            num_scalar_prefetch=6, grid=(t // tm,),
            in_specs=[pl.BlockSpec((N_EXPERTS, tm), lambda i, *_: (0, i)),
                      pl.BlockSpec((tm, d), lambda i, *_: (i, 0))],
            out_specs=pl.BlockSpec(memory_space=pl.ANY),
            scratch_shapes=[pltpu.VMEM((ROWS_TILE, d), BF16), pltpu.VMEM((SORT_CHUNK, tm), BF16),
                            pltpu.VMEM((RUN_ALIGN, d), BF16),
                            pltpu.SemaphoreType.DMA(()), pltpu.SemaphoreType.DMA(())]),
        out_shape=jax.ShapeDtypeStruct((n_rows_max, d), BF16),
        compiler_params=_cparams(("arbitrary",)),
    )(pe, po, goff, used, pad_start, pad_cnt, key, hb)


def _expert_kernel(bx_ref, be_ref, bnew_ref, breal_ref, x_ref, wg_ref, wu_ref, wd_ref, o_ref,
                   wg_sc, wu_sc, wd_sc):
    b = pl.program_id(0)

    @pl.when(bnew_ref[b] == 1)
    def _():
        wg_sc[...] = wg_ref[0, 0].astype(BF16)
        wu_sc[...] = wu_ref[0, 0].astype(BF16)
        wd_sc[...] = wd_ref[0, 0].astype(BF16)

    @pl.when(breal_ref[b] == 0)
    def _():
        o_ref[...] = jnp.zeros_like(o_ref)

    @pl.when(breal_ref[b] == 1)
    def _():
        parts = [slice(r, r + BM_CHAIN) for r in range(0, BM_EXPERT, BM_CHAIN)]
        xs = [x_ref[rs, :] for rs in parts]
        gs = [_bdot(x, wg_sc[...]) for x in xs]
        us = [_bdot(x, wu_sc[...]) for x in xs]
        acts = [(_silu(g) * u).astype(BF16) for g, u in zip(gs, us)]
        for rs, act in zip(parts, acts):
            o_ref[rs, :] = _bdot(act, wd_sc[...]).astype(BF16)


def _experts(meta, xs, layer, w_gate, w_up, w_down):
    r, d = xs.shape
    ed = w_gate.shape[-1]
    bm = BM_EXPERT
    omap = lambda b, bx, be, bnew, breal: (b, 0)
    xmap = lambda b, bx, be, bnew, breal: (bx[b], 0)
    wmap = lambda b, bx, be, bnew, breal: (layer, be[b], 0, 0)
    return pl.pallas_call(
        _expert_kernel,
        grid_spec=pltpu.PrefetchScalarGridSpec(
            num_scalar_prefetch=4, grid=(r // bm,),
            in_specs=[pl.BlockSpec((bm, d), xmap), pl.BlockSpec((1, 1, d, ed), wmap),
                      pl.BlockSpec((1, 1, d, ed), wmap), pl.BlockSpec((1, 1, ed, d), wmap)],
            out_specs=pl.BlockSpec((bm, d), omap),
            scratch_shapes=[pltpu.VMEM((d, ed), BF16), pltpu.VMEM((d, ed), BF16),
                            pltpu.VMEM((ed, d), BF16)]),
        out_shape=jax.ShapeDtypeStruct((r, d), BF16),
        compiler_params=_cparams(("arbitrary",)),
    )(*meta, xs, w_gate, w_up, w_down)


def _fetch_tile(pe_ref, po_ref, goff_ref, tile, key_ref, wkey_ref, ys_ref, ybuf, q_sc, sem):
    ch = SORT_CHUNK
    ppc = ch // RUN_ALIGN
    npieces = ybuf.shape[0] // RUN_ALIGN
    last = ys_ref.shape[0] - RUN_ALIGN

    def copy(j):
        idx = tile * PIECES_PAD + j
        g0 = jnp.minimum(goff_ref[tile * N_EXPERTS + pe_ref[idx]] + po_ref[idx], last)
        return pltpu.make_async_copy(ys_ref.at[pl.ds(pl.multiple_of(g0, RUN_ALIGN), RUN_ALIGN)],
                                     ybuf.at[pl.ds(pl.multiple_of(j * RUN_ALIGN, RUN_ALIGN), RUN_ALIGN)], sem)

    def issue():
        for j in range(npieces):
            copy(j).start()

    def build():
        for ci in range(ybuf.shape[0] // ch):
            rows = _piece_rows(pe_ref, po_ref, key_ref, tile * PIECES_PAD + ci * ppc, ppc, wkey_ref)
            q_sc[:, ci * ch:(ci + 1) * ch] = jnp.concatenate(rows, axis=0).T.astype(BF16)

    def wait_all():
        def body(j, carry):
            copy(j).wait()
            return carry

        lax.fori_loop(0, npieces, body, 0)

    return issue, build, wait_all


def _combine_kernel(pe_ref, po_ref, goff_ref, key_ref, wkey_ref, keyn_ref, wkeyn_ref, ys_ref, base_ref,
                    g_ref, fn_ref, o_ref, ybuf_a, ybuf_b, q_a, q_b, sem_a, sem_b, *, final, ntiles):
    i = pl.program_id(0)
    nxt_tile = jnp.minimum(i + 1, ntiles - 1)
    slots = ((ybuf_a, q_a, sem_a), (ybuf_b, q_b, sem_b))

    @pl.when(i == 0)
    def _():
        issue, build, _ = _fetch_tile(pe_ref, po_ref, goff_ref, i, key_ref, wkey_ref, ys_ref, *slots[0])
        issue()
        build()

    for parity in range(2):
        @pl.when(i % 2 == parity)
        def _():
            ybuf, q_sc, sem = slots[parity]
            def drain(j, carry):
                pltpu.make_async_copy(ys_ref.at[pl.ds(0, RUN_ALIGN)], ybuf.at[pl.ds(0, RUN_ALIGN)], sem).wait()
                return carry

            lax.fori_loop(0, ybuf.shape[0] // RUN_ALIGN, drain, 0)
            issue, build, wait_next = _fetch_tile(pe_ref, po_ref, goff_ref, nxt_tile, keyn_ref, wkeyn_ref,
                                                  ys_ref, *slots[1 - parity])
            issue()
            out = base_ref[...] + g_ref[0] * _bdot(q_sc[...], ybuf[...])
            build()
            if final:
                out = _rms(out, fn_ref[...])
            o_ref[...] = out

            @pl.when(i == ntiles - 1)
            def _():
                wait_next()


def _combine(meta, pe, po, key, wkey, ys, base, gt, fnorm, seq, final):
    t, d = base.shape
    tm = TM_SORT
    nt = seq // tm
    ntiles = t // tm
    cur = pl.BlockSpec((N_EXPERTS, tm), lambda i, *_: (0, i))
    nxt = pl.BlockSpec((N_EXPERTS, tm), lambda i, *_: (0, jnp.minimum(i + 1, ntiles - 1)))
    return pl.pallas_call(
        functools.partial(_combine_kernel, final=final, ntiles=ntiles),
        grid_spec=pltpu.PrefetchScalarGridSpec(
            num_scalar_prefetch=3, grid=(ntiles,),
            in_specs=[cur, cur, nxt, nxt,
                      pl.BlockSpec(memory_space=pl.ANY),
                      pl.BlockSpec((tm, d), lambda i, *_: (i, 0)),
                      pl.BlockSpec((1, 1, d), lambda i, *_: (i // nt, 0, 0)),
                      pl.BlockSpec((1, d), lambda i, *_: (0, 0))],
            out_specs=pl.BlockSpec((tm, d), lambda i, *_: (i, 0)),
            scratch_shapes=[pltpu.VMEM((ROWS_TILE, d), BF16), pltpu.VMEM((ROWS_TILE, d), BF16),
                            pltpu.VMEM((tm, ROWS_TILE), BF16), pltpu.VMEM((tm, ROWS_TILE), BF16),
                            pltpu.SemaphoreType.DMA(()), pltpu.SemaphoreType.DMA(())]),
        out_shape=jax.ShapeDtypeStruct((t, d), F32),
        compiler_params=_cparams(("arbitrary",)),
    )(pe, po, meta[0], key, wkey, key, wkey, ys, base, gt, fnorm.reshape(1, d))


def _moe(x, sh, sc, gt, nw, w_router, e_bias, layer, w_gate, w_up, w_down, wsg, wsu, wsd, fnorm, final):
    bsz, s, d = x.shape
    t = bsz * s
    n_rows_max = _rows_max(t)
    hb, base, key, wkey, cnt, pe, po = _router(x, sh, sc, gt, nw, w_router, e_bias, wsg, wsu, wsd)
    pe = pe.reshape(-1)
    po = po.reshape(-1)
    layout_meta, block_meta = _sort_meta(cnt, n_rows_max)
    xs = _dispatch(layout_meta, pe, po, key, hb, n_rows_max)
    ys = _experts(block_meta, xs, layer, w_gate, w_up, w_down)
    out = _combine(layout_meta, pe, po, key, wkey, ys, base, gt, fnorm, s, final)
    return out.reshape(bsz, s, d)


def _kv_kernel(x_ref, sh_ref, sc_ref, nw_ref, wk_ref, wvt_ref, k_ref, vt_ref, km_ref):
    x = x_ref[0]
    hb = (_rms(x, nw_ref[...]) * (1.0 + sc_ref[0]) + sh_ref[0]).astype(BF16)
    k = _bdot(hb, wk_ref[...])
    vt = _bdot_nt(wvt_ref[...], hb)
    for p in range(k.shape[1] // LANES):
        k_ref[0, p, 0] = k[:, p * LANES:(p + 1) * LANES].astype(BF16)
        vt_ref[0, p, 0] = vt[p * LANES:(p + 1) * LANES, :].astype(BF16)
    km_ref[0, 0] = jnp.mean(k, axis=0, keepdims=True)


def _kv(x, sh, sc, nw, w_k, w_v):
    bsz, s, d = x.shape
    nb = s // B_BLOCK
    npair = d // LANES
    vec = pl.BlockSpec((1, 1, d), lambda b, j: (b, 0, 0))
    full2 = lambda shape: pl.BlockSpec(shape, lambda b, j: (0, 0))
    return pl.pallas_call(
        _kv_kernel,
        grid=(bsz, nb),
        in_specs=[pl.BlockSpec((1, B_BLOCK, d), lambda b, j: (b, j, 0)), vec, vec,
                  full2((1, d)), full2((d, d)), full2((d, d))],
        out_specs=[pl.BlockSpec((1, npair, 1, B_BLOCK, LANES), lambda b, j: (b, 0, j, 0, 0)),
                   pl.BlockSpec((1, npair, 1, LANES, B_BLOCK), lambda b, j: (b, 0, j, 0, 0)),
                   pl.BlockSpec((1, 1, 1, d), lambda b, j: (b, j, 0, 0))],
        out_shape=[jax.ShapeDtypeStruct((bsz, npair, nb, B_BLOCK, LANES), BF16),
                   jax.ShapeDtypeStruct((bsz, npair, nb, LANES, B_BLOCK), BF16),
                   jax.ShapeDtypeStruct((bsz, nb, 1, d), F32)],
        compiler_params=_cparams(("arbitrary", "arbitrary")),
    )(x, sh, sc, nw.reshape(1, d), w_k.astype(BF16), w_v.T.astype(BF16))


def _attn_kernel(x_ref, sh_ref, sc_ref, g_ref, nw_ref, wqt_ref, wo_ref, k_ref, vt_ref, km_ref,
                 o_ref, qt_sc, qs_sc, acc_sc, sel_sc, m_sc, l_sc, *, nb, n_sel):
    qb = pl.program_id(1)
    x = x_ref[0]
    bq = x.shape[0]
    npair = qt_sc.shape[0]
    nbp = km_ref.shape[2]
    hd = LANES // 2
    scale = float(hd) ** -0.5 * 1.4426950408889634
    h = _rms(x, nw_ref[...]) * (1.0 + sc_ref[0]) + sh_ref[0]
    qt = _bdot_nt(wqt_ref[...], h.astype(BF16))
    for p in range(npair):
        qt_sc[p] = qt[p * LANES:(p + 1) * LANES, :]

    subn = lax.broadcasted_iota(I32, (nbp, bq), 0)
    past = subn < qb
    krow = lax.broadcasted_iota(I32, (B_BLOCK, bq), 0)
    qcol = lax.broadcasted_iota(I32, (B_BLOCK, bq), 1)
    causal = krow <= qcol
    rowh = lax.broadcasted_iota(I32, (LANES, 1), 0)

    grp = ATTN_PAIR_UNROLL
    heads = [(u, e) for u in range(grp) for e in range(2)]
    ones_rows = jnp.ones((SUM_ROWS, B_BLOCK), BF16)

    def vsum(vt2, e):
        return jnp.concatenate([vt2[e * hd:(e + 1) * hd, :], ones_rows], axis=0)

    def own_body(gi, carry):
        ps = [gi * grp + u for u in range(grp)]
        q2ts = [qt_sc[p] for p in ps]
        kms = [km_ref[0, p] for p in ps]
        kown = [k_ref[0, p, qb] for p in ps]
        vown = [vt_ref[0, p, qb] for p in ps]
        qets = [jnp.where((rowh >= hd) if e == 1 else (rowh < hd), q2ts[u], 0.0) for u, e in heads]
        qsts = [(q * scale).astype(BF16) for q in qets]
        ss = [jnp.where(causal, _bdot(kown[u], qsts[i]), NEG_INF) for i, (u, e) in enumerate(heads)]
        gates = [_dot3(kms[u], qets[i]) for i, (u, e) in enumerate(heads)]
        ms = [jnp.max(s, axis=0, keepdims=True) for s in ss]
        pes = [jnp.exp2(s - m) for s, m in zip(ss, ms)]
        pvs = [_bdot(vsum(vown[u], e), pes[i].astype(BF16)) for i, (u, e) in enumerate(heads)]
        accs = [pv[:hd] for pv in pvs]
        ls = [pv[hd:hd + 1] for pv in pvs]
        sels = []
        for gate in gates:
            selt = jnp.zeros((nbp, bq), F32)
            for n in range(nb):
                gn = gate[n:n + 1, :]
                beats = jnp.logical_or(gate > gn, jnp.logical_and(gate == gn, subn < n))
                beats = jnp.logical_and(beats, past)
                cnt = jnp.sum(beats.astype(F32), axis=0, keepdims=True)
                selt = jnp.where(subn == n, (cnt < n_sel).astype(F32), selt)
            sels.append(selt)
        for i, (u, e) in enumerate(heads):
            p = ps[u]
            sel_sc[p, e] = sels[i]
            qs_sc[p, e] = qsts[i]
            m_sc[p, e] = ms[i]
            l_sc[p, e] = ls[i]
            acc_sc[p, e * hd:(e + 1) * hd, :] = accs[i]
        return carry

    lax.fori_loop(0, npair // grp, own_body, 0)

    def kb_body(kb, carry):
        def group_body(gi, c2):
            ps = [gi * grp + u for u in range(grp)]
            kbl = [k_ref[0, p, kb] for p in ps]
            vbl = [vt_ref[0, p, kb] for p in ps]
            qsts = [qs_sc[ps[u], e] for u, e in heads]
            rows = [sel_sc[ps[u], e, pl.ds(kb, 1), :] for u, e in heads]
            m_old = [m_sc[ps[u], e] for u, e in heads]
            l_old = [l_sc[ps[u], e] for u, e in heads]
            a_old = [acc_sc[ps[u], e * hd:(e + 1) * hd, :] for u, e in heads]
            ss = [jnp.where(rows[i] > 0.5, _bdot(kbl[u], qsts[i]), NEG_INF)
                  for i, (u, e) in enumerate(heads)]
            m_new = [jnp.maximum(m, jnp.max(s, axis=0, keepdims=True)) for m, s in zip(m_old, ss)]
            alphas = [jnp.exp2(m - mn) for m, mn in zip(m_old, m_new)]
            pes = [jnp.exp2(s - mn) for s, mn in zip(ss, m_new)]
            pvs = [_bdot(vsum(vbl[u], e), pes[i].astype(BF16)) for i, (u, e) in enumerate(heads)]
            l_new = [a * l + pv[hd:hd + 1] for a, l, pv in zip(alphas, l_old, pvs)]
            a_new = [a * ao + pv[:hd] for a, ao, pv in zip(alphas, a_old, pvs)]
            for i, (u, e) in enumerate(heads):
                p = ps[u]
                m_sc[p, e] = m_new[i]
                l_sc[p, e] = l_new[i]
                acc_sc[p, e * hd:(e + 1) * hd, :] = a_new[i]
            return c2

        lax.fori_loop(0, npair // grp, group_body, 0)
        return carry

    lax.fori_loop(0, qb, kb_body, 0)

    parts = []
    for p in range(npair):
        for e in range(2):
            parts.append(acc_sc[p, e * hd:(e + 1) * hd, :] / l_sc[p, e])
    ot = jnp.concatenate(parts, axis=0)
    o_ref[0] = x + g_ref[0] * _bdot(ot.T.astype(BF16), wo_ref[...])


def _attn(x, sh, sc, gt, nw, w_q, w_o, k5, vt5, km2):
    bsz, s, d = x.shape
    nb = s // B_BLOCK
    npair = d // LANES
    nbp = km2.shape[2]
    n_sel = min(B_TOPK, nb - 1)
    vec = pl.BlockSpec((1, 1, d), lambda b, j: (b, 0, 0))
    full2 = lambda shape: pl.BlockSpec(shape, lambda b, j: (0, 0))
    return pl.pallas_call(
        functools.partial(_attn_kernel, nb=nb, n_sel=n_sel),
        grid=(bsz, nb),
        in_specs=[pl.BlockSpec((1, B_BLOCK, d), lambda b, j: (b, j, 0)), vec, vec, vec,
                  full2((1, d)), full2((d, d)), full2((d, d)),
                  pl.BlockSpec((1, npair, nb, B_BLOCK, LANES), lambda b, j: (b, 0, 0, 0, 0)),
                  pl.BlockSpec((1, npair, nb, LANES, B_BLOCK), lambda b, j: (b, 0, 0, 0, 0)),
                  pl.BlockSpec((1, npair, nbp, LANES), lambda b, j: (b, 0, 0, 0))],
        out_specs=pl.BlockSpec((1, B_BLOCK, d), lambda b, j: (b, j, 0)),
        out_shape=jax.ShapeDtypeStruct((bsz, s, d), F32),
        scratch_shapes=[pltpu.VMEM((npair, LANES, B_BLOCK), F32),
                        pltpu.VMEM((npair, 2, LANES, B_BLOCK), BF16),
                        pltpu.VMEM((npair, LANES, B_BLOCK), F32),
                        pltpu.VMEM((npair, 2, nbp, B_BLOCK), F32),
                        pltpu.VMEM((npair, 2, 1, B_BLOCK), F32),
                        pltpu.VMEM((npair, 2, 1, B_BLOCK), F32)],
        compiler_params=_cparams(("arbitrary", "arbitrary")),
    )(x, sh, sc, gt, nw.reshape(1, d), w_q.T.astype(BF16), w_o.astype(BF16), k5, vt5, km2)


def kernel(x, c, ada_w, ada_b, norm_mix, norm_ffn, a_w_in, a_b_in, a_ln_g, a_ln_b, a_w_s, a_b_s,
           a_w_out, kv_norm, kv_ada_w, kv_ada_b, kv_w_k, kv_w_v, b_w_q, b_w_o, moe_router, moe_bias,
           moe_w_gate, moe_w_up, moe_w_down, sh_w_gate, sh_w_up, sh_w_down, final_norm):
    bsz, s, d = x.shape
    depth = ada_w.shape[0]
    n_a = a_w_in.shape[0]
    assert s % B_BLOCK == 0 and s % TM_GMLP == 0 and d % LANES == 0
    nb = s // B_BLOCK
    npair = d // LANES
    nbp = -(-nb // 8) * 8

    def split(m, n):
        return [m[:, i * d:(i + 1) * d].reshape(bsz, 1, d) for i in range(n)]

    layer_mods = _ada(c, ada_w, ada_b)
    k5 = vt5 = km2 = None
    for i in range(depth):
        sh1, sc1, g1, sh2, sc2, g2 = split(layer_mods[i], 6)
        if i < n_a:
            x = _gmlp(x, sh1, sc1, g1, norm_mix[i], a_w_in[i], a_b_in[i], a_ln_g[i], a_ln_b[i],
                      a_w_s[i], a_b_s[i], a_w_out[i])
        else:
            if k5 is None:
                ksh, ksc = split(_ada(c, kv_ada_w[None], kv_ada_b[None])[0], 2)
                k5, vt5, km = _kv(x, ksh, ksc, kv_norm, kv_w_k, kv_w_v)
                km = km.reshape(bsz, nb, npair, LANES).transpose(0, 2, 1, 3)
                km2 = jnp.pad(km, ((0, 0), (0, 0), (0, nbp - nb), (0, 0)))
            j = i - n_a
            x = _attn(x, sh1, sc1, g1, norm_mix[i], b_w_q[j], b_w_o[j], k5, vt5, km2)
        x = _moe(x, sh2, sc2, g2, norm_ffn[i], moe_router[i], moe_bias[i], i, moe_w_gate,
                 moe_w_up, moe_w_down, sh_w_gate[i], sh_w_up[i], sh_w_down[i],
                 final_norm, i == depth - 1)
    return x
```

```python
import functools

import jax
import jax.numpy as jnp
from jax import lax
from jax.experimental import pallas as pl
from jax.experimental.pallas import tpu as pltpu

F32 = jnp.float32
BF16 = jnp.bfloat16
I32 = jnp.int32

RMS_EPS = 1e-6
LN_EPS = 1e-5
NEG_INF = -1e30

A_CHUNK = 128
A_GROUPS = 8
B_HEADS = 16
B_BLOCK = 256
B_TOPK = 3
N_EXPERTS = 64
TOP_K = 8
N_GROUPS = 8
TOPK_GROUPS = 4
ROUTED_SCALE = 2.5

LANES = 128
VMEM_LIMIT = 56 * 1024 * 1024

TM_GMLP = 256
TM_SORT = 512
RUN_ALIGN = 16
SORT_CHUNK = 256
ROWS_TILE_USED = TM_SORT * TOP_K + N_EXPERTS * (RUN_ALIGN - 1)
ROWS_TILE = -(-ROWS_TILE_USED // SORT_CHUNK) * SORT_CHUNK
PIECES_TILE = ROWS_TILE // RUN_ALIGN
PIECES_PAD = -(-PIECES_TILE // LANES) * LANES
BM_EXPERT = 512
BM_CHAIN = 256
ATTN_PAIR_UNROLL = 4
SUM_ROWS = 16


def _cparams(sem):
    return pltpu.CompilerParams(dimension_semantics=sem, vmem_limit_bytes=VMEM_LIMIT)


def _sigmoid(x):
    return 1.0 / (1.0 + jnp.exp(-x))


def _silu(x):
    return x * _sigmoid(x)


def _gelu_tanh(x):
    return 0.5 * x * (1.0 + jnp.tanh(0.7978845608028654 * (x + 0.044715 * (x * x * x))))


def _rms(x, g):
    return x * lax.rsqrt(jnp.mean(x * x, axis=-1, keepdims=True) + RMS_EPS) * g


def _bdot(a, b):
    return jnp.dot(a, b, preferred_element_type=F32)


def _bdot_nt(a, b):
    return lax.dot_general(a, b, (((1,), (1,)), ((), ())), preferred_element_type=F32)


def _split(a):
    hi = a.astype(BF16)
    lo = (a - hi.astype(F32)).astype(BF16)
    return hi, lo


def _dot3_nt(a, b):
    ah, al = _split(a)
    bh, bl = _split(b)
    return _bdot_nt(ah, bh) + (_bdot_nt(ah, bl) + _bdot_nt(al, bh))


def _dot3(a, b):
    ah, al = _split(a)
    bh, bl = _split(b)
    return _bdot(ah, bh) + (_bdot(ah, bl) + _bdot(al, bh))


def _ada_kernel(c_ref, w_ref, b_ref, o_ref):
    a = _silu(c_ref[...]).astype(BF16)
    o_ref[0] = _bdot(a, w_ref[0].astype(BF16)) + b_ref[0]


def _ada(c, w, b):
    bsz, d = c.shape
    nl, _, n = w.shape
    tn = 1024
    return pl.pallas_call(
        _ada_kernel,
        grid=(nl, n // tn),
        in_specs=[pl.BlockSpec((bsz, d), lambda l, j: (0, 0)),
                  pl.BlockSpec((1, d, tn), lambda l, j: (l, 0, j)),
                  pl.BlockSpec((1, 1, tn), lambda l, j: (l, 0, j))],
        out_specs=pl.BlockSpec((1, bsz, tn), lambda l, j: (l, 0, j)),
        out_shape=jax.ShapeDtypeStruct((nl, bsz, n), F32),
        compiler_params=_cparams(("arbitrary", "arbitrary")),
    )(c, w, b.reshape(nl, 1, n))


def _gmlp_kernel(x_ref, sh_ref, sc_ref, g_ref, nw_ref, win_ref, bin_ref, lng_ref, lnb_ref,
                 ws_ref, bst_ref, wout_ref, o_ref, y_sc):
    x = x_ref[0]
    tm = x.shape[0]
    h = _rms(x, nw_ref[...]) * (1.0 + sc_ref[0]) + sh_ref[0]
    z = _gelu_tanh(_bdot(h.astype(BF16), win_ref[...]) + bin_ref[...])
    aw = z.shape[1] // 2
    gd = aw // A_GROUPS
    u = z[:, :aw]
    v = z[:, aw:]
    mu = jnp.mean(v, axis=-1, keepdims=True)
    dv = v - mu
    var = jnp.mean(dv * dv, axis=-1, keepdims=True)
    vn = (dv * lax.rsqrt(var + LN_EPS) * lng_ref[...] + lnb_ref[...]).astype(BF16)
    row = lax.broadcasted_iota(I32, (A_CHUNK, A_CHUNK), 0)
    col = lax.broadcasted_iota(I32, (A_CHUNK, A_CHUNK), 1)
    causal = col <= row
    for g in range(A_GROUPS):
        wg = jnp.where(causal, ws_ref[g], 0.0).astype(BF16)
        bcol = bst_ref[:, g:g + 1]
        for ci in range(tm // A_CHUNK):
            rs = slice(ci * A_CHUNK, (ci + 1) * A_CHUNK)
            cs = slice(g * gd, (g + 1) * gd)
            sv = _bdot(wg, vn[rs, cs]) + bcol
            y_sc[rs, cs] = (u[rs, cs] * sv).astype(BF16)
    o_ref[0] = x + g_ref[0] * _bdot(y_sc[...], wout_ref[...])


def _gmlp(x, sh, sc, gt, nw, w_in, b_in, ln_g, ln_b, w_s, b_s, w_out):
    bsz, s, d = x.shape
    tm = TM_GMLP
    n_in = w_in.shape[1]
    aw = n_in // 2
    vec = pl.BlockSpec((1, 1, d), lambda b, j: (b, 0, 0))
    full2 = lambda shape: pl.BlockSpec(shape, lambda b, j: (0, 0))
    return pl.pallas_call(
        _gmlp_kernel,
        grid=(bsz, s // tm),
        in_specs=[pl.BlockSpec((1, tm, d), lambda b, j: (b, j, 0)), vec, vec, vec,
                  full2((1, d)), full2((d, n_in)), full2((1, n_in)), full2((1, aw)), full2((1, aw)),
                  pl.BlockSpec((A_GROUPS, A_CHUNK, A_CHUNK), lambda b, j: (0, 0, 0)),
                  full2((A_CHUNK, A_GROUPS)), full2((aw, d))],
        out_specs=pl.BlockSpec((1, tm, d), lambda b, j: (b, j, 0)),
        out_shape=jax.ShapeDtypeStruct((bsz, s, d), F32),
        scratch_shapes=[pltpu.VMEM((tm, aw), BF16)],
        compiler_params=_cparams(("arbitrary", "arbitrary")),
    )(x, sh, sc, gt, nw.reshape(1, d), w_in.astype(BF16), b_in.reshape(1, n_in),
      ln_g.reshape(1, aw), ln_b.reshape(1, aw), w_s, b_s.T, w_out.astype(BF16))


def _router_kernel(x_ref, sh_ref, sc_ref, g_ref, nw_ref, wrt_ref, bias_ref, wsg_ref, wsu_ref, wsd_ref,
                   h_ref, base_ref, key_ref, wkey_ref, cnt_ref, pe_ref, po_ref):
    x = x_ref[0]
    tm = x.shape[0]
    h = _rms(x, nw_ref[...]) * (1.0 + sc_ref[0]) + sh_ref[0]
    hb = h.astype(BF16)
    h_ref[...] = hb
    act = (_silu(_bdot(hb, wsg_ref[...])) * _bdot(hb, wsu_ref[...])).astype(BF16)
    base_ref[...] = x + g_ref[0] * _bdot(act, wsd_ref[...])

    scores = _sigmoid(_dot3_nt(wrt_ref[...], h))
    choice = scores + bias_ref[...]
    gsz = N_EXPERTS // N_GROUPS
    sub = lax.broadcasted_iota(I32, (gsz, tm), 0)
    blocks = [choice[g * gsz:(g + 1) * gsz] for g in range(N_GROUPS)]
    gscore = []
    for blk in blocks:
        m1 = jnp.max(blk, axis=0, keepdims=True)
        i1 = jnp.min(jnp.where(blk == m1, sub, gsz), axis=0, keepdims=True)
        m2 = jnp.max(jnp.where(sub == i1, -jnp.inf, blk), axis=0, keepdims=True)
        gscore.append(m1 + m2)
    masked = []
    for g in range(N_GROUPS):
        beats = jnp.zeros((1, tm), F32)
        for m in range(N_GROUPS):
            if m == g:
                continue
            b = gscore[m] > gscore[g]
            if m < g:
                b = jnp.logical_or(b, gscore[m] == gscore[g])
            beats = beats + b.astype(F32)
        masked.append(jnp.where(beats < TOPK_GROUPS, blocks[g], NEG_INF))
    cur = jnp.concatenate(masked, axis=0)
    eio = lax.broadcasted_iota(I32, (N_EXPERTS, tm), 0)
    chosen = jnp.zeros((N_EXPERTS, tm), jnp.bool_)
    wsum = jnp.zeros((1, tm), F32)
    for _ in range(TOP_K):
        m = jnp.max(cur, axis=0, keepdims=True)
        idx = jnp.min(jnp.where(cur == m, eio, N_EXPERTS), axis=0, keepdims=True)
        sel = eio == idx
        chosen = jnp.logical_or(chosen, sel)
        wsum = wsum + jnp.sum(jnp.where(sel, scores, 0.0), axis=0, keepdims=True)
        cur = jnp.where(sel, -jnp.inf, cur)
    wkey_ref[...] = jnp.where(chosen, scores / wsum * ROUTED_SCALE, 0.0)

    onehot = chosen.astype(BF16)
    r_i = lax.broadcasted_iota(I32, (tm, tm), 0)
    c_i = lax.broadcasted_iota(I32, (tm, tm), 1)
    before = (r_i < c_i).astype(BF16)
    prior = _bdot(onehot, before)
    key_ref[...] = jnp.where(chosen, prior, -1.0).astype(I32)
    cnt = jnp.sum(chosen.astype(F32), axis=1, keepdims=True)
    cnt_ref[0] = cnt

    run_p = jnp.floor((cnt + (RUN_ALIGN - 1)) / RUN_ALIGN)
    ppc = SORT_CHUNK // RUN_ALIGN
    tot = jnp.sum(run_p, axis=0, keepdims=True)
    fill = jnp.ceil(tot / ppc) * ppc - tot
    run_p = run_p + jnp.where(lax.broadcasted_iota(I32, (N_EXPERTS, 1), 0) == N_EXPERTS - 1, fill, 0.0)
    e_r = lax.broadcasted_iota(I32, (N_EXPERTS, N_EXPERTS), 0)
    e_c = lax.broadcasted_iota(I32, (N_EXPERTS, N_EXPERTS), 1)
    incl = (e_c <= e_r).astype(BF16)
    lend = _bdot(incl, jnp.broadcast_to(run_p, (N_EXPERTS, LANES)).astype(BF16))[:, 0:1]
    loff = lend - run_p
    pj = lax.broadcasted_iota(I32, (N_EXPERTS, PIECES_PAD), 1).astype(F32)
    er = lax.broadcasted_iota(I32, (N_EXPERTS, PIECES_PAD), 0).astype(F32)
    pe = jnp.minimum(jnp.sum((lend <= pj).astype(F32), axis=0, keepdims=True), N_EXPERTS - 1.0)
    lo = jnp.sum(jnp.where(er == pe, loff, 0.0), axis=0, keepdims=True)
    pe_ref[0] = pe.astype(I32)
    po_ref[0] = ((pj[0:1, :] - lo) * RUN_ALIGN).astype(I32)


def _router(x, sh, sc, gt, nw, w_router, e_bias, wsg, wsu, wsd):
    bsz, s, d = x.shape
    t = bsz * s
    tm = TM_SORT
    nt = s // tm
    sd = wsg.shape[1]
    vec = pl.BlockSpec((1, 1, d), lambda b, j: (b, 0, 0))
    full2 = lambda shape: pl.BlockSpec(shape, lambda b, j: (0, 0))
    tok = pl.BlockSpec((tm, d), lambda b, j: (b * nt + j, 0))
    etok = pl.BlockSpec((N_EXPERTS, tm), lambda b, j: (0, b * nt + j))
    ptab = pl.BlockSpec((1, 1, PIECES_PAD), lambda b, j: (b * nt + j, 0, 0))
    return pl.pallas_call(
        _router_kernel,
        grid=(bsz, nt),
        in_specs=[pl.BlockSpec((1, tm, d), lambda b, j: (b, j, 0)), vec, vec, vec,
                  full2((1, d)), full2((N_EXPERTS, d)), full2((N_EXPERTS, 1)),
                  full2((d, sd)), full2((d, sd)), full2((sd, d))],
        out_specs=[tok, tok, etok, etok,
                   pl.BlockSpec((1, N_EXPERTS, 1), lambda b, j: (b * nt + j, 0, 0)), ptab, ptab],
        out_shape=[jax.ShapeDtypeStruct((t, d), BF16), jax.ShapeDtypeStruct((t, d), F32),
                   jax.ShapeDtypeStruct((N_EXPERTS, t), I32), jax.ShapeDtypeStruct((N_EXPERTS, t), F32),
                   jax.ShapeDtypeStruct((t // tm, N_EXPERTS, 1), F32),
                   jax.ShapeDtypeStruct((t // tm, 1, PIECES_PAD), I32),
                   jax.ShapeDtypeStruct((t // tm, 1, PIECES_PAD), I32)],
        compiler_params=_cparams(("arbitrary", "arbitrary")),
    )(x, sh, sc, gt, nw.reshape(1, d), w_router.T, e_bias.reshape(N_EXPERTS, 1),
      wsg.astype(BF16), wsu.astype(BF16), wsd.astype(BF16))


def _rows_max(t):
    rows = (t // TM_SORT) * ROWS_TILE + N_EXPERTS * (BM_EXPERT - RUN_ALIGN)
    return -(-rows // BM_EXPERT) * BM_EXPERT


def _sort_meta(cnt, n_rows_max):
    nts = cnt.shape[0]
    c = cnt.reshape(nts, N_EXPERTS).astype(I32)
    run = (c + (RUN_ALIGN - 1)) // RUN_ALIGN * RUN_ALIGN
    fill = (-jnp.sum(run, axis=1, keepdims=True)) % SORT_CHUNK
    run = jnp.concatenate([run[:, :-1], run[:, -1:] + fill], axis=1)
    used = jnp.sum(run, axis=1)
    per_e = jnp.sum(run, axis=0)
    seg = (per_e + (BM_EXPERT - 1)) // BM_EXPERT * BM_EXPERT
    ends = jnp.cumsum(seg)
    starts = ends - seg
    goff = starts[None, :] + jnp.cumsum(run, axis=0) - run
    pad_start = starts + per_e
    pad_cnt = (seg - per_e) // RUN_ALIGN

    nb = n_rows_max // BM_EXPERT
    blo = jnp.arange(nb, dtype=I32) * BM_EXPERT
    be = jnp.sum((ends[None, :] <= blo[:, None]).astype(I32), axis=1)
    breal = (be < N_EXPERTS).astype(I32)
    be = jnp.minimum(be, N_EXPERTS - 1).astype(I32)
    prev = jnp.concatenate([jnp.full((1,), -1, I32), be[:-1]])
    bnew = (be != prev).astype(I32)
    bx = jnp.where(breal == 1, jnp.arange(nb, dtype=I32), 0)
    i32 = lambda a: a.reshape(-1).astype(I32)
    return (i32(goff), i32(used), i32(pad_start), i32(pad_cnt)), (bx, be, bnew, breal)


def _piece_copies(pe_ref, po_ref, goff_ref, used_ref, tile, local, remote, sem, to_remote):
    def copy(j):
        idx = tile * PIECES_PAD + j
        l0 = pl.multiple_of(j * RUN_ALIGN, RUN_ALIGN)
        g0 = pl.multiple_of(goff_ref[tile * N_EXPERTS + pe_ref[idx]] + po_ref[idx], RUN_ALIGN)
        lref = local.at[pl.ds(l0, RUN_ALIGN)]
        gref = remote.at[pl.ds(g0, RUN_ALIGN)]
        return pltpu.make_async_copy(lref, gref, sem) if to_remote else pltpu.make_async_copy(gref, lref, sem)

    def wait_all():
        def body(j, carry):
            copy(j).wait()
            return carry

        lax.fori_loop(0, used_ref[tile] // RUN_ALIGN, body, 0)

    return copy, wait_all


def _pad_copies(pad_start_ref, pad_cnt_ref, tile, ntiles, zeros, remote, sem):
    share = -(-N_EXPERTS // ntiles)

    def apply(act):
        for q in range(share):
            e = tile * share + q
            ec = jnp.minimum(e, N_EXPERTS - 1)
            n = jnp.where(e < N_EXPERTS, pad_cnt_ref[ec], 0)
            base = pad_start_ref[ec]

            def body(j, carry):
                g0 = pl.multiple_of(base + j * RUN_ALIGN, RUN_ALIGN)
                getattr(pltpu.make_async_copy(zeros, remote.at[pl.ds(g0, RUN_ALIGN)], sem), act)()
                return carry

            lax.fori_loop(0, n, body, 0)

    return apply


def _piece_rows(pe_ref, po_ref, key_ref, first_piece, npieces, val_ref=None):
    tm = key_ref.shape[1]
    sub = lax.broadcasted_iota(I32, (RUN_ALIGN, tm), 0)
    out = []
    for jj in range(npieces):
        j = first_piece + jj
        e = pe_ref[j]
        hit = (key_ref[pl.ds(e, 1), :] - po_ref[j]) == sub
        val = 1.0 if val_ref is None else val_ref[pl.ds(e, 1), :]
        out.append(jnp.where(hit, val, 0.0))
    return out


def _dispatch_kernel(pe_ref, po_ref, goff_ref, used_ref, pad_start_ref, pad_cnt_ref,
                     key_ref, h_ref, xs_ref, xbuf, p_sc, zbuf, sem, zsem, *, ntiles):
    i = pl.program_id(0)
    hb = h_ref[...]
    ch = SORT_CHUNK
    ppc = ch // RUN_ALIGN
    zbuf[...] = jnp.zeros_like(zbuf)
    pads = _pad_copies(pad_start_ref, pad_cnt_ref, i, ntiles, zbuf, xs_ref, zsem)
    pads("start")

    copy, wait_all = _piece_copies(pe_ref, po_ref, goff_ref, used_ref, i, xbuf, xs_ref, sem, True)

    def build(ci):
        rows = _piece_rows(pe_ref, po_ref, key_ref, i * PIECES_PAD + ci * ppc, ppc)
        for jj, p in enumerate(rows):
            p_sc[jj * RUN_ALIGN:(jj + 1) * RUN_ALIGN, :] = p.astype(BF16)
        xbuf[pl.ds(pl.multiple_of(ci * ch, ch), ch), :] = _bdot(p_sc[...], hb).astype(BF16)

    def send(ci):
        for jj in range(ppc):
            copy(ci * ppc + jj).start(priority=jj % 2)

    def step(ci, carry):
        send(ci - 1)
        build(ci)
        return carry

    nchunk = used_ref[i] // ch
    build(0)
    lax.fori_loop(1, nchunk, step, 0)
    send(nchunk - 1)
    pads("wait")
    wait_all()


def _dispatch(meta, pe, po, key, hb, n_rows_max):
    t, d = hb.shape
    tm = TM_SORT
    goff, used, pad_start, pad_cnt = meta
    return pl.pallas_call(
        functools.partial(_dispatch_kernel, ntiles=t // tm),
        grid_spec=pltpu.PrefetchScalarGridSpec(
            num_scalar_prefetch=6, grid=(t // tm,),
            in_specs=[pl.BlockSpec((N_EXPERTS, tm), lambda i, *_: (0, i)),
                      pl.BlockSpec((tm, d), lambda i, *_: (i, 0))],
            out_specs=pl.BlockSpec(memory_space=pl.ANY),
            scratch_shapes=[pltpu.VMEM((ROWS_TILE, d), BF16), pltpu.VMEM((SORT_CHUNK, tm), BF16),
                            pltpu.VMEM((RUN_ALIGN, d), BF16),
                            pltpu.SemaphoreType.DMA(()), pltpu.SemaphoreType.DMA(())]),
        out_shape=jax.ShapeDtypeStruct((n_rows_max, d), BF16),
        compiler_params=_cparams(("arbitrary",)),
    )(pe, po, goff, used, pad_start, pad_cnt, key, hb)


def _expert_kernel(bx_ref, be_ref, bnew_ref, breal_ref, x_ref, wg_ref, wu_ref, wd_ref, o_ref,
                   wg_sc, wu_sc, wd_sc):
    b = pl.program_id(0)

    @pl.when(bnew_ref[b] == 1)
    def _():
        wg_sc[...] = wg_ref[0, 0].astype(BF16)
        wu_sc[...] = wu_ref[0, 0].astype(BF16)
        wd_sc[...] = wd_ref[0, 0].astype(BF16)

    @pl.when(breal_ref[b] == 0)
    def _():
        o_ref[...] = jnp.zeros_like(o_ref)

    @pl.when(breal_ref[b] == 1)
    def _():
        parts = [slice(r, r + BM_CHAIN) for r in range(0, BM_EXPERT, BM_CHAIN)]
        xs = [x_ref[rs, :] for rs in parts]
        gs = [_bdot(x, wg_sc[...]) for x in xs]
        us = [_bdot(x, wu_sc[...]) for x in xs]
        acts = [(_silu(g) * u).astype(BF16) for g, u in zip(gs, us)]
        for rs, act in zip(parts, acts):
            o_ref[rs, :] = _bdot(act, wd_sc[...]).astype(BF16)


def _experts(meta, xs, layer, w_gate, w_up, w_down):
    r, d = xs.shape
    ed = w_gate.shape[-1]
    bm = BM_EXPERT
    omap = lambda b, bx, be, bnew, breal: (b, 0)
    xmap = lambda b, bx, be, bnew, breal: (bx[b], 0)
    wmap = lambda b, bx, be, bnew, breal: (layer, be[b], 0, 0)
    return pl.pallas_call(
        _expert_kernel,
        grid_spec=pltpu.PrefetchScalarGridSpec(
            num_scalar_prefetch=4, grid=(r // bm,),
            in_specs=[pl.BlockSpec((bm, d), xmap), pl.BlockSpec((1, 1, d, ed), wmap),
                      pl.BlockSpec((1, 1, d, ed), wmap), pl.BlockSpec((1, 1, ed, d), wmap)],
            out_specs=pl.BlockSpec((bm, d), omap),
            scratch_shapes=[pltpu.VMEM((d, ed), BF16), pltpu.VMEM((d, ed), BF16),
                            pltpu.VMEM((ed, d), BF16)]),
        out_shape=jax.ShapeDtypeStruct((r, d), BF16),
        compiler_params=_cparams(("arbitrary",)),
    )(*meta, xs, w_gate, w_up, w_down)


def _fetch_tile(pe_ref, po_ref, goff_ref, tile, key_ref, wkey_ref, ys_ref, ybuf, q_sc, sem):
    ch = SORT_CHUNK
    ppc = ch // RUN_ALIGN
    npieces = ybuf.shape[0] // RUN_ALIGN
    last = ys_ref.shape[0] - RUN_ALIGN

    def copy(j):
        idx = tile * PIECES_PAD + j
        g0 = jnp.minimum(goff_ref[tile * N_EXPERTS + pe_ref[idx]] + po_ref[idx], last)
        return pltpu.make_async_copy(ys_ref.at[pl.ds(pl.multiple_of(g0, RUN_ALIGN), RUN_ALIGN)],
                                     ybuf.at[pl.ds(pl.multiple_of(j * RUN_ALIGN, RUN_ALIGN), RUN_ALIGN)], sem)

    def issue():
        for j in range(npieces):
            copy(j).start(priority=j % 2)

    def build():
        for ci in range(ybuf.shape[0] // ch):
            rows = _piece_rows(pe_ref, po_ref, key_ref, tile * PIECES_PAD + ci * ppc, ppc, wkey_ref)
            q_sc[:, ci * ch:(ci + 1) * ch] = jnp.concatenate(rows, axis=0).T.astype(BF16)

    def wait_all():
        def body(j, carry):
            copy(j).wait()
            return carry

        lax.fori_loop(0, npieces, body, 0)

    return issue, build, wait_all


def _combine_kernel(pe_ref, po_ref, goff_ref, key_ref, wkey_ref, keyn_ref, wkeyn_ref, ys_ref, base_ref,
                    g_ref, fn_ref, o_ref, ybuf_a, ybuf_b, q_a, q_b, sem_a, sem_b, *, final, ntiles):
    i = pl.program_id(0)
    nxt_tile = jnp.minimum(i + 1, ntiles - 1)
    slots = ((ybuf_a, q_a, sem_a), (ybuf_b, q_b, sem_b))

    @pl.when(i == 0)
    def _():
        issue, build, _ = _fetch_tile(pe_ref, po_ref, goff_ref, i, key_ref, wkey_ref, ys_ref, *slots[0])
        issue()
        build()

    for parity in range(2):
        @pl.when(i % 2 == parity)
        def _():
            ybuf, q_sc, sem = slots[parity]
            def drain(j, carry):
                pltpu.make_async_copy(ys_ref.at[pl.ds(0, RUN_ALIGN)], ybuf.at[pl.ds(0, RUN_ALIGN)], sem).wait()
                return carry

            lax.fori_loop(0, ybuf.shape[0] // RUN_ALIGN, drain, 0)
            issue, build, wait_next = _fetch_tile(pe_ref, po_ref, goff_ref, nxt_tile, keyn_ref, wkeyn_ref,
                                                  ys_ref, *slots[1 - parity])
            issue()
            out = base_ref[...] + g_ref[0] * _bdot(q_sc[...], ybuf[...])
            build()
            if final:
                out = _rms(out, fn_ref[...])
            o_ref[...] = out

            @pl.when(i == ntiles - 1)
            def _():
                wait_next()


def _combine(meta, pe, po, key, wkey, ys, base, gt, fnorm, seq, final):
    t, d = base.shape
    tm = TM_SORT
    nt = seq // tm
    ntiles = t // tm
    cur = pl.BlockSpec((N_EXPERTS, tm), lambda i, *_: (0, i))
    nxt = pl.BlockSpec((N_EXPERTS, tm), lambda i, *_: (0, jnp.minimum(i + 1, ntiles - 1)))
    return pl.pallas_call(
        functools.partial(_combine_kernel, final=final, ntiles=ntiles),
        grid_spec=pltpu.PrefetchScalarGridSpec(
            num_scalar_prefetch=3, grid=(ntiles,),
            in_specs=[cur, cur, nxt, nxt,
                      pl.BlockSpec(memory_space=pl.ANY),
                      pl.BlockSpec((tm, d), lambda i, *_: (i, 0)),
                      pl.BlockSpec((1, 1, d), lambda i, *_: (i // nt, 0, 0)),
                      pl.BlockSpec((1, d), lambda i, *_: (0, 0))],
            out_specs=pl.BlockSpec((tm, d), lambda i, *_: (i, 0)),
            scratch_shapes=[pltpu.VMEM((ROWS_TILE, d), BF16), pltpu.VMEM((ROWS_TILE, d), BF16),
                            pltpu.VMEM((tm, ROWS_TILE), BF16), pltpu.VMEM((tm, ROWS_TILE), BF16),
                            pltpu.SemaphoreType.DMA(()), pltpu.SemaphoreType.DMA(())]),
        out_shape=jax.ShapeDtypeStruct((t, d), F32),
        compiler_params=_cparams(("arbitrary",)),
    )(pe, po, meta[0], key, wkey, key, wkey, ys, base, gt, fnorm.reshape(1, d))


def _moe(x, sh, sc, gt, nw, w_router, e_bias, layer, w_gate, w_up, w_down, wsg, wsu, wsd, fnorm, final):
    bsz, s, d = x.shape
    t = bsz * s
    n_rows_max = _rows_max(t)
    hb, base, key, wkey, cnt, pe, po = _router(x, sh, sc, gt, nw, w_router, e_bias, wsg, wsu, wsd)
    pe = pe.reshape(-1)
    po = po.reshape(-1)
    layout_meta, block_meta = _sort_meta(cnt, n_rows_max)
    xs = _dispatch(layout_meta, pe, po, key, hb, n_rows_max)
    ys = _experts(block_meta, xs, layer, w_gate, w_up, w_down)
    out = _combine(layout_meta, pe, po, key, wkey, ys, base, gt, fnorm, s, final)
    return out.reshape(bsz, s, d)


def _kv_kernel(x_ref, sh_ref, sc_ref, nw_ref, wk_ref, wvt_ref, k_ref, vt_ref, km_ref):
    x = x_ref[0]
    hb = (_rms(x, nw_ref[...]) * (1.0 + sc_ref[0]) + sh_ref[0]).astype(BF16)
    k = _bdot(hb, wk_ref[...])
    vt = _bdot_nt(wvt_ref[...], hb)
    for p in range(k.shape[1] // LANES):
        k_ref[0, p, 0] = k[:, p * LANES:(p + 1) * LANES].astype(BF16)
        vt_ref[0, p, 0] = vt[p * LANES:(p + 1) * LANES, :].astype(BF16)
    km_ref[0, 0] = jnp.mean(k, axis=0, keepdims=True)


def _kv(x, sh, sc, nw, w_k, w_v):
    bsz, s, d = x.shape
    nb = s // B_BLOCK
    npair = d // LANES
    vec = pl.BlockSpec((1, 1, d), lambda b, j: (b, 0, 0))
    full2 = lambda shape: pl.BlockSpec(shape, lambda b, j: (0, 0))
    return pl.pallas_call(
        _kv_kernel,
        grid=(bsz, nb),
        in_specs=[pl.BlockSpec((1, B_BLOCK, d), lambda b, j: (b, j, 0)), vec, vec,
                  full2((1, d)), full2((d, d)), full2((d, d))],
        out_specs=[pl.BlockSpec((1, npair, 1, B_BLOCK, LANES), lambda b, j: (b, 0, j, 0, 0)),
                   pl.BlockSpec((1, npair, 1, LANES, B_BLOCK), lambda b, j: (b, 0, j, 0, 0)),
                   pl.BlockSpec((1, 1, 1, d), lambda b, j: (b, j, 0, 0))],
        out_shape=[jax.ShapeDtypeStruct((bsz, npair, nb, B_BLOCK, LANES), BF16),
                   jax.ShapeDtypeStruct((bsz, npair, nb, LANES, B_BLOCK), BF16),
                   jax.ShapeDtypeStruct((bsz, nb, 1, d), F32)],
        compiler_params=_cparams(("arbitrary", "arbitrary")),
    )(x, sh, sc, nw.reshape(1, d), w_k.astype(BF16), w_v.T.astype(BF16))


def _attn_kernel(x_ref, sh_ref, sc_ref, g_ref, nw_ref, wqt_ref, wo_ref, k_ref, vt_ref, km_ref,
                 o_ref, qt_sc, qs_sc, acc_sc, sel_sc, m_sc, l_sc, *, nb, n_sel):
    qb = pl.program_id(1)
    x = x_ref[0]
    bq = x.shape[0]
    npair = qt_sc.shape[0]
    nbp = km_ref.shape[2]
    hd = LANES // 2
    scale = float(hd) ** -0.5 * 1.4426950408889634
    h = _rms(x, nw_ref[...]) * (1.0 + sc_ref[0]) + sh_ref[0]
    qt = _bdot_nt(wqt_ref[...], h.astype(BF16))
    for p in range(npair):
        qt_sc[p] = qt[p * LANES:(p + 1) * LANES, :]

    subn = lax.broadcasted_iota(I32, (nbp, bq), 0)
    past = subn < qb
    krow = lax.broadcasted_iota(I32, (B_BLOCK, bq), 0)
    qcol = lax.broadcasted_iota(I32, (B_BLOCK, bq), 1)
    causal = krow <= qcol
    rowh = lax.broadcasted_iota(I32, (LANES, 1), 0)

    grp = ATTN_PAIR_UNROLL
    heads = [(u, e) for u in range(grp) for e in range(2)]
    ones_rows = jnp.ones((SUM_ROWS, B_BLOCK), BF16)

    def vsum(vt2, e):
        return jnp.concatenate([vt2[e * hd:(e + 1) * hd, :], ones_rows], axis=0)

    def own_body(gi, carry):
        ps = [gi * grp + u for u in range(grp)]
        q2ts = [qt_sc[p] for p in ps]
        kms = [km_ref[0, p] for p in ps]
        kown = [k_ref[0, p, qb] for p in ps]
        vown = [vt_ref[0, p, qb] for p in ps]
        qets = [jnp.where((rowh >= hd) if e == 1 else (rowh < hd), q2ts[u], 0.0) for u, e in heads]
        qsts = [(q * scale).astype(BF16) for q in qets]
        ss = [jnp.where(causal, _bdot(kown[u], qsts[i]), NEG_INF) for i, (u, e) in enumerate(heads)]
        gates = [_dot3(kms[u], qets[i]) for i, (u, e) in enumerate(heads)]
        ms = [jnp.max(s, axis=0, keepdims=True) for s in ss]
        pes = [jnp.exp2(s - m) for s, m in zip(ss, ms)]
        pvs = [_bdot(vsum(vown[u], e), pes[i].astype(BF16)) for i, (u, e) in enumerate(heads)]
        accs = [pv[:hd] for pv in pvs]
        ls = [pv[hd:hd + 1] for pv in pvs]
        sels = []
        for gate in gates:
            selt = jnp.zeros((nbp, bq), F32)
            for n in range(nb):
                gn = gate[n:n + 1, :]
                beats = jnp.logical_or(gate > gn, jnp.logical_and(gate == gn, subn < n))
                beats = jnp.logical_and(beats, past)
                cnt = jnp.sum(beats.astype(F32), axis=0, keepdims=True)
                selt = jnp.where(subn == n, (cnt < n_sel).astype(F32), selt)
            sels.append(selt)
        for i, (u, e) in enumerate(heads):
            p = ps[u]
            sel_sc[p, e] = sels[i]
            qs_sc[p, e] = qsts[i]
            m_sc[p, e] = ms[i]
            l_sc[p, e] = ls[i]
            acc_sc[p, e * hd:(e + 1) * hd, :] = accs[i]
        return carry

    lax.fori_loop(0, npair // grp, own_body, 0)

    def kb_body(kb, carry):
        def group_body(gi, c2):
            ps = [gi * grp + u for u in range(grp)]
            kbl = [k_ref[0, p, kb] for p in ps]
            vbl = [vt_ref[0, p, kb] for p in ps]
            qsts = [qs_sc[ps[u], e] for u, e in heads]
            rows = [sel_sc[ps[u], e, pl.ds(kb, 1), :] for u, e in heads]
            m_old = [m_sc[ps[u], e] for u, e in heads]
            l_old = [l_sc[ps[u], e] for u, e in heads]
            a_old = [acc_sc[ps[u], e * hd:(e + 1) * hd, :] for u, e in heads]
            ss = [jnp.where(rows[i] > 0.5, _bdot(kbl[u], qsts[i]), NEG_INF)
                  for i, (u, e) in enumerate(heads)]
            m_new = [jnp.maximum(m, jnp.max(s, axis=0, keepdims=True)) for m, s in zip(m_old, ss)]
            alphas = [jnp.exp2(m - mn) for m, mn in zip(m_old, m_new)]
            pes = [jnp.exp2(s - mn) for s, mn in zip(ss, m_new)]
            pvs = [_bdot(vsum(vbl[u], e), pes[i].astype(BF16)) for i, (u, e) in enumerate(heads)]
            l_new = [a * l + pv[hd:hd + 1] for a, l, pv in zip(alphas, l_old, pvs)]
            a_new = [a * ao + pv[:hd] for a, ao, pv in zip(alphas, a_old, pvs)]
            for i, (u, e) in enumerate(heads):
                p = ps[u]
                m_sc[p, e] = m_new[i]
                l_sc[p, e] = l_new[i]
                acc_sc[p, e * hd:(e + 1) * hd, :] = a_new[i]
            return c2

        lax.fori_loop(0, npair // grp, group_body, 0)
        return carry

    lax.fori_loop(0, qb, kb_body, 0)

    parts = []
    for p in range(npair):
        for e in range(2):
            parts.append(acc_sc[p, e * hd:(e + 1) * hd, :] / l_sc[p, e])
    ot = jnp.concatenate(parts, axis=0)
    o_ref[0] = x + g_ref[0] * _bdot(ot.T.astype(BF16), wo_ref[...])


def _attn(x, sh, sc, gt, nw, w_q, w_o, k5, vt5, km2):
    bsz, s, d = x.shape
    nb = s // B_BLOCK
    npair = d // LANES
    nbp = km2.shape[2]
    n_sel = min(B_TOPK, nb - 1)
    vec = pl.BlockSpec((1, 1, d), lambda b, j: (b, 0, 0))
    full2 = lambda shape: pl.BlockSpec(shape, lambda b, j: (0, 0))
    return pl.pallas_call(
        functools.partial(_attn_kernel, nb=nb, n_sel=n_sel),
        grid=(bsz, nb),
        in_specs=[pl.BlockSpec((1, B_BLOCK, d), lambda b, j: (b, j, 0)), vec, vec, vec,
                  full2((1, d)), full2((d, d)), full2((d, d)),
                  pl.BlockSpec((1, npair, nb, B_BLOCK, LANES), lambda b, j: (b, 0, 0, 0, 0)),
                  pl.BlockSpec((1, npair, nb, LANES, B_BLOCK), lambda b, j: (b, 0, 0, 0, 0)),
                  pl.BlockSpec((1, npair, nbp, LANES), lambda b, j: (b, 0, 0, 0))],
        out_specs=pl.BlockSpec((1, B_BLOCK, d), lambda b, j: (b, j, 0)),
        out_shape=jax.ShapeDtypeStruct((bsz, s, d), F32),
        scratch_shapes=[pltpu.VMEM((npair, LANES, B_BLOCK), F32),
                        pltpu.VMEM((npair, 2, LANES, B_BLOCK), BF16),
                        pltpu.VMEM((npair, LANES, B_BLOCK), F32),
                        pltpu.VMEM((npair, 2, nbp, B_BLOCK), F32),
                        pltpu.VMEM((npair, 2, 1, B_BLOCK), F32),
                        pltpu.VMEM((npair, 2, 1, B_BLOCK), F32)],
        compiler_params=_cparams(("arbitrary", "arbitrary")),
    )(x, sh, sc, gt, nw.reshape(1, d), w_q.T.astype(BF16), w_o.astype(BF16), k5, vt5, km2)


def kernel(x, c, ada_w, ada_b, norm_mix, norm_ffn, a_w_in, a_b_in, a_ln_g, a_ln_b, a_w_s, a_b_s,
           a_w_out, kv_norm, kv_ada_w, kv_ada_b, kv_w_k, kv_w_v, b_w_q, b_w_o, moe_router, moe_bias,
           moe_w_gate, moe_w_up, moe_w_down, sh_w_gate, sh_w_up, sh_w_down, final_norm):
    bsz, s, d = x.shape
    depth = ada_w.shape[0]
    n_a = a_w_in.shape[0]
    assert s % B_BLOCK == 0 and s % TM_GMLP == 0 and d % LANES == 0
    nb = s // B_BLOCK
    npair = d // LANES
    nbp = -(-nb // 8) * 8

    def split(m, n):
        return [m[:, i * d:(i + 1) * d].reshape(bsz, 1, d) for i in range(n)]

    layer_mods = _ada(c, ada_w, ada_b)
    k5 = vt5 = km2 = None
    for i in range(depth):
        sh1, sc1, g1, sh2, sc2, g2 = split(layer_mods[i], 6)
        if i < n_a:
            x = _gmlp(x, sh1, sc1, g1, norm_mix[i], a_w_in[i], a_b_in[i], a_ln_g[i], a_ln_b[i],
                      a_w_s[i], a_b_s[i], a_w_out[i])
        else:
            if k5 is None:
                ksh, ksc = split(_ada(c, kv_ada_w[None], kv_ada_b[None])[0], 2)
                k5, vt5, km = _kv(x, ksh, ksc, kv_norm, kv_w_k, kv_w_v)
                km = km.reshape(bsz, nb, npair, LANES).transpose(0, 2, 1, 3)
                km2 = jnp.pad(km, ((0, 0), (0, 0), (0, nbp - nb), (0, 0)))
            j = i - n_a
            x = _attn(x, sh1, sc1, g1, norm_mix[i], b_w_q[j], b_w_o[j], k5, vt5, km2)
        x = _moe(x, sh2, sc2, g2, norm_ffn[i], moe_router[i], moe_bias[i], i, moe_w_gate,
                 moe_w_up, moe_w_down, sh_w_gate[i], sh_w_up[i], sh_w_down[i],
                 final_norm, i == depth - 1)
    return x
```

```python
import functools

import jax
import jax.numpy as jnp
from jax import lax
from jax.experimental import pallas as pl
from jax.experimental.pallas import tpu as pltpu

F32 = jnp.float32
BF16 = jnp.bfloat16
I32 = jnp.int32

RMS_EPS = 1e-6
LN_EPS = 1e-5
NEG_INF = -1e30

A_CHUNK = 128
A_GROUPS = 8
B_HEADS = 16
B_BLOCK = 256
B_TOPK = 3
N_EXPERTS = 64
TOP_K = 8
N_GROUPS = 8
TOPK_GROUPS = 4
ROUTED_SCALE = 2.5

LANES = 128
VMEM_LIMIT = 56 * 1024 * 1024

TM_GMLP = 256
TM_SORT = 512
RUN_ALIGN = 16
SORT_CHUNK = 256
ROWS_TILE_USED = TM_SORT * TOP_K + N_EXPERTS * (RUN_ALIGN - 1)
ROWS_TILE = -(-ROWS_TILE_USED // SORT_CHUNK) * SORT_CHUNK
PIECES_TILE = ROWS_TILE // RUN_ALIGN
PIECES_PAD = -(-PIECES_TILE // LANES) * LANES
BM_EXPERT = 1024
BM_CHAIN = 256
ATTN_PAIR_UNROLL = 4
SUM_ROWS = 16


def _cparams(sem):
    return pltpu.CompilerParams(dimension_semantics=sem, vmem_limit_bytes=VMEM_LIMIT)


def _sigmoid(x):
    return 1.0 / (1.0 + jnp.exp(-x))


def _silu(x):
    return x * _sigmoid(x)


def _gelu_tanh(x):
    return 0.5 * x * (1.0 + jnp.tanh(0.7978845608028654 * (x + 0.044715 * (x * x * x))))


def _rms(x, g):
    return x * lax.rsqrt(jnp.mean(x * x, axis=-1, keepdims=True) + RMS_EPS) * g


def _bdot(a, b):
    return jnp.dot(a, b, preferred_element_type=F32)


def _bdot_nt(a, b):
    return lax.dot_general(a, b, (((1,), (1,)), ((), ())), preferred_element_type=F32)


def _split(a):
    hi = a.astype(BF16)
    lo = (a - hi.astype(F32)).astype(BF16)
    return hi, lo


def _dot3_nt(a, b):
    ah, al = _split(a)
    bh, bl = _split(b)
    return _bdot_nt(ah, bh) + (_bdot_nt(ah, bl) + _bdot_nt(al, bh))


def _dot3(a, b):
    ah, al = _split(a)
    bh, bl = _split(b)
    return _bdot(ah, bh) + (_bdot(ah, bl) + _bdot(al, bh))


def _ada_kernel(c_ref, w_ref, b_ref, o_ref):
    a = _silu(c_ref[...]).astype(BF16)
    o_ref[0] = _bdot(a, w_ref[0].astype(BF16)) + b_ref[0]


def _ada(c, w, b):
    bsz, d = c.shape
    nl, _, n = w.shape
    tn = 1024
    return pl.pallas_call(
        _ada_kernel,
        grid=(nl, n // tn),
        in_specs=[pl.BlockSpec((bsz, d), lambda l, j: (0, 0)),
                  pl.BlockSpec((1, d, tn), lambda l, j: (l, 0, j)),
                  pl.BlockSpec((1, 1, tn), lambda l, j: (l, 0, j))],
        out_specs=pl.BlockSpec((1, bsz, tn), lambda l, j: (l, 0, j)),
        out_shape=jax.ShapeDtypeStruct((nl, bsz, n), F32),
        compiler_params=_cparams(("arbitrary", "arbitrary")),
    )(c, w, b.reshape(nl, 1, n))


def _gmlp_kernel(x_ref, sh_ref, sc_ref, g_ref, nw_ref, win_ref, bin_ref, lng_ref, lnb_ref,
                 ws_ref, bst_ref, wout_ref, o_ref, y_sc):
    x = x_ref[0]
    tm = x.shape[0]
    h = _rms(x, nw_ref[...]) * (1.0 + sc_ref[0]) + sh_ref[0]
    z = _gelu_tanh(_bdot(h.astype(BF16), win_ref[...]) + bin_ref[...])
    aw = z.shape[1] // 2
    gd = aw // A_GROUPS
    u = z[:, :aw]
    v = z[:, aw:]
    mu = jnp.mean(v, axis=-1, keepdims=True)
    dv = v - mu
    var = jnp.mean(dv * dv, axis=-1, keepdims=True)
    vn = (dv * lax.rsqrt(var + LN_EPS) * lng_ref[...] + lnb_ref[...]).astype(BF16)
    row = lax.broadcasted_iota(I32, (A_CHUNK, A_CHUNK), 0)
    col = lax.broadcasted_iota(I32, (A_CHUNK, A_CHUNK), 1)
    causal = col <= row
    for g in range(A_GROUPS):
        wg = jnp.where(causal, ws_ref[g], 0.0).astype(BF16)
        bcol = bst_ref[:, g:g + 1]
        for ci in range(tm // A_CHUNK):
            rs = slice(ci * A_CHUNK, (ci + 1) * A_CHUNK)
            cs = slice(g * gd, (g + 1) * gd)
            sv = _bdot(wg, vn[rs, cs]) + bcol
            y_sc[rs, cs] = (u[rs, cs] * sv).astype(BF16)
    o_ref[0] = x + g_ref[0] * _bdot(y_sc[...], wout_ref[...])


def _gmlp(x, sh, sc, gt, nw, w_in, b_in, ln_g, ln_b, w_s, b_s, w_out):
    bsz, s, d = x.shape
    tm = TM_GMLP
    n_in = w_in.shape[1]
    aw = n_in // 2
    vec = pl.BlockSpec((1, 1, d), lambda b, j: (b, 0, 0))
    full2 = lambda shape: pl.BlockSpec(shape, lambda b, j: (0, 0))
    return pl.pallas_call(
        _gmlp_kernel,
        grid=(bsz, s // tm),
        in_specs=[pl.BlockSpec((1, tm, d), lambda b, j: (b, j, 0)), vec, vec, vec,
                  full2((1, d)), full2((d, n_in)), full2((1, n_in)), full2((1, aw)), full2((1, aw)),
                  pl.BlockSpec((A_GROUPS, A_CHUNK, A_CHUNK), lambda b, j: (0, 0, 0)),
                  full2((A_CHUNK, A_GROUPS)), full2((aw, d))],
        out_specs=pl.BlockSpec((1, tm, d), lambda b, j: (b, j, 0)),
        out_shape=jax.ShapeDtypeStruct((bsz, s, d), F32),
        scratch_shapes=[pltpu.VMEM((tm, aw), BF16)],
        compiler_params=_cparams(("arbitrary", "arbitrary")),
    )(x, sh, sc, gt, nw.reshape(1, d), w_in.astype(BF16), b_in.reshape(1, n_in),
      ln_g.reshape(1, aw), ln_b.reshape(1, aw), w_s, b_s.T, w_out.astype(BF16))


def _router_kernel(x_ref, sh_ref, sc_ref, g_ref, nw_ref, wrt_ref, bias_ref, wsg_ref, wsu_ref, wsd_ref,
                   h_ref, base_ref, key_ref, wkey_ref, cnt_ref, pe_ref, po_ref):
    x = x_ref[0]
    tm = x.shape[0]
    h = _rms(x, nw_ref[...]) * (1.0 + sc_ref[0]) + sh_ref[0]
    hb = h.astype(BF16)
    h_ref[...] = hb
    act = (_silu(_bdot(hb, wsg_ref[...])) * _bdot(hb, wsu_ref[...])).astype(BF16)
    base_ref[...] = x + g_ref[0] * _bdot(act, wsd_ref[...])

    scores = _sigmoid(_dot3_nt(wrt_ref[...], h))
    choice = scores + bias_ref[...]
    gsz = N_EXPERTS // N_GROUPS
    sub = lax.broadcasted_iota(I32, (gsz, tm), 0)
    blocks = [choice[g * gsz:(g + 1) * gsz] for g in range(N_GROUPS)]
    gscore = []
    for blk in blocks:
        m1 = jnp.max(blk, axis=0, keepdims=True)
        i1 = jnp.min(jnp.where(blk == m1, sub, gsz), axis=0, keepdims=True)
        m2 = jnp.max(jnp.where(sub == i1, -jnp.inf, blk), axis=0, keepdims=True)
        gscore.append(m1 + m2)
    masked = []
    for g in range(N_GROUPS):
        beats = jnp.zeros((1, tm), F32)
        for m in range(N_GROUPS):
            if m == g:
                continue
            b = gscore[m] > gscore[g]
            if m < g:
                b = jnp.logical_or(b, gscore[m] == gscore[g])
            beats = beats + b.astype(F32)
        masked.append(jnp.where(beats < TOPK_GROUPS, blocks[g], NEG_INF))
    cur = jnp.concatenate(masked, axis=0)
    eio = lax.broadcasted_iota(I32, (N_EXPERTS, tm), 0)
    chosen = jnp.zeros((N_EXPERTS, tm), jnp.bool_)
    wsum = jnp.zeros((1, tm), F32)
    for _ in range(TOP_K):
        m = jnp.max(cur, axis=0, keepdims=True)
        idx = jnp.min(jnp.where(cur == m, eio, N_EXPERTS), axis=0, keepdims=True)
        sel = eio == idx
        chosen = jnp.logical_or(chosen, sel)
        wsum = wsum + jnp.sum(jnp.where(sel, scores, 0.0), axis=0, keepdims=True)
        cur = jnp.where(sel, -jnp.inf, cur)
    wkey_ref[...] = jnp.where(chosen, scores / wsum * ROUTED_SCALE, 0.0)

    onehot = chosen.astype(BF16)
    r_i = lax.broadcasted_iota(I32, (tm, tm), 0)
    c_i = lax.broadcasted_iota(I32, (tm, tm), 1)
    before = (r_i < c_i).astype(BF16)
    prior = _bdot(onehot, before)
    key_ref[...] = jnp.where(chosen, prior, -1.0).astype(I32)
    cnt = jnp.sum(chosen.astype(F32), axis=1, keepdims=True)
    cnt_ref[0] = cnt

    run_p = jnp.floor((cnt + (RUN_ALIGN - 1)) / RUN_ALIGN)
    ppc = SORT_CHUNK // RUN_ALIGN
    tot = jnp.sum(run_p, axis=0, keepdims=True)
    fill = jnp.ceil(tot / ppc) * ppc - tot
    run_p = run_p + jnp.where(lax.broadcasted_iota(I32, (N_EXPERTS, 1), 0) == N_EXPERTS - 1, fill, 0.0)
    e_r = lax.broadcasted_iota(I32, (N_EXPERTS, N_EXPERTS), 0)
    e_c = lax.broadcasted_iota(I32, (N_EXPERTS, N_EXPERTS), 1)
    incl = (e_c <= e_r).astype(BF16)
    lend = _bdot(incl, jnp.broadcast_to(run_p, (N_EXPERTS, LANES)).astype(BF16))[:, 0:1]
    loff = lend - run_p
    pj = lax.broadcasted_iota(I32, (N_EXPERTS, PIECES_PAD), 1).astype(F32)
    er = lax.broadcasted_iota(I32, (N_EXPERTS, PIECES_PAD), 0).astype(F32)
    pe = jnp.minimum(jnp.sum((lend <= pj).astype(F32), axis=0, keepdims=True), N_EXPERTS - 1.0)
    lo = jnp.sum(jnp.where(er == pe, loff, 0.0), axis=0, keepdims=True)
    pe_ref[0] = pe.astype(I32)
    po_ref[0] = ((pj[0:1, :] - lo) * RUN_ALIGN).astype(I32)


def _router(x, sh, sc, gt, nw, w_router, e_bias, wsg, wsu, wsd):
    bsz, s, d = x.shape
    t = bsz * s
    tm = TM_SORT
    nt = s // tm
    sd = wsg.shape[1]
    vec = pl.BlockSpec((1, 1, d), lambda b, j: (b, 0, 0))
    full2 = lambda shape: pl.BlockSpec(shape, lambda b, j: (0, 0))
    tok = pl.BlockSpec((tm, d), lambda b, j: (b * nt + j, 0))
    etok = pl.BlockSpec((N_EXPERTS, tm), lambda b, j: (0, b * nt + j))
    ptab = pl.BlockSpec((1, 1, PIECES_PAD), lambda b, j: (b * nt + j, 0, 0))
    return pl.pallas_call(
        _router_kernel,
        grid=(bsz, nt),
        in_specs=[pl.BlockSpec((1, tm, d), lambda b, j: (b, j, 0)), vec, vec, vec,
                  full2((1, d)), full2((N_EXPERTS, d)), full2((N_EXPERTS, 1)),
                  full2((d, sd)), full2((d, sd)), full2((sd, d))],
        out_specs=[tok, tok, etok, etok,
                   pl.BlockSpec((1, N_EXPERTS, 1), lambda b, j: (b * nt + j, 0, 0)), ptab, ptab],
        out_shape=[jax.ShapeDtypeStruct((t, d), BF16), jax.ShapeDtypeStruct((t, d), F32),
                   jax.ShapeDtypeStruct((N_EXPERTS, t), I32), jax.ShapeDtypeStruct((N_EXPERTS, t), F32),
                   jax.ShapeDtypeStruct((t // tm, N_EXPERTS, 1), F32),
                   jax.ShapeDtypeStruct((t // tm, 1, PIECES_PAD), I32),
                   jax.ShapeDtypeStruct((t // tm, 1, PIECES_PAD), I32)],
        compiler_params=_cparams(("arbitrary", "arbitrary")),
    )(x, sh, sc, gt, nw.reshape(1, d), w_router.T, e_bias.reshape(N_EXPERTS, 1),
      wsg.astype(BF16), wsu.astype(BF16), wsd.astype(BF16))


def _rows_max(t):
    rows = (t // TM_SORT) * ROWS_TILE + N_EXPERTS * (BM_EXPERT - RUN_ALIGN)
    return -(-rows // BM_EXPERT) * BM_EXPERT


def _sort_meta(cnt, n_rows_max):
    nts = cnt.shape[0]
    c = cnt.reshape(nts, N_EXPERTS).astype(I32)
    run = (c + (RUN_ALIGN - 1)) // RUN_ALIGN * RUN_ALIGN
    fill = (-jnp.sum(run, axis=1, keepdims=True)) % SORT_CHUNK
    run = jnp.concatenate([run[:, :-1], run[:, -1:] + fill], axis=1)
    used = jnp.sum(run, axis=1)
    per_e = jnp.sum(run, axis=0)
    seg = (per_e + (BM_EXPERT - 1)) // BM_EXPERT * BM_EXPERT
    ends = jnp.cumsum(seg)
    starts = ends - seg
    goff = starts[None, :] + jnp.cumsum(run, axis=0) - run
    pad_start = starts + per_e
    pad_cnt = (seg - per_e) // RUN_ALIGN

    nb = n_rows_max // BM_EXPERT
    blo = jnp.arange(nb, dtype=I32) * BM_EXPERT
    be = jnp.sum((ends[None, :] <= blo[:, None]).astype(I32), axis=1)
    breal = (be < N_EXPERTS).astype(I32)
    be = jnp.minimum(be, N_EXPERTS - 1).astype(I32)
    prev = jnp.concatenate([jnp.full((1,), -1, I32), be[:-1]])
    bnew = (be != prev).astype(I32)
    bx = jnp.where(breal == 1, jnp.arange(nb, dtype=I32), 0)
    i32 = lambda a: a.reshape(-1).astype(I32)
    return (i32(goff), i32(used), i32(pad_start), i32(pad_cnt)), (bx, be, bnew, breal)


def _piece_copies(pe_ref, po_ref, goff_ref, used_ref, tile, local, remote, sem, to_remote):
    def copy(j):
        idx = tile * PIECES_PAD + j
        l0 = pl.multiple_of(j * RUN_ALIGN, RUN_ALIGN)
        g0 = pl.multiple_of(goff_ref[tile * N_EXPERTS + pe_ref[idx]] + po_ref[idx], RUN_ALIGN)
        lref = local.at[pl.ds(l0, RUN_ALIGN)]
        gref = remote.at[pl.ds(g0, RUN_ALIGN)]
        return pltpu.make_async_copy(lref, gref, sem) if to_remote else pltpu.make_async_copy(gref, lref, sem)

    def wait_all():
        def body(j, carry):
            copy(j).wait()
            return carry

        lax.fori_loop(0, used_ref[tile] // RUN_ALIGN, body, 0)

    return copy, wait_all


def _pad_copies(pad_start_ref, pad_cnt_ref, tile, ntiles, zeros, remote, sem):
    share = -(-N_EXPERTS // ntiles)

    def apply(act):
        for q in range(share):
            e = tile * share + q
            ec = jnp.minimum(e, N_EXPERTS - 1)
            n = jnp.where(e < N_EXPERTS, pad_cnt_ref[ec], 0)
            base = pad_start_ref[ec]

            def body(j, carry):
                g0 = pl.multiple_of(base + j * RUN_ALIGN, RUN_ALIGN)
                getattr(pltpu.make_async_copy(zeros, remote.at[pl.ds(g0, RUN_ALIGN)], sem), act)()
                return carry

            lax.fori_loop(0, n, body, 0)

    return apply


def _piece_rows(pe_ref, po_ref, key_ref, first_piece, npieces, val_ref=None):
    tm = key_ref.shape[1]
    sub = lax.broadcasted_iota(I32, (RUN_ALIGN, tm), 0)
    out = []
    for jj in range(npieces):
        j = first_piece + jj
        e = pe_ref[j]
        hit = (key_ref[pl.ds(e, 1), :] - po_ref[j]) == sub
        val = 1.0 if val_ref is None else val_ref[pl.ds(e, 1), :]
        out.append(jnp.where(hit, val, 0.0))
    return out


def _dispatch_kernel(pe_ref, po_ref, goff_ref, used_ref, pad_start_ref, pad_cnt_ref,
                     key_ref, h_ref, xs_ref, xbuf, p_sc, zbuf, sem, zsem, *, ntiles):
    i = pl.program_id(0)
    hb = h_ref[...]
    ch = SORT_CHUNK
    ppc = ch // RUN_ALIGN
    zbuf[...] = jnp.zeros_like(zbuf)
    pads = _pad_copies(pad_start_ref, pad_cnt_ref, i, ntiles, zbuf, xs_ref, zsem)
    pads("start")

    copy, wait_all = _piece_copies(pe_ref, po_ref, goff_ref, used_ref, i, xbuf, xs_ref, sem, True)

    def build(ci):
        rows = _piece_rows(pe_ref, po_ref, key_ref, i * PIECES_PAD + ci * ppc, ppc)
        for jj, p in enumerate(rows):
            p_sc[jj * RUN_ALIGN:(jj + 1) * RUN_ALIGN, :] = p.astype(BF16)
        xbuf[pl.ds(pl.multiple_of(ci * ch, ch), ch), :] = _bdot(p_sc[...], hb).astype(BF16)

    def send(ci):
        for jj in range(ppc):
            copy(ci * ppc + jj).start()

    def step(ci, carry):
        send(ci - 1)
        build(ci)
        return carry

    nchunk = used_ref[i] // ch
    build(0)
    lax.fori_loop(1, nchunk, step, 0)
    send(nchunk - 1)
    pads("wait")
    wait_all()


def _dispatch(meta, pe, po, key, hb, n_rows_max):
    t, d = hb.shape
    tm = TM_SORT
    goff, used, pad_start, pad_cnt = meta
    return pl.pallas_call(
        functools.partial(_dispatch_kernel, ntiles=t // tm),
        grid_spec=pltpu.PrefetchScalarGridSpec(
            num_scalar_prefetch=6, grid=(t // tm,),
            in_specs=[pl.BlockSpec((N_EXPERTS, tm), lambda i, *_: (0, i)),
                      pl.BlockSpec((tm, d), lambda i, *_: (i, 0))],
            out_specs=pl.BlockSpec(memory_space=pl.ANY),
            scratch_shapes=[pltpu.VMEM((ROWS_TILE, d), BF16), pltpu.VMEM((SORT_CHUNK, tm), BF16),
                            pltpu.VMEM((RUN_ALIGN, d), BF16),
                            pltpu.SemaphoreType.DMA(()), pltpu.SemaphoreType.DMA(())]),
        out_shape=jax.ShapeDtypeStruct((n_rows_max, d), BF16),
        compiler_params=_cparams(("arbitrary",)),
    )(pe, po, goff, used, pad_start, pad_cnt, key, hb)


def _expert_kernel(bx_ref, be_ref, bnew_ref, breal_ref, x_ref, wg_ref, wu_ref, wd_ref, o_ref,
                   wg_sc, wu_sc, wd_sc):
    b = pl.program_id(0)

    @pl.when(bnew_ref[b] == 1)
    def _():
        wg_sc[...] = wg_ref[0, 0].astype(BF16)
        wu_sc[...] = wu_ref[0, 0].astype(BF16)
        wd_sc[...] = wd_ref[0, 0].astype(BF16)

    @pl.when(breal_ref[b] == 0)
    def _():
        o_ref[...] = jnp.zeros_like(o_ref)

    @pl.when(breal_ref[b] == 1)
    def _():
        parts = [slice(r, r + BM_CHAIN) for r in range(0, BM_EXPERT, BM_CHAIN)]
        xs = [x_ref[rs, :] for rs in parts]
        gs = [_bdot(x, wg_sc[...]) for x in xs]
        us = [_bdot(x, wu_sc[...]) for x in xs]
        acts = [(_silu(g) * u).astype(BF16) for g, u in zip(gs, us)]
        for rs, act in zip(parts, acts):
            o_ref[rs, :] = _bdot(act, wd_sc[...]).astype(BF16)


def _experts(meta, xs, layer, w_gate, w_up, w_down):
    r, d = xs.shape
    ed = w_gate.shape[-1]
    bm = BM_EXPERT
    omap = lambda b, bx, be, bnew, breal: (b, 0)
    xmap = lambda b, bx, be, bnew, breal: (bx[b], 0)
    wmap = lambda b, bx, be, bnew, breal: (layer, be[b], 0, 0)
    return pl.pallas_call(
        _expert_kernel,
        grid_spec=pltpu.PrefetchScalarGridSpec(
            num_scalar_prefetch=4, grid=(r // bm,),
            in_specs=[pl.BlockSpec((bm, d), xmap), pl.BlockSpec((1, 1, d, ed), wmap),
                      pl.BlockSpec((1, 1, d, ed), wmap), pl.BlockSpec((1, 1, ed, d), wmap)],
            out_specs=pl.BlockSpec((bm, d), omap),
            scratch_shapes=[pltpu.VMEM((d, ed), BF16), pltpu.VMEM((d, ed), BF16),
                            pltpu.VMEM((ed, d), BF16)]),
        out_shape=jax.ShapeDtypeStruct((r, d), BF16),
        compiler_params=_cparams(("arbitrary",)),
    )(*meta, xs, w_gate, w_up, w_down)


def _combine_kernel(pe_ref, po_ref, goff_ref, used_ref, key_ref, wkey_ref, ys_ref, base_ref, g_ref,
                    fn_ref, o_ref, ybuf, q_sc, sem, *, final):
    i = pl.program_id(0)
    tm = base_ref.shape[0]
    ch = SORT_CHUNK
    ppc = ch // RUN_ALIGN

    @pl.when(i == 0)
    def _():
        ybuf[...] = jnp.zeros_like(ybuf)

    copy, wait_all = _piece_copies(pe_ref, po_ref, goff_ref, used_ref, i, ybuf, ys_ref, sem, False)

    for ci in range(ybuf.shape[0] // ch):
        cs = slice(ci * ch, (ci + 1) * ch)

        @pl.when(ci * ch < used_ref[i])
        def _():
            for jj in range(ppc):
                copy(ci * ppc + jj).start()
            rows = _piece_rows(pe_ref, po_ref, key_ref, i * PIECES_PAD + ci * ppc, ppc, wkey_ref)
            q_sc[:, cs] = jnp.concatenate(rows, axis=0).T.astype(BF16)

        @pl.when(ci * ch >= used_ref[i])
        def _():
            q_sc[:, cs] = jnp.zeros((tm, ch), BF16)

    wait_all()
    out = base_ref[...] + g_ref[0] * _bdot(q_sc[...], ybuf[...])
    if final:
        out = _rms(out, fn_ref[...])
    o_ref[...] = out


def _combine(meta, pe, po, key, wkey, ys, base, gt, fnorm, seq, final):
    t, d = base.shape
    tm = TM_SORT
    nt = seq // tm
    return pl.pallas_call(
        functools.partial(_combine_kernel, final=final),
        grid_spec=pltpu.PrefetchScalarGridSpec(
            num_scalar_prefetch=4, grid=(t // tm,),
            in_specs=[pl.BlockSpec((N_EXPERTS, tm), lambda i, *_: (0, i)),
                      pl.BlockSpec((N_EXPERTS, tm), lambda i, *_: (0, i)),
                      pl.BlockSpec(memory_space=pl.ANY),
                      pl.BlockSpec((tm, d), lambda i, *_: (i, 0)),
                      pl.BlockSpec((1, 1, d), lambda i, *_: (i // nt, 0, 0)),
                      pl.BlockSpec((1, d), lambda i, *_: (0, 0))],
            out_specs=pl.BlockSpec((tm, d), lambda i, *_: (i, 0)),
            scratch_shapes=[pltpu.VMEM((ROWS_TILE, d), BF16), pltpu.VMEM((tm, ROWS_TILE), BF16),
                            pltpu.SemaphoreType.DMA(())]),
        out_shape=jax.ShapeDtypeStruct((t, d), F32),
        compiler_params=_cparams(("arbitrary",)),
    )(pe, po, meta[0], meta[1], key, wkey, ys, base, gt, fnorm.reshape(1, d))


def _moe(x, sh, sc, gt, nw, w_router, e_bias, layer, w_gate, w_up, w_down, wsg, wsu, wsd, fnorm, final):
    bsz, s, d = x.shape
    t = bsz * s
    n_rows_max = _rows_max(t)
    hb, base, key, wkey, cnt, pe, po = _router(x, sh, sc, gt, nw, w_router, e_bias, wsg, wsu, wsd)
    pe = pe.reshape(-1)
    po = po.reshape(-1)
    layout_meta, block_meta = _sort_meta(cnt, n_rows_max)
    xs = _dispatch(layout_meta, pe, po, key, hb, n_rows_max)
    ys = _experts(block_meta, xs, layer, w_gate, w_up, w_down)
    out = _combine(layout_meta, pe, po, key, wkey, ys, base, gt, fnorm, s, final)
    return out.reshape(bsz, s, d)


def _kv_kernel(x_ref, sh_ref, sc_ref, nw_ref, wk_ref, wvt_ref, k_ref, vt_ref, km_ref):
    x = x_ref[0]
    hb = (_rms(x, nw_ref[...]) * (1.0 + sc_ref[0]) + sh_ref[0]).astype(BF16)
    k = _bdot(hb, wk_ref[...])
    vt = _bdot_nt(wvt_ref[...], hb)
    for p in range(k.shape[1] // LANES):
        k_ref[0, p, 0] = k[:, p * LANES:(p + 1) * LANES].astype(BF16)
        vt_ref[0, p, 0] = vt[p * LANES:(p + 1) * LANES, :].astype(BF16)
    km_ref[0, 0] = jnp.mean(k, axis=0, keepdims=True)


def _kv(x, sh, sc, nw, w_k, w_v):
    bsz, s, d = x.shape
    nb = s // B_BLOCK
    npair = d // LANES
    vec = pl.BlockSpec((1, 1, d), lambda b, j: (b, 0, 0))
    full2 = lambda shape: pl.BlockSpec(shape, lambda b, j: (0, 0))
    return pl.pallas_call(
        _kv_kernel,
        grid=(bsz, nb),
        in_specs=[pl.BlockSpec((1, B_BLOCK, d), lambda b, j: (b, j, 0)), vec, vec,
                  full2((1, d)), full2((d, d)), full2((d, d))],
        out_specs=[pl.BlockSpec((1, npair, 1, B_BLOCK, LANES), lambda b, j: (b, 0, j, 0, 0)),
                   pl.BlockSpec((1, npair, 1, LANES, B_BLOCK), lambda b, j: (b, 0, j, 0, 0)),
                   pl.BlockSpec((1, 1, 1, d), lambda b, j: (b, j, 0, 0))],
        out_shape=[jax.ShapeDtypeStruct((bsz, npair, nb, B_BLOCK, LANES), BF16),
                   jax.ShapeDtypeStruct((bsz, npair, nb, LANES, B_BLOCK), BF16),
                   jax.ShapeDtypeStruct((bsz, nb, 1, d), F32)],
        compiler_params=_cparams(("arbitrary", "arbitrary")),
    )(x, sh, sc, nw.reshape(1, d), w_k.astype(BF16), w_v.T.astype(BF16))


def _attn_kernel(x_ref, sh_ref, sc_ref, g_ref, nw_ref, wqt_ref, wo_ref, k_ref, vt_ref, km_ref,
                 o_ref, qt_sc, qs_sc, acc_sc, sel_sc, m_sc, l_sc, *, nb, n_sel):
    qb = pl.program_id(1)
    x = x_ref[0]
    bq = x.shape[0]
    npair = qt_sc.shape[0]
    nbp = km_ref.shape[2]
    hd = LANES // 2
    scale = float(hd) ** -0.5 * 1.4426950408889634
    h = _rms(x, nw_ref[...]) * (1.0 + sc_ref[0]) + sh_ref[0]
    qt = _bdot_nt(wqt_ref[...], h.astype(BF16))
    for p in range(npair):
        qt_sc[p] = qt[p * LANES:(p + 1) * LANES, :]

    subn = lax.broadcasted_iota(I32, (nbp, bq), 0)
    past = subn < qb
    krow = lax.broadcasted_iota(I32, (B_BLOCK, bq), 0)
    qcol = lax.broadcasted_iota(I32, (B_BLOCK, bq), 1)
    causal = krow <= qcol
    rowh = lax.broadcasted_iota(I32, (LANES, 1), 0)

    grp = ATTN_PAIR_UNROLL
    heads = [(u, e) for u in range(grp) for e in range(2)]
    ones_rows = jnp.ones((SUM_ROWS, B_BLOCK), BF16)

    def vsum(vt2, e):
        return jnp.concatenate([vt2[e * hd:(e + 1) * hd, :], ones_rows], axis=0)

    def own_body(gi, carry):
        ps = [gi * grp + u for u in range(grp)]
        q2ts = [qt_sc[p] for p in ps]
        kms = [km_ref[0, p] for p in ps]
        kown = [k_ref[0, p, qb] for p in ps]
        vown = [vt_ref[0, p, qb] for p in ps]
        qets = [jnp.where((rowh >= hd) if e == 1 else (rowh < hd), q2ts[u], 0.0) for u, e in heads]
        qsts = [(q * scale).astype(BF16) for q in qets]
        ss = [jnp.where(causal, _bdot(kown[u], qsts[i]), NEG_INF) for i, (u, e) in enumerate(heads)]
        gates = [_dot3(kms[u], qets[i]) for i, (u, e) in enumerate(heads)]
        ms = [jnp.max(s, axis=0, keepdims=True) for s in ss]
        pes = [jnp.exp2(s - m) for s, m in zip(ss, ms)]
        pvs = [_bdot(vsum(vown[u], e), pes[i].astype(BF16)) for i, (u, e) in enumerate(heads)]
        accs = [pv[:hd] for pv in pvs]
        ls = [pv[hd:hd + 1] for pv in pvs]
        sels = []
        for gate in gates:
            selt = jnp.zeros((nbp, bq), F32)
            for n in range(nb):
                gn = gate[n:n + 1, :]
                beats = jnp.logical_or(gate > gn, jnp.logical_and(gate == gn, subn < n))
                beats = jnp.logical_and(beats, past)
                cnt = jnp.sum(beats.astype(F32), axis=0, keepdims=True)
                selt = jnp.where(subn == n, (cnt < n_sel).astype(F32), selt)
            sels.append(selt)
        for i, (u, e) in enumerate(heads):
            p = ps[u]
            sel_sc[p, e] = sels[i]
            qs_sc[p, e] = qsts[i]
            m_sc[p, e] = ms[i]
            l_sc[p, e] = ls[i]
            acc_sc[p, e * hd:(e + 1) * hd, :] = accs[i]
        return carry

    lax.fori_loop(0, npair // grp, own_body, 0)

    def kb_body(kb, carry):
        def group_body(gi, c2):
            ps = [gi * grp + u for u in range(grp)]
            kbl = [k_ref[0, p, kb] for p in ps]
            vbl = [vt_ref[0, p, kb] for p in ps]
            qsts = [qs_sc[ps[u], e] for u, e in heads]
            rows = [sel_sc[ps[u], e, pl.ds(kb, 1), :] for u, e in heads]
            m_old = [m_sc[ps[u], e] for u, e in heads]
            l_old = [l_sc[ps[u], e] for u, e in heads]
            a_old = [acc_sc[ps[u], e * hd:(e + 1) * hd, :] for u, e in heads]
            ss = [jnp.where(rows[i] > 0.5, _bdot(kbl[u], qsts[i]), NEG_INF)
                  for i, (u, e) in enumerate(heads)]
            m_new = [jnp.maximum(m, jnp.max(s, axis=0, keepdims=True)) for m, s in zip(m_old, ss)]
            alphas = [jnp.exp2(m - mn) for m, mn in zip(m_old, m_new)]
            pes = [jnp.exp2(s - mn) for s, mn in zip(ss, m_new)]
            pvs = [_bdot(vsum(vbl[u], e), pes[i].astype(BF16)) for i, (u, e) in enumerate(heads)]
            l_new = [a * l + pv[hd:hd + 1] for a, l, pv in zip(alphas, l_old, pvs)]
            a_new = [a * ao + pv[:hd] for a, ao, pv in zip(alphas, a_old, pvs)]
            for i, (u, e) in enumerate(heads):
                p = ps[u]
                m_sc[p, e] = m_new[i]
                l_sc[p, e] = l_new[i]
                acc_sc[p, e * hd:(e + 1) * hd, :] = a_new[i]
            return c2

        lax.fori_loop(0, npair // grp, group_body, 0)
        return carry

    lax.fori_loop(0, qb, kb_body, 0)

    parts = []
    for p in range(npair):
        for e in range(2):
            parts.append(acc_sc[p, e * hd:(e + 1) * hd, :] / l_sc[p, e])
    ot = jnp.concatenate(parts, axis=0)
    o_ref[0] = x + g_ref[0] * _bdot(ot.T.astype(BF16), wo_ref[...])


def _attn(x, sh, sc, gt, nw, w_q, w_o, k5, vt5, km2):
    bsz, s, d = x.shape
    nb = s // B_BLOCK
    npair = d // LANES
    nbp = km2.shape[2]
    n_sel = min(B_TOPK, nb - 1)
    vec = pl.BlockSpec((1, 1, d), lambda b, j: (b, 0, 0))
    full2 = lambda shape: pl.BlockSpec(shape, lambda b, j: (0, 0))
    return pl.pallas_call(
        functools.partial(_attn_kernel, nb=nb, n_sel=n_sel),
        grid=(bsz, nb),
        in_specs=[pl.BlockSpec((1, B_BLOCK, d), lambda b, j: (b, j, 0)), vec, vec, vec,
                  full2((1, d)), full2((d, d)), full2((d, d)),
                  pl.BlockSpec((1, npair, nb, B_BLOCK, LANES), lambda b, j: (b, 0, 0, 0, 0)),
                  pl.BlockSpec((1, npair, nb, LANES, B_BLOCK), lambda b, j: (b, 0, 0, 0, 0)),
                  pl.BlockSpec((1, npair, nbp, LANES), lambda b, j: (b, 0, 0, 0))],
        out_specs=pl.BlockSpec((1, B_BLOCK, d), lambda b, j: (b, j, 0)),
        out_shape=jax.ShapeDtypeStruct((bsz, s, d), F32),
        scratch_shapes=[pltpu.VMEM((npair, LANES, B_BLOCK), F32),
                        pltpu.VMEM((npair, 2, LANES, B_BLOCK), BF16),
                        pltpu.VMEM((npair, LANES, B_BLOCK), F32),
                        pltpu.VMEM((npair, 2, nbp, B_BLOCK), F32),
                        pltpu.VMEM((npair, 2, 1, B_BLOCK), F32),
                        pltpu.VMEM((npair, 2, 1, B_BLOCK), F32)],
        compiler_params=_cparams(("arbitrary", "arbitrary")),
    )(x, sh, sc, gt, nw.reshape(1, d), w_q.T.astype(BF16), w_o.astype(BF16), k5, vt5, km2)


def kernel(x, c, ada_w, ada_b, norm_mix, norm_ffn, a_w_in, a_b_in, a_ln_g, a_ln_b, a_w_s, a_b_s,
           a_w_out, kv_norm, kv_ada_w, kv_ada_b, kv_w_k, kv_w_v, b_w_q, b_w_o, moe_router, moe_bias,
           moe_w_gate, moe_w_up, moe_w_down, sh_w_gate, sh_w_up, sh_w_down, final_norm):
    bsz, s, d = x.shape
    depth = ada_w.shape[0]
    n_a = a_w_in.shape[0]
    assert s % B_BLOCK == 0 and s % TM_GMLP == 0 and d % LANES == 0
    nb = s // B_BLOCK
    npair = d // LANES
    nbp = -(-nb // 8) * 8

    def split(m, n):
        return [m[:, i * d:(i + 1) * d].reshape(bsz, 1, d) for i in range(n)]

    layer_mods = _ada(c, ada_w, ada_b)
    k5 = vt5 = km2 = None
    for i in range(depth):
        sh1, sc1, g1, sh2, sc2, g2 = split(layer_mods[i], 6)
        if i < n_a:
            x = _gmlp(x, sh1, sc1, g1, norm_mix[i], a_w_in[i], a_b_in[i], a_ln_g[i], a_ln_b[i],
                      a_w_s[i], a_b_s[i], a_w_out[i])
        else:
            if k5 is None:
                ksh, ksc = split(_ada(c, kv_ada_w[None], kv_ada_b[None])[0], 2)
                k5, vt5, km = _kv(x, ksh, ksc, kv_norm, kv_w_k, kv_w_v)
                km = km.reshape(bsz, nb, npair, LANES).transpose(0, 2, 1, 3)
                km2 = jnp.pad(km, ((0, 0), (0, 0), (0, nbp - nb), (0, 0)))
            j = i - n_a
            x = _attn(x, sh1, sc1, g1, norm_mix[i], b_w_q[j], b_w_o[j], k5, vt5, km2)
        x = _moe(x, sh2, sc2, g2, norm_ffn[i], moe_router[i], moe_bias[i], i, moe_w_gate,
                 moe_w_up, moe_w_down, sh_w_gate[i], sh_w_up[i], sh_w_down[i],
                 final_norm, i == depth - 1)
    return x
```

```python
import functools

import jax
import jax.numpy as jnp
from jax import lax
from jax.experimental import pallas as pl
from jax.experimental.pallas import tpu as pltpu

F32 = jnp.float32
BF16 = jnp.bfloat16
I32 = jnp.int32

RMS_EPS = 1e-6
LN_EPS = 1e-5
NEG_INF = -1e30

A_CHUNK = 128
A_GROUPS = 8
B_HEADS = 16
B_BLOCK = 256
B_TOPK = 3
N_EXPERTS = 64
TOP_K = 8
N_GROUPS = 8
TOPK_GROUPS = 4
ROUTED_SCALE = 2.5

LANES = 128
VMEM_LIMIT = 56 * 1024 * 1024

TM_SORT = 512
RUN_ALIGN = 16
SORT_CHUNK = 256
ROWS_TILE_USED = TM_SORT * TOP_K + N_EXPERTS * (RUN_ALIGN - 1)
ROWS_TILE = -(-ROWS_TILE_USED // SORT_CHUNK) * SORT_CHUNK
PIECES_TILE = ROWS_TILE // RUN_ALIGN
PIECES_PAD = -(-PIECES_TILE // LANES) * LANES
BM_EXPERT = 1024
BM_CHAIN = 256
ATTN_PAIR_UNROLL = 4
SUM_ROWS = 16


def _cparams(sem):
    return pltpu.CompilerParams(dimension_semantics=sem, vmem_limit_bytes=VMEM_LIMIT)


def _sigmoid(x):
    return 1.0 / (1.0 + jnp.exp(-x))


def _silu(x):
    return x * _sigmoid(x)


def _gelu_tanh(x):
    return 0.5 * x * (1.0 + jnp.tanh(0.7978845608028654 * (x + 0.044715 * (x * x * x))))


def _rms(x, g):
    return x * lax.rsqrt(jnp.mean(x * x, axis=-1, keepdims=True) + RMS_EPS) * g


def _bdot(a, b):
    return jnp.dot(a, b, preferred_element_type=F32)


def _bdot_nt(a, b):
    return lax.dot_general(a, b, (((1,), (1,)), ((), ())), preferred_element_type=F32)


def _split(a):
    hi = a.astype(BF16)
    lo = (a - hi.astype(F32)).astype(BF16)
    return hi, lo


def _dot3_nt(a, b):
    ah, al = _split(a)
    bh, bl = _split(b)
    return _bdot_nt(ah, bh) + (_bdot_nt(ah, bl) + _bdot_nt(al, bh))


def _dot3(a, b):
    ah, al = _split(a)
    bh, bl = _split(b)
    return _bdot(ah, bh) + (_bdot(ah, bl) + _bdot(al, bh))


def _ada_kernel(c_ref, w_ref, b_ref, o_ref):
    a = _silu(c_ref[...]).astype(BF16)
    o_ref[0] = _bdot(a, w_ref[0].astype(BF16)) + b_ref[0]


def _ada(c, w, b):
    bsz, d = c.shape
    nl, _, n = w.shape
    tn = 1024
    return pl.pallas_call(
        _ada_kernel,
        grid=(nl, n // tn),
        in_specs=[pl.BlockSpec((bsz, d), lambda l, j: (0, 0)),
                  pl.BlockSpec((1, d, tn), lambda l, j: (l, 0, j)),
                  pl.BlockSpec((1, 1, tn), lambda l, j: (l, 0, j))],
        out_specs=pl.BlockSpec((1, bsz, tn), lambda l, j: (l, 0, j)),
        out_shape=jax.ShapeDtypeStruct((nl, bsz, n), F32),
        compiler_params=_cparams(("arbitrary", "arbitrary")),
    )(c, w, b.reshape(nl, 1, n))


def _gmlp_body(x, sh_ref, sc_ref, g_ref, nw_ref, win_ref, bin_ref, lng_ref, lnb_ref,
               ws_ref, bst_ref, wout_ref, y_sc):
    tm = x.shape[0]
    h = _rms(x, nw_ref[...]) * (1.0 + sc_ref[0]) + sh_ref[0]
    z = _gelu_tanh(_bdot(h.astype(BF16), win_ref[...]) + bin_ref[...])
    aw = z.shape[1] // 2
    gd = aw // A_GROUPS
    u = z[:, :aw]
    v = z[:, aw:]
    mu = jnp.mean(v, axis=-1, keepdims=True)
    dv = v - mu
    var = jnp.mean(dv * dv, axis=-1, keepdims=True)
    vn = (dv * lax.rsqrt(var + LN_EPS) * lng_ref[...] + lnb_ref[...]).astype(BF16)
    row = lax.broadcasted_iota(I32, (A_CHUNK, A_CHUNK), 0)
    col = lax.broadcasted_iota(I32, (A_CHUNK, A_CHUNK), 1)
    causal = col <= row
    for g in range(A_GROUPS):
        wg = jnp.where(causal, ws_ref[g], 0.0).astype(BF16)
        bcol = bst_ref[:, g:g + 1]
        for ci in range(tm // A_CHUNK):
            rs = slice(ci * A_CHUNK, (ci + 1) * A_CHUNK)
            cs = slice(g * gd, (g + 1) * gd)
            sv = _bdot(wg, vn[rs, cs]) + bcol
            y_sc[rs, cs] = (u[rs, cs] * sv).astype(BF16)
    return x + g_ref[0] * _bdot(y_sc[...], wout_ref[...])


def _route(x, sh_ref, sc_ref, g_ref, nw_ref, wrt_ref, bias_ref, wsg_ref, wsu_ref, wsd_ref,
           h_ref, base_ref, key_ref, wkey_ref, cnt_ref, pe_ref, po_ref):
    tm = x.shape[0]
    h = _rms(x, nw_ref[...]) * (1.0 + sc_ref[0]) + sh_ref[0]
    hb = h.astype(BF16)
    h_ref[...] = hb
    act = (_silu(_bdot(hb, wsg_ref[...])) * _bdot(hb, wsu_ref[...])).astype(BF16)
    base_ref[...] = x + g_ref[0] * _bdot(act, wsd_ref[...])

    scores = _sigmoid(_dot3_nt(wrt_ref[...], h))
    choice = scores + bias_ref[...]
    gsz = N_EXPERTS // N_GROUPS
    sub = lax.broadcasted_iota(I32, (gsz, tm), 0)
    blocks = [choice[g * gsz:(g + 1) * gsz] for g in range(N_GROUPS)]
    gscore = []
    for blk in blocks:
        m1 = jnp.max(blk, axis=0, keepdims=True)
        i1 = jnp.min(jnp.where(blk == m1, sub, gsz), axis=0, keepdims=True)
        m2 = jnp.max(jnp.where(sub == i1, -jnp.inf, blk), axis=0, keepdims=True)
        gscore.append(m1 + m2)
    masked = []
    for g in range(N_GROUPS):
        beats = jnp.zeros((1, tm), F32)
        for m in range(N_GROUPS):
            if m == g:
                continue
            b = gscore[m] > gscore[g]
            if m < g:
                b = jnp.logical_or(b, gscore[m] == gscore[g])
            beats = beats + b.astype(F32)
        masked.append(jnp.where(beats < TOPK_GROUPS, blocks[g], NEG_INF))
    cur = jnp.concatenate(masked, axis=0)
    eio = lax.broadcasted_iota(I32, (N_EXPERTS, tm), 0)
    chosen = jnp.zeros((N_EXPERTS, tm), jnp.bool_)
    wsum = jnp.zeros((1, tm), F32)
    for _ in range(TOP_K):
        m = jnp.max(cur, axis=0, keepdims=True)
        idx = jnp.min(jnp.where(cur == m, eio, N_EXPERTS), axis=0, keepdims=True)
        sel = eio == idx
        chosen = jnp.logical_or(chosen, sel)
        wsum = wsum + jnp.sum(jnp.where(sel, scores, 0.0), axis=0, keepdims=True)
        cur = jnp.where(sel, -jnp.inf, cur)
    wkey_ref[...] = jnp.where(chosen, scores / wsum * ROUTED_SCALE, 0.0)

    onehot = chosen.astype(BF16)
    r_i = lax.broadcasted_iota(I32, (tm, tm), 0)
    c_i = lax.broadcasted_iota(I32, (tm, tm), 1)
    before = (r_i < c_i).astype(BF16)
    prior = _bdot(onehot, before)
    key_ref[...] = jnp.where(chosen, prior, -1.0).astype(I32)
    cnt = jnp.sum(chosen.astype(F32), axis=1, keepdims=True)
    cnt_ref[0] = cnt

    run_p = jnp.floor((cnt + (RUN_ALIGN - 1)) / RUN_ALIGN)
    ppc = SORT_CHUNK // RUN_ALIGN
    tot = jnp.sum(run_p, axis=0, keepdims=True)
    fill = jnp.ceil(tot / ppc) * ppc - tot
    run_p = run_p + jnp.where(lax.broadcasted_iota(I32, (N_EXPERTS, 1), 0) == N_EXPERTS - 1, fill, 0.0)
    e_r = lax.broadcasted_iota(I32, (N_EXPERTS, N_EXPERTS), 0)
    e_c = lax.broadcasted_iota(I32, (N_EXPERTS, N_EXPERTS), 1)
    incl = (e_c <= e_r).astype(BF16)
    lend = _bdot(incl, jnp.broadcast_to(run_p, (N_EXPERTS, LANES)).astype(BF16))[:, 0:1]
    loff = lend - run_p
    pj = lax.broadcasted_iota(I32, (N_EXPERTS, PIECES_PAD), 1).astype(F32)
    er = lax.broadcasted_iota(I32, (N_EXPERTS, PIECES_PAD), 0).astype(F32)
    pe = jnp.minimum(jnp.sum((lend <= pj).astype(F32), axis=0, keepdims=True), N_EXPERTS - 1.0)
    lo = jnp.sum(jnp.where(er == pe, loff, 0.0), axis=0, keepdims=True)
    pe_ref[0] = pe.astype(I32)
    po_ref[0] = ((pj[0:1, :] - lo) * RUN_ALIGN).astype(I32)


N_GMLP_REFS = 11
N_ROUTE_IN = 9


def _router_kernel(x_ref, *refs):
    _route(x_ref[0], *refs)


def _gmlp_router_kernel(x_ref, *refs):
    y_sc = refs[-1]
    x1 = _gmlp_body(x_ref[0], *refs[:N_GMLP_REFS], y_sc)
    _route(x1, *refs[N_GMLP_REFS:-1])


def _router(x, mods, nw, w_router, e_bias, wsg, wsu, wsd, mixer=None):
    bsz, s, d = x.shape
    t = bsz * s
    tm = TM_SORT
    nt = s // tm
    sd = wsg.shape[1]
    vec = pl.BlockSpec((1, 1, d), lambda b, j: (b, 0, 0))
    full2 = lambda shape: pl.BlockSpec(shape, lambda b, j: (0, 0))
    tok = pl.BlockSpec((tm, d), lambda b, j: (b * nt + j, 0))
    etok = pl.BlockSpec((N_EXPERTS, tm), lambda b, j: (0, b * nt + j))
    ptab = pl.BlockSpec((1, 1, PIECES_PAD), lambda b, j: (b * nt + j, 0, 0))
    sh, sc, gt = mods
    in_specs = [vec, vec, vec, full2((1, d)), full2((N_EXPERTS, d)), full2((N_EXPERTS, 1)),
                full2((d, sd)), full2((d, sd)), full2((sd, d))]
    args = [sh, sc, gt, nw.reshape(1, d), w_router.T, e_bias.reshape(N_EXPERTS, 1),
            wsg.astype(BF16), wsu.astype(BF16), wsd.astype(BF16)]
    kern, scratch = _router_kernel, []
    if mixer is not None:
        msh, msc, mgt, mnw, w_in, b_in, ln_g, ln_b, w_s, b_s, w_out = mixer
        n_in = w_in.shape[1]
        aw = n_in // 2
        in_specs = [vec, vec, vec, full2((1, d)), full2((d, n_in)), full2((1, n_in)), full2((1, aw)),
                    full2((1, aw)), pl.BlockSpec((A_GROUPS, A_CHUNK, A_CHUNK), lambda b, j: (0, 0, 0)),
                    full2((A_CHUNK, A_GROUPS)), full2((aw, d))] + in_specs
        args = [msh, msc, mgt, mnw.reshape(1, d), w_in.astype(BF16), b_in.reshape(1, n_in),
                ln_g.reshape(1, aw), ln_b.reshape(1, aw), w_s, b_s.T, w_out.astype(BF16)] + args
        kern, scratch = _gmlp_router_kernel, [pltpu.VMEM((tm, aw), BF16)]
    return pl.pallas_call(
        kern,
        grid=(bsz, nt),
        in_specs=[pl.BlockSpec((1, tm, d), lambda b, j: (b, j, 0))] + in_specs,
        out_specs=[tok, tok, etok, etok,
                   pl.BlockSpec((1, N_EXPERTS, 1), lambda b, j: (b * nt + j, 0, 0)), ptab, ptab],
        out_shape=[jax.ShapeDtypeStruct((t, d), BF16), jax.ShapeDtypeStruct((t, d), F32),
                   jax.ShapeDtypeStruct((N_EXPERTS, t), I32), jax.ShapeDtypeStruct((N_EXPERTS, t), F32),
                   jax.ShapeDtypeStruct((t // tm, N_EXPERTS, 1), F32),
                   jax.ShapeDtypeStruct((t // tm, 1, PIECES_PAD), I32),
                   jax.ShapeDtypeStruct((t // tm, 1, PIECES_PAD), I32)],
        scratch_shapes=scratch,
        compiler_params=_cparams(("arbitrary", "arbitrary")),
    )(x, *args)


def _rows_max(t):
    rows = (t // TM_SORT) * ROWS_TILE + N_EXPERTS * (BM_EXPERT - RUN_ALIGN)
    return -(-rows // BM_EXPERT) * BM_EXPERT


def _sort_meta(cnt, n_rows_max):
    nts = cnt.shape[0]
    c = cnt.reshape(nts, N_EXPERTS).astype(I32)
    run = (c + (RUN_ALIGN - 1)) // RUN_ALIGN * RUN_ALIGN
    fill = (-jnp.sum(run, axis=1, keepdims=True)) % SORT_CHUNK
    run = jnp.concatenate([run[:, :-1], run[:, -1:] + fill], axis=1)
    used = jnp.sum(run, axis=1)
    per_e = jnp.sum(run, axis=0)
    seg = (per_e + (BM_EXPERT - 1)) // BM_EXPERT * BM_EXPERT
    ends = jnp.cumsum(seg)
    starts = ends - seg
    goff = starts[None, :] + jnp.cumsum(run, axis=0) - run
    pad_start = starts + per_e
    pad_cnt = (seg - per_e) // RUN_ALIGN

    nb = n_rows_max // BM_EXPERT
    blo = jnp.arange(nb, dtype=I32) * BM_EXPERT
    be = jnp.sum((ends[None, :] <= blo[:, None]).astype(I32), axis=1)
    breal = (be < N_EXPERTS).astype(I32)
    be = jnp.minimum(be, N_EXPERTS - 1).astype(I32)
    prev = jnp.concatenate([jnp.full((1,), -1, I32), be[:-1]])
    bnew = (be != prev).astype(I32)
    bx = jnp.where(breal == 1, jnp.arange(nb, dtype=I32), 0)
    i32 = lambda a: a.reshape(-1).astype(I32)
    return (i32(goff), i32(used), i32(pad_start), i32(pad_cnt)), (bx, be, bnew, breal)


def _piece_copies(pe_ref, po_ref, goff_ref, used_ref, tile, local, remote, sem, to_remote):
    def copy(j):
        idx = tile * PIECES_PAD + j
        l0 = pl.multiple_of(j * RUN_ALIGN, RUN_ALIGN)
        g0 = pl.multiple_of(goff_ref[tile * N_EXPERTS + pe_ref[idx]] + po_ref[idx], RUN_ALIGN)
        lref = local.at[pl.ds(l0, RUN_ALIGN)]
        gref = remote.at[pl.ds(g0, RUN_ALIGN)]
        return pltpu.make_async_copy(lref, gref, sem) if to_remote else pltpu.make_async_copy(gref, lref, sem)

    def wait_all():
        def body(j, carry):
            copy(j).wait()
            return carry

        lax.fori_loop(0, used_ref[tile] // RUN_ALIGN, body, 0)

    return copy, wait_all


def _pad_copies(pad_start_ref, pad_cnt_ref, tile, ntiles, zeros, remote, sem):
    share = -(-N_EXPERTS // ntiles)

    def apply(act):
        for q in range(share):
            e = tile * share + q
            ec = jnp.minimum(e, N_EXPERTS - 1)
            n = jnp.where(e < N_EXPERTS, pad_cnt_ref[ec], 0)
            base = pad_start_ref[ec]

            def body(j, carry):
                g0 = pl.multiple_of(base + j * RUN_ALIGN, RUN_ALIGN)
                getattr(pltpu.make_async_copy(zeros, remote.at[pl.ds(g0, RUN_ALIGN)], sem), act)()
                return carry

            lax.fori_loop(0, n, body, 0)

    return apply


def _piece_rows(pe_ref, po_ref, key_ref, first_piece, npieces, val_ref=None):
    tm = key_ref.shape[1]
    sub = lax.broadcasted_iota(I32, (RUN_ALIGN, tm), 0)
    out = []
    for jj in range(npieces):
        j = first_piece + jj
        e = pe_ref[j]
        hit = (key_ref[pl.ds(e, 1), :] - po_ref[j]) == sub
        val = 1.0 if val_ref is None else val_ref[pl.ds(e, 1), :]
        out.append(jnp.where(hit, val, 0.0))
    return out


def _dispatch_kernel(pe_ref, po_ref, goff_ref, used_ref, pad_start_ref, pad_cnt_ref,
                     key_ref, h_ref, xs_ref, xbuf, p_sc, zbuf, sem, zsem, *, ntiles):
    i = pl.program_id(0)
    hb = h_ref[...]
    ch = SORT_CHUNK
    ppc = ch // RUN_ALIGN
    zbuf[...] = jnp.zeros_like(zbuf)
    pads = _pad_copies(pad_start_ref, pad_cnt_ref, i, ntiles, zbuf, xs_ref, zsem)
    pads("start")

    copy, wait_all = _piece_copies(pe_ref, po_ref, goff_ref, used_ref, i, xbuf, xs_ref, sem, True)

    def build(ci):
        rows = _piece_rows(pe_ref, po_ref, key_ref, i * PIECES_PAD + ci * ppc, ppc)
        for jj, p in enumerate(rows):
            p_sc[jj * RUN_ALIGN:(jj + 1) * RUN_ALIGN, :] = p.astype(BF16)
        xbuf[pl.ds(pl.multiple_of(ci * ch, ch), ch), :] = _bdot(p_sc[...], hb).astype(BF16)

    def send(ci):
        for jj in range(ppc):
            copy(ci * ppc + jj).start()

    def step(ci, carry):
        send(ci - 1)
        build(ci)
        return carry

    nchunk = used_ref[i] // ch
    build(0)
    lax.fori_loop(1, nchunk, step, 0)
    send(nchunk - 1)
    pads("wait")
    wait_all()


def _dispatch(meta, pe, po, key, hb, n_rows_max):
    t, d = hb.shape
    tm = TM_SORT
    goff, used, pad_start, pad_cnt = meta
    return pl.pallas_call(
        functools.partial(_dispatch_kernel, ntiles=t // tm),
        grid_spec=pltpu.PrefetchScalarGridSpec(
            num_scalar_prefetch=6, grid=(t // tm,),
            in_specs=[pl.BlockSpec((N_EXPERTS, tm), lambda i, *_: (0, i)),
                      pl.BlockSpec((tm, d), lambda i, *_: (i, 0))],
            out_specs=pl.BlockSpec(memory_space=pl.ANY),
            scratch_shapes=[pltpu.VMEM((ROWS_TILE, d), BF16), pltpu.VMEM((SORT_CHUNK, tm), BF16),
                            pltpu.VMEM((RUN_ALIGN, d), BF16),
                            pltpu.SemaphoreType.DMA(()), pltpu.SemaphoreType.DMA(())]),
        out_shape=jax.ShapeDtypeStruct((n_rows_max, d), BF16),
        compiler_params=_cparams(("arbitrary",)),
    )(pe, po, goff, used, pad_start, pad_cnt, key, hb)


def _expert_kernel(bx_ref, be_ref, bnew_ref, breal_ref, x_ref, wg_ref, wu_ref, wd_ref, o_ref,
                   wg_sc, wu_sc, wd_sc):
    b = pl.program_id(0)

    @pl.when(bnew_ref[b] == 1)
    def _():
        wg_sc[...] = wg_ref[0, 0].astype(BF16)
        wu_sc[...] = wu_ref[0, 0].astype(BF16)
        wd_sc[...] = wd_ref[0, 0].astype(BF16)

    @pl.when(breal_ref[b] == 0)
    def _():
        o_ref[...] = jnp.zeros_like(o_ref)

    @pl.when(breal_ref[b] == 1)
    def _():
        parts = [slice(r, r + BM_CHAIN) for r in range(0, BM_EXPERT, BM_CHAIN)]
        xs = [x_ref[rs, :] for rs in parts]
        gs = [_bdot(x, wg_sc[...]) for x in xs]
        us = [_bdot(x, wu_sc[...]) for x in xs]
        acts = [(_silu(g) * u).astype(BF16) for g, u in zip(gs, us)]
        for rs, act in zip(parts, acts):
            o_ref[rs, :] = _bdot(act, wd_sc[...]).astype(BF16)


def _experts(meta, xs, layer, w_gate, w_up, w_down):
    r, d = xs.shape
    ed = w_gate.shape[-1]
    bm = BM_EXPERT
    omap = lambda b, bx, be, bnew, breal: (b, 0)
    xmap = lambda b, bx, be, bnew, breal: (bx[b], 0)
    wmap = lambda b, bx, be, bnew, breal: (layer, be[b], 0, 0)
    return pl.pallas_call(
        _expert_kernel,
        grid_spec=pltpu.PrefetchScalarGridSpec(
            num_scalar_prefetch=4, grid=(r // bm,),
            in_specs=[pl.BlockSpec((bm, d), xmap), pl.BlockSpec((1, 1, d, ed), wmap),
                      pl.BlockSpec((1, 1, d, ed), wmap), pl.BlockSpec((1, 1, ed, d), wmap)],
            out_specs=pl.BlockSpec((bm, d), omap),
            scratch_shapes=[pltpu.VMEM((d, ed), BF16), pltpu.VMEM((d, ed), BF16),
                            pltpu.VMEM((ed, d), BF16)]),
        out_shape=jax.ShapeDtypeStruct((r, d), BF16),
        compiler_params=_cparams(("arbitrary",)),
    )(*meta, xs, w_gate, w_up, w_down)


def _combine_kernel(pe_ref, po_ref, goff_ref, used_ref, key_ref, wkey_ref, ys_ref, base_ref, g_ref,
                    fn_ref, o_ref, ybuf, q_sc, sem, *, final):
    i = pl.program_id(0)
    tm = base_ref.shape[0]
    ch = SORT_CHUNK
    ppc = ch // RUN_ALIGN

    @pl.when(i == 0)
    def _():
        ybuf[...] = jnp.zeros_like(ybuf)

    copy, wait_all = _piece_copies(pe_ref, po_ref, goff_ref, used_ref, i, ybuf, ys_ref, sem, False)

    for ci in range(ybuf.shape[0] // ch):
        cs = slice(ci * ch, (ci + 1) * ch)

        @pl.when(ci * ch < used_ref[i])
        def _():
            for jj in range(ppc):
                copy(ci * ppc + jj).start()
            rows = _piece_rows(pe_ref, po_ref, key_ref, i * PIECES_PAD + ci * ppc, ppc, wkey_ref)
            q_sc[:, cs] = jnp.concatenate(rows, axis=0).T.astype(BF16)

        @pl.when(ci * ch >= used_ref[i])
        def _():
            q_sc[:, cs] = jnp.zeros((tm, ch), BF16)

    wait_all()
    out = base_ref[...] + g_ref[0] * _bdot(q_sc[...], ybuf[...])
    if final:
        out = _rms(out, fn_ref[...])
    o_ref[...] = out


def _combine(meta, pe, po, key, wkey, ys, base, gt, fnorm, seq, final):
    t, d = base.shape
    tm = TM_SORT
    nt = seq // tm
    return pl.pallas_call(
        functools.partial(_combine_kernel, final=final),
        grid_spec=pltpu.PrefetchScalarGridSpec(
            num_scalar_prefetch=4, grid=(t // tm,),
            in_specs=[pl.BlockSpec((N_EXPERTS, tm), lambda i, *_: (0, i)),
                      pl.BlockSpec((N_EXPERTS, tm), lambda i, *_: (0, i)),
                      pl.BlockSpec(memory_space=pl.ANY),
                      pl.BlockSpec((tm, d), lambda i, *_: (i, 0)),
                      pl.BlockSpec((1, 1, d), lambda i, *_: (i // nt, 0, 0)),
                      pl.BlockSpec((1, d), lambda i, *_: (0, 0))],
            out_specs=pl.BlockSpec((tm, d), lambda i, *_: (i, 0)),
            scratch_shapes=[pltpu.VMEM((ROWS_TILE, d), BF16), pltpu.VMEM((tm, ROWS_TILE), BF16),
                            pltpu.SemaphoreType.DMA(())]),
        out_shape=jax.ShapeDtypeStruct((t, d), F32),
        compiler_params=_cparams(("arbitrary",)),
    )(pe, po, meta[0], meta[1], key, wkey, ys, base, gt, fnorm.reshape(1, d))


def _moe(x, mods, nw, w_router, e_bias, layer, w_gate, w_up, w_down, wsg, wsu, wsd, fnorm, final, mixer=None):
    bsz, s, d = x.shape
    t = bsz * s
    gt = mods[2]
    n_rows_max = _rows_max(t)
    hb, base, key, wkey, cnt, pe, po = _router(x, mods, nw, w_router, e_bias, wsg, wsu, wsd, mixer)
    pe = pe.reshape(-1)
    po = po.reshape(-1)
    layout_meta, block_meta = _sort_meta(cnt, n_rows_max)
    xs = _dispatch(layout_meta, pe, po, key, hb, n_rows_max)
    ys = _experts(block_meta, xs, layer, w_gate, w_up, w_down)
    out = _combine(layout_meta, pe, po, key, wkey, ys, base, gt, fnorm, s, final)
    return out.reshape(bsz, s, d)


def _kv_kernel(x_ref, sh_ref, sc_ref, nw_ref, wk_ref, wvt_ref, k_ref, vt_ref, km_ref):
    x = x_ref[0]
    hb = (_rms(x, nw_ref[...]) * (1.0 + sc_ref[0]) + sh_ref[0]).astype(BF16)
    k = _bdot(hb, wk_ref[...])
    vt = _bdot_nt(wvt_ref[...], hb)
    for p in range(k.shape[1] // LANES):
        k_ref[0, p, 0] = k[:, p * LANES:(p + 1) * LANES].astype(BF16)
        vt_ref[0, p, 0] = vt[p * LANES:(p + 1) * LANES, :].astype(BF16)
    km_ref[0, 0] = jnp.mean(k, axis=0, keepdims=True)


def _kv(x, sh, sc, nw, w_k, w_v):
    bsz, s, d = x.shape
    nb = s // B_BLOCK
    npair = d // LANES
    vec = pl.BlockSpec((1, 1, d), lambda b, j: (b, 0, 0))
    full2 = lambda shape: pl.BlockSpec(shape, lambda b, j: (0, 0))
    return pl.pallas_call(
        _kv_kernel,
        grid=(bsz, nb),
        in_specs=[pl.BlockSpec((1, B_BLOCK, d), lambda b, j: (b, j, 0)), vec, vec,
                  full2((1, d)), full2((d, d)), full2((d, d))],
        out_specs=[pl.BlockSpec((1, npair, 1, B_BLOCK, LANES), lambda b, j: (b, 0, j, 0, 0)),
                   pl.BlockSpec((1, npair, 1, LANES, B_BLOCK), lambda b, j: (b, 0, j, 0, 0)),
                   pl.BlockSpec((1, 1, 1, d), lambda b, j: (b, j, 0, 0))],
        out_shape=[jax.ShapeDtypeStruct((bsz, npair, nb, B_BLOCK, LANES), BF16),
                   jax.ShapeDtypeStruct((bsz, npair, nb, LANES, B_BLOCK), BF16),
                   jax.ShapeDtypeStruct((bsz, nb, 1, d), F32)],
        compiler_params=_cparams(("arbitrary", "arbitrary")),
    )(x, sh, sc, nw.reshape(1, d), w_k.astype(BF16), w_v.T.astype(BF16))


def _attn_kernel(x_ref, sh_ref, sc_ref, g_ref, nw_ref, wqt_ref, wo_ref, k_ref, vt_ref, km_ref,
                 o_ref, qt_sc, qs_sc, acc_sc, sel_sc, m_sc, l_sc, *, nb, n_sel):
    qb = pl.program_id(1)
    x = x_ref[0]
    bq = x.shape[0]
    npair = qt_sc.shape[0]
    nbp = km_ref.shape[2]
    hd = LANES // 2
    scale = float(hd) ** -0.5 * 1.4426950408889634
    h = _rms(x, nw_ref[...]) * (1.0 + sc_ref[0]) + sh_ref[0]
    qt = _bdot_nt(wqt_ref[...], h.astype(BF16))
    for p in range(npair):
        qt_sc[p] = qt[p * LANES:(p + 1) * LANES, :]

    subn = lax.broadcasted_iota(I32, (nbp, bq), 0)
    past = subn < qb
    krow = lax.broadcasted_iota(I32, (B_BLOCK, bq), 0)
    qcol = lax.broadcasted_iota(I32, (B_BLOCK, bq), 1)
    causal = krow <= qcol
    rowh = lax.broadcasted_iota(I32, (LANES, 1), 0)

    grp = ATTN_PAIR_UNROLL
    heads = [(u, e) for u in range(grp) for e in range(2)]
    ones_rows = jnp.ones((SUM_ROWS, B_BLOCK), BF16)

    def vsum(vt2, e):
        return jnp.concatenate([vt2[e * hd:(e + 1) * hd, :], ones_rows], axis=0)

    def own_body(gi, carry):
        ps = [gi * grp + u for u in range(grp)]
        q2ts = [qt_sc[p] for p in ps]
        kms = [km_ref[0, p] for p in ps]
        kown = [k_ref[0, p, qb] for p in ps]
        vown = [vt_ref[0, p, qb] for p in ps]
        qets = [jnp.where((rowh >= hd) if e == 1 else (rowh < hd), q2ts[u], 0.0) for u, e in heads]
        qsts = [(q * scale).astype(BF16) for q in qets]
        ss = [jnp.where(causal, _bdot(kown[u], qsts[i]), NEG_INF) for i, (u, e) in enumerate(heads)]
        gates = [_dot3(kms[u], qets[i]) for i, (u, e) in enumerate(heads)]
        ms = [jnp.max(s, axis=0, keepdims=True) for s in ss]
        pes = [jnp.exp2(s - m) for s, m in zip(ss, ms)]
        pvs = [_bdot(vsum(vown[u], e), pes[i].astype(BF16)) for i, (u, e) in enumerate(heads)]
        accs = [pv[:hd] for pv in pvs]
        ls = [pv[hd:hd + 1] for pv in pvs]
        sels = []
        for gate in gates:
            selt = jnp.zeros((nbp, bq), F32)
            for n in range(nb):
                gn = gate[n:n + 1, :]
                beats = jnp.logical_or(gate > gn, jnp.logical_and(gate == gn, subn < n))
                beats = jnp.logical_and(beats, past)
                cnt = jnp.sum(beats.astype(F32), axis=0, keepdims=True)
                selt = jnp.where(subn == n, (cnt < n_sel).astype(F32), selt)
            sels.append(selt)
        for i, (u, e) in enumerate(heads):
            p = ps[u]
            sel_sc[p, e] = sels[i]
            qs_sc[p, e] = qsts[i]
            m_sc[p, e] = ms[i]
            l_sc[p, e] = ls[i]
            acc_sc[p, e * hd:(e + 1) * hd, :] = accs[i]
        return carry

    lax.fori_loop(0, npair // grp, own_body, 0)

    def kb_body(kb, carry):
        def group_body(gi, c2):
            ps = [gi * grp + u for u in range(grp)]
            kbl = [k_ref[0, p, kb] for p in ps]
            vbl = [vt_ref[0, p, kb] for p in ps]
            qsts = [qs_sc[ps[u], e] for u, e in heads]
            rows = [sel_sc[ps[u], e, pl.ds(kb, 1), :] for u, e in heads]
            m_old = [m_sc[ps[u], e] for u, e in heads]
            l_old = [l_sc[ps[u], e] for u, e in heads]
            a_old = [acc_sc[ps[u], e * hd:(e + 1) * hd, :] for u, e in heads]
            ss = [jnp.where(rows[i] > 0.5, _bdot(kbl[u], qsts[i]), NEG_INF)
                  for i, (u, e) in enumerate(heads)]
            m_new = [jnp.maximum(m, jnp.max(s, axis=0, keepdims=True)) for m, s in zip(m_old, ss)]
            alphas = [jnp.exp2(m - mn) for m, mn in zip(m_old, m_new)]
            pes = [jnp.exp2(s - mn) for s, mn in zip(ss, m_new)]
            pvs = [_bdot(vsum(vbl[u], e), pes[i].astype(BF16)) for i, (u, e) in enumerate(heads)]
            l_new = [a * l + pv[hd:hd + 1] for a, l, pv in zip(alphas, l_old, pvs)]
            a_new = [a * ao + pv[:hd] for a, ao, pv in zip(alphas, a_old, pvs)]
            for i, (u, e) in enumerate(heads):
                p = ps[u]
                m_sc[p, e] = m_new[i]
                l_sc[p, e] = l_new[i]
                acc_sc[p, e * hd:(e + 1) * hd, :] = a_new[i]
            return c2

        lax.fori_loop(0, npair // grp, group_body, 0)
        return carry

    lax.fori_loop(0, qb, kb_body, 0)

    parts = []
    for p in range(npair):
        for e in range(2):
            parts.append(acc_sc[p, e * hd:(e + 1) * hd, :] / l_sc[p, e])
    ot = jnp.concatenate(parts, axis=0)
    o_ref[0] = x + g_ref[0] * _bdot(ot.T.astype(BF16), wo_ref[...])


def _attn(x, sh, sc, gt, nw, w_q, w_o, k5, vt5, km2):
    bsz, s, d = x.shape
    nb = s // B_BLOCK
    npair = d // LANES
    nbp = km2.shape[2]
    n_sel = min(B_TOPK, nb - 1)
    vec = pl.BlockSpec((1, 1, d), lambda b, j: (b, 0, 0))
    full2 = lambda shape: pl.BlockSpec(shape, lambda b, j: (0, 0))
    return pl.pallas_call(
        functools.partial(_attn_kernel, nb=nb, n_sel=n_sel),
        grid=(bsz, nb),
        in_specs=[pl.BlockSpec((1, B_BLOCK, d), lambda b, j: (b, j, 0)), vec, vec, vec,
                  full2((1, d)), full2((d, d)), full2((d, d)),
                  pl.BlockSpec((1, npair, nb, B_BLOCK, LANES), lambda b, j: (b, 0, 0, 0, 0)),
                  pl.BlockSpec((1, npair, nb, LANES, B_BLOCK), lambda b, j: (b, 0, 0, 0, 0)),
                  pl.BlockSpec((1, npair, nbp, LANES), lambda b, j: (b, 0, 0, 0))],
        out_specs=pl.BlockSpec((1, B_BLOCK, d), lambda b, j: (b, j, 0)),
        out_shape=jax.ShapeDtypeStruct((bsz, s, d), F32),
        scratch_shapes=[pltpu.VMEM((npair, LANES, B_BLOCK), F32),
                        pltpu.VMEM((npair, 2, LANES, B_BLOCK), BF16),
                        pltpu.VMEM((npair, LANES, B_BLOCK), F32),
                        pltpu.VMEM((npair, 2, nbp, B_BLOCK), F32),
                        pltpu.VMEM((npair, 2, 1, B_BLOCK), F32),
                        pltpu.VMEM((npair, 2, 1, B_BLOCK), F32)],
        compiler_params=_cparams(("arbitrary", "arbitrary")),
    )(x, sh, sc, gt, nw.reshape(1, d), w_q.T.astype(BF16), w_o.astype(BF16), k5, vt5, km2)


def kernel(x, c, ada_w, ada_b, norm_mix, norm_ffn, a_w_in, a_b_in, a_ln_g, a_ln_b, a_w_s, a_b_s,
           a_w_out, kv_norm, kv_ada_w, kv_ada_b, kv_w_k, kv_w_v, b_w_q, b_w_o, moe_router, moe_bias,
           moe_w_gate, moe_w_up, moe_w_down, sh_w_gate, sh_w_up, sh_w_down, final_norm):
    bsz, s, d = x.shape
    depth = ada_w.shape[0]
    n_a = a_w_in.shape[0]
    assert s % B_BLOCK == 0 and s % TM_SORT == 0 and TM_SORT % A_CHUNK == 0 and d % LANES == 0
    nb = s // B_BLOCK
    npair = d // LANES
    nbp = -(-nb // 8) * 8

    def split(m, n):
        return [m[:, i * d:(i + 1) * d].reshape(bsz, 1, d) for i in range(n)]

    layer_mods = _ada(c, ada_w, ada_b)
    k5 = vt5 = km2 = None
    for i in range(depth):
        sh1, sc1, g1, sh2, sc2, g2 = split(layer_mods[i], 6)
        mixer = None
        if i < n_a:
            mixer = (sh1, sc1, g1, norm_mix[i], a_w_in[i], a_b_in[i], a_ln_g[i], a_ln_b[i],
                     a_w_s[i], a_b_s[i], a_w_out[i])
        else:
            if k5 is None:
                ksh, ksc = split(_ada(c, kv_ada_w[None], kv_ada_b[None])[0], 2)
                k5, vt5, km = _kv(x, ksh, ksc, kv_norm, kv_w_k, kv_w_v)
                km = km.reshape(bsz, nb, npair, LANES).transpose(0, 2, 1, 3)
                km2 = jnp.pad(km, ((0, 0), (0, 0), (0, nbp - nb), (0, 0)))
            j = i - n_a
            x = _attn(x, sh1, sc1, g1, norm_mix[i], b_w_q[j], b_w_o[j], k5, vt5, km2)
        x = _moe(x, (sh2, sc2, g2), norm_ffn[i], moe_router[i], moe_bias[i], i, moe_w_gate,
                 moe_w_up, moe_w_down, sh_w_gate[i], sh_w_up[i], sh_w_down[i],
                 final_norm, i == depth - 1, mixer)
    return x
```

```python
import functools

import jax
import jax.numpy as jnp
from jax import lax
from jax.experimental import pallas as pl
from jax.experimental.pallas import tpu as pltpu

F32 = jnp.float32
BF16 = jnp.bfloat16
I32 = jnp.int32

RMS_EPS = 1e-6
LN_EPS = 1e-5
NEG_INF = -1e30

A_CHUNK = 128
A_GROUPS = 8
B_HEADS = 16
B_BLOCK = 256
B_TOPK = 3
N_EXPERTS = 64
TOP_K = 8
N_GROUPS = 8
TOPK_GROUPS = 4
ROUTED_SCALE = 2.5

LANES = 128
VMEM_LIMIT = 56 * 1024 * 1024

TM_SORT = 512
RUN_ALIGN = 16
SORT_CHUNK = 256
ROWS_TILE_USED = TM_SORT * TOP_K + N_EXPERTS * (RUN_ALIGN - 1)
ROWS_TILE = -(-ROWS_TILE_USED // SORT_CHUNK) * SORT_CHUNK
PIECES_TILE = ROWS_TILE // RUN_ALIGN
PIECES_PAD = -(-PIECES_TILE // LANES) * LANES
BM_EXPERT = 1024
BM_CHAIN = 256
ATTN_PAIR_UNROLL = 4
SUM_ROWS = 16


def _cparams(sem):
    return pltpu.CompilerParams(dimension_semantics=sem, vmem_limit_bytes=VMEM_LIMIT)


def _sigmoid(x):
    return 1.0 / (1.0 + jnp.exp(-x))


def _silu(x):
    return x * _sigmoid(x)


def _gelu_tanh(x):
    return 0.5 * x * (1.0 + jnp.tanh(0.7978845608028654 * (x + 0.044715 * (x * x * x))))


def _rms(x, g):
    return x * lax.rsqrt(jnp.mean(x * x, axis=-1, keepdims=True) + RMS_EPS) * g


def _bdot(a, b):
    return jnp.dot(a, b, preferred_element_type=F32)


def _bdot_nt(a, b):
    return lax.dot_general(a, b, (((1,), (1,)), ((), ())), preferred_element_type=F32)


def _split(a):
    hi = a.astype(BF16)
    lo = (a - hi.astype(F32)).astype(BF16)
    return hi, lo


def _dot3_nt(a, b):
    ah, al = _split(a)
    bh, bl = _split(b)
    return _bdot_nt(ah, bh) + (_bdot_nt(ah, bl) + _bdot_nt(al, bh))


def _dot3(a, b):
    ah, al = _split(a)
    bh, bl = _split(b)
    return _bdot(ah, bh) + (_bdot(ah, bl) + _bdot(al, bh))


def _ada_kernel(c_ref, w_ref, b_ref, o_ref):
    a = _silu(c_ref[...]).astype(BF16)
    o_ref[0] = _bdot(a, w_ref[0].astype(BF16)) + b_ref[0]


def _ada(c, w, b):
    bsz, d = c.shape
    nl, _, n = w.shape
    tn = 1024
    return pl.pallas_call(
        _ada_kernel,
        grid=(nl, n // tn),
        in_specs=[pl.BlockSpec((bsz, d), lambda l, j: (0, 0)),
                  pl.BlockSpec((1, d, tn), lambda l, j: (l, 0, j)),
                  pl.BlockSpec((1, 1, tn), lambda l, j: (l, 0, j))],
        out_specs=pl.BlockSpec((1, bsz, tn), lambda l, j: (l, 0, j)),
        out_shape=jax.ShapeDtypeStruct((nl, bsz, n), F32),
        compiler_params=_cparams(("arbitrary", "arbitrary")),
    )(c, w, b.reshape(nl, 1, n))


def _gmlp_body(x, sh_ref, sc_ref, g_ref, nw_ref, win_ref, bin_ref, lng_ref, lnb_ref,
               ws_ref, bst_ref, wout_ref, y_sc):
    tm = x.shape[0]
    h = _rms(x, nw_ref[...]) * (1.0 + sc_ref[0]) + sh_ref[0]
    z = _gelu_tanh(_bdot(h.astype(BF16), win_ref[...]) + bin_ref[...])
    aw = z.shape[1] // 2
    gd = aw // A_GROUPS
    u = z[:, :aw]
    v = z[:, aw:]
    mu = jnp.mean(v, axis=-1, keepdims=True)
    dv = v - mu
    var = jnp.mean(dv * dv, axis=-1, keepdims=True)
    vn = (dv * lax.rsqrt(var + LN_EPS) * lng_ref[...] + lnb_ref[...]).astype(BF16)
    row = lax.broadcasted_iota(I32, (A_CHUNK, A_CHUNK), 0)
    col = lax.broadcasted_iota(I32, (A_CHUNK, A_CHUNK), 1)
    causal = col <= row
    for g in range(A_GROUPS):
        wg = jnp.where(causal, ws_ref[g], 0.0).astype(BF16)
        bcol = bst_ref[:, g:g + 1]
        for ci in range(tm // A_CHUNK):
            rs = slice(ci * A_CHUNK, (ci + 1) * A_CHUNK)
            cs = slice(g * gd, (g + 1) * gd)
            sv = _bdot(wg, vn[rs, cs]) + bcol
            y_sc[rs, cs] = (u[rs, cs] * sv).astype(BF16)
    return x + g_ref[0] * _bdot(y_sc[...], wout_ref[...])


def _route(x, sh_ref, sc_ref, g_ref, nw_ref, wrt_ref, bias_ref, wsg_ref, wsu_ref, wsd_ref,
           h_ref, base_ref, key_ref, wkey_ref, cnt_ref, pe_ref, po_ref):
    tm = x.shape[0]
    h = _rms(x, nw_ref[...]) * (1.0 + sc_ref[0]) + sh_ref[0]
    hb = h.astype(BF16)
    h_ref[...] = hb
    act = (_silu(_bdot(hb, wsg_ref[...])) * _bdot(hb, wsu_ref[...])).astype(BF16)
    base_ref[...] = x + g_ref[0] * _bdot(act, wsd_ref[...])

    scores = _sigmoid(_dot3_nt(wrt_ref[...], h))
    choice = scores + bias_ref[...]
    gsz = N_EXPERTS // N_GROUPS
    sub = lax.broadcasted_iota(I32, (gsz, tm), 0)
    blocks = [choice[g * gsz:(g + 1) * gsz] for g in range(N_GROUPS)]
    gscore = []
    for blk in blocks:
        m1 = jnp.max(blk, axis=0, keepdims=True)
        i1 = jnp.min(jnp.where(blk == m1, sub, gsz), axis=0, keepdims=True)
        m2 = jnp.max(jnp.where(sub == i1, -jnp.inf, blk), axis=0, keepdims=True)
        gscore.append(m1 + m2)
    masked = []
    for g in range(N_GROUPS):
        beats = jnp.zeros((1, tm), F32)
        for m in range(N_GROUPS):
            if m == g:
                continue
            b = gscore[m] > gscore[g]
            if m < g:
                b = jnp.logical_or(b, gscore[m] == gscore[g])
            beats = beats + b.astype(F32)
        masked.append(jnp.where(beats < TOPK_GROUPS, blocks[g], NEG_INF))
    cur = jnp.concatenate(masked, axis=0)
    eio = lax.broadcasted_iota(I32, (N_EXPERTS, tm), 0)
    chosen = jnp.zeros((N_EXPERTS, tm), jnp.bool_)
    wsum = jnp.zeros((1, tm), F32)
    for _ in range(TOP_K):
        m = jnp.max(cur, axis=0, keepdims=True)
        idx = jnp.min(jnp.where(cur == m, eio, N_EXPERTS), axis=0, keepdims=True)
        sel = eio == idx
        chosen = jnp.logical_or(chosen, sel)
        wsum = wsum + jnp.sum(jnp.where(sel, scores, 0.0), axis=0, keepdims=True)
        cur = jnp.where(sel, -jnp.inf, cur)
    wkey_ref[...] = jnp.where(chosen, scores / wsum * ROUTED_SCALE, 0.0)

    onehot = chosen.astype(BF16)
    r_i = lax.broadcasted_iota(I32, (tm, tm), 0)
    c_i = lax.broadcasted_iota(I32, (tm, tm), 1)
    before = (r_i < c_i).astype(BF16)
    prior = _bdot(onehot, before)
    key_ref[...] = jnp.where(chosen, prior, -1.0).astype(I32)
    cnt = jnp.sum(chosen.astype(F32), axis=1, keepdims=True)
    cnt_ref[0] = cnt

    run_p = jnp.floor((cnt + (RUN_ALIGN - 1)) / RUN_ALIGN)
    ppc = SORT_CHUNK // RUN_ALIGN
    tot = jnp.sum(run_p, axis=0, keepdims=True)
    fill = jnp.ceil(tot / ppc) * ppc - tot
    run_p = run_p + jnp.where(lax.broadcasted_iota(I32, (N_EXPERTS, 1), 0) == N_EXPERTS - 1, fill, 0.0)
    e_r = lax.broadcasted_iota(I32, (N_EXPERTS, N_EXPERTS), 0)
    e_c = lax.broadcasted_iota(I32, (N_EXPERTS, N_EXPERTS), 1)
    incl = (e_c <= e_r).astype(BF16)
    lend = _bdot(incl, jnp.broadcast_to(run_p, (N_EXPERTS, LANES)).astype(BF16))[:, 0:1]
    loff = lend - run_p
    pj = lax.broadcasted_iota(I32, (N_EXPERTS, PIECES_PAD), 1).astype(F32)
    er = lax.broadcasted_iota(I32, (N_EXPERTS, PIECES_PAD), 0).astype(F32)
    pe = jnp.minimum(jnp.sum((lend <= pj).astype(F32), axis=0, keepdims=True), N_EXPERTS - 1.0)
    lo = jnp.sum(jnp.where(er == pe, loff, 0.0), axis=0, keepdims=True)
    pe_ref[0] = pe.astype(I32)
    po_ref[0] = ((pj[0:1, :] - lo) * RUN_ALIGN).astype(I32)


N_GMLP_REFS = 11
N_ROUTE_IN = 9


def _router_kernel(x_ref, *refs):
    _route(x_ref[0], *refs)


def _gmlp_router_kernel(x_ref, *refs):
    y_sc = refs[-1]
    x1 = _gmlp_body(x_ref[0], *refs[:N_GMLP_REFS], y_sc)
    _route(x1, *refs[N_GMLP_REFS:-1])


def _router(x, mods, nw, w_router, e_bias, wsg, wsu, wsd, mixer=None):
    bsz, s, d = x.shape
    t = bsz * s
    tm = TM_SORT
    nt = s // tm
    sd = wsg.shape[1]
    vec = pl.BlockSpec((1, 1, d), lambda b, j: (b, 0, 0))
    full2 = lambda shape: pl.BlockSpec(shape, lambda b, j: (0, 0))
    tok = pl.BlockSpec((tm, d), lambda b, j: (b * nt + j, 0))
    etok = pl.BlockSpec((N_EXPERTS, tm), lambda b, j: (0, b * nt + j))
    ptab = pl.BlockSpec((1, 1, PIECES_PAD), lambda b, j: (b * nt + j, 0, 0))
    sh, sc, gt = mods
    in_specs = [vec, vec, vec, full2((1, d)), full2((N_EXPERTS, d)), full2((N_EXPERTS, 1)),
                full2((d, sd)), full2((d, sd)), full2((sd, d))]
    args = [sh, sc, gt, nw.reshape(1, d), w_router.T, e_bias.reshape(N_EXPERTS, 1),
            wsg.astype(BF16), wsu.astype(BF16), wsd.astype(BF16)]
    kern, scratch = _router_kernel, []
    if mixer is not None:
        msh, msc, mgt, mnw, w_in, b_in, ln_g, ln_b, w_s, b_s, w_out = mixer
        n_in = w_in.shape[1]
        aw = n_in // 2
        in_specs = [vec, vec, vec, full2((1, d)), full2((d, n_in)), full2((1, n_in)), full2((1, aw)),
                    full2((1, aw)), pl.BlockSpec((A_GROUPS, A_CHUNK, A_CHUNK), lambda b, j: (0, 0, 0)),
                    full2((A_CHUNK, A_GROUPS)), full2((aw, d))] + in_specs
        args = [msh, msc, mgt, mnw.reshape(1, d), w_in.astype(BF16), b_in.reshape(1, n_in),
                ln_g.reshape(1, aw), ln_b.reshape(1, aw), w_s, b_s.T, w_out.astype(BF16)] + args
        kern, scratch = _gmlp_router_kernel, [pltpu.VMEM((tm, aw), BF16)]
    return pl.pallas_call(
        kern,
        grid=(bsz, nt),
        in_specs=[pl.BlockSpec((1, tm, d), lambda b, j: (b, j, 0))] + in_specs,
        out_specs=[tok, tok, etok, etok,
                   pl.BlockSpec((1, N_EXPERTS, 1), lambda b, j: (b * nt + j, 0, 0)), ptab, ptab],
        out_shape=[jax.ShapeDtypeStruct((t, d), BF16), jax.ShapeDtypeStruct((t, d), F32),
                   jax.ShapeDtypeStruct((N_EXPERTS, t), I32), jax.ShapeDtypeStruct((N_EXPERTS, t), F32),
                   jax.ShapeDtypeStruct((t // tm, N_EXPERTS, 1), F32),
                   jax.ShapeDtypeStruct((t // tm, 1, PIECES_PAD), I32),
                   jax.ShapeDtypeStruct((t // tm, 1, PIECES_PAD), I32)],
        scratch_shapes=scratch,
        compiler_params=_cparams(("arbitrary", "arbitrary")),
    )(x, *args)


def _rows_max(t):
    rows = (t // TM_SORT) * ROWS_TILE + N_EXPERTS * (BM_EXPERT - RUN_ALIGN)
    return -(-rows // BM_EXPERT) * BM_EXPERT


def _sort_meta(cnt, n_rows_max):
    nts = cnt.shape[0]
    c = cnt.reshape(nts, N_EXPERTS).astype(I32)
    run = (c + (RUN_ALIGN - 1)) // RUN_ALIGN * RUN_ALIGN
    fill = (-jnp.sum(run, axis=1, keepdims=True)) % SORT_CHUNK
    run = jnp.concatenate([run[:, :-1], run[:, -1:] + fill], axis=1)
    used = jnp.sum(run, axis=1)
    per_e = jnp.sum(run, axis=0)
    seg = (per_e + (BM_EXPERT - 1)) // BM_EXPERT * BM_EXPERT
    ends = jnp.cumsum(seg)
    starts = ends - seg
    goff = starts[None, :] + jnp.cumsum(run, axis=0) - run
    pad_start = starts + per_e
    live = (per_e + (BM_CHAIN - 1)) // BM_CHAIN * BM_CHAIN
    pad_cnt = (live - per_e) // RUN_ALIGN

    nb = n_rows_max // BM_EXPERT
    blo = jnp.arange(nb, dtype=I32) * BM_EXPERT
    be = jnp.sum((ends[None, :] <= blo[:, None]).astype(I32), axis=1)
    real = be < N_EXPERTS
    be = jnp.minimum(be, N_EXPERTS - 1).astype(I32)
    prev = jnp.concatenate([jnp.full((1,), -1, I32), be[:-1]])
    bnew = (be != prev).astype(I32)
    bx = jnp.where(real, jnp.arange(nb, dtype=I32), 0)
    left = live[be] - (blo - starts[be])
    bn = jnp.where(real, jnp.clip((left + BM_CHAIN - 1) // BM_CHAIN, 0, BM_EXPERT // BM_CHAIN), 0)
    i32 = lambda a: a.reshape(-1).astype(I32)
    return (i32(goff), i32(used), i32(pad_start), i32(pad_cnt)), (bx, be, bnew, i32(bn))


def _piece_copies(pe_ref, po_ref, goff_ref, used_ref, tile, local, remote, sem, to_remote):
    def copy(j):
        idx = tile * PIECES_PAD + j
        l0 = pl.multiple_of(j * RUN_ALIGN, RUN_ALIGN)
        g0 = pl.multiple_of(goff_ref[tile * N_EXPERTS + pe_ref[idx]] + po_ref[idx], RUN_ALIGN)
        lref = local.at[pl.ds(l0, RUN_ALIGN)]
        gref = remote.at[pl.ds(g0, RUN_ALIGN)]
        return pltpu.make_async_copy(lref, gref, sem) if to_remote else pltpu.make_async_copy(gref, lref, sem)

    def wait_all():
        def body(j, carry):
            copy(j).wait()
            return carry

        lax.fori_loop(0, used_ref[tile] // RUN_ALIGN, body, 0)

    return copy, wait_all


def _pad_copies(pad_start_ref, pad_cnt_ref, tile, ntiles, zeros, remote, sem):
    share = -(-N_EXPERTS // ntiles)

    def apply(act):
        for q in range(share):
            e = tile * share + q
            ec = jnp.minimum(e, N_EXPERTS - 1)
            n = jnp.where(e < N_EXPERTS, pad_cnt_ref[ec], 0)
            base = pad_start_ref[ec]

            def body(j, carry):
                g0 = pl.multiple_of(base + j * RUN_ALIGN, RUN_ALIGN)
                getattr(pltpu.make_async_copy(zeros, remote.at[pl.ds(g0, RUN_ALIGN)], sem), act)()
                return carry

            lax.fori_loop(0, n, body, 0)

    return apply


def _piece_rows(pe_ref, po_ref, key_ref, first_piece, npieces, val_ref=None):
    tm = key_ref.shape[1]
    sub = lax.broadcasted_iota(I32, (RUN_ALIGN, tm), 0)
    out = []
    for jj in range(npieces):
        j = first_piece + jj
        e = pe_ref[j]
        hit = (key_ref[pl.ds(e, 1), :] - po_ref[j]) == sub
        val = 1.0 if val_ref is None else val_ref[pl.ds(e, 1), :]
        out.append(jnp.where(hit, val, 0.0))
    return out


def _dispatch_kernel(pe_ref, po_ref, goff_ref, used_ref, pad_start_ref, pad_cnt_ref,
                     key_ref, h_ref, xs_ref, xbuf, p_sc, zbuf, sem, zsem, *, ntiles):
    i = pl.program_id(0)
    hb = h_ref[...]
    ch = SORT_CHUNK
    ppc = ch // RUN_ALIGN
    zbuf[...] = jnp.zeros_like(zbuf)
    pads = _pad_copies(pad_start_ref, pad_cnt_ref, i, ntiles, zbuf, xs_ref, zsem)
    pads("start")

    copy, wait_all = _piece_copies(pe_ref, po_ref, goff_ref, used_ref, i, xbuf, xs_ref, sem, True)

    def build(ci):
        rows = _piece_rows(pe_ref, po_ref, key_ref, i * PIECES_PAD + ci * ppc, ppc)
        for jj, p in enumerate(rows):
            p_sc[jj * RUN_ALIGN:(jj + 1) * RUN_ALIGN, :] = p.astype(BF16)
        xbuf[pl.ds(pl.multiple_of(ci * ch, ch), ch), :] = _bdot(p_sc[...], hb).astype(BF16)

    def send(ci):
        for jj in range(ppc):
            copy(ci * ppc + jj).start()

    def step(ci, carry):
        send(ci - 1)
        build(ci)
        return carry

    nchunk = used_ref[i] // ch
    build(0)
    lax.fori_loop(1, nchunk, step, 0)
    send(nchunk - 1)
    pads("wait")
    wait_all()


def _dispatch(meta, pe, po, key, hb, n_rows_max):
    t, d = hb.shape
    tm = TM_SORT
    goff, used, pad_start, pad_cnt = meta
    return pl.pallas_call(
        functools.partial(_dispatch_kernel, ntiles=t // tm),
        grid_spec=pltpu.PrefetchScalarGridSpec(
            num_scalar_prefetch=6, grid=(t // tm,),
            in_specs=[pl.BlockSpec((N_EXPERTS, tm), lambda i, *_: (0, i)),
                      pl.BlockSpec((tm, d), lambda i, *_: (i, 0))],
            out_specs=pl.BlockSpec(memory_space=pl.ANY),
            scratch_shapes=[pltpu.VMEM((ROWS_TILE, d), BF16), pltpu.VMEM((SORT_CHUNK, tm), BF16),
                            pltpu.VMEM((RUN_ALIGN, d), BF16),
                            pltpu.SemaphoreType.DMA(()), pltpu.SemaphoreType.DMA(())]),
        out_shape=jax.ShapeDtypeStruct((n_rows_max, d), BF16),
        compiler_params=_cparams(("arbitrary",)),
    )(pe, po, goff, used, pad_start, pad_cnt, key, hb)


def _expert_kernel(bx_ref, be_ref, bnew_ref, bn_ref, x_ref, wg_ref, wu_ref, wd_ref, o_ref,
                   wg_sc, wu_sc, wd_sc):
    b = pl.program_id(0)
    n = bn_ref[b]
    nchain = BM_EXPERT // BM_CHAIN
    parts = [slice(r, r + BM_CHAIN) for r in range(0, BM_EXPERT, BM_CHAIN)]

    @pl.when(bnew_ref[b] == 1)
    def _():
        wg_sc[...] = wg_ref[0, 0].astype(BF16)
        wu_sc[...] = wu_ref[0, 0].astype(BF16)
        wd_sc[...] = wd_ref[0, 0].astype(BF16)

    @pl.when(n == nchain)
    def _():
        xs = [x_ref[rs, :] for rs in parts]
        gs = [_bdot(x, wg_sc[...]) for x in xs]
        us = [_bdot(x, wu_sc[...]) for x in xs]
        acts = [(_silu(g) * u).astype(BF16) for g, u in zip(gs, us)]
        for rs, act in zip(parts, acts):
            o_ref[rs, :] = _bdot(act, wd_sc[...]).astype(BF16)

    @pl.when(n < nchain)
    def _():
        for c, rs in enumerate(parts):
            @pl.when(c < n)
            def _():
                x = x_ref[rs, :]
                act = (_silu(_bdot(x, wg_sc[...])) * _bdot(x, wu_sc[...])).astype(BF16)
                o_ref[rs, :] = _bdot(act, wd_sc[...]).astype(BF16)

            @pl.when(c >= n)
            def _():
                o_ref[rs, :] = jnp.zeros((BM_CHAIN, o_ref.shape[1]), BF16)


def _experts(meta, xs, layer, w_gate, w_up, w_down):
    r, d = xs.shape
    ed = w_gate.shape[-1]
    bm = BM_EXPERT
    omap = lambda b, bx, be, bnew, bn: (b, 0)
    xmap = lambda b, bx, be, bnew, bn: (bx[b], 0)
    wmap = lambda b, bx, be, bnew, bn: (layer, be[b], 0, 0)
    return pl.pallas_call(
        _expert_kernel,
        grid_spec=pltpu.PrefetchScalarGridSpec(
            num_scalar_prefetch=4, grid=(r // bm,),
            in_specs=[pl.BlockSpec((bm, d), xmap), pl.BlockSpec((1, 1, d, ed), wmap),
                      pl.BlockSpec((1, 1, d, ed), wmap), pl.BlockSpec((1, 1, ed, d), wmap)],
            out_specs=pl.BlockSpec((bm, d), omap),
            scratch_shapes=[pltpu.VMEM((d, ed), BF16), pltpu.VMEM((d, ed), BF16),
                            pltpu.VMEM((ed, d), BF16)]),
        out_shape=jax.ShapeDtypeStruct((r, d), BF16),
        compiler_params=_cparams(("arbitrary",)),
    )(*meta, xs, w_gate, w_up, w_down)


def _combine_kernel(pe_ref, po_ref, goff_ref, used_ref, key_ref, wkey_ref, ys_ref, base_ref, g_ref,
                    fn_ref, o_ref, ybuf, q_sc, sem, *, final):
    i = pl.program_id(0)
    tm = base_ref.shape[0]
    ch = SORT_CHUNK
    ppc = ch // RUN_ALIGN

    @pl.when(i == 0)
    def _():
        ybuf[...] = jnp.zeros_like(ybuf)

    copy, wait_all = _piece_copies(pe_ref, po_ref, goff_ref, used_ref, i, ybuf, ys_ref, sem, False)

    for ci in range(ybuf.shape[0] // ch):
        cs = slice(ci * ch, (ci + 1) * ch)

        @pl.when(ci * ch < used_ref[i])
        def _():
            for jj in range(ppc):
                copy(ci * ppc + jj).start()
            rows = _piece_rows(pe_ref, po_ref, key_ref, i * PIECES_PAD + ci * ppc, ppc, wkey_ref)
            q_sc[:, cs] = jnp.concatenate(rows, axis=0).T.astype(BF16)

        @pl.when(ci * ch >= used_ref[i])
        def _():
            q_sc[:, cs] = jnp.zeros((tm, ch), BF16)

    wait_all()
    out = base_ref[...] + g_ref[0] * _bdot(q_sc[...], ybuf[...])
    if final:
        out = _rms(out, fn_ref[...])
    o_ref[...] = out


def _combine(meta, pe, po, key, wkey, ys, base, gt, fnorm, seq, final):
    t, d = base.shape
    tm = TM_SORT
    nt = seq // tm
    return pl.pallas_call(
        functools.partial(_combine_kernel, final=final),
        grid_spec=pltpu.PrefetchScalarGridSpec(
            num_scalar_prefetch=4, grid=(t // tm,),
            in_specs=[pl.BlockSpec((N_EXPERTS, tm), lambda i, *_: (0, i)),
                      pl.BlockSpec((N_EXPERTS, tm), lambda i, *_: (0, i)),
                      pl.BlockSpec(memory_space=pl.ANY),
                      pl.BlockSpec((tm, d), lambda i, *_: (i, 0)),
                      pl.BlockSpec((1, 1, d), lambda i, *_: (i // nt, 0, 0)),
                      pl.BlockSpec((1, d), lambda i, *_: (0, 0))],
            out_specs=pl.BlockSpec((tm, d), lambda i, *_: (i, 0)),
            scratch_shapes=[pltpu.VMEM((ROWS_TILE, d), BF16), pltpu.VMEM((tm, ROWS_TILE), BF16),
                            pltpu.SemaphoreType.DMA(())]),
        out_shape=jax.ShapeDtypeStruct((t, d), F32),
        compiler_params=_cparams(("arbitrary",)),
    )(pe, po, meta[0], meta[1], key, wkey, ys, base, gt, fnorm.reshape(1, d))


def _moe(x, mods, nw, w_router, e_bias, layer, w_gate, w_up, w_down, wsg, wsu, wsd, fnorm, final, mixer=None):
    bsz, s, d = x.shape
    t = bsz * s
    gt = mods[2]
    n_rows_max = _rows_max(t)
    hb, base, key, wkey, cnt, pe, po = _router(x, mods, nw, w_router, e_bias, wsg, wsu, wsd, mixer)
    pe = pe.reshape(-1)
    po = po.reshape(-1)
    layout_meta, block_meta = _sort_meta(cnt, n_rows_max)
    xs = _dispatch(layout_meta, pe, po, key, hb, n_rows_max)
    ys = _experts(block_meta, xs, layer, w_gate, w_up, w_down)
    out = _combine(layout_meta, pe, po, key, wkey, ys, base, gt, fnorm, s, final)
    return out.reshape(bsz, s, d)


def _kv_kernel(x_ref, sh_ref, sc_ref, nw_ref, wk_ref, wvt_ref, k_ref, vt_ref, km_ref):
    x = x_ref[0]
    hb = (_rms(x, nw_ref[...]) * (1.0 + sc_ref[0]) + sh_ref[0]).astype(BF16)
    k = _bdot(hb, wk_ref[...])
    vt = _bdot_nt(wvt_ref[...], hb)
    for p in range(k.shape[1] // LANES):
        k_ref[0, p, 0] = k[:, p * LANES:(p + 1) * LANES].astype(BF16)
        vt_ref[0, p, 0] = vt[p * LANES:(p + 1) * LANES, :].astype(BF16)
    km_ref[0, 0] = jnp.mean(k, axis=0, keepdims=True)


def _kv(x, sh, sc, nw, w_k, w_v):
    bsz, s, d = x.shape
    nb = s // B_BLOCK
    npair = d // LANES
    vec = pl.BlockSpec((1, 1, d), lambda b, j: (b, 0, 0))
    full2 = lambda shape: pl.BlockSpec(shape, lambda b, j: (0, 0))
    return pl.pallas_call(
        _kv_kernel,
        grid=(bsz, nb),
        in_specs=[pl.BlockSpec((1, B_BLOCK, d), lambda b, j: (b, j, 0)), vec, vec,
                  full2((1, d)), full2((d, d)), full2((d, d))],
        out_specs=[pl.BlockSpec((1, npair, 1, B_BLOCK, LANES), lambda b, j: (b, 0, j, 0, 0)),
                   pl.BlockSpec((1, npair, 1, LANES, B_BLOCK), lambda b, j: (b, 0, j, 0, 0)),
                   pl.BlockSpec((1, 1, 1, d), lambda b, j: (b, j, 0, 0))],
        out_shape=[jax.ShapeDtypeStruct((bsz, npair, nb, B_BLOCK, LANES), BF16),
                   jax.ShapeDtypeStruct((bsz, npair, nb, LANES, B_BLOCK), BF16),
                   jax.ShapeDtypeStruct((bsz, nb, 1, d), F32)],
        compiler_params=_cparams(("arbitrary", "arbitrary")),
    )(x, sh, sc, nw.reshape(1, d), w_k.astype(BF16), w_v.T.astype(BF16))


def _attn_kernel(x_ref, sh_ref, sc_ref, g_ref, nw_ref, wqt_ref, wo_ref, k_ref, vt_ref, km_ref,
                 o_ref, qt_sc, qs_sc, acc_sc, sel_sc, m_sc, l_sc, *, nb, n_sel):
    qb = pl.program_id(1)
    x = x_ref[0]
    bq = x.shape[0]
    npair = qt_sc.shape[0]
    nbp = km_ref.shape[2]
    hd = LANES // 2
    scale = float(hd) ** -0.5 * 1.4426950408889634
    h = _rms(x, nw_ref[...]) * (1.0 + sc_ref[0]) + sh_ref[0]
    qt = _bdot_nt(wqt_ref[...], h.astype(BF16))
    for p in range(npair):
        qt_sc[p] = qt[p * LANES:(p + 1) * LANES, :]

    subn = lax.broadcasted_iota(I32, (nbp, bq), 0)
    past = subn < qb
    krow = lax.broadcasted_iota(I32, (B_BLOCK, bq), 0)
    qcol = lax.broadcasted_iota(I32, (B_BLOCK, bq), 1)
    causal = krow <= qcol
    rowh = lax.broadcasted_iota(I32, (LANES, 1), 0)

    grp = ATTN_PAIR_UNROLL
    heads = [(u, e) for u in range(grp) for e in range(2)]
    ones_rows = jnp.ones((SUM_ROWS, B_BLOCK), BF16)

    def vsum(vt2, e):
        return jnp.concatenate([vt2[e * hd:(e + 1) * hd, :], ones_rows], axis=0)

    def own_body(gi, carry):
        ps = [gi * grp + u for u in range(grp)]
        q2ts = [qt_sc[p] for p in ps]
        kms = [km_ref[0, p] for p in ps]
        kown = [k_ref[0, p, qb] for p in ps]
        vown = [vt_ref[0, p, qb] for p in ps]
        qets = [jnp.where((rowh >= hd) if e == 1 else (rowh < hd), q2ts[u], 0.0) for u, e in heads]
        qsts = [(q * scale).astype(BF16) for q in qets]
        ss = [jnp.where(causal, _bdot(kown[u], qsts[i]), NEG_INF) for i, (u, e) in enumerate(heads)]
        gates = [_dot3(kms[u], qets[i]) for i, (u, e) in enumerate(heads)]
        ms = [jnp.max(s, axis=0, keepdims=True) for s in ss]
        pes = [jnp.exp2(s - m) for s, m in zip(ss, ms)]
        pvs = [_bdot(vsum(vown[u], e), pes[i].astype(BF16)) for i, (u, e) in enumerate(heads)]
        accs = [pv[:hd] for pv in pvs]
        ls = [pv[hd:hd + 1] for pv in pvs]
        sels = []
        for gate in gates:
            selt = jnp.zeros((nbp, bq), F32)
            for n in range(nb):
                gn = gate[n:n + 1, :]
                beats = jnp.logical_or(gate > gn, jnp.logical_and(gate == gn, subn < n))
                beats = jnp.logical_and(beats, past)
                cnt = jnp.sum(beats.astype(F32), axis=0, keepdims=True)
                selt = jnp.where(subn == n, (cnt < n_sel).astype(F32), selt)
            sels.append(selt)
        for i, (u, e) in enumerate(heads):
            p = ps[u]
            sel_sc[p, e] = sels[i]
            qs_sc[p, e] = qsts[i]
            m_sc[p, e] = ms[i]
            l_sc[p, e] = ls[i]
            acc_sc[p, e * hd:(e + 1) * hd, :] = accs[i]
        return carry

    lax.fori_loop(0, npair // grp, own_body, 0)

    def kb_body(kb, carry):
        def group_body(gi, c2):
            ps = [gi * grp + u for u in range(grp)]
            kbl = [k_ref[0, p, kb] for p in ps]
            vbl = [vt_ref[0, p, kb] for p in ps]
            qsts = [qs_sc[ps[u], e] for u, e in heads]
            rows = [sel_sc[ps[u], e, pl.ds(kb, 1), :] for u, e in heads]
            m_old = [m_sc[ps[u], e] for u, e in heads]
            l_old = [l_sc[ps[u], e] for u, e in heads]
            a_old = [acc_sc[ps[u], e * hd:(e + 1) * hd, :] for u, e in heads]
            ss = [jnp.where(rows[i] > 0.5, _bdot(kbl[u], qsts[i]), NEG_INF)
                  for i, (u, e) in enumerate(heads)]
            m_new = [jnp.maximum(m, jnp.max(s, axis=0, keepdims=True)) for m, s in zip(m_old, ss)]
            alphas = [jnp.exp2(m - mn) for m, mn in zip(m_old, m_new)]
            pes = [jnp.exp2(s - mn) for s, mn in zip(ss, m_new)]
            pvs = [_bdot(vsum(vbl[u], e), pes[i].astype(BF16)) for i, (u, e) in enumerate(heads)]
            l_new = [a * l + pv[hd:hd + 1] for a, l, pv in zip(alphas, l_old, pvs)]
            a_new = [a * ao + pv[:hd] for a, ao, pv in zip(alphas, a_old, pvs)]
            for i, (u, e) in enumerate(heads):
                p = ps[u]
                m_sc[p, e] = m_new[i]
                l_sc[p, e] = l_new[i]
                acc_sc[p, e * hd:(e + 1) * hd, :] = a_new[i]
            return c2

        lax.fori_loop(0, npair // grp, group_body, 0)
        return carry

    lax.fori_loop(0, qb, kb_body, 0)

    parts = []
    for p in range(npair):
        for e in range(2):
            parts.append(acc_sc[p, e * hd:(e + 1) * hd, :] / l_sc[p, e])
    ot = jnp.concatenate(parts, axis=0)
    o_ref[0] = x + g_ref[0] * _bdot(ot.T.astype(BF16), wo_ref[...])


def _attn(x, sh, sc, gt, nw, w_q, w_o, k5, vt5, km2):
    bsz, s, d = x.shape
    nb = s // B_BLOCK
    npair = d // LANES
    nbp = km2.shape[2]
    n_sel = min(B_TOPK, nb - 1)
    vec = pl.BlockSpec((1, 1, d), lambda b, j: (b, 0, 0))
    full2 = lambda shape: pl.BlockSpec(shape, lambda b, j: (0, 0))
    return pl.pallas_call(
        functools.partial(_attn_kernel, nb=nb, n_sel=n_sel),
        grid=(bsz, nb),
        in_specs=[pl.BlockSpec((1, B_BLOCK, d), lambda b, j: (b, j, 0)), vec, vec, vec,
                  full2((1, d)), full2((d, d)), full2((d, d)),
                  pl.BlockSpec((1, npair, nb, B_BLOCK, LANES), lambda b, j: (b, 0, 0, 0, 0)),
                  pl.BlockSpec((1, npair, nb, LANES, B_BLOCK), lambda b, j: (b, 0, 0, 0, 0)),
                  pl.BlockSpec((1, npair, nbp, LANES), lambda b, j: (b, 0, 0, 0))],
        out_specs=pl.BlockSpec((1, B_BLOCK, d), lambda b, j: (b, j, 0)),
        out_shape=jax.ShapeDtypeStruct((bsz, s, d), F32),
        scratch_shapes=[pltpu.VMEM((npair, LANES, B_BLOCK), F32),
                        pltpu.VMEM((npair, 2, LANES, B_BLOCK), BF16),
                        pltpu.VMEM((npair, LANES, B_BLOCK), F32),
                        pltpu.VMEM((npair, 2, nbp, B_BLOCK), F32),
                        pltpu.VMEM((npair, 2, 1, B_BLOCK), F32),
                        pltpu.VMEM((npair, 2, 1, B_BLOCK), F32)],
        compiler_params=_cparams(("arbitrary", "arbitrary")),
    )(x, sh, sc, gt, nw.reshape(1, d), w_q.T.astype(BF16), w_o.astype(BF16), k5, vt5, km2)


def kernel(x, c, ada_w, ada_b, norm_mix, norm_ffn, a_w_in, a_b_in, a_ln_g, a_ln_b, a_w_s, a_b_s,
           a_w_out, kv_norm, kv_ada_w, kv_ada_b, kv_w_k, kv_w_v, b_w_q, b_w_o, moe_router, moe_bias,
           moe_w_gate, moe_w_up, moe_w_down, sh_w_gate, sh_w_up, sh_w_down, final_norm):
    bsz, s, d = x.shape
    depth = ada_w.shape[0]
    n_a = a_w_in.shape[0]
    assert s % B_BLOCK == 0 and s % TM_SORT == 0 and TM_SORT % A_CHUNK == 0 and d % LANES == 0
    nb = s // B_BLOCK
    npair = d // LANES
    nbp = -(-nb // 8) * 8

    def split(m, n):
        return [m[:, i * d:(i + 1) * d].reshape(bsz, 1, d) for i in range(n)]

    layer_mods = _ada(c, ada_w, ada_b)
    k5 = vt5 = km2 = None
    for i in range(depth):
        sh1, sc1, g1, sh2, sc2, g2 = split(layer_mods[i], 6)
        mixer = None
        if i < n_a:
            mixer = (sh1, sc1, g1, norm_mix[i], a_w_in[i], a_b_in[i], a_ln_g[i], a_ln_b[i],
                     a_w_s[i], a_b_s[i], a_w_out[i])
        else:
            if k5 is None:
                ksh, ksc = split(_ada(c, kv_ada_w[None], kv_ada_b[None])[0], 2)
                k5, vt5, km = _kv(x, ksh, ksc, kv_norm, kv_w_k, kv_w_v)
                km = km.reshape(bsz, nb, npair, LANES).transpose(0, 2, 1, 3)
                km2 = jnp.pad(km, ((0, 0), (0, 0), (0, nbp - nb), (0, 0)))
            j = i - n_a
            x = _attn(x, sh1, sc1, g1, norm_mix[i], b_w_q[j], b_w_o[j], k5, vt5, km2)
        x = _moe(x, (sh2, sc2, g2), norm_ffn[i], moe_router[i], moe_bias[i], i, moe_w_gate,
                 moe_w_up, moe_w_down, sh_w_gate[i], sh_w_up[i], sh_w_down[i],
                 final_norm, i == depth - 1, mixer)
    return x
```

```python
import functools

import jax
import jax.numpy as jnp
from jax import lax
from jax.experimental import pallas as pl
from jax.experimental.pallas import tpu as pltpu

F32 = jnp.float32
BF16 = jnp.bfloat16
I32 = jnp.int32

RMS_EPS = 1e-6
LN_EPS = 1e-5
NEG_INF = -1e30

A_CHUNK = 128
A_GROUPS = 8
B_HEADS = 16
B_BLOCK = 256
B_TOPK = 3
N_EXPERTS = 64
TOP_K = 8
N_GROUPS = 8
TOPK_GROUPS = 4
ROUTED_SCALE = 2.5

LANES = 128
VMEM_LIMIT = 56 * 1024 * 1024

TM_SORT = 512
RUN_ALIGN = 16
SORT_CHUNK = 256
ROWS_TILE_USED = TM_SORT * TOP_K + N_EXPERTS * (RUN_ALIGN - 1)
ROWS_TILE = -(-ROWS_TILE_USED // SORT_CHUNK) * SORT_CHUNK
PIECES_TILE = ROWS_TILE // RUN_ALIGN
PIECES_PAD = -(-PIECES_TILE // LANES) * LANES
BM_EXPERT = 1024
BM_CHAIN = 256
ATTN_PAIR_UNROLL = 4
SUM_ROWS = 16


def _cparams(sem):
    return pltpu.CompilerParams(dimension_semantics=sem, vmem_limit_bytes=VMEM_LIMIT)


def _sigmoid(x):
    return 1.0 / (1.0 + jnp.exp(-x))


def _silu(x):
    return x * _sigmoid(x)


def _gelu_tanh(x):
    return 0.5 * x * (1.0 + jnp.tanh(0.7978845608028654 * (x + 0.044715 * (x * x * x))))


def _rms(x, g):
    return x * lax.rsqrt(jnp.mean(x * x, axis=-1, keepdims=True) + RMS_EPS) * g


def _bdot(a, b):
    return jnp.dot(a, b, preferred_element_type=F32)


def _bdot_nt(a, b):
    return lax.dot_general(a, b, (((1,), (1,)), ((), ())), preferred_element_type=F32)


def _split(a):
    hi = a.astype(BF16)
    lo = (a - hi.astype(F32)).astype(BF16)
    return hi, lo


def _dot3_nt(a, b):
    ah, al = _split(a)
    bh, bl = _split(b)
    return _bdot_nt(ah, bh) + (_bdot_nt(ah, bl) + _bdot_nt(al, bh))


def _dot3(a, b):
    ah, al = _split(a)
    bh, bl = _split(b)
    return _bdot(ah, bh) + (_bdot(ah, bl) + _bdot(al, bh))


def _ada_kernel(c_ref, w_ref, b_ref, o_ref):
    a = _silu(c_ref[...]).astype(BF16)
    o_ref[0] = _bdot(a, w_ref[0].astype(BF16)) + b_ref[0]


def _ada(c, w, b):
    bsz, d = c.shape
    nl, _, n = w.shape
    tn = 1024
    return pl.pallas_call(
        _ada_kernel,
        grid=(nl, n // tn),
        in_specs=[pl.BlockSpec((bsz, d), lambda l, j: (0, 0)),
                  pl.BlockSpec((1, d, tn), lambda l, j: (l, 0, j)),
                  pl.BlockSpec((1, 1, tn), lambda l, j: (l, 0, j))],
        out_specs=pl.BlockSpec((1, bsz, tn), lambda l, j: (l, 0, j)),
        out_shape=jax.ShapeDtypeStruct((nl, bsz, n), F32),
        compiler_params=_cparams(("arbitrary", "arbitrary")),
    )(c, w, b.reshape(nl, 1, n))


def _gmlp_body(x, sh_ref, sc_ref, g_ref, nw_ref, win_ref, bin_ref, lng_ref, lnb_ref,
               ws_ref, bst_ref, wout_ref, y_sc):
    tm = x.shape[0]
    h = _rms(x, nw_ref[...]) * (1.0 + sc_ref[0]) + sh_ref[0]
    z = _gelu_tanh(_bdot(h.astype(BF16), win_ref[...]) + bin_ref[...])
    aw = z.shape[1] // 2
    gd = aw // A_GROUPS
    u = z[:, :aw]
    v = z[:, aw:]
    mu = jnp.mean(v, axis=-1, keepdims=True)
    dv = v - mu
    var = jnp.mean(dv * dv, axis=-1, keepdims=True)
    vn = (dv * lax.rsqrt(var + LN_EPS) * lng_ref[...] + lnb_ref[...]).astype(BF16)
    row = lax.broadcasted_iota(I32, (A_CHUNK, A_CHUNK), 0)
    col = lax.broadcasted_iota(I32, (A_CHUNK, A_CHUNK), 1)
    causal = col <= row
    for g in range(A_GROUPS):
        wg = jnp.where(causal, ws_ref[g], 0.0).astype(BF16)
        bcol = bst_ref[:, g:g + 1]
        for ci in range(tm // A_CHUNK):
            rs = slice(ci * A_CHUNK, (ci + 1) * A_CHUNK)
            cs = slice(g * gd, (g + 1) * gd)
            sv = _bdot(wg, vn[rs, cs]) + bcol
            y_sc[rs, cs] = (u[rs, cs] * sv).astype(BF16)
    return x + g_ref[0] * _bdot(y_sc[...], wout_ref[...])


def _route(x, sh_ref, sc_ref, g_ref, nw_ref, wrt_ref, bias_ref, wsg_ref, wsu_ref, wsd_ref,
           h_ref, base_ref, key_ref, wkey_ref, cnt_ref, pe_ref, po_ref):
    tm = x.shape[0]
    h = _rms(x, nw_ref[...]) * (1.0 + sc_ref[0]) + sh_ref[0]
    hb = h.astype(BF16)
    h_ref[...] = hb
    act = (_silu(_bdot(hb, wsg_ref[...])) * _bdot(hb, wsu_ref[...])).astype(BF16)
    base_ref[...] = x + g_ref[0] * _bdot(act, wsd_ref[...])

    scores = _sigmoid(_dot3_nt(wrt_ref[...], h))
    choice = scores + bias_ref[...]
    gsz = N_EXPERTS // N_GROUPS
    sub = lax.broadcasted_iota(I32, (gsz, tm), 0)
    blocks = [choice[g * gsz:(g + 1) * gsz] for g in range(N_GROUPS)]
    gscore = []
    for blk in blocks:
        m1 = jnp.max(blk, axis=0, keepdims=True)
        i1 = jnp.min(jnp.where(blk == m1, sub, gsz), axis=0, keepdims=True)
        m2 = jnp.max(jnp.where(sub == i1, -jnp.inf, blk), axis=0, keepdims=True)
        gscore.append(m1 + m2)
    masked = []
    for g in range(N_GROUPS):
        beats = jnp.zeros((1, tm), F32)
        for m in range(N_GROUPS):
            if m == g:
                continue
            b = gscore[m] > gscore[g]
            if m < g:
                b = jnp.logical_or(b, gscore[m] == gscore[g])
            beats = beats + b.astype(F32)
        masked.append(jnp.where(beats < TOPK_GROUPS, blocks[g], NEG_INF))
    cur = jnp.concatenate(masked, axis=0)
    eio = lax.broadcasted_iota(I32, (N_EXPERTS, tm), 0)
    chosen = jnp.zeros((N_EXPERTS, tm), jnp.bool_)
    wsum = jnp.zeros((1, tm), F32)
    for _ in range(TOP_K):
        m = jnp.max(cur, axis=0, keepdims=True)
        idx = jnp.min(jnp.where(cur == m, eio, N_EXPERTS), axis=0, keepdims=True)
        sel = eio == idx
        chosen = jnp.logical_or(chosen, sel)
        wsum = wsum + jnp.sum(jnp.where(sel, scores, 0.0), axis=0, keepdims=True)
        cur = jnp.where(sel, -jnp.inf, cur)
    wkey_ref[...] = jnp.where(chosen, scores / wsum * ROUTED_SCALE, 0.0)

    onehot = chosen.astype(BF16)
    r_i = lax.broadcasted_iota(I32, (tm, tm), 0)
    c_i = lax.broadcasted_iota(I32, (tm, tm), 1)
    before = (r_i < c_i).astype(BF16)
    prior = _bdot(onehot, before)
    key_ref[...] = jnp.where(chosen, prior, -1.0).astype(I32)
    cnt = jnp.sum(chosen.astype(F32), axis=1, keepdims=True)
    cnt_ref[0] = cnt

    run_p = jnp.floor((cnt + (RUN_ALIGN - 1)) / RUN_ALIGN)
    ppc = SORT_CHUNK // RUN_ALIGN
    tot = jnp.sum(run_p, axis=0, keepdims=True)
    fill = jnp.ceil(tot / ppc) * ppc - tot
    run_p = run_p + jnp.where(lax.broadcasted_iota(I32, (N_EXPERTS, 1), 0) == N_EXPERTS - 1, fill, 0.0)
    e_r = lax.broadcasted_iota(I32, (N_EXPERTS, N_EXPERTS), 0)
    e_c = lax.broadcasted_iota(I32, (N_EXPERTS, N_EXPERTS), 1)
    incl = (e_c <= e_r).astype(BF16)
    lend = _bdot(incl, jnp.broadcast_to(run_p, (N_EXPERTS, LANES)).astype(BF16))[:, 0:1]
    loff = lend - run_p
    pj = lax.broadcasted_iota(I32, (N_EXPERTS, PIECES_PAD), 1).astype(F32)
    er = lax.broadcasted_iota(I32, (N_EXPERTS, PIECES_PAD), 0).astype(F32)
    pe = jnp.minimum(jnp.sum((lend <= pj).astype(F32), axis=0, keepdims=True), N_EXPERTS - 1.0)
    lo = jnp.sum(jnp.where(er == pe, loff, 0.0), axis=0, keepdims=True)
    pe_ref[0] = pe.astype(I32)
    po_ref[0] = ((pj[0:1, :] - lo) * RUN_ALIGN).astype(I32)


N_GMLP_REFS = 11
N_ROUTE_IN = 9


def _router_kernel(x_ref, *refs):
    _route(x_ref[0], *refs)


def _gmlp_router_kernel(x_ref, *refs):
    y_sc = refs[-1]
    x1 = _gmlp_body(x_ref[0], *refs[:N_GMLP_REFS], y_sc)
    _route(x1, *refs[N_GMLP_REFS:-1])


def _router(x, mods, nw, w_router, e_bias, wsg, wsu, wsd, mixer=None):
    bsz, s, d = x.shape
    t = bsz * s
    tm = TM_SORT
    nt = s // tm
    sd = wsg.shape[1]
    vec = pl.BlockSpec((1, 1, d), lambda b, j: (b, 0, 0))
    full2 = lambda shape: pl.BlockSpec(shape, lambda b, j: (0, 0))
    tok = pl.BlockSpec((tm, d), lambda b, j: (b * nt + j, 0))
    etok = pl.BlockSpec((N_EXPERTS, tm), lambda b, j: (0, b * nt + j))
    ptab = pl.BlockSpec((1, 1, PIECES_PAD), lambda b, j: (b * nt + j, 0, 0))
    sh, sc, gt = mods
    in_specs = [vec, vec, vec, full2((1, d)), full2((N_EXPERTS, d)), full2((N_EXPERTS, 1)),
                full2((d, sd)), full2((d, sd)), full2((sd, d))]
    args = [sh, sc, gt, nw.reshape(1, d), w_router.T, e_bias.reshape(N_EXPERTS, 1),
            wsg.astype(BF16), wsu.astype(BF16), wsd.astype(BF16)]
    kern, scratch = _router_kernel, []
    if mixer is not None:
        msh, msc, mgt, mnw, w_in, b_in, ln_g, ln_b, w_s, b_s, w_out = mixer
        n_in = w_in.shape[1]
        aw = n_in // 2
        in_specs = [vec, vec, vec, full2((1, d)), full2((d, n_in)), full2((1, n_in)), full2((1, aw)),
                    full2((1, aw)), pl.BlockSpec((A_GROUPS, A_CHUNK, A_CHUNK), lambda b, j: (0, 0, 0)),
                    full2((A_CHUNK, A_GROUPS)), full2((aw, d))] + in_specs
        args = [msh, msc, mgt, mnw.reshape(1, d), w_in.astype(BF16), b_in.reshape(1, n_in),
                ln_g.reshape(1, aw), ln_b.reshape(1, aw), w_s, b_s.T, w_out.astype(BF16)] + args
        kern, scratch = _gmlp_router_kernel, [pltpu.VMEM((tm, aw), BF16)]
    return pl.pallas_call(
        kern,
        grid=(bsz, nt),
        in_specs=[pl.BlockSpec((1, tm, d), lambda b, j: (b, j, 0))] + in_specs,
        out_specs=[tok, tok, etok, etok,
                   pl.BlockSpec((1, N_EXPERTS, 1), lambda b, j: (b * nt + j, 0, 0)), ptab, ptab],
        out_shape=[jax.ShapeDtypeStruct((t, d), BF16), jax.ShapeDtypeStruct((t, d), F32),
                   jax.ShapeDtypeStruct((N_EXPERTS, t), I32), jax.ShapeDtypeStruct((N_EXPERTS, t), F32),
                   jax.ShapeDtypeStruct((t // tm, N_EXPERTS, 1), F32),
                   jax.ShapeDtypeStruct((t // tm, 1, PIECES_PAD), I32),
                   jax.ShapeDtypeStruct((t // tm, 1, PIECES_PAD), I32)],
        scratch_shapes=scratch,
        compiler_params=_cparams(("arbitrary", "arbitrary")),
    )(x, *args)


def _rows_max(t):
    rows = (t // TM_SORT) * ROWS_TILE + N_EXPERTS * (BM_EXPERT - RUN_ALIGN)
    return -(-rows // BM_EXPERT) * BM_EXPERT


def _sort_meta(cnt, n_rows_max):
    nts = cnt.shape[0]
    c = cnt.reshape(nts, N_EXPERTS).astype(I32)
    run = (c + (RUN_ALIGN - 1)) // RUN_ALIGN * RUN_ALIGN
    fill = (-jnp.sum(run, axis=1, keepdims=True)) % SORT_CHUNK
    run = jnp.concatenate([run[:, :-1], run[:, -1:] + fill], axis=1)
    used = jnp.sum(run, axis=1)
    per_e = jnp.sum(run, axis=0)
    seg = (per_e + (BM_EXPERT - 1)) // BM_EXPERT * BM_EXPERT
    ends = jnp.cumsum(seg)
    starts = ends - seg
    goff = starts[None, :] + jnp.cumsum(run, axis=0) - run
    pad_start = starts + per_e
    pad_cnt = (seg - per_e) // RUN_ALIGN

    nb = n_rows_max // BM_EXPERT
    blo = jnp.arange(nb, dtype=I32) * BM_EXPERT
    be = jnp.sum((ends[None, :] <= blo[:, None]).astype(I32), axis=1)
    real = be < N_EXPERTS
    be = jnp.minimum(be, N_EXPERTS - 1).astype(I32)
    prev = jnp.concatenate([jnp.full((1,), -1, I32), be[:-1]])
    bnew = jnp.logical_and(real, be != prev)
    bidx = jnp.arange(nb, dtype=I32)
    bx = jnp.where(real, bidx, 0)
    later = jnp.logical_and(bidx[None, :] > bidx[:, None], jnp.logical_and(real[None, :], be[None, :] != be[:, None]))
    nxt = jnp.min(jnp.where(later, bidx[None, :], nb), axis=1)
    bfetch = jnp.where(nxt < nb, be[jnp.minimum(nxt, nb - 1)], -1)
    i32 = lambda a: a.reshape(-1).astype(I32)
    return (i32(goff), i32(used), i32(pad_start), i32(pad_cnt)), (bx, be, i32(bnew), i32(real), i32(bfetch))


def _piece_copies(pe_ref, po_ref, goff_ref, used_ref, tile, local, remote, sem, to_remote):
    def copy(j):
        idx = tile * PIECES_PAD + j
        l0 = pl.multiple_of(j * RUN_ALIGN, RUN_ALIGN)
        g0 = pl.multiple_of(goff_ref[tile * N_EXPERTS + pe_ref[idx]] + po_ref[idx], RUN_ALIGN)
        lref = local.at[pl.ds(l0, RUN_ALIGN)]
        gref = remote.at[pl.ds(g0, RUN_ALIGN)]
        return pltpu.make_async_copy(lref, gref, sem) if to_remote else pltpu.make_async_copy(gref, lref, sem)

    def wait_all():
        def body(j, carry):
            copy(j).wait()
            return carry

        lax.fori_loop(0, used_ref[tile] // RUN_ALIGN, body, 0)

    return copy, wait_all


def _pad_copies(pad_start_ref, pad_cnt_ref, tile, ntiles, zeros, remote, sem):
    share = -(-N_EXPERTS // ntiles)

    def apply(act):
        for q in range(share):
            e = tile * share + q
            ec = jnp.minimum(e, N_EXPERTS - 1)
            n = jnp.where(e < N_EXPERTS, pad_cnt_ref[ec], 0)
            base = pad_start_ref[ec]

            def body(j, carry):
                g0 = pl.multiple_of(base + j * RUN_ALIGN, RUN_ALIGN)
                getattr(pltpu.make_async_copy(zeros, remote.at[pl.ds(g0, RUN_ALIGN)], sem), act)()
                return carry

            lax.fori_loop(0, n, body, 0)

    return apply


def _piece_rows(pe_ref, po_ref, key_ref, first_piece, npieces, val_ref=None):
    tm = key_ref.shape[1]
    sub = lax.broadcasted_iota(I32, (RUN_ALIGN, tm), 0)
    out = []
    for jj in range(npieces):
        j = first_piece + jj
        e = pe_ref[j]
        hit = (key_ref[pl.ds(e, 1), :] - po_ref[j]) == sub
        val = 1.0 if val_ref is None else val_ref[pl.ds(e, 1), :]
        out.append(jnp.where(hit, val, 0.0))
    return out


def _dispatch_kernel(pe_ref, po_ref, goff_ref, used_ref, pad_start_ref, pad_cnt_ref,
                     key_ref, h_ref, xs_ref, xbuf, p_sc, zbuf, sem, zsem, *, ntiles):
    i = pl.program_id(0)
    hb = h_ref[...]
    ch = SORT_CHUNK
    ppc = ch // RUN_ALIGN
    zbuf[...] = jnp.zeros_like(zbuf)
    pads = _pad_copies(pad_start_ref, pad_cnt_ref, i, ntiles, zbuf, xs_ref, zsem)
    pads("start")

    copy, wait_all = _piece_copies(pe_ref, po_ref, goff_ref, used_ref, i, xbuf, xs_ref, sem, True)

    def build(ci):
        rows = _piece_rows(pe_ref, po_ref, key_ref, i * PIECES_PAD + ci * ppc, ppc)
        for jj, p in enumerate(rows):
            p_sc[jj * RUN_ALIGN:(jj + 1) * RUN_ALIGN, :] = p.astype(BF16)
        xbuf[pl.ds(pl.multiple_of(ci * ch, ch), ch), :] = _bdot(p_sc[...], hb).astype(BF16)

    def send(ci):
        for jj in range(ppc):
            copy(ci * ppc + jj).start()

    def step(ci, carry):
        send(ci - 1)
        build(ci)
        return carry

    nchunk = used_ref[i] // ch
    build(0)
    lax.fori_loop(1, nchunk, step, 0)
    send(nchunk - 1)
    pads("wait")
    wait_all()


def _dispatch(meta, pe, po, key, hb, n_rows_max):
    t, d = hb.shape
    tm = TM_SORT
    goff, used, pad_start, pad_cnt = meta
    return pl.pallas_call(
        functools.partial(_dispatch_kernel, ntiles=t // tm),
        grid_spec=pltpu.PrefetchScalarGridSpec(
            num_scalar_prefetch=6, grid=(t // tm,),
            in_specs=[pl.BlockSpec((N_EXPERTS, tm), lambda i, *_: (0, i)),
                      pl.BlockSpec((tm, d), lambda i, *_: (i, 0))],
            out_specs=pl.BlockSpec(memory_space=pl.ANY),
            scratch_shapes=[pltpu.VMEM((ROWS_TILE, d), BF16), pltpu.VMEM((SORT_CHUNK, tm), BF16),
                            pltpu.VMEM((RUN_ALIGN, d), BF16),
                            pltpu.SemaphoreType.DMA(()), pltpu.SemaphoreType.DMA(())]),
        out_shape=jax.ShapeDtypeStruct((n_rows_max, d), BF16),
        compiler_params=_cparams(("arbitrary",)),
    )(pe, po, goff, used, pad_start, pad_cnt, key, hb)


def _expert_kernel(bx_ref, be_ref, bnew_ref, breal_ref, bfetch_ref, x_ref, wg_hbm, wu_hbm, wd_hbm, o_ref,
                   wg_st, wu_st, wd_st, wg_sc, wu_sc, wd_sc, sem, *, layer):
    b = pl.program_id(0)

    def fetch(e):
        return [pltpu.make_async_copy(src.at[layer, e], dst, sem.at[i])
                for i, (src, dst) in enumerate(((wg_hbm, wg_st), (wu_hbm, wu_st), (wd_hbm, wd_st)))]

    @pl.when(b == 0)
    def _():
        for cp in fetch(be_ref[0]):
            cp.start()

    @pl.when(bnew_ref[b] == 1)
    def _():
        for cp in fetch(be_ref[b]):
            cp.wait()
        wg_sc[...] = wg_st[...].astype(BF16)
        wu_sc[...] = wu_st[...].astype(BF16)
        wd_sc[...] = wd_st[...].astype(BF16)

        @pl.when(bfetch_ref[b] >= 0)
        def _():
            for cp in fetch(bfetch_ref[b]):
                cp.start()

    @pl.when(breal_ref[b] == 0)
    def _():
        o_ref[...] = jnp.zeros_like(o_ref)

    @pl.when(breal_ref[b] == 1)
    def _():
        parts = [slice(r, r + BM_CHAIN) for r in range(0, BM_EXPERT, BM_CHAIN)]
        xs = [x_ref[rs, :] for rs in parts]
        gs = [_bdot(x, wg_sc[...]) for x in xs]
        us = [_bdot(x, wu_sc[...]) for x in xs]
        acts = [(_silu(g) * u).astype(BF16) for g, u in zip(gs, us)]
        for rs, act in zip(parts, acts):
            o_ref[rs, :] = _bdot(act, wd_sc[...]).astype(BF16)


def _experts(meta, xs, layer, w_gate, w_up, w_down):
    r, d = xs.shape
    ed = w_gate.shape[-1]
    bm = BM_EXPERT
    omap = lambda b, bx, *_: (b, 0)
    xmap = lambda b, bx, *_: (bx[b], 0)
    hbm = pl.BlockSpec(memory_space=pl.ANY)
    return pl.pallas_call(
        functools.partial(_expert_kernel, layer=layer),
        grid_spec=pltpu.PrefetchScalarGridSpec(
            num_scalar_prefetch=5, grid=(r // bm,),
            in_specs=[pl.BlockSpec((bm, d), xmap), hbm, hbm, hbm],
            out_specs=pl.BlockSpec((bm, d), omap),
            scratch_shapes=[pltpu.VMEM((d, ed), F32), pltpu.VMEM((d, ed), F32), pltpu.VMEM((ed, d), F32),
                            pltpu.VMEM((d, ed), BF16), pltpu.VMEM((d, ed), BF16), pltpu.VMEM((ed, d), BF16),
                            pltpu.SemaphoreType.DMA((3,))]),
        out_shape=jax.ShapeDtypeStruct((r, d), BF16),
        compiler_params=_cparams(("arbitrary",)),
    )(*meta, xs, w_gate, w_up, w_down)


def _combine_kernel(pe_ref, po_ref, goff_ref, used_ref, key_ref, wkey_ref, ys_ref, base_ref, g_ref,
                    fn_ref, o_ref, ybuf, q_sc, sem, *, final):
    i = pl.program_id(0)
    tm = base_ref.shape[0]
    ch = SORT_CHUNK
    ppc = ch // RUN_ALIGN

    @pl.when(i == 0)
    def _():
        ybuf[...] = jnp.zeros_like(ybuf)

    copy, wait_all = _piece_copies(pe_ref, po_ref, goff_ref, used_ref, i, ybuf, ys_ref, sem, False)

    for ci in range(ybuf.shape[0] // ch):
        cs = slice(ci * ch, (ci + 1) * ch)

        @pl.when(ci * ch < used_ref[i])
        def _():
            for jj in range(ppc):
                copy(ci * ppc + jj).start()
            rows = _piece_rows(pe_ref, po_ref, key_ref, i * PIECES_PAD + ci * ppc, ppc, wkey_ref)
            q_sc[:, cs] = jnp.concatenate(rows, axis=0).T.astype(BF16)

        @pl.when(ci * ch >= used_ref[i])
        def _():
            q_sc[:, cs] = jnp.zeros((tm, ch), BF16)

    wait_all()
    out = base_ref[...] + g_ref[0] * _bdot(q_sc[...], ybuf[...])
    if final:
        out = _rms(out, fn_ref[...])
    o_ref[...] = out


def _combine(meta, pe, po, key, wkey, ys, base, gt, fnorm, seq, final):
    t, d = base.shape
    tm = TM_SORT
    nt = seq // tm
    return pl.pallas_call(
        functools.partial(_combine_kernel, final=final),
        grid_spec=pltpu.PrefetchScalarGridSpec(
            num_scalar_prefetch=4, grid=(t // tm,),
            in_specs=[pl.BlockSpec((N_EXPERTS, tm), lambda i, *_: (0, i)),
                      pl.BlockSpec((N_EXPERTS, tm), lambda i, *_: (0, i)),
                      pl.BlockSpec(memory_space=pl.ANY),
                      pl.BlockSpec((tm, d), lambda i, *_: (i, 0)),
                      pl.BlockSpec((1, 1, d), lambda i, *_: (i // nt, 0, 0)),
                      pl.BlockSpec((1, d), lambda i, *_: (0, 0))],
            out_specs=pl.BlockSpec((tm, d), lambda i, *_: (i, 0)),
            scratch_shapes=[pltpu.VMEM((ROWS_TILE, d), BF16), pltpu.VMEM((tm, ROWS_TILE), BF16),
                            pltpu.SemaphoreType.DMA(())]),
        out_shape=jax.ShapeDtypeStruct((t, d), F32),
        compiler_params=_cparams(("arbitrary",)),
    )(pe, po, meta[0], meta[1], key, wkey, ys, base, gt, fnorm.reshape(1, d))


def _moe(x, mods, nw, w_router, e_bias, layer, w_gate, w_up, w_down, wsg, wsu, wsd, fnorm, final, mixer=None):
    bsz, s, d = x.shape
    t = bsz * s
    gt = mods[2]
    n_rows_max = _rows_max(t)
    hb, base, key, wkey, cnt, pe, po = _router(x, mods, nw, w_router, e_bias, wsg, wsu, wsd, mixer)
    pe = pe.reshape(-1)
    po = po.reshape(-1)
    layout_meta, block_meta = _sort_meta(cnt, n_rows_max)
    xs = _dispatch(layout_meta, pe, po, key, hb, n_rows_max)
    ys = _experts(block_meta, xs, layer, w_gate, w_up, w_down)
    out = _combine(layout_meta, pe, po, key, wkey, ys, base, gt, fnorm, s, final)
    return out.reshape(bsz, s, d)


def _kv_kernel(x_ref, sh_ref, sc_ref, nw_ref, wk_ref, wvt_ref, k_ref, vt_ref, km_ref):
    x = x_ref[0]
    hb = (_rms(x, nw_ref[...]) * (1.0 + sc_ref[0]) + sh_ref[0]).astype(BF16)
    k = _bdot(hb, wk_ref[...])
    vt = _bdot_nt(wvt_ref[...], hb)
    for p in range(k.shape[1] // LANES):
        k_ref[0, p, 0] = k[:, p * LANES:(p + 1) * LANES].astype(BF16)
        vt_ref[0, p, 0] = vt[p * LANES:(p + 1) * LANES, :].astype(BF16)
    km_ref[0, 0] = jnp.mean(k, axis=0, keepdims=True)


def _kv(x, sh, sc, nw, w_k, w_v):
    bsz, s, d = x.shape
    nb = s // B_BLOCK
    npair = d // LANES
    vec = pl.BlockSpec((1, 1, d), lambda b, j: (b, 0, 0))
    full2 = lambda shape: pl.BlockSpec(shape, lambda b, j: (0, 0))
    return pl.pallas_call(
        _kv_kernel,
        grid=(bsz, nb),
        in_specs=[pl.BlockSpec((1, B_BLOCK, d), lambda b, j: (b, j, 0)), vec, vec,
                  full2((1, d)), full2((d, d)), full2((d, d))],
        out_specs=[pl.BlockSpec((1, npair, 1, B_BLOCK, LANES), lambda b, j: (b, 0, j, 0, 0)),
                   pl.BlockSpec((1, npair, 1, LANES, B_BLOCK), lambda b, j: (b, 0, j, 0, 0)),
                   pl.BlockSpec((1, 1, 1, d), lambda b, j: (b, j, 0, 0))],
        out_shape=[jax.ShapeDtypeStruct((bsz, npair, nb, B_BLOCK, LANES), BF16),
                   jax.ShapeDtypeStruct((bsz, npair, nb, LANES, B_BLOCK), BF16),
                   jax.ShapeDtypeStruct((bsz, nb, 1, d), F32)],
        compiler_params=_cparams(("arbitrary", "arbitrary")),
    )(x, sh, sc, nw.reshape(1, d), w_k.astype(BF16), w_v.T.astype(BF16))


def _attn_kernel(x_ref, sh_ref, sc_ref, g_ref, nw_ref, wqt_ref, wo_ref, k_ref, vt_ref, km_ref,
                 o_ref, qt_sc, qs_sc, acc_sc, sel_sc, m_sc, l_sc, *, nb, n_sel):
    qb = pl.program_id(1)
    x = x_ref[0]
    bq = x.shape[0]
    npair = qt_sc.shape[0]
    nbp = km_ref.shape[2]
    hd = LANES // 2
    scale = float(hd) ** -0.5 * 1.4426950408889634
    h = _rms(x, nw_ref[...]) * (1.0 + sc_ref[0]) + sh_ref[0]
    qt = _bdot_nt(wqt_ref[...], h.astype(BF16))
    for p in range(npair):
        qt_sc[p] = qt[p * LANES:(p + 1) * LANES, :]

    subn = lax.broadcasted_iota(I32, (nbp, bq), 0)
    past = subn < qb
    krow = lax.broadcasted_iota(I32, (B_BLOCK, bq), 0)
    qcol = lax.broadcasted_iota(I32, (B_BLOCK, bq), 1)
    causal = krow <= qcol
    rowh = lax.broadcasted_iota(I32, (LANES, 1), 0)

    grp = ATTN_PAIR_UNROLL
    heads = [(u, e) for u in range(grp) for e in range(2)]
    ones_rows = jnp.ones((SUM_ROWS, B_BLOCK), BF16)

    def vsum(vt2, e):
        return jnp.concatenate([vt2[e * hd:(e + 1) * hd, :], ones_rows], axis=0)

    def own_body(gi, carry):
        ps = [gi * grp + u for u in range(grp)]
        q2ts = [qt_sc[p] for p in ps]
        kms = [km_ref[0, p] for p in ps]
        kown = [k_ref[0, p, qb] for p in ps]
        vown = [vt_ref[0, p, qb] for p in ps]
        qets = [jnp.where((rowh >= hd) if e == 1 else (rowh < hd), q2ts[u], 0.0) for u, e in heads]
        qsts = [(q * scale).astype(BF16) for q in qets]
        ss = [jnp.where(causal, _bdot(kown[u], qsts[i]), NEG_INF) for i, (u, e) in enumerate(heads)]
        gates = [_dot3(kms[u], qets[i]) for i, (u, e) in enumerate(heads)]
        ms = [jnp.max(s, axis=0, keepdims=True) for s in ss]
        pes = [jnp.exp2(s - m) for s, m in zip(ss, ms)]
        pvs = [_bdot(vsum(vown[u], e), pes[i].astype(BF16)) for i, (u, e) in enumerate(heads)]
        accs = [pv[:hd] for pv in pvs]
        ls = [pv[hd:hd + 1] for pv in pvs]
        sels = []
        for gate in gates:
            selt = jnp.zeros((nbp, bq), F32)
            for n in range(nb):
                gn = gate[n:n + 1, :]
                beats = jnp.logical_or(gate > gn, jnp.logical_and(gate == gn, subn < n))
                beats = jnp.logical_and(beats, past)
                cnt = jnp.sum(beats.astype(F32), axis=0, keepdims=True)
                selt = jnp.where(subn == n, (cnt < n_sel).astype(F32), selt)
            sels.append(selt)
        for i, (u, e) in enumerate(heads):
            p = ps[u]
            sel_sc[p, e] = sels[i]
            qs_sc[p, e] = qsts[i]
            m_sc[p, e] = ms[i]
            l_sc[p, e] = ls[i]
            acc_sc[p, e * hd:(e + 1) * hd, :] = accs[i]
        return carry

    lax.fori_loop(0, npair // grp, own_body, 0)

    def kb_body(kb, carry):
        def group_body(gi, c2):
            ps = [gi * grp + u for u in range(grp)]
            kbl = [k_ref[0, p, kb] for p in ps]
            vbl = [vt_ref[0, p, kb] for p in ps]
            qsts = [qs_sc[ps[u], e] for u, e in heads]
            rows = [sel_sc[ps[u], e, pl.ds(kb, 1), :] for u, e in heads]
            m_old = [m_sc[ps[u], e] for u, e in heads]
            l_old = [l_sc[ps[u], e] for u, e in heads]
            a_old = [acc_sc[ps[u], e * hd:(e + 1) * hd, :] for u, e in heads]
            ss = [jnp.where(rows[i] > 0.5, _bdot(kbl[u], qsts[i]), NEG_INF)
                  for i, (u, e) in enumerate(heads)]
            m_new = [jnp.maximum(m, jnp.max(s, axis=0, keepdims=True)) for m, s in zip(m_old, ss)]
            alphas = [jnp.exp2(m - mn) for m, mn in zip(m_old, m_new)]
            pes = [jnp.exp2(s - mn) for s, mn in zip(ss, m_new)]
            pvs = [_bdot(vsum(vbl[u], e), pes[i].astype(BF16)) for i, (u, e) in enumerate(heads)]
            l_new = [a * l + pv[hd:hd + 1] for a, l, pv in zip(alphas, l_old, pvs)]
            a_new = [a * ao + pv[:hd] for a, ao, pv in zip(alphas, a_old, pvs)]
            for i, (u, e) in enumerate(heads):
                p = ps[u]
                m_sc[p, e] = m_new[i]
                l_sc[p, e] = l_new[i]
                acc_sc[p, e * hd:(e + 1) * hd, :] = a_new[i]
            return c2

        lax.fori_loop(0, npair // grp, group_body, 0)
        return carry

    lax.fori_loop(0, qb, kb_body, 0)

    parts = []
    for p in range(npair):
        for e in range(2):
            parts.append(acc_sc[p, e * hd:(e + 1) * hd, :] / l_sc[p, e])
    ot = jnp.concatenate(parts, axis=0)
    o_ref[0] = x + g_ref[0] * _bdot(ot.T.astype(BF16), wo_ref[...])


def _attn(x, sh, sc, gt, nw, w_q, w_o, k5, vt5, km2):
    bsz, s, d = x.shape
    nb = s // B_BLOCK
    npair = d // LANES
    nbp = km2.shape[2]
    n_sel = min(B_TOPK, nb - 1)
    vec = pl.BlockSpec((1, 1, d), lambda b, j: (b, 0, 0))
    full2 = lambda shape: pl.BlockSpec(shape, lambda b, j: (0, 0))
    return pl.pallas_call(
        functools.partial(_attn_kernel, nb=nb, n_sel=n_sel),
        grid=(bsz, nb),
        in_specs=[pl.BlockSpec((1, B_BLOCK, d), lambda b, j: (b, j, 0)), vec, vec, vec,
                  full2((1, d)), full2((d, d)), full2((d, d)),
                  pl.BlockSpec((1, npair, nb, B_BLOCK, LANES), lambda b, j: (b, 0, 0, 0, 0)),
                  pl.BlockSpec((1, npair, nb, LANES, B_BLOCK), lambda b, j: (b, 0, 0, 0, 0)),
                  pl.BlockSpec((1, npair, nbp, LANES), lambda b, j: (b, 0, 0, 0))],
        out_specs=pl.BlockSpec((1, B_BLOCK, d), lambda b, j: (b, j, 0)),
        out_shape=jax.ShapeDtypeStruct((bsz, s, d), F32),
        scratch_shapes=[pltpu.VMEM((npair, LANES, B_BLOCK), F32),
                        pltpu.VMEM((npair, 2, LANES, B_BLOCK), BF16),
                        pltpu.VMEM((npair, LANES, B_BLOCK), F32),
                        pltpu.VMEM((npair, 2, nbp, B_BLOCK), F32),
                        pltpu.VMEM((npair, 2, 1, B_BLOCK), F32),
                        pltpu.VMEM((npair, 2, 1, B_BLOCK), F32)],
        compiler_params=_cparams(("arbitrary", "arbitrary")),
    )(x, sh, sc, gt, nw.reshape(1, d), w_q.T.astype(BF16), w_o.astype(BF16), k5, vt5, km2)


def kernel(x, c, ada_w, ada_b, norm_mix, norm_ffn, a_w_in, a_b_in, a_ln_g, a_ln_b, a_w_s, a_b_s,
           a_w_out, kv_norm, kv_ada_w, kv_ada_b, kv_w_k, kv_w_v, b_w_q, b_w_o, moe_router, moe_bias,
           moe_w_gate, moe_w_up, moe_w_down, sh_w_gate, sh_w_up, sh_w_down, final_norm):
    bsz, s, d = x.shape
    depth = ada_w.shape[0]
    n_a = a_w_in.shape[0]
    assert s % B_BLOCK == 0 and s % TM_SORT == 0 and TM_SORT % A_CHUNK == 0 and d % LANES == 0
    nb = s // B_BLOCK
    npair = d // LANES
    nbp = -(-nb // 8) * 8

    def split(m, n):
        return [m[:, i * d:(i + 1) * d].reshape(bsz, 1, d) for i in range(n)]

    layer_mods = _ada(c, ada_w, ada_b)
    k5 = vt5 = km2 = None
    for i in range(depth):
        sh1, sc1, g1, sh2, sc2, g2 = split(layer_mods[i], 6)
        mixer = None
        if i < n_a:
            mixer = (sh1, sc1, g1, norm_mix[i], a_w_in[i], a_b_in[i], a_ln_g[i], a_ln_b[i],
                     a_w_s[i], a_b_s[i], a_w_out[i])
        else:
            if k5 is None:
                ksh, ksc = split(_ada(c, kv_ada_w[None], kv_ada_b[None])[0], 2)
                k5, vt5, km = _kv(x, ksh, ksc, kv_norm, kv_w_k, kv_w_v)
                km = km.reshape(bsz, nb, npair, LANES).transpose(0, 2, 1, 3)
                km2 = jnp.pad(km, ((0, 0), (0, 0), (0, nbp - nb), (0, 0)))
            j = i - n_a
            x = _attn(x, sh1, sc1, g1, norm_mix[i], b_w_q[j], b_w_o[j], k5, vt5, km2)
        x = _moe(x, (sh2, sc2, g2), norm_ffn[i], moe_router[i], moe_bias[i], i, moe_w_gate,
                 moe_w_up, moe_w_down, sh_w_gate[i], sh_w_up[i], sh_w_down[i],
                 final_norm, i == depth - 1, mixer)
    return x
```

```python
import functools

import jax
import jax.numpy as jnp
from jax import lax
from jax.experimental import pallas as pl
from jax.experimental.pallas import tpu as pltpu

F32 = jnp.float32
BF16 = jnp.bfloat16
I32 = jnp.int32

RMS_EPS = 1e-6
LN_EPS = 1e-5
NEG_INF = -1e30

A_CHUNK = 128
A_GROUPS = 8
B_HEADS = 16
B_BLOCK = 256
B_TOPK = 3
N_EXPERTS = 64
TOP_K = 8
N_GROUPS = 8
TOPK_GROUPS = 4
ROUTED_SCALE = 2.5

LANES = 128
VMEM_LIMIT = 56 * 1024 * 1024

TM_SORT = 512
RUN_ALIGN = 16
SORT_CHUNK = 256
ROWS_TILE_USED = TM_SORT * TOP_K + N_EXPERTS * (RUN_ALIGN - 1)
ROWS_TILE = -(-ROWS_TILE_USED // SORT_CHUNK) * SORT_CHUNK
PIECES_TILE = ROWS_TILE // RUN_ALIGN
PIECES_PAD = -(-PIECES_TILE // LANES) * LANES
BM_EXPERT = 1024
BM_CHAIN = 256
ATTN_PAIR_UNROLL = 4
SUM_ROWS = 16


def _cparams(sem):
    return pltpu.CompilerParams(dimension_semantics=sem, vmem_limit_bytes=VMEM_LIMIT)


def _sigmoid(x):
    return 1.0 / (1.0 + jnp.exp(-x))


def _silu(x):
    return x * _sigmoid(x)


def _gelu_tanh(x):
    return 0.5 * x * (1.0 + jnp.tanh(0.7978845608028654 * (x + 0.044715 * (x * x * x))))


def _rms(x, g):
    return x * lax.rsqrt(jnp.mean(x * x, axis=-1, keepdims=True) + RMS_EPS) * g


def _bdot(a, b):
    return jnp.dot(a, b, preferred_element_type=F32)


def _bdot_nt(a, b):
    return lax.dot_general(a, b, (((1,), (1,)), ((), ())), preferred_element_type=F32)


def _split(a):
    hi = a.astype(BF16)
    lo = (a - hi.astype(F32)).astype(BF16)
    return hi, lo


def _dot3_nt(a, b):
    ah, al = _split(a)
    bh, bl = _split(b)
    return _bdot_nt(ah, bh) + (_bdot_nt(ah, bl) + _bdot_nt(al, bh))


def _dot3(a, b):
    ah, al = _split(a)
    bh, bl = _split(b)
    return _bdot(ah, bh) + (_bdot(ah, bl) + _bdot(al, bh))


def _ada_kernel(c_ref, w_ref, b_ref, o_ref):
    a = _silu(c_ref[...]).astype(BF16)
    o_ref[0] = _bdot(a, w_ref[0].astype(BF16)) + b_ref[0]


def _ada(c, w, b):
    bsz, d = c.shape
    nl, _, n = w.shape
    tn = 1024
    return pl.pallas_call(
        _ada_kernel,
        grid=(nl, n // tn),
        in_specs=[pl.BlockSpec((bsz, d), lambda l, j: (0, 0)),
                  pl.BlockSpec((1, d, tn), lambda l, j: (l, 0, j)),
                  pl.BlockSpec((1, 1, tn), lambda l, j: (l, 0, j))],
        out_specs=pl.BlockSpec((1, bsz, tn), lambda l, j: (l, 0, j)),
        out_shape=jax.ShapeDtypeStruct((nl, bsz, n), F32),
        compiler_params=_cparams(("arbitrary", "arbitrary")),
    )(c, w, b.reshape(nl, 1, n))


def _gmlp_body(x, sh_ref, sc_ref, g_ref, nw_ref, win_ref, bin_ref, lng_ref, lnb_ref,
               ws_ref, bst_ref, wout_ref, y_sc):
    tm = x.shape[0]
    h = _rms(x, nw_ref[...]) * (1.0 + sc_ref[0]) + sh_ref[0]
    z = _gelu_tanh(_bdot(h.astype(BF16), win_ref[...]) + bin_ref[...])
    aw = z.shape[1] // 2
    gd = aw // A_GROUPS
    u = z[:, :aw]
    v = z[:, aw:]
    mu = jnp.mean(v, axis=-1, keepdims=True)
    dv = v - mu
    var = jnp.mean(dv * dv, axis=-1, keepdims=True)
    vn = (dv * lax.rsqrt(var + LN_EPS) * lng_ref[...] + lnb_ref[...]).astype(BF16)
    row = lax.broadcasted_iota(I32, (A_CHUNK, A_CHUNK), 0)
    col = lax.broadcasted_iota(I32, (A_CHUNK, A_CHUNK), 1)
    causal = col <= row
    for g in range(A_GROUPS):
        wg = jnp.where(causal, ws_ref[g], 0.0).astype(BF16)
        bcol = bst_ref[:, g:g + 1]
        for ci in range(tm // A_CHUNK):
            rs = slice(ci * A_CHUNK, (ci + 1) * A_CHUNK)
            cs = slice(g * gd, (g + 1) * gd)
            sv = _bdot(wg, vn[rs, cs]) + bcol
            y_sc[rs, cs] = (u[rs, cs] * sv).astype(BF16)
    return x + g_ref[0] * _bdot(y_sc[...], wout_ref[...])


def _route(x, sh_ref, sc_ref, g_ref, nw_ref, wrt_ref, bias_ref, wsg_ref, wsu_ref, wsd_ref,
           h_ref, base_ref, key_ref, wkey_ref, cnt_ref, pe_ref, po_ref):
    tm = x.shape[0]
    h = _rms(x, nw_ref[...]) * (1.0 + sc_ref[0]) + sh_ref[0]
    hb = h.astype(BF16)
    h_ref[...] = hb
    act = (_silu(_bdot(hb, wsg_ref[...])) * _bdot(hb, wsu_ref[...])).astype(BF16)
    base_ref[...] = x + g_ref[0] * _bdot(act, wsd_ref[...])

    scores = _sigmoid(_dot3_nt(wrt_ref[...], h))
    choice = scores + bias_ref[...]
    gsz = N_EXPERTS // N_GROUPS
    sub = lax.broadcasted_iota(I32, (gsz, tm), 0)
    blocks = [choice[g * gsz:(g + 1) * gsz] for g in range(N_GROUPS)]
    gscore = []
    for blk in blocks:
        m1 = jnp.max(blk, axis=0, keepdims=True)
        i1 = jnp.min(jnp.where(blk == m1, sub, gsz), axis=0, keepdims=True)
        m2 = jnp.max(jnp.where(sub == i1, -jnp.inf, blk), axis=0, keepdims=True)
        gscore.append(m1 + m2)
    masked = []
    for g in range(N_GROUPS):
        beats = jnp.zeros((1, tm), F32)
        for m in range(N_GROUPS):
            if m == g:
                continue
            b = gscore[m] > gscore[g]
            if m < g:
                b = jnp.logical_or(b, gscore[m] == gscore[g])
            beats = beats + b.astype(F32)
        masked.append(jnp.where(beats < TOPK_GROUPS, blocks[g], NEG_INF))
    cur = jnp.concatenate(masked, axis=0)
    eio = lax.broadcasted_iota(I32, (N_EXPERTS, tm), 0)
    chosen = jnp.zeros((N_EXPERTS, tm), jnp.bool_)
    wsum = jnp.zeros((1, tm), F32)
    for _ in range(TOP_K):
        m = jnp.max(cur, axis=0, keepdims=True)
        idx = jnp.min(jnp.where(cur == m, eio, N_EXPERTS), axis=0, keepdims=True)
        sel = eio == idx
        chosen = jnp.logical_or(chosen, sel)
        wsum = wsum + jnp.sum(jnp.where(sel, scores, 0.0), axis=0, keepdims=True)
        cur = jnp.where(sel, -jnp.inf, cur)
    wkey_ref[...] = jnp.where(chosen, scores / wsum * ROUTED_SCALE, 0.0)

    onehot = chosen.astype(BF16)
    r_i = lax.broadcasted_iota(I32, (tm, tm), 0)
    c_i = lax.broadcasted_iota(I32, (tm, tm), 1)
    before = (r_i < c_i).astype(BF16)
    prior = _bdot(onehot, before)
    key_ref[...] = jnp.where(chosen, prior, -1.0).astype(I32)
    cnt = jnp.sum(chosen.astype(F32), axis=1, keepdims=True)
    cnt_ref[0] = cnt

    run_p = jnp.floor((cnt + (RUN_ALIGN - 1)) / RUN_ALIGN)
    ppc = SORT_CHUNK // RUN_ALIGN
    tot = jnp.sum(run_p, axis=0, keepdims=True)
    fill = jnp.ceil(tot / ppc) * ppc - tot
    run_p = run_p + jnp.where(lax.broadcasted_iota(I32, (N_EXPERTS, 1), 0) == N_EXPERTS - 1, fill, 0.0)
    e_r = lax.broadcasted_iota(I32, (N_EXPERTS, N_EXPERTS), 0)
    e_c = lax.broadcasted_iota(I32, (N_EXPERTS, N_EXPERTS), 1)
    incl = (e_c <= e_r).astype(BF16)
    lend = _bdot(incl, jnp.broadcast_to(run_p, (N_EXPERTS, LANES)).astype(BF16))[:, 0:1]
    loff = lend - run_p
    pj = lax.broadcasted_iota(I32, (N_EXPERTS, PIECES_PAD), 1).astype(F32)
    er = lax.broadcasted_iota(I32, (N_EXPERTS, PIECES_PAD), 0).astype(F32)
    pe = jnp.minimum(jnp.sum((lend <= pj).astype(F32), axis=0, keepdims=True), N_EXPERTS - 1.0)
    lo = jnp.sum(jnp.where(er == pe, loff, 0.0), axis=0, keepdims=True)
    pe_ref[0] = pe.astype(I32)
    po_ref[0] = ((pj[0:1, :] - lo) * RUN_ALIGN).astype(I32)


N_ROUTE_REFS = 16


def _router_kernel(x_ref, *refs):
    _route(x_ref[0], *refs)


def _mixer_router_kernel(x_ref, *refs, body, n_mix):
    route_refs = refs[n_mix:n_mix + N_ROUTE_REFS]
    x1 = body(x_ref, refs[:n_mix], refs[n_mix + N_ROUTE_REFS:])
    _route(x1, *route_refs)


def _gmlp_mixer(mods, nw, w_in, b_in, ln_g, ln_b, w_s, b_s, w_out):
    sh, sc, gt = mods
    d, n_in = w_in.shape
    aw = n_in // 2
    vec = pl.BlockSpec((1, 1, d), lambda b, j: (b, 0, 0))
    full2 = lambda shape: pl.BlockSpec(shape, lambda b, j: (0, 0))
    specs = [vec, vec, vec, full2((1, d)), full2((d, n_in)), full2((1, n_in)), full2((1, aw)),
             full2((1, aw)), pl.BlockSpec((A_GROUPS, A_CHUNK, A_CHUNK), lambda b, j: (0, 0, 0)),
             full2((A_CHUNK, A_GROUPS)), full2((aw, d))]
    args = [sh, sc, gt, nw.reshape(1, d), w_in.astype(BF16), b_in.reshape(1, n_in),
            ln_g.reshape(1, aw), ln_b.reshape(1, aw), w_s, b_s.T, w_out.astype(BF16)]

    def body(x_ref, refs, scratch):
        return _gmlp_body(x_ref[0], *refs, *scratch)

    return body, specs, args, [pltpu.VMEM((TM_SORT, aw), BF16)]


def _router(x, mods, nw, w_router, e_bias, wsg, wsu, wsd, mixer=None):
    bsz, s, d = x.shape
    t = bsz * s
    tm = TM_SORT
    nt = s // tm
    sd = wsg.shape[1]
    vec = pl.BlockSpec((1, 1, d), lambda b, j: (b, 0, 0))
    full2 = lambda shape: pl.BlockSpec(shape, lambda b, j: (0, 0))
    tok = pl.BlockSpec((tm, d), lambda b, j: (b * nt + j, 0))
    etok = pl.BlockSpec((N_EXPERTS, tm), lambda b, j: (0, b * nt + j))
    ptab = pl.BlockSpec((1, 1, PIECES_PAD), lambda b, j: (b * nt + j, 0, 0))
    sh, sc, gt = mods
    in_specs = [vec, vec, vec, full2((1, d)), full2((N_EXPERTS, d)), full2((N_EXPERTS, 1)),
                full2((d, sd)), full2((d, sd)), full2((sd, d))]
    args = [sh, sc, gt, nw.reshape(1, d), w_router.T, e_bias.reshape(N_EXPERTS, 1),
            wsg.astype(BF16), wsu.astype(BF16), wsd.astype(BF16)]
    kern, scratch = _router_kernel, []
    if mixer is not None:
        body, mspecs, margs, scratch = mixer
        kern = functools.partial(_mixer_router_kernel, body=body, n_mix=len(margs))
        in_specs = mspecs + in_specs
        args = margs + args
    return pl.pallas_call(
        kern,
        grid=(bsz, nt),
        in_specs=[pl.BlockSpec((1, tm, d), lambda b, j: (b, j, 0))] + in_specs,
        out_specs=[tok, tok, etok, etok,
                   pl.BlockSpec((1, N_EXPERTS, 1), lambda b, j: (b * nt + j, 0, 0)), ptab, ptab],
        out_shape=[jax.ShapeDtypeStruct((t, d), BF16), jax.ShapeDtypeStruct((t, d), F32),
                   jax.ShapeDtypeStruct((N_EXPERTS, t), I32), jax.ShapeDtypeStruct((N_EXPERTS, t), F32),
                   jax.ShapeDtypeStruct((t // tm, N_EXPERTS, 1), F32),
                   jax.ShapeDtypeStruct((t // tm, 1, PIECES_PAD), I32),
                   jax.ShapeDtypeStruct((t // tm, 1, PIECES_PAD), I32)],
        scratch_shapes=scratch,
        compiler_params=_cparams(("arbitrary", "arbitrary")),
    )(x, *args)


def _rows_max(t):
    rows = (t // TM_SORT) * ROWS_TILE + N_EXPERTS * (BM_EXPERT - RUN_ALIGN)
    return -(-rows // BM_EXPERT) * BM_EXPERT


def _sort_meta(cnt, n_rows_max):
    nts = cnt.shape[0]
    c = cnt.reshape(nts, N_EXPERTS).astype(I32)
    run = (c + (RUN_ALIGN - 1)) // RUN_ALIGN * RUN_ALIGN
    fill = (-jnp.sum(run, axis=1, keepdims=True)) % SORT_CHUNK
    run = jnp.concatenate([run[:, :-1], run[:, -1:] + fill], axis=1)
    used = jnp.sum(run, axis=1)
    per_e = jnp.sum(run, axis=0)
    seg = (per_e + (BM_EXPERT - 1)) // BM_EXPERT * BM_EXPERT
    ends = jnp.cumsum(seg)
    starts = ends - seg
    goff = starts[None, :] + jnp.cumsum(run, axis=0) - run
    pad_start = starts + per_e
    pad_cnt = (seg - per_e) // RUN_ALIGN

    nb = n_rows_max // BM_EXPERT
    blo = jnp.arange(nb, dtype=I32) * BM_EXPERT
    be = jnp.sum((ends[None, :] <= blo[:, None]).astype(I32), axis=1)
    real = be < N_EXPERTS
    be = jnp.minimum(be, N_EXPERTS - 1).astype(I32)
    prev = jnp.concatenate([jnp.full((1,), -1, I32), be[:-1]])
    bnew = jnp.logical_and(real, be != prev)
    bidx = jnp.arange(nb, dtype=I32)
    bx = jnp.where(real, bidx, 0)
    later = jnp.logical_and(bidx[None, :] > bidx[:, None], jnp.logical_and(real[None, :], be[None, :] != be[:, None]))
    nxt = jnp.min(jnp.where(later, bidx[None, :], nb), axis=1)
    bfetch = jnp.where(nxt < nb, be[jnp.minimum(nxt, nb - 1)], -1)
    i32 = lambda a: a.reshape(-1).astype(I32)
    return (i32(goff), i32(used), i32(pad_start), i32(pad_cnt)), (bx, be, i32(bnew), i32(real), i32(bfetch))


def _piece_copies(pe_ref, po_ref, goff_ref, used_ref, tile, local, remote, sem, to_remote):
    def copy(j):
        idx = tile * PIECES_PAD + j
        l0 = pl.multiple_of(j * RUN_ALIGN, RUN_ALIGN)
        g0 = pl.multiple_of(goff_ref[tile * N_EXPERTS + pe_ref[idx]] + po_ref[idx], RUN_ALIGN)
        lref = local.at[pl.ds(l0, RUN_ALIGN)]
        gref = remote.at[pl.ds(g0, RUN_ALIGN)]
        return pltpu.make_async_copy(lref, gref, sem) if to_remote else pltpu.make_async_copy(gref, lref, sem)

    def wait_all():
        def body(j, carry):
            copy(j).wait()
            return carry

        lax.fori_loop(0, used_ref[tile] // RUN_ALIGN, body, 0)

    return copy, wait_all


def _pad_copies(pad_start_ref, pad_cnt_ref, tile, ntiles, zeros, remote, sem):
    share = -(-N_EXPERTS // ntiles)

    def apply(act):
        for q in range(share):
            e = tile * share + q
            ec = jnp.minimum(e, N_EXPERTS - 1)
            n = jnp.where(e < N_EXPERTS, pad_cnt_ref[ec], 0)
            base = pad_start_ref[ec]

            def body(j, carry):
                g0 = pl.multiple_of(base + j * RUN_ALIGN, RUN_ALIGN)
                getattr(pltpu.make_async_copy(zeros, remote.at[pl.ds(g0, RUN_ALIGN)], sem), act)()
                return carry

            lax.fori_loop(0, n, body, 0)

    return apply


def _piece_rows(pe_ref, po_ref, key_ref, first_piece, npieces, val_ref=None):
    tm = key_ref.shape[1]
    sub = lax.broadcasted_iota(I32, (RUN_ALIGN, tm), 0)
    out = []
    for jj in range(npieces):
        j = first_piece + jj
        e = pe_ref[j]
        hit = (key_ref[pl.ds(e, 1), :] - po_ref[j]) == sub
        val = 1.0 if val_ref is None else val_ref[pl.ds(e, 1), :]
        out.append(jnp.where(hit, val, 0.0))
    return out


def _dispatch_kernel(pe_ref, po_ref, goff_ref, used_ref, pad_start_ref, pad_cnt_ref,
                     key_ref, h_ref, xs_ref, xbuf, p_sc, zbuf, sem, zsem, *, ntiles):
    i = pl.program_id(0)
    hb = h_ref[...]
    ch = SORT_CHUNK
    ppc = ch // RUN_ALIGN
    zbuf[...] = jnp.zeros_like(zbuf)
    pads = _pad_copies(pad_start_ref, pad_cnt_ref, i, ntiles, zbuf, xs_ref, zsem)
    pads("start")

    copy, wait_all = _piece_copies(pe_ref, po_ref, goff_ref, used_ref, i, xbuf, xs_ref, sem, True)

    def build(ci):
        rows = _piece_rows(pe_ref, po_ref, key_ref, i * PIECES_PAD + ci * ppc, ppc)
        for jj, p in enumerate(rows):
            p_sc[jj * RUN_ALIGN:(jj + 1) * RUN_ALIGN, :] = p.astype(BF16)
        xbuf[pl.ds(pl.multiple_of(ci * ch, ch), ch), :] = _bdot(p_sc[...], hb).astype(BF16)

    def send(ci):
        for jj in range(ppc):
            copy(ci * ppc + jj).start()

    def step(ci, carry):
        send(ci - 1)
        build(ci)
        return carry

    nchunk = used_ref[i] // ch
    build(0)
    lax.fori_loop(1, nchunk, step, 0)
    send(nchunk - 1)
    pads("wait")
    wait_all()


def _dispatch(meta, pe, po, key, hb, n_rows_max):
    t, d = hb.shape
    tm = TM_SORT
    goff, used, pad_start, pad_cnt = meta
    return pl.pallas_call(
        functools.partial(_dispatch_kernel, ntiles=t // tm),
        grid_spec=pltpu.PrefetchScalarGridSpec(
            num_scalar_prefetch=6, grid=(t // tm,),
            in_specs=[pl.BlockSpec((N_EXPERTS, tm), lambda i, *_: (0, i)),
                      pl.BlockSpec((tm, d), lambda i, *_: (i, 0))],
            out_specs=pl.BlockSpec(memory_space=pl.ANY),
            scratch_shapes=[pltpu.VMEM((ROWS_TILE, d), BF16), pltpu.VMEM((SORT_CHUNK, tm), BF16),
                            pltpu.VMEM((RUN_ALIGN, d), BF16),
                            pltpu.SemaphoreType.DMA(()), pltpu.SemaphoreType.DMA(())]),
        out_shape=jax.ShapeDtypeStruct((n_rows_max, d), BF16),
        compiler_params=_cparams(("arbitrary",)),
    )(pe, po, goff, used, pad_start, pad_cnt, key, hb)


def _expert_kernel(bx_ref, be_ref, bnew_ref, breal_ref, bfetch_ref, x_ref, wg_hbm, wu_hbm, wd_hbm, o_ref,
                   wg_st, wu_st, wd_st, wg_sc, wu_sc, wd_sc, sem, *, layer):
    b = pl.program_id(0)

    def fetch(e):
        return [pltpu.make_async_copy(src.at[layer, e], dst, sem.at[i])
                for i, (src, dst) in enumerate(((wg_hbm, wg_st), (wu_hbm, wu_st), (wd_hbm, wd_st)))]

    @pl.when(b == 0)
    def _():
        for cp in fetch(be_ref[0]):
            cp.start()

    @pl.when(bnew_ref[b] == 1)
    def _():
        for cp in fetch(be_ref[b]):
            cp.wait()
        wg_sc[...] = wg_st[...].astype(BF16)
        wu_sc[...] = wu_st[...].astype(BF16)
        wd_sc[...] = wd_st[...].astype(BF16)

        @pl.when(bfetch_ref[b] >= 0)
        def _():
            for cp in fetch(bfetch_ref[b]):
                cp.start()

    @pl.when(breal_ref[b] == 0)
    def _():
        o_ref[...] = jnp.zeros_like(o_ref)

    @pl.when(breal_ref[b] == 1)
    def _():
        parts = [slice(r, r + BM_CHAIN) for r in range(0, BM_EXPERT, BM_CHAIN)]
        xs = [x_ref[rs, :] for rs in parts]
        gs = [_bdot(x, wg_sc[...]) for x in xs]
        us = [_bdot(x, wu_sc[...]) for x in xs]
        acts = [(_silu(g) * u).astype(BF16) for g, u in zip(gs, us)]
        for rs, act in zip(parts, acts):
            o_ref[rs, :] = _bdot(act, wd_sc[...]).astype(BF16)


def _experts(meta, xs, layer, w_gate, w_up, w_down):
    r, d = xs.shape
    ed = w_gate.shape[-1]
    bm = BM_EXPERT
    omap = lambda b, bx, *_: (b, 0)
    xmap = lambda b, bx, *_: (bx[b], 0)
    hbm = pl.BlockSpec(memory_space=pl.ANY)
    return pl.pallas_call(
        functools.partial(_expert_kernel, layer=layer),
        grid_spec=pltpu.PrefetchScalarGridSpec(
            num_scalar_prefetch=5, grid=(r // bm,),
            in_specs=[pl.BlockSpec((bm, d), xmap), hbm, hbm, hbm],
            out_specs=pl.BlockSpec((bm, d), omap),
            scratch_shapes=[pltpu.VMEM((d, ed), F32), pltpu.VMEM((d, ed), F32), pltpu.VMEM((ed, d), F32),
                            pltpu.VMEM((d, ed), BF16), pltpu.VMEM((d, ed), BF16), pltpu.VMEM((ed, d), BF16),
                            pltpu.SemaphoreType.DMA((3,))]),
        out_shape=jax.ShapeDtypeStruct((r, d), BF16),
        compiler_params=_cparams(("arbitrary",)),
    )(*meta, xs, w_gate, w_up, w_down)


def _combine_kernel(pe_ref, po_ref, goff_ref, used_ref, key_ref, wkey_ref, ys_ref, base_ref, g_ref,
                    fn_ref, o_ref, ybuf, q_sc, sem, *, final):
    i = pl.program_id(0)
    tm = base_ref.shape[0]
    ch = SORT_CHUNK
    ppc = ch // RUN_ALIGN

    @pl.when(i == 0)
    def _():
        ybuf[...] = jnp.zeros_like(ybuf)

    copy, wait_all = _piece_copies(pe_ref, po_ref, goff_ref, used_ref, i, ybuf, ys_ref, sem, False)

    for ci in range(ybuf.shape[0] // ch):
        cs = slice(ci * ch, (ci + 1) * ch)

        @pl.when(ci * ch < used_ref[i])
        def _():
            for jj in range(ppc):
                copy(ci * ppc + jj).start()
            rows = _piece_rows(pe_ref, po_ref, key_ref, i * PIECES_PAD + ci * ppc, ppc, wkey_ref)
            q_sc[:, cs] = jnp.concatenate(rows, axis=0).T.astype(BF16)

        @pl.when(ci * ch >= used_ref[i])
        def _():
            q_sc[:, cs] = jnp.zeros((tm, ch), BF16)

    wait_all()
    out = base_ref[...] + g_ref[0] * _bdot(q_sc[...], ybuf[...])
    if final:
        out = _rms(out, fn_ref[...])
    o_ref[...] = out


def _combine(meta, pe, po, key, wkey, ys, base, gt, fnorm, seq, final):
    t, d = base.shape
    tm = TM_SORT
    nt = seq // tm
    return pl.pallas_call(
        functools.partial(_combine_kernel, final=final),
        grid_spec=pltpu.PrefetchScalarGridSpec(
            num_scalar_prefetch=4, grid=(t // tm,),
            in_specs=[pl.BlockSpec((N_EXPERTS, tm), lambda i, *_: (0, i)),
                      pl.BlockSpec((N_EXPERTS, tm), lambda i, *_: (0, i)),
                      pl.BlockSpec(memory_space=pl.ANY),
                      pl.BlockSpec((tm, d), lambda i, *_: (i, 0)),
                      pl.BlockSpec((1, 1, d), lambda i, *_: (i // nt, 0, 0)),
                      pl.BlockSpec((1, d), lambda i, *_: (0, 0))],
            out_specs=pl.BlockSpec((tm, d), lambda i, *_: (i, 0)),
            scratch_shapes=[pltpu.VMEM((ROWS_TILE, d), BF16), pltpu.VMEM((tm, ROWS_TILE), BF16),
                            pltpu.SemaphoreType.DMA(())]),
        out_shape=jax.ShapeDtypeStruct((t, d), F32),
        compiler_params=_cparams(("arbitrary",)),
    )(pe, po, meta[0], meta[1], key, wkey, ys, base, gt, fnorm.reshape(1, d))


def _moe(x, mods, nw, w_router, e_bias, layer, w_gate, w_up, w_down, wsg, wsu, wsd, fnorm, final, mixer):
    bsz, s, d = x.shape
    t = bsz * s
    gt = mods[2]
    n_rows_max = _rows_max(t)
    hb, base, key, wkey, cnt, pe, po = _router(x, mods, nw, w_router, e_bias, wsg, wsu, wsd, mixer)
    pe = pe.reshape(-1)
    po = po.reshape(-1)
    layout_meta, block_meta = _sort_meta(cnt, n_rows_max)
    xs = _dispatch(layout_meta, pe, po, key, hb, n_rows_max)
    ys = _experts(block_meta, xs, layer, w_gate, w_up, w_down)
    out = _combine(layout_meta, pe, po, key, wkey, ys, base, gt, fnorm, s, final)
    return out.reshape(bsz, s, d)


def _kv_kernel(x_ref, sh_ref, sc_ref, nw_ref, wk_ref, wvt_ref, k_ref, vt_ref, km_ref):
    x = x_ref[0]
    hb = (_rms(x, nw_ref[...]) * (1.0 + sc_ref[0]) + sh_ref[0]).astype(BF16)
    k = _bdot(hb, wk_ref[...])
    vt = _bdot_nt(wvt_ref[...], hb)
    for p in range(k.shape[1] // LANES):
        k_ref[0, p, 0] = k[:, p * LANES:(p + 1) * LANES].astype(BF16)
        vt_ref[0, p, 0] = vt[p * LANES:(p + 1) * LANES, :].astype(BF16)
    km_ref[0, 0] = jnp.mean(k, axis=0, keepdims=True)


def _kv(x, sh, sc, nw, w_k, w_v):
    bsz, s, d = x.shape
    nb = s // B_BLOCK
    npair = d // LANES
    vec = pl.BlockSpec((1, 1, d), lambda b, j: (b, 0, 0))
    full2 = lambda shape: pl.BlockSpec(shape, lambda b, j: (0, 0))
    return pl.pallas_call(
        _kv_kernel,
        grid=(bsz, nb),
        in_specs=[pl.BlockSpec((1, B_BLOCK, d), lambda b, j: (b, j, 0)), vec, vec,
                  full2((1, d)), full2((d, d)), full2((d, d))],
        out_specs=[pl.BlockSpec((1, npair, 1, B_BLOCK, LANES), lambda b, j: (b, 0, j, 0, 0)),
                   pl.BlockSpec((1, npair, 1, LANES, B_BLOCK), lambda b, j: (b, 0, j, 0, 0)),
                   pl.BlockSpec((1, 1, 1, d), lambda b, j: (b, j, 0, 0))],
        out_shape=[jax.ShapeDtypeStruct((bsz, npair, nb, B_BLOCK, LANES), BF16),
                   jax.ShapeDtypeStruct((bsz, npair, nb, LANES, B_BLOCK), BF16),
                   jax.ShapeDtypeStruct((bsz, nb, 1, d), F32)],
        compiler_params=_cparams(("arbitrary", "arbitrary")),
    )(x, sh, sc, nw.reshape(1, d), w_k.astype(BF16), w_v.T.astype(BF16))


def _attn_body(x, qb, sh_ref, sc_ref, g_ref, nw_ref, wqt_ref, wo_ref, k_ref, vt_ref, km_ref,
               qt_sc, qs_sc, acc_sc, sel_sc, m_sc, l_sc, *, nb, n_sel):
    bq = x.shape[0]
    npair = qt_sc.shape[0]
    nbp = km_ref.shape[2]
    hd = LANES // 2
    scale = float(hd) ** -0.5 * 1.4426950408889634
    h = _rms(x, nw_ref[...]) * (1.0 + sc_ref[0]) + sh_ref[0]
    qt = _bdot_nt(wqt_ref[...], h.astype(BF16))
    for p in range(npair):
        qt_sc[p] = qt[p * LANES:(p + 1) * LANES, :]

    subn = lax.broadcasted_iota(I32, (nbp, bq), 0)
    past = subn < qb
    krow = lax.broadcasted_iota(I32, (B_BLOCK, bq), 0)
    qcol = lax.broadcasted_iota(I32, (B_BLOCK, bq), 1)
    causal = krow <= qcol
    rowh = lax.broadcasted_iota(I32, (LANES, 1), 0)

    grp = ATTN_PAIR_UNROLL
    heads = [(u, e) for u in range(grp) for e in range(2)]
    ones_rows = jnp.ones((SUM_ROWS, B_BLOCK), BF16)

    def vsum(vt2, e):
        return jnp.concatenate([vt2[e * hd:(e + 1) * hd, :], ones_rows], axis=0)

    def own_body(gi, carry):
        ps = [gi * grp + u for u in range(grp)]
        q2ts = [qt_sc[p] for p in ps]
        kms = [km_ref[0, p] for p in ps]
        kown = [k_ref[0, p, qb] for p in ps]
        vown = [vt_ref[0, p, qb] for p in ps]
        qets = [jnp.where((rowh >= hd) if e == 1 else (rowh < hd), q2ts[u], 0.0) for u, e in heads]
        qsts = [(q * scale).astype(BF16) for q in qets]
        ss = [jnp.where(causal, _bdot(kown[u], qsts[i]), NEG_INF) for i, (u, e) in enumerate(heads)]
        gates = [_dot3(kms[u], qets[i]) for i, (u, e) in enumerate(heads)]
        ms = [jnp.max(s, axis=0, keepdims=True) for s in ss]
        pes = [jnp.exp2(s - m) for s, m in zip(ss, ms)]
        pvs = [_bdot(vsum(vown[u], e), pes[i].astype(BF16)) for i, (u, e) in enumerate(heads)]
        accs = [pv[:hd] for pv in pvs]
        ls = [pv[hd:hd + 1] for pv in pvs]
        sels = []
        for gate in gates:
            selt = jnp.zeros((nbp, bq), F32)
            for n in range(nb):
                gn = gate[n:n + 1, :]
                beats = jnp.logical_or(gate > gn, jnp.logical_and(gate == gn, subn < n))
                beats = jnp.logical_and(beats, past)
                cnt = jnp.sum(beats.astype(F32), axis=0, keepdims=True)
                selt = jnp.where(subn == n, (cnt < n_sel).astype(F32), selt)
            sels.append(selt)
        for i, (u, e) in enumerate(heads):
            p = ps[u]
            sel_sc[p, e] = sels[i]
            qs_sc[p, e] = qsts[i]
            m_sc[p, e] = ms[i]
            l_sc[p, e] = ls[i]
            acc_sc[p, e * hd:(e + 1) * hd, :] = accs[i]
        return carry

    lax.fori_loop(0, npair // grp, own_body, 0)

    def kb_body(kb, carry):
        def group_body(gi, c2):
            ps = [gi * grp + u for u in range(grp)]
            kbl = [k_ref[0, p, kb] for p in ps]
            vbl = [vt_ref[0, p, kb] for p in ps]
            qsts = [qs_sc[ps[u], e] for u, e in heads]
            rows = [sel_sc[ps[u], e, pl.ds(kb, 1), :] for u, e in heads]
            m_old = [m_sc[ps[u], e] for u, e in heads]
            l_old = [l_sc[ps[u], e] for u, e in heads]
            a_old = [acc_sc[ps[u], e * hd:(e + 1) * hd, :] for u, e in heads]
            ss = [jnp.where(rows[i] > 0.5, _bdot(kbl[u], qsts[i]), NEG_INF)
                  for i, (u, e) in enumerate(heads)]
            m_new = [jnp.maximum(m, jnp.max(s, axis=0, keepdims=True)) for m, s in zip(m_old, ss)]
            alphas = [jnp.exp2(m - mn) for m, mn in zip(m_old, m_new)]
            pes = [jnp.exp2(s - mn) for s, mn in zip(ss, m_new)]
            pvs = [_bdot(vsum(vbl[u], e), pes[i].astype(BF16)) for i, (u, e) in enumerate(heads)]
            l_new = [a * l + pv[hd:hd + 1] for a, l, pv in zip(alphas, l_old, pvs)]
            a_new = [a * ao + pv[:hd] for a, ao, pv in zip(alphas, a_old, pvs)]
            for i, (u, e) in enumerate(heads):
                p = ps[u]
                m_sc[p, e] = m_new[i]
                l_sc[p, e] = l_new[i]
                acc_sc[p, e * hd:(e + 1) * hd, :] = a_new[i]
            return c2

        lax.fori_loop(0, npair // grp, group_body, 0)
        return carry

    lax.fori_loop(0, qb, kb_body, 0)

    parts = []
    for p in range(npair):
        for e in range(2):
            parts.append(acc_sc[p, e * hd:(e + 1) * hd, :] / l_sc[p, e])
    ot = jnp.concatenate(parts, axis=0)
    return x + g_ref[0] * _bdot(ot.T.astype(BF16), wo_ref[...])


def _attn_mixer(mods, nw, w_q, w_o, k5, vt5, km2):
    sh, sc, gt = mods
    d = w_q.shape[0]
    npair, nb = k5.shape[1], k5.shape[2]
    nbp = km2.shape[2]
    n_sel = min(B_TOPK, nb - 1)
    per = TM_SORT // B_BLOCK
    vec = pl.BlockSpec((1, 1, d), lambda b, j: (b, 0, 0))
    full2 = lambda shape: pl.BlockSpec(shape, lambda b, j: (0, 0))
    specs = [vec, vec, vec, full2((1, d)), full2((d, d)), full2((d, d)),
             pl.BlockSpec((1, npair, nb, B_BLOCK, LANES), lambda b, j: (b, 0, 0, 0, 0)),
             pl.BlockSpec((1, npair, nb, LANES, B_BLOCK), lambda b, j: (b, 0, 0, 0, 0)),
             pl.BlockSpec((1, npair, nbp, LANES), lambda b, j: (b, 0, 0, 0))]
    args = [sh, sc, gt, nw.reshape(1, d), w_q.T.astype(BF16), w_o.astype(BF16), k5, vt5, km2]
    scratch = [pltpu.VMEM((npair, LANES, B_BLOCK), F32),
               pltpu.VMEM((npair, 2, LANES, B_BLOCK), BF16),
               pltpu.VMEM((npair, LANES, B_BLOCK), F32),
               pltpu.VMEM((npair, 2, nbp, B_BLOCK), F32),
               pltpu.VMEM((npair, 2, 1, B_BLOCK), F32),
               pltpu.VMEM((npair, 2, 1, B_BLOCK), F32)]

    def body(x_ref, refs, scr):
        j = pl.program_id(1)
        outs = [_attn_body(x_ref[0, h * B_BLOCK:(h + 1) * B_BLOCK, :], j * per + h, *refs, *scr,
                           nb=nb, n_sel=n_sel) for h in range(per)]
        return jnp.concatenate(outs, axis=0)

    return body, specs, args, scratch


def kernel(x, c, ada_w, ada_b, norm_mix, norm_ffn, a_w_in, a_b_in, a_ln_g, a_ln_b, a_w_s, a_b_s,
           a_w_out, kv_norm, kv_ada_w, kv_ada_b, kv_w_k, kv_w_v, b_w_q, b_w_o, moe_router, moe_bias,
           moe_w_gate, moe_w_up, moe_w_down, sh_w_gate, sh_w_up, sh_w_down, final_norm):
    bsz, s, d = x.shape
    depth = ada_w.shape[0]
    n_a = a_w_in.shape[0]
    assert s % B_BLOCK == 0 and s % TM_SORT == 0 and TM_SORT % A_CHUNK == 0 and d % LANES == 0
    nb = s // B_BLOCK
    npair = d // LANES
    nbp = -(-nb // 8) * 8

    def split(m, n):
        return [m[:, i * d:(i + 1) * d].reshape(bsz, 1, d) for i in range(n)]

    layer_mods = _ada(c, ada_w, ada_b)
    k5 = vt5 = km2 = None
    for i in range(depth):
        sh1, sc1, g1, sh2, sc2, g2 = split(layer_mods[i], 6)
        if i < n_a:
            mixer = _gmlp_mixer((sh1, sc1, g1), norm_mix[i], a_w_in[i], a_b_in[i], a_ln_g[i], a_ln_b[i],
                                a_w_s[i], a_b_s[i], a_w_out[i])
        else:
            if k5 is None:
                ksh, ksc = split(_ada(c, kv_ada_w[None], kv_ada_b[None])[0], 2)
                k5, vt5, km = _kv(x, ksh, ksc, kv_norm, kv_w_k, kv_w_v)
                km = km.reshape(bsz, nb, npair, LANES).transpose(0, 2, 1, 3)
                km2 = jnp.pad(km, ((0, 0), (0, 0), (0, nbp - nb), (0, 0)))
            j = i - n_a
            mixer = _attn_mixer((sh1, sc1, g1), norm_mix[i], b_w_q[j], b_w_o[j], k5, vt5, km2)
        x = _moe(x, (sh2, sc2, g2), norm_ffn[i], moe_router[i], moe_bias[i], i, moe_w_gate,
                 moe_w_up, moe_w_down, sh_w_gate[i], sh_w_up[i], sh_w_down[i],
                 final_norm, i == depth - 1, mixer)
    return x
```

```python
import functools

import jax
import jax.numpy as jnp
from jax import lax
from jax.experimental import pallas as pl
from jax.experimental.pallas import tpu as pltpu

F32 = jnp.float32
BF16 = jnp.bfloat16
I32 = jnp.int32

RMS_EPS = 1e-6
LN_EPS = 1e-5
NEG_INF = -1e30

A_CHUNK = 128
A_GROUPS = 8
B_HEADS = 16
B_BLOCK = 256
B_TOPK = 3
N_EXPERTS = 64
TOP_K = 8
N_GROUPS = 8
TOPK_GROUPS = 4
ROUTED_SCALE = 2.5

LANES = 128
VMEM_LIMIT = 56 * 1024 * 1024

TM_SORT = 512
RUN_ALIGN = 16
SORT_CHUNK = 256
ROWS_TILE_USED = TM_SORT * TOP_K + N_EXPERTS * (RUN_ALIGN - 1)
ROWS_TILE = -(-ROWS_TILE_USED // SORT_CHUNK) * SORT_CHUNK
PIECES_TILE = ROWS_TILE // RUN_ALIGN
PIECES_PAD = -(-PIECES_TILE // LANES) * LANES
BM_EXPERT = 1024
BM_CHAIN = 256
ATTN_PAIR_UNROLL = 4
ATTN_PAIR_UNROLL_PAST = 8
SUM_ROWS = 16


def _cparams(sem):
    return pltpu.CompilerParams(dimension_semantics=sem, vmem_limit_bytes=VMEM_LIMIT)


def _sigmoid(x):
    return 1.0 / (1.0 + jnp.exp(-x))


def _silu(x):
    return x * _sigmoid(x)


def _gelu_tanh(x):
    hx = 0.5 * x
    return hx + hx * jnp.tanh(x * (0.7978845608028654 + 0.035677408136300125 * (x * x)))


def _rms(x, g):
    return x * lax.rsqrt(jnp.mean(x * x, axis=-1, keepdims=True) + RMS_EPS) * g


def _bdot(a, b):
    return jnp.dot(a, b, preferred_element_type=F32)


def _bdot_nt(a, b):
    return lax.dot_general(a, b, (((1,), (1,)), ((), ())), preferred_element_type=F32)


def _split(a):
    hi = a.astype(BF16)
    lo = (a - hi.astype(F32)).astype(BF16)
    return hi, lo


def _dot3_nt(a, b):
    ah, al = _split(a)
    bh, bl = _split(b)
    return _bdot_nt(ah, bh) + (_bdot_nt(ah, bl) + _bdot_nt(al, bh))


def _dot3(a, b):
    ah, al = _split(a)
    bh, bl = _split(b)
    return _bdot(ah, bh) + (_bdot(ah, bl) + _bdot(al, bh))


def _ada_kernel(c_ref, w_ref, b_ref, o_ref):
    a = _silu(c_ref[...]).astype(BF16)
    o_ref[0] = _bdot(a, w_ref[0].astype(BF16)) + b_ref[0]


def _ada(c, w, b):
    bsz, d = c.shape
    nl, _, n = w.shape
    tn = 1024
    return pl.pallas_call(
        _ada_kernel,
        grid=(nl, n // tn),
        in_specs=[pl.BlockSpec((bsz, d), lambda l, j: (0, 0)),
                  pl.BlockSpec((1, d, tn), lambda l, j: (l, 0, j)),
                  pl.BlockSpec((1, 1, tn), lambda l, j: (l, 0, j))],
        out_specs=pl.BlockSpec((1, bsz, tn), lambda l, j: (l, 0, j)),
        out_shape=jax.ShapeDtypeStruct((nl, bsz, n), F32),
        compiler_params=_cparams(("arbitrary", "arbitrary")),
    )(c, w, b.reshape(nl, 1, n))


def _gmlp_body(x, sh_ref, sc_ref, g_ref, nw_ref, win_ref, bin_ref, lng_ref, lnb_ref,
               ws_ref, bst_ref, wout_ref, y_sc):
    tm = x.shape[0]
    h = _rms(x, nw_ref[...]) * (1.0 + sc_ref[0]) + sh_ref[0]
    z = _gelu_tanh(_bdot(h.astype(BF16), win_ref[...]) + bin_ref[...])
    aw = z.shape[1] // 2
    gd = aw // A_GROUPS
    u = z[:, :aw]
    v = z[:, aw:]
    mu = jnp.mean(v, axis=-1, keepdims=True)
    dv = v - mu
    var = jnp.mean(dv * dv, axis=-1, keepdims=True)
    vn = (dv * lax.rsqrt(var + LN_EPS) * lng_ref[...] + lnb_ref[...]).astype(BF16)
    row = lax.broadcasted_iota(I32, (A_CHUNK, A_CHUNK), 0)
    col = lax.broadcasted_iota(I32, (A_CHUNK, A_CHUNK), 1)
    causal = col <= row
    for g in range(A_GROUPS):
        wg = jnp.where(causal, ws_ref[g], 0.0).astype(BF16)
        bcol = bst_ref[:, g:g + 1]
        for ci in range(tm // A_CHUNK):
            rs = slice(ci * A_CHUNK, (ci + 1) * A_CHUNK)
            cs = slice(g * gd, (g + 1) * gd)
            sv = _bdot(wg, vn[rs, cs]) + bcol
            y_sc[rs, cs] = (u[rs, cs] * sv).astype(BF16)
    return x + g_ref[0] * _bdot(y_sc[...], wout_ref[...])


def _route(x, sh_ref, sc_ref, g_ref, nw_ref, wrt_ref, bias_ref, wsg_ref, wsu_ref, wsd_ref,
           h_ref, base_ref, key_ref, wkey_ref, cnt_ref, pe_ref, po_ref):
    tm = x.shape[0]
    h = _rms(x, nw_ref[...]) * (1.0 + sc_ref[0]) + sh_ref[0]
    hb = h.astype(BF16)
    h_ref[...] = hb
    act = (_silu(_bdot(hb, wsg_ref[...])) * _bdot(hb, wsu_ref[...])).astype(BF16)
    base_ref[...] = x + g_ref[0] * _bdot(act, wsd_ref[...])

    scores = _sigmoid(_dot3_nt(wrt_ref[...], h))
    choice = scores + bias_ref[...]
    gsz = N_EXPERTS // N_GROUPS
    sub = lax.broadcasted_iota(I32, (gsz, tm), 0)
    blocks = [choice[g * gsz:(g + 1) * gsz] for g in range(N_GROUPS)]
    gscore = []
    for blk in blocks:
        m1 = jnp.max(blk, axis=0, keepdims=True)
        i1 = jnp.min(jnp.where(blk == m1, sub, gsz), axis=0, keepdims=True)
        m2 = jnp.max(jnp.where(sub == i1, -jnp.inf, blk), axis=0, keepdims=True)
        gscore.append(m1 + m2)
    masked = []
    for g in range(N_GROUPS):
        beats = jnp.zeros((1, tm), F32)
        for m in range(N_GROUPS):
            if m == g:
                continue
            b = gscore[m] > gscore[g]
            if m < g:
                b = jnp.logical_or(b, gscore[m] == gscore[g])
            beats = beats + b.astype(F32)
        masked.append(jnp.where(beats < TOPK_GROUPS, blocks[g], NEG_INF))
    cur = jnp.concatenate(masked, axis=0)
    eio = lax.broadcasted_iota(I32, (N_EXPERTS, tm), 0)
    chosen = jnp.zeros((N_EXPERTS, tm), jnp.bool_)
    wsum = jnp.zeros((1, tm), F32)
    for _ in range(TOP_K):
        m = jnp.max(cur, axis=0, keepdims=True)
        idx = jnp.min(jnp.where(cur == m, eio, N_EXPERTS), axis=0, keepdims=True)
        sel = eio == idx
        chosen = jnp.logical_or(chosen, sel)
        wsum = wsum + jnp.sum(jnp.where(sel, scores, 0.0), axis=0, keepdims=True)
        cur = jnp.where(sel, -jnp.inf, cur)
    wkey_ref[...] = jnp.where(chosen, scores / wsum * ROUTED_SCALE, 0.0)

    onehot = chosen.astype(BF16)
    r_i = lax.broadcasted_iota(I32, (tm, tm), 0)
    c_i = lax.broadcasted_iota(I32, (tm, tm), 1)
    before = (r_i < c_i).astype(BF16)
    prior = _bdot(onehot, before)
    key_ref[...] = jnp.where(chosen, prior, -1.0).astype(I32)
    cnt = jnp.sum(chosen.astype(F32), axis=1, keepdims=True)
    cnt_ref[0] = cnt

    run_p = jnp.floor((cnt + (RUN_ALIGN - 1)) / RUN_ALIGN)
    ppc = SORT_CHUNK // RUN_ALIGN
    tot = jnp.sum(run_p, axis=0, keepdims=True)
    fill = jnp.ceil(tot / ppc) * ppc - tot
    run_p = run_p + jnp.where(lax.broadcasted_iota(I32, (N_EXPERTS, 1), 0) == N_EXPERTS - 1, fill, 0.0)
    e_r = lax.broadcasted_iota(I32, (N_EXPERTS, N_EXPERTS), 0)
    e_c = lax.broadcasted_iota(I32, (N_EXPERTS, N_EXPERTS), 1)
    incl = (e_c <= e_r).astype(BF16)
    lend = _bdot(incl, jnp.broadcast_to(run_p, (N_EXPERTS, LANES)).astype(BF16))[:, 0:1]
    loff = lend - run_p
    pj = lax.broadcasted_iota(I32, (N_EXPERTS, PIECES_PAD), 1).astype(F32)
    er = lax.broadcasted_iota(I32, (N_EXPERTS, PIECES_PAD), 0).astype(F32)
    pe = jnp.minimum(jnp.sum((lend <= pj).astype(F32), axis=0, keepdims=True), N_EXPERTS - 1.0)
    lo = jnp.sum(jnp.where(er == pe, loff, 0.0), axis=0, keepdims=True)
    pe_ref[0] = pe.astype(I32)
    po_ref[0] = ((pj[0:1, :] - lo) * RUN_ALIGN).astype(I32)


N_ROUTE_REFS = 16


def _router_kernel(x_ref, *refs):
    _route(x_ref[0], *refs)


def _mixer_router_kernel(x_ref, *refs, body, n_mix):
    route_refs = refs[n_mix:n_mix + N_ROUTE_REFS]
    x1 = body(x_ref, refs[:n_mix], refs[n_mix + N_ROUTE_REFS:])
    _route(x1, *route_refs)


def _gmlp_mixer(mods, nw, w_in, b_in, ln_g, ln_b, w_s, b_s, w_out):
    sh, sc, gt = mods
    d, n_in = w_in.shape
    aw = n_in // 2
    vec = pl.BlockSpec((1, 1, d), lambda b, j: (b, 0, 0))
    full2 = lambda shape: pl.BlockSpec(shape, lambda b, j: (0, 0))
    specs = [vec, vec, vec, full2((1, d)), full2((d, n_in)), full2((1, n_in)), full2((1, aw)),
             full2((1, aw)), pl.BlockSpec((A_GROUPS, A_CHUNK, A_CHUNK), lambda b, j: (0, 0, 0)),
             full2((A_CHUNK, A_GROUPS)), full2((aw, d))]
    args = [sh, sc, gt, nw.reshape(1, d), w_in.astype(BF16), b_in.reshape(1, n_in),
            ln_g.reshape(1, aw), ln_b.reshape(1, aw), w_s, b_s.T, w_out.astype(BF16)]

    def body(x_ref, refs, scratch):
        return _gmlp_body(x_ref[0], *refs, *scratch)

    return body, specs, args, [pltpu.VMEM((TM_SORT, aw), BF16)]


def _router(x, mods, nw, w_router, e_bias, wsg, wsu, wsd, mixer=None):
    bsz, s, d = x.shape
    t = bsz * s
    tm = TM_SORT
    nt = s // tm
    sd = wsg.shape[1]
    vec = pl.BlockSpec((1, 1, d), lambda b, j: (b, 0, 0))
    full2 = lambda shape: pl.BlockSpec(shape, lambda b, j: (0, 0))
    tok = pl.BlockSpec((tm, d), lambda b, j: (b * nt + j, 0))
    etok = pl.BlockSpec((N_EXPERTS, tm), lambda b, j: (0, b * nt + j))
    ptab = pl.BlockSpec((1, 1, PIECES_PAD), lambda b, j: (b * nt + j, 0, 0))
    sh, sc, gt = mods
    in_specs = [vec, vec, vec, full2((1, d)), full2((N_EXPERTS, d)), full2((N_EXPERTS, 1)),
                full2((d, sd)), full2((d, sd)), full2((sd, d))]
    args = [sh, sc, gt, nw.reshape(1, d), w_router.T, e_bias.reshape(N_EXPERTS, 1),
            wsg.astype(BF16), wsu.astype(BF16), wsd.astype(BF16)]
    kern, scratch = _router_kernel, []
    if mixer is not None:
        body, mspecs, margs, scratch = mixer
        kern = functools.partial(_mixer_router_kernel, body=body, n_mix=len(margs))
        in_specs = mspecs + in_specs
        args = margs + args
    return pl.pallas_call(
        kern,
        grid=(bsz, nt),
        in_specs=[pl.BlockSpec((1, tm, d), lambda b, j: (b, j, 0))] + in_specs,
        out_specs=[tok, tok, etok, etok,
                   pl.BlockSpec((1, N_EXPERTS, 1), lambda b, j: (b * nt + j, 0, 0)), ptab, ptab],
        out_shape=[jax.ShapeDtypeStruct((t, d), BF16), jax.ShapeDtypeStruct((t, d), F32),
                   jax.ShapeDtypeStruct((N_EXPERTS, t), I32), jax.ShapeDtypeStruct((N_EXPERTS, t), F32),
                   jax.ShapeDtypeStruct((t // tm, N_EXPERTS, 1), F32),
                   jax.ShapeDtypeStruct((t // tm, 1, PIECES_PAD), I32),
                   jax.ShapeDtypeStruct((t // tm, 1, PIECES_PAD), I32)],
        scratch_shapes=scratch,
        compiler_params=_cparams(("arbitrary", "arbitrary")),
    )(x, *args)


def _rows_max(t):
    rows = (t // TM_SORT) * ROWS_TILE + N_EXPERTS * (BM_EXPERT - RUN_ALIGN)
    return -(-rows // BM_EXPERT) * BM_EXPERT


def _sort_meta(cnt, n_rows_max):
    nts = cnt.shape[0]
    c = cnt.reshape(nts, N_EXPERTS).astype(I32)
    run = (c + (RUN_ALIGN - 1)) // RUN_ALIGN * RUN_ALIGN
    fill = (-jnp.sum(run, axis=1, keepdims=True)) % SORT_CHUNK
    run = jnp.concatenate([run[:, :-1], run[:, -1:] + fill], axis=1)
    used = jnp.sum(run, axis=1)
    per_e = jnp.sum(run, axis=0)
    seg = (per_e + (BM_EXPERT - 1)) // BM_EXPERT * BM_EXPERT
    ends = jnp.cumsum(seg)
    starts = ends - seg
    goff = starts[None, :] + jnp.cumsum(run, axis=0) - run
    pad_start = starts + per_e
    pad_cnt = (seg - per_e) // RUN_ALIGN

    nb = n_rows_max // BM_EXPERT
    blo = jnp.arange(nb, dtype=I32) * BM_EXPERT
    be = jnp.sum((ends[None, :] <= blo[:, None]).astype(I32), axis=1)
    real = be < N_EXPERTS
    be = jnp.minimum(be, N_EXPERTS - 1).astype(I32)
    prev = jnp.concatenate([jnp.full((1,), -1, I32), be[:-1]])
    bnew = jnp.logical_and(real, be != prev)
    bidx = jnp.arange(nb, dtype=I32)
    bx = jnp.where(real, bidx, 0)
    later = jnp.logical_and(bidx[None, :] > bidx[:, None], jnp.logical_and(real[None, :], be[None, :] != be[:, None]))
    nxt = jnp.min(jnp.where(later, bidx[None, :], nb), axis=1)
    bfetch = jnp.where(nxt < nb, be[jnp.minimum(nxt, nb - 1)], -1)
    i32 = lambda a: a.reshape(-1).astype(I32)
    return (i32(goff), i32(used), i32(pad_start), i32(pad_cnt)), (bx, be, i32(bnew), i32(real), i32(bfetch))


def _piece_copies(pe_ref, po_ref, goff_ref, used_ref, tile, local, remote, sem, to_remote):
    def copy(j):
        idx = tile * PIECES_PAD + j
        l0 = pl.multiple_of(j * RUN_ALIGN, RUN_ALIGN)
        g0 = pl.multiple_of(goff_ref[tile * N_EXPERTS + pe_ref[idx]] + po_ref[idx], RUN_ALIGN)
        lref = local.at[pl.ds(l0, RUN_ALIGN)]
        gref = remote.at[pl.ds(g0, RUN_ALIGN)]
        return pltpu.make_async_copy(lref, gref, sem) if to_remote else pltpu.make_async_copy(gref, lref, sem)

    def wait_all():
        def body(j, carry):
            copy(j).wait()
            return carry

        lax.fori_loop(0, used_ref[tile] // RUN_ALIGN, body, 0)

    return copy, wait_all


def _pad_copies(pad_start_ref, pad_cnt_ref, tile, ntiles, zeros, remote, sem):
    share = -(-N_EXPERTS // ntiles)

    def apply(act):
        for q in range(share):
            e = tile * share + q
            ec = jnp.minimum(e, N_EXPERTS - 1)
            n = jnp.where(e < N_EXPERTS, pad_cnt_ref[ec], 0)
            base = pad_start_ref[ec]

            def body(j, carry):
                g0 = pl.multiple_of(base + j * RUN_ALIGN, RUN_ALIGN)
                getattr(pltpu.make_async_copy(zeros, remote.at[pl.ds(g0, RUN_ALIGN)], sem), act)()
                return carry

            lax.fori_loop(0, n, body, 0)

    return apply


def _piece_rows(pe_ref, po_ref, key_ref, first_piece, npieces, val_ref=None):
    tm = key_ref.shape[1]
    sub = lax.broadcasted_iota(I32, (RUN_ALIGN, tm), 0)
    out = []
    for jj in range(npieces):
        j = first_piece + jj
        e = pe_ref[j]
        hit = (key_ref[pl.ds(e, 1), :] - po_ref[j]) == sub
        val = 1.0 if val_ref is None else val_ref[pl.ds(e, 1), :]
        out.append(jnp.where(hit, val, 0.0))
    return out


def _dispatch_kernel(pe_ref, po_ref, goff_ref, used_ref, pad_start_ref, pad_cnt_ref,
                     key_ref, h_ref, xs_ref, xbuf, p_sc, zbuf, sem, zsem, *, ntiles):
    i = pl.program_id(0)
    hb = h_ref[...]
    ch = SORT_CHUNK
    ppc = ch // RUN_ALIGN
    zbuf[...] = jnp.zeros_like(zbuf)
    pads = _pad_copies(pad_start_ref, pad_cnt_ref, i, ntiles, zbuf, xs_ref, zsem)
    pads("start")

    copy, wait_all = _piece_copies(pe_ref, po_ref, goff_ref, used_ref, i, xbuf, xs_ref, sem, True)

    def build(ci):
        rows = _piece_rows(pe_ref, po_ref, key_ref, i * PIECES_PAD + ci * ppc, ppc)
        for jj, p in enumerate(rows):
            p_sc[jj * RUN_ALIGN:(jj + 1) * RUN_ALIGN, :] = p.astype(BF16)
        xbuf[pl.ds(pl.multiple_of(ci * ch, ch), ch), :] = _bdot(p_sc[...], hb).astype(BF16)

    def send(ci):
        for jj in range(ppc):
            copy(ci * ppc + jj).start()

    def step(ci, carry):
        send(ci - 1)
        build(ci)
        return carry

    nchunk = used_ref[i] // ch
    build(0)
    lax.fori_loop(1, nchunk, step, 0)
    send(nchunk - 1)
    pads("wait")
    wait_all()


def _dispatch(meta, pe, po, key, hb, n_rows_max):
    t, d = hb.shape
    tm = TM_SORT
    goff, used, pad_start, pad_cnt = meta
    return pl.pallas_call(
        functools.partial(_dispatch_kernel, ntiles=t // tm),
        grid_spec=pltpu.PrefetchScalarGridSpec(
            num_scalar_prefetch=6, grid=(t // tm,),
            in_specs=[pl.BlockSpec((N_EXPERTS, tm), lambda i, *_: (0, i)),
                      pl.BlockSpec((tm, d), lambda i, *_: (i, 0))],
            out_specs=pl.BlockSpec(memory_space=pl.ANY),
            scratch_shapes=[pltpu.VMEM((ROWS_TILE, d), BF16), pltpu.VMEM((SORT_CHUNK, tm), BF16),
                            pltpu.VMEM((RUN_ALIGN, d), BF16),
                            pltpu.SemaphoreType.DMA(()), pltpu.SemaphoreType.DMA(())]),
        out_shape=jax.ShapeDtypeStruct((n_rows_max, d), BF16),
        compiler_params=_cparams(("arbitrary",)),
    )(pe, po, goff, used, pad_start, pad_cnt, key, hb)


def _expert_kernel(bx_ref, be_ref, bnew_ref, breal_ref, bfetch_ref, x_ref, wg_hbm, wu_hbm, wd_hbm, o_ref,
                   wg_st, wu_st, wd_st, wg_sc, wu_sc, wd_sc, sem, *, layer):
    b = pl.program_id(0)

    def fetch(e):
        return [pltpu.make_async_copy(src.at[layer, e], dst, sem.at[i])
                for i, (src, dst) in enumerate(((wg_hbm, wg_st), (wu_hbm, wu_st), (wd_hbm, wd_st)))]

    @pl.when(b == 0)
    def _():
        for cp in fetch(be_ref[0]):
            cp.start()

    @pl.when(bnew_ref[b] == 1)
    def _():
        for cp in fetch(be_ref[b]):
            cp.wait()
        wg_sc[...] = wg_st[...].astype(BF16)
        wu_sc[...] = wu_st[...].astype(BF16)
        wd_sc[...] = wd_st[...].astype(BF16)

        @pl.when(bfetch_ref[b] >= 0)
        def _():
            for cp in fetch(bfetch_ref[b]):
                cp.start()

    @pl.when(breal_ref[b] == 0)
    def _():
        o_ref[...] = jnp.zeros_like(o_ref)

    @pl.when(breal_ref[b] == 1)
    def _():
        parts = [slice(r, r + BM_CHAIN) for r in range(0, BM_EXPERT, BM_CHAIN)]
        xs = [x_ref[rs, :] for rs in parts]
        gs = [_bdot(x, wg_sc[...]) for x in xs]
        us = [_bdot(x, wu_sc[...]) for x in xs]
        acts = [(_silu(g) * u).astype(BF16) for g, u in zip(gs, us)]
        for rs, act in zip(parts, acts):
            o_ref[rs, :] = _bdot(act, wd_sc[...]).astype(BF16)


def _experts(meta, xs, layer, w_gate, w_up, w_down):
    r, d = xs.shape
    ed = w_gate.shape[-1]
    bm = BM_EXPERT
    omap = lambda b, bx, *_: (b, 0)
    xmap = lambda b, bx, *_: (bx[b], 0)
    hbm = pl.BlockSpec(memory_space=pl.ANY)
    return pl.pallas_call(
        functools.partial(_expert_kernel, layer=layer),
        grid_spec=pltpu.PrefetchScalarGridSpec(
            num_scalar_prefetch=5, grid=(r // bm,),
            in_specs=[pl.BlockSpec((bm, d), xmap), hbm, hbm, hbm],
            out_specs=pl.BlockSpec((bm, d), omap),
            scratch_shapes=[pltpu.VMEM((d, ed), F32), pltpu.VMEM((d, ed), F32), pltpu.VMEM((ed, d), F32),
                            pltpu.VMEM((d, ed), BF16), pltpu.VMEM((d, ed), BF16), pltpu.VMEM((ed, d), BF16),
                            pltpu.SemaphoreType.DMA((3,))]),
        out_shape=jax.ShapeDtypeStruct((r, d), BF16),
        compiler_params=_cparams(("arbitrary",)),
    )(*meta, xs, w_gate, w_up, w_down)


def _combine_kernel(pe_ref, po_ref, goff_ref, used_ref, key_ref, wkey_ref, ys_ref, base_ref, g_ref,
                    fn_ref, o_ref, ybuf, q_sc, sem, *, final):
    i = pl.program_id(0)
    tm = base_ref.shape[0]
    ch = SORT_CHUNK
    ppc = ch // RUN_ALIGN

    @pl.when(i == 0)
    def _():
        ybuf[...] = jnp.zeros_like(ybuf)

    copy, wait_all = _piece_copies(pe_ref, po_ref, goff_ref, used_ref, i, ybuf, ys_ref, sem, False)

    for ci in range(ybuf.shape[0] // ch):
        cs = slice(ci * ch, (ci + 1) * ch)

        @pl.when(ci * ch < used_ref[i])
        def _():
            for jj in range(ppc):
                copy(ci * ppc + jj).start()
            rows = _piece_rows(pe_ref, po_ref, key_ref, i * PIECES_PAD + ci * ppc, ppc, wkey_ref)
            q_sc[:, cs] = jnp.concatenate(rows, axis=0).T.astype(BF16)

        @pl.when(ci * ch >= used_ref[i])
        def _():
            q_sc[:, cs] = jnp.zeros((tm, ch), BF16)

    wait_all()
    out = base_ref[...] + g_ref[0] * _bdot(q_sc[...], ybuf[...])
    if final:
        out = _rms(out, fn_ref[...])
    o_ref[...] = out


def _combine(meta, pe, po, key, wkey, ys, base, gt, fnorm, seq, final):
    t, d = base.shape
    tm = TM_SORT
    nt = seq // tm
    return pl.pallas_call(
        functools.partial(_combine_kernel, final=final),
        grid_spec=pltpu.PrefetchScalarGridSpec(
            num_scalar_prefetch=4, grid=(t // tm,),
            in_specs=[pl.BlockSpec((N_EXPERTS, tm), lambda i, *_: (0, i)),
                      pl.BlockSpec((N_EXPERTS, tm), lambda i, *_: (0, i)),
                      pl.BlockSpec(memory_space=pl.ANY),
                      pl.BlockSpec((tm, d), lambda i, *_: (i, 0)),
                      pl.BlockSpec((1, 1, d), lambda i, *_: (i // nt, 0, 0)),
                      pl.BlockSpec((1, d), lambda i, *_: (0, 0))],
            out_specs=pl.BlockSpec((tm, d), lambda i, *_: (i, 0)),
            scratch_shapes=[pltpu.VMEM((ROWS_TILE, d), BF16), pltpu.VMEM((tm, ROWS_TILE), BF16),
                            pltpu.SemaphoreType.DMA(())]),
        out_shape=jax.ShapeDtypeStruct((t, d), F32),
        compiler_params=_cparams(("arbitrary",)),
    )(pe, po, meta[0], meta[1], key, wkey, ys, base, gt, fnorm.reshape(1, d))


def _moe(x, mods, nw, w_router, e_bias, layer, w_gate, w_up, w_down, wsg, wsu, wsd, fnorm, final, mixer):
    bsz, s, d = x.shape
    t = bsz * s
    gt = mods[2]
    n_rows_max = _rows_max(t)
    hb, base, key, wkey, cnt, pe, po = _router(x, mods, nw, w_router, e_bias, wsg, wsu, wsd, mixer)
    pe = pe.reshape(-1)
    po = po.reshape(-1)
    layout_meta, block_meta = _sort_meta(cnt, n_rows_max)
    xs = _dispatch(layout_meta, pe, po, key, hb, n_rows_max)
    ys = _experts(block_meta, xs, layer, w_gate, w_up, w_down)
    out = _combine(layout_meta, pe, po, key, wkey, ys, base, gt, fnorm, s, final)
    return out.reshape(bsz, s, d)


def _kv_kernel(x_ref, sh_ref, sc_ref, nw_ref, wk_ref, wvt_ref, k_ref, vt_ref, km_ref):
    x = x_ref[0]
    hb = (_rms(x, nw_ref[...]) * (1.0 + sc_ref[0]) + sh_ref[0]).astype(BF16)
    k = _bdot(hb, wk_ref[...])
    vt = _bdot_nt(wvt_ref[...], hb)
    for p in range(k.shape[1] // LANES):
        k_ref[0, p, 0] = k[:, p * LANES:(p + 1) * LANES].astype(BF16)
        vt_ref[0, p, 0] = vt[p * LANES:(p + 1) * LANES, :].astype(BF16)
    km_ref[0, 0] = jnp.mean(k, axis=0, keepdims=True)


def _kv(x, sh, sc, nw, w_k, w_v):
    bsz, s, d = x.shape
    nb = s // B_BLOCK
    npair = d // LANES
    vec = pl.BlockSpec((1, 1, d), lambda b, j: (b, 0, 0))
    full2 = lambda shape: pl.BlockSpec(shape, lambda b, j: (0, 0))
    return pl.pallas_call(
        _kv_kernel,
        grid=(bsz, nb),
        in_specs=[pl.BlockSpec((1, B_BLOCK, d), lambda b, j: (b, j, 0)), vec, vec,
                  full2((1, d)), full2((d, d)), full2((d, d))],
        out_specs=[pl.BlockSpec((1, npair, 1, B_BLOCK, LANES), lambda b, j: (b, 0, j, 0, 0)),
                   pl.BlockSpec((1, npair, 1, LANES, B_BLOCK), lambda b, j: (b, 0, j, 0, 0)),
                   pl.BlockSpec((1, 1, 1, d), lambda b, j: (b, j, 0, 0))],
        out_shape=[jax.ShapeDtypeStruct((bsz, npair, nb, B_BLOCK, LANES), BF16),
                   jax.ShapeDtypeStruct((bsz, npair, nb, LANES, B_BLOCK), BF16),
                   jax.ShapeDtypeStruct((bsz, nb, 1, d), F32)],
        compiler_params=_cparams(("arbitrary", "arbitrary")),
    )(x, sh, sc, nw.reshape(1, d), w_k.astype(BF16), w_v.T.astype(BF16))


def _attn_body(x, qb, sh_ref, sc_ref, g_ref, nw_ref, wqt_ref, wo_ref, k_ref, vt_ref, km_ref,
               qt_sc, qs_sc, acc_sc, sel_sc, m_sc, l_sc, *, nb, n_sel):
    bq = x.shape[0]
    npair = qt_sc.shape[0]
    nbp = km_ref.shape[2]
    hd = LANES // 2
    scale = float(hd) ** -0.5 * 1.4426950408889634
    h = _rms(x, nw_ref[...]) * (1.0 + sc_ref[0]) + sh_ref[0]
    qt = _bdot_nt(wqt_ref[...], h.astype(BF16))
    for p in range(npair):
        qt_sc[p] = qt[p * LANES:(p + 1) * LANES, :]

    subn = lax.broadcasted_iota(I32, (nbp, bq), 0)
    past = subn < qb
    krow = lax.broadcasted_iota(I32, (B_BLOCK, bq), 0)
    qcol = lax.broadcasted_iota(I32, (B_BLOCK, bq), 1)
    causal = krow <= qcol
    rowh = lax.broadcasted_iota(I32, (LANES, 1), 0)

    grp = ATTN_PAIR_UNROLL
    heads = [(u, e) for u in range(grp) for e in range(2)]
    ones_rows = jnp.ones((SUM_ROWS, B_BLOCK), BF16)

    def vsum(vt2, e):
        return jnp.concatenate([vt2[e * hd:(e + 1) * hd, :], ones_rows], axis=0)

    def own_body(gi, carry):
        ps = [gi * grp + u for u in range(grp)]
        q2ts = [qt_sc[p] for p in ps]
        kms = [km_ref[0, p] for p in ps]
        kown = [k_ref[0, p, qb] for p in ps]
        vown = [vt_ref[0, p, qb] for p in ps]
        qets = [jnp.where((rowh >= hd) if e == 1 else (rowh < hd), q2ts[u], 0.0) for u, e in heads]
        qsts = [(q * scale).astype(BF16) for q in qets]
        ss = [jnp.where(causal, _bdot(kown[u], qsts[i]), NEG_INF) for i, (u, e) in enumerate(heads)]
        gates = [_dot3(kms[u], qets[i]) for i, (u, e) in enumerate(heads)]
        ms = [jnp.max(s, axis=0, keepdims=True) for s in ss]
        pes = [jnp.exp2(s - m) for s, m in zip(ss, ms)]
        pvs = [_bdot(vsum(vown[u], e), pes[i].astype(BF16)) for i, (u, e) in enumerate(heads)]
        accs = [pv[:hd] for pv in pvs]
        ls = [pv[hd:hd + 1] for pv in pvs]
        sels = []
        for gate in gates:
            selt = jnp.zeros((nbp, bq), F32)
            for n in range(nb):
                gn = gate[n:n + 1, :]
                beats = jnp.logical_or(gate > gn, jnp.logical_and(gate == gn, subn < n))
                beats = jnp.logical_and(beats, past)
                cnt = jnp.sum(beats.astype(F32), axis=0, keepdims=True)
                selt = jnp.where(subn == n, (cnt < n_sel).astype(F32), selt)
            sels.append(selt)
        for i, (u, e) in enumerate(heads):
            p = ps[u]
            sel_sc[p, e] = sels[i]
            qs_sc[p, e] = qsts[i]
            m_sc[p, e] = ms[i]
            l_sc[p, e] = ls[i]
            acc_sc[p, e * hd:(e + 1) * hd, :] = accs[i]
        return carry

    lax.fori_loop(0, npair // grp, own_body, 0)

    grp = ATTN_PAIR_UNROLL_PAST
    heads = [(u, e) for u in range(grp) for e in range(2)]

    def kb_body(kb, carry):
        def group_body(gi, c2):
            ps = [gi * grp + u for u in range(grp)]
            kbl = [k_ref[0, p, kb] for p in ps]
            vbl = [vt_ref[0, p, kb] for p in ps]
            qsts = [qs_sc[ps[u], e] for u, e in heads]
            rows = [sel_sc[ps[u], e, pl.ds(kb, 1), :] for u, e in heads]
            m_old = [m_sc[ps[u], e] for u, e in heads]
            l_old = [l_sc[ps[u], e] for u, e in heads]
            a_old = [acc_sc[ps[u], e * hd:(e + 1) * hd, :] for u, e in heads]
            ss = [jnp.where(rows[i] > 0.5, _bdot(kbl[u], qsts[i]), NEG_INF)
                  for i, (u, e) in enumerate(heads)]
            m_new = [jnp.maximum(m, jnp.max(s, axis=0, keepdims=True)) for m, s in zip(m_old, ss)]
            alphas = [jnp.exp2(m - mn) for m, mn in zip(m_old, m_new)]
            pes = [jnp.exp2(s - mn) for s, mn in zip(ss, m_new)]
            pvs = [_bdot(vsum(vbl[u], e), pes[i].astype(BF16)) for i, (u, e) in enumerate(heads)]
            l_new = [a * l + pv[hd:hd + 1] for a, l, pv in zip(alphas, l_old, pvs)]
            a_new = [a * ao + pv[:hd] for a, ao, pv in zip(alphas, a_old, pvs)]
            for i, (u, e) in enumerate(heads):
                p = ps[u]
                m_sc[p, e] = m_new[i]
                l_sc[p, e] = l_new[i]
                acc_sc[p, e * hd:(e + 1) * hd, :] = a_new[i]
            return c2

        lax.fori_loop(0, npair // grp, group_body, 0)
        return carry

    lax.fori_loop(0, qb, kb_body, 0)

    parts = []
    for p in range(npair):
        for e in range(2):
            parts.append(acc_sc[p, e * hd:(e + 1) * hd, :] / l_sc[p, e])
    ot = jnp.concatenate(parts, axis=0)
    return x + g_ref[0] * _bdot(ot.T.astype(BF16), wo_ref[...])


def _attn_mixer(mods, nw, w_q, w_o, k5, vt5, km2):
    sh, sc, gt = mods
    d = w_q.shape[0]
    npair, nb = k5.shape[1], k5.shape[2]
    nbp = km2.shape[2]
    n_sel = min(B_TOPK, nb - 1)
    per = TM_SORT // B_BLOCK
    vec = pl.BlockSpec((1, 1, d), lambda b, j: (b, 0, 0))
    full2 = lambda shape: pl.BlockSpec(shape, lambda b, j: (0, 0))
    specs = [vec, vec, vec, full2((1, d)), full2((d, d)), full2((d, d)),
             pl.BlockSpec((1, npair, nb, B_BLOCK, LANES), lambda b, j: (b, 0, 0, 0, 0)),
             pl.BlockSpec((1, npair, nb, LANES, B_BLOCK), lambda b, j: (b, 0, 0, 0, 0)),
             pl.BlockSpec((1, npair, nbp, LANES), lambda b, j: (b, 0, 0, 0))]
    args = [sh, sc, gt, nw.reshape(1, d), w_q.T.astype(BF16), w_o.astype(BF16), k5, vt5, km2]
    scratch = [pltpu.VMEM((npair, LANES, B_BLOCK), F32),
               pltpu.VMEM((npair, 2, LANES, B_BLOCK), BF16),
               pltpu.VMEM((npair, LANES, B_BLOCK), F32),
               pltpu.VMEM((npair, 2, nbp, B_BLOCK), F32),
               pltpu.VMEM((npair, 2, 1, B_BLOCK), F32),
               pltpu.VMEM((npair, 2, 1, B_BLOCK), F32)]

    def body(x_ref, refs, scr):
        j = pl.program_id(1)
        outs = [_attn_body(x_ref[0, h * B_BLOCK:(h + 1) * B_BLOCK, :], j * per + h, *refs, *scr,
                           nb=nb, n_sel=n_sel) for h in range(per)]
        return jnp.concatenate(outs, axis=0)

    return body, specs, args, scratch


def kernel(x, c, ada_w, ada_b, norm_mix, norm_ffn, a_w_in, a_b_in, a_ln_g, a_ln_b, a_w_s, a_b_s,
           a_w_out, kv_norm, kv_ada_w, kv_ada_b, kv_w_k, kv_w_v, b_w_q, b_w_o, moe_router, moe_bias,
           moe_w_gate, moe_w_up, moe_w_down, sh_w_gate, sh_w_up, sh_w_down, final_norm):
    bsz, s, d = x.shape
    depth = ada_w.shape[0]
    n_a = a_w_in.shape[0]
    assert s % B_BLOCK == 0 and s % TM_SORT == 0 and TM_SORT % A_CHUNK == 0 and d % LANES == 0
    nb = s // B_BLOCK
    npair = d // LANES
    nbp = -(-nb // 8) * 8

    def split(m, n):
        return [m[:, i * d:(i + 1) * d].reshape(bsz, 1, d) for i in range(n)]

    layer_mods = _ada(c, ada_w, ada_b)
    k5 = vt5 = km2 = None
    for i in range(depth):
        sh1, sc1, g1, sh2, sc2, g2 = split(layer_mods[i], 6)
        if i < n_a:
            mixer = _gmlp_mixer((sh1, sc1, g1), norm_mix[i], a_w_in[i], a_b_in[i], a_ln_g[i], a_ln_b[i],
                                a_w_s[i], a_b_s[i], a_w_out[i])
        else:
            if k5 is None:
                ksh, ksc = split(_ada(c, kv_ada_w[None], kv_ada_b[None])[0], 2)
                k5, vt5, km = _kv(x, ksh, ksc, kv_norm, kv_w_k, kv_w_v)
                km = km.reshape(bsz, nb, npair, LANES).transpose(0, 2, 1, 3)
                km2 = jnp.pad(km, ((0, 0), (0, 0), (0, nbp - nb), (0, 0)))
            j = i - n_a
            mixer = _attn_mixer((sh1, sc1, g1), norm_mix[i], b_w_q[j], b_w_o[j], k5, vt5, km2)
        x = _moe(x, (sh2, sc2, g2), norm_ffn[i], moe_router[i], moe_bias[i], i, moe_w_gate,
                 moe_w_up, moe_w_down, sh_w_gate[i], sh_w_up[i], sh_w_down[i],
                 final_norm, i == depth - 1, mixer)
    return x
```

```python
import functools

import jax
import jax.numpy as jnp
from jax import lax
from jax.experimental import pallas as pl
from jax.experimental.pallas import tpu as pltpu

F32 = jnp.float32
BF16 = jnp.bfloat16
I32 = jnp.int32

RMS_EPS = 1e-6
LN_EPS = 1e-5
NEG_INF = -1e30

A_CHUNK = 128
A_GROUPS = 8
B_HEADS = 16
B_BLOCK = 256
B_TOPK = 3
N_EXPERTS = 64
TOP_K = 8
N_GROUPS = 8
TOPK_GROUPS = 4
ROUTED_SCALE = 2.5

LANES = 128
VMEM_LIMIT = 56 * 1024 * 1024

TM_SORT = 512
RUN_ALIGN = 16
SORT_CHUNK = 256
ROWS_TILE_USED = TM_SORT * TOP_K + N_EXPERTS * (RUN_ALIGN - 1)
ROWS_TILE = -(-ROWS_TILE_USED // SORT_CHUNK) * SORT_CHUNK
PIECES_TILE = ROWS_TILE // RUN_ALIGN
PIECES_PAD = -(-PIECES_TILE // LANES) * LANES
BM_EXPERT = 1024
BM_CHAIN = 256
ATTN_PAIR_UNROLL = 4
ATTN_PAIR_UNROLL_PAST = 8
SUM_ROWS = 16


def _cparams(sem):
    return pltpu.CompilerParams(dimension_semantics=sem, vmem_limit_bytes=VMEM_LIMIT)


def _sigmoid(x):
    return 1.0 / (1.0 + jnp.exp(-x))


def _silu(x):
    return x * _sigmoid(x)


def _gelu_tanh(x):
    hx = 0.5 * x
    return hx + hx * jnp.tanh(x * (0.7978845608028654 + 0.035677408136300125 * (x * x)))


def _rms(x, g):
    return x * lax.rsqrt(jnp.mean(x * x, axis=-1, keepdims=True) + RMS_EPS) * g


def _bdot(a, b):
    return jnp.dot(a, b, preferred_element_type=F32)


def _bdot_nt(a, b):
    return lax.dot_general(a, b, (((1,), (1,)), ((), ())), preferred_element_type=F32)


def _split(a):
    hi = a.astype(BF16)
    lo = (a - hi.astype(F32)).astype(BF16)
    return hi, lo


def _dot3_nt(a, b):
    ah, al = _split(a)
    bh, bl = _split(b)
    return _bdot_nt(ah, bh) + (_bdot_nt(ah, bl) + _bdot_nt(al, bh))


def _dot3(a, b):
    ah, al = _split(a)
    bh, bl = _split(b)
    return _bdot(ah, bh) + (_bdot(ah, bl) + _bdot(al, bh))


def _ada_kernel(c_ref, w_ref, b_ref, o_ref):
    a = _silu(c_ref[...]).astype(BF16)
    o_ref[0] = _bdot(a, w_ref[0].astype(BF16)) + b_ref[0]


def _ada(c, w, b):
    bsz, d = c.shape
    nl, _, n = w.shape
    tn = 1024
    return pl.pallas_call(
        _ada_kernel,
        grid=(nl, n // tn),
        in_specs=[pl.BlockSpec((bsz, d), lambda l, j: (0, 0)),
                  pl.BlockSpec((1, d, tn), lambda l, j: (l, 0, j)),
                  pl.BlockSpec((1, 1, tn), lambda l, j: (l, 0, j))],
        out_specs=pl.BlockSpec((1, bsz, tn), lambda l, j: (l, 0, j)),
        out_shape=jax.ShapeDtypeStruct((nl, bsz, n), F32),
        compiler_params=_cparams(("arbitrary", "arbitrary")),
    )(c, w, b.reshape(nl, 1, n))


def _gmlp_body(x, sh_ref, sc_ref, g_ref, nw_ref, win_ref, bin_ref, lng_ref, lnb_ref,
               ws_ref, bst_ref, wout_ref, y_sc):
    tm = x.shape[0]
    h = _rms(x, nw_ref[...]) * (1.0 + sc_ref[0]) + sh_ref[0]
    z = _gelu_tanh(_bdot(h.astype(BF16), win_ref[...]) + bin_ref[...])
    aw = z.shape[1] // 2
    gd = aw // A_GROUPS
    u = z[:, :aw]
    v = z[:, aw:]
    mu = jnp.mean(v, axis=-1, keepdims=True)
    dv = v - mu
    var = jnp.mean(dv * dv, axis=-1, keepdims=True)
    vn = (dv * lax.rsqrt(var + LN_EPS) * lng_ref[...] + lnb_ref[...]).astype(BF16)
    row = lax.broadcasted_iota(I32, (A_CHUNK, A_CHUNK), 0)
    col = lax.broadcasted_iota(I32, (A_CHUNK, A_CHUNK), 1)
    causal = col <= row
    for g in range(A_GROUPS):
        wg = jnp.where(causal, ws_ref[g], 0.0).astype(BF16)
        bcol = bst_ref[:, g:g + 1]
        for ci in range(tm // A_CHUNK):
            rs = slice(ci * A_CHUNK, (ci + 1) * A_CHUNK)
            cs = slice(g * gd, (g + 1) * gd)
            sv = _bdot(wg, vn[rs, cs]) + bcol
            y_sc[rs, cs] = (u[rs, cs] * sv).astype(BF16)
    return x + g_ref[0] * _bdot(y_sc[...], wout_ref[...])


def _route(x, sh_ref, sc_ref, g_ref, nw_ref, wrt_ref, bias_ref, wsg_ref, wsu_ref, wsd_ref,
           h_ref, base_ref, key_ref, wkey_ref, cnt_ref, pe_ref, po_ref):
    tm = x.shape[0]
    h = _rms(x, nw_ref[...]) * (1.0 + sc_ref[0]) + sh_ref[0]
    hb = h.astype(BF16)
    h_ref[...] = hb
    act = (_silu(_bdot(hb, wsg_ref[...])) * _bdot(hb, wsu_ref[...])).astype(BF16)
    base_ref[...] = x + g_ref[0] * _bdot(act, wsd_ref[...])

    scores = _sigmoid(_dot3_nt(wrt_ref[...], h))
    choice = scores + bias_ref[...]
    gsz = N_EXPERTS // N_GROUPS
    sub = lax.broadcasted_iota(I32, (gsz, tm), 0)
    blocks = [choice[g * gsz:(g + 1) * gsz] for g in range(N_GROUPS)]
    gscore = []
    for blk in blocks:
        m1 = jnp.max(blk, axis=0, keepdims=True)
        i1 = jnp.min(jnp.where(blk == m1, sub, gsz), axis=0, keepdims=True)
        m2 = jnp.max(jnp.where(sub == i1, -jnp.inf, blk), axis=0, keepdims=True)
        gscore.append(m1 + m2)
    masked = []
    for g in range(N_GROUPS):
        beats = jnp.zeros((1, tm), F32)
        for m in range(N_GROUPS):
            if m == g:
                continue
            b = gscore[m] > gscore[g]
            if m < g:
                b = jnp.logical_or(b, gscore[m] == gscore[g])
            beats = beats + b.astype(F32)
        masked.append(jnp.where(beats < TOPK_GROUPS, blocks[g], NEG_INF))
    cur = jnp.concatenate(masked, axis=0)
    eio = lax.broadcasted_iota(I32, (N_EXPERTS, tm), 0)
    chosen = jnp.zeros((N_EXPERTS, tm), jnp.bool_)
    wsum = jnp.zeros((1, tm), F32)
    for _ in range(TOP_K):
        m = jnp.max(cur, axis=0, keepdims=True)
        idx = jnp.min(jnp.where(cur == m, eio, N_EXPERTS), axis=0, keepdims=True)
        sel = eio == idx
        chosen = jnp.logical_or(chosen, sel)
        wsum = wsum + jnp.sum(jnp.where(sel, scores, 0.0), axis=0, keepdims=True)
        cur = jnp.where(sel, -jnp.inf, cur)
    wkey_ref[...] = jnp.where(chosen, scores / wsum * ROUTED_SCALE, 0.0)

    onehot = chosen.astype(BF16)
    r_i = lax.broadcasted_iota(I32, (tm, tm), 0)
    c_i = lax.broadcasted_iota(I32, (tm, tm), 1)
    before = (r_i < c_i).astype(BF16)
    prior = _bdot(onehot, before)
    key_ref[...] = jnp.where(chosen, prior, -1.0).astype(I32)
    cnt = jnp.sum(chosen.astype(F32), axis=1, keepdims=True)
    cnt_ref[0] = cnt

    run_p = jnp.floor((cnt + (RUN_ALIGN - 1)) / RUN_ALIGN)
    ppc = SORT_CHUNK // RUN_ALIGN
    tot = jnp.sum(run_p, axis=0, keepdims=True)
    fill = jnp.ceil(tot / ppc) * ppc - tot
    run_p = run_p + jnp.where(lax.broadcasted_iota(I32, (N_EXPERTS, 1), 0) == N_EXPERTS - 1, fill, 0.0)
    e_r = lax.broadcasted_iota(I32, (N_EXPERTS, N_EXPERTS), 0)
    e_c = lax.broadcasted_iota(I32, (N_EXPERTS, N_EXPERTS), 1)
    incl = (e_c <= e_r).astype(BF16)
    lend = _bdot(incl, jnp.broadcast_to(run_p, (N_EXPERTS, LANES)).astype(BF16))[:, 0:1]
    loff = lend - run_p
    pj = lax.broadcasted_iota(I32, (N_EXPERTS, PIECES_PAD), 1).astype(F32)
    er = lax.broadcasted_iota(I32, (N_EXPERTS, PIECES_PAD), 0).astype(F32)
    pe = jnp.minimum(jnp.sum((lend <= pj).astype(F32), axis=0, keepdims=True), N_EXPERTS - 1.0)
    lo = jnp.sum(jnp.where(er == pe, loff, 0.0), axis=0, keepdims=True)
    pe_ref[0] = pe.astype(I32)
    po_ref[0] = ((pj[0:1, :] - lo) * RUN_ALIGN).astype(I32)


N_ROUTE_REFS = 16


def _router_kernel(x_ref, *refs):
    _route(x_ref[0], *refs)


def _mixer_router_kernel(x_ref, *refs, body, n_mix):
    route_refs = refs[n_mix:n_mix + N_ROUTE_REFS]
    x1 = body(x_ref, refs[:n_mix], refs[n_mix + N_ROUTE_REFS:])
    _route(x1, *route_refs)


def _gmlp_mixer(mods, nw, w_in, b_in, ln_g, ln_b, w_s, b_s, w_out):
    sh, sc, gt = mods
    d, n_in = w_in.shape
    aw = n_in // 2
    vec = pl.BlockSpec((1, 1, d), lambda b, j: (b, 0, 0))
    full2 = lambda shape: pl.BlockSpec(shape, lambda b, j: (0, 0))
    specs = [vec, vec, vec, full2((1, d)), full2((d, n_in)), full2((1, n_in)), full2((1, aw)),
             full2((1, aw)), pl.BlockSpec((A_GROUPS, A_CHUNK, A_CHUNK), lambda b, j: (0, 0, 0)),
             full2((A_CHUNK, A_GROUPS)), full2((aw, d))]
    args = [sh, sc, gt, nw.reshape(1, d), w_in.astype(BF16), b_in.reshape(1, n_in),
            ln_g.reshape(1, aw), ln_b.reshape(1, aw), w_s, b_s.T, w_out.astype(BF16)]

    def body(x_ref, refs, scratch):
        return _gmlp_body(x_ref[0], *refs, *scratch)

    return body, specs, args, [pltpu.VMEM((TM_SORT, aw), BF16)]


def _router(x, mods, nw, w_router, e_bias, wsg, wsu, wsd, mixer=None):
    bsz, s, d = x.shape
    t = bsz * s
    tm = TM_SORT
    nt = s // tm
    sd = wsg.shape[1]
    vec = pl.BlockSpec((1, 1, d), lambda b, j: (b, 0, 0))
    full2 = lambda shape: pl.BlockSpec(shape, lambda b, j: (0, 0))
    tok = pl.BlockSpec((tm, d), lambda b, j: (b * nt + j, 0))
    etok = pl.BlockSpec((N_EXPERTS, tm), lambda b, j: (0, b * nt + j))
    ptab = pl.BlockSpec((1, 1, PIECES_PAD), lambda b, j: (b * nt + j, 0, 0))
    sh, sc, gt = mods
    in_specs = [vec, vec, vec, full2((1, d)), full2((N_EXPERTS, d)), full2((N_EXPERTS, 1)),
                full2((d, sd)), full2((d, sd)), full2((sd, d))]
    args = [sh, sc, gt, nw.reshape(1, d), w_router.T, e_bias.reshape(N_EXPERTS, 1),
            wsg.astype(BF16), wsu.astype(BF16), wsd.astype(BF16)]
    kern, scratch = _router_kernel, []
    if mixer is not None:
        body, mspecs, margs, scratch = mixer
        kern = functools.partial(_mixer_router_kernel, body=body, n_mix=len(margs))
        in_specs = mspecs + in_specs
        args = margs + args
    return pl.pallas_call(
        kern,
        grid=(bsz, nt),
        in_specs=[pl.BlockSpec((1, tm, d), lambda b, j: (b, j, 0))] + in_specs,
        out_specs=[tok, tok, etok, etok,
                   pl.BlockSpec((1, N_EXPERTS, 1), lambda b, j: (b * nt + j, 0, 0)), ptab, ptab],
        out_shape=[jax.ShapeDtypeStruct((t, d), BF16), jax.ShapeDtypeStruct((t, d), F32),
                   jax.ShapeDtypeStruct((N_EXPERTS, t), I32), jax.ShapeDtypeStruct((N_EXPERTS, t), F32),
                   jax.ShapeDtypeStruct((t // tm, N_EXPERTS, 1), F32),
                   jax.ShapeDtypeStruct((t // tm, 1, PIECES_PAD), I32),
                   jax.ShapeDtypeStruct((t // tm, 1, PIECES_PAD), I32)],
        scratch_shapes=scratch,
        compiler_params=_cparams(("arbitrary", "arbitrary")),
    )(x, *args)


def _rows_max(t):
    rows = (t // TM_SORT) * ROWS_TILE + N_EXPERTS * (BM_EXPERT - RUN_ALIGN)
    return -(-rows // BM_EXPERT) * BM_EXPERT


def _sort_meta(cnt, n_rows_max):
    nts = cnt.shape[0]
    c = cnt.reshape(nts, N_EXPERTS).astype(I32)
    run = (c + (RUN_ALIGN - 1)) // RUN_ALIGN * RUN_ALIGN
    fill = (-jnp.sum(run, axis=1, keepdims=True)) % SORT_CHUNK
    run = jnp.concatenate([run[:, :-1], run[:, -1:] + fill], axis=1)
    used = jnp.sum(run, axis=1)
    per_e = jnp.sum(run, axis=0)
    seg = (per_e + (BM_EXPERT - 1)) // BM_EXPERT * BM_EXPERT
    ends = jnp.cumsum(seg)
    starts = ends - seg
    goff = starts[None, :] + jnp.cumsum(run, axis=0) - run
    pad_start = starts + per_e
    pad_cnt = (seg - per_e) // RUN_ALIGN

    nb = n_rows_max // BM_EXPERT
    blo = jnp.arange(nb, dtype=I32) * BM_EXPERT
    be = jnp.sum((ends[None, :] <= blo[:, None]).astype(I32), axis=1)
    real = be < N_EXPERTS
    be = jnp.minimum(be, N_EXPERTS - 1).astype(I32)
    prev = jnp.concatenate([jnp.full((1,), -1, I32), be[:-1]])
    bnew = jnp.logical_and(real, be != prev)
    bidx = jnp.arange(nb, dtype=I32)
    bx = jnp.where(real, bidx, 0)
    later = jnp.logical_and(bidx[None, :] > bidx[:, None], jnp.logical_and(real[None, :], be[None, :] != be[:, None]))
    nxt = jnp.min(jnp.where(later, bidx[None, :], nb), axis=1)
    bfetch = jnp.where(nxt < nb, be[jnp.minimum(nxt, nb - 1)], -1)
    i32 = lambda a: a.reshape(-1).astype(I32)
    return (i32(goff), i32(used), i32(pad_start), i32(pad_cnt)), (bx, be, i32(bnew), i32(real), i32(bfetch))


def _piece_copies(pe_ref, po_ref, goff_ref, used_ref, tile, local, remote, sem, to_remote):
    def copy(j):
        idx = tile * PIECES_PAD + j
        l0 = pl.multiple_of(j * RUN_ALIGN, RUN_ALIGN)
        g0 = pl.multiple_of(goff_ref[tile * N_EXPERTS + pe_ref[idx]] + po_ref[idx], RUN_ALIGN)
        lref = local.at[pl.ds(l0, RUN_ALIGN)]
        gref = remote.at[pl.ds(g0, RUN_ALIGN)]
        return pltpu.make_async_copy(lref, gref, sem) if to_remote else pltpu.make_async_copy(gref, lref, sem)

    def wait_all():
        def body(j, carry):
            copy(j).wait()
            return carry

        lax.fori_loop(0, used_ref[tile] // RUN_ALIGN, body, 0)

    return copy, wait_all


def _pad_copies(pad_start_ref, pad_cnt_ref, tile, ntiles, zeros, remote, sem):
    share = -(-N_EXPERTS // ntiles)

    def apply(act):
        for q in range(share):
            e = tile * share + q
            ec = jnp.minimum(e, N_EXPERTS - 1)
            n = jnp.where(e < N_EXPERTS, pad_cnt_ref[ec], 0)
            base = pad_start_ref[ec]

            def body(j, carry):
                g0 = pl.multiple_of(base + j * RUN_ALIGN, RUN_ALIGN)
                getattr(pltpu.make_async_copy(zeros, remote.at[pl.ds(g0, RUN_ALIGN)], sem), act)()
                return carry

            lax.fori_loop(0, n, body, 0)

    return apply


def _piece_rows(pe_ref, po_ref, key_ref, first_piece, npieces, val_ref=None):
    tm = key_ref.shape[1]
    sub = lax.broadcasted_iota(I32, (RUN_ALIGN, tm), 0)
    out = []
    for jj in range(npieces):
        j = first_piece + jj
        e = pe_ref[j]
        hit = (key_ref[pl.ds(e, 1), :] - po_ref[j]) == sub
        val = 1.0 if val_ref is None else val_ref[pl.ds(e, 1), :]
        out.append(jnp.where(hit, val, 0.0))
    return out


def _dispatch_kernel(pe_ref, po_ref, goff_ref, used_ref, pad_start_ref, pad_cnt_ref,
                     key_ref, h_ref, xs_ref, xbuf, p_sc, zbuf, sem, zsem, *, ntiles):
    i = pl.program_id(0)
    hb = h_ref[...]
    ch = SORT_CHUNK
    ppc = ch // RUN_ALIGN
    zbuf[...] = jnp.zeros_like(zbuf)
    pads = _pad_copies(pad_start_ref, pad_cnt_ref, i, ntiles, zbuf, xs_ref, zsem)
    pads("start")

    copy, wait_all = _piece_copies(pe_ref, po_ref, goff_ref, used_ref, i, xbuf, xs_ref, sem, True)

    def build(ci):
        rows = _piece_rows(pe_ref, po_ref, key_ref, i * PIECES_PAD + ci * ppc, ppc)
        for jj, p in enumerate(rows):
            p_sc[jj * RUN_ALIGN:(jj + 1) * RUN_ALIGN, :] = p.astype(BF16)
        xbuf[pl.ds(pl.multiple_of(ci * ch, ch), ch), :] = _bdot(p_sc[...], hb).astype(BF16)

    def send(ci):
        for jj in range(ppc):
            copy(ci * ppc + jj).start()

    def step(ci, carry):
        send(ci - 1)
        build(ci)
        return carry

    nchunk = used_ref[i] // ch
    build(0)
    lax.fori_loop(1, nchunk, step, 0)
    send(nchunk - 1)
    pads("wait")
    wait_all()


def _dispatch(meta, pe, po, key, hb, n_rows_max):
    t, d = hb.shape
    tm = TM_SORT
    goff, used, pad_start, pad_cnt = meta
    return pl.pallas_call(
        functools.partial(_dispatch_kernel, ntiles=t // tm),
        grid_spec=pltpu.PrefetchScalarGridSpec(
            num_scalar_prefetch=6, grid=(t // tm,),
            in_specs=[pl.BlockSpec((N_EXPERTS, tm), lambda i, *_: (0, i)),
                      pl.BlockSpec((tm, d), lambda i, *_: (i, 0))],
            out_specs=pl.BlockSpec(memory_space=pl.ANY),
            scratch_shapes=[pltpu.VMEM((ROWS_TILE, d), BF16), pltpu.VMEM((SORT_CHUNK, tm), BF16),
                            pltpu.VMEM((RUN_ALIGN, d), BF16),
                            pltpu.SemaphoreType.DMA(()), pltpu.SemaphoreType.DMA(())]),
        out_shape=jax.ShapeDtypeStruct((n_rows_max, d), BF16),
        compiler_params=_cparams(("arbitrary",)),
    )(pe, po, goff, used, pad_start, pad_cnt, key, hb)


def _expert_kernel(bx_ref, be_ref, bnew_ref, breal_ref, bfetch_ref, x_ref, wg_hbm, wu_hbm, wd_hbm, o_ref,
                   wg_st, wu_st, wd_st, wg_sc, wu_sc, wd_sc, sem, *, layer):
    b = pl.program_id(0)

    def fetch(e):
        return [pltpu.make_async_copy(src.at[layer, e], dst, sem.at[i])
                for i, (src, dst) in enumerate(((wg_hbm, wg_st), (wu_hbm, wu_st), (wd_hbm, wd_st)))]

    @pl.when(b == 0)
    def _():
        for cp in fetch(be_ref[0]):
            cp.start()

    @pl.when(bnew_ref[b] == 1)
    def _():
        for cp in fetch(be_ref[b]):
            cp.wait()
        wg_sc[...] = wg_st[...].astype(BF16)
        wu_sc[...] = wu_st[...].astype(BF16)
        wd_sc[...] = wd_st[...].astype(BF16)

        @pl.when(bfetch_ref[b] >= 0)
        def _():
            for cp in fetch(bfetch_ref[b]):
                cp.start()

    @pl.when(breal_ref[b] == 0)
    def _():
        o_ref[...] = jnp.zeros_like(o_ref)

    @pl.when(breal_ref[b] == 1)
    def _():
        parts = [slice(r, r + BM_CHAIN) for r in range(0, BM_EXPERT, BM_CHAIN)]
        xs = [x_ref[rs, :] for rs in parts]
        gs = [_bdot(x, wg_sc[...]) for x in xs]
        us = [_bdot(x, wu_sc[...]) for x in xs]
        acts = [(_silu(g) * u).astype(BF16) for g, u in zip(gs, us)]
        for rs, act in zip(parts, acts):
            o_ref[rs, :] = _bdot(act, wd_sc[...]).astype(BF16)


def _experts(meta, xs, layer, w_gate, w_up, w_down):
    r, d = xs.shape
    ed = w_gate.shape[-1]
    bm = BM_EXPERT
    omap = lambda b, bx, *_: (b, 0)
    xmap = lambda b, bx, *_: (bx[b], 0)
    hbm = pl.BlockSpec(memory_space=pl.ANY)
    return pl.pallas_call(
        functools.partial(_expert_kernel, layer=layer),
        grid_spec=pltpu.PrefetchScalarGridSpec(
            num_scalar_prefetch=5, grid=(r // bm,),
            in_specs=[pl.BlockSpec((bm, d), xmap), hbm, hbm, hbm],
            out_specs=pl.BlockSpec((bm, d), omap),
            scratch_shapes=[pltpu.VMEM((d, ed), F32), pltpu.VMEM((d, ed), F32), pltpu.VMEM((ed, d), F32),
                            pltpu.VMEM((d, ed), BF16), pltpu.VMEM((d, ed), BF16), pltpu.VMEM((ed, d), BF16),
                            pltpu.SemaphoreType.DMA((3,))]),
        out_shape=jax.ShapeDtypeStruct((r, d), BF16),
        compiler_params=_cparams(("arbitrary",)),
    )(*meta, xs, w_gate, w_up, w_down)


def _combine_kernel(pe_ref, po_ref, goff_ref, used_ref, key_ref, wkey_ref, ys_ref, base_ref, g_ref,
                    fn_ref, *rest, final, with_kv):
    if with_kv:
        kv_in, o_ref, kv_out, (ybuf, q_sc, sem) = rest[:5], rest[5], rest[6:9], rest[9:]
    else:
        o_ref, ybuf, q_sc, sem = rest
    i = pl.program_id(0)
    tm = base_ref.shape[0]
    ch = SORT_CHUNK
    ppc = ch // RUN_ALIGN

    @pl.when(i == 0)
    def _():
        ybuf[...] = jnp.zeros_like(ybuf)

    copy, wait_all = _piece_copies(pe_ref, po_ref, goff_ref, used_ref, i, ybuf, ys_ref, sem, False)

    for ci in range(ybuf.shape[0] // ch):
        cs = slice(ci * ch, (ci + 1) * ch)

        @pl.when(ci * ch < used_ref[i])
        def _():
            for jj in range(ppc):
                copy(ci * ppc + jj).start()
            rows = _piece_rows(pe_ref, po_ref, key_ref, i * PIECES_PAD + ci * ppc, ppc, wkey_ref)
            q_sc[:, cs] = jnp.concatenate(rows, axis=0).T.astype(BF16)

        @pl.when(ci * ch >= used_ref[i])
        def _():
            q_sc[:, cs] = jnp.zeros((tm, ch), BF16)

    wait_all()
    out = base_ref[...] + g_ref[0] * _bdot(q_sc[...], ybuf[...])
    if final:
        out = _rms(out, fn_ref[...])
    o_ref[...] = out
    if with_kv:
        _kv_body(out, *kv_in, *kv_out)


def _combine(meta, pe, po, key, wkey, ys, base, gt, fnorm, seq, final, kv=None):
    t, d = base.shape
    tm = TM_SORT
    nt = seq // tm
    bsz = t // seq
    vecb = pl.BlockSpec((1, 1, d), lambda i, *_: (i // nt, 0, 0))
    full2 = lambda shape: pl.BlockSpec(shape, lambda i, *_: (0, 0))
    in_specs = [pl.BlockSpec((N_EXPERTS, tm), lambda i, *_: (0, i)),
                pl.BlockSpec((N_EXPERTS, tm), lambda i, *_: (0, i)),
                pl.BlockSpec(memory_space=pl.ANY),
                pl.BlockSpec((tm, d), lambda i, *_: (i, 0)), vecb, full2((1, d))]
    args = [key, wkey, ys, base, gt, fnorm.reshape(1, d)]
    out_specs = [pl.BlockSpec((tm, d), lambda i, *_: (i, 0))]
    out_shape = [jax.ShapeDtypeStruct((t, d), F32)]
    if kv is not None:
        ksh, ksc, knw, w_k, w_v = kv
        nb, npair, per = seq // B_BLOCK, d // LANES, tm // B_BLOCK
        in_specs += [vecb, vecb, full2((1, d)), full2((d, d)), full2((d, d))]
        args += [ksh, ksc, knw.reshape(1, d), w_k.astype(BF16), w_v.T.astype(BF16)]
        out_specs += [pl.BlockSpec((1, npair, per, B_BLOCK, LANES), lambda i, *_: (i // nt, 0, i % nt, 0, 0)),
                      pl.BlockSpec((1, npair, per, LANES, B_BLOCK), lambda i, *_: (i // nt, 0, i % nt, 0, 0)),
                      pl.BlockSpec((1, per, 1, d), lambda i, *_: (i // nt, i % nt, 0, 0))]
        out_shape += [jax.ShapeDtypeStruct((bsz, npair, nb, B_BLOCK, LANES), BF16),
                      jax.ShapeDtypeStruct((bsz, npair, nb, LANES, B_BLOCK), BF16),
                      jax.ShapeDtypeStruct((bsz, nb, 1, d), F32)]
    res = pl.pallas_call(
        functools.partial(_combine_kernel, final=final, with_kv=kv is not None),
        grid_spec=pltpu.PrefetchScalarGridSpec(
            num_scalar_prefetch=4, grid=(t // tm,),
            in_specs=in_specs, out_specs=out_specs,
            scratch_shapes=[pltpu.VMEM((ROWS_TILE, d), BF16), pltpu.VMEM((tm, ROWS_TILE), BF16),
                            pltpu.SemaphoreType.DMA(())]),
        out_shape=out_shape,
        compiler_params=_cparams(("arbitrary",)),
    )(pe, po, meta[0], meta[1], *args)
    return res if kv is not None else res[0]


def _moe(x, mods, nw, w_router, e_bias, layer, w_gate, w_up, w_down, wsg, wsu, wsd, fnorm, final, mixer, kv=None):
    bsz, s, d = x.shape
    t = bsz * s
    gt = mods[2]
    n_rows_max = _rows_max(t)
    hb, base, key, wkey, cnt, pe, po = _router(x, mods, nw, w_router, e_bias, wsg, wsu, wsd, mixer)
    pe = pe.reshape(-1)
    po = po.reshape(-1)
    layout_meta, block_meta = _sort_meta(cnt, n_rows_max)
    xs = _dispatch(layout_meta, pe, po, key, hb, n_rows_max)
    ys = _experts(block_meta, xs, layer, w_gate, w_up, w_down)
    out = _combine(layout_meta, pe, po, key, wkey, ys, base, gt, fnorm, s, final, kv)
    if kv is not None:
        return (out[0].reshape(bsz, s, d),) + tuple(out[1:])
    return out.reshape(bsz, s, d)


def _kv_body(x, sh_ref, sc_ref, nw_ref, wk_ref, wvt_ref, k_ref, vt_ref, km_ref):
    hb = (_rms(x, nw_ref[...]) * (1.0 + sc_ref[0]) + sh_ref[0]).astype(BF16)
    k = _bdot(hb, wk_ref[...])
    vt = _bdot_nt(wvt_ref[...], hb)
    for blk in range(x.shape[0] // B_BLOCK):
        rs = slice(blk * B_BLOCK, (blk + 1) * B_BLOCK)
        for p in range(k.shape[1] // LANES):
            k_ref[0, p, blk] = k[rs, p * LANES:(p + 1) * LANES].astype(BF16)
            vt_ref[0, p, blk] = vt[p * LANES:(p + 1) * LANES, rs].astype(BF16)
        km_ref[0, blk] = jnp.mean(k[rs], axis=0, keepdims=True)


def _kv_kernel(x_ref, *refs):
    _kv_body(x_ref[0], *refs)


def _kv(x, sh, sc, nw, w_k, w_v):
    bsz, s, d = x.shape
    nb = s // B_BLOCK
    npair = d // LANES
    vec = pl.BlockSpec((1, 1, d), lambda b, j: (b, 0, 0))
    full2 = lambda shape: pl.BlockSpec(shape, lambda b, j: (0, 0))
    return pl.pallas_call(
        _kv_kernel,
        grid=(bsz, nb),
        in_specs=[pl.BlockSpec((1, B_BLOCK, d), lambda b, j: (b, j, 0)), vec, vec,
                  full2((1, d)), full2((d, d)), full2((d, d))],
        out_specs=[pl.BlockSpec((1, npair, 1, B_BLOCK, LANES), lambda b, j: (b, 0, j, 0, 0)),
                   pl.BlockSpec((1, npair, 1, LANES, B_BLOCK), lambda b, j: (b, 0, j, 0, 0)),
                   pl.BlockSpec((1, 1, 1, d), lambda b, j: (b, j, 0, 0))],
        out_shape=[jax.ShapeDtypeStruct((bsz, npair, nb, B_BLOCK, LANES), BF16),
                   jax.ShapeDtypeStruct((bsz, npair, nb, LANES, B_BLOCK), BF16),
                   jax.ShapeDtypeStruct((bsz, nb, 1, d), F32)],
        compiler_params=_cparams(("arbitrary", "arbitrary")),
    )(x, sh, sc, nw.reshape(1, d), w_k.astype(BF16), w_v.T.astype(BF16))


def _attn_body(x, qb, sh_ref, sc_ref, g_ref, nw_ref, wqt_ref, wo_ref, k_ref, vt_ref, km_ref,
               qt_sc, qs_sc, acc_sc, sel_sc, m_sc, l_sc, *, nb, n_sel):
    bq = x.shape[0]
    npair = qt_sc.shape[0]
    nbp = km_ref.shape[2]
    hd = LANES // 2
    scale = float(hd) ** -0.5 * 1.4426950408889634
    h = _rms(x, nw_ref[...]) * (1.0 + sc_ref[0]) + sh_ref[0]
    qt = _bdot_nt(wqt_ref[...], h.astype(BF16))
    for p in range(npair):
        qt_sc[p] = qt[p * LANES:(p + 1) * LANES, :]

    subn = lax.broadcasted_iota(I32, (nbp, bq), 0)
    past = subn < qb
    krow = lax.broadcasted_iota(I32, (B_BLOCK, bq), 0)
    qcol = lax.broadcasted_iota(I32, (B_BLOCK, bq), 1)
    causal = krow <= qcol
    rowh = lax.broadcasted_iota(I32, (LANES, 1), 0)

    grp = ATTN_PAIR_UNROLL
    heads = [(u, e) for u in range(grp) for e in range(2)]
    ones_rows = jnp.ones((SUM_ROWS, B_BLOCK), BF16)

    def vsum(vt2, e):
        return jnp.concatenate([vt2[e * hd:(e + 1) * hd, :], ones_rows], axis=0)

    def own_body(gi, carry):
        ps = [gi * grp + u for u in range(grp)]
        q2ts = [qt_sc[p] for p in ps]
        kms = [km_ref[0, p] for p in ps]
        kown = [k_ref[0, p, qb] for p in ps]
        vown = [vt_ref[0, p, qb] for p in ps]
        qets = [jnp.where((rowh >= hd) if e == 1 else (rowh < hd), q2ts[u], 0.0) for u, e in heads]
        qsts = [(q * scale).astype(BF16) for q in qets]
        ss = [jnp.where(causal, _bdot(kown[u], qsts[i]), NEG_INF) for i, (u, e) in enumerate(heads)]
        gates = [_dot3(kms[u], qets[i]) for i, (u, e) in enumerate(heads)]
        ms = [jnp.max(s, axis=0, keepdims=True) for s in ss]
        pes = [jnp.exp2(s - m) for s, m in zip(ss, ms)]
        pvs = [_bdot(vsum(vown[u], e), pes[i].astype(BF16)) for i, (u, e) in enumerate(heads)]
        accs = [pv[:hd] for pv in pvs]
        ls = [pv[hd:hd + 1] for pv in pvs]
        sels = []
        for gate in gates:
            selt = jnp.zeros((nbp, bq), F32)
            for n in range(nb):
                gn = gate[n:n + 1, :]
                beats = jnp.logical_or(gate > gn, jnp.logical_and(gate == gn, subn < n))
                beats = jnp.logical_and(beats, past)
                cnt = jnp.sum(beats.astype(F32), axis=0, keepdims=True)
                selt = jnp.where(subn == n, (cnt < n_sel).astype(F32), selt)
            sels.append(selt)
        for i, (u, e) in enumerate(heads):
            p = ps[u]
            sel_sc[p, e] = sels[i]
            qs_sc[p, e] = qsts[i]
            m_sc[p, e] = ms[i]
            l_sc[p, e] = ls[i]
            acc_sc[p, e * hd:(e + 1) * hd, :] = accs[i]
        return carry

    lax.fori_loop(0, npair // grp, own_body, 0)

    grp = ATTN_PAIR_UNROLL_PAST
    heads = [(u, e) for u in range(grp) for e in range(2)]

    def kb_body(kb, carry):
        def group_body(gi, c2):
            ps = [gi * grp + u for u in range(grp)]
            kbl = [k_ref[0, p, kb] for p in ps]
            vbl = [vt_ref[0, p, kb] for p in ps]
            qsts = [qs_sc[ps[u], e] for u, e in heads]
            rows = [sel_sc[ps[u], e, pl.ds(kb, 1), :] for u, e in heads]
            m_old = [m_sc[ps[u], e] for u, e in heads]
            l_old = [l_sc[ps[u], e] for u, e in heads]
            a_old = [acc_sc[ps[u], e * hd:(e + 1) * hd, :] for u, e in heads]
            ss = [jnp.where(rows[i] > 0.5, _bdot(kbl[u], qsts[i]), NEG_INF)
                  for i, (u, e) in enumerate(heads)]
            m_new = [jnp.maximum(m, jnp.max(s, axis=0, keepdims=True)) for m, s in zip(m_old, ss)]
            alphas = [jnp.exp2(m - mn) for m, mn in zip(m_old, m_new)]
            pes = [jnp.exp2(s - mn) for s, mn in zip(ss, m_new)]
            pvs = [_bdot(vsum(vbl[u], e), pes[i].astype(BF16)) for i, (u, e) in enumerate(heads)]
            l_new = [a * l + pv[hd:hd + 1] for a, l, pv in zip(alphas, l_old, pvs)]
            a_new = [a * ao + pv[:hd] for a, ao, pv in zip(alphas, a_old, pvs)]
            for i, (u, e) in enumerate(heads):
                p = ps[u]
                m_sc[p, e] = m_new[i]
                l_sc[p, e] = l_new[i]
                acc_sc[p, e * hd:(e + 1) * hd, :] = a_new[i]
            return c2

        lax.fori_loop(0, npair // grp, group_body, 0)
        return carry

    lax.fori_loop(0, qb, kb_body, 0)

    parts = []
    for p in range(npair):
        for e in range(2):
            parts.append(acc_sc[p, e * hd:(e + 1) * hd, :] / l_sc[p, e])
    ot = jnp.concatenate(parts, axis=0)
    return x + g_ref[0] * _bdot(ot.T.astype(BF16), wo_ref[...])


def _attn_mixer(mods, nw, w_q, w_o, k5, vt5, km2):
    sh, sc, gt = mods
    d = w_q.shape[0]
    npair, nb = k5.shape[1], k5.shape[2]
    nbp = km2.shape[2]
    n_sel = min(B_TOPK, nb - 1)
    per = TM_SORT // B_BLOCK
    vec = pl.BlockSpec((1, 1, d), lambda b, j: (b, 0, 0))
    full2 = lambda shape: pl.BlockSpec(shape, lambda b, j: (0, 0))
    specs = [vec, vec, vec, full2((1, d)), full2((d, d)), full2((d, d)),
             pl.BlockSpec((1, npair, nb, B_BLOCK, LANES), lambda b, j: (b, 0, 0, 0, 0)),
             pl.BlockSpec((1, npair, nb, LANES, B_BLOCK), lambda b, j: (b, 0, 0, 0, 0)),
             pl.BlockSpec((1, npair, nbp, LANES), lambda b, j: (b, 0, 0, 0))]
    args = [sh, sc, gt, nw.reshape(1, d), w_q.T.astype(BF16), w_o.astype(BF16), k5, vt5, km2]
    scratch = [pltpu.VMEM((npair, LANES, B_BLOCK), F32),
               pltpu.VMEM((npair, 2, LANES, B_BLOCK), BF16),
               pltpu.VMEM((npair, LANES, B_BLOCK), F32),
               pltpu.VMEM((npair, 2, nbp, B_BLOCK), F32),
               pltpu.VMEM((npair, 2, 1, B_BLOCK), F32),
               pltpu.VMEM((npair, 2, 1, B_BLOCK), F32)]

    def body(x_ref, refs, scr):
        j = pl.program_id(1)
        outs = [_attn_body(x_ref[0, h * B_BLOCK:(h + 1) * B_BLOCK, :], j * per + h, *refs, *scr,
                           nb=nb, n_sel=n_sel) for h in range(per)]
        return jnp.concatenate(outs, axis=0)

    return body, specs, args, scratch


def kernel(x, c, ada_w, ada_b, norm_mix, norm_ffn, a_w_in, a_b_in, a_ln_g, a_ln_b, a_w_s, a_b_s,
           a_w_out, kv_norm, kv_ada_w, kv_ada_b, kv_w_k, kv_w_v, b_w_q, b_w_o, moe_router, moe_bias,
           moe_w_gate, moe_w_up, moe_w_down, sh_w_gate, sh_w_up, sh_w_down, final_norm):
    bsz, s, d = x.shape
    depth = ada_w.shape[0]
    n_a = a_w_in.shape[0]
    assert s % B_BLOCK == 0 and s % TM_SORT == 0 and TM_SORT % A_CHUNK == 0 and d % LANES == 0
    nb = s // B_BLOCK
    npair = d // LANES
    nbp = -(-nb // 8) * 8

    def split(m, n):
        return [m[:, i * d:(i + 1) * d].reshape(bsz, 1, d) for i in range(n)]

    def pair_major(km):
        km = km.reshape(bsz, nb, npair, LANES).transpose(0, 2, 1, 3)
        return jnp.pad(km, ((0, 0), (0, 0), (0, nbp - nb), (0, 0)))

    layer_mods = _ada(c, ada_w, ada_b)
    k5 = vt5 = km2 = None
    for i in range(depth):
        sh1, sc1, g1, sh2, sc2, g2 = split(layer_mods[i], 6)
        if i < n_a:
            mixer = _gmlp_mixer((sh1, sc1, g1), norm_mix[i], a_w_in[i], a_b_in[i], a_ln_g[i], a_ln_b[i],
                                a_w_s[i], a_b_s[i], a_w_out[i])
        else:
            if k5 is None:
                ksh, ksc = split(_ada(c, kv_ada_w[None], kv_ada_b[None])[0], 2)
                k5, vt5, km = _kv(x, ksh, ksc, kv_norm, kv_w_k, kv_w_v)
                km2 = pair_major(km)
            j = i - n_a
            mixer = _attn_mixer((sh1, sc1, g1), norm_mix[i], b_w_q[j], b_w_o[j], k5, vt5, km2)
        kv = None
        if i == n_a - 1 and i + 1 < depth:
            ksh, ksc = split(_ada(c, kv_ada_w[None], kv_ada_b[None])[0], 2)
            kv = (ksh, ksc, kv_norm, kv_w_k, kv_w_v)
        res = _moe(x, (sh2, sc2, g2), norm_ffn[i], moe_router[i], moe_bias[i], i, moe_w_gate,
                   moe_w_up, moe_w_down, sh_w_gate[i], sh_w_up[i], sh_w_down[i],
                   final_norm, i == depth - 1, mixer, kv)
        if kv is not None:
            x, k5, vt5, km = res
            km2 = pair_major(km)
        else:
            x = res
    return x
```

```python
import functools

import jax
import jax.numpy as jnp
from jax import lax
from jax.experimental import pallas as pl
from jax.experimental.pallas import tpu as pltpu

F32 = jnp.float32
BF16 = jnp.bfloat16
I32 = jnp.int32

RMS_EPS = 1e-6
LN_EPS = 1e-5
NEG_INF = -1e30

A_CHUNK = 128
A_GROUPS = 8
B_HEADS = 16
B_BLOCK = 256
B_TOPK = 3
N_EXPERTS = 64
TOP_K = 8
N_GROUPS = 8
TOPK_GROUPS = 4
ROUTED_SCALE = 2.5

LANES = 128
VMEM_LIMIT = 56 * 1024 * 1024

TM_SORT = 512
RUN_ALIGN = 16
SORT_CHUNK = 256
ROWS_TILE_USED = TM_SORT * TOP_K + N_EXPERTS * (RUN_ALIGN - 1)
ROWS_TILE = -(-ROWS_TILE_USED // SORT_CHUNK) * SORT_CHUNK
PIECES_TILE = ROWS_TILE // RUN_ALIGN
PIECES_PAD = -(-PIECES_TILE // LANES) * LANES
BM_EXPERT = 1024
BM_SEG = 512
BM_CHAIN = 256
ATTN_PAIR_UNROLL = 4
ATTN_PAIR_UNROLL_PAST = 8
SUM_ROWS = 16


def _cparams(sem):
    return pltpu.CompilerParams(dimension_semantics=sem, vmem_limit_bytes=VMEM_LIMIT)


def _sigmoid(x):
    return 1.0 / (1.0 + jnp.exp(-x))


def _silu(x):
    return x * _sigmoid(x)


def _gelu_tanh(x):
    hx = 0.5 * x
    return hx + hx * jnp.tanh(x * (0.7978845608028654 + 0.035677408136300125 * (x * x)))


def _rms(x, g):
    return x * lax.rsqrt(jnp.mean(x * x, axis=-1, keepdims=True) + RMS_EPS) * g


def _bdot(a, b):
    return jnp.dot(a, b, preferred_element_type=F32)


def _bdot_nt(a, b):
    return lax.dot_general(a, b, (((1,), (1,)), ((), ())), preferred_element_type=F32)


def _split(a):
    hi = a.astype(BF16)
    lo = (a - hi.astype(F32)).astype(BF16)
    return hi, lo


def _dot3_nt(a, b):
    ah, al = _split(a)
    bh, bl = _split(b)
    return _bdot_nt(ah, bh) + (_bdot_nt(ah, bl) + _bdot_nt(al, bh))


def _dot3(a, b):
    ah, al = _split(a)
    bh, bl = _split(b)
    return _bdot(ah, bh) + (_bdot(ah, bl) + _bdot(al, bh))


def _ada_kernel(c_ref, w_ref, b_ref, o_ref):
    a = _silu(c_ref[...]).astype(BF16)
    o_ref[0] = _bdot(a, w_ref[0].astype(BF16)) + b_ref[0]


def _ada(c, w, b):
    bsz, d = c.shape
    nl, _, n = w.shape
    tn = 1024
    return pl.pallas_call(
        _ada_kernel,
        grid=(nl, n // tn),
        in_specs=[pl.BlockSpec((bsz, d), lambda l, j: (0, 0)),
                  pl.BlockSpec((1, d, tn), lambda l, j: (l, 0, j)),
                  pl.BlockSpec((1, 1, tn), lambda l, j: (l, 0, j))],
        out_specs=pl.BlockSpec((1, bsz, tn), lambda l, j: (l, 0, j)),
        out_shape=jax.ShapeDtypeStruct((nl, bsz, n), F32),
        compiler_params=_cparams(("arbitrary", "arbitrary")),
    )(c, w, b.reshape(nl, 1, n))


def _gmlp_body(x, sh_ref, sc_ref, g_ref, nw_ref, win_ref, bin_ref, lng_ref, lnb_ref,
               ws_ref, bst_ref, wout_ref, y_sc):
    tm = x.shape[0]
    h = _rms(x, nw_ref[...]) * (1.0 + sc_ref[0]) + sh_ref[0]
    z = _gelu_tanh(_bdot(h.astype(BF16), win_ref[...]) + bin_ref[...])
    aw = z.shape[1] // 2
    gd = aw // A_GROUPS
    u = z[:, :aw]
    v = z[:, aw:]
    mu = jnp.mean(v, axis=-1, keepdims=True)
    dv = v - mu
    var = jnp.mean(dv * dv, axis=-1, keepdims=True)
    vn = (dv * lax.rsqrt(var + LN_EPS) * lng_ref[...] + lnb_ref[...]).astype(BF16)
    row = lax.broadcasted_iota(I32, (A_CHUNK, A_CHUNK), 0)
    col = lax.broadcasted_iota(I32, (A_CHUNK, A_CHUNK), 1)
    causal = col <= row
    for g in range(A_GROUPS):
        wg = jnp.where(causal, ws_ref[g], 0.0).astype(BF16)
        bcol = bst_ref[:, g:g + 1]
        for ci in range(tm // A_CHUNK):
            rs = slice(ci * A_CHUNK, (ci + 1) * A_CHUNK)
            cs = slice(g * gd, (g + 1) * gd)
            sv = _bdot(wg, vn[rs, cs]) + bcol
            y_sc[rs, cs] = (u[rs, cs] * sv).astype(BF16)
    return x + g_ref[0] * _bdot(y_sc[...], wout_ref[...])


def _route(x, sh_ref, sc_ref, g_ref, nw_ref, wrt_ref, bias_ref, wsg_ref, wsu_ref, wsd_ref,
           h_ref, base_ref, key_ref, wkey_ref, cnt_ref, pe_ref, po_ref):
    tm = x.shape[0]
    h = _rms(x, nw_ref[...]) * (1.0 + sc_ref[0]) + sh_ref[0]
    hb = h.astype(BF16)
    h_ref[...] = hb
    act = (_silu(_bdot(hb, wsg_ref[...])) * _bdot(hb, wsu_ref[...])).astype(BF16)
    base_ref[...] = x + g_ref[0] * _bdot(act, wsd_ref[...])

    scores = _sigmoid(_dot3_nt(wrt_ref[...], h))
    choice = scores + bias_ref[...]
    gsz = N_EXPERTS // N_GROUPS
    sub = lax.broadcasted_iota(I32, (gsz, tm), 0)
    blocks = [choice[g * gsz:(g + 1) * gsz] for g in range(N_GROUPS)]
    gscore = []
    for blk in blocks:
        m1 = jnp.max(blk, axis=0, keepdims=True)
        i1 = jnp.min(jnp.where(blk == m1, sub, gsz), axis=0, keepdims=True)
        m2 = jnp.max(jnp.where(sub == i1, -jnp.inf, blk), axis=0, keepdims=True)
        gscore.append(m1 + m2)
    masked = []
    for g in range(N_GROUPS):
        beats = jnp.zeros((1, tm), F32)
        for m in range(N_GROUPS):
            if m == g:
                continue
            b = gscore[m] > gscore[g]
            if m < g:
                b = jnp.logical_or(b, gscore[m] == gscore[g])
            beats = beats + b.astype(F32)
        masked.append(jnp.where(beats < TOPK_GROUPS, blocks[g], NEG_INF))
    cur = jnp.concatenate(masked, axis=0)
    eio = lax.broadcasted_iota(I32, (N_EXPERTS, tm), 0)
    chosen = jnp.zeros((N_EXPERTS, tm), jnp.bool_)
    wsum = jnp.zeros((1, tm), F32)
    for _ in range(TOP_K):
        m = jnp.max(cur, axis=0, keepdims=True)
        idx = jnp.min(jnp.where(cur == m, eio, N_EXPERTS), axis=0, keepdims=True)
        sel = eio == idx
        chosen = jnp.logical_or(chosen, sel)
        wsum = wsum + jnp.sum(jnp.where(sel, scores, 0.0), axis=0, keepdims=True)
        cur = jnp.where(sel, -jnp.inf, cur)
    wkey_ref[...] = jnp.where(chosen, scores / wsum * ROUTED_SCALE, 0.0)

    onehot = chosen.astype(BF16)
    r_i = lax.broadcasted_iota(I32, (tm, tm), 0)
    c_i = lax.broadcasted_iota(I32, (tm, tm), 1)
    before = (r_i < c_i).astype(BF16)
    prior = _bdot(onehot, before)
    key_ref[...] = jnp.where(chosen, prior, -1.0).astype(I32)
    cnt = jnp.sum(chosen.astype(F32), axis=1, keepdims=True)
    cnt_ref[0] = cnt

    run_p = jnp.floor((cnt + (RUN_ALIGN - 1)) / RUN_ALIGN)
    ppc = SORT_CHUNK // RUN_ALIGN
    tot = jnp.sum(run_p, axis=0, keepdims=True)
    fill = jnp.ceil(tot / ppc) * ppc - tot
    run_p = run_p + jnp.where(lax.broadcasted_iota(I32, (N_EXPERTS, 1), 0) == N_EXPERTS - 1, fill, 0.0)
    e_r = lax.broadcasted_iota(I32, (N_EXPERTS, N_EXPERTS), 0)
    e_c = lax.broadcasted_iota(I32, (N_EXPERTS, N_EXPERTS), 1)
    incl = (e_c <= e_r).astype(BF16)
    lend = _bdot(incl, jnp.broadcast_to(run_p, (N_EXPERTS, LANES)).astype(BF16))[:, 0:1]
    loff = lend - run_p
    pj = lax.broadcasted_iota(I32, (N_EXPERTS, PIECES_PAD), 1).astype(F32)
    er = lax.broadcasted_iota(I32, (N_EXPERTS, PIECES_PAD), 0).astype(F32)
    pe = jnp.minimum(jnp.sum((lend <= pj).astype(F32), axis=0, keepdims=True), N_EXPERTS - 1.0)
    lo = jnp.sum(jnp.where(er == pe, loff, 0.0), axis=0, keepdims=True)
    pe_ref[0] = pe.astype(I32)
    po_ref[0] = ((pj[0:1, :] - lo) * RUN_ALIGN).astype(I32)


N_ROUTE_REFS = 16


def _router_kernel(x_ref, *refs):
    _route(x_ref[0], *refs)


def _mixer_router_kernel(x_ref, *refs, body, n_mix):
    route_refs = refs[n_mix:n_mix + N_ROUTE_REFS]
    x1 = body(x_ref, refs[:n_mix], refs[n_mix + N_ROUTE_REFS:])
    _route(x1, *route_refs)


def _gmlp_mixer(mods, nw, w_in, b_in, ln_g, ln_b, w_s, b_s, w_out):
    sh, sc, gt = mods
    d, n_in = w_in.shape
    aw = n_in // 2
    vec = pl.BlockSpec((1, 1, d), lambda b, j: (b, 0, 0))
    full2 = lambda shape: pl.BlockSpec(shape, lambda b, j: (0, 0))
    specs = [vec, vec, vec, full2((1, d)), full2((d, n_in)), full2((1, n_in)), full2((1, aw)),
             full2((1, aw)), pl.BlockSpec((A_GROUPS, A_CHUNK, A_CHUNK), lambda b, j: (0, 0, 0)),
             full2((A_CHUNK, A_GROUPS)), full2((aw, d))]
    args = [sh, sc, gt, nw.reshape(1, d), w_in.astype(BF16), b_in.reshape(1, n_in),
            ln_g.reshape(1, aw), ln_b.reshape(1, aw), w_s, b_s.T, w_out.astype(BF16)]

    def body(x_ref, refs, scratch):
        return _gmlp_body(x_ref[0], *refs, *scratch)

    return body, specs, args, [pltpu.VMEM((TM_SORT, aw), BF16)]


def _router(x, mods, nw, w_router, e_bias, wsg, wsu, wsd, mixer=None):
    bsz, s, d = x.shape
    t = bsz * s
    tm = TM_SORT
    nt = s // tm
    sd = wsg.shape[1]
    vec = pl.BlockSpec((1, 1, d), lambda b, j: (b, 0, 0))
    full2 = lambda shape: pl.BlockSpec(shape, lambda b, j: (0, 0))
    tok = pl.BlockSpec((tm, d), lambda b, j: (b * nt + j, 0))
    etok = pl.BlockSpec((N_EXPERTS, tm), lambda b, j: (0, b * nt + j))
    ptab = pl.BlockSpec((1, 1, PIECES_PAD), lambda b, j: (b * nt + j, 0, 0))
    sh, sc, gt = mods
    in_specs = [vec, vec, vec, full2((1, d)), full2((N_EXPERTS, d)), full2((N_EXPERTS, 1)),
                full2((d, sd)), full2((d, sd)), full2((sd, d))]
    args = [sh, sc, gt, nw.reshape(1, d), w_router.T, e_bias.reshape(N_EXPERTS, 1),
            wsg.astype(BF16), wsu.astype(BF16), wsd.astype(BF16)]
    kern, scratch = _router_kernel, []
    if mixer is not None:
        body, mspecs, margs, scratch = mixer
        kern = functools.partial(_mixer_router_kernel, body=body, n_mix=len(margs))
        in_specs = mspecs + in_specs
        args = margs + args
    return pl.pallas_call(
        kern,
        grid=(bsz, nt),
        in_specs=[pl.BlockSpec((1, tm, d), lambda b, j: (b, j, 0))] + in_specs,
        out_specs=[tok, tok, etok, etok,
                   pl.BlockSpec((1, N_EXPERTS, 1), lambda b, j: (b * nt + j, 0, 0)), ptab, ptab],
        out_shape=[jax.ShapeDtypeStruct((t, d), BF16), jax.ShapeDtypeStruct((t, d), F32),
                   jax.ShapeDtypeStruct((N_EXPERTS, t), I32), jax.ShapeDtypeStruct((N_EXPERTS, t), F32),
                   jax.ShapeDtypeStruct((t // tm, N_EXPERTS, 1), F32),
                   jax.ShapeDtypeStruct((t // tm, 1, PIECES_PAD), I32),
                   jax.ShapeDtypeStruct((t // tm, 1, PIECES_PAD), I32)],
        scratch_shapes=scratch,
        compiler_params=_cparams(("arbitrary", "arbitrary")),
    )(x, *args)


def _rows_max(t):
    rows = (t // TM_SORT) * ROWS_TILE + N_EXPERTS * (BM_SEG - RUN_ALIGN)
    return -(-rows // BM_EXPERT) * BM_EXPERT


def _sort_meta(cnt, n_rows_max):
    nts = cnt.shape[0]
    c = cnt.reshape(nts, N_EXPERTS).astype(I32)
    run = (c + (RUN_ALIGN - 1)) // RUN_ALIGN * RUN_ALIGN
    fill = (-jnp.sum(run, axis=1, keepdims=True)) % SORT_CHUNK
    run = jnp.concatenate([run[:, :-1], run[:, -1:] + fill], axis=1)
    used = jnp.sum(run, axis=1)
    per_e = jnp.sum(run, axis=0)
    seg = (per_e + (BM_SEG - 1)) // BM_SEG * BM_SEG
    ends = jnp.cumsum(seg)
    starts = ends - seg
    goff = starts[None, :] + jnp.cumsum(run, axis=0) - run
    pad_start = starts + per_e
    pad_cnt = (seg - per_e) // RUN_ALIGN

    nh = n_rows_max // BM_SEG
    hidx = jnp.arange(nh, dtype=I32)
    he = jnp.sum((ends[None, :] <= (hidx * BM_SEG)[:, None]).astype(I32), axis=1)
    real = he < N_EXPERTS
    he = jnp.minimum(he, N_EXPERTS - 1).astype(I32)
    prev = jnp.concatenate([jnp.full((1,), -1, I32), he[:-1]])
    hnew = jnp.logical_and(real, he != prev)
    hslot = (jnp.cumsum(hnew.astype(I32)) - 1) % 2
    later = jnp.logical_and(hidx[None, :] > hidx[:, None], hnew[None, :])
    nxt = jnp.min(jnp.where(later, hidx[None, :], nh), axis=1)
    hfetch = jnp.where(nxt < nh, he[jnp.minimum(nxt, nh - 1)], -1)
    per_blk = BM_EXPERT // BM_SEG
    bx = jnp.where(real[::per_blk], jnp.arange(nh // per_blk, dtype=I32), 0)
    i32 = lambda a: a.reshape(-1).astype(I32)
    return ((i32(goff), i32(used), i32(pad_start), i32(pad_cnt)),
            (bx, he, i32(hnew), i32(real), i32(hslot), i32(hfetch)))


def _piece_copies(pe_ref, po_ref, goff_ref, used_ref, tile, local, remote, sem, to_remote):
    def copy(j):
        idx = tile * PIECES_PAD + j
        l0 = pl.multiple_of(j * RUN_ALIGN, RUN_ALIGN)
        g0 = pl.multiple_of(goff_ref[tile * N_EXPERTS + pe_ref[idx]] + po_ref[idx], RUN_ALIGN)
        lref = local.at[pl.ds(l0, RUN_ALIGN)]
        gref = remote.at[pl.ds(g0, RUN_ALIGN)]
        return pltpu.make_async_copy(lref, gref, sem) if to_remote else pltpu.make_async_copy(gref, lref, sem)

    def wait_all():
        def body(j, carry):
            copy(j).wait()
            return carry

        lax.fori_loop(0, used_ref[tile] // RUN_ALIGN, body, 0)

    return copy, wait_all


def _pad_copies(pad_start_ref, pad_cnt_ref, tile, ntiles, zeros, remote, sem):
    share = -(-N_EXPERTS // ntiles)

    def apply(act):
        for q in range(share):
            e = tile * share + q
            ec = jnp.minimum(e, N_EXPERTS - 1)
            n = jnp.where(e < N_EXPERTS, pad_cnt_ref[ec], 0)
            base = pad_start_ref[ec]

            def body(j, carry):
                g0 = pl.multiple_of(base + j * RUN_ALIGN, RUN_ALIGN)
                getattr(pltpu.make_async_copy(zeros, remote.at[pl.ds(g0, RUN_ALIGN)], sem), act)()
                return carry

            lax.fori_loop(0, n, body, 0)

    return apply


def _piece_rows(pe_ref, po_ref, key_ref, first_piece, npieces, val_ref=None):
    tm = key_ref.shape[1]
    sub = lax.broadcasted_iota(I32, (RUN_ALIGN, tm), 0)
    out = []
    for jj in range(npieces):
        j = first_piece + jj
        e = pe_ref[j]
        hit = (key_ref[pl.ds(e, 1), :] - po_ref[j]) == sub
        val = 1.0 if val_ref is None else val_ref[pl.ds(e, 1), :]
        out.append(jnp.where(hit, val, 0.0))
    return out


def _dispatch_kernel(pe_ref, po_ref, goff_ref, used_ref, pad_start_ref, pad_cnt_ref,
                     key_ref, h_ref, xs_ref, xbuf, p_sc, zbuf, sem, zsem, *, ntiles):
    i = pl.program_id(0)
    hb = h_ref[...]
    ch = SORT_CHUNK
    ppc = ch // RUN_ALIGN
    zbuf[...] = jnp.zeros_like(zbuf)
    pads = _pad_copies(pad_start_ref, pad_cnt_ref, i, ntiles, zbuf, xs_ref, zsem)
    pads("start")

    copy, wait_all = _piece_copies(pe_ref, po_ref, goff_ref, used_ref, i, xbuf, xs_ref, sem, True)

    def build(ci):
        rows = _piece_rows(pe_ref, po_ref, key_ref, i * PIECES_PAD + ci * ppc, ppc)
        for jj, p in enumerate(rows):
            p_sc[jj * RUN_ALIGN:(jj + 1) * RUN_ALIGN, :] = p.astype(BF16)
        xbuf[pl.ds(pl.multiple_of(ci * ch, ch), ch), :] = _bdot(p_sc[...], hb).astype(BF16)

    def send(ci):
        for jj in range(ppc):
            copy(ci * ppc + jj).start()

    def step(ci, carry):
        send(ci - 1)
        build(ci)
        return carry

    nchunk = used_ref[i] // ch
    build(0)
    lax.fori_loop(1, nchunk, step, 0)
    send(nchunk - 1)
    pads("wait")
    wait_all()


def _dispatch(meta, pe, po, key, hb, n_rows_max):
    t, d = hb.shape
    tm = TM_SORT
    goff, used, pad_start, pad_cnt = meta
    return pl.pallas_call(
        functools.partial(_dispatch_kernel, ntiles=t // tm),
        grid_spec=pltpu.PrefetchScalarGridSpec(
            num_scalar_prefetch=6, grid=(t // tm,),
            in_specs=[pl.BlockSpec((N_EXPERTS, tm), lambda i, *_: (0, i)),
                      pl.BlockSpec((tm, d), lambda i, *_: (i, 0))],
            out_specs=pl.BlockSpec(memory_space=pl.ANY),
            scratch_shapes=[pltpu.VMEM((ROWS_TILE, d), BF16), pltpu.VMEM((SORT_CHUNK, tm), BF16),
                            pltpu.VMEM((RUN_ALIGN, d), BF16),
                            pltpu.SemaphoreType.DMA(()), pltpu.SemaphoreType.DMA(())]),
        out_shape=jax.ShapeDtypeStruct((n_rows_max, d), BF16),
        compiler_params=_cparams(("arbitrary",)),
    )(pe, po, goff, used, pad_start, pad_cnt, key, hb)


def _expert_kernel(bx_ref, he_ref, hnew_ref, hreal_ref, hslot_ref, hfetch_ref, x_ref, wg_hbm, wu_hbm, wd_hbm,
                   o_ref, wg_st, wu_st, wd_st, wg_sc, wu_sc, wd_sc, sem, *, layer):
    b = pl.program_id(0)
    per_blk = BM_EXPERT // BM_SEG
    cpp = BM_SEG // BM_CHAIN

    def fetch(e):
        return [pltpu.make_async_copy(src.at[layer, e], dst, sem.at[i])
                for i, (src, dst) in enumerate(((wg_hbm, wg_st), (wu_hbm, wu_st), (wd_hbm, wd_st)))]

    @pl.when(b == 0)
    def _():
        for cp in fetch(he_ref[0]):
            cp.start()

    for part in range(per_blk):
        h = b * per_blk + part

        @pl.when(hnew_ref[h] == 1)
        def _():
            for cp in fetch(he_ref[h]):
                cp.wait()
            s = hslot_ref[h]
            wg_sc[s] = wg_st[...].astype(BF16)
            wu_sc[s] = wu_st[...].astype(BF16)
            wd_sc[s] = wd_st[...].astype(BF16)

            @pl.when(hfetch_ref[h] >= 0)
            def _():
                for cp in fetch(hfetch_ref[h]):
                    cp.start()

    def run_parts(nparts):
        chains = [(slice((p * cpp + c) * BM_CHAIN, (p * cpp + c + 1) * BM_CHAIN), hslot_ref[b * per_blk + p])
                  for p in range(nparts) for c in range(cpp)]
        xs = [x_ref[rs, :] for rs, _ in chains]
        gs = [_bdot(x, wg_sc[s]) for x, (_, s) in zip(xs, chains)]
        us = [_bdot(x, wu_sc[s]) for x, (_, s) in zip(xs, chains)]
        acts = [(_silu(g) * u).astype(BF16) for g, u in zip(gs, us)]
        for (rs, s), act in zip(chains, acts):
            o_ref[rs, :] = _bdot(act, wd_sc[s]).astype(BF16)
        if nparts < per_blk:
            o_ref[nparts * BM_SEG:, :] = jnp.zeros((BM_EXPERT - nparts * BM_SEG, o_ref.shape[1]), BF16)

    nlive = hreal_ref[b * per_blk]
    for part in range(1, per_blk):
        nlive = nlive + hreal_ref[b * per_blk + part]

    @pl.when(nlive == 0)
    def _():
        o_ref[...] = jnp.zeros_like(o_ref)

    for n in range(1, per_blk + 1):
        @pl.when(nlive == n)
        def _():
            run_parts(n)


def _experts(meta, xs, layer, w_gate, w_up, w_down):
    r, d = xs.shape
    ed = w_gate.shape[-1]
    bm = BM_EXPERT
    omap = lambda b, bx, *_: (b, 0)
    xmap = lambda b, bx, *_: (bx[b], 0)
    hbm = pl.BlockSpec(memory_space=pl.ANY)
    return pl.pallas_call(
        functools.partial(_expert_kernel, layer=layer),
        grid_spec=pltpu.PrefetchScalarGridSpec(
            num_scalar_prefetch=6, grid=(r // bm,),
            in_specs=[pl.BlockSpec((bm, d), xmap), hbm, hbm, hbm],
            out_specs=pl.BlockSpec((bm, d), omap),
            scratch_shapes=[pltpu.VMEM((d, ed), F32), pltpu.VMEM((d, ed), F32), pltpu.VMEM((ed, d), F32),
                            pltpu.VMEM((2, d, ed), BF16), pltpu.VMEM((2, d, ed), BF16),
                            pltpu.VMEM((2, ed, d), BF16), pltpu.SemaphoreType.DMA((3,))]),
        out_shape=jax.ShapeDtypeStruct((r, d), BF16),
        compiler_params=_cparams(("arbitrary",)),
    )(*meta, xs, w_gate, w_up, w_down)


def _combine_kernel(pe_ref, po_ref, goff_ref, used_ref, key_ref, wkey_ref, ys_ref, base_ref, g_ref,
                    fn_ref, *rest, final, with_kv):
    if with_kv:
        kv_in, o_ref, kv_out, (ybuf, q_sc, sem) = rest[:5], rest[5], rest[6:9], rest[9:]
    else:
        o_ref, ybuf, q_sc, sem = rest
    i = pl.program_id(0)
    tm = base_ref.shape[0]
    ch = SORT_CHUNK
    ppc = ch // RUN_ALIGN

    @pl.when(i == 0)
    def _():
        ybuf[...] = jnp.zeros_like(ybuf)

    copy, wait_all = _piece_copies(pe_ref, po_ref, goff_ref, used_ref, i, ybuf, ys_ref, sem, False)

    for ci in range(ybuf.shape[0] // ch):
        cs = slice(ci * ch, (ci + 1) * ch)

        @pl.when(ci * ch < used_ref[i])
        def _():
            for jj in range(ppc):
                copy(ci * ppc + jj).start()
            rows = _piece_rows(pe_ref, po_ref, key_ref, i * PIECES_PAD + ci * ppc, ppc, wkey_ref)
            q_sc[:, cs] = jnp.concatenate(rows, axis=0).T.astype(BF16)

        @pl.when(ci * ch >= used_ref[i])
        def _():
            q_sc[:, cs] = jnp.zeros((tm, ch), BF16)

    wait_all()
    out = base_ref[...] + g_ref[0] * _bdot(q_sc[...], ybuf[...])
    if final:
        out = _rms(out, fn_ref[...])
    o_ref[...] = out
    if with_kv:
        _kv_body(out, *kv_in, *kv_out)


def _combine(meta, pe, po, key, wkey, ys, base, gt, fnorm, seq, final, kv=None):
    t, d = base.shape
    tm = TM_SORT
    nt = seq // tm
    bsz = t // seq
    vecb = pl.BlockSpec((1, 1, d), lambda i, *_: (i // nt, 0, 0))
    full2 = lambda shape: pl.BlockSpec(shape, lambda i, *_: (0, 0))
    in_specs = [pl.BlockSpec((N_EXPERTS, tm), lambda i, *_: (0, i)),
                pl.BlockSpec((N_EXPERTS, tm), lambda i, *_: (0, i)),
                pl.BlockSpec(memory_space=pl.ANY),
                pl.BlockSpec((tm, d), lambda i, *_: (i, 0)), vecb, full2((1, d))]
    args = [key, wkey, ys, base, gt, fnorm.reshape(1, d)]
    out_specs = [pl.BlockSpec((tm, d), lambda i, *_: (i, 0))]
    out_shape = [jax.ShapeDtypeStruct((t, d), F32)]
    if kv is not None:
        ksh, ksc, knw, w_k, w_v = kv
        nb, npair, per = seq // B_BLOCK, d // LANES, tm // B_BLOCK
        in_specs += [vecb, vecb, full2((1, d)), full2((d, d)), full2((d, d))]
        args += [ksh, ksc, knw.reshape(1, d), w_k.astype(BF16), w_v.T.astype(BF16)]
        out_specs += [pl.BlockSpec((1, npair, per, B_BLOCK, LANES), lambda i, *_: (i // nt, 0, i % nt, 0, 0)),
                      pl.BlockSpec((1, npair, per, LANES, B_BLOCK), lambda i, *_: (i // nt, 0, i % nt, 0, 0)),
                      pl.BlockSpec((1, per, 1, d), lambda i, *_: (i // nt, i % nt, 0, 0))]
        out_shape += [jax.ShapeDtypeStruct((bsz, npair, nb, B_BLOCK, LANES), BF16),
                      jax.ShapeDtypeStruct((bsz, npair, nb, LANES, B_BLOCK), BF16),
                      jax.ShapeDtypeStruct((bsz, nb, 1, d), F32)]
    res = pl.pallas_call(
        functools.partial(_combine_kernel, final=final, with_kv=kv is not None),
        grid_spec=pltpu.PrefetchScalarGridSpec(
            num_scalar_prefetch=4, grid=(t // tm,),
            in_specs=in_specs, out_specs=out_specs,
            scratch_shapes=[pltpu.VMEM((ROWS_TILE, d), BF16), pltpu.VMEM((tm, ROWS_TILE), BF16),
                            pltpu.SemaphoreType.DMA(())]),
        out_shape=out_shape,
        compiler_params=_cparams(("arbitrary",)),
    )(pe, po, meta[0], meta[1], *args)
    return res if kv is not None else res[0]


def _moe(x, mods, nw, w_router, e_bias, layer, w_gate, w_up, w_down, wsg, wsu, wsd, fnorm, final, mixer, kv=None):
    bsz, s, d = x.shape
    t = bsz * s
    gt = mods[2]
    n_rows_max = _rows_max(t)
    hb, base, key, wkey, cnt, pe, po = _router(x, mods, nw, w_router, e_bias, wsg, wsu, wsd, mixer)
    pe = pe.reshape(-1)
    po = po.reshape(-1)
    layout_meta, block_meta = _sort_meta(cnt, n_rows_max)
    xs = _dispatch(layout_meta, pe, po, key, hb, n_rows_max)
    ys = _experts(block_meta, xs, layer, w_gate, w_up, w_down)
    out = _combine(layout_meta, pe, po, key, wkey, ys, base, gt, fnorm, s, final, kv)
    if kv is not None:
        return (out[0].reshape(bsz, s, d),) + tuple(out[1:])
    return out.reshape(bsz, s, d)


def _kv_body(x, sh_ref, sc_ref, nw_ref, wk_ref, wvt_ref, k_ref, vt_ref, km_ref):
    hb = (_rms(x, nw_ref[...]) * (1.0 + sc_ref[0]) + sh_ref[0]).astype(BF16)
    k = _bdot(hb, wk_ref[...])
    vt = _bdot_nt(wvt_ref[...], hb)
    for blk in range(x.shape[0] // B_BLOCK):
        rs = slice(blk * B_BLOCK, (blk + 1) * B_BLOCK)
        for p in range(k.shape[1] // LANES):
            k_ref[0, p, blk] = k[rs, p * LANES:(p + 1) * LANES].astype(BF16)
            vt_ref[0, p, blk] = vt[p * LANES:(p + 1) * LANES, rs].astype(BF16)
        km_ref[0, blk] = jnp.mean(k[rs], axis=0, keepdims=True)


def _kv_kernel(x_ref, *refs):
    _kv_body(x_ref[0], *refs)


def _kv(x, sh, sc, nw, w_k, w_v):
    bsz, s, d = x.shape
    nb = s // B_BLOCK
    npair = d // LANES
    vec = pl.BlockSpec((1, 1, d), lambda b, j: (b, 0, 0))
    full2 = lambda shape: pl.BlockSpec(shape, lambda b, j: (0, 0))
    return pl.pallas_call(
        _kv_kernel,
        grid=(bsz, nb),
        in_specs=[pl.BlockSpec((1, B_BLOCK, d), lambda b, j: (b, j, 0)), vec, vec,
                  full2((1, d)), full2((d, d)), full2((d, d))],
        out_specs=[pl.BlockSpec((1, npair, 1, B_BLOCK, LANES), lambda b, j: (b, 0, j, 0, 0)),
                   pl.BlockSpec((1, npair, 1, LANES, B_BLOCK), lambda b, j: (b, 0, j, 0, 0)),
                   pl.BlockSpec((1, 1, 1, d), lambda b, j: (b, j, 0, 0))],
        out_shape=[jax.ShapeDtypeStruct((bsz, npair, nb, B_BLOCK, LANES), BF16),
                   jax.ShapeDtypeStruct((bsz, npair, nb, LANES, B_BLOCK), BF16),
                   jax.ShapeDtypeStruct((bsz, nb, 1, d), F32)],
        compiler_params=_cparams(("arbitrary", "arbitrary")),
    )(x, sh, sc, nw.reshape(1, d), w_k.astype(BF16), w_v.T.astype(BF16))


def _attn_body(x, qb, sh_ref, sc_ref, g_ref, nw_ref, wqt_ref, wo_ref, k_ref, vt_ref, km_ref,
               qt_sc, qs_sc, acc_sc, sel_sc, m_sc, l_sc, *, nb, n_sel):
    bq = x.shape[0]
    npair = qt_sc.shape[0]
    nbp = km_ref.shape[2]
    hd = LANES // 2
    scale = float(hd) ** -0.5 * 1.4426950408889634
    h = _rms(x, nw_ref[...]) * (1.0 + sc_ref[0]) + sh_ref[0]
    qt = _bdot_nt(wqt_ref[...], h.astype(BF16))
    for p in range(npair):
        qt_sc[p] = qt[p * LANES:(p + 1) * LANES, :]

    subn = lax.broadcasted_iota(I32, (nbp, bq), 0)
    past = subn < qb
    krow = lax.broadcasted_iota(I32, (B_BLOCK, bq), 0)
    qcol = lax.broadcasted_iota(I32, (B_BLOCK, bq), 1)
    causal = krow <= qcol
    rowh = lax.broadcasted_iota(I32, (LANES, 1), 0)

    grp = ATTN_PAIR_UNROLL
    heads = [(u, e) for u in range(grp) for e in range(2)]
    ones_rows = jnp.ones((SUM_ROWS, B_BLOCK), BF16)

    def vsum(vt2, e):
        return jnp.concatenate([vt2[e * hd:(e + 1) * hd, :], ones_rows], axis=0)

    def own_body(gi, carry):
        ps = [gi * grp + u for u in range(grp)]
        q2ts = [qt_sc[p] for p in ps]
        kms = [km_ref[0, p] for p in ps]
        kown = [k_ref[0, p, qb] for p in ps]
        vown = [vt_ref[0, p, qb] for p in ps]
        qets = [jnp.where((rowh >= hd) if e == 1 else (rowh < hd), q2ts[u], 0.0) for u, e in heads]
        qsts = [(q * scale).astype(BF16) for q in qets]
        ss = [jnp.where(causal, _bdot(kown[u], qsts[i]), NEG_INF) for i, (u, e) in enumerate(heads)]
        gates = [_dot3(kms[u], qets[i]) for i, (u, e) in enumerate(heads)]
        ms = [jnp.max(s, axis=0, keepdims=True) for s in ss]
        pes = [jnp.exp2(s - m) for s, m in zip(ss, ms)]
        pvs = [_bdot(vsum(vown[u], e), pes[i].astype(BF16)) for i, (u, e) in enumerate(heads)]
        accs = [pv[:hd] for pv in pvs]
        ls = [pv[hd:hd + 1] for pv in pvs]
        sels = []
        for gate in gates:
            selt = jnp.zeros((nbp, bq), F32)
            for n in range(nb):
                gn = gate[n:n + 1, :]
                beats = jnp.logical_or(gate > gn, jnp.logical_and(gate == gn, subn < n))
                beats = jnp.logical_and(beats, past)
                cnt = jnp.sum(beats.astype(F32), axis=0, keepdims=True)
                selt = jnp.where(subn == n, (cnt < n_sel).astype(F32), selt)
            sels.append(selt)
        for i, (u, e) in enumerate(heads):
            p = ps[u]
            sel_sc[p, e] = sels[i]
            qs_sc[p, e] = qsts[i]
            m_sc[p, e] = ms[i]
            l_sc[p, e] = ls[i]
            acc_sc[p, e * hd:(e + 1) * hd, :] = accs[i]
        return carry

    lax.fori_loop(0, npair // grp, own_body, 0)

    grp = ATTN_PAIR_UNROLL_PAST
    heads = [(u, e) for u in range(grp) for e in range(2)]

    def kb_body(kb, carry):
        def group_body(gi, c2):
            ps = [gi * grp + u for u in range(grp)]
            kbl = [k_ref[0, p, kb] for p in ps]
            vbl = [vt_ref[0, p, kb] for p in ps]
            qsts = [qs_sc[ps[u], e] for u, e in heads]
            rows = [sel_sc[ps[u], e, pl.ds(kb, 1), :] for u, e in heads]
            m_old = [m_sc[ps[u], e] for u, e in heads]
            l_old = [l_sc[ps[u], e] for u, e in heads]
            a_old = [acc_sc[ps[u], e * hd:(e + 1) * hd, :] for u, e in heads]
            ss = [jnp.where(rows[i] > 0.5, _bdot(kbl[u], qsts[i]), NEG_INF)
                  for i, (u, e) in enumerate(heads)]
            m_new = [jnp.maximum(m, jnp.max(s, axis=0, keepdims=True)) for m, s in zip(m_old, ss)]
            alphas = [jnp.exp2(m - mn) for m, mn in zip(m_old, m_new)]
            pes = [jnp.exp2(s - mn) for s, mn in zip(ss, m_new)]
            pvs = [_bdot(vsum(vbl[u], e), pes[i].astype(BF16)) for i, (u, e) in enumerate(heads)]
            l_new = [a * l + pv[hd:hd + 1] for a, l, pv in zip(alphas, l_old, pvs)]
            a_new = [a * ao + pv[:hd] for a, ao, pv in zip(alphas, a_old, pvs)]
            for i, (u, e) in enumerate(heads):
                p = ps[u]
                m_sc[p, e] = m_new[i]
                l_sc[p, e] = l_new[i]
                acc_sc[p, e * hd:(e + 1) * hd, :] = a_new[i]
            return c2

        lax.fori_loop(0, npair // grp, group_body, 0)
        return carry

    lax.fori_loop(0, qb, kb_body, 0)

    parts = []
    for p in range(npair):
        for e in range(2):
            parts.append(acc_sc[p, e * hd:(e + 1) * hd, :] / l_sc[p, e])
    ot = jnp.concatenate(parts, axis=0)
    return x + g_ref[0] * _bdot(ot.T.astype(BF16), wo_ref[...])


def _attn_mixer(mods, nw, w_q, w_o, k5, vt5, km2):
    sh, sc, gt = mods
    d = w_q.shape[0]
    npair, nb = k5.shape[1], k5.shape[2]
    nbp = km2.shape[2]
    n_sel = min(B_TOPK, nb - 1)
    per = TM_SORT // B_BLOCK
    vec = pl.BlockSpec((1, 1, d), lambda b, j: (b, 0, 0))
    full2 = lambda shape: pl.BlockSpec(shape, lambda b, j: (0, 0))
    specs = [vec, vec, vec, full2((1, d)), full2((d, d)), full2((d, d)),
             pl.BlockSpec((1, npair, nb, B_BLOCK, LANES), lambda b, j: (b, 0, 0, 0, 0)),
             pl.BlockSpec((1, npair, nb, LANES, B_BLOCK), lambda b, j: (b, 0, 0, 0, 0)),
             pl.BlockSpec((1, npair, nbp, LANES), lambda b, j: (b, 0, 0, 0))]
    args = [sh, sc, gt, nw.reshape(1, d), w_q.T.astype(BF16), w_o.astype(BF16), k5, vt5, km2]
    scratch = [pltpu.VMEM((npair, LANES, B_BLOCK), F32),
               pltpu.VMEM((npair, 2, LANES, B_BLOCK), BF16),
               pltpu.VMEM((npair, LANES, B_BLOCK), F32),
               pltpu.VMEM((npair, 2, nbp, B_BLOCK), F32),
               pltpu.VMEM((npair, 2, 1, B_BLOCK), F32),
               pltpu.VMEM((npair, 2, 1, B_BLOCK), F32)]

    def body(x_ref, refs, scr):
        j = pl.program_id(1)
        outs = [_attn_body(x_ref[0, h * B_BLOCK:(h + 1) * B_BLOCK, :], j * per + h, *refs, *scr,
                           nb=nb, n_sel=n_sel) for h in range(per)]
        return jnp.concatenate(outs, axis=0)

    return body, specs, args, scratch


def kernel(x, c, ada_w, ada_b, norm_mix, norm_ffn, a_w_in, a_b_in, a_ln_g, a_ln_b, a_w_s, a_b_s,
           a_w_out, kv_norm, kv_ada_w, kv_ada_b, kv_w_k, kv_w_v, b_w_q, b_w_o, moe_router, moe_bias,
           moe_w_gate, moe_w_up, moe_w_down, sh_w_gate, sh_w_up, sh_w_down, final_norm):
    bsz, s, d = x.shape
    depth = ada_w.shape[0]
    n_a = a_w_in.shape[0]
    assert s % B_BLOCK == 0 and s % TM_SORT == 0 and TM_SORT % A_CHUNK == 0 and d % LANES == 0
    nb = s // B_BLOCK
    npair = d // LANES
    nbp = -(-nb // 8) * 8

    def split(m, n):
        return [m[:, i * d:(i + 1) * d].reshape(bsz, 1, d) for i in range(n)]

    def pair_major(km):
        km = km.reshape(bsz, nb, npair, LANES).transpose(0, 2, 1, 3)
        return jnp.pad(km, ((0, 0), (0, 0), (0, nbp - nb), (0, 0)))

    layer_mods = _ada(c, ada_w, ada_b)
    k5 = vt5 = km2 = None
    for i in range(depth):
        sh1, sc1, g1, sh2, sc2, g2 = split(layer_mods[i], 6)
        if i < n_a:
            mixer = _gmlp_mixer((sh1, sc1, g1), norm_mix[i], a_w_in[i], a_b_in[i], a_ln_g[i], a_ln_b[i],
                                a_w_s[i], a_b_s[i], a_w_out[i])
        else:
            if k5 is None:
                ksh, ksc = split(_ada(c, kv_ada_w[None], kv_ada_b[None])[0], 2)
                k5, vt5, km = _kv(x, ksh, ksc, kv_norm, kv_w_k, kv_w_v)
                km2 = pair_major(km)
            j = i - n_a
            mixer = _attn_mixer((sh1, sc1, g1), norm_mix[i], b_w_q[j], b_w_o[j], k5, vt5, km2)
        kv = None
        if i == n_a - 1 and i + 1 < depth:
            ksh, ksc = split(_ada(c, kv_ada_w[None], kv_ada_b[None])[0], 2)
            kv = (ksh, ksc, kv_norm, kv_w_k, kv_w_v)
        res = _moe(x, (sh2, sc2, g2), norm_ffn[i], moe_router[i], moe_bias[i], i, moe_w_gate,
                   moe_w_up, moe_w_down, sh_w_gate[i], sh_w_up[i], sh_w_down[i],
                   final_norm, i == depth - 1, mixer, kv)
        if kv is not None:
            x, k5, vt5, km = res
            km2 = pair_major(km)
        else:
            x = res
    return x
```

```python
import functools

import jax
import jax.numpy as jnp
from jax import lax
from jax.experimental import pallas as pl
from jax.experimental.pallas import tpu as pltpu

F32 = jnp.float32
BF16 = jnp.bfloat16
I32 = jnp.int32

RMS_EPS = 1e-6
LN_EPS = 1e-5
NEG_INF = -1e30

A_CHUNK = 128
A_GROUPS = 8
B_HEADS = 16
B_BLOCK = 256
B_TOPK = 3
N_EXPERTS = 64
TOP_K = 8
N_GROUPS = 8
TOPK_GROUPS = 4
ROUTED_SCALE = 2.5

LANES = 128
VMEM_LIMIT = 56 * 1024 * 1024

TM_SORT = 512
RUN_ALIGN = 16
SORT_CHUNK = 256
ROWS_TILE_USED = TM_SORT * TOP_K + N_EXPERTS * (RUN_ALIGN - 1)
ROWS_TILE = -(-ROWS_TILE_USED // SORT_CHUNK) * SORT_CHUNK
PIECES_TILE = ROWS_TILE // RUN_ALIGN
PIECES_PAD = -(-PIECES_TILE // LANES) * LANES
BM_EXPERT = 1024
BM_SEG = 256
BM_CHAIN = 256
ATTN_PAIR_UNROLL = 4
ATTN_PAIR_UNROLL_PAST = 8
SUM_ROWS = 16


def _cparams(sem):
    return pltpu.CompilerParams(dimension_semantics=sem, vmem_limit_bytes=VMEM_LIMIT)


def _sigmoid(x):
    return 1.0 / (1.0 + jnp.exp(-x))


def _silu(x):
    return x * _sigmoid(x)


def _gelu_tanh(x):
    hx = 0.5 * x
    return hx + hx * jnp.tanh(x * (0.7978845608028654 + 0.035677408136300125 * (x * x)))


def _rms(x, g):
    return x * lax.rsqrt(jnp.mean(x * x, axis=-1, keepdims=True) + RMS_EPS) * g


def _bdot(a, b):
    return jnp.dot(a, b, preferred_element_type=F32)


def _bdot_nt(a, b):
    return lax.dot_general(a, b, (((1,), (1,)), ((), ())), preferred_element_type=F32)


def _split(a):
    hi = a.astype(BF16)
    lo = (a - hi.astype(F32)).astype(BF16)
    return hi, lo


def _dot3_nt(a, b):
    ah, al = _split(a)
    bh, bl = _split(b)
    return _bdot_nt(ah, bh) + (_bdot_nt(ah, bl) + _bdot_nt(al, bh))


def _dot3(a, b):
    ah, al = _split(a)
    bh, bl = _split(b)
    return _bdot(ah, bh) + (_bdot(ah, bl) + _bdot(al, bh))


def _ada_kernel(c_ref, w_ref, b_ref, o_ref):
    a = _silu(c_ref[...]).astype(BF16)
    o_ref[0] = _bdot(a, w_ref[0].astype(BF16)) + b_ref[0]


def _ada(c, w, b):
    bsz, d = c.shape
    nl, _, n = w.shape
    tn = 1024
    return pl.pallas_call(
        _ada_kernel,
        grid=(nl, n // tn),
        in_specs=[pl.BlockSpec((bsz, d), lambda l, j: (0, 0)),
                  pl.BlockSpec((1, d, tn), lambda l, j: (l, 0, j)),
                  pl.BlockSpec((1, 1, tn), lambda l, j: (l, 0, j))],
        out_specs=pl.BlockSpec((1, bsz, tn), lambda l, j: (l, 0, j)),
        out_shape=jax.ShapeDtypeStruct((nl, bsz, n), F32),
        compiler_params=_cparams(("arbitrary", "arbitrary")),
    )(c, w, b.reshape(nl, 1, n))


def _gmlp_body(x, sh_ref, sc_ref, g_ref, nw_ref, win_ref, bin_ref, lng_ref, lnb_ref,
               ws_ref, bst_ref, wout_ref, y_sc):
    tm = x.shape[0]
    h = _rms(x, nw_ref[...]) * (1.0 + sc_ref[0]) + sh_ref[0]
    z = _gelu_tanh(_bdot(h.astype(BF16), win_ref[...]) + bin_ref[...])
    aw = z.shape[1] // 2
    gd = aw // A_GROUPS
    u = z[:, :aw]
    v = z[:, aw:]
    mu = jnp.mean(v, axis=-1, keepdims=True)
    dv = v - mu
    var = jnp.mean(dv * dv, axis=-1, keepdims=True)
    vn = (dv * lax.rsqrt(var + LN_EPS) * lng_ref[...] + lnb_ref[...]).astype(BF16)
    row = lax.broadcasted_iota(I32, (A_CHUNK, A_CHUNK), 0)
    col = lax.broadcasted_iota(I32, (A_CHUNK, A_CHUNK), 1)
    causal = col <= row
    for g in range(A_GROUPS):
        wg = jnp.where(causal, ws_ref[g], 0.0).astype(BF16)
        bcol = bst_ref[:, g:g + 1]
        for ci in range(tm // A_CHUNK):
            rs = slice(ci * A_CHUNK, (ci + 1) * A_CHUNK)
            cs = slice(g * gd, (g + 1) * gd)
            sv = _bdot(wg, vn[rs, cs]) + bcol
            y_sc[rs, cs] = (u[rs, cs] * sv).astype(BF16)
    return x + g_ref[0] * _bdot(y_sc[...], wout_ref[...])


def _route(x, sh_ref, sc_ref, g_ref, nw_ref, wrt_ref, bias_ref, wsg_ref, wsu_ref, wsd_ref,
           h_ref, base_ref, key_ref, wkey_ref, cnt_ref, pe_ref, po_ref):
    tm = x.shape[0]
    h = _rms(x, nw_ref[...]) * (1.0 + sc_ref[0]) + sh_ref[0]
    hb = h.astype(BF16)
    h_ref[...] = hb
    act = (_silu(_bdot(hb, wsg_ref[...])) * _bdot(hb, wsu_ref[...])).astype(BF16)
    base_ref[...] = x + g_ref[0] * _bdot(act, wsd_ref[...])

    scores = _sigmoid(_dot3_nt(wrt_ref[...], h))
    choice = scores + bias_ref[...]
    gsz = N_EXPERTS // N_GROUPS
    sub = lax.broadcasted_iota(I32, (gsz, tm), 0)
    blocks = [choice[g * gsz:(g + 1) * gsz] for g in range(N_GROUPS)]
    gscore = []
    for blk in blocks:
        m1 = jnp.max(blk, axis=0, keepdims=True)
        i1 = jnp.min(jnp.where(blk == m1, sub, gsz), axis=0, keepdims=True)
        m2 = jnp.max(jnp.where(sub == i1, -jnp.inf, blk), axis=0, keepdims=True)
        gscore.append(m1 + m2)
    masked = []
    for g in range(N_GROUPS):
        beats = jnp.zeros((1, tm), F32)
        for m in range(N_GROUPS):
            if m == g:
                continue
            b = gscore[m] > gscore[g]
            if m < g:
                b = jnp.logical_or(b, gscore[m] == gscore[g])
            beats = beats + b.astype(F32)
        masked.append(jnp.where(beats < TOPK_GROUPS, blocks[g], NEG_INF))
    cur = jnp.concatenate(masked, axis=0)
    eio = lax.broadcasted_iota(I32, (N_EXPERTS, tm), 0)
    chosen = jnp.zeros((N_EXPERTS, tm), jnp.bool_)
    wsum = jnp.zeros((1, tm), F32)
    for _ in range(TOP_K):
        m = jnp.max(cur, axis=0, keepdims=True)
        idx = jnp.min(jnp.where(cur == m, eio, N_EXPERTS), axis=0, keepdims=True)
        sel = eio == idx
        chosen = jnp.logical_or(chosen, sel)
        wsum = wsum + jnp.sum(jnp.where(sel, scores, 0.0), axis=0, keepdims=True)
        cur = jnp.where(sel, -jnp.inf, cur)
    wkey_ref[...] = jnp.where(chosen, scores / wsum * ROUTED_SCALE, 0.0)

    onehot = chosen.astype(BF16)
    r_i = lax.broadcasted_iota(I32, (tm, tm), 0)
    c_i = lax.broadcasted_iota(I32, (tm, tm), 1)
    before = (r_i < c_i).astype(BF16)
    prior = _bdot(onehot, before)
    key_ref[...] = jnp.where(chosen, prior, -1.0).astype(I32)
    cnt = jnp.sum(chosen.astype(F32), axis=1, keepdims=True)
    cnt_ref[0] = cnt

    run_p = jnp.floor((cnt + (RUN_ALIGN - 1)) / RUN_ALIGN)
    ppc = SORT_CHUNK // RUN_ALIGN
    tot = jnp.sum(run_p, axis=0, keepdims=True)
    fill = jnp.ceil(tot / ppc) * ppc - tot
    run_p = run_p + jnp.where(lax.broadcasted_iota(I32, (N_EXPERTS, 1), 0) == N_EXPERTS - 1, fill, 0.0)
    e_r = lax.broadcasted_iota(I32, (N_EXPERTS, N_EXPERTS), 0)
    e_c = lax.broadcasted_iota(I32, (N_EXPERTS, N_EXPERTS), 1)
    incl = (e_c <= e_r).astype(BF16)
    lend = _bdot(incl, jnp.broadcast_to(run_p, (N_EXPERTS, LANES)).astype(BF16))[:, 0:1]
    loff = lend - run_p
    pj = lax.broadcasted_iota(I32, (N_EXPERTS, PIECES_PAD), 1).astype(F32)
    er = lax.broadcasted_iota(I32, (N_EXPERTS, PIECES_PAD), 0).astype(F32)
    pe = jnp.minimum(jnp.sum((lend <= pj).astype(F32), axis=0, keepdims=True), N_EXPERTS - 1.0)
    lo = jnp.sum(jnp.where(er == pe, loff, 0.0), axis=0, keepdims=True)
    pe_ref[0] = pe.astype(I32)
    po_ref[0] = ((pj[0:1, :] - lo) * RUN_ALIGN).astype(I32)


N_ROUTE_REFS = 16


def _router_kernel(x_ref, *refs):
    _route(x_ref[0], *refs)


def _mixer_router_kernel(x_ref, *refs, body, n_mix):
    route_refs = refs[n_mix:n_mix + N_ROUTE_REFS]
    x1 = body(x_ref, refs[:n_mix], refs[n_mix + N_ROUTE_REFS:])
    _route(x1, *route_refs)


def _gmlp_mixer(mods, nw, w_in, b_in, ln_g, ln_b, w_s, b_s, w_out):
    sh, sc, gt = mods
    d, n_in = w_in.shape
    aw = n_in // 2
    vec = pl.BlockSpec((1, 1, d), lambda b, j: (b, 0, 0))
    full2 = lambda shape: pl.BlockSpec(shape, lambda b, j: (0, 0))
    specs = [vec, vec, vec, full2((1, d)), full2((d, n_in)), full2((1, n_in)), full2((1, aw)),
             full2((1, aw)), pl.BlockSpec((A_GROUPS, A_CHUNK, A_CHUNK), lambda b, j: (0, 0, 0)),
             full2((A_CHUNK, A_GROUPS)), full2((aw, d))]
    args = [sh, sc, gt, nw.reshape(1, d), w_in.astype(BF16), b_in.reshape(1, n_in),
            ln_g.reshape(1, aw), ln_b.reshape(1, aw), w_s, b_s.T, w_out.astype(BF16)]

    def body(x_ref, refs, scratch):
        return _gmlp_body(x_ref[0], *refs, *scratch)

    return body, specs, args, [pltpu.VMEM((TM_SORT, aw), BF16)]


def _router(x, mods, nw, w_router, e_bias, wsg, wsu, wsd, mixer=None):
    bsz, s, d = x.shape
    t = bsz * s
    tm = TM_SORT
    nt = s // tm
    sd = wsg.shape[1]
    vec = pl.BlockSpec((1, 1, d), lambda b, j: (b, 0, 0))
    full2 = lambda shape: pl.BlockSpec(shape, lambda b, j: (0, 0))
    tok = pl.BlockSpec((tm, d), lambda b, j: (b * nt + j, 0))
    etok = pl.BlockSpec((N_EXPERTS, tm), lambda b, j: (0, b * nt + j))
    ptab = pl.BlockSpec((1, 1, PIECES_PAD), lambda b, j: (b * nt + j, 0, 0))
    sh, sc, gt = mods
    in_specs = [vec, vec, vec, full2((1, d)), full2((N_EXPERTS, d)), full2((N_EXPERTS, 1)),
                full2((d, sd)), full2((d, sd)), full2((sd, d))]
    args = [sh, sc, gt, nw.reshape(1, d), w_router.T, e_bias.reshape(N_EXPERTS, 1),
            wsg.astype(BF16), wsu.astype(BF16), wsd.astype(BF16)]
    kern, scratch = _router_kernel, []
    if mixer is not None:
        body, mspecs, margs, scratch = mixer
        kern = functools.partial(_mixer_router_kernel, body=body, n_mix=len(margs))
        in_specs = mspecs + in_specs
        args = margs + args
    return pl.pallas_call(
        kern,
        grid=(bsz, nt),
        in_specs=[pl.BlockSpec((1, tm, d), lambda b, j: (b, j, 0))] + in_specs,
        out_specs=[tok, tok, etok, etok,
                   pl.BlockSpec((1, N_EXPERTS, 1), lambda b, j: (b * nt + j, 0, 0)), ptab, ptab],
        out_shape=[jax.ShapeDtypeStruct((t, d), BF16), jax.ShapeDtypeStruct((t, d), F32),
                   jax.ShapeDtypeStruct((N_EXPERTS, t), I32), jax.ShapeDtypeStruct((N_EXPERTS, t), F32),
                   jax.ShapeDtypeStruct((t // tm, N_EXPERTS, 1), F32),
                   jax.ShapeDtypeStruct((t // tm, 1, PIECES_PAD), I32),
                   jax.ShapeDtypeStruct((t // tm, 1, PIECES_PAD), I32)],
        scratch_shapes=scratch,
        compiler_params=_cparams(("arbitrary", "arbitrary")),
    )(x, *args)


def _rows_max(t):
    rows = (t // TM_SORT) * ROWS_TILE + N_EXPERTS * (BM_SEG - RUN_ALIGN)
    return -(-rows // BM_EXPERT) * BM_EXPERT


def _sort_meta(cnt, n_rows_max):
    nts = cnt.shape[0]
    c = cnt.reshape(nts, N_EXPERTS).astype(I32)
    run = (c + (RUN_ALIGN - 1)) // RUN_ALIGN * RUN_ALIGN
    fill = (-jnp.sum(run, axis=1, keepdims=True)) % SORT_CHUNK
    run = jnp.concatenate([run[:, :-1], run[:, -1:] + fill], axis=1)
    used = jnp.sum(run, axis=1)
    per_e = jnp.sum(run, axis=0)
    seg = (per_e + (BM_SEG - 1)) // BM_SEG * BM_SEG
    ends = jnp.cumsum(seg)
    starts = ends - seg
    goff = starts[None, :] + jnp.cumsum(run, axis=0) - run
    pad_start = starts + per_e
    pad_cnt = (seg - per_e) // RUN_ALIGN

    nh = n_rows_max // BM_SEG
    hidx = jnp.arange(nh, dtype=I32)
    he = jnp.sum((ends[None, :] <= (hidx * BM_SEG)[:, None]).astype(I32), axis=1)
    real = he < N_EXPERTS
    he = jnp.minimum(he, N_EXPERTS - 1).astype(I32)
    prev = jnp.concatenate([jnp.full((1,), -1, I32), he[:-1]])
    hnew = jnp.logical_and(real, he != prev)
    per_blk = BM_EXPERT // BM_SEG
    hslot = (jnp.cumsum(hnew.astype(I32)) - 1) % per_blk
    later = jnp.logical_and(hidx[None, :] > hidx[:, None], hnew[None, :])
    nxt = jnp.min(jnp.where(later, hidx[None, :], nh), axis=1)
    hfetch = jnp.where(nxt < nh, he[jnp.minimum(nxt, nh - 1)], -1)
    bx = jnp.where(real[::per_blk], jnp.arange(nh // per_blk, dtype=I32), 0)
    i32 = lambda a: a.reshape(-1).astype(I32)
    return ((i32(goff), i32(used), i32(pad_start), i32(pad_cnt)),
            (bx, he, i32(hnew), i32(real), i32(hslot), i32(hfetch)))


def _piece_copies(pe_ref, po_ref, goff_ref, used_ref, tile, local, remote, sem, to_remote):
    def copy(j):
        idx = tile * PIECES_PAD + j
        l0 = pl.multiple_of(j * RUN_ALIGN, RUN_ALIGN)
        g0 = pl.multiple_of(goff_ref[tile * N_EXPERTS + pe_ref[idx]] + po_ref[idx], RUN_ALIGN)
        lref = local.at[pl.ds(l0, RUN_ALIGN)]
        gref = remote.at[pl.ds(g0, RUN_ALIGN)]
        return pltpu.make_async_copy(lref, gref, sem) if to_remote else pltpu.make_async_copy(gref, lref, sem)

    def wait_all():
        def body(j, carry):
            copy(j).wait()
            return carry

        lax.fori_loop(0, used_ref[tile] // RUN_ALIGN, body, 0)

    return copy, wait_all


def _pad_copies(pad_start_ref, pad_cnt_ref, tile, ntiles, zeros, remote, sem):
    share = -(-N_EXPERTS // ntiles)

    def apply(act):
        for q in range(share):
            e = tile * share + q
            ec = jnp.minimum(e, N_EXPERTS - 1)
            n = jnp.where(e < N_EXPERTS, pad_cnt_ref[ec], 0)
            base = pad_start_ref[ec]

            def body(j, carry):
                g0 = pl.multiple_of(base + j * RUN_ALIGN, RUN_ALIGN)
                getattr(pltpu.make_async_copy(zeros, remote.at[pl.ds(g0, RUN_ALIGN)], sem), act)()
                return carry

            lax.fori_loop(0, n, body, 0)

    return apply


def _piece_rows(pe_ref, po_ref, key_ref, first_piece, npieces, val_ref=None):
    tm = key_ref.shape[1]
    sub = lax.broadcasted_iota(I32, (RUN_ALIGN, tm), 0)
    out = []
    for jj in range(npieces):
        j = first_piece + jj
        e = pe_ref[j]
        hit = (key_ref[pl.ds(e, 1), :] - po_ref[j]) == sub
        val = 1.0 if val_ref is None else val_ref[pl.ds(e, 1), :]
        out.append(jnp.where(hit, val, 0.0))
    return out


def _dispatch_kernel(pe_ref, po_ref, goff_ref, used_ref, pad_start_ref, pad_cnt_ref,
                     key_ref, h_ref, xs_ref, xbuf, p_sc, zbuf, sem, zsem, *, ntiles):
    i = pl.program_id(0)
    hb = h_ref[...]
    ch = SORT_CHUNK
    ppc = ch // RUN_ALIGN
    zbuf[...] = jnp.zeros_like(zbuf)
    pads = _pad_copies(pad_start_ref, pad_cnt_ref, i, ntiles, zbuf, xs_ref, zsem)
    pads("start")

    copy, wait_all = _piece_copies(pe_ref, po_ref, goff_ref, used_ref, i, xbuf, xs_ref, sem, True)

    def build(ci):
        rows = _piece_rows(pe_ref, po_ref, key_ref, i * PIECES_PAD + ci * ppc, ppc)
        for jj, p in enumerate(rows):
            p_sc[jj * RUN_ALIGN:(jj + 1) * RUN_ALIGN, :] = p.astype(BF16)
        xbuf[pl.ds(pl.multiple_of(ci * ch, ch), ch), :] = _bdot(p_sc[...], hb).astype(BF16)

    def send(ci):
        for jj in range(ppc):
            copy(ci * ppc + jj).start()

    def step(ci, carry):
        send(ci - 1)
        build(ci)
        return carry

    nchunk = used_ref[i] // ch
    build(0)
    lax.fori_loop(1, nchunk, step, 0)
    send(nchunk - 1)
    pads("wait")
    wait_all()


def _dispatch(meta, pe, po, key, hb, n_rows_max):
    t, d = hb.shape
    tm = TM_SORT
    goff, used, pad_start, pad_cnt = meta
    return pl.pallas_call(
        functools.partial(_dispatch_kernel, ntiles=t // tm),
        grid_spec=pltpu.PrefetchScalarGridSpec(
            num_scalar_prefetch=6, grid=(t // tm,),
            in_specs=[pl.BlockSpec((N_EXPERTS, tm), lambda i, *_: (0, i)),
                      pl.BlockSpec((tm, d), lambda i, *_: (i, 0))],
            out_specs=pl.BlockSpec(memory_space=pl.ANY),
            scratch_shapes=[pltpu.VMEM((ROWS_TILE, d), BF16), pltpu.VMEM((SORT_CHUNK, tm), BF16),
                            pltpu.VMEM((RUN_ALIGN, d), BF16),
                            pltpu.SemaphoreType.DMA(()), pltpu.SemaphoreType.DMA(())]),
        out_shape=jax.ShapeDtypeStruct((n_rows_max, d), BF16),
        compiler_params=_cparams(("arbitrary",)),
    )(pe, po, goff, used, pad_start, pad_cnt, key, hb)


def _expert_kernel(bx_ref, he_ref, hnew_ref, hreal_ref, hslot_ref, hfetch_ref, x_ref, wg_hbm, wu_hbm, wd_hbm,
                   o_ref, wg_st, wu_st, wd_st, wg_sc, wu_sc, wd_sc, sem, *, layer):
    b = pl.program_id(0)
    per_blk = BM_EXPERT // BM_SEG
    cpp = BM_SEG // BM_CHAIN

    def fetch(e):
        return [pltpu.make_async_copy(src.at[layer, e], dst, sem.at[i])
                for i, (src, dst) in enumerate(((wg_hbm, wg_st), (wu_hbm, wu_st), (wd_hbm, wd_st)))]

    @pl.when(b == 0)
    def _():
        for cp in fetch(he_ref[0]):
            cp.start()

    for part in range(per_blk):
        h = b * per_blk + part

        @pl.when(hnew_ref[h] == 1)
        def _():
            for cp in fetch(he_ref[h]):
                cp.wait()
            s = hslot_ref[h]
            wg_sc[s] = wg_st[...].astype(BF16)
            wu_sc[s] = wu_st[...].astype(BF16)
            wd_sc[s] = wd_st[...].astype(BF16)

            @pl.when(hfetch_ref[h] >= 0)
            def _():
                for cp in fetch(hfetch_ref[h]):
                    cp.start()

    def run_parts(nparts):
        chains = [(slice((p * cpp + c) * BM_CHAIN, (p * cpp + c + 1) * BM_CHAIN), hslot_ref[b * per_blk + p])
                  for p in range(nparts) for c in range(cpp)]
        xs = [x_ref[rs, :] for rs, _ in chains]
        gs = [_bdot(x, wg_sc[s]) for x, (_, s) in zip(xs, chains)]
        us = [_bdot(x, wu_sc[s]) for x, (_, s) in zip(xs, chains)]
        acts = [(_silu(g) * u).astype(BF16) for g, u in zip(gs, us)]
        for (rs, s), act in zip(chains, acts):
            o_ref[rs, :] = _bdot(act, wd_sc[s]).astype(BF16)
        if nparts < per_blk:
            o_ref[nparts * BM_SEG:, :] = jnp.zeros((BM_EXPERT - nparts * BM_SEG, o_ref.shape[1]), BF16)

    nlive = hreal_ref[b * per_blk]
    for part in range(1, per_blk):
        nlive = nlive + hreal_ref[b * per_blk + part]

    @pl.when(nlive == 0)
    def _():
        o_ref[...] = jnp.zeros_like(o_ref)

    for n in range(1, per_blk + 1):
        @pl.when(nlive == n)
        def _():
            run_parts(n)


def _experts(meta, xs, layer, w_gate, w_up, w_down):
    r, d = xs.shape
    ed = w_gate.shape[-1]
    bm = BM_EXPERT
    nslot = BM_EXPERT // BM_SEG
    omap = lambda b, bx, *_: (b, 0)
    xmap = lambda b, bx, *_: (bx[b], 0)
    hbm = pl.BlockSpec(memory_space=pl.ANY)
    return pl.pallas_call(
        functools.partial(_expert_kernel, layer=layer),
        grid_spec=pltpu.PrefetchScalarGridSpec(
            num_scalar_prefetch=6, grid=(r // bm,),
            in_specs=[pl.BlockSpec((bm, d), xmap), hbm, hbm, hbm],
            out_specs=pl.BlockSpec((bm, d), omap),
            scratch_shapes=[pltpu.VMEM((d, ed), F32), pltpu.VMEM((d, ed), F32), pltpu.VMEM((ed, d), F32),
                            pltpu.VMEM((nslot, d, ed), BF16), pltpu.VMEM((nslot, d, ed), BF16),
                            pltpu.VMEM((nslot, ed, d), BF16), pltpu.SemaphoreType.DMA((3,))]),
        out_shape=jax.ShapeDtypeStruct((r, d), BF16),
        compiler_params=_cparams(("arbitrary",)),
    )(*meta, xs, w_gate, w_up, w_down)


def _combine_kernel(pe_ref, po_ref, goff_ref, used_ref, key_ref, wkey_ref, ys_ref, base_ref, g_ref,
                    fn_ref, *rest, final, with_kv):
    if with_kv:
        kv_in, o_ref, kv_out, (ybuf, q_sc, sem) = rest[:5], rest[5], rest[6:9], rest[9:]
    else:
        o_ref, ybuf, q_sc, sem = rest
    i = pl.program_id(0)
    tm = base_ref.shape[0]
    ch = SORT_CHUNK
    ppc = ch // RUN_ALIGN

    @pl.when(i == 0)
    def _():
        ybuf[...] = jnp.zeros_like(ybuf)

    copy, wait_all = _piece_copies(pe_ref, po_ref, goff_ref, used_ref, i, ybuf, ys_ref, sem, False)

    for ci in range(ybuf.shape[0] // ch):
        cs = slice(ci * ch, (ci + 1) * ch)

        @pl.when(ci * ch < used_ref[i])
        def _():
            for jj in range(ppc):
                copy(ci * ppc + jj).start()
            rows = _piece_rows(pe_ref, po_ref, key_ref, i * PIECES_PAD + ci * ppc, ppc, wkey_ref)
            q_sc[:, cs] = jnp.concatenate(rows, axis=0).T.astype(BF16)

        @pl.when(ci * ch >= used_ref[i])
        def _():
            q_sc[:, cs] = jnp.zeros((tm, ch), BF16)

    wait_all()
    out = base_ref[...] + g_ref[0] * _bdot(q_sc[...], ybuf[...])
    if final:
        out = _rms(out, fn_ref[...])
    o_ref[...] = out
    if with_kv:
        _kv_body(out, *kv_in, *kv_out)


def _combine(meta, pe, po, key, wkey, ys, base, gt, fnorm, seq, final, kv=None):
    t, d = base.shape
    tm = TM_SORT
    nt = seq // tm
    bsz = t // seq
    vecb = pl.BlockSpec((1, 1, d), lambda i, *_: (i // nt, 0, 0))
    full2 = lambda shape: pl.BlockSpec(shape, lambda i, *_: (0, 0))
    in_specs = [pl.BlockSpec((N_EXPERTS, tm), lambda i, *_: (0, i)),
                pl.BlockSpec((N_EXPERTS, tm), lambda i, *_: (0, i)),
                pl.BlockSpec(memory_space=pl.ANY),
                pl.BlockSpec((tm, d), lambda i, *_: (i, 0)), vecb, full2((1, d))]
    args = [key, wkey, ys, base, gt, fnorm.reshape(1, d)]
    out_specs = [pl.BlockSpec((tm, d), lambda i, *_: (i, 0))]
    out_shape = [jax.ShapeDtypeStruct((t, d), F32)]
    if kv is not None:
        ksh, ksc, knw, w_k, w_v = kv
        nb, npair, per = seq // B_BLOCK, d // LANES, tm // B_BLOCK
        in_specs += [vecb, vecb, full2((1, d)), full2((d, d)), full2((d, d))]
        args += [ksh, ksc, knw.reshape(1, d), w_k.astype(BF16), w_v.T.astype(BF16)]
        out_specs += [pl.BlockSpec((1, npair, per, B_BLOCK, LANES), lambda i, *_: (i // nt, 0, i % nt, 0, 0)),
                      pl.BlockSpec((1, npair, per, LANES, B_BLOCK), lambda i, *_: (i // nt, 0, i % nt, 0, 0)),
                      pl.BlockSpec((1, per, 1, d), lambda i, *_: (i // nt, i % nt, 0, 0))]
        out_shape += [jax.ShapeDtypeStruct((bsz, npair, nb, B_BLOCK, LANES), BF16),
                      jax.ShapeDtypeStruct((bsz, npair, nb, LANES, B_BLOCK), BF16),
                      jax.ShapeDtypeStruct((bsz, nb, 1, d), F32)]
    res = pl.pallas_call(
        functools.partial(_combine_kernel, final=final, with_kv=kv is not None),
        grid_spec=pltpu.PrefetchScalarGridSpec(
            num_scalar_prefetch=4, grid=(t // tm,),
            in_specs=in_specs, out_specs=out_specs,
            scratch_shapes=[pltpu.VMEM((ROWS_TILE, d), BF16), pltpu.VMEM((tm, ROWS_TILE), BF16),
                            pltpu.SemaphoreType.DMA(())]),
        out_shape=out_shape,
        compiler_params=_cparams(("arbitrary",)),
    )(pe, po, meta[0], meta[1], *args)
    return res if kv is not None else res[0]


def _moe(x, mods, nw, w_router, e_bias, layer, w_gate, w_up, w_down, wsg, wsu, wsd, fnorm, final, mixer, kv=None):
    bsz, s, d = x.shape
    t = bsz * s
    gt = mods[2]
    n_rows_max = _rows_max(t)
    hb, base, key, wkey, cnt, pe, po = _router(x, mods, nw, w_router, e_bias, wsg, wsu, wsd, mixer)
    pe = pe.reshape(-1)
    po = po.reshape(-1)
    layout_meta, block_meta = _sort_meta(cnt, n_rows_max)
    xs = _dispatch(layout_meta, pe, po, key, hb, n_rows_max)
    ys = _experts(block_meta, xs, layer, w_gate, w_up, w_down)
    out = _combine(layout_meta, pe, po, key, wkey, ys, base, gt, fnorm, s, final, kv)
    if kv is not None:
        return (out[0].reshape(bsz, s, d),) + tuple(out[1:])
    return out.reshape(bsz, s, d)


def _kv_body(x, sh_ref, sc_ref, nw_ref, wk_ref, wvt_ref, k_ref, vt_ref, km_ref):
    hb = (_rms(x, nw_ref[...]) * (1.0 + sc_ref[0]) + sh_ref[0]).astype(BF16)
    k = _bdot(hb, wk_ref[...])
    vt = _bdot_nt(wvt_ref[...], hb)
    for blk in range(x.shape[0] // B_BLOCK):
        rs = slice(blk * B_BLOCK, (blk + 1) * B_BLOCK)
        for p in range(k.shape[1] // LANES):
            k_ref[0, p, blk] = k[rs, p * LANES:(p + 1) * LANES].astype(BF16)
            vt_ref[0, p, blk] = vt[p * LANES:(p + 1) * LANES, rs].astype(BF16)
        km_ref[0, blk] = jnp.mean(k[rs], axis=0, keepdims=True)


def _kv_kernel(x_ref, *refs):
    _kv_body(x_ref[0], *refs)


def _kv(x, sh, sc, nw, w_k, w_v):
    bsz, s, d = x.shape
    nb = s // B_BLOCK
    npair = d // LANES
    vec = pl.BlockSpec((1, 1, d), lambda b, j: (b, 0, 0))
    full2 = lambda shape: pl.BlockSpec(shape, lambda b, j: (0, 0))
    return pl.pallas_call(
        _kv_kernel,
        grid=(bsz, nb),
        in_specs=[pl.BlockSpec((1, B_BLOCK, d), lambda b, j: (b, j, 0)), vec, vec,
                  full2((1, d)), full2((d, d)), full2((d, d))],
        out_specs=[pl.BlockSpec((1, npair, 1, B_BLOCK, LANES), lambda b, j: (b, 0, j, 0, 0)),
                   pl.BlockSpec((1, npair, 1, LANES, B_BLOCK), lambda b, j: (b, 0, j, 0, 0)),
                   pl.BlockSpec((1, 1, 1, d), lambda b, j: (b, j, 0, 0))],
        out_shape=[jax.ShapeDtypeStruct((bsz, npair, nb, B_BLOCK, LANES), BF16),
                   jax.ShapeDtypeStruct((bsz, npair, nb, LANES, B_BLOCK), BF16),
                   jax.ShapeDtypeStruct((bsz, nb, 1, d), F32)],
        compiler_params=_cparams(("arbitrary", "arbitrary")),
    )(x, sh, sc, nw.reshape(1, d), w_k.astype(BF16), w_v.T.astype(BF16))


def _attn_body(x, qb, sh_ref, sc_ref, g_ref, nw_ref, wqt_ref, wo_ref, k_ref, vt_ref, km_ref,
               qt_sc, qs_sc, acc_sc, sel_sc, m_sc, l_sc, *, nb, n_sel):
    bq = x.shape[0]
    npair = qt_sc.shape[0]
    nbp = km_ref.shape[2]
    hd = LANES // 2
    scale = float(hd) ** -0.5 * 1.4426950408889634
    h = _rms(x, nw_ref[...]) * (1.0 + sc_ref[0]) + sh_ref[0]
    qt = _bdot_nt(wqt_ref[...], h.astype(BF16))
    for p in range(npair):
        qt_sc[p] = qt[p * LANES:(p + 1) * LANES, :]

    subn = lax.broadcasted_iota(I32, (nbp, bq), 0)
    past = subn < qb
    krow = lax.broadcasted_iota(I32, (B_BLOCK, bq), 0)
    qcol = lax.broadcasted_iota(I32, (B_BLOCK, bq), 1)
    causal = krow <= qcol
    rowh = lax.broadcasted_iota(I32, (LANES, 1), 0)

    grp = ATTN_PAIR_UNROLL
    heads = [(u, e) for u in range(grp) for e in range(2)]
    ones_rows = jnp.ones((SUM_ROWS, B_BLOCK), BF16)

    def vsum(vt2, e):
        return jnp.concatenate([vt2[e * hd:(e + 1) * hd, :], ones_rows], axis=0)

    def own_body(gi, carry):
        ps = [gi * grp + u for u in range(grp)]
        q2ts = [qt_sc[p] for p in ps]
        kms = [km_ref[0, p] for p in ps]
        kown = [k_ref[0, p, qb] for p in ps]
        vown = [vt_ref[0, p, qb] for p in ps]
        qets = [jnp.where((rowh >= hd) if e == 1 else (rowh < hd), q2ts[u], 0.0) for u, e in heads]
        qsts = [(q * scale).astype(BF16) for q in qets]
        ss = [jnp.where(causal, _bdot(kown[u], qsts[i]), NEG_INF) for i, (u, e) in enumerate(heads)]
        gates = [_dot3(kms[u], qets[i]) for i, (u, e) in enumerate(heads)]
        ms = [jnp.max(s, axis=0, keepdims=True) for s in ss]
        pes = [jnp.exp2(s - m) for s, m in zip(ss, ms)]
        pvs = [_bdot(vsum(vown[u], e), pes[i].astype(BF16)) for i, (u, e) in enumerate(heads)]
        accs = [pv[:hd] for pv in pvs]
        ls = [pv[hd:hd + 1] for pv in pvs]
        sels = []
        for gate in gates:
            selt = jnp.zeros((nbp, bq), F32)
            for n in range(nb):
                gn = gate[n:n + 1, :]
                beats = jnp.logical_or(gate > gn, jnp.logical_and(gate == gn, subn < n))
                beats = jnp.logical_and(beats, past)
                cnt = jnp.sum(beats.astype(F32), axis=0, keepdims=True)
                selt = jnp.where(subn == n, (cnt < n_sel).astype(F32), selt)
            sels.append(selt)
        for i, (u, e) in enumerate(heads):
            p = ps[u]
            sel_sc[p, e] = sels[i]
            qs_sc[p, e] = qsts[i]
            m_sc[p, e] = ms[i]
            l_sc[p, e] = ls[i]
            acc_sc[p, e * hd:(e + 1) * hd, :] = accs[i]
        return carry

    lax.fori_loop(0, npair // grp, own_body, 0)

    grp = ATTN_PAIR_UNROLL_PAST
    heads = [(u, e) for u in range(grp) for e in range(2)]

    def kb_body(kb, carry):
        def group_body(gi, c2):
            ps = [gi * grp + u for u in range(grp)]
            kbl = [k_ref[0, p, kb] for p in ps]
            vbl = [vt_ref[0, p, kb] for p in ps]
            qsts = [qs_sc[ps[u], e] for u, e in heads]
            rows = [sel_sc[ps[u], e, pl.ds(kb, 1), :] for u, e in heads]
            m_old = [m_sc[ps[u], e] for u, e in heads]
            l_old = [l_sc[ps[u], e] for u, e in heads]
            a_old = [acc_sc[ps[u], e * hd:(e + 1) * hd, :] for u, e in heads]
            ss = [jnp.where(rows[i] > 0.5, _bdot(kbl[u], qsts[i]), NEG_INF)
                  for i, (u, e) in enumerate(heads)]
            m_new = [jnp.maximum(m, jnp.max(s, axis=0, keepdims=True)) for m, s in zip(m_old, ss)]
            alphas = [jnp.exp2(m - mn) for m, mn in zip(m_old, m_new)]
            pes = [jnp.exp2(s - mn) for s, mn in zip(ss, m_new)]
            pvs = [_bdot(vsum(vbl[u], e), pes[i].astype(BF16)) for i, (u, e) in enumerate(heads)]
            l_new = [a * l + pv[hd:hd + 1] for a, l, pv in zip(alphas, l_old, pvs)]
            a_new = [a * ao + pv[:hd] for a, ao, pv in zip(alphas, a_old, pvs)]
            for i, (u, e) in enumerate(heads):
                p = ps[u]
                m_sc[p, e] = m_new[i]
                l_sc[p, e] = l_new[i]
                acc_sc[p, e * hd:(e + 1) * hd, :] = a_new[i]
            return c2

        lax.fori_loop(0, npair // grp, group_body, 0)
        return carry

    lax.fori_loop(0, qb, kb_body, 0)

    parts = []
    for p in range(npair):
        for e in range(2):
            parts.append(acc_sc[p, e * hd:(e + 1) * hd, :] / l_sc[p, e])
    ot = jnp.concatenate(parts, axis=0)
    return x + g_ref[0] * _bdot(ot.T.astype(BF16), wo_ref[...])


def _attn_mixer(mods, nw, w_q, w_o, k5, vt5, km2):
    sh, sc, gt = mods
    d = w_q.shape[0]
    npair, nb = k5.shape[1], k5.shape[2]
    nbp = km2.shape[2]
    n_sel = min(B_TOPK, nb - 1)
    per = TM_SORT // B_BLOCK
    vec = pl.BlockSpec((1, 1, d), lambda b, j: (b, 0, 0))
    full2 = lambda shape: pl.BlockSpec(shape, lambda b, j: (0, 0))
    specs = [vec, vec, vec, full2((1, d)), full2((d, d)), full2((d, d)),
             pl.BlockSpec((1, npair, nb, B_BLOCK, LANES), lambda b, j: (b, 0, 0, 0, 0)),
             pl.BlockSpec((1, npair, nb, LANES, B_BLOCK), lambda b, j: (b, 0, 0, 0, 0)),
             pl.BlockSpec((1, npair, nbp, LANES), lambda b, j: (b, 0, 0, 0))]
    args = [sh, sc, gt, nw.reshape(1, d), w_q.T.astype(BF16), w_o.astype(BF16), k5, vt5, km2]
    scratch = [pltpu.VMEM((npair, LANES, B_BLOCK), F32),
               pltpu.VMEM((npair, 2, LANES, B_BLOCK), BF16),
               pltpu.VMEM((npair, LANES, B_BLOCK), F32),
               pltpu.VMEM((npair, 2, nbp, B_BLOCK), F32),
               pltpu.VMEM((npair, 2, 1, B_BLOCK), F32),
               pltpu.VMEM((npair, 2, 1, B_BLOCK), F32)]

    def body(x_ref, refs, scr):
        j = pl.program_id(1)
        outs = [_attn_body(x_ref[0, h * B_BLOCK:(h + 1) * B_BLOCK, :], j * per + h, *refs, *scr,
                           nb=nb, n_sel=n_sel) for h in range(per)]
        return jnp.concatenate(outs, axis=0)

    return body, specs, args, scratch


def kernel(x, c, ada_w, ada_b, norm_mix, norm_ffn, a_w_in, a_b_in, a_ln_g, a_ln_b, a_w_s, a_b_s,
           a_w_out, kv_norm, kv_ada_w, kv_ada_b, kv_w_k, kv_w_v, b_w_q, b_w_o, moe_router, moe_bias,
           moe_w_gate, moe_w_up, moe_w_down, sh_w_gate, sh_w_up, sh_w_down, final_norm):
    bsz, s, d = x.shape
    depth = ada_w.shape[0]
    n_a = a_w_in.shape[0]
    assert s % B_BLOCK == 0 and s % TM_SORT == 0 and TM_SORT % A_CHUNK == 0 and d % LANES == 0
    nb = s // B_BLOCK
    npair = d // LANES
    nbp = -(-nb // 8) * 8

    def split(m, n):
        return [m[:, i * d:(i + 1) * d].reshape(bsz, 1, d) for i in range(n)]

    def pair_major(km):
        km = km.reshape(bsz, nb, npair, LANES).transpose(0, 2, 1, 3)
        return jnp.pad(km, ((0, 0), (0, 0), (0, nbp - nb), (0, 0)))

    layer_mods = _ada(c, ada_w, ada_b)
    k5 = vt5 = km2 = None
    for i in range(depth):
        sh1, sc1, g1, sh2, sc2, g2 = split(layer_mods[i], 6)
        if i < n_a:
            mixer = _gmlp_mixer((sh1, sc1, g1), norm_mix[i], a_w_in[i], a_b_in[i], a_ln_g[i], a_ln_b[i],
                                a_w_s[i], a_b_s[i], a_w_out[i])
        else:
            if k5 is None:
                ksh, ksc = split(_ada(c, kv_ada_w[None], kv_ada_b[None])[0], 2)
                k5, vt5, km = _kv(x, ksh, ksc, kv_norm, kv_w_k, kv_w_v)
                km2 = pair_major(km)
            j = i - n_a
            mixer = _attn_mixer((sh1, sc1, g1), norm_mix[i], b_w_q[j], b_w_o[j], k5, vt5, km2)
        kv = None
        if i == n_a - 1 and i + 1 < depth:
            ksh, ksc = split(_ada(c, kv_ada_w[None], kv_ada_b[None])[0], 2)
            kv = (ksh, ksc, kv_norm, kv_w_k, kv_w_v)
        res = _moe(x, (sh2, sc2, g2), norm_ffn[i], moe_router[i], moe_bias[i], i, moe_w_gate,
                   moe_w_up, moe_w_down, sh_w_gate[i], sh_w_up[i], sh_w_down[i],
                   final_norm, i == depth - 1, mixer, kv)
        if kv is not None:
            x, k5, vt5, km = res
            km2 = pair_major(km)
        else:
            x = res
    return x
```

```python
import functools

import jax
import jax.numpy as jnp
from jax import lax
from jax.experimental import pallas as pl
from jax.experimental.pallas import tpu as pltpu

F32 = jnp.float32
BF16 = jnp.bfloat16
I32 = jnp.int32

RMS_EPS = 1e-6
LN_EPS = 1e-5
NEG_INF = -1e30

A_CHUNK = 128
A_GROUPS = 8
B_HEADS = 16
B_BLOCK = 256
B_TOPK = 3
N_EXPERTS = 64
TOP_K = 8
N_GROUPS = 8
TOPK_GROUPS = 4
ROUTED_SCALE = 2.5

LANES = 128
VMEM_LIMIT = 56 * 1024 * 1024

TM_SORT = 512
RUN_ALIGN = 16
SORT_CHUNK = 256
ROWS_TILE_USED = TM_SORT * TOP_K + N_EXPERTS * (RUN_ALIGN - 1)
ROWS_TILE = -(-ROWS_TILE_USED // SORT_CHUNK) * SORT_CHUNK
PIECES_TILE = ROWS_TILE // RUN_ALIGN
PIECES_PAD = -(-PIECES_TILE // LANES) * LANES
BM_EXPERT = 2048
BM_SEG = 256
BM_CHAIN = 256
ATTN_PAIR_UNROLL = 4
ATTN_PAIR_UNROLL_PAST = 8
SUM_ROWS = 16


def _cparams(sem):
    return pltpu.CompilerParams(dimension_semantics=sem, vmem_limit_bytes=VMEM_LIMIT)


def _sigmoid(x):
    return 1.0 / (1.0 + jnp.exp(-x))


def _silu(x):
    return x * _sigmoid(x)


def _gelu_tanh(x):
    hx = 0.5 * x
    return hx + hx * jnp.tanh(x * (0.7978845608028654 + 0.035677408136300125 * (x * x)))


def _rms(x, g):
    return x * lax.rsqrt(jnp.mean(x * x, axis=-1, keepdims=True) + RMS_EPS) * g


def _bdot(a, b):
    return jnp.dot(a, b, preferred_element_type=F32)


def _bdot_nt(a, b):
    return lax.dot_general(a, b, (((1,), (1,)), ((), ())), preferred_element_type=F32)


def _split(a):
    hi = a.astype(BF16)
    lo = (a - hi.astype(F32)).astype(BF16)
    return hi, lo


def _dot3_nt(a, b):
    ah, al = _split(a)
    bh, bl = _split(b)
    return _bdot_nt(ah, bh) + (_bdot_nt(ah, bl) + _bdot_nt(al, bh))


def _dot3(a, b):
    ah, al = _split(a)
    bh, bl = _split(b)
    return _bdot(ah, bh) + (_bdot(ah, bl) + _bdot(al, bh))


def _ada_kernel(c_ref, w_ref, b_ref, o_ref):
    a = _silu(c_ref[...]).astype(BF16)
    o_ref[0] = _bdot(a, w_ref[0].astype(BF16)) + b_ref[0]


def _ada(c, w, b):
    bsz, d = c.shape
    nl, _, n = w.shape
    tn = 1024
    return pl.pallas_call(
        _ada_kernel,
        grid=(nl, n // tn),
        in_specs=[pl.BlockSpec((bsz, d), lambda l, j: (0, 0)),
                  pl.BlockSpec((1, d, tn), lambda l, j: (l, 0, j)),
                  pl.BlockSpec((1, 1, tn), lambda l, j: (l, 0, j))],
        out_specs=pl.BlockSpec((1, bsz, tn), lambda l, j: (l, 0, j)),
        out_shape=jax.ShapeDtypeStruct((nl, bsz, n), F32),
        compiler_params=_cparams(("arbitrary", "arbitrary")),
    )(c, w, b.reshape(nl, 1, n))


def _gmlp_body(x, sh_ref, sc_ref, g_ref, nw_ref, win_ref, bin_ref, lng_ref, lnb_ref,
               ws_ref, bst_ref, wout_ref, y_sc):
    tm = x.shape[0]
    h = _rms(x, nw_ref[...]) * (1.0 + sc_ref[0]) + sh_ref[0]
    z = _gelu_tanh(_bdot(h.astype(BF16), win_ref[...]) + bin_ref[...])
    aw = z.shape[1] // 2
    gd = aw // A_GROUPS
    u = z[:, :aw]
    v = z[:, aw:]
    mu = jnp.mean(v, axis=-1, keepdims=True)
    dv = v - mu
    var = jnp.mean(dv * dv, axis=-1, keepdims=True)
    vn = (dv * lax.rsqrt(var + LN_EPS) * lng_ref[...] + lnb_ref[...]).astype(BF16)
    row = lax.broadcasted_iota(I32, (A_CHUNK, A_CHUNK), 0)
    col = lax.broadcasted_iota(I32, (A_CHUNK, A_CHUNK), 1)
    causal = col <= row
    for g in range(A_GROUPS):
        wg = jnp.where(causal, ws_ref[g], 0.0).astype(BF16)
        bcol = bst_ref[:, g:g + 1]
        for ci in range(tm // A_CHUNK):
            rs = slice(ci * A_CHUNK, (ci + 1) * A_CHUNK)
            cs = slice(g * gd, (g + 1) * gd)
            sv = _bdot(wg, vn[rs, cs]) + bcol
            y_sc[rs, cs] = (u[rs, cs] * sv).astype(BF16)
    return x + g_ref[0] * _bdot(y_sc[...], wout_ref[...])


def _route(x, sh_ref, sc_ref, g_ref, nw_ref, wrt_ref, bias_ref, wsg_ref, wsu_ref, wsd_ref,
           h_ref, base_ref, key_ref, wkey_ref, cnt_ref, pe_ref, po_ref):
    tm = x.shape[0]
    h = _rms(x, nw_ref[...]) * (1.0 + sc_ref[0]) + sh_ref[0]
    hb = h.astype(BF16)
    h_ref[...] = hb
    act = (_silu(_bdot(hb, wsg_ref[...])) * _bdot(hb, wsu_ref[...])).astype(BF16)
    base_ref[...] = x + g_ref[0] * _bdot(act, wsd_ref[...])

    scores = _sigmoid(_dot3_nt(wrt_ref[...], h))
    choice = scores + bias_ref[...]
    gsz = N_EXPERTS // N_GROUPS
    sub = lax.broadcasted_iota(I32, (gsz, tm), 0)
    blocks = [choice[g * gsz:(g + 1) * gsz] for g in range(N_GROUPS)]
    gscore = []
    for blk in blocks:
        m1 = jnp.max(blk, axis=0, keepdims=True)
        i1 = jnp.min(jnp.where(blk == m1, sub, gsz), axis=0, keepdims=True)
        m2 = jnp.max(jnp.where(sub == i1, -jnp.inf, blk), axis=0, keepdims=True)
        gscore.append(m1 + m2)
    masked = []
    for g in range(N_GROUPS):
        beats = jnp.zeros((1, tm), F32)
        for m in range(N_GROUPS):
            if m == g:
                continue
            b = gscore[m] > gscore[g]
            if m < g:
                b = jnp.logical_or(b, gscore[m] == gscore[g])
            beats = beats + b.astype(F32)
        masked.append(jnp.where(beats < TOPK_GROUPS, blocks[g], NEG_INF))
    cur = jnp.concatenate(masked, axis=0)
    eio = lax.broadcasted_iota(I32, (N_EXPERTS, tm), 0)
    chosen = jnp.zeros((N_EXPERTS, tm), jnp.bool_)
    wsum = jnp.zeros((1, tm), F32)
    for _ in range(TOP_K):
        m = jnp.max(cur, axis=0, keepdims=True)
        idx = jnp.min(jnp.where(cur == m, eio, N_EXPERTS), axis=0, keepdims=True)
        sel = eio == idx
        chosen = jnp.logical_or(chosen, sel)
        wsum = wsum + jnp.sum(jnp.where(sel, scores, 0.0), axis=0, keepdims=True)
        cur = jnp.where(sel, -jnp.inf, cur)
    wkey_ref[...] = jnp.where(chosen, scores / wsum * ROUTED_SCALE, 0.0)

    onehot = chosen.astype(BF16)
    r_i = lax.broadcasted_iota(I32, (tm, tm), 0)
    c_i = lax.broadcasted_iota(I32, (tm, tm), 1)
    before = (r_i < c_i).astype(BF16)
    prior = _bdot(onehot, before)
    key_ref[...] = jnp.where(chosen, prior, -1.0).astype(I32)
    cnt = jnp.sum(chosen.astype(F32), axis=1, keepdims=True)
    cnt_ref[0] = cnt

    run_p = jnp.floor((cnt + (RUN_ALIGN - 1)) / RUN_ALIGN)
    ppc = SORT_CHUNK // RUN_ALIGN
    tot = jnp.sum(run_p, axis=0, keepdims=True)
    fill = jnp.ceil(tot / ppc) * ppc - tot
    run_p = run_p + jnp.where(lax.broadcasted_iota(I32, (N_EXPERTS, 1), 0) == N_EXPERTS - 1, fill, 0.0)
    e_r = lax.broadcasted_iota(I32, (N_EXPERTS, N_EXPERTS), 0)
    e_c = lax.broadcasted_iota(I32, (N_EXPERTS, N_EXPERTS), 1)
    incl = (e_c <= e_r).astype(BF16)
    lend = _bdot(incl, jnp.broadcast_to(run_p, (N_EXPERTS, LANES)).astype(BF16))[:, 0:1]
    loff = lend - run_p
    pj = lax.broadcasted_iota(I32, (N_EXPERTS, PIECES_PAD), 1).astype(F32)
    er = lax.broadcasted_iota(I32, (N_EXPERTS, PIECES_PAD), 0).astype(F32)
    pe = jnp.minimum(jnp.sum((lend <= pj).astype(F32), axis=0, keepdims=True), N_EXPERTS - 1.0)
    lo = jnp.sum(jnp.where(er == pe, loff, 0.0), axis=0, keepdims=True)
    pe_ref[0] = pe.astype(I32)
    po_ref[0] = ((pj[0:1, :] - lo) * RUN_ALIGN).astype(I32)


N_ROUTE_REFS = 16


def _router_kernel(x_ref, *refs):
    _route(x_ref[0], *refs)


def _mixer_router_kernel(x_ref, *refs, body, n_mix):
    route_refs = refs[n_mix:n_mix + N_ROUTE_REFS]
    x1 = body(x_ref, refs[:n_mix], refs[n_mix + N_ROUTE_REFS:])
    _route(x1, *route_refs)


def _gmlp_mixer(mods, nw, w_in, b_in, ln_g, ln_b, w_s, b_s, w_out):
    sh, sc, gt = mods
    d, n_in = w_in.shape
    aw = n_in // 2
    vec = pl.BlockSpec((1, 1, d), lambda b, j: (b, 0, 0))
    full2 = lambda shape: pl.BlockSpec(shape, lambda b, j: (0, 0))
    specs = [vec, vec, vec, full2((1, d)), full2((d, n_in)), full2((1, n_in)), full2((1, aw)),
             full2((1, aw)), pl.BlockSpec((A_GROUPS, A_CHUNK, A_CHUNK), lambda b, j: (0, 0, 0)),
             full2((A_CHUNK, A_GROUPS)), full2((aw, d))]
    args = [sh, sc, gt, nw.reshape(1, d), w_in.astype(BF16), b_in.reshape(1, n_in),
            ln_g.reshape(1, aw), ln_b.reshape(1, aw), w_s, b_s.T, w_out.astype(BF16)]

    def body(x_ref, refs, scratch):
        return _gmlp_body(x_ref[0], *refs, *scratch)

    return body, specs, args, [pltpu.VMEM((TM_SORT, aw), BF16)]


def _router(x, mods, nw, w_router, e_bias, wsg, wsu, wsd, mixer=None):
    bsz, s, d = x.shape
    t = bsz * s
    tm = TM_SORT
    nt = s // tm
    sd = wsg.shape[1]
    vec = pl.BlockSpec((1, 1, d), lambda b, j: (b, 0, 0))
    full2 = lambda shape: pl.BlockSpec(shape, lambda b, j: (0, 0))
    tok = pl.BlockSpec((tm, d), lambda b, j: (b * nt + j, 0))
    etok = pl.BlockSpec((N_EXPERTS, tm), lambda b, j: (0, b * nt + j))
    ptab = pl.BlockSpec((1, 1, PIECES_PAD), lambda b, j: (b * nt + j, 0, 0))
    sh, sc, gt = mods
    in_specs = [vec, vec, vec, full2((1, d)), full2((N_EXPERTS, d)), full2((N_EXPERTS, 1)),
                full2((d, sd)), full2((d, sd)), full2((sd, d))]
    args = [sh, sc, gt, nw.reshape(1, d), w_router.T, e_bias.reshape(N_EXPERTS, 1),
            wsg.astype(BF16), wsu.astype(BF16), wsd.astype(BF16)]
    kern, scratch = _router_kernel, []
    if mixer is not None:
        body, mspecs, margs, scratch = mixer
        kern = functools.partial(_mixer_router_kernel, body=body, n_mix=len(margs))
        in_specs = mspecs + in_specs
        args = margs + args
    return pl.pallas_call(
        kern,
        grid=(bsz, nt),
        in_specs=[pl.BlockSpec((1, tm, d), lambda b, j: (b, j, 0))] + in_specs,
        out_specs=[tok, tok, etok, etok,
                   pl.BlockSpec((1, N_EXPERTS, 1), lambda b, j: (b * nt + j, 0, 0)), ptab, ptab],
        out_shape=[jax.ShapeDtypeStruct((t, d), BF16), jax.ShapeDtypeStruct((t, d), F32),
                   jax.ShapeDtypeStruct((N_EXPERTS, t), I32), jax.ShapeDtypeStruct((N_EXPERTS, t), F32),
                   jax.ShapeDtypeStruct((t // tm, N_EXPERTS, 1), F32),
                   jax.ShapeDtypeStruct((t // tm, 1, PIECES_PAD), I32),
                   jax.ShapeDtypeStruct((t // tm, 1, PIECES_PAD), I32)],
        scratch_shapes=scratch,
        compiler_params=_cparams(("arbitrary", "arbitrary")),
    )(x, *args)


def _rows_max(t):
    rows = (t // TM_SORT) * ROWS_TILE + N_EXPERTS * (BM_SEG - RUN_ALIGN)
    return -(-rows // BM_EXPERT) * BM_EXPERT


def _sort_meta(cnt, n_rows_max):
    nts = cnt.shape[0]
    c = cnt.reshape(nts, N_EXPERTS).astype(I32)
    run = (c + (RUN_ALIGN - 1)) // RUN_ALIGN * RUN_ALIGN
    fill = (-jnp.sum(run, axis=1, keepdims=True)) % SORT_CHUNK
    run = jnp.concatenate([run[:, :-1], run[:, -1:] + fill], axis=1)
    used = jnp.sum(run, axis=1)
    per_e = jnp.sum(run, axis=0)
    seg = (per_e + (BM_SEG - 1)) // BM_SEG * BM_SEG
    ends = jnp.cumsum(seg)
    starts = ends - seg
    goff = starts[None, :] + jnp.cumsum(run, axis=0) - run
    pad_start = starts + per_e
    pad_cnt = (seg - per_e) // RUN_ALIGN

    nh = n_rows_max // BM_SEG
    hidx = jnp.arange(nh, dtype=I32)
    he = jnp.sum((ends[None, :] <= (hidx * BM_SEG)[:, None]).astype(I32), axis=1)
    real = he < N_EXPERTS
    he = jnp.minimum(he, N_EXPERTS - 1).astype(I32)
    prev = jnp.concatenate([jnp.full((1,), -1, I32), he[:-1]])
    hnew = jnp.logical_and(real, he != prev)
    per_blk = BM_EXPERT // BM_SEG
    hslot = (jnp.cumsum(hnew.astype(I32)) - 1) % per_blk
    later = jnp.logical_and(hidx[None, :] > hidx[:, None], hnew[None, :])
    nxt = jnp.min(jnp.where(later, hidx[None, :], nh), axis=1)
    hfetch = jnp.where(nxt < nh, he[jnp.minimum(nxt, nh - 1)], -1)
    bx = jnp.where(real[::per_blk], jnp.arange(nh // per_blk, dtype=I32), 0)
    i32 = lambda a: a.reshape(-1).astype(I32)
    return ((i32(goff), i32(used), i32(pad_start), i32(pad_cnt)),
            (bx, he, i32(hnew), i32(real), i32(hslot), i32(hfetch)))


def _piece_copies(pe_ref, po_ref, goff_ref, used_ref, tile, local, remote, sem, to_remote):
    def copy(j):
        idx = tile * PIECES_PAD + j
        l0 = pl.multiple_of(j * RUN_ALIGN, RUN_ALIGN)
        g0 = pl.multiple_of(goff_ref[tile * N_EXPERTS + pe_ref[idx]] + po_ref[idx], RUN_ALIGN)
        lref = local.at[pl.ds(l0, RUN_ALIGN)]
        gref = remote.at[pl.ds(g0, RUN_ALIGN)]
        return pltpu.make_async_copy(lref, gref, sem) if to_remote else pltpu.make_async_copy(gref, lref, sem)

    def wait_all():
        def body(j, carry):
            copy(j).wait()
            return carry

        lax.fori_loop(0, used_ref[tile] // RUN_ALIGN, body, 0)

    return copy, wait_all


def _pad_copies(pad_start_ref, pad_cnt_ref, tile, ntiles, zeros, remote, sem):
    share = -(-N_EXPERTS // ntiles)

    def apply(act):
        for q in range(share):
            e = tile * share + q
            ec = jnp.minimum(e, N_EXPERTS - 1)
            n = jnp.where(e < N_EXPERTS, pad_cnt_ref[ec], 0)
            base = pad_start_ref[ec]

            def body(j, carry):
                g0 = pl.multiple_of(base + j * RUN_ALIGN, RUN_ALIGN)
                getattr(pltpu.make_async_copy(zeros, remote.at[pl.ds(g0, RUN_ALIGN)], sem), act)()
                return carry

            lax.fori_loop(0, n, body, 0)

    return apply


def _piece_rows(pe_ref, po_ref, key_ref, first_piece, npieces, val_ref=None):
    tm = key_ref.shape[1]
    sub = lax.broadcasted_iota(I32, (RUN_ALIGN, tm), 0)
    out = []
    for jj in range(npieces):
        j = first_piece + jj
        e = pe_ref[j]
        hit = (key_ref[pl.ds(e, 1), :] - po_ref[j]) == sub
        val = 1.0 if val_ref is None else val_ref[pl.ds(e, 1), :]
        out.append(jnp.where(hit, val, 0.0))
    return out


def _dispatch_kernel(pe_ref, po_ref, goff_ref, used_ref, pad_start_ref, pad_cnt_ref,
                     key_ref, h_ref, xs_ref, xbuf, p_sc, zbuf, sem, zsem, *, ntiles):
    i = pl.program_id(0)
    hb = h_ref[...]
    ch = SORT_CHUNK
    ppc = ch // RUN_ALIGN
    zbuf[...] = jnp.zeros_like(zbuf)
    pads = _pad_copies(pad_start_ref, pad_cnt_ref, i, ntiles, zbuf, xs_ref, zsem)
    pads("start")

    copy, wait_all = _piece_copies(pe_ref, po_ref, goff_ref, used_ref, i, xbuf, xs_ref, sem, True)

    def build(ci):
        rows = _piece_rows(pe_ref, po_ref, key_ref, i * PIECES_PAD + ci * ppc, ppc)
        for jj, p in enumerate(rows):
            p_sc[jj * RUN_ALIGN:(jj + 1) * RUN_ALIGN, :] = p.astype(BF16)
        xbuf[pl.ds(pl.multiple_of(ci * ch, ch), ch), :] = _bdot(p_sc[...], hb).astype(BF16)

    def send(ci):
        for jj in range(ppc):
            copy(ci * ppc + jj).start()

    def step(ci, carry):
        send(ci - 1)
        build(ci)
        return carry

    nchunk = used_ref[i] // ch
    build(0)
    lax.fori_loop(1, nchunk, step, 0)
    send(nchunk - 1)
    pads("wait")
    wait_all()


def _dispatch(meta, pe, po, key, hb, n_rows_max):
    t, d = hb.shape
    tm = TM_SORT
    goff, used, pad_start, pad_cnt = meta
    return pl.pallas_call(
        functools.partial(_dispatch_kernel, ntiles=t // tm),
        grid_spec=pltpu.PrefetchScalarGridSpec(
            num_scalar_prefetch=6, grid=(t // tm,),
            in_specs=[pl.BlockSpec((N_EXPERTS, tm), lambda i, *_: (0, i)),
                      pl.BlockSpec((tm, d), lambda i, *_: (i, 0))],
            out_specs=pl.BlockSpec(memory_space=pl.ANY),
            scratch_shapes=[pltpu.VMEM((ROWS_TILE, d), BF16), pltpu.VMEM((SORT_CHUNK, tm), BF16),
                            pltpu.VMEM((RUN_ALIGN, d), BF16),
                            pltpu.SemaphoreType.DMA(()), pltpu.SemaphoreType.DMA(())]),
        out_shape=jax.ShapeDtypeStruct((n_rows_max, d), BF16),
        compiler_params=_cparams(("arbitrary",)),
    )(pe, po, goff, used, pad_start, pad_cnt, key, hb)


def _expert_kernel(bx_ref, he_ref, hnew_ref, hreal_ref, hslot_ref, hfetch_ref, x_ref, wg_hbm, wu_hbm, wd_hbm,
                   o_ref, wg_st, wu_st, wd_st, wg_sc, wu_sc, wd_sc, sem, *, layer):
    b = pl.program_id(0)
    per_blk = BM_EXPERT // BM_SEG
    cpp = BM_SEG // BM_CHAIN

    def fetch(e):
        return [pltpu.make_async_copy(src.at[layer, e], dst, sem.at[i])
                for i, (src, dst) in enumerate(((wg_hbm, wg_st), (wu_hbm, wu_st), (wd_hbm, wd_st)))]

    @pl.when(b == 0)
    def _():
        for cp in fetch(he_ref[0]):
            cp.start()

    for part in range(per_blk):
        h = b * per_blk + part

        @pl.when(hnew_ref[h] == 1)
        def _():
            for cp in fetch(he_ref[h]):
                cp.wait()
            s = hslot_ref[h]
            wg_sc[s] = wg_st[...].astype(BF16)
            wu_sc[s] = wu_st[...].astype(BF16)
            wd_sc[s] = wd_st[...].astype(BF16)

            @pl.when(hfetch_ref[h] >= 0)
            def _():
                for cp in fetch(hfetch_ref[h]):
                    cp.start()

    def run_parts(parts):
        chains = [(slice((p * cpp + c) * BM_CHAIN, (p * cpp + c + 1) * BM_CHAIN), hslot_ref[b * per_blk + p])
                  for p in parts for c in range(cpp)]
        xs = [x_ref[rs, :] for rs, _ in chains]
        gs = [_bdot(x, wg_sc[s]) for x, (_, s) in zip(xs, chains)]
        us = [_bdot(x, wu_sc[s]) for x, (_, s) in zip(xs, chains)]
        acts = [(_silu(g) * u).astype(BF16) for g, u in zip(gs, us)]
        for (rs, s), act in zip(chains, acts):
            o_ref[rs, :] = _bdot(act, wd_sc[s]).astype(BF16)

    nlive = hreal_ref[b * per_blk]
    for part in range(1, per_blk):
        nlive = nlive + hreal_ref[b * per_blk + part]

    @pl.when(nlive == per_blk)
    def _():
        run_parts(range(per_blk))

    @pl.when(nlive < per_blk)
    def _():
        for part in range(per_blk):
            @pl.when(part < nlive)
            def _():
                run_parts([part])

            @pl.when(part >= nlive)
            def _():
                o_ref[part * BM_SEG:(part + 1) * BM_SEG, :] = jnp.zeros((BM_SEG, o_ref.shape[1]), BF16)


def _experts(meta, xs, layer, w_gate, w_up, w_down):
    r, d = xs.shape
    ed = w_gate.shape[-1]
    bm = BM_EXPERT
    nslot = BM_EXPERT // BM_SEG
    omap = lambda b, bx, *_: (b, 0)
    xmap = lambda b, bx, *_: (bx[b], 0)
    hbm = pl.BlockSpec(memory_space=pl.ANY)
    return pl.pallas_call(
        functools.partial(_expert_kernel, layer=layer),
        grid_spec=pltpu.PrefetchScalarGridSpec(
            num_scalar_prefetch=6, grid=(r // bm,),
            in_specs=[pl.BlockSpec((bm, d), xmap), hbm, hbm, hbm],
            out_specs=pl.BlockSpec((bm, d), omap),
            scratch_shapes=[pltpu.VMEM((d, ed), F32), pltpu.VMEM((d, ed), F32), pltpu.VMEM((ed, d), F32),
                            pltpu.VMEM((nslot, d, ed), BF16), pltpu.VMEM((nslot, d, ed), BF16),
                            pltpu.VMEM((nslot, ed, d), BF16), pltpu.SemaphoreType.DMA((3,))]),
        out_shape=jax.ShapeDtypeStruct((r, d), BF16),
        compiler_params=_cparams(("arbitrary",)),
    )(*meta, xs, w_gate, w_up, w_down)


def _combine_kernel(pe_ref, po_ref, goff_ref, used_ref, key_ref, wkey_ref, ys_ref, base_ref, g_ref,
                    fn_ref, *rest, final, with_kv):
    if with_kv:
        kv_in, o_ref, kv_out, (ybuf, q_sc, sem) = rest[:5], rest[5], rest[6:9], rest[9:]
    else:
        o_ref, ybuf, q_sc, sem = rest
    i = pl.program_id(0)
    tm = base_ref.shape[0]
    ch = SORT_CHUNK
    ppc = ch // RUN_ALIGN

    @pl.when(i == 0)
    def _():
        ybuf[...] = jnp.zeros_like(ybuf)

    copy, wait_all = _piece_copies(pe_ref, po_ref, goff_ref, used_ref, i, ybuf, ys_ref, sem, False)

    for ci in range(ybuf.shape[0] // ch):
        cs = slice(ci * ch, (ci + 1) * ch)

        @pl.when(ci * ch < used_ref[i])
        def _():
            for jj in range(ppc):
                copy(ci * ppc + jj).start()
            rows = _piece_rows(pe_ref, po_ref, key_ref, i * PIECES_PAD + ci * ppc, ppc, wkey_ref)
            q_sc[:, cs] = jnp.concatenate(rows, axis=0).T.astype(BF16)

        @pl.when(ci * ch >= used_ref[i])
        def _():
            q_sc[:, cs] = jnp.zeros((tm, ch), BF16)

    wait_all()
    out = base_ref[...] + g_ref[0] * _bdot(q_sc[...], ybuf[...])
    if final:
        out = _rms(out, fn_ref[...])
    o_ref[...] = out
    if with_kv:
        _kv_body(out, *kv_in, *kv_out)


def _combine(meta, pe, po, key, wkey, ys, base, gt, fnorm, seq, final, kv=None):
    t, d = base.shape
    tm = TM_SORT
    nt = seq // tm
    bsz = t // seq
    vecb = pl.BlockSpec((1, 1, d), lambda i, *_: (i // nt, 0, 0))
    full2 = lambda shape: pl.BlockSpec(shape, lambda i, *_: (0, 0))
    in_specs = [pl.BlockSpec((N_EXPERTS, tm), lambda i, *_: (0, i)),
                pl.BlockSpec((N_EXPERTS, tm), lambda i, *_: (0, i)),
                pl.BlockSpec(memory_space=pl.ANY),
                pl.BlockSpec((tm, d), lambda i, *_: (i, 0)), vecb, full2((1, d))]
    args = [key, wkey, ys, base, gt, fnorm.reshape(1, d)]
    out_specs = [pl.BlockSpec((tm, d), lambda i, *_: (i, 0))]
    out_shape = [jax.ShapeDtypeStruct((t, d), F32)]
    if kv is not None:
        ksh, ksc, knw, w_k, w_v = kv
        nb, npair, per = seq // B_BLOCK, d // LANES, tm // B_BLOCK
        in_specs += [vecb, vecb, full2((1, d)), full2((d, d)), full2((d, d))]
        args += [ksh, ksc, knw.reshape(1, d), w_k.astype(BF16), w_v.T.astype(BF16)]
        out_specs += [pl.BlockSpec((1, npair, per, B_BLOCK, LANES), lambda i, *_: (i // nt, 0, i % nt, 0, 0)),
                      pl.BlockSpec((1, npair, per, LANES, B_BLOCK), lambda i, *_: (i // nt, 0, i % nt, 0, 0)),
                      pl.BlockSpec((1, per, 1, d), lambda i, *_: (i // nt, i % nt, 0, 0))]
        out_shape += [jax.ShapeDtypeStruct((bsz, npair, nb, B_BLOCK, LANES), BF16),
                      jax.ShapeDtypeStruct((bsz, npair, nb, LANES, B_BLOCK), BF16),
                      jax.ShapeDtypeStruct((bsz, nb, 1, d), F32)]
    res = pl.pallas_call(
        functools.partial(_combine_kernel, final=final, with_kv=kv is not None),
        grid_spec=pltpu.PrefetchScalarGridSpec(
            num_scalar_prefetch=4, grid=(t // tm,),
            in_specs=in_specs, out_specs=out_specs,
            scratch_shapes=[pltpu.VMEM((ROWS_TILE, d), BF16), pltpu.VMEM((tm, ROWS_TILE), BF16),
                            pltpu.SemaphoreType.DMA(())]),
        out_shape=out_shape,
        compiler_params=_cparams(("arbitrary",)),
    )(pe, po, meta[0], meta[1], *args)
    return res if kv is not None else res[0]


def _moe(x, mods, nw, w_router, e_bias, layer, w_gate, w_up, w_down, wsg, wsu, wsd, fnorm, final, mixer, kv=None):
    bsz, s, d = x.shape
    t = bsz * s
    gt = mods[2]
    n_rows_max = _rows_max(t)
    hb, base, key, wkey, cnt, pe, po = _router(x, mods, nw, w_router, e_bias, wsg, wsu, wsd, mixer)
    pe = pe.reshape(-1)
    po = po.reshape(-1)
    layout_meta, block_meta = _sort_meta(cnt, n_rows_max)
    xs = _dispatch(layout_meta, pe, po, key, hb, n_rows_max)
    ys = _experts(block_meta, xs, layer, w_gate, w_up, w_down)
    out = _combine(layout_meta, pe, po, key, wkey, ys, base, gt, fnorm, s, final, kv)
    if kv is not None:
        return (out[0].reshape(bsz, s, d),) + tuple(out[1:])
    return out.reshape(bsz, s, d)


def _kv_body(x, sh_ref, sc_ref, nw_ref, wk_ref, wvt_ref, k_ref, vt_ref, km_ref):
    hb = (_rms(x, nw_ref[...]) * (1.0 + sc_ref[0]) + sh_ref[0]).astype(BF16)
    k = _bdot(hb, wk_ref[...])
    vt = _bdot_nt(wvt_ref[...], hb)
    for blk in range(x.shape[0] // B_BLOCK):
        rs = slice(blk * B_BLOCK, (blk + 1) * B_BLOCK)
        for p in range(k.shape[1] // LANES):
            k_ref[0, p, blk] = k[rs, p * LANES:(p + 1) * LANES].astype(BF16)
            vt_ref[0, p, blk] = vt[p * LANES:(p + 1) * LANES, rs].astype(BF16)
        km_ref[0, blk] = jnp.mean(k[rs], axis=0, keepdims=True)


def _kv_kernel(x_ref, *refs):
    _kv_body(x_ref[0], *refs)


def _kv(x, sh, sc, nw, w_k, w_v):
    bsz, s, d = x.shape
    nb = s // B_BLOCK
    npair = d // LANES
    vec = pl.BlockSpec((1, 1, d), lambda b, j: (b, 0, 0))
    full2 = lambda shape: pl.BlockSpec(shape, lambda b, j: (0, 0))
    return pl.pallas_call(
        _kv_kernel,
        grid=(bsz, nb),
        in_specs=[pl.BlockSpec((1, B_BLOCK, d), lambda b, j: (b, j, 0)), vec, vec,
                  full2((1, d)), full2((d, d)), full2((d, d))],
        out_specs=[pl.BlockSpec((1, npair, 1, B_BLOCK, LANES), lambda b, j: (b, 0, j, 0, 0)),
                   pl.BlockSpec((1, npair, 1, LANES, B_BLOCK), lambda b, j: (b, 0, j, 0, 0)),
                   pl.BlockSpec((1, 1, 1, d), lambda b, j: (b, j, 0, 0))],
        out_shape=[jax.ShapeDtypeStruct((bsz, npair, nb, B_BLOCK, LANES), BF16),
                   jax.ShapeDtypeStruct((bsz, npair, nb, LANES, B_BLOCK), BF16),
                   jax.ShapeDtypeStruct((bsz, nb, 1, d), F32)],
        compiler_params=_cparams(("arbitrary", "arbitrary")),
    )(x, sh, sc, nw.reshape(1, d), w_k.astype(BF16), w_v.T.astype(BF16))


def _attn_body(x, qb, sh_ref, sc_ref, g_ref, nw_ref, wqt_ref, wo_ref, k_ref, vt_ref, km_ref,
               qt_sc, qs_sc, acc_sc, sel_sc, m_sc, l_sc, *, nb, n_sel):
    bq = x.shape[0]
    npair = qt_sc.shape[0]
    nbp = km_ref.shape[2]
    hd = LANES // 2
    scale = float(hd) ** -0.5 * 1.4426950408889634
    h = _rms(x, nw_ref[...]) * (1.0 + sc_ref[0]) + sh_ref[0]
    qt = _bdot_nt(wqt_ref[...], h.astype(BF16))
    for p in range(npair):
        qt_sc[p] = qt[p * LANES:(p + 1) * LANES, :]

    subn = lax.broadcasted_iota(I32, (nbp, bq), 0)
    past = subn < qb
    krow = lax.broadcasted_iota(I32, (B_BLOCK, bq), 0)
    qcol = lax.broadcasted_iota(I32, (B_BLOCK, bq), 1)
    causal = krow <= qcol
    rowh = lax.broadcasted_iota(I32, (LANES, 1), 0)

    grp = ATTN_PAIR_UNROLL
    heads = [(u, e) for u in range(grp) for e in range(2)]
    ones_rows = jnp.ones((SUM_ROWS, B_BLOCK), BF16)

    def vsum(vt2, e):
        return jnp.concatenate([vt2[e * hd:(e + 1) * hd, :], ones_rows], axis=0)

    def own_body(gi, carry):
        ps = [gi * grp + u for u in range(grp)]
        q2ts = [qt_sc[p] for p in ps]
        kms = [km_ref[0, p] for p in ps]
        kown = [k_ref[0, p, qb] for p in ps]
        vown = [vt_ref[0, p, qb] for p in ps]
        qets = [jnp.where((rowh >= hd) if e == 1 else (rowh < hd), q2ts[u], 0.0) for u, e in heads]
        qsts = [(q * scale).astype(BF16) for q in qets]
        ss = [jnp.where(causal, _bdot(kown[u], qsts[i]), NEG_INF) for i, (u, e) in enumerate(heads)]
        gates = [_dot3(kms[u], qets[i]) for i, (u, e) in enumerate(heads)]
        ms = [jnp.max(s, axis=0, keepdims=True) for s in ss]
        pes = [jnp.exp2(s - m) for s, m in zip(ss, ms)]
        pvs = [_bdot(vsum(vown[u], e), pes[i].astype(BF16)) for i, (u, e) in enumerate(heads)]
        accs = [pv[:hd] for pv in pvs]
        ls = [pv[hd:hd + 1] for pv in pvs]
        sels = []
        for gate in gates:
            selt = jnp.zeros((nbp, bq), F32)
            for n in range(nb):
                gn = gate[n:n + 1, :]
                beats = jnp.logical_or(gate > gn, jnp.logical_and(gate == gn, subn < n))
                beats = jnp.logical_and(beats, past)
                cnt = jnp.sum(beats.astype(F32), axis=0, keepdims=True)
                selt = jnp.where(subn == n, (cnt < n_sel).astype(F32), selt)
            sels.append(selt)
        for i, (u, e) in enumerate(heads):
            p = ps[u]
            sel_sc[p, e] = sels[i]
            qs_sc[p, e] = qsts[i]
            m_sc[p, e] = ms[i]
            l_sc[p, e] = ls[i]
            acc_sc[p, e * hd:(e + 1) * hd, :] = accs[i]
        return carry

    lax.fori_loop(0, npair // grp, own_body, 0)

    grp = ATTN_PAIR_UNROLL_PAST
    heads = [(u, e) for u in range(grp) for e in range(2)]

    def kb_body(kb, carry):
        def group_body(gi, c2):
            ps = [gi * grp + u for u in range(grp)]
            kbl = [k_ref[0, p, kb] for p in ps]
            vbl = [vt_ref[0, p, kb] for p in ps]
            qsts = [qs_sc[ps[u], e] for u, e in heads]
            rows = [sel_sc[ps[u], e, pl.ds(kb, 1), :] for u, e in heads]
            m_old = [m_sc[ps[u], e] for u, e in heads]
            l_old = [l_sc[ps[u], e] for u, e in heads]
            a_old = [acc_sc[ps[u], e * hd:(e + 1) * hd, :] for u, e in heads]
            ss = [jnp.where(rows[i] > 0.5, _bdot(kbl[u], qsts[i]), NEG_INF)
                  for i, (u, e) in enumerate(heads)]
            m_new = [jnp.maximum(m, jnp.max(s, axis=0, keepdims=True)) for m, s in zip(m_old, ss)]
            alphas = [jnp.exp2(m - mn) for m, mn in zip(m_old, m_new)]
            pes = [jnp.exp2(s - mn) for s, mn in zip(ss, m_new)]
            pvs = [_bdot(vsum(vbl[u], e), pes[i].astype(BF16)) for i, (u, e) in enumerate(heads)]
            l_new = [a * l + pv[hd:hd + 1] for a, l, pv in zip(alphas, l_old, pvs)]
            a_new = [a * ao + pv[:hd] for a, ao, pv in zip(alphas, a_old, pvs)]
            for i, (u, e) in enumerate(heads):
                p = ps[u]
                m_sc[p, e] = m_new[i]
                l_sc[p, e] = l_new[i]
                acc_sc[p, e * hd:(e + 1) * hd, :] = a_new[i]
            return c2

        lax.fori_loop(0, npair // grp, group_body, 0)
        return carry

    lax.fori_loop(0, qb, kb_body, 0)

    parts = []
    for p in range(npair):
        for e in range(2):
            parts.append(acc_sc[p, e * hd:(e + 1) * hd, :] / l_sc[p, e])
    ot = jnp.concatenate(parts, axis=0)
    return x + g_ref[0] * _bdot(ot.T.astype(BF16), wo_ref[...])


def _attn_mixer(mods, nw, w_q, w_o, k5, vt5, km2):
    sh, sc, gt = mods
    d = w_q.shape[0]
    npair, nb = k5.shape[1], k5.shape[2]
    nbp = km2.shape[2]
    n_sel = min(B_TOPK, nb - 1)
    per = TM_SORT // B_BLOCK
    vec = pl.BlockSpec((1, 1, d), lambda b, j: (b, 0, 0))
    full2 = lambda shape: pl.BlockSpec(shape, lambda b, j: (0, 0))
    specs = [vec, vec, vec, full2((1, d)), full2((d, d)), full2((d, d)),
             pl.BlockSpec((1, npair, nb, B_BLOCK, LANES), lambda b, j: (b, 0, 0, 0, 0)),
             pl.BlockSpec((1, npair, nb, LANES, B_BLOCK), lambda b, j: (b, 0, 0, 0, 0)),
             pl.BlockSpec((1, npair, nbp, LANES), lambda b, j: (b, 0, 0, 0))]
    args = [sh, sc, gt, nw.reshape(1, d), w_q.T.astype(BF16), w_o.astype(BF16), k5, vt5, km2]
    scratch = [pltpu.VMEM((npair, LANES, B_BLOCK), F32),
               pltpu.VMEM((npair, 2, LANES, B_BLOCK), BF16),
               pltpu.VMEM((npair, LANES, B_BLOCK), F32),
               pltpu.VMEM((npair, 2, nbp, B_BLOCK), F32),
               pltpu.VMEM((npair, 2, 1, B_BLOCK), F32),
               pltpu.VMEM((npair, 2, 1, B_BLOCK), F32)]

    def body(x_ref, refs, scr):
        j = pl.program_id(1)
        outs = [_attn_body(x_ref[0, h * B_BLOCK:(h + 1) * B_BLOCK, :], j * per + h, *refs, *scr,
                           nb=nb, n_sel=n_sel) for h in range(per)]
        return jnp.concatenate(outs, axis=0)

    return body, specs, args, scratch


def kernel(x, c, ada_w, ada_b, norm_mix, norm_ffn, a_w_in, a_b_in, a_ln_g, a_ln_b, a_w_s, a_b_s,
           a_w_out, kv_norm, kv_ada_w, kv_ada_b, kv_w_k, kv_w_v, b_w_q, b_w_o, moe_router, moe_bias,
           moe_w_gate, moe_w_up, moe_w_down, sh_w_gate, sh_w_up, sh_w_down, final_norm):
    bsz, s, d = x.shape
    depth = ada_w.shape[0]
    n_a = a_w_in.shape[0]
    assert s % B_BLOCK == 0 and s % TM_SORT == 0 and TM_SORT % A_CHUNK == 0 and d % LANES == 0
    nb = s // B_BLOCK
    npair = d // LANES
    nbp = -(-nb // 8) * 8

    def split(m, n):
        return [m[:, i * d:(i + 1) * d].reshape(bsz, 1, d) for i in range(n)]

    def pair_major(km):
        km = km.reshape(bsz, nb, npair, LANES).transpose(0, 2, 1, 3)
        return jnp.pad(km, ((0, 0), (0, 0), (0, nbp - nb), (0, 0)))

    layer_mods = _ada(c, ada_w, ada_b)
    k5 = vt5 = km2 = None
    for i in range(depth):
        sh1, sc1, g1, sh2, sc2, g2 = split(layer_mods[i], 6)
        if i < n_a:
            mixer = _gmlp_mixer((sh1, sc1, g1), norm_mix[i], a_w_in[i], a_b_in[i], a_ln_g[i], a_ln_b[i],
                                a_w_s[i], a_b_s[i], a_w_out[i])
        else:
            if k5 is None:
                ksh, ksc = split(_ada(c, kv_ada_w[None], kv_ada_b[None])[0], 2)
                k5, vt5, km = _kv(x, ksh, ksc, kv_norm, kv_w_k, kv_w_v)
                km2 = pair_major(km)
            j = i - n_a
            mixer = _attn_mixer((sh1, sc1, g1), norm_mix[i], b_w_q[j], b_w_o[j], k5, vt5, km2)
        kv = None
        if i == n_a - 1 and i + 1 < depth:
            ksh, ksc = split(_ada(c, kv_ada_w[None], kv_ada_b[None])[0], 2)
            kv = (ksh, ksc, kv_norm, kv_w_k, kv_w_v)
        res = _moe(x, (sh2, sc2, g2), norm_ffn[i], moe_router[i], moe_bias[i], i, moe_w_gate,
                   moe_w_up, moe_w_down, sh_w_gate[i], sh_w_up[i], sh_w_down[i],
                   final_norm, i == depth - 1, mixer, kv)
        if kv is not None:
            x, k5, vt5, km = res
            km2 = pair_major(km)
        else:
            x = res
    return x
```

```python
import functools

import jax
import jax.numpy as jnp
from jax import lax
from jax.experimental import pallas as pl
from jax.experimental.pallas import tpu as pltpu

F32 = jnp.float32
BF16 = jnp.bfloat16
I32 = jnp.int32

RMS_EPS = 1e-6
LN_EPS = 1e-5
NEG_INF = -1e30

A_CHUNK = 128
A_GROUPS = 8
B_HEADS = 16
B_BLOCK = 256
B_TOPK = 3
N_EXPERTS = 64
TOP_K = 8
N_GROUPS = 8
TOPK_GROUPS = 4
ROUTED_SCALE = 2.5

LANES = 128
VMEM_LIMIT = 56 * 1024 * 1024

TM_SORT = 512
RUN_ALIGN = 16
SORT_CHUNK = 256
ROWS_TILE_USED = TM_SORT * TOP_K + N_EXPERTS * (RUN_ALIGN - 1)
ROWS_TILE = -(-ROWS_TILE_USED // SORT_CHUNK) * SORT_CHUNK
PIECES_TILE = ROWS_TILE // RUN_ALIGN
PIECES_PAD = -(-PIECES_TILE // LANES) * LANES
BM_EXPERT = 2048
BM_SEG = 256
BM_CHAIN = 256
ATTN_PAIR_UNROLL = 4
ATTN_PAIR_UNROLL_PAST = 8
SUM_ROWS = 16


def _cparams(sem):
    return pltpu.CompilerParams(dimension_semantics=sem, vmem_limit_bytes=VMEM_LIMIT)


def _sigmoid(x):
    return 1.0 / (1.0 + jnp.exp(-x))


def _silu(x):
    return x * _sigmoid(x)


def _gelu_tanh(x):
    hx = 0.5 * x
    return hx + hx * jnp.tanh(x * (0.7978845608028654 + 0.035677408136300125 * (x * x)))


def _rms(x, g):
    return x * lax.rsqrt(jnp.mean(x * x, axis=-1, keepdims=True) + RMS_EPS) * g


def _bdot(a, b):
    return jnp.dot(a, b, preferred_element_type=F32)


def _bdot_nt(a, b):
    return lax.dot_general(a, b, (((1,), (1,)), ((), ())), preferred_element_type=F32)


def _split(a):
    hi = a.astype(BF16)
    lo = (a - hi.astype(F32)).astype(BF16)
    return hi, lo


def _dot3_nt(a, b):
    ah, al = _split(a)
    bh, bl = _split(b)
    return _bdot_nt(ah, bh) + (_bdot_nt(ah, bl) + _bdot_nt(al, bh))


def _dot3(a, b):
    ah, al = _split(a)
    bh, bl = _split(b)
    return _bdot(ah, bh) + (_bdot(ah, bl) + _bdot(al, bh))


def _ada_kernel(c_ref, w_ref, b_ref, o_ref):
    a = _silu(c_ref[...]).astype(BF16)
    o_ref[0] = _bdot(a, w_ref[0].astype(BF16)) + b_ref[0]


def _ada(c, w, b):
    bsz, d = c.shape
    nl, _, n = w.shape
    tn = 1024
    return pl.pallas_call(
        _ada_kernel,
        grid=(nl, n // tn),
        in_specs=[pl.BlockSpec((bsz, d), lambda l, j: (0, 0)),
                  pl.BlockSpec((1, d, tn), lambda l, j: (l, 0, j)),
                  pl.BlockSpec((1, 1, tn), lambda l, j: (l, 0, j))],
        out_specs=pl.BlockSpec((1, bsz, tn), lambda l, j: (l, 0, j)),
        out_shape=jax.ShapeDtypeStruct((nl, bsz, n), F32),
        compiler_params=_cparams(("arbitrary", "arbitrary")),
    )(c, w, b.reshape(nl, 1, n))


def _gmlp_body(x, sh_ref, sc_ref, g_ref, nw_ref, win_ref, bin_ref, lng_ref, lnb_ref,
               ws_ref, bst_ref, wout_ref, y_sc):
    tm = x.shape[0]
    h = _rms(x, nw_ref[...]) * (1.0 + sc_ref[0]) + sh_ref[0]
    z = _gelu_tanh(_bdot(h.astype(BF16), win_ref[...]) + bin_ref[...])
    aw = z.shape[1] // 2
    gd = aw // A_GROUPS
    u = z[:, :aw]
    v = z[:, aw:]
    mu = jnp.mean(v, axis=-1, keepdims=True)
    dv = v - mu
    var = jnp.mean(dv * dv, axis=-1, keepdims=True)
    vn = (dv * lax.rsqrt(var + LN_EPS) * lng_ref[...] + lnb_ref[...]).astype(BF16)
    row = lax.broadcasted_iota(I32, (A_CHUNK, A_CHUNK), 0)
    col = lax.broadcasted_iota(I32, (A_CHUNK, A_CHUNK), 1)
    causal = col <= row
    for g in range(A_GROUPS):
        wg = jnp.where(causal, ws_ref[g], 0.0).astype(BF16)
        bcol = bst_ref[:, g:g + 1]
        for ci in range(tm // A_CHUNK):
            rs = slice(ci * A_CHUNK, (ci + 1) * A_CHUNK)
            cs = slice(g * gd, (g + 1) * gd)
            sv = _bdot(wg, vn[rs, cs]) + bcol
            y_sc[rs, cs] = (u[rs, cs] * sv).astype(BF16)
    return x + g_ref[0] * _bdot(y_sc[...], wout_ref[...])


def _route(x, sh_ref, sc_ref, g_ref, nw_ref, wrt_ref, bias_ref, wsg_ref, wsu_ref, wsd_ref,
           h_ref, base_ref, key_ref, wkey_ref, cnt_ref, pe_ref, po_ref):
    tm = x.shape[0]
    h = _rms(x, nw_ref[...]) * (1.0 + sc_ref[0]) + sh_ref[0]
    hb = h.astype(BF16)
    h_ref[...] = hb
    act = (_silu(_bdot(hb, wsg_ref[...])) * _bdot(hb, wsu_ref[...])).astype(BF16)
    base_ref[...] = x + g_ref[0] * _bdot(act, wsd_ref[...])

    scores = _sigmoid(_dot3_nt(wrt_ref[...], h))
    choice = scores + bias_ref[...]
    gsz = N_EXPERTS // N_GROUPS
    sub = lax.broadcasted_iota(I32, (gsz, tm), 0)
    blocks = [choice[g * gsz:(g + 1) * gsz] for g in range(N_GROUPS)]
    gscore = []
    for blk in blocks:
        m1 = jnp.max(blk, axis=0, keepdims=True)
        i1 = jnp.min(jnp.where(blk == m1, sub, gsz), axis=0, keepdims=True)
        m2 = jnp.max(jnp.where(sub == i1, -jnp.inf, blk), axis=0, keepdims=True)
        gscore.append(m1 + m2)
    masked = []
    for g in range(N_GROUPS):
        beats = jnp.zeros((1, tm), F32)
        for m in range(N_GROUPS):
            if m == g:
                continue
            b = gscore[m] > gscore[g]
            if m < g:
                b = jnp.logical_or(b, gscore[m] == gscore[g])
            beats = beats + b.astype(F32)
        masked.append(jnp.where(beats < TOPK_GROUPS, blocks[g], NEG_INF))
    cur = jnp.concatenate(masked, axis=0)
    eio = lax.broadcasted_iota(I32, (N_EXPERTS, tm), 0)
    chosen = jnp.zeros((N_EXPERTS, tm), jnp.bool_)
    wsum = jnp.zeros((1, tm), F32)
    for _ in range(TOP_K):
        m = jnp.max(cur, axis=0, keepdims=True)
        idx = jnp.min(jnp.where(cur == m, eio, N_EXPERTS), axis=0, keepdims=True)
        sel = eio == idx
        chosen = jnp.logical_or(chosen, sel)
        wsum = wsum + jnp.sum(jnp.where(sel, scores, 0.0), axis=0, keepdims=True)
        cur = jnp.where(sel, -jnp.inf, cur)
    wkey_ref[...] = jnp.where(chosen, scores / wsum * ROUTED_SCALE, 0.0)

    onehot = chosen.astype(BF16)
    r_i = lax.broadcasted_iota(I32, (tm, tm), 0)
    c_i = lax.broadcasted_iota(I32, (tm, tm), 1)
    before = (r_i < c_i).astype(BF16)
    prior = _bdot(onehot, before)
    key_ref[...] = jnp.where(chosen, prior, -1.0).astype(I32)
    cnt = jnp.sum(chosen.astype(F32), axis=1, keepdims=True)
    cnt_ref[0] = cnt

    run_p = jnp.floor((cnt + (RUN_ALIGN - 1)) / RUN_ALIGN)
    ppc = SORT_CHUNK // RUN_ALIGN
    tot = jnp.sum(run_p, axis=0, keepdims=True)
    fill = jnp.ceil(tot / ppc) * ppc - tot
    run_p = run_p + jnp.where(lax.broadcasted_iota(I32, (N_EXPERTS, 1), 0) == N_EXPERTS - 1, fill, 0.0)
    e_r = lax.broadcasted_iota(I32, (N_EXPERTS, N_EXPERTS), 0)
    e_c = lax.broadcasted_iota(I32, (N_EXPERTS, N_EXPERTS), 1)
    incl = (e_c <= e_r).astype(BF16)
    lend = _bdot(incl, jnp.broadcast_to(run_p, (N_EXPERTS, LANES)).astype(BF16))[:, 0:1]
    loff = lend - run_p
    pj = lax.broadcasted_iota(I32, (N_EXPERTS, PIECES_PAD), 1).astype(F32)
    er = lax.broadcasted_iota(I32, (N_EXPERTS, PIECES_PAD), 0).astype(F32)
    pe = jnp.minimum(jnp.sum((lend <= pj).astype(F32), axis=0, keepdims=True), N_EXPERTS - 1.0)
    lo = jnp.sum(jnp.where(er == pe, loff, 0.0), axis=0, keepdims=True)
    pe_ref[0] = pe.astype(I32)
    po_ref[0] = ((pj[0:1, :] - lo) * RUN_ALIGN).astype(I32)


N_ROUTE_REFS = 16


def _router_kernel(x_ref, *refs):
    _route(x_ref[0], *refs)


def _mixer_router_kernel(x_ref, *refs, body, n_mix):
    route_refs = refs[n_mix:n_mix + N_ROUTE_REFS]
    x1 = body(x_ref, refs[:n_mix], refs[n_mix + N_ROUTE_REFS:])
    _route(x1, *route_refs)


def _gmlp_mixer(mods, nw, w_in, b_in, ln_g, ln_b, w_s, b_s, w_out):
    sh, sc, gt = mods
    d, n_in = w_in.shape
    aw = n_in // 2
    vec = pl.BlockSpec((1, 1, d), lambda b, j: (b, 0, 0))
    full2 = lambda shape: pl.BlockSpec(shape, lambda b, j: (0, 0))
    specs = [vec, vec, vec, full2((1, d)), full2((d, n_in)), full2((1, n_in)), full2((1, aw)),
             full2((1, aw)), pl.BlockSpec((A_GROUPS, A_CHUNK, A_CHUNK), lambda b, j: (0, 0, 0)),
             full2((A_CHUNK, A_GROUPS)), full2((aw, d))]
    args = [sh, sc, gt, nw.reshape(1, d), w_in.astype(BF16), b_in.reshape(1, n_in),
            ln_g.reshape(1, aw), ln_b.reshape(1, aw), w_s, b_s.T, w_out.astype(BF16)]

    def body(x_ref, refs, scratch):
        return _gmlp_body(x_ref[0], *refs, *scratch)

    return body, specs, args, [pltpu.VMEM((TM_SORT, aw), BF16)]


def _router(x, mods, nw, w_router, e_bias, wsg, wsu, wsd, mixer=None):
    bsz, s, d = x.shape
    t = bsz * s
    tm = TM_SORT
    nt = s // tm
    sd = wsg.shape[1]
    vec = pl.BlockSpec((1, 1, d), lambda b, j: (b, 0, 0))
    full2 = lambda shape: pl.BlockSpec(shape, lambda b, j: (0, 0))
    tok = pl.BlockSpec((tm, d), lambda b, j: (b * nt + j, 0))
    etok = pl.BlockSpec((N_EXPERTS, tm), lambda b, j: (0, b * nt + j))
    ptab = pl.BlockSpec((1, 1, PIECES_PAD), lambda b, j: (b * nt + j, 0, 0))
    sh, sc, gt = mods
    in_specs = [vec, vec, vec, full2((1, d)), full2((N_EXPERTS, d)), full2((N_EXPERTS, 1)),
                full2((d, sd)), full2((d, sd)), full2((sd, d))]
    args = [sh, sc, gt, nw.reshape(1, d), w_router.T, e_bias.reshape(N_EXPERTS, 1),
            wsg.astype(BF16), wsu.astype(BF16), wsd.astype(BF16)]
    kern, scratch = _router_kernel, []
    if mixer is not None:
        body, mspecs, margs, scratch = mixer
        kern = functools.partial(_mixer_router_kernel, body=body, n_mix=len(margs))
        in_specs = mspecs + in_specs
        args = margs + args
    return pl.pallas_call(
        kern,
        grid=(bsz, nt),
        in_specs=[pl.BlockSpec((1, tm, d), lambda b, j: (b, j, 0))] + in_specs,
        out_specs=[tok, tok, etok, etok,
                   pl.BlockSpec((1, N_EXPERTS, 1), lambda b, j: (b * nt + j, 0, 0)), ptab, ptab],
        out_shape=[jax.ShapeDtypeStruct((t, d), BF16), jax.ShapeDtypeStruct((t, d), F32),
                   jax.ShapeDtypeStruct((N_EXPERTS, t), I32), jax.ShapeDtypeStruct((N_EXPERTS, t), F32),
                   jax.ShapeDtypeStruct((t // tm, N_EXPERTS, 1), F32),
                   jax.ShapeDtypeStruct((t // tm, 1, PIECES_PAD), I32),
                   jax.ShapeDtypeStruct((t // tm, 1, PIECES_PAD), I32)],
        scratch_shapes=scratch,
        compiler_params=_cparams(("arbitrary", "arbitrary")),
    )(x, *args)


def _rows_max(t):
    rows = (t // TM_SORT) * ROWS_TILE + N_EXPERTS * (BM_SEG - RUN_ALIGN)
    return -(-rows // BM_EXPERT) * BM_EXPERT


def _sort_meta(cnt, n_rows_max):
    nts = cnt.shape[0]
    c = cnt.reshape(nts, N_EXPERTS).astype(I32)
    run = (c + (RUN_ALIGN - 1)) // RUN_ALIGN * RUN_ALIGN
    fill = (-jnp.sum(run, axis=1, keepdims=True)) % SORT_CHUNK
    run = jnp.concatenate([run[:, :-1], run[:, -1:] + fill], axis=1)
    used = jnp.sum(run, axis=1)
    per_e = jnp.sum(run, axis=0)
    seg = (per_e + (BM_SEG - 1)) // BM_SEG * BM_SEG
    ends = jnp.cumsum(seg)
    starts = ends - seg
    goff = starts[None, :] + jnp.cumsum(run, axis=0) - run
    pad_start = starts + per_e
    pad_cnt = (seg - per_e) // RUN_ALIGN

    nh = n_rows_max // BM_SEG
    hidx = jnp.arange(nh, dtype=I32)
    he = jnp.sum((ends[None, :] <= (hidx * BM_SEG)[:, None]).astype(I32), axis=1)
    real = he < N_EXPERTS
    he = jnp.minimum(he, N_EXPERTS - 1).astype(I32)
    prev = jnp.concatenate([jnp.full((1,), -1, I32), he[:-1]])
    hnew = jnp.logical_and(real, he != prev)
    per_blk = BM_EXPERT // BM_SEG
    hslot = (jnp.cumsum(hnew.astype(I32)) - 1) % per_blk
    later = jnp.logical_and(hidx[None, :] > hidx[:, None], hnew[None, :])
    nxt = jnp.min(jnp.where(later, hidx[None, :], nh), axis=1)
    hfetch = jnp.where(nxt < nh, he[jnp.minimum(nxt, nh - 1)], -1)
    bx = jnp.where(real[::per_blk], jnp.arange(nh // per_blk, dtype=I32), 0)
    i32 = lambda a: a.reshape(-1).astype(I32)
    return ((i32(goff), i32(used), i32(pad_start), i32(pad_cnt)),
            (bx, he, i32(hnew), i32(real), i32(hslot), i32(hfetch)))


def _piece_copies(pe_ref, po_ref, goff_ref, used_ref, tile, local, remote, sem, to_remote):
    def copy(j):
        idx = tile * PIECES_PAD + j
        l0 = pl.multiple_of(j * RUN_ALIGN, RUN_ALIGN)
        g0 = pl.multiple_of(goff_ref[tile * N_EXPERTS + pe_ref[idx]] + po_ref[idx], RUN_ALIGN)
        lref = local.at[pl.ds(l0, RUN_ALIGN)]
        gref = remote.at[pl.ds(g0, RUN_ALIGN)]
        return pltpu.make_async_copy(lref, gref, sem) if to_remote else pltpu.make_async_copy(gref, lref, sem)

    def wait_all():
        def body(j, carry):
            copy(j).wait()
            return carry

        lax.fori_loop(0, used_ref[tile] // RUN_ALIGN, body, 0)

    return copy, wait_all


def _pad_copies(pad_start_ref, pad_cnt_ref, tile, ntiles, zeros, remote, sem):
    share = -(-N_EXPERTS // ntiles)

    def apply(act):
        for q in range(share):
            e = tile * share + q
            ec = jnp.minimum(e, N_EXPERTS - 1)
            n = jnp.where(e < N_EXPERTS, pad_cnt_ref[ec], 0)
            base = pad_start_ref[ec]

            def body(j, carry):
                g0 = pl.multiple_of(base + j * RUN_ALIGN, RUN_ALIGN)
                getattr(pltpu.make_async_copy(zeros, remote.at[pl.ds(g0, RUN_ALIGN)], sem), act)()
                return carry

            lax.fori_loop(0, n, body, 0)

    return apply


def _piece_rows(pe_ref, po_ref, key_ref, first_piece, npieces, val_ref=None):
    tm = key_ref.shape[1]
    sub = lax.broadcasted_iota(I32, (RUN_ALIGN, tm), 0)
    out = []
    for jj in range(npieces):
        j = first_piece + jj
        e = pe_ref[j]
        hit = (key_ref[pl.ds(e, 1), :] - po_ref[j]) == sub
        val = 1.0 if val_ref is None else val_ref[pl.ds(e, 1), :]
        out.append(jnp.where(hit, val, 0.0))
    return out


def _dispatch_kernel(pe_ref, po_ref, goff_ref, used_ref, pad_start_ref, pad_cnt_ref,
                     key_ref, h_ref, xs_ref, xbuf, p_sc, zbuf, sem, zsem, *, ntiles):
    i = pl.program_id(0)
    hb = h_ref[...]
    ch = SORT_CHUNK
    ppc = ch // RUN_ALIGN
    zbuf[...] = jnp.zeros_like(zbuf)
    pads = _pad_copies(pad_start_ref, pad_cnt_ref, i, ntiles, zbuf, xs_ref, zsem)
    pads("start")

    copy, wait_all = _piece_copies(pe_ref, po_ref, goff_ref, used_ref, i, xbuf, xs_ref, sem, True)

    def build(ci):
        rows = _piece_rows(pe_ref, po_ref, key_ref, i * PIECES_PAD + ci * ppc, ppc)
        for jj, p in enumerate(rows):
            p_sc[jj * RUN_ALIGN:(jj + 1) * RUN_ALIGN, :] = p.astype(BF16)
        xbuf[pl.ds(pl.multiple_of(ci * ch, ch), ch), :] = _bdot(p_sc[...], hb).astype(BF16)

    def send(ci):
        for jj in range(ppc):
            copy(ci * ppc + jj).start()

    def step(ci, carry):
        send(ci - 1)
        build(ci)
        return carry

    nchunk = used_ref[i] // ch
    build(0)
    lax.fori_loop(1, nchunk, step, 0)
    send(nchunk - 1)
    pads("wait")
    wait_all()


def _dispatch(meta, pe, po, key, hb, n_rows_max):
    t, d = hb.shape
    tm = TM_SORT
    goff, used, pad_start, pad_cnt = meta
    return pl.pallas_call(
        functools.partial(_dispatch_kernel, ntiles=t // tm),
        grid_spec=pltpu.PrefetchScalarGridSpec(
            num_scalar_prefetch=6, grid=(t // tm,),
            in_specs=[pl.BlockSpec((N_EXPERTS, tm), lambda i, *_: (0, i)),
                      pl.BlockSpec((tm, d), lambda i, *_: (i, 0))],
            out_specs=pl.BlockSpec(memory_space=pl.ANY),
            scratch_shapes=[pltpu.VMEM((ROWS_TILE, d), BF16), pltpu.VMEM((SORT_CHUNK, tm), BF16),
                            pltpu.VMEM((RUN_ALIGN, d), BF16),
                            pltpu.SemaphoreType.DMA(()), pltpu.SemaphoreType.DMA(())]),
        out_shape=jax.ShapeDtypeStruct((n_rows_max, d), BF16),
        compiler_params=_cparams(("arbitrary",)),
    )(pe, po, goff, used, pad_start, pad_cnt, key, hb)


def _expert_kernel(bx_ref, he_ref, hnew_ref, hreal_ref, hslot_ref, hfetch_ref, x_ref, wg_hbm, wu_hbm, wd_hbm,
                   o_ref, wg_st, wu_st, wd_st, wg_sc, wu_sc, wd_sc, sem, *, layer):
    b = pl.program_id(0)
    per_blk = BM_EXPERT // BM_SEG
    cpp = BM_SEG // BM_CHAIN

    def fetch(e):
        return [pltpu.make_async_copy(src.at[layer, e], dst, sem.at[i])
                for i, (src, dst) in enumerate(((wg_hbm, wg_st), (wu_hbm, wu_st), (wd_hbm, wd_st)))]

    @pl.when(b == 0)
    def _():
        for cp in fetch(he_ref[0]):
            cp.start()

    for part in range(per_blk):
        h = b * per_blk + part

        @pl.when(hnew_ref[h] == 1)
        def _():
            for cp in fetch(he_ref[h]):
                cp.wait()
            s = hslot_ref[h]
            wg_sc[s] = wg_st[...].astype(BF16)
            wu_sc[s] = wu_st[...].astype(BF16)
            wd_sc[s] = wd_st[...].astype(BF16)

            @pl.when(hfetch_ref[h] >= 0)
            def _():
                for cp in fetch(hfetch_ref[h]):
                    cp.start()

    def run_parts(parts):
        chains = [(slice((p * cpp + c) * BM_CHAIN, (p * cpp + c + 1) * BM_CHAIN), hslot_ref[b * per_blk + p])
                  for p in parts for c in range(cpp)]
        xs = [x_ref[rs, :] for rs, _ in chains]
        gs = [_bdot(x, wg_sc[s]) for x, (_, s) in zip(xs, chains)]
        us = [_bdot(x, wu_sc[s]) for x, (_, s) in zip(xs, chains)]
        acts = [(_silu(g) * u).astype(BF16) for g, u in zip(gs, us)]
        for (rs, s), act in zip(chains, acts):
            o_ref[rs, :] = _bdot(act, wd_sc[s]).astype(BF16)

    nlive = hreal_ref[b * per_blk]
    for part in range(1, per_blk):
        nlive = nlive + hreal_ref[b * per_blk + part]

    @pl.when(nlive == per_blk)
    def _():
        run_parts(range(per_blk))

    @pl.when(nlive < per_blk)
    def _():
        for part in range(per_blk):
            @pl.when(part < nlive)
            def _():
                run_parts([part])

            @pl.when(part >= nlive)
            def _():
                o_ref[part * BM_SEG:(part + 1) * BM_SEG, :] = jnp.zeros((BM_SEG, o_ref.shape[1]), BF16)


def _experts(meta, xs, layer, w_gate, w_up, w_down):
    r, d = xs.shape
    ed = w_gate.shape[-1]
    bm = BM_EXPERT
    nslot = BM_EXPERT // BM_SEG
    omap = lambda b, bx, *_: (b, 0)
    xmap = lambda b, bx, *_: (bx[b], 0)
    hbm = pl.BlockSpec(memory_space=pl.ANY)
    return pl.pallas_call(
        functools.partial(_expert_kernel, layer=layer),
        grid_spec=pltpu.PrefetchScalarGridSpec(
            num_scalar_prefetch=6, grid=(r // bm,),
            in_specs=[pl.BlockSpec((bm, d), xmap), hbm, hbm, hbm],
            out_specs=pl.BlockSpec((bm, d), omap),
            scratch_shapes=[pltpu.VMEM((d, ed), F32), pltpu.VMEM((d, ed), F32), pltpu.VMEM((ed, d), F32),
                            pltpu.VMEM((nslot, d, ed), BF16), pltpu.VMEM((nslot, d, ed), BF16),
                            pltpu.VMEM((nslot, ed, d), BF16), pltpu.SemaphoreType.DMA((3,))]),
        out_shape=jax.ShapeDtypeStruct((r, d), BF16),
        compiler_params=_cparams(("arbitrary",)),
    )(*meta, xs, w_gate, w_up, w_down)


def _combine_kernel(pe_ref, po_ref, goff_ref, used_ref, key_ref, wkey_ref, ys_ref, base_ref, g_ref,
                    fn_ref, *rest, final, with_kv):
    if with_kv:
        kv_in, o_ref, kv_out, (ybuf, q_sc, sem) = rest[:5], rest[5], rest[6:9], rest[9:]
    else:
        o_ref, ybuf, q_sc, sem = rest
    i = pl.program_id(0)
    tm = base_ref.shape[0]
    ch = SORT_CHUNK
    ppc = ch // RUN_ALIGN

    @pl.when(i == 0)
    def _():
        ybuf[...] = jnp.zeros_like(ybuf)

    copy, wait_all = _piece_copies(pe_ref, po_ref, goff_ref, used_ref, i, ybuf, ys_ref, sem, False)

    def issue(j, carry):
        copy(j).start()
        return carry

    lax.fori_loop(0, used_ref[i] // RUN_ALIGN, issue, 0)

    for ci in range(ybuf.shape[0] // ch):
        cs = slice(ci * ch, (ci + 1) * ch)

        @pl.when(ci * ch < used_ref[i])
        def _():
            rows = _piece_rows(pe_ref, po_ref, key_ref, i * PIECES_PAD + ci * ppc, ppc, wkey_ref)
            for jj, row in enumerate(rows):
                q_sc[ci * ch + jj * RUN_ALIGN:ci * ch + (jj + 1) * RUN_ALIGN, :] = row.astype(BF16)

        @pl.when(ci * ch >= used_ref[i])
        def _():
            q_sc[cs, :] = jnp.zeros((ch, tm), BF16)

    wait_all()
    out = base_ref[...] + g_ref[0] * lax.dot_general(q_sc[...], ybuf[...], (((0,), (0,)), ((), ())),
                                                     preferred_element_type=F32)
    if final:
        out = _rms(out, fn_ref[...])
    o_ref[...] = out
    if with_kv:
        _kv_body(out, *kv_in, *kv_out)


def _combine(meta, pe, po, key, wkey, ys, base, gt, fnorm, seq, final, kv=None):
    t, d = base.shape
    tm = TM_SORT
    nt = seq // tm
    bsz = t // seq
    vecb = pl.BlockSpec((1, 1, d), lambda i, *_: (i // nt, 0, 0))
    full2 = lambda shape: pl.BlockSpec(shape, lambda i, *_: (0, 0))
    in_specs = [pl.BlockSpec((N_EXPERTS, tm), lambda i, *_: (0, i)),
                pl.BlockSpec((N_EXPERTS, tm), lambda i, *_: (0, i)),
                pl.BlockSpec(memory_space=pl.ANY),
                pl.BlockSpec((tm, d), lambda i, *_: (i, 0)), vecb, full2((1, d))]
    args = [key, wkey, ys, base, gt, fnorm.reshape(1, d)]
    out_specs = [pl.BlockSpec((tm, d), lambda i, *_: (i, 0))]
    out_shape = [jax.ShapeDtypeStruct((t, d), F32)]
    if kv is not None:
        ksh, ksc, knw, w_k, w_v = kv
        nb, npair, per = seq // B_BLOCK, d // LANES, tm // B_BLOCK
        in_specs += [vecb, vecb, full2((1, d)), full2((d, d)), full2((d, d))]
        args += [ksh, ksc, knw.reshape(1, d), w_k.astype(BF16), w_v.T.astype(BF16)]
        out_specs += [pl.BlockSpec((1, npair, per, B_BLOCK, LANES), lambda i, *_: (i // nt, 0, i % nt, 0, 0)),
                      pl.BlockSpec((1, npair, per, LANES, B_BLOCK), lambda i, *_: (i // nt, 0, i % nt, 0, 0)),
                      pl.BlockSpec((1, per, 1, d), lambda i, *_: (i // nt, i % nt, 0, 0))]
        out_shape += [jax.ShapeDtypeStruct((bsz, npair, nb, B_BLOCK, LANES), BF16),
                      jax.ShapeDtypeStruct((bsz, npair, nb, LANES, B_BLOCK), BF16),
                      jax.ShapeDtypeStruct((bsz, nb, 1, d), F32)]
    res = pl.pallas_call(
        functools.partial(_combine_kernel, final=final, with_kv=kv is not None),
        grid_spec=pltpu.PrefetchScalarGridSpec(
            num_scalar_prefetch=4, grid=(t // tm,),
            in_specs=in_specs, out_specs=out_specs,
            scratch_shapes=[pltpu.VMEM((ROWS_TILE, d), BF16), pltpu.VMEM((ROWS_TILE, tm), BF16),
                            pltpu.SemaphoreType.DMA(())]),
        out_shape=out_shape,
        compiler_params=_cparams(("arbitrary",)),
    )(pe, po, meta[0], meta[1], *args)
    return res if kv is not None else res[0]


def _moe(x, mods, nw, w_router, e_bias, layer, w_gate, w_up, w_down, wsg, wsu, wsd, fnorm, final, mixer, kv=None):
    bsz, s, d = x.shape
    t = bsz * s
    gt = mods[2]
    n_rows_max = _rows_max(t)
    hb, base, key, wkey, cnt, pe, po = _router(x, mods, nw, w_router, e_bias, wsg, wsu, wsd, mixer)
    pe = pe.reshape(-1)
    po = po.reshape(-1)
    layout_meta, block_meta = _sort_meta(cnt, n_rows_max)
    xs = _dispatch(layout_meta, pe, po, key, hb, n_rows_max)
    ys = _experts(block_meta, xs, layer, w_gate, w_up, w_down)
    out = _combine(layout_meta, pe, po, key, wkey, ys, base, gt, fnorm, s, final, kv)
    if kv is not None:
        return (out[0].reshape(bsz, s, d),) + tuple(out[1:])
    return out.reshape(bsz, s, d)


def _kv_body(x, sh_ref, sc_ref, nw_ref, wk_ref, wvt_ref, k_ref, vt_ref, km_ref):
    hb = (_rms(x, nw_ref[...]) * (1.0 + sc_ref[0]) + sh_ref[0]).astype(BF16)
    k = _bdot(hb, wk_ref[...])
    vt = _bdot_nt(wvt_ref[...], hb)
    for blk in range(x.shape[0] // B_BLOCK):
        rs = slice(blk * B_BLOCK, (blk + 1) * B_BLOCK)
        for p in range(k.shape[1] // LANES):
            k_ref[0, p, blk] = k[rs, p * LANES:(p + 1) * LANES].astype(BF16)
            vt_ref[0, p, blk] = vt[p * LANES:(p + 1) * LANES, rs].astype(BF16)
        km_ref[0, blk] = jnp.mean(k[rs], axis=0, keepdims=True)


def _kv_kernel(x_ref, *refs):
    _kv_body(x_ref[0], *refs)


def _kv(x, sh, sc, nw, w_k, w_v):
    bsz, s, d = x.shape
    nb = s // B_BLOCK
    npair = d // LANES
    vec = pl.BlockSpec((1, 1, d), lambda b, j: (b, 0, 0))
    full2 = lambda shape: pl.BlockSpec(shape, lambda b, j: (0, 0))
    return pl.pallas_call(
        _kv_kernel,
        grid=(bsz, nb),
        in_specs=[pl.BlockSpec((1, B_BLOCK, d), lambda b, j: (b, j, 0)), vec, vec,
                  full2((1, d)), full2((d, d)), full2((d, d))],
        out_specs=[pl.BlockSpec((1, npair, 1, B_BLOCK, LANES), lambda b, j: (b, 0, j, 0, 0)),
                   pl.BlockSpec((1, npair, 1, LANES, B_BLOCK), lambda b, j: (b, 0, j, 0, 0)),
                   pl.BlockSpec((1, 1, 1, d), lambda b, j: (b, j, 0, 0))],
        out_shape=[jax.ShapeDtypeStruct((bsz, npair, nb, B_BLOCK, LANES), BF16),
                   jax.ShapeDtypeStruct((bsz, npair, nb, LANES, B_BLOCK), BF16),
                   jax.ShapeDtypeStruct((bsz, nb, 1, d), F32)],
        compiler_params=_cparams(("arbitrary", "arbitrary")),
    )(x, sh, sc, nw.reshape(1, d), w_k.astype(BF16), w_v.T.astype(BF16))


def _attn_body(x, qb, sh_ref, sc_ref, g_ref, nw_ref, wqt_ref, wo_ref, k_ref, vt_ref, km_ref,
               qt_sc, qs_sc, acc_sc, sel_sc, m_sc, l_sc, *, nb, n_sel):
    bq = x.shape[0]
    npair = qt_sc.shape[0]
    nbp = km_ref.shape[2]
    hd = LANES // 2
    scale = float(hd) ** -0.5 * 1.4426950408889634
    h = _rms(x, nw_ref[...]) * (1.0 + sc_ref[0]) + sh_ref[0]
    qt = _bdot_nt(wqt_ref[...], h.astype(BF16))
    for p in range(npair):
        qt_sc[p] = qt[p * LANES:(p + 1) * LANES, :]

    subn = lax.broadcasted_iota(I32, (nbp, bq), 0)
    past = subn < qb
    krow = lax.broadcasted_iota(I32, (B_BLOCK, bq), 0)
    qcol = lax.broadcasted_iota(I32, (B_BLOCK, bq), 1)
    causal = krow <= qcol
    rowh = lax.broadcasted_iota(I32, (LANES, 1), 0)

    grp = ATTN_PAIR_UNROLL
    heads = [(u, e) for u in range(grp) for e in range(2)]
    ones_rows = jnp.ones((SUM_ROWS, B_BLOCK), BF16)

    def vsum(vt2, e):
        return jnp.concatenate([vt2[e * hd:(e + 1) * hd, :], ones_rows], axis=0)

    def own_body(gi, carry):
        ps = [gi * grp + u for u in range(grp)]
        q2ts = [qt_sc[p] for p in ps]
        kms = [km_ref[0, p] for p in ps]
        kown = [k_ref[0, p, qb] for p in ps]
        vown = [vt_ref[0, p, qb] for p in ps]
        qets = [jnp.where((rowh >= hd) if e == 1 else (rowh < hd), q2ts[u], 0.0) for u, e in heads]
        qsts = [(q * scale).astype(BF16) for q in qets]
        ss = [jnp.where(causal, _bdot(kown[u], qsts[i]), NEG_INF) for i, (u, e) in enumerate(heads)]
        gates = [_dot3(kms[u], qets[i]) for i, (u, e) in enumerate(heads)]
        ms = [jnp.max(s, axis=0, keepdims=True) for s in ss]
        pes = [jnp.exp2(s - m) for s, m in zip(ss, ms)]
        pvs = [_bdot(vsum(vown[u], e), pes[i].astype(BF16)) for i, (u, e) in enumerate(heads)]
        accs = [pv[:hd] for pv in pvs]
        ls = [pv[hd:hd + 1] for pv in pvs]
        sels = []
        for gate in gates:
            selt = jnp.zeros((nbp, bq), F32)
            for n in range(nb):
                gn = gate[n:n + 1, :]
                beats = jnp.logical_or(gate > gn, jnp.logical_and(gate == gn, subn < n))
                beats = jnp.logical_and(beats, past)
                cnt = jnp.sum(beats.astype(F32), axis=0, keepdims=True)
                selt = jnp.where(subn == n, (cnt < n_sel).astype(F32), selt)
            sels.append(selt)
        for i, (u, e) in enumerate(heads):
            p = ps[u]
            sel_sc[p, e] = sels[i]
            qs_sc[p, e] = qsts[i]
            m_sc[p, e] = ms[i]
            l_sc[p, e] = ls[i]
            acc_sc[p, e * hd:(e + 1) * hd, :] = accs[i]
        return carry

    lax.fori_loop(0, npair // grp, own_body, 0)

    grp = ATTN_PAIR_UNROLL_PAST
    heads = [(u, e) for u in range(grp) for e in range(2)]

    def kb_body(kb, carry):
        def group_body(gi, c2):
            ps = [gi * grp + u for u in range(grp)]
            kbl = [k_ref[0, p, kb] for p in ps]
            vbl = [vt_ref[0, p, kb] for p in ps]
            qsts = [qs_sc[ps[u], e] for u, e in heads]
            rows = [sel_sc[ps[u], e, pl.ds(kb, 1), :] for u, e in heads]
            m_old = [m_sc[ps[u], e] for u, e in heads]
            l_old = [l_sc[ps[u], e] for u, e in heads]
            a_old = [acc_sc[ps[u], e * hd:(e + 1) * hd, :] for u, e in heads]
            ss = [jnp.where(rows[i] > 0.5, _bdot(kbl[u], qsts[i]), NEG_INF)
                  for i, (u, e) in enumerate(heads)]
            m_new = [jnp.maximum(m, jnp.max(s, axis=0, keepdims=True)) for m, s in zip(m_old, ss)]
            alphas = [jnp.exp2(m - mn) for m, mn in zip(m_old, m_new)]
            pes = [jnp.exp2(s - mn) for s, mn in zip(ss, m_new)]
            pvs = [_bdot(vsum(vbl[u], e), pes[i].astype(BF16)) for i, (u, e) in enumerate(heads)]
            l_new = [a * l + pv[hd:hd + 1] for a, l, pv in zip(alphas, l_old, pvs)]
            a_new = [a * ao + pv[:hd] for a, ao, pv in zip(alphas, a_old, pvs)]
            for i, (u, e) in enumerate(heads):
                p = ps[u]
                m_sc[p, e] = m_new[i]
                l_sc[p, e] = l_new[i]
                acc_sc[p, e * hd:(e + 1) * hd, :] = a_new[i]
            return c2

        lax.fori_loop(0, npair // grp, group_body, 0)
        return carry

    lax.fori_loop(0, qb, kb_body, 0)

    parts = []
    for p in range(npair):
        for e in range(2):
            parts.append(acc_sc[p, e * hd:(e + 1) * hd, :] / l_sc[p, e])
    ot = jnp.concatenate(parts, axis=0)
    return x + g_ref[0] * _bdot(ot.T.astype(BF16), wo_ref[...])


def _attn_mixer(mods, nw, w_q, w_o, k5, vt5, km2):
    sh, sc, gt = mods
    d = w_q.shape[0]
    npair, nb = k5.shape[1], k5.shape[2]
    nbp = km2.shape[2]
    n_sel = min(B_TOPK, nb - 1)
    per = TM_SORT // B_BLOCK
    vec = pl.BlockSpec((1, 1, d), lambda b, j: (b, 0, 0))
    full2 = lambda shape: pl.BlockSpec(shape, lambda b, j: (0, 0))
    specs = [vec, vec, vec, full2((1, d)), full2((d, d)), full2((d, d)),
             pl.BlockSpec((1, npair, nb, B_BLOCK, LANES), lambda b, j: (b, 0, 0, 0, 0)),
             pl.BlockSpec((1, npair, nb, LANES, B_BLOCK), lambda b, j: (b, 0, 0, 0, 0)),
             pl.BlockSpec((1, npair, nbp, LANES), lambda b, j: (b, 0, 0, 0))]
    args = [sh, sc, gt, nw.reshape(1, d), w_q.T.astype(BF16), w_o.astype(BF16), k5, vt5, km2]
    scratch = [pltpu.VMEM((npair, LANES, B_BLOCK), F32),
               pltpu.VMEM((npair, 2, LANES, B_BLOCK), BF16),
               pltpu.VMEM((npair, LANES, B_BLOCK), F32),
               pltpu.VMEM((npair, 2, nbp, B_BLOCK), F32),
               pltpu.VMEM((npair, 2, 1, B_BLOCK), F32),
               pltpu.VMEM((npair, 2, 1, B_BLOCK), F32)]

    def body(x_ref, refs, scr):
        j = pl.program_id(1)
        outs = [_attn_body(x_ref[0, h * B_BLOCK:(h + 1) * B_BLOCK, :], j * per + h, *refs, *scr,
                           nb=nb, n_sel=n_sel) for h in range(per)]
        return jnp.concatenate(outs, axis=0)

    return body, specs, args, scratch


def kernel(x, c, ada_w, ada_b, norm_mix, norm_ffn, a_w_in, a_b_in, a_ln_g, a_ln_b, a_w_s, a_b_s,
           a_w_out, kv_norm, kv_ada_w, kv_ada_b, kv_w_k, kv_w_v, b_w_q, b_w_o, moe_router, moe_bias,
           moe_w_gate, moe_w_up, moe_w_down, sh_w_gate, sh_w_up, sh_w_down, final_norm):
    bsz, s, d = x.shape
    depth = ada_w.shape[0]
    n_a = a_w_in.shape[0]
    assert s % B_BLOCK == 0 and s % TM_SORT == 0 and TM_SORT % A_CHUNK == 0 and d % LANES == 0
    nb = s // B_BLOCK
    npair = d // LANES
    nbp = -(-nb // 8) * 8

    def split(m, n):
        return [m[:, i * d:(i + 1) * d].reshape(bsz, 1, d) for i in range(n)]

    def pair_major(km):
        km = km.reshape(bsz, nb, npair, LANES).transpose(0, 2, 1, 3)
        return jnp.pad(km, ((0, 0), (0, 0), (0, nbp - nb), (0, 0)))

    layer_mods = _ada(c, ada_w, ada_b)
    k5 = vt5 = km2 = None
    for i in range(depth):
        sh1, sc1, g1, sh2, sc2, g2 = split(layer_mods[i], 6)
        if i < n_a:
            mixer = _gmlp_mixer((sh1, sc1, g1), norm_mix[i], a_w_in[i], a_b_in[i], a_ln_g[i], a_ln_b[i],
                                a_w_s[i], a_b_s[i], a_w_out[i])
        else:
            if k5 is None:
                ksh, ksc = split(_ada(c, kv_ada_w[None], kv_ada_b[None])[0], 2)
                k5, vt5, km = _kv(x, ksh, ksc, kv_norm, kv_w_k, kv_w_v)
                km2 = pair_major(km)
            j = i - n_a
            mixer = _attn_mixer((sh1, sc1, g1), norm_mix[i], b_w_q[j], b_w_o[j], k5, vt5, km2)
        kv = None
        if i == n_a - 1 and i + 1 < depth:
            ksh, ksc = split(_ada(c, kv_ada_w[None], kv_ada_b[None])[0], 2)
            kv = (ksh, ksc, kv_norm, kv_w_k, kv_w_v)
        res = _moe(x, (sh2, sc2, g2), norm_ffn[i], moe_router[i], moe_bias[i], i, moe_w_gate,
                   moe_w_up, moe_w_down, sh_w_gate[i], sh_w_up[i], sh_w_down[i],
                   final_norm, i == depth - 1, mixer, kv)
        if kv is not None:
            x, k5, vt5, km = res
            km2 = pair_major(km)
        else:
            x = res
    return x
```

```python
import functools

import jax
import jax.numpy as jnp
from jax import lax
from jax.experimental import pallas as pl
from jax.experimental.pallas import tpu as pltpu

F32 = jnp.float32
BF16 = jnp.bfloat16
I32 = jnp.int32

RMS_EPS = 1e-6
LN_EPS = 1e-5
NEG_INF = -1e30

A_CHUNK = 128
A_GROUPS = 8
B_HEADS = 16
B_BLOCK = 256
B_TOPK = 3
N_EXPERTS = 64
TOP_K = 8
N_GROUPS = 8
TOPK_GROUPS = 4
ROUTED_SCALE = 2.5

LANES = 128
VMEM_LIMIT = 56 * 1024 * 1024

TM_SORT = 512
RUN_ALIGN = 16
SORT_CHUNK = 256
ROWS_TILE_USED = TM_SORT * TOP_K + N_EXPERTS * (RUN_ALIGN - 1)
ROWS_TILE = -(-ROWS_TILE_USED // SORT_CHUNK) * SORT_CHUNK
PIECES_TILE = ROWS_TILE // RUN_ALIGN
PIECES_PAD = -(-PIECES_TILE // LANES) * LANES
BM_EXPERT = 2048
BM_SEG = 256
BM_CHAIN = 256
EXPERT_CHAINS = 4
ATTN_PAIR_UNROLL = 4
ATTN_PAIR_UNROLL_PAST = 8
SUM_ROWS = 16


def _cparams(sem):
    return pltpu.CompilerParams(dimension_semantics=sem, vmem_limit_bytes=VMEM_LIMIT)


def _sigmoid(x):
    return 1.0 / (1.0 + jnp.exp(-x))


def _silu(x):
    return x * _sigmoid(x)


def _gelu_tanh(x):
    hx = 0.5 * x
    return hx + hx * jnp.tanh(x * (0.7978845608028654 + 0.035677408136300125 * (x * x)))


def _rms(x, g):
    return x * lax.rsqrt(jnp.mean(x * x, axis=-1, keepdims=True) + RMS_EPS) * g


def _bdot(a, b):
    return jnp.dot(a, b, preferred_element_type=F32)


def _bdot_nt(a, b):
    return lax.dot_general(a, b, (((1,), (1,)), ((), ())), preferred_element_type=F32)


def _split(a):
    hi = a.astype(BF16)
    lo = (a - hi.astype(F32)).astype(BF16)
    return hi, lo


def _dot3_nt(a, b):
    ah, al = _split(a)
    bh, bl = _split(b)
    return _bdot_nt(ah, bh) + (_bdot_nt(ah, bl) + _bdot_nt(al, bh))


def _dot3(a, b):
    ah, al = _split(a)
    bh, bl = _split(b)
    return _bdot(ah, bh) + (_bdot(ah, bl) + _bdot(al, bh))


def _ada_kernel(c_ref, w_ref, b_ref, o_ref):
    a = _silu(c_ref[...]).astype(BF16)
    o_ref[0] = _bdot(a, w_ref[0].astype(BF16)) + b_ref[0]


def _ada(c, w, b):
    bsz, d = c.shape
    nl, _, n = w.shape
    tn = 1024
    return pl.pallas_call(
        _ada_kernel,
        grid=(nl, n // tn),
        in_specs=[pl.BlockSpec((bsz, d), lambda l, j: (0, 0)),
                  pl.BlockSpec((1, d, tn), lambda l, j: (l, 0, j)),
                  pl.BlockSpec((1, 1, tn), lambda l, j: (l, 0, j))],
        out_specs=pl.BlockSpec((1, bsz, tn), lambda l, j: (l, 0, j)),
        out_shape=jax.ShapeDtypeStruct((nl, bsz, n), F32),
        compiler_params=_cparams(("arbitrary", "arbitrary")),
    )(c, w, b.reshape(nl, 1, n))


def _gmlp_body(x, sh_ref, sc_ref, g_ref, nw_ref, win_ref, bin_ref, lng_ref, lnb_ref,
               ws_ref, bst_ref, wout_ref, y_sc):
    tm = x.shape[0]
    h = _rms(x, nw_ref[...]) * (1.0 + sc_ref[0]) + sh_ref[0]
    z = _gelu_tanh(_bdot(h.astype(BF16), win_ref[...]) + bin_ref[...])
    aw = z.shape[1] // 2
    gd = aw // A_GROUPS
    u = z[:, :aw]
    v = z[:, aw:]
    mu = jnp.mean(v, axis=-1, keepdims=True)
    dv = v - mu
    var = jnp.mean(dv * dv, axis=-1, keepdims=True)
    vn = (dv * lax.rsqrt(var + LN_EPS) * lng_ref[...] + lnb_ref[...]).astype(BF16)
    row = lax.broadcasted_iota(I32, (A_CHUNK, A_CHUNK), 0)
    col = lax.broadcasted_iota(I32, (A_CHUNK, A_CHUNK), 1)
    causal = col <= row
    for g in range(A_GROUPS):
        wg = jnp.where(causal, ws_ref[g], 0.0).astype(BF16)
        bcol = bst_ref[:, g:g + 1]
        for ci in range(tm // A_CHUNK):
            rs = slice(ci * A_CHUNK, (ci + 1) * A_CHUNK)
            cs = slice(g * gd, (g + 1) * gd)
            sv = _bdot(wg, vn[rs, cs]) + bcol
            y_sc[rs, cs] = (u[rs, cs] * sv).astype(BF16)
    return x + g_ref[0] * _bdot(y_sc[...], wout_ref[...])


def _route(x, sh_ref, sc_ref, g_ref, nw_ref, wrt_ref, bias_ref, wsg_ref, wsu_ref, wsd_ref,
           h_ref, base_ref, key_ref, wkey_ref, cnt_ref, pe_ref, po_ref):
    tm = x.shape[0]
    h = _rms(x, nw_ref[...]) * (1.0 + sc_ref[0]) + sh_ref[0]
    hb = h.astype(BF16)
    h_ref[...] = hb
    act = (_silu(_bdot(hb, wsg_ref[...])) * _bdot(hb, wsu_ref[...])).astype(BF16)
    base_ref[...] = x + g_ref[0] * _bdot(act, wsd_ref[...])

    scores = _sigmoid(_dot3_nt(wrt_ref[...], h))
    choice = scores + bias_ref[...]
    gsz = N_EXPERTS // N_GROUPS
    sub = lax.broadcasted_iota(I32, (gsz, tm), 0)
    blocks = [choice[g * gsz:(g + 1) * gsz] for g in range(N_GROUPS)]
    gscore = []
    for blk in blocks:
        m1 = jnp.max(blk, axis=0, keepdims=True)
        i1 = jnp.min(jnp.where(blk == m1, sub, gsz), axis=0, keepdims=True)
        m2 = jnp.max(jnp.where(sub == i1, -jnp.inf, blk), axis=0, keepdims=True)
        gscore.append(m1 + m2)
    masked = []
    for g in range(N_GROUPS):
        beats = jnp.zeros((1, tm), F32)
        for m in range(N_GROUPS):
            if m == g:
                continue
            b = gscore[m] > gscore[g]
            if m < g:
                b = jnp.logical_or(b, gscore[m] == gscore[g])
            beats = beats + b.astype(F32)
        masked.append(jnp.where(beats < TOPK_GROUPS, blocks[g], NEG_INF))
    cur = jnp.concatenate(masked, axis=0)
    eio = lax.broadcasted_iota(I32, (N_EXPERTS, tm), 0)
    chosen = jnp.zeros((N_EXPERTS, tm), jnp.bool_)
    wsum = jnp.zeros((1, tm), F32)
    for _ in range(TOP_K):
        m = jnp.max(cur, axis=0, keepdims=True)
        idx = jnp.min(jnp.where(cur == m, eio, N_EXPERTS), axis=0, keepdims=True)
        sel = eio == idx
        chosen = jnp.logical_or(chosen, sel)
        wsum = wsum + jnp.sum(jnp.where(sel, scores, 0.0), axis=0, keepdims=True)
        cur = jnp.where(sel, -jnp.inf, cur)
    wkey_ref[...] = jnp.where(chosen, scores / wsum * ROUTED_SCALE, 0.0)

    onehot = chosen.astype(BF16)
    r_i = lax.broadcasted_iota(I32, (tm, tm), 0)
    c_i = lax.broadcasted_iota(I32, (tm, tm), 1)
    before = (r_i < c_i).astype(BF16)
    prior = _bdot(onehot, before)
    key_ref[...] = jnp.where(chosen, prior, -1.0).astype(I32)
    cnt = jnp.sum(chosen.astype(F32), axis=1, keepdims=True)
    cnt_ref[0] = cnt

    run_p = jnp.floor((cnt + (RUN_ALIGN - 1)) / RUN_ALIGN)
    ppc = SORT_CHUNK // RUN_ALIGN
    tot = jnp.sum(run_p, axis=0, keepdims=True)
    fill = jnp.ceil(tot / ppc) * ppc - tot
    run_p = run_p + jnp.where(lax.broadcasted_iota(I32, (N_EXPERTS, 1), 0) == N_EXPERTS - 1, fill, 0.0)
    e_r = lax.broadcasted_iota(I32, (N_EXPERTS, N_EXPERTS), 0)
    e_c = lax.broadcasted_iota(I32, (N_EXPERTS, N_EXPERTS), 1)
    incl = (e_c <= e_r).astype(BF16)
    lend = _bdot(incl, jnp.broadcast_to(run_p, (N_EXPERTS, LANES)).astype(BF16))[:, 0:1]
    loff = lend - run_p
    pj = lax.broadcasted_iota(I32, (N_EXPERTS, PIECES_PAD), 1).astype(F32)
    er = lax.broadcasted_iota(I32, (N_EXPERTS, PIECES_PAD), 0).astype(F32)
    pe = jnp.minimum(jnp.sum((lend <= pj).astype(F32), axis=0, keepdims=True), N_EXPERTS - 1.0)
    lo = jnp.sum(jnp.where(er == pe, loff, 0.0), axis=0, keepdims=True)
    pe_ref[0] = pe.astype(I32)
    po_ref[0] = ((pj[0:1, :] - lo) * RUN_ALIGN).astype(I32)


N_ROUTE_REFS = 16


def _router_kernel(x_ref, *refs):
    _route(x_ref[0], *refs)


def _mixer_router_kernel(x_ref, *refs, body, n_mix):
    route_refs = refs[n_mix:n_mix + N_ROUTE_REFS]
    x1 = body(x_ref, refs[:n_mix], refs[n_mix + N_ROUTE_REFS:])
    _route(x1, *route_refs)


def _gmlp_mixer(mods, nw, w_in, b_in, ln_g, ln_b, w_s, b_s, w_out):
    sh, sc, gt = mods
    d, n_in = w_in.shape
    aw = n_in // 2
    vec = pl.BlockSpec((1, 1, d), lambda b, j: (b, 0, 0))
    full2 = lambda shape: pl.BlockSpec(shape, lambda b, j: (0, 0))
    specs = [vec, vec, vec, full2((1, d)), full2((d, n_in)), full2((1, n_in)), full2((1, aw)),
             full2((1, aw)), pl.BlockSpec((A_GROUPS, A_CHUNK, A_CHUNK), lambda b, j: (0, 0, 0)),
             full2((A_CHUNK, A_GROUPS)), full2((aw, d))]
    args = [sh, sc, gt, nw.reshape(1, d), w_in.astype(BF16), b_in.reshape(1, n_in),
            ln_g.reshape(1, aw), ln_b.reshape(1, aw), w_s, b_s.T, w_out.astype(BF16)]

    def body(x_ref, refs, scratch):
        return _gmlp_body(x_ref[0], *refs, *scratch)

    return body, specs, args, [pltpu.VMEM((TM_SORT, aw), BF16)]


def _router(x, mods, nw, w_router, e_bias, wsg, wsu, wsd, mixer=None):
    bsz, s, d = x.shape
    t = bsz * s
    tm = TM_SORT
    nt = s // tm
    sd = wsg.shape[1]
    vec = pl.BlockSpec((1, 1, d), lambda b, j: (b, 0, 0))
    full2 = lambda shape: pl.BlockSpec(shape, lambda b, j: (0, 0))
    tok = pl.BlockSpec((tm, d), lambda b, j: (b * nt + j, 0))
    etok = pl.BlockSpec((N_EXPERTS, tm), lambda b, j: (0, b * nt + j))
    ptab = pl.BlockSpec((1, 1, PIECES_PAD), lambda b, j: (b * nt + j, 0, 0))
    sh, sc, gt = mods
    in_specs = [vec, vec, vec, full2((1, d)), full2((N_EXPERTS, d)), full2((N_EXPERTS, 1)),
                full2((d, sd)), full2((d, sd)), full2((sd, d))]
    args = [sh, sc, gt, nw.reshape(1, d), w_router.T, e_bias.reshape(N_EXPERTS, 1),
            wsg.astype(BF16), wsu.astype(BF16), wsd.astype(BF16)]
    kern, scratch = _router_kernel, []
    if mixer is not None:
        body, mspecs, margs, scratch = mixer
        kern = functools.partial(_mixer_router_kernel, body=body, n_mix=len(margs))
        in_specs = mspecs + in_specs
        args = margs + args
    return pl.pallas_call(
        kern,
        grid=(bsz, nt),
        in_specs=[pl.BlockSpec((1, tm, d), lambda b, j: (b, j, 0))] + in_specs,
        out_specs=[tok, tok, etok, etok,
                   pl.BlockSpec((1, N_EXPERTS, 1), lambda b, j: (b * nt + j, 0, 0)), ptab, ptab],
        out_shape=[jax.ShapeDtypeStruct((t, d), BF16), jax.ShapeDtypeStruct((t, d), F32),
                   jax.ShapeDtypeStruct((N_EXPERTS, t), I32), jax.ShapeDtypeStruct((N_EXPERTS, t), F32),
                   jax.ShapeDtypeStruct((t // tm, N_EXPERTS, 1), F32),
                   jax.ShapeDtypeStruct((t // tm, 1, PIECES_PAD), I32),
                   jax.ShapeDtypeStruct((t // tm, 1, PIECES_PAD), I32)],
        scratch_shapes=scratch,
        compiler_params=_cparams(("arbitrary", "arbitrary")),
    )(x, *args)


def _rows_max(t):
    rows = (t // TM_SORT) * ROWS_TILE + N_EXPERTS * (BM_SEG - RUN_ALIGN)
    return -(-rows // BM_EXPERT) * BM_EXPERT


def _sort_meta(cnt, n_rows_max):
    nts = cnt.shape[0]
    c = cnt.reshape(nts, N_EXPERTS).astype(I32)
    run = (c + (RUN_ALIGN - 1)) // RUN_ALIGN * RUN_ALIGN
    fill = (-jnp.sum(run, axis=1, keepdims=True)) % SORT_CHUNK
    run = jnp.concatenate([run[:, :-1], run[:, -1:] + fill], axis=1)
    used = jnp.sum(run, axis=1)
    per_e = jnp.sum(run, axis=0)
    seg = (per_e + (BM_SEG - 1)) // BM_SEG * BM_SEG
    ends = jnp.cumsum(seg)
    starts = ends - seg
    goff = starts[None, :] + jnp.cumsum(run, axis=0) - run
    pad_start = starts + per_e
    pad_cnt = (seg - per_e) // RUN_ALIGN

    nh = n_rows_max // BM_SEG
    hidx = jnp.arange(nh, dtype=I32)
    he = jnp.sum((ends[None, :] <= (hidx * BM_SEG)[:, None]).astype(I32), axis=1)
    real = he < N_EXPERTS
    he = jnp.minimum(he, N_EXPERTS - 1).astype(I32)
    prev = jnp.concatenate([jnp.full((1,), -1, I32), he[:-1]])
    hnew = jnp.logical_and(real, he != prev)
    per_blk = BM_EXPERT // BM_SEG
    hslot = (jnp.cumsum(hnew.astype(I32)) - 1) % per_blk
    later = jnp.logical_and(hidx[None, :] > hidx[:, None], hnew[None, :])
    nxt = jnp.min(jnp.where(later, hidx[None, :], nh), axis=1)
    hfetch = jnp.where(nxt < nh, he[jnp.minimum(nxt, nh - 1)], -1)
    bx = jnp.where(real[::per_blk], jnp.arange(nh // per_blk, dtype=I32), 0)
    i32 = lambda a: a.reshape(-1).astype(I32)
    return ((i32(goff), i32(used), i32(pad_start), i32(pad_cnt)),
            (bx, he, i32(hnew), i32(real), i32(hslot), i32(hfetch)))


def _piece_copies(pe_ref, po_ref, goff_ref, used_ref, tile, local, remote, sem, to_remote):
    def copy(j):
        idx = tile * PIECES_PAD + j
        l0 = pl.multiple_of(j * RUN_ALIGN, RUN_ALIGN)
        g0 = pl.multiple_of(goff_ref[tile * N_EXPERTS + pe_ref[idx]] + po_ref[idx], RUN_ALIGN)
        lref = local.at[pl.ds(l0, RUN_ALIGN)]
        gref = remote.at[pl.ds(g0, RUN_ALIGN)]
        return pltpu.make_async_copy(lref, gref, sem) if to_remote else pltpu.make_async_copy(gref, lref, sem)

    def wait_all():
        def body(j, carry):
            copy(j).wait()
            return carry

        lax.fori_loop(0, used_ref[tile] // RUN_ALIGN, body, 0)

    return copy, wait_all


def _pad_copies(pad_start_ref, pad_cnt_ref, tile, ntiles, zeros, remote, sem):
    share = -(-N_EXPERTS // ntiles)

    def apply(act):
        for q in range(share):
            e = tile * share + q
            ec = jnp.minimum(e, N_EXPERTS - 1)
            n = jnp.where(e < N_EXPERTS, pad_cnt_ref[ec], 0)
            base = pad_start_ref[ec]

            def body(j, carry):
                g0 = pl.multiple_of(base + j * RUN_ALIGN, RUN_ALIGN)
                getattr(pltpu.make_async_copy(zeros, remote.at[pl.ds(g0, RUN_ALIGN)], sem), act)()
                return carry

            lax.fori_loop(0, n, body, 0)

    return apply


def _piece_rows(pe_ref, po_ref, key_ref, first_piece, npieces, val_ref=None):
    tm = key_ref.shape[1]
    sub = lax.broadcasted_iota(I32, (RUN_ALIGN, tm), 0)
    out = []
    for jj in range(npieces):
        j = first_piece + jj
        e = pe_ref[j]
        hit = (key_ref[pl.ds(e, 1), :] - po_ref[j]) == sub
        val = 1.0 if val_ref is None else val_ref[pl.ds(e, 1), :]
        out.append(jnp.where(hit, val, 0.0))
    return out


def _dispatch_kernel(pe_ref, po_ref, goff_ref, used_ref, pad_start_ref, pad_cnt_ref,
                     key_ref, h_ref, xs_ref, xbuf, p_sc, zbuf, sem, zsem, *, ntiles):
    i = pl.program_id(0)
    hb = h_ref[...]
    ch = SORT_CHUNK
    ppc = ch // RUN_ALIGN
    zbuf[...] = jnp.zeros_like(zbuf)
    pads = _pad_copies(pad_start_ref, pad_cnt_ref, i, ntiles, zbuf, xs_ref, zsem)
    pads("start")

    copy, wait_all = _piece_copies(pe_ref, po_ref, goff_ref, used_ref, i, xbuf, xs_ref, sem, True)

    def build(ci):
        rows = _piece_rows(pe_ref, po_ref, key_ref, i * PIECES_PAD + ci * ppc, ppc)
        for jj, p in enumerate(rows):
            p_sc[jj * RUN_ALIGN:(jj + 1) * RUN_ALIGN, :] = p.astype(BF16)
        xbuf[pl.ds(pl.multiple_of(ci * ch, ch), ch), :] = _bdot(p_sc[...], hb).astype(BF16)

    def send(ci):
        for jj in range(ppc):
            copy(ci * ppc + jj).start()

    def step(ci, carry):
        send(ci - 1)
        build(ci)
        return carry

    nchunk = used_ref[i] // ch
    build(0)
    lax.fori_loop(1, nchunk, step, 0)
    send(nchunk - 1)
    pads("wait")
    wait_all()


def _dispatch(meta, pe, po, key, hb, n_rows_max):
    t, d = hb.shape
    tm = TM_SORT
    goff, used, pad_start, pad_cnt = meta
    return pl.pallas_call(
        functools.partial(_dispatch_kernel, ntiles=t // tm),
        grid_spec=pltpu.PrefetchScalarGridSpec(
            num_scalar_prefetch=6, grid=(t // tm,),
            in_specs=[pl.BlockSpec((N_EXPERTS, tm), lambda i, *_: (0, i)),
                      pl.BlockSpec((tm, d), lambda i, *_: (i, 0))],
            out_specs=pl.BlockSpec(memory_space=pl.ANY),
            scratch_shapes=[pltpu.VMEM((ROWS_TILE, d), BF16), pltpu.VMEM((SORT_CHUNK, tm), BF16),
                            pltpu.VMEM((RUN_ALIGN, d), BF16),
                            pltpu.SemaphoreType.DMA(()), pltpu.SemaphoreType.DMA(())]),
        out_shape=jax.ShapeDtypeStruct((n_rows_max, d), BF16),
        compiler_params=_cparams(("arbitrary",)),
    )(pe, po, goff, used, pad_start, pad_cnt, key, hb)


def _expert_kernel(bx_ref, he_ref, hnew_ref, hreal_ref, hslot_ref, hfetch_ref, x_ref, wg_hbm, wu_hbm, wd_hbm,
                   o_ref, wg_st, wu_st, wd_st, wg_sc, wu_sc, wd_sc, sem, *, layer):
    b = pl.program_id(0)
    per_blk = BM_EXPERT // BM_SEG
    cpp = BM_SEG // BM_CHAIN

    def fetch(e):
        return [pltpu.make_async_copy(src.at[layer, e], dst, sem.at[i])
                for i, (src, dst) in enumerate(((wg_hbm, wg_st), (wu_hbm, wu_st), (wd_hbm, wd_st)))]

    @pl.when(b == 0)
    def _():
        for cp in fetch(he_ref[0]):
            cp.start()

    for part in range(per_blk):
        h = b * per_blk + part

        @pl.when(hnew_ref[h] == 1)
        def _():
            for cp in fetch(he_ref[h]):
                cp.wait()
            s = hslot_ref[h]
            wg_sc[s] = wg_st[...].astype(BF16)
            wu_sc[s] = wu_st[...].astype(BF16)
            wd_sc[s] = wd_st[...].astype(BF16)

            @pl.when(hfetch_ref[h] >= 0)
            def _():
                for cp in fetch(hfetch_ref[h]):
                    cp.start()

    def run_parts(parts):
        every = [(slice((p * cpp + c) * BM_CHAIN, (p * cpp + c + 1) * BM_CHAIN), hslot_ref[b * per_blk + p])
                 for p in parts for c in range(cpp)]
        for g0 in range(0, len(every), EXPERT_CHAINS):
            chains = every[g0:g0 + EXPERT_CHAINS]
            xs = [x_ref[rs, :] for rs, _ in chains]
            gs = [_bdot(x, wg_sc[s]) for x, (_, s) in zip(xs, chains)]
            us = [_bdot(x, wu_sc[s]) for x, (_, s) in zip(xs, chains)]
            acts = [(_silu(g) * u).astype(BF16) for g, u in zip(gs, us)]
            for (rs, s), act in zip(chains, acts):
                o_ref[rs, :] = _bdot(act, wd_sc[s]).astype(BF16)

    nlive = hreal_ref[b * per_blk]
    for part in range(1, per_blk):
        nlive = nlive + hreal_ref[b * per_blk + part]

    @pl.when(nlive == per_blk)
    def _():
        run_parts(range(per_blk))

    @pl.when(nlive < per_blk)
    def _():
        for part in range(per_blk):
            @pl.when(part < nlive)
            def _():
                run_parts([part])

            @pl.when(part >= nlive)
            def _():
                o_ref[part * BM_SEG:(part + 1) * BM_SEG, :] = jnp.zeros((BM_SEG, o_ref.shape[1]), BF16)


def _experts(meta, xs, layer, w_gate, w_up, w_down):
    r, d = xs.shape
    ed = w_gate.shape[-1]
    bm = BM_EXPERT
    nslot = BM_EXPERT // BM_SEG
    omap = lambda b, bx, *_: (b, 0)
    xmap = lambda b, bx, *_: (bx[b], 0)
    hbm = pl.BlockSpec(memory_space=pl.ANY)
    return pl.pallas_call(
        functools.partial(_expert_kernel, layer=layer),
        grid_spec=pltpu.PrefetchScalarGridSpec(
            num_scalar_prefetch=6, grid=(r // bm,),
            in_specs=[pl.BlockSpec((bm, d), xmap), hbm, hbm, hbm],
            out_specs=pl.BlockSpec((bm, d), omap),
            scratch_shapes=[pltpu.VMEM((d, ed), F32), pltpu.VMEM((d, ed), F32), pltpu.VMEM((ed, d), F32),
                            pltpu.VMEM((nslot, d, ed), BF16), pltpu.VMEM((nslot, d, ed), BF16),
                            pltpu.VMEM((nslot, ed, d), BF16), pltpu.SemaphoreType.DMA((3,))]),
        out_shape=jax.ShapeDtypeStruct((r, d), BF16),
        compiler_params=_cparams(("arbitrary",)),
    )(*meta, xs, w_gate, w_up, w_down)


def _combine_kernel(pe_ref, po_ref, goff_ref, used_ref, key_ref, wkey_ref, ys_ref, base_ref, g_ref,
                    fn_ref, *rest, final, with_kv):
    if with_kv:
        kv_in, o_ref, kv_out, (ybuf, q_sc, sem) = rest[:5], rest[5], rest[6:9], rest[9:]
    else:
        o_ref, ybuf, q_sc, sem = rest
    i = pl.program_id(0)
    tm = base_ref.shape[0]
    ch = SORT_CHUNK
    ppc = ch // RUN_ALIGN

    @pl.when(i == 0)
    def _():
        ybuf[...] = jnp.zeros_like(ybuf)

    copy, wait_all = _piece_copies(pe_ref, po_ref, goff_ref, used_ref, i, ybuf, ys_ref, sem, False)

    for ci in range(ybuf.shape[0] // ch):
        cs = slice(ci * ch, (ci + 1) * ch)

        @pl.when(ci * ch < used_ref[i])
        def _():
            for jj in range(ppc):
                copy(ci * ppc + jj).start()
            rows = _piece_rows(pe_ref, po_ref, key_ref, i * PIECES_PAD + ci * ppc, ppc, wkey_ref)
            for jj, row in enumerate(rows):
                q_sc[ci * ch + jj * RUN_ALIGN:ci * ch + (jj + 1) * RUN_ALIGN, :] = row.astype(BF16)

        @pl.when(ci * ch >= used_ref[i])
        def _():
            q_sc[cs, :] = jnp.zeros((ch, tm), BF16)

    wait_all()
    out = base_ref[...] + g_ref[0] * lax.dot_general(q_sc[...], ybuf[...], (((0,), (0,)), ((), ())),
                                                     preferred_element_type=F32)
    if final:
        out = _rms(out, fn_ref[...])
    o_ref[...] = out
    if with_kv:
        _kv_body(out, *kv_in, *kv_out)


def _combine(meta, pe, po, key, wkey, ys, base, gt, fnorm, seq, final, kv=None):
    t, d = base.shape
    tm = TM_SORT
    nt = seq // tm
    bsz = t // seq
    vecb = pl.BlockSpec((1, 1, d), lambda i, *_: (i // nt, 0, 0))
    full2 = lambda shape: pl.BlockSpec(shape, lambda i, *_: (0, 0))
    in_specs = [pl.BlockSpec((N_EXPERTS, tm), lambda i, *_: (0, i)),
                pl.BlockSpec((N_EXPERTS, tm), lambda i, *_: (0, i)),
                pl.BlockSpec(memory_space=pl.ANY),
                pl.BlockSpec((tm, d), lambda i, *_: (i, 0)), vecb, full2((1, d))]
    args = [key, wkey, ys, base, gt, fnorm.reshape(1, d)]
    out_specs = [pl.BlockSpec((tm, d), lambda i, *_: (i, 0))]
    out_shape = [jax.ShapeDtypeStruct((t, d), F32)]
    if kv is not None:
        ksh, ksc, knw, w_k, w_v = kv
        nb, npair, per = seq // B_BLOCK, d // LANES, tm // B_BLOCK
        in_specs += [vecb, vecb, full2((1, d)), full2((d, d)), full2((d, d))]
        args += [ksh, ksc, knw.reshape(1, d), w_k.astype(BF16), w_v.T.astype(BF16)]
        out_specs += [pl.BlockSpec((1, npair, per, B_BLOCK, LANES), lambda i, *_: (i // nt, 0, i % nt, 0, 0)),
                      pl.BlockSpec((1, npair, per, LANES, B_BLOCK), lambda i, *_: (i // nt, 0, i % nt, 0, 0)),
                      pl.BlockSpec((1, per, 1, d), lambda i, *_: (i // nt, i % nt, 0, 0))]
        out_shape += [jax.ShapeDtypeStruct((bsz, npair, nb, B_BLOCK, LANES), BF16),
                      jax.ShapeDtypeStruct((bsz, npair, nb, LANES, B_BLOCK), BF16),
                      jax.ShapeDtypeStruct((bsz, nb, 1, d), F32)]
    res = pl.pallas_call(
        functools.partial(_combine_kernel, final=final, with_kv=kv is not None),
        grid_spec=pltpu.PrefetchScalarGridSpec(
            num_scalar_prefetch=4, grid=(t // tm,),
            in_specs=in_specs, out_specs=out_specs,
            scratch_shapes=[pltpu.VMEM((ROWS_TILE, d), BF16), pltpu.VMEM((ROWS_TILE, tm), BF16),
                            pltpu.SemaphoreType.DMA(())]),
        out_shape=out_shape,
        compiler_params=_cparams(("arbitrary",)),
    )(pe, po, meta[0], meta[1], *args)
    return res if kv is not None else res[0]


def _moe(x, mods, nw, w_router, e_bias, layer, w_gate, w_up, w_down, wsg, wsu, wsd, fnorm, final, mixer, kv=None):
    bsz, s, d = x.shape
    t = bsz * s
    gt = mods[2]
    n_rows_max = _rows_max(t)
    hb, base, key, wkey, cnt, pe, po = _router(x, mods, nw, w_router, e_bias, wsg, wsu, wsd, mixer)
    pe = pe.reshape(-1)
    po = po.reshape(-1)
    layout_meta, block_meta = _sort_meta(cnt, n_rows_max)
    xs = _dispatch(layout_meta, pe, po, key, hb, n_rows_max)
    ys = _experts(block_meta, xs, layer, w_gate, w_up, w_down)
    out = _combine(layout_meta, pe, po, key, wkey, ys, base, gt, fnorm, s, final, kv)
    if kv is not None:
        return (out[0].reshape(bsz, s, d),) + tuple(out[1:])
    return out.reshape(bsz, s, d)


def _kv_body(x, sh_ref, sc_ref, nw_ref, wk_ref, wvt_ref, k_ref, vt_ref, km_ref):
    hb = (_rms(x, nw_ref[...]) * (1.0 + sc_ref[0]) + sh_ref[0]).astype(BF16)
    k = _bdot(hb, wk_ref[...])
    vt = _bdot_nt(wvt_ref[...], hb)
    for blk in range(x.shape[0] // B_BLOCK):
        rs = slice(blk * B_BLOCK, (blk + 1) * B_BLOCK)
        for p in range(k.shape[1] // LANES):
            k_ref[0, p, blk] = k[rs, p * LANES:(p + 1) * LANES].astype(BF16)
            vt_ref[0, p, blk] = vt[p * LANES:(p + 1) * LANES, rs].astype(BF16)
        km_ref[0, blk] = jnp.mean(k[rs], axis=0, keepdims=True)


def _kv_kernel(x_ref, *refs):
    _kv_body(x_ref[0], *refs)


def _kv(x, sh, sc, nw, w_k, w_v):
    bsz, s, d = x.shape
    nb = s // B_BLOCK
    npair = d // LANES
    vec = pl.BlockSpec((1, 1, d), lambda b, j: (b, 0, 0))
    full2 = lambda shape: pl.BlockSpec(shape, lambda b, j: (0, 0))
    return pl.pallas_call(
        _kv_kernel,
        grid=(bsz, nb),
        in_specs=[pl.BlockSpec((1, B_BLOCK, d), lambda b, j: (b, j, 0)), vec, vec,
                  full2((1, d)), full2((d, d)), full2((d, d))],
        out_specs=[pl.BlockSpec((1, npair, 1, B_BLOCK, LANES), lambda b, j: (b, 0, j, 0, 0)),
                   pl.BlockSpec((1, npair, 1, LANES, B_BLOCK), lambda b, j: (b, 0, j, 0, 0)),
                   pl.BlockSpec((1, 1, 1, d), lambda b, j: (b, j, 0, 0))],
        out_shape=[jax.ShapeDtypeStruct((bsz, npair, nb, B_BLOCK, LANES), BF16),
                   jax.ShapeDtypeStruct((bsz, npair, nb, LANES, B_BLOCK), BF16),
                   jax.ShapeDtypeStruct((bsz, nb, 1, d), F32)],
        compiler_params=_cparams(("arbitrary", "arbitrary")),
    )(x, sh, sc, nw.reshape(1, d), w_k.astype(BF16), w_v.T.astype(BF16))


def _attn_body(x, qb, sh_ref, sc_ref, g_ref, nw_ref, wqt_ref, wo_ref, k_ref, vt_ref, km_ref,
               qt_sc, qs_sc, acc_sc, sel_sc, m_sc, l_sc, *, nb, n_sel):
    bq = x.shape[0]
    npair = qt_sc.shape[0]
    nbp = km_ref.shape[2]
    hd = LANES // 2
    scale = float(hd) ** -0.5 * 1.4426950408889634
    h = _rms(x, nw_ref[...]) * (1.0 + sc_ref[0]) + sh_ref[0]
    qt = _bdot_nt(wqt_ref[...], h.astype(BF16))
    for p in range(npair):
        qt_sc[p] = qt[p * LANES:(p + 1) * LANES, :]

    subn = lax.broadcasted_iota(I32, (nbp, bq), 0)
    past = subn < qb
    krow = lax.broadcasted_iota(I32, (B_BLOCK, bq), 0)
    qcol = lax.broadcasted_iota(I32, (B_BLOCK, bq), 1)
    causal = krow <= qcol
    rowh = lax.broadcasted_iota(I32, (LANES, 1), 0)

    grp = ATTN_PAIR_UNROLL
    heads = [(u, e) for u in range(grp) for e in range(2)]
    ones_rows = jnp.ones((SUM_ROWS, B_BLOCK), BF16)

    def vsum(vt2, e):
        return jnp.concatenate([vt2[e * hd:(e + 1) * hd, :], ones_rows], axis=0)

    def own_body(gi, carry):
        ps = [gi * grp + u for u in range(grp)]
        q2ts = [qt_sc[p] for p in ps]
        kms = [km_ref[0, p] for p in ps]
        kown = [k_ref[0, p, qb] for p in ps]
        vown = [vt_ref[0, p, qb] for p in ps]
        qets = [jnp.where((rowh >= hd) if e == 1 else (rowh < hd), q2ts[u], 0.0) for u, e in heads]
        qsts = [(q * scale).astype(BF16) for q in qets]
        ss = [jnp.where(causal, _bdot(kown[u], qsts[i]), NEG_INF) for i, (u, e) in enumerate(heads)]
        gates = [_dot3(kms[u], qets[i]) for i, (u, e) in enumerate(heads)]
        ms = [jnp.max(s, axis=0, keepdims=True) for s in ss]
        pes = [jnp.exp2(s - m) for s, m in zip(ss, ms)]
        pvs = [_bdot(vsum(vown[u], e), pes[i].astype(BF16)) for i, (u, e) in enumerate(heads)]
        accs = [pv[:hd] for pv in pvs]
        ls = [pv[hd:hd + 1] for pv in pvs]
        sels = []
        for gate in gates:
            selt = jnp.zeros((nbp, bq), F32)
            for n in range(nb):
                gn = gate[n:n + 1, :]
                beats = jnp.logical_or(gate > gn, jnp.logical_and(gate == gn, subn < n))
                beats = jnp.logical_and(beats, past)
                cnt = jnp.sum(beats.astype(F32), axis=0, keepdims=True)
                selt = jnp.where(subn == n, (cnt < n_sel).astype(F32), selt)
            sels.append(selt)
        for i, (u, e) in enumerate(heads):
            p = ps[u]
            sel_sc[p, e] = sels[i]
            qs_sc[p, e] = qsts[i]
            m_sc[p, e] = ms[i]
            l_sc[p, e] = ls[i]
            acc_sc[p, e * hd:(e + 1) * hd, :] = accs[i]
        return carry

    lax.fori_loop(0, npair // grp, own_body, 0)

    grp = ATTN_PAIR_UNROLL_PAST
    heads = [(u, e) for u in range(grp) for e in range(2)]

    def kb_body(kb, carry):
        def group_body(gi, c2):
            ps = [gi * grp + u for u in range(grp)]
            kbl = [k_ref[0, p, kb] for p in ps]
            vbl = [vt_ref[0, p, kb] for p in ps]
            qsts = [qs_sc[ps[u], e] for u, e in heads]
            rows = [sel_sc[ps[u], e, pl.ds(kb, 1), :] for u, e in heads]
            m_old = [m_sc[ps[u], e] for u, e in heads]
            l_old = [l_sc[ps[u], e] for u, e in heads]
            a_old = [acc_sc[ps[u], e * hd:(e + 1) * hd, :] for u, e in heads]
            ss = [jnp.where(rows[i] > 0.5, _bdot(kbl[u], qsts[i]), NEG_INF)
                  for i, (u, e) in enumerate(heads)]
            m_new = [jnp.maximum(m, jnp.max(s, axis=0, keepdims=True)) for m, s in zip(m_old, ss)]
            alphas = [jnp.exp2(m - mn) for m, mn in zip(m_old, m_new)]
            pes = [jnp.exp2(s - mn) for s, mn in zip(ss, m_new)]
            pvs = [_bdot(vsum(vbl[u], e), pes[i].astype(BF16)) for i, (u, e) in enumerate(heads)]
            l_new = [a * l + pv[hd:hd + 1] for a, l, pv in zip(alphas, l_old, pvs)]
            a_new = [a * ao + pv[:hd] for a, ao, pv in zip(alphas, a_old, pvs)]
            for i, (u, e) in enumerate(heads):
                p = ps[u]
                m_sc[p, e] = m_new[i]
                l_sc[p, e] = l_new[i]
                acc_sc[p, e * hd:(e + 1) * hd, :] = a_new[i]
            return c2

        lax.fori_loop(0, npair // grp, group_body, 0)
        return carry

    lax.fori_loop(0, qb, kb_body, 0)

    parts = []
    for p in range(npair):
        for e in range(2):
            parts.append(acc_sc[p, e * hd:(e + 1) * hd, :] / l_sc[p, e])
    ot = jnp.concatenate(parts, axis=0)
    return x + g_ref[0] * _bdot(ot.T.astype(BF16), wo_ref[...])


def _attn_mixer(mods, nw, w_q, w_o, k5, vt5, km2):
    sh, sc, gt = mods
    d = w_q.shape[0]
    npair, nb = k5.shape[1], k5.shape[2]
    nbp = km2.shape[2]
    n_sel = min(B_TOPK, nb - 1)
    per = TM_SORT // B_BLOCK
    vec = pl.BlockSpec((1, 1, d), lambda b, j: (b, 0, 0))
    full2 = lambda shape: pl.BlockSpec(shape, lambda b, j: (0, 0))
    specs = [vec, vec, vec, full2((1, d)), full2((d, d)), full2((d, d)),
             pl.BlockSpec((1, npair, nb, B_BLOCK, LANES), lambda b, j: (b, 0, 0, 0, 0)),
             pl.BlockSpec((1, npair, nb, LANES, B_BLOCK), lambda b, j: (b, 0, 0, 0, 0)),
             pl.BlockSpec((1, npair, nbp, LANES), lambda b, j: (b, 0, 0, 0))]
    args = [sh, sc, gt, nw.reshape(1, d), w_q.T.astype(BF16), w_o.astype(BF16), k5, vt5, km2]
    scratch = [pltpu.VMEM((npair, LANES, B_BLOCK), F32),
               pltpu.VMEM((npair, 2, LANES, B_BLOCK), BF16),
               pltpu.VMEM((npair, LANES, B_BLOCK), F32),
               pltpu.VMEM((npair, 2, nbp, B_BLOCK), F32),
               pltpu.VMEM((npair, 2, 1, B_BLOCK), F32),
               pltpu.VMEM((npair, 2, 1, B_BLOCK), F32)]

    def body(x_ref, refs, scr):
        j = pl.program_id(1)
        outs = [_attn_body(x_ref[0, h * B_BLOCK:(h + 1) * B_BLOCK, :], j * per + h, *refs, *scr,
                           nb=nb, n_sel=n_sel) for h in range(per)]
        return jnp.concatenate(outs, axis=0)

    return body, specs, args, scratch


def kernel(x, c, ada_w, ada_b, norm_mix, norm_ffn, a_w_in, a_b_in, a_ln_g, a_ln_b, a_w_s, a_b_s,
           a_w_out, kv_norm, kv_ada_w, kv_ada_b, kv_w_k, kv_w_v, b_w_q, b_w_o, moe_router, moe_bias,
           moe_w_gate, moe_w_up, moe_w_down, sh_w_gate, sh_w_up, sh_w_down, final_norm):
    bsz, s, d = x.shape
    depth = ada_w.shape[0]
    n_a = a_w_in.shape[0]
    assert s % B_BLOCK == 0 and s % TM_SORT == 0 and TM_SORT % A_CHUNK == 0 and d % LANES == 0
    nb = s // B_BLOCK
    npair = d // LANES
    nbp = -(-nb // 8) * 8

    def split(m, n):
        return [m[:, i * d:(i + 1) * d].reshape(bsz, 1, d) for i in range(n)]

    def pair_major(km):
        km = km.reshape(bsz, nb, npair, LANES).transpose(0, 2, 1, 3)
        return jnp.pad(km, ((0, 0), (0, 0), (0, nbp - nb), (0, 0)))

    layer_mods = _ada(c, ada_w, ada_b)
    k5 = vt5 = km2 = None
    for i in range(depth):
        sh1, sc1, g1, sh2, sc2, g2 = split(layer_mods[i], 6)
        if i < n_a:
            mixer = _gmlp_mixer((sh1, sc1, g1), norm_mix[i], a_w_in[i], a_b_in[i], a_ln_g[i], a_ln_b[i],
                                a_w_s[i], a_b_s[i], a_w_out[i])
        else:
            if k5 is None:
                ksh, ksc = split(_ada(c, kv_ada_w[None], kv_ada_b[None])[0], 2)
                k5, vt5, km = _kv(x, ksh, ksc, kv_norm, kv_w_k, kv_w_v)
                km2 = pair_major(km)
            j = i - n_a
            mixer = _attn_mixer((sh1, sc1, g1), norm_mix[i], b_w_q[j], b_w_o[j], k5, vt5, km2)
        kv = None
        if i == n_a - 1 and i + 1 < depth:
            ksh, ksc = split(_ada(c, kv_ada_w[None], kv_ada_b[None])[0], 2)
            kv = (ksh, ksc, kv_norm, kv_w_k, kv_w_v)
        res = _moe(x, (sh2, sc2, g2), norm_ffn[i], moe_router[i], moe_bias[i], i, moe_w_gate,
                   moe_w_up, moe_w_down, sh_w_gate[i], sh_w_up[i], sh_w_down[i],
                   final_norm, i == depth - 1, mixer, kv)
        if kv is not None:
            x, k5, vt5, km = res
            km2 = pair_major(km)
        else:
            x = res
    return x
```

```python
import functools

import jax
import jax.numpy as jnp
from jax import lax
from jax.experimental import pallas as pl
from jax.experimental.pallas import tpu as pltpu

F32 = jnp.float32
BF16 = jnp.bfloat16
I32 = jnp.int32

RMS_EPS = 1e-6
LN_EPS = 1e-5
NEG_INF = -1e30

A_CHUNK = 128
A_GROUPS = 8
B_HEADS = 16
B_BLOCK = 256
B_TOPK = 3
N_EXPERTS = 64
TOP_K = 8
N_GROUPS = 8
TOPK_GROUPS = 4
ROUTED_SCALE = 2.5

LANES = 128
VMEM_LIMIT = 56 * 1024 * 1024

TM_SORT = 512
RUN_ALIGN = 16
SORT_CHUNK = 512
ROWS_TILE_USED = TM_SORT * TOP_K + N_EXPERTS * (RUN_ALIGN - 1)
ROWS_TILE = -(-ROWS_TILE_USED // SORT_CHUNK) * SORT_CHUNK
PIECES_TILE = ROWS_TILE // RUN_ALIGN
PIECES_PAD = -(-PIECES_TILE // LANES) * LANES
BM_EXPERT = 2048
BM_SEG = 256
BM_CHAIN = 256
EXPERT_CHAINS = 4
ATTN_PAIR_UNROLL = 4
ATTN_PAIR_UNROLL_PAST = 8
SUM_ROWS = 16


def _cparams(sem):
    return pltpu.CompilerParams(dimension_semantics=sem, vmem_limit_bytes=VMEM_LIMIT)


def _sigmoid(x):
    return 1.0 / (1.0 + jnp.exp(-x))


def _silu(x):
    return x * _sigmoid(x)


def _gelu_tanh(x):
    hx = 0.5 * x
    return hx + hx * jnp.tanh(x * (0.7978845608028654 + 0.035677408136300125 * (x * x)))


def _rms(x, g):
    return x * lax.rsqrt(jnp.mean(x * x, axis=-1, keepdims=True) + RMS_EPS) * g


def _bdot(a, b):
    return jnp.dot(a, b, preferred_element_type=F32)


def _bdot_nt(a, b):
    return lax.dot_general(a, b, (((1,), (1,)), ((), ())), preferred_element_type=F32)


def _split(a):
    hi = a.astype(BF16)
    lo = (a - hi.astype(F32)).astype(BF16)
    return hi, lo


def _dot3_nt(a, b):
    ah, al = _split(a)
    bh, bl = _split(b)
    return _bdot_nt(ah, bh) + (_bdot_nt(ah, bl) + _bdot_nt(al, bh))


def _dot3(a, b):
    ah, al = _split(a)
    bh, bl = _split(b)
    return _bdot(ah, bh) + (_bdot(ah, bl) + _bdot(al, bh))


def _ada_kernel(c_ref, w_ref, b_ref, o_ref):
    a = _silu(c_ref[...]).astype(BF16)
    o_ref[0] = _bdot(a, w_ref[0].astype(BF16)) + b_ref[0]


def _ada(c, w, b):
    bsz, d = c.shape
    nl, _, n = w.shape
    tn = 1024
    return pl.pallas_call(
        _ada_kernel,
        grid=(nl, n // tn),
        in_specs=[pl.BlockSpec((bsz, d), lambda l, j: (0, 0)),
                  pl.BlockSpec((1, d, tn), lambda l, j: (l, 0, j)),
                  pl.BlockSpec((1, 1, tn), lambda l, j: (l, 0, j))],
        out_specs=pl.BlockSpec((1, bsz, tn), lambda l, j: (l, 0, j)),
        out_shape=jax.ShapeDtypeStruct((nl, bsz, n), F32),
        compiler_params=_cparams(("arbitrary", "arbitrary")),
    )(c, w, b.reshape(nl, 1, n))


def _gmlp_body(x, sh_ref, sc_ref, g_ref, nw_ref, win_ref, bin_ref, lng_ref, lnb_ref,
               ws_ref, bst_ref, wout_ref, y_sc):
    tm = x.shape[0]
    h = _rms(x, nw_ref[...]) * (1.0 + sc_ref[0]) + sh_ref[0]
    z = _gelu_tanh(_bdot(h.astype(BF16), win_ref[...]) + bin_ref[...])
    aw = z.shape[1] // 2
    gd = aw // A_GROUPS
    u = z[:, :aw]
    v = z[:, aw:]
    mu = jnp.mean(v, axis=-1, keepdims=True)
    dv = v - mu
    var = jnp.mean(dv * dv, axis=-1, keepdims=True)
    vn = (dv * lax.rsqrt(var + LN_EPS) * lng_ref[...] + lnb_ref[...]).astype(BF16)
    row = lax.broadcasted_iota(I32, (A_CHUNK, A_CHUNK), 0)
    col = lax.broadcasted_iota(I32, (A_CHUNK, A_CHUNK), 1)
    causal = col <= row
    for g in range(A_GROUPS):
        wg = jnp.where(causal, ws_ref[g], 0.0).astype(BF16)
        bcol = bst_ref[:, g:g + 1]
        for ci in range(tm // A_CHUNK):
            rs = slice(ci * A_CHUNK, (ci + 1) * A_CHUNK)
            cs = slice(g * gd, (g + 1) * gd)
            sv = _bdot(wg, vn[rs, cs]) + bcol
            y_sc[rs, cs] = (u[rs, cs] * sv).astype(BF16)
    return x + g_ref[0] * _bdot(y_sc[...], wout_ref[...])


def _route(x, sh_ref, sc_ref, g_ref, nw_ref, wrt_ref, bias_ref, wsg_ref, wsu_ref, wsd_ref,
           h_ref, base_ref, key_ref, wkey_ref, cnt_ref, pe_ref, po_ref):
    tm = x.shape[0]
    h = _rms(x, nw_ref[...]) * (1.0 + sc_ref[0]) + sh_ref[0]
    hb = h.astype(BF16)
    h_ref[...] = hb
    act = (_silu(_bdot(hb, wsg_ref[...])) * _bdot(hb, wsu_ref[...])).astype(BF16)
    base_ref[...] = x + g_ref[0] * _bdot(act, wsd_ref[...])

    scores = _sigmoid(_dot3_nt(wrt_ref[...], h))
    choice = scores + bias_ref[...]
    gsz = N_EXPERTS // N_GROUPS
    sub = lax.broadcasted_iota(I32, (gsz, tm), 0)
    blocks = [choice[g * gsz:(g + 1) * gsz] for g in range(N_GROUPS)]
    gscore = []
    for blk in blocks:
        m1 = jnp.max(blk, axis=0, keepdims=True)
        i1 = jnp.min(jnp.where(blk == m1, sub, gsz), axis=0, keepdims=True)
        m2 = jnp.max(jnp.where(sub == i1, -jnp.inf, blk), axis=0, keepdims=True)
        gscore.append(m1 + m2)
    masked = []
    for g in range(N_GROUPS):
        beats = jnp.zeros((1, tm), F32)
        for m in range(N_GROUPS):
            if m == g:
                continue
            b = gscore[m] > gscore[g]
            if m < g:
                b = jnp.logical_or(b, gscore[m] == gscore[g])
            beats = beats + b.astype(F32)
        masked.append(jnp.where(beats < TOPK_GROUPS, blocks[g], NEG_INF))
    cur = jnp.concatenate(masked, axis=0)
    eio = lax.broadcasted_iota(I32, (N_EXPERTS, tm), 0)
    chosen = jnp.zeros((N_EXPERTS, tm), jnp.bool_)
    wsum = jnp.zeros((1, tm), F32)
    for _ in range(TOP_K):
        m = jnp.max(cur, axis=0, keepdims=True)
        idx = jnp.min(jnp.where(cur == m, eio, N_EXPERTS), axis=0, keepdims=True)
        sel = eio == idx
        chosen = jnp.logical_or(chosen, sel)
        wsum = wsum + jnp.sum(jnp.where(sel, scores, 0.0), axis=0, keepdims=True)
        cur = jnp.where(sel, -jnp.inf, cur)
    wkey_ref[...] = jnp.where(chosen, scores / wsum * ROUTED_SCALE, 0.0)

    onehot = chosen.astype(BF16)
    r_i = lax.broadcasted_iota(I32, (tm, tm), 0)
    c_i = lax.broadcasted_iota(I32, (tm, tm), 1)
    before = (r_i < c_i).astype(BF16)
    prior = _bdot(onehot, before)
    key_ref[...] = jnp.where(chosen, prior, -1.0).astype(I32)
    cnt = jnp.sum(chosen.astype(F32), axis=1, keepdims=True)
    cnt_ref[0] = cnt

    run_p = jnp.floor((cnt + (RUN_ALIGN - 1)) / RUN_ALIGN)
    ppc = SORT_CHUNK // RUN_ALIGN
    tot = jnp.sum(run_p, axis=0, keepdims=True)
    fill = jnp.ceil(tot / ppc) * ppc - tot
    run_p = run_p + jnp.where(lax.broadcasted_iota(I32, (N_EXPERTS, 1), 0) == N_EXPERTS - 1, fill, 0.0)
    e_r = lax.broadcasted_iota(I32, (N_EXPERTS, N_EXPERTS), 0)
    e_c = lax.broadcasted_iota(I32, (N_EXPERTS, N_EXPERTS), 1)
    incl = (e_c <= e_r).astype(BF16)
    lend = _bdot(incl, jnp.broadcast_to(run_p, (N_EXPERTS, LANES)).astype(BF16))[:, 0:1]
    loff = lend - run_p
    pj = lax.broadcasted_iota(I32, (N_EXPERTS, PIECES_PAD), 1).astype(F32)
    er = lax.broadcasted_iota(I32, (N_EXPERTS, PIECES_PAD), 0).astype(F32)
    pe = jnp.minimum(jnp.sum((lend <= pj).astype(F32), axis=0, keepdims=True), N_EXPERTS - 1.0)
    lo = jnp.sum(jnp.where(er == pe, loff, 0.0), axis=0, keepdims=True)
    pe_ref[0] = pe.astype(I32)
    po_ref[0] = ((pj[0:1, :] - lo) * RUN_ALIGN).astype(I32)


N_ROUTE_REFS = 16


def _router_kernel(x_ref, *refs):
    _route(x_ref[0], *refs)


def _mixer_router_kernel(x_ref, *refs, body, n_mix):
    route_refs = refs[n_mix:n_mix + N_ROUTE_REFS]
    x1 = body(x_ref, refs[:n_mix], refs[n_mix + N_ROUTE_REFS:])
    _route(x1, *route_refs)


def _gmlp_mixer(mods, nw, w_in, b_in, ln_g, ln_b, w_s, b_s, w_out):
    sh, sc, gt = mods
    d, n_in = w_in.shape
    aw = n_in // 2
    vec = pl.BlockSpec((1, 1, d), lambda b, j: (b, 0, 0))
    full2 = lambda shape: pl.BlockSpec(shape, lambda b, j: (0, 0))
    specs = [vec, vec, vec, full2((1, d)), full2((d, n_in)), full2((1, n_in)), full2((1, aw)),
             full2((1, aw)), pl.BlockSpec((A_GROUPS, A_CHUNK, A_CHUNK), lambda b, j: (0, 0, 0)),
             full2((A_CHUNK, A_GROUPS)), full2((aw, d))]
    args = [sh, sc, gt, nw.reshape(1, d), w_in.astype(BF16), b_in.reshape(1, n_in),
            ln_g.reshape(1, aw), ln_b.reshape(1, aw), w_s, b_s.T, w_out.astype(BF16)]

    def body(x_ref, refs, scratch):
        return _gmlp_body(x_ref[0], *refs, *scratch)

    return body, specs, args, [pltpu.VMEM((TM_SORT, aw), BF16)]


def _router(x, mods, nw, w_router, e_bias, wsg, wsu, wsd, mixer=None):
    bsz, s, d = x.shape
    t = bsz * s
    tm = TM_SORT
    nt = s // tm
    sd = wsg.shape[1]
    vec = pl.BlockSpec((1, 1, d), lambda b, j: (b, 0, 0))
    full2 = lambda shape: pl.BlockSpec(shape, lambda b, j: (0, 0))
    tok = pl.BlockSpec((tm, d), lambda b, j: (b * nt + j, 0))
    etok = pl.BlockSpec((N_EXPERTS, tm), lambda b, j: (0, b * nt + j))
    ptab = pl.BlockSpec((1, 1, PIECES_PAD), lambda b, j: (b * nt + j, 0, 0))
    sh, sc, gt = mods
    in_specs = [vec, vec, vec, full2((1, d)), full2((N_EXPERTS, d)), full2((N_EXPERTS, 1)),
                full2((d, sd)), full2((d, sd)), full2((sd, d))]
    args = [sh, sc, gt, nw.reshape(1, d), w_router.T, e_bias.reshape(N_EXPERTS, 1),
            wsg.astype(BF16), wsu.astype(BF16), wsd.astype(BF16)]
    kern, scratch = _router_kernel, []
    if mixer is not None:
        body, mspecs, margs, scratch = mixer
        kern = functools.partial(_mixer_router_kernel, body=body, n_mix=len(margs))
        in_specs = mspecs + in_specs
        args = margs + args
    return pl.pallas_call(
        kern,
        grid=(bsz, nt),
        in_specs=[pl.BlockSpec((1, tm, d), lambda b, j: (b, j, 0))] + in_specs,
        out_specs=[tok, tok, etok, etok,
                   pl.BlockSpec((1, N_EXPERTS, 1), lambda b, j: (b * nt + j, 0, 0)), ptab, ptab],
        out_shape=[jax.ShapeDtypeStruct((t, d), BF16), jax.ShapeDtypeStruct((t, d), F32),
                   jax.ShapeDtypeStruct((N_EXPERTS, t), I32), jax.ShapeDtypeStruct((N_EXPERTS, t), F32),
                   jax.ShapeDtypeStruct((t // tm, N_EXPERTS, 1), F32),
                   jax.ShapeDtypeStruct((t // tm, 1, PIECES_PAD), I32),
                   jax.ShapeDtypeStruct((t // tm, 1, PIECES_PAD), I32)],
        scratch_shapes=scratch,
        compiler_params=_cparams(("arbitrary", "arbitrary")),
    )(x, *args)


def _rows_max(t):
    rows = (t // TM_SORT) * ROWS_TILE + N_EXPERTS * (BM_SEG - RUN_ALIGN)
    return -(-rows // BM_EXPERT) * BM_EXPERT


def _sort_meta(cnt, n_rows_max):
    nts = cnt.shape[0]
    c = cnt.reshape(nts, N_EXPERTS).astype(I32)
    run = (c + (RUN_ALIGN - 1)) // RUN_ALIGN * RUN_ALIGN
    fill = (-jnp.sum(run, axis=1, keepdims=True)) % SORT_CHUNK
    run = jnp.concatenate([run[:, :-1], run[:, -1:] + fill], axis=1)
    used = jnp.sum(run, axis=1)
    per_e = jnp.sum(run, axis=0)
    seg = (per_e + (BM_SEG - 1)) // BM_SEG * BM_SEG
    ends = jnp.cumsum(seg)
    starts = ends - seg
    goff = starts[None, :] + jnp.cumsum(run, axis=0) - run
    pad_start = starts + per_e
    pad_cnt = (seg - per_e) // RUN_ALIGN

    nh = n_rows_max // BM_SEG
    hidx = jnp.arange(nh, dtype=I32)
    he = jnp.sum((ends[None, :] <= (hidx * BM_SEG)[:, None]).astype(I32), axis=1)
    real = he < N_EXPERTS
    he = jnp.minimum(he, N_EXPERTS - 1).astype(I32)
    prev = jnp.concatenate([jnp.full((1,), -1, I32), he[:-1]])
    hnew = jnp.logical_and(real, he != prev)
    per_blk = BM_EXPERT // BM_SEG
    hslot = (jnp.cumsum(hnew.astype(I32)) - 1) % per_blk
    later = jnp.logical_and(hidx[None, :] > hidx[:, None], hnew[None, :])
    nxt = jnp.min(jnp.where(later, hidx[None, :], nh), axis=1)
    hfetch = jnp.where(nxt < nh, he[jnp.minimum(nxt, nh - 1)], -1)
    bx = jnp.where(real[::per_blk], jnp.arange(nh // per_blk, dtype=I32), 0)
    i32 = lambda a: a.reshape(-1).astype(I32)
    return ((i32(goff), i32(used), i32(pad_start), i32(pad_cnt)),
            (bx, he, i32(hnew), i32(real), i32(hslot), i32(hfetch)))


def _piece_copies(pe_ref, po_ref, goff_ref, used_ref, tile, local, remote, sem, to_remote):
    def copy(j):
        idx = tile * PIECES_PAD + j
        l0 = pl.multiple_of(j * RUN_ALIGN, RUN_ALIGN)
        g0 = pl.multiple_of(goff_ref[tile * N_EXPERTS + pe_ref[idx]] + po_ref[idx], RUN_ALIGN)
        lref = local.at[pl.ds(l0, RUN_ALIGN)]
        gref = remote.at[pl.ds(g0, RUN_ALIGN)]
        return pltpu.make_async_copy(lref, gref, sem) if to_remote else pltpu.make_async_copy(gref, lref, sem)

    def wait_all():
        def body(j, carry):
            copy(j).wait()
            return carry

        lax.fori_loop(0, used_ref[tile] // RUN_ALIGN, body, 0)

    return copy, wait_all


def _pad_copies(pad_start_ref, pad_cnt_ref, tile, ntiles, zeros, remote, sem):
    share = -(-N_EXPERTS // ntiles)

    def apply(act):
        for q in range(share):
            e = tile * share + q
            ec = jnp.minimum(e, N_EXPERTS - 1)
            n = jnp.where(e < N_EXPERTS, pad_cnt_ref[ec], 0)
            base = pad_start_ref[ec]

            def body(j, carry):
                g0 = pl.multiple_of(base + j * RUN_ALIGN, RUN_ALIGN)
                getattr(pltpu.make_async_copy(zeros, remote.at[pl.ds(g0, RUN_ALIGN)], sem), act)()
                return carry

            lax.fori_loop(0, n, body, 0)

    return apply


def _piece_rows(pe_ref, po_ref, key_ref, first_piece, npieces, val_ref=None):
    tm = key_ref.shape[1]
    sub = lax.broadcasted_iota(I32, (RUN_ALIGN, tm), 0)
    out = []
    for jj in range(npieces):
        j = first_piece + jj
        e = pe_ref[j]
        hit = (key_ref[pl.ds(e, 1), :] - po_ref[j]) == sub
        val = 1.0 if val_ref is None else val_ref[pl.ds(e, 1), :]
        out.append(jnp.where(hit, val, 0.0))
    return out


def _dispatch_kernel(pe_ref, po_ref, goff_ref, used_ref, pad_start_ref, pad_cnt_ref,
                     key_ref, h_ref, xs_ref, xbuf, p_sc, zbuf, sem, zsem, *, ntiles):
    i = pl.program_id(0)
    hb = h_ref[...]
    ch = SORT_CHUNK
    ppc = ch // RUN_ALIGN
    zbuf[...] = jnp.zeros_like(zbuf)
    pads = _pad_copies(pad_start_ref, pad_cnt_ref, i, ntiles, zbuf, xs_ref, zsem)
    pads("start")

    copy, wait_all = _piece_copies(pe_ref, po_ref, goff_ref, used_ref, i, xbuf, xs_ref, sem, True)

    def build(ci):
        rows = _piece_rows(pe_ref, po_ref, key_ref, i * PIECES_PAD + ci * ppc, ppc)
        for jj, p in enumerate(rows):
            p_sc[jj * RUN_ALIGN:(jj + 1) * RUN_ALIGN, :] = p.astype(BF16)
        xbuf[pl.ds(pl.multiple_of(ci * ch, ch), ch), :] = _bdot(p_sc[...], hb).astype(BF16)

    def send(ci):
        for jj in range(ppc):
            copy(ci * ppc + jj).start()

    def step(ci, carry):
        send(ci - 1)
        build(ci)
        return carry

    nchunk = used_ref[i] // ch
    build(0)
    lax.fori_loop(1, nchunk, step, 0)
    send(nchunk - 1)
    pads("wait")
    wait_all()


def _dispatch(meta, pe, po, key, hb, n_rows_max):
    t, d = hb.shape
    tm = TM_SORT
    goff, used, pad_start, pad_cnt = meta
    return pl.pallas_call(
        functools.partial(_dispatch_kernel, ntiles=t // tm),
        grid_spec=pltpu.PrefetchScalarGridSpec(
            num_scalar_prefetch=6, grid=(t // tm,),
            in_specs=[pl.BlockSpec((N_EXPERTS, tm), lambda i, *_: (0, i)),
                      pl.BlockSpec((tm, d), lambda i, *_: (i, 0))],
            out_specs=pl.BlockSpec(memory_space=pl.ANY),
            scratch_shapes=[pltpu.VMEM((ROWS_TILE, d), BF16), pltpu.VMEM((SORT_CHUNK, tm), BF16),
                            pltpu.VMEM((RUN_ALIGN, d), BF16),
                            pltpu.SemaphoreType.DMA(()), pltpu.SemaphoreType.DMA(())]),
        out_shape=jax.ShapeDtypeStruct((n_rows_max, d), BF16),
        compiler_params=_cparams(("arbitrary",)),
    )(pe, po, goff, used, pad_start, pad_cnt, key, hb)


def _expert_kernel(bx_ref, he_ref, hnew_ref, hreal_ref, hslot_ref, hfetch_ref, x_ref, wg_hbm, wu_hbm, wd_hbm,
                   o_ref, wg_st, wu_st, wd_st, wg_sc, wu_sc, wd_sc, sem, *, layer):
    b = pl.program_id(0)
    per_blk = BM_EXPERT // BM_SEG
    cpp = BM_SEG // BM_CHAIN

    def fetch(e):
        return [pltpu.make_async_copy(src.at[layer, e], dst, sem.at[i])
                for i, (src, dst) in enumerate(((wg_hbm, wg_st), (wu_hbm, wu_st), (wd_hbm, wd_st)))]

    @pl.when(b == 0)
    def _():
        for cp in fetch(he_ref[0]):
            cp.start()

    for part in range(per_blk):
        h = b * per_blk + part

        @pl.when(hnew_ref[h] == 1)
        def _():
            for cp in fetch(he_ref[h]):
                cp.wait()
            s = hslot_ref[h]
            wg_sc[s] = wg_st[...].astype(BF16)
            wu_sc[s] = wu_st[...].astype(BF16)
            wd_sc[s] = wd_st[...].astype(BF16)

            @pl.when(hfetch_ref[h] >= 0)
            def _():
                for cp in fetch(hfetch_ref[h]):
                    cp.start()

    def run_parts(parts):
        every = [(slice((p * cpp + c) * BM_CHAIN, (p * cpp + c + 1) * BM_CHAIN), hslot_ref[b * per_blk + p])
                 for p in parts for c in range(cpp)]
        for g0 in range(0, len(every), EXPERT_CHAINS):
            chains = every[g0:g0 + EXPERT_CHAINS]
            xs = [x_ref[rs, :] for rs, _ in chains]
            gs = [_bdot(x, wg_sc[s]) for x, (_, s) in zip(xs, chains)]
            us = [_bdot(x, wu_sc[s]) for x, (_, s) in zip(xs, chains)]
            acts = [(_silu(g) * u).astype(BF16) for g, u in zip(gs, us)]
            for (rs, s), act in zip(chains, acts):
                o_ref[rs, :] = _bdot(act, wd_sc[s]).astype(BF16)

    nlive = hreal_ref[b * per_blk]
    for part in range(1, per_blk):
        nlive = nlive + hreal_ref[b * per_blk + part]

    @pl.when(nlive == per_blk)
    def _():
        run_parts(range(per_blk))

    @pl.when(nlive < per_blk)
    def _():
        for part in range(per_blk):
            @pl.when(part < nlive)
            def _():
                run_parts([part])

            @pl.when(part >= nlive)
            def _():
                o_ref[part * BM_SEG:(part + 1) * BM_SEG, :] = jnp.zeros((BM_SEG, o_ref.shape[1]), BF16)


def _experts(meta, xs, layer, w_gate, w_up, w_down):
    r, d = xs.shape
    ed = w_gate.shape[-1]
    bm = BM_EXPERT
    nslot = BM_EXPERT // BM_SEG
    omap = lambda b, bx, *_: (b, 0)
    xmap = lambda b, bx, *_: (bx[b], 0)
    hbm = pl.BlockSpec(memory_space=pl.ANY)
    return pl.pallas_call(
        functools.partial(_expert_kernel, layer=layer),
        grid_spec=pltpu.PrefetchScalarGridSpec(
            num_scalar_prefetch=6, grid=(r // bm,),
            in_specs=[pl.BlockSpec((bm, d), xmap), hbm, hbm, hbm],
            out_specs=pl.BlockSpec((bm, d), omap),
            scratch_shapes=[pltpu.VMEM((d, ed), F32), pltpu.VMEM((d, ed), F32), pltpu.VMEM((ed, d), F32),
                            pltpu.VMEM((nslot, d, ed), BF16), pltpu.VMEM((nslot, d, ed), BF16),
                            pltpu.VMEM((nslot, ed, d), BF16), pltpu.SemaphoreType.DMA((3,))]),
        out_shape=jax.ShapeDtypeStruct((r, d), BF16),
        compiler_params=_cparams(("arbitrary",)),
    )(*meta, xs, w_gate, w_up, w_down)


def _combine_kernel(pe_ref, po_ref, goff_ref, used_ref, key_ref, wkey_ref, ys_ref, base_ref, g_ref,
                    fn_ref, *rest, final, with_kv):
    if with_kv:
        kv_in, o_ref, kv_out, (ybuf, q_sc, sem) = rest[:5], rest[5], rest[6:9], rest[9:]
    else:
        o_ref, ybuf, q_sc, sem = rest
    i = pl.program_id(0)
    tm = base_ref.shape[0]
    ch = SORT_CHUNK
    ppc = ch // RUN_ALIGN

    @pl.when(i == 0)
    def _():
        ybuf[...] = jnp.zeros_like(ybuf)

    copy, wait_all = _piece_copies(pe_ref, po_ref, goff_ref, used_ref, i, ybuf, ys_ref, sem, False)

    for ci in range(ybuf.shape[0] // ch):
        cs = slice(ci * ch, (ci + 1) * ch)

        @pl.when(ci * ch < used_ref[i])
        def _():
            for jj in range(ppc):
                copy(ci * ppc + jj).start()
            rows = _piece_rows(pe_ref, po_ref, key_ref, i * PIECES_PAD + ci * ppc, ppc, wkey_ref)
            for jj, row in enumerate(rows):
                q_sc[ci * ch + jj * RUN_ALIGN:ci * ch + (jj + 1) * RUN_ALIGN, :] = row.astype(BF16)

        @pl.when(ci * ch >= used_ref[i])
        def _():
            q_sc[cs, :] = jnp.zeros((ch, tm), BF16)

    wait_all()
    out = base_ref[...] + g_ref[0] * lax.dot_general(q_sc[...], ybuf[...], (((0,), (0,)), ((), ())),
                                                     preferred_element_type=F32)
    if final:
        out = _rms(out, fn_ref[...])
    o_ref[...] = out
    if with_kv:
        _kv_body(out, *kv_in, *kv_out)


def _combine(meta, pe, po, key, wkey, ys, base, gt, fnorm, seq, final, kv=None):
    t, d = base.shape
    tm = TM_SORT
    nt = seq // tm
    bsz = t // seq
    vecb = pl.BlockSpec((1, 1, d), lambda i, *_: (i // nt, 0, 0))
    full2 = lambda shape: pl.BlockSpec(shape, lambda i, *_: (0, 0))
    in_specs = [pl.BlockSpec((N_EXPERTS, tm), lambda i, *_: (0, i)),
                pl.BlockSpec((N_EXPERTS, tm), lambda i, *_: (0, i)),
                pl.BlockSpec(memory_space=pl.ANY),
                pl.BlockSpec((tm, d), lambda i, *_: (i, 0)), vecb, full2((1, d))]
    args = [key, wkey, ys, base, gt, fnorm.reshape(1, d)]
    out_specs = [pl.BlockSpec((tm, d), lambda i, *_: (i, 0))]
    out_shape = [jax.ShapeDtypeStruct((t, d), F32)]
    if kv is not None:
        ksh, ksc, knw, w_k, w_v = kv
        nb, npair, per = seq // B_BLOCK, d // LANES, tm // B_BLOCK
        in_specs += [vecb, vecb, full2((1, d)), full2((d, d)), full2((d, d))]
        args += [ksh, ksc, knw.reshape(1, d), w_k.astype(BF16), w_v.T.astype(BF16)]
        out_specs += [pl.BlockSpec((1, npair, per, B_BLOCK, LANES), lambda i, *_: (i // nt, 0, i % nt, 0, 0)),
                      pl.BlockSpec((1, npair, per, LANES, B_BLOCK), lambda i, *_: (i // nt, 0, i % nt, 0, 0)),
                      pl.BlockSpec((1, per, 1, d), lambda i, *_: (i // nt, i % nt, 0, 0))]
        out_shape += [jax.ShapeDtypeStruct((bsz, npair, nb, B_BLOCK, LANES), BF16),
                      jax.ShapeDtypeStruct((bsz, npair, nb, LANES, B_BLOCK), BF16),
                      jax.ShapeDtypeStruct((bsz, nb, 1, d), F32)]
    res = pl.pallas_call(
        functools.partial(_combine_kernel, final=final, with_kv=kv is not None),
        grid_spec=pltpu.PrefetchScalarGridSpec(
            num_scalar_prefetch=4, grid=(t // tm,),
            in_specs=in_specs, out_specs=out_specs,
            scratch_shapes=[pltpu.VMEM((ROWS_TILE, d), BF16), pltpu.VMEM((ROWS_TILE, tm), BF16),
                            pltpu.SemaphoreType.DMA(())]),
        out_shape=out_shape,
        compiler_params=_cparams(("arbitrary",)),
    )(pe, po, meta[0], meta[1], *args)
    return res if kv is not None else res[0]


def _moe(x, mods, nw, w_router, e_bias, layer, w_gate, w_up, w_down, wsg, wsu, wsd, fnorm, final, mixer, kv=None):
    bsz, s, d = x.shape
    t = bsz * s
    gt = mods[2]
    n_rows_max = _rows_max(t)
    hb, base, key, wkey, cnt, pe, po = _router(x, mods, nw, w_router, e_bias, wsg, wsu, wsd, mixer)
    pe = pe.reshape(-1)
    po = po.reshape(-1)
    layout_meta, block_meta = _sort_meta(cnt, n_rows_max)
    xs = _dispatch(layout_meta, pe, po, key, hb, n_rows_max)
    ys = _experts(block_meta, xs, layer, w_gate, w_up, w_down)
    out = _combine(layout_meta, pe, po, key, wkey, ys, base, gt, fnorm, s, final, kv)
    if kv is not None:
        return (out[0].reshape(bsz, s, d),) + tuple(out[1:])
    return out.reshape(bsz, s, d)


def _kv_body(x, sh_ref, sc_ref, nw_ref, wk_ref, wvt_ref, k_ref, vt_ref, km_ref):
    hb = (_rms(x, nw_ref[...]) * (1.0 + sc_ref[0]) + sh_ref[0]).astype(BF16)
    k = _bdot(hb, wk_ref[...])
    vt = _bdot_nt(wvt_ref[...], hb)
    for blk in range(x.shape[0] // B_BLOCK):
        rs = slice(blk * B_BLOCK, (blk + 1) * B_BLOCK)
        for p in range(k.shape[1] // LANES):
            k_ref[0, p, blk] = k[rs, p * LANES:(p + 1) * LANES].astype(BF16)
            vt_ref[0, p, blk] = vt[p * LANES:(p + 1) * LANES, rs].astype(BF16)
        km_ref[0, blk] = jnp.mean(k[rs], axis=0, keepdims=True)


def _kv_kernel(x_ref, *refs):
    _kv_body(x_ref[0], *refs)


def _kv(x, sh, sc, nw, w_k, w_v):
    bsz, s, d = x.shape
    nb = s // B_BLOCK
    npair = d // LANES
    vec = pl.BlockSpec((1, 1, d), lambda b, j: (b, 0, 0))
    full2 = lambda shape: pl.BlockSpec(shape, lambda b, j: (0, 0))
    return pl.pallas_call(
        _kv_kernel,
        grid=(bsz, nb),
        in_specs=[pl.BlockSpec((1, B_BLOCK, d), lambda b, j: (b, j, 0)), vec, vec,
                  full2((1, d)), full2((d, d)), full2((d, d))],
        out_specs=[pl.BlockSpec((1, npair, 1, B_BLOCK, LANES), lambda b, j: (b, 0, j, 0, 0)),
                   pl.BlockSpec((1, npair, 1, LANES, B_BLOCK), lambda b, j: (b, 0, j, 0, 0)),
                   pl.BlockSpec((1, 1, 1, d), lambda b, j: (b, j, 0, 0))],
        out_shape=[jax.ShapeDtypeStruct((bsz, npair, nb, B_BLOCK, LANES), BF16),
                   jax.ShapeDtypeStruct((bsz, npair, nb, LANES, B_BLOCK), BF16),
                   jax.ShapeDtypeStruct((bsz, nb, 1, d), F32)],
        compiler_params=_cparams(("arbitrary", "arbitrary")),
    )(x, sh, sc, nw.reshape(1, d), w_k.astype(BF16), w_v.T.astype(BF16))


def _attn_body(x, qb, sh_ref, sc_ref, g_ref, nw_ref, wqt_ref, wo_ref, k_ref, vt_ref, km_ref,
               qt_sc, qs_sc, acc_sc, sel_sc, m_sc, l_sc, *, nb, n_sel):
    bq = x.shape[0]
    npair = qt_sc.shape[0]
    nbp = km_ref.shape[2]
    hd = LANES // 2
    scale = float(hd) ** -0.5 * 1.4426950408889634
    h = _rms(x, nw_ref[...]) * (1.0 + sc_ref[0]) + sh_ref[0]
    qt = _bdot_nt(wqt_ref[...], h.astype(BF16))
    for p in range(npair):
        qt_sc[p] = qt[p * LANES:(p + 1) * LANES, :]

    subn = lax.broadcasted_iota(I32, (nbp, bq), 0)
    past = subn < qb
    krow = lax.broadcasted_iota(I32, (B_BLOCK, bq), 0)
    qcol = lax.broadcasted_iota(I32, (B_BLOCK, bq), 1)
    causal = krow <= qcol
    rowh = lax.broadcasted_iota(I32, (LANES, 1), 0)

    grp = ATTN_PAIR_UNROLL
    heads = [(u, e) for u in range(grp) for e in range(2)]
    ones_rows = jnp.ones((SUM_ROWS, B_BLOCK), BF16)

    def vsum(vt2, e):
        return jnp.concatenate([vt2[e * hd:(e + 1) * hd, :], ones_rows], axis=0)

    def own_body(gi, carry):
        ps = [gi * grp + u for u in range(grp)]
        q2ts = [qt_sc[p] for p in ps]
        kms = [km_ref[0, p] for p in ps]
        kown = [k_ref[0, p, qb] for p in ps]
        vown = [vt_ref[0, p, qb] for p in ps]
        qets = [jnp.where((rowh >= hd) if e == 1 else (rowh < hd), q2ts[u], 0.0) for u, e in heads]
        qsts = [(q * scale).astype(BF16) for q in qets]
        ss = [jnp.where(causal, _bdot(kown[u], qsts[i]), NEG_INF) for i, (u, e) in enumerate(heads)]
        gates = [_dot3(kms[u], qets[i]) for i, (u, e) in enumerate(heads)]
        ms = [jnp.max(s, axis=0, keepdims=True) for s in ss]
        pes = [jnp.exp2(s - m) for s, m in zip(ss, ms)]
        pvs = [_bdot(vsum(vown[u], e), pes[i].astype(BF16)) for i, (u, e) in enumerate(heads)]
        accs = [pv[:hd] for pv in pvs]
        ls = [pv[hd:hd + 1] for pv in pvs]
        sels = []
        for gate in gates:
            selt = jnp.zeros((nbp, bq), F32)
            for n in range(nb):
                gn = gate[n:n + 1, :]
                beats = jnp.logical_or(gate > gn, jnp.logical_and(gate == gn, subn < n))
                beats = jnp.logical_and(beats, past)
                cnt = jnp.sum(beats.astype(F32), axis=0, keepdims=True)
                selt = jnp.where(subn == n, (cnt < n_sel).astype(F32), selt)
            sels.append(selt)
        for i, (u, e) in enumerate(heads):
            p = ps[u]
            sel_sc[p, e] = sels[i]
            qs_sc[p, e] = qsts[i]
            m_sc[p, e] = ms[i]
            l_sc[p, e] = ls[i]
            acc_sc[p, e * hd:(e + 1) * hd, :] = accs[i]
        return carry

    lax.fori_loop(0, npair // grp, own_body, 0)

    grp = ATTN_PAIR_UNROLL_PAST
    heads = [(u, e) for u in range(grp) for e in range(2)]

    def kb_body(kb, carry):
        def group_body(gi, c2):
            ps = [gi * grp + u for u in range(grp)]
            kbl = [k_ref[0, p, kb] for p in ps]
            vbl = [vt_ref[0, p, kb] for p in ps]
            qsts = [qs_sc[ps[u], e] for u, e in heads]
            rows = [sel_sc[ps[u], e, pl.ds(kb, 1), :] for u, e in heads]
            m_old = [m_sc[ps[u], e] for u, e in heads]
            l_old = [l_sc[ps[u], e] for u, e in heads]
            a_old = [acc_sc[ps[u], e * hd:(e + 1) * hd, :] for u, e in heads]
            ss = [jnp.where(rows[i] > 0.5, _bdot(kbl[u], qsts[i]), NEG_INF)
                  for i, (u, e) in enumerate(heads)]
            m_new = [jnp.maximum(m, jnp.max(s, axis=0, keepdims=True)) for m, s in zip(m_old, ss)]
            alphas = [jnp.exp2(m - mn) for m, mn in zip(m_old, m_new)]
            pes = [jnp.exp2(s - mn) for s, mn in zip(ss, m_new)]
            pvs = [_bdot(vsum(vbl[u], e), pes[i].astype(BF16)) for i, (u, e) in enumerate(heads)]
            l_new = [a * l + pv[hd:hd + 1] for a, l, pv in zip(alphas, l_old, pvs)]
            a_new = [a * ao + pv[:hd] for a, ao, pv in zip(alphas, a_old, pvs)]
            for i, (u, e) in enumerate(heads):
                p = ps[u]
                m_sc[p, e] = m_new[i]
                l_sc[p, e] = l_new[i]
                acc_sc[p, e * hd:(e + 1) * hd, :] = a_new[i]
            return c2

        lax.fori_loop(0, npair // grp, group_body, 0)
        return carry

    lax.fori_loop(0, qb, kb_body, 0)

    parts = []
    for p in range(npair):
        for e in range(2):
            parts.append(acc_sc[p, e * hd:(e + 1) * hd, :] / l_sc[p, e])
    ot = jnp.concatenate(parts, axis=0)
    return x + g_ref[0] * _bdot(ot.T.astype(BF16), wo_ref[...])


def _attn_mixer(mods, nw, w_q, w_o, k5, vt5, km2):
    sh, sc, gt = mods
    d = w_q.shape[0]
    npair, nb = k5.shape[1], k5.shape[2]
    nbp = km2.shape[2]
    n_sel = min(B_TOPK, nb - 1)
    per = TM_SORT // B_BLOCK
    vec = pl.BlockSpec((1, 1, d), lambda b, j: (b, 0, 0))
    full2 = lambda shape: pl.BlockSpec(shape, lambda b, j: (0, 0))
    specs = [vec, vec, vec, full2((1, d)), full2((d, d)), full2((d, d)),
             pl.BlockSpec((1, npair, nb, B_BLOCK, LANES), lambda b, j: (b, 0, 0, 0, 0)),
             pl.BlockSpec((1, npair, nb, LANES, B_BLOCK), lambda b, j: (b, 0, 0, 0, 0)),
             pl.BlockSpec((1, npair, nbp, LANES), lambda b, j: (b, 0, 0, 0))]
    args = [sh, sc, gt, nw.reshape(1, d), w_q.T.astype(BF16), w_o.astype(BF16), k5, vt5, km2]
    scratch = [pltpu.VMEM((npair, LANES, B_BLOCK), F32),
               pltpu.VMEM((npair, 2, LANES, B_BLOCK), BF16),
               pltpu.VMEM((npair, LANES, B_BLOCK), F32),
               pltpu.VMEM((npair, 2, nbp, B_BLOCK), F32),
               pltpu.VMEM((npair, 2, 1, B_BLOCK), F32),
               pltpu.VMEM((npair, 2, 1, B_BLOCK), F32)]

    def body(x_ref, refs, scr):
        j = pl.program_id(1)
        outs = [_attn_body(x_ref[0, h * B_BLOCK:(h + 1) * B_BLOCK, :], j * per + h, *refs, *scr,
                           nb=nb, n_sel=n_sel) for h in range(per)]
        return jnp.concatenate(outs, axis=0)

    return body, specs, args, scratch


def kernel(x, c, ada_w, ada_b, norm_mix, norm_ffn, a_w_in, a_b_in, a_ln_g, a_ln_b, a_w_s, a_b_s,
           a_w_out, kv_norm, kv_ada_w, kv_ada_b, kv_w_k, kv_w_v, b_w_q, b_w_o, moe_router, moe_bias,
           moe_w_gate, moe_w_up, moe_w_down, sh_w_gate, sh_w_up, sh_w_down, final_norm):
    bsz, s, d = x.shape
    depth = ada_w.shape[0]
    n_a = a_w_in.shape[0]
    assert s % B_BLOCK == 0 and s % TM_SORT == 0 and TM_SORT % A_CHUNK == 0 and d % LANES == 0
    nb = s // B_BLOCK
    npair = d // LANES
    nbp = -(-nb // 8) * 8

    def split(m, n):
        return [m[:, i * d:(i + 1) * d].reshape(bsz, 1, d) for i in range(n)]

    def pair_major(km):
        km = km.reshape(bsz, nb, npair, LANES).transpose(0, 2, 1, 3)
        return jnp.pad(km, ((0, 0), (0, 0), (0, nbp - nb), (0, 0)))

    layer_mods = _ada(c, ada_w, ada_b)
    k5 = vt5 = km2 = None
    for i in range(depth):
        sh1, sc1, g1, sh2, sc2, g2 = split(layer_mods[i], 6)
        if i < n_a:
            mixer = _gmlp_mixer((sh1, sc1, g1), norm_mix[i], a_w_in[i], a_b_in[i], a_ln_g[i], a_ln_b[i],
                                a_w_s[i], a_b_s[i], a_w_out[i])
        else:
            if k5 is None:
                ksh, ksc = split(_ada(c, kv_ada_w[None], kv_ada_b[None])[0], 2)
                k5, vt5, km = _kv(x, ksh, ksc, kv_norm, kv_w_k, kv_w_v)
                km2 = pair_major(km)
            j = i - n_a
            mixer = _attn_mixer((sh1, sc1, g1), norm_mix[i], b_w_q[j], b_w_o[j], k5, vt5, km2)
        kv = None
        if i == n_a - 1 and i + 1 < depth:
            ksh, ksc = split(_ada(c, kv_ada_w[None], kv_ada_b[None])[0], 2)
            kv = (ksh, ksc, kv_norm, kv_w_k, kv_w_v)
        res = _moe(x, (sh2, sc2, g2), norm_ffn[i], moe_router[i], moe_bias[i], i, moe_w_gate,
                   moe_w_up, moe_w_down, sh_w_gate[i], sh_w_up[i], sh_w_down[i],
                   final_norm, i == depth - 1, mixer, kv)
        if kv is not None:
            x, k5, vt5, km = res
            km2 = pair_major(km)
        else:
            x = res
    return x
```

```python
import functools

import jax
import jax.numpy as jnp
from jax import lax
from jax.experimental import pallas as pl
from jax.experimental.pallas import tpu as pltpu

F32 = jnp.float32
BF16 = jnp.bfloat16
I32 = jnp.int32

RMS_EPS = 1e-6
LN_EPS = 1e-5
NEG_INF = -1e30

A_CHUNK = 128
A_GROUPS = 8
B_HEADS = 16
B_BLOCK = 256
B_TOPK = 3
N_EXPERTS = 64
TOP_K = 8
N_GROUPS = 8
TOPK_GROUPS = 4
ROUTED_SCALE = 2.5

LANES = 128
VMEM_LIMIT = 56 * 1024 * 1024

TM_SORT = 512
RUN_ALIGN = 16
SORT_CHUNK = 512
ROWS_TILE_USED = TM_SORT * TOP_K + N_EXPERTS * (RUN_ALIGN - 1)
ROWS_TILE = -(-ROWS_TILE_USED // SORT_CHUNK) * SORT_CHUNK
PIECES_TILE = ROWS_TILE // RUN_ALIGN
PIECES_PAD = -(-PIECES_TILE // LANES) * LANES
BM_EXPERT = 2048
BM_SEG = 256
BM_CHAIN = 256
EXPERT_CHAINS = 4
ATTN_PAIR_UNROLL = 4
ATTN_PAIR_UNROLL_PAST = 8
SUM_ROWS = 16


def _cparams(sem):
    return pltpu.CompilerParams(dimension_semantics=sem, vmem_limit_bytes=VMEM_LIMIT)


def _sigmoid(x):
    return 1.0 / (1.0 + jnp.exp(-x))


def _silu(x):
    return x * _sigmoid(x)


def _gelu_tanh(x):
    hx = 0.5 * x
    return hx + hx * jnp.tanh(x * (0.7978845608028654 + 0.035677408136300125 * (x * x)))


def _rms(x, g):
    return x * lax.rsqrt(jnp.mean(x * x, axis=-1, keepdims=True) + RMS_EPS) * g


def _bdot(a, b):
    return jnp.dot(a, b, preferred_element_type=F32)


def _bdot_nt(a, b):
    return lax.dot_general(a, b, (((1,), (1,)), ((), ())), preferred_element_type=F32)


def _split(a):
    hi = a.astype(BF16)
    lo = (a - hi.astype(F32)).astype(BF16)
    return hi, lo


def _dot3_nt(a, b):
    ah, al = _split(a)
    bh, bl = _split(b)
    return _bdot_nt(ah, bh) + (_bdot_nt(ah, bl) + _bdot_nt(al, bh))


def _dot3(a, b):
    ah, al = _split(a)
    bh, bl = _split(b)
    return _bdot(ah, bh) + (_bdot(ah, bl) + _bdot(al, bh))


def _ada_kernel(c_ref, w_ref, b_ref, o_ref):
    a = _silu(c_ref[...]).astype(BF16)
    o_ref[0] = _bdot(a, w_ref[0].astype(BF16)) + b_ref[0]


def _ada(c, w, b):
    bsz, d = c.shape
    nl, _, n = w.shape
    tn = 1024
    return pl.pallas_call(
        _ada_kernel,
        grid=(nl, n // tn),
        in_specs=[pl.BlockSpec((bsz, d), lambda l, j: (0, 0)),
                  pl.BlockSpec((1, d, tn), lambda l, j: (l, 0, j)),
                  pl.BlockSpec((1, 1, tn), lambda l, j: (l, 0, j))],
        out_specs=pl.BlockSpec((1, bsz, tn), lambda l, j: (l, 0, j)),
        out_shape=jax.ShapeDtypeStruct((nl, bsz, n), F32),
        compiler_params=_cparams(("arbitrary", "arbitrary")),
    )(c, w, b.reshape(nl, 1, n))


def _gmlp_body(x, sh_ref, sc_ref, g_ref, nw_ref, win_ref, bin_ref, lng_ref, lnb_ref,
               ws_ref, bst_ref, wout_ref, y_sc):
    tm = x.shape[0]
    h = _rms(x, nw_ref[...]) * (1.0 + sc_ref[0]) + sh_ref[0]
    z = _gelu_tanh(_bdot(h.astype(BF16), win_ref[...]) + bin_ref[...])
    aw = z.shape[1] // 2
    gd = aw // A_GROUPS
    u = z[:, :aw]
    v = z[:, aw:]
    mu = jnp.mean(v, axis=-1, keepdims=True)
    dv = v - mu
    var = jnp.mean(dv * dv, axis=-1, keepdims=True)
    vn = (dv * lax.rsqrt(var + LN_EPS) * lng_ref[...] + lnb_ref[...]).astype(BF16)
    row = lax.broadcasted_iota(I32, (A_CHUNK, A_CHUNK), 0)
    col = lax.broadcasted_iota(I32, (A_CHUNK, A_CHUNK), 1)
    causal = col <= row
    for g in range(A_GROUPS):
        wg = jnp.where(causal, ws_ref[g], 0.0).astype(BF16)
        bcol = bst_ref[:, g:g + 1]
        for ci in range(tm // A_CHUNK):
            rs = slice(ci * A_CHUNK, (ci + 1) * A_CHUNK)
            cs = slice(g * gd, (g + 1) * gd)
            sv = _bdot(wg, vn[rs, cs]) + bcol
            y_sc[rs, cs] = (u[rs, cs] * sv).astype(BF16)
    return x + g_ref[0] * _bdot(y_sc[...], wout_ref[...])


def _route(x, sh_ref, sc_ref, g_ref, nw_ref, wrt_ref, bias_ref, wsg_ref, wsu_ref, wsd_ref,
           h_ref, base_ref, key_ref, wkey_ref, cnt_ref, pe_ref, po_ref):
    tm = x.shape[0]
    h = _rms(x, nw_ref[...]) * (1.0 + sc_ref[0]) + sh_ref[0]
    hb = h.astype(BF16)
    h_ref[...] = hb
    act = (_silu(_bdot(hb, wsg_ref[...])) * _bdot(hb, wsu_ref[...])).astype(BF16)
    base_ref[...] = x + g_ref[0] * _bdot(act, wsd_ref[...])

    scores = _sigmoid(_dot3_nt(wrt_ref[...], h))
    choice = scores + bias_ref[...]
    gsz = N_EXPERTS // N_GROUPS
    sub = lax.broadcasted_iota(I32, (gsz, tm), 0)
    blocks = [choice[g * gsz:(g + 1) * gsz] for g in range(N_GROUPS)]
    gscore = []
    for blk in blocks:
        m1 = jnp.max(blk, axis=0, keepdims=True)
        i1 = jnp.min(jnp.where(blk == m1, sub, gsz), axis=0, keepdims=True)
        m2 = jnp.max(jnp.where(sub == i1, -jnp.inf, blk), axis=0, keepdims=True)
        gscore.append(m1 + m2)
    masked = []
    for g in range(N_GROUPS):
        beats = jnp.zeros((1, tm), F32)
        for m in range(N_GROUPS):
            if m == g:
                continue
            b = gscore[m] > gscore[g]
            if m < g:
                b = jnp.logical_or(b, gscore[m] == gscore[g])
            beats = beats + b.astype(F32)
        masked.append(jnp.where(beats < TOPK_GROUPS, blocks[g], NEG_INF))
    cur = jnp.concatenate(masked, axis=0)
    eio = lax.broadcasted_iota(I32, (N_EXPERTS, tm), 0)
    chosen = jnp.zeros((N_EXPERTS, tm), jnp.bool_)
    wsum = jnp.zeros((1, tm), F32)
    for _ in range(TOP_K):
        m = jnp.max(cur, axis=0, keepdims=True)
        idx = jnp.min(jnp.where(cur == m, eio, N_EXPERTS), axis=0, keepdims=True)
        sel = eio == idx
        chosen = jnp.logical_or(chosen, sel)
        wsum = wsum + jnp.sum(jnp.where(sel, scores, 0.0), axis=0, keepdims=True)
        cur = jnp.where(sel, -jnp.inf, cur)
    wkey_ref[...] = jnp.where(chosen, scores / wsum * ROUTED_SCALE, 0.0)

    onehot = chosen.astype(BF16)
    r_i = lax.broadcasted_iota(I32, (tm, tm), 0)
    c_i = lax.broadcasted_iota(I32, (tm, tm), 1)
    before = (r_i < c_i).astype(BF16)
    prior = _bdot(onehot, before)
    key_ref[...] = jnp.where(chosen, prior, -1.0).astype(I32)
    cnt = jnp.sum(chosen.astype(F32), axis=1, keepdims=True)
    cnt_ref[0] = cnt

    run_p = jnp.floor((cnt + (RUN_ALIGN - 1)) / RUN_ALIGN)
    ppc = SORT_CHUNK // RUN_ALIGN
    tot = jnp.sum(run_p, axis=0, keepdims=True)
    fill = jnp.ceil(tot / ppc) * ppc - tot
    run_p = run_p + jnp.where(lax.broadcasted_iota(I32, (N_EXPERTS, 1), 0) == N_EXPERTS - 1, fill, 0.0)
    e_r = lax.broadcasted_iota(I32, (N_EXPERTS, N_EXPERTS), 0)
    e_c = lax.broadcasted_iota(I32, (N_EXPERTS, N_EXPERTS), 1)
    incl = (e_c <= e_r).astype(BF16)
    lend = _bdot(incl, jnp.broadcast_to(run_p, (N_EXPERTS, LANES)).astype(BF16))[:, 0:1]
    loff = lend - run_p
    pj = lax.broadcasted_iota(I32, (N_EXPERTS, PIECES_PAD), 1).astype(F32)
    er = lax.broadcasted_iota(I32, (N_EXPERTS, PIECES_PAD), 0).astype(F32)
    pe = jnp.minimum(jnp.sum((lend <= pj).astype(F32), axis=0, keepdims=True), N_EXPERTS - 1.0)
    lo = jnp.sum(jnp.where(er == pe, loff, 0.0), axis=0, keepdims=True)
    pe_ref[0] = pe.astype(I32)
    po_ref[0] = ((pj[0:1, :] - lo) * RUN_ALIGN).astype(I32)


N_ROUTE_REFS = 16


def _router_kernel(x_ref, *refs):
    _route(x_ref[0], *refs)


def _mixer_router_kernel(x_ref, *refs, body, n_mix):
    route_refs = refs[n_mix:n_mix + N_ROUTE_REFS]
    x1 = body(x_ref, refs[:n_mix], refs[n_mix + N_ROUTE_REFS:])
    _route(x1, *route_refs)


def _gmlp_mixer(mods, nw, w_in, b_in, ln_g, ln_b, w_s, b_s, w_out):
    sh, sc, gt = mods
    d, n_in = w_in.shape
    aw = n_in // 2
    vec = pl.BlockSpec((1, 1, d), lambda b, j: (b, 0, 0))
    full2 = lambda shape: pl.BlockSpec(shape, lambda b, j: (0, 0))
    specs = [vec, vec, vec, full2((1, d)), full2((d, n_in)), full2((1, n_in)), full2((1, aw)),
             full2((1, aw)), pl.BlockSpec((A_GROUPS, A_CHUNK, A_CHUNK), lambda b, j: (0, 0, 0)),
             full2((A_CHUNK, A_GROUPS)), full2((aw, d))]
    args = [sh, sc, gt, nw.reshape(1, d), w_in.astype(BF16), b_in.reshape(1, n_in),
            ln_g.reshape(1, aw), ln_b.reshape(1, aw), w_s, b_s.T, w_out.astype(BF16)]

    def body(x_ref, refs, scratch):
        return _gmlp_body(x_ref[0], *refs, *scratch)

    return body, specs, args, [pltpu.VMEM((TM_SORT, aw), BF16)]


def _router(x, mods, nw, w_router, e_bias, wsg, wsu, wsd, mixer=None):
    bsz, s, d = x.shape
    t = bsz * s
    tm = TM_SORT
    nt = s // tm
    sd = wsg.shape[1]
    vec = pl.BlockSpec((1, 1, d), lambda b, j: (b, 0, 0))
    full2 = lambda shape: pl.BlockSpec(shape, lambda b, j: (0, 0))
    tok = pl.BlockSpec((tm, d), lambda b, j: (b * nt + j, 0))
    etok = pl.BlockSpec((N_EXPERTS, tm), lambda b, j: (0, b * nt + j))
    ptab = pl.BlockSpec((1, 1, PIECES_PAD), lambda b, j: (b * nt + j, 0, 0))
    sh, sc, gt = mods
    in_specs = [vec, vec, vec, full2((1, d)), full2((N_EXPERTS, d)), full2((N_EXPERTS, 1)),
                full2((d, sd)), full2((d, sd)), full2((sd, d))]
    args = [sh, sc, gt, nw.reshape(1, d), w_router.T, e_bias.reshape(N_EXPERTS, 1),
            wsg.astype(BF16), wsu.astype(BF16), wsd.astype(BF16)]
    kern, scratch = _router_kernel, []
    if mixer is not None:
        body, mspecs, margs, scratch = mixer
        kern = functools.partial(_mixer_router_kernel, body=body, n_mix=len(margs))
        in_specs = mspecs + in_specs
        args = margs + args
    return pl.pallas_call(
        kern,
        grid=(bsz, nt),
        in_specs=[pl.BlockSpec((1, tm, d), lambda b, j: (b, j, 0))] + in_specs,
        out_specs=[tok, tok, etok, etok,
                   pl.BlockSpec((1, N_EXPERTS, 1), lambda b, j: (b * nt + j, 0, 0)), ptab, ptab],
        out_shape=[jax.ShapeDtypeStruct((t, d), BF16), jax.ShapeDtypeStruct((t, d), F32),
                   jax.ShapeDtypeStruct((N_EXPERTS, t), I32), jax.ShapeDtypeStruct((N_EXPERTS, t), F32),
                   jax.ShapeDtypeStruct((t // tm, N_EXPERTS, 1), F32),
                   jax.ShapeDtypeStruct((t // tm, 1, PIECES_PAD), I32),
                   jax.ShapeDtypeStruct((t // tm, 1, PIECES_PAD), I32)],
        scratch_shapes=scratch,
        compiler_params=_cparams(("arbitrary", "arbitrary")),
    )(x, *args)


def _rows_max(t):
    rows = (t // TM_SORT) * ROWS_TILE + N_EXPERTS * (BM_SEG - RUN_ALIGN)
    return -(-rows // BM_EXPERT) * BM_EXPERT


def _sort_meta(cnt, n_rows_max):
    nts = cnt.shape[0]
    c = cnt.reshape(nts, N_EXPERTS).astype(I32)
    run = (c + (RUN_ALIGN - 1)) // RUN_ALIGN * RUN_ALIGN
    fill = (-jnp.sum(run, axis=1, keepdims=True)) % SORT_CHUNK
    run = jnp.concatenate([run[:, :-1], run[:, -1:] + fill], axis=1)
    used = jnp.sum(run, axis=1)
    per_e = jnp.sum(run, axis=0)
    seg = (per_e + (BM_SEG - 1)) // BM_SEG * BM_SEG
    ends = jnp.cumsum(seg)
    starts = ends - seg
    goff = starts[None, :] + jnp.cumsum(run, axis=0) - run
    pad_start = starts + per_e
    pad_cnt = (seg - per_e) // RUN_ALIGN

    nh = n_rows_max // BM_SEG
    hidx = jnp.arange(nh, dtype=I32)
    he = jnp.sum((ends[None, :] <= (hidx * BM_SEG)[:, None]).astype(I32), axis=1)
    real = he < N_EXPERTS
    he = jnp.minimum(he, N_EXPERTS - 1).astype(I32)
    prev = jnp.concatenate([jnp.full((1,), -1, I32), he[:-1]])
    hnew = jnp.logical_and(real, he != prev)
    per_blk = BM_EXPERT // BM_SEG
    hslot = (jnp.cumsum(hnew.astype(I32)) - 1) % per_blk
    later = jnp.logical_and(hidx[None, :] > hidx[:, None], hnew[None, :])
    nxt = jnp.min(jnp.where(later, hidx[None, :], nh), axis=1)
    hfetch = jnp.where(nxt < nh, he[jnp.minimum(nxt, nh - 1)], -1)
    bx = jnp.where(real[::per_blk], jnp.arange(nh // per_blk, dtype=I32), 0)
    i32 = lambda a: a.reshape(-1).astype(I32)
    return ((i32(goff), i32(used), i32(pad_start), i32(pad_cnt)),
            (bx, he, i32(hnew), i32(real), i32(hslot), i32(hfetch)))


def _piece_copies(pe_ref, po_ref, goff_ref, used_ref, tile, local, remote, sem, to_remote):
    def copy(j):
        idx = tile * PIECES_PAD + j
        l0 = pl.multiple_of(j * RUN_ALIGN, RUN_ALIGN)
        g0 = pl.multiple_of(goff_ref[tile * N_EXPERTS + pe_ref[idx]] + po_ref[idx], RUN_ALIGN)
        lref = local.at[pl.ds(l0, RUN_ALIGN)]
        gref = remote.at[pl.ds(g0, RUN_ALIGN)]
        return pltpu.make_async_copy(lref, gref, sem) if to_remote else pltpu.make_async_copy(gref, lref, sem)

    def wait_all():
        def body(j, carry):
            copy(j).wait()
            return carry

        lax.fori_loop(0, used_ref[tile] // RUN_ALIGN, body, 0)

    return copy, wait_all


def _pad_copies(pad_start_ref, pad_cnt_ref, tile, ntiles, zeros, remote, sem):
    share = -(-N_EXPERTS // ntiles)

    def apply(act):
        for q in range(share):
            e = tile * share + q
            ec = jnp.minimum(e, N_EXPERTS - 1)
            n = jnp.where(e < N_EXPERTS, pad_cnt_ref[ec], 0)
            base = pad_start_ref[ec]

            def body(j, carry):
                g0 = pl.multiple_of(base + j * RUN_ALIGN, RUN_ALIGN)
                getattr(pltpu.make_async_copy(zeros, remote.at[pl.ds(g0, RUN_ALIGN)], sem), act)()
                return carry

            lax.fori_loop(0, n, body, 0)

    return apply


def _piece_rows(pe_ref, po_ref, key_ref, first_piece, npieces, val_ref=None):
    tm = key_ref.shape[1]
    sub = lax.broadcasted_iota(I32, (RUN_ALIGN, tm), 0)
    out = []
    for jj in range(npieces):
        j = first_piece + jj
        e = pe_ref[j]
        hit = (key_ref[pl.ds(e, 1), :] - po_ref[j]) == sub
        val = 1.0 if val_ref is None else val_ref[pl.ds(e, 1), :]
        out.append(jnp.where(hit, val, 0.0))
    return out


def _dispatch_kernel(pe_ref, po_ref, goff_ref, used_ref, pad_start_ref, pad_cnt_ref,
                     key_ref, h_ref, xs_ref, xbuf, p_sc, zbuf, sem, zsem, *, ntiles):
    i = pl.program_id(0)
    hb = h_ref[...]
    ch = SORT_CHUNK
    ppc = ch // RUN_ALIGN
    zbuf[...] = jnp.zeros_like(zbuf)
    pads = _pad_copies(pad_start_ref, pad_cnt_ref, i, ntiles, zbuf, xs_ref, zsem)
    pads("start")

    copy, wait_all = _piece_copies(pe_ref, po_ref, goff_ref, used_ref, i, xbuf, xs_ref, sem, True)

    def build(ci):
        rows = _piece_rows(pe_ref, po_ref, key_ref, i * PIECES_PAD + ci * ppc, ppc)
        for jj, p in enumerate(rows):
            p_sc[jj * RUN_ALIGN:(jj + 1) * RUN_ALIGN, :] = p.astype(BF16)
        xbuf[pl.ds(pl.multiple_of(ci * ch, ch), ch), :] = _bdot(p_sc[...], hb).astype(BF16)

    def send(ci):
        for jj in range(ppc):
            copy(ci * ppc + jj).start()

    def step(ci, carry):
        send(ci - 1)
        build(ci)
        return carry

    nchunk = used_ref[i] // ch
    build(0)
    lax.fori_loop(1, nchunk, step, 0)
    send(nchunk - 1)
    pads("wait")
    wait_all()


def _dispatch(meta, pe, po, key, hb, n_rows_max):
    t, d = hb.shape
    tm = TM_SORT
    goff, used, pad_start, pad_cnt = meta
    return pl.pallas_call(
        functools.partial(_dispatch_kernel, ntiles=t // tm),
        grid_spec=pltpu.PrefetchScalarGridSpec(
            num_scalar_prefetch=6, grid=(t // tm,),
            in_specs=[pl.BlockSpec((N_EXPERTS, tm), lambda i, *_: (0, i)),
                      pl.BlockSpec((tm, d), lambda i, *_: (i, 0))],
            out_specs=pl.BlockSpec(memory_space=pl.ANY),
            scratch_shapes=[pltpu.VMEM((ROWS_TILE, d), BF16), pltpu.VMEM((SORT_CHUNK, tm), BF16),
                            pltpu.VMEM((RUN_ALIGN, d), BF16),
                            pltpu.SemaphoreType.DMA(()), pltpu.SemaphoreType.DMA(())]),
        out_shape=jax.ShapeDtypeStruct((n_rows_max, d), BF16),
        compiler_params=_cparams(("arbitrary",)),
    )(pe, po, goff, used, pad_start, pad_cnt, key, hb)


def _expert_kernel(bx_ref, he_ref, hnew_ref, hreal_ref, hslot_ref, hfetch_ref, x_ref, wg_hbm, wu_hbm, wd_hbm,
                   o_ref, wg_st, wu_st, wd_st, wg_sc, wu_sc, wd_sc, sem, *, layer):
    b = pl.program_id(0)
    per_blk = BM_EXPERT // BM_SEG
    cpp = BM_SEG // BM_CHAIN

    def fetch(e):
        return [pltpu.make_async_copy(src.at[layer, e], dst, sem.at[i])
                for i, (src, dst) in enumerate(((wg_hbm, wg_st), (wu_hbm, wu_st), (wd_hbm, wd_st)))]

    @pl.when(b == 0)
    def _():
        for cp in fetch(he_ref[0]):
            cp.start()

    for part in range(per_blk):
        h = b * per_blk + part

        @pl.when(hnew_ref[h] == 1)
        def _():
            for cp in fetch(he_ref[h]):
                cp.wait()
            s = hslot_ref[h]
            wg_sc[s] = wg_st[...].astype(BF16)
            wu_sc[s] = wu_st[...].astype(BF16)
            wd_sc[s] = wd_st[...].astype(BF16)

            @pl.when(hfetch_ref[h] >= 0)
            def _():
                for cp in fetch(hfetch_ref[h]):
                    cp.start()

    def run_parts(parts):
        every = [(slice((p * cpp + c) * BM_CHAIN, (p * cpp + c + 1) * BM_CHAIN), hslot_ref[b * per_blk + p])
                 for p in parts for c in range(cpp)]
        for g0 in range(0, len(every), EXPERT_CHAINS):
            chains = every[g0:g0 + EXPERT_CHAINS]
            xs = [x_ref[rs, :] for rs, _ in chains]
            gs = [_bdot(x, wg_sc[s]) for x, (_, s) in zip(xs, chains)]
            us = [_bdot(x, wu_sc[s]) for x, (_, s) in zip(xs, chains)]
            acts = [(_silu(g) * u).astype(BF16) for g, u in zip(gs, us)]
            for (rs, s), act in zip(chains, acts):
                o_ref[rs, :] = _bdot(act, wd_sc[s]).astype(BF16)

    nlive = hreal_ref[b * per_blk]
    for part in range(1, per_blk):
        nlive = nlive + hreal_ref[b * per_blk + part]

    @pl.when(nlive == per_blk)
    def _():
        run_parts(range(per_blk))

    @pl.when(nlive < per_blk)
    def _():
        for part in range(per_blk):
            @pl.when(part < nlive)
            def _():
                run_parts([part])

            @pl.when(part >= nlive)
            def _():
                o_ref[part * BM_SEG:(part + 1) * BM_SEG, :] = jnp.zeros((BM_SEG, o_ref.shape[1]), BF16)


def _experts(meta, xs, layer, w_gate, w_up, w_down):
    r, d = xs.shape
    ed = w_gate.shape[-1]
    bm = BM_EXPERT
    nslot = BM_EXPERT // BM_SEG
    omap = lambda b, bx, *_: (b, 0)
    xmap = lambda b, bx, *_: (bx[b], 0)
    hbm = pl.BlockSpec(memory_space=pl.ANY)
    return pl.pallas_call(
        functools.partial(_expert_kernel, layer=layer),
        grid_spec=pltpu.PrefetchScalarGridSpec(
            num_scalar_prefetch=6, grid=(r // bm,),
            in_specs=[pl.BlockSpec((bm, d), xmap), hbm, hbm, hbm],
            out_specs=pl.BlockSpec((bm, d), omap),
            scratch_shapes=[pltpu.VMEM((d, ed), F32), pltpu.VMEM((d, ed), F32), pltpu.VMEM((ed, d), F32),
                            pltpu.VMEM((nslot, d, ed), BF16), pltpu.VMEM((nslot, d, ed), BF16),
                            pltpu.VMEM((nslot, ed, d), BF16), pltpu.SemaphoreType.DMA((3,))]),
        out_shape=jax.ShapeDtypeStruct((r, d), BF16),
        compiler_params=_cparams(("arbitrary",)),
    )(*meta, xs, w_gate, w_up, w_down)


def _combine_kernel(pe_ref, po_ref, goff_ref, used_ref, key_ref, wkey_ref, ys_ref, base_ref, g_ref,
                    fn_ref, *rest, final, with_kv):
    if with_kv:
        kv_in, o_ref, kv_out, (ybuf, q_sc, sem) = rest[:5], rest[5], rest[6:9], rest[9:]
    else:
        o_ref, ybuf, q_sc, sem = rest
    i = pl.program_id(0)
    tm = base_ref.shape[0]
    ch = SORT_CHUNK
    ppc = ch // RUN_ALIGN

    @pl.when(i == 0)
    def _():
        ybuf[...] = jnp.zeros_like(ybuf)

    copy, wait_all = _piece_copies(pe_ref, po_ref, goff_ref, used_ref, i, ybuf, ys_ref, sem, False)

    for ci in range(ybuf.shape[0] // ch):
        cs = slice(ci * ch, (ci + 1) * ch)

        @pl.when(ci * ch < used_ref[i])
        def _():
            for jj in range(ppc):
                copy(ci * ppc + jj).start()
            rows = _piece_rows(pe_ref, po_ref, key_ref, i * PIECES_PAD + ci * ppc, ppc, wkey_ref)
            for jj, row in enumerate(rows):
                q_sc[ci * ch + jj * RUN_ALIGN:ci * ch + (jj + 1) * RUN_ALIGN, :] = row.astype(BF16)

        @pl.when(ci * ch >= used_ref[i])
        def _():
            q_sc[cs, :] = jnp.zeros((ch, tm), BF16)

    wait_all()
    out = base_ref[...] + g_ref[0] * lax.dot_general(q_sc[...], ybuf[...], (((0,), (0,)), ((), ())),
                                                     preferred_element_type=F32)
    if final:
        out = _rms(out, fn_ref[...])
    o_ref[...] = out
    if with_kv:
        _kv_body(out, *kv_in, *kv_out)


def _combine(meta, pe, po, key, wkey, ys, base, gt, fnorm, seq, final, kv=None):
    t, d = base.shape
    tm = TM_SORT
    nt = seq // tm
    bsz = t // seq
    vecb = pl.BlockSpec((1, 1, d), lambda i, *_: (i // nt, 0, 0))
    full2 = lambda shape: pl.BlockSpec(shape, lambda i, *_: (0, 0))
    in_specs = [pl.BlockSpec((N_EXPERTS, tm), lambda i, *_: (0, i)),
                pl.BlockSpec((N_EXPERTS, tm), lambda i, *_: (0, i)),
                pl.BlockSpec(memory_space=pl.ANY),
                pl.BlockSpec((tm, d), lambda i, *_: (i, 0)), vecb, full2((1, d))]
    args = [key, wkey, ys, base, gt, fnorm.reshape(1, d)]
    out_specs = [pl.BlockSpec((tm, d), lambda i, *_: (i, 0))]
    out_shape = [jax.ShapeDtypeStruct((t, d), F32)]
    if kv is not None:
        ksh, ksc, knw, w_k, w_v = kv
        nb, npair, per = seq // B_BLOCK, d // LANES, tm // B_BLOCK
        in_specs += [vecb, vecb, full2((1, d)), full2((d, d)), full2((d, d))]
        args += [ksh, ksc, knw.reshape(1, d), w_k.astype(BF16), w_v.T.astype(BF16)]
        out_specs += [pl.BlockSpec((1, npair, per, B_BLOCK, LANES), lambda i, *_: (i // nt, 0, i % nt, 0, 0)),
                      pl.BlockSpec((1, npair, per, LANES, B_BLOCK), lambda i, *_: (i // nt, 0, i % nt, 0, 0)),
                      pl.BlockSpec((1, per, 1, d), lambda i, *_: (i // nt, i % nt, 0, 0))]
        out_shape += [jax.ShapeDtypeStruct((bsz, npair, nb, B_BLOCK, LANES), BF16),
                      jax.ShapeDtypeStruct((bsz, npair, nb, LANES, B_BLOCK), BF16),
                      jax.ShapeDtypeStruct((bsz, nb, 1, d), F32)]
    res = pl.pallas_call(
        functools.partial(_combine_kernel, final=final, with_kv=kv is not None),
        grid_spec=pltpu.PrefetchScalarGridSpec(
            num_scalar_prefetch=4, grid=(t // tm,),
            in_specs=in_specs, out_specs=out_specs,
            scratch_shapes=[pltpu.VMEM((ROWS_TILE, d), BF16), pltpu.VMEM((ROWS_TILE, tm), BF16),
                            pltpu.SemaphoreType.DMA(())]),
        out_shape=out_shape,
        compiler_params=_cparams(("arbitrary",)),
    )(pe, po, meta[0], meta[1], *args)
    return res if kv is not None else res[0]


def _moe(x, mods, nw, w_router, e_bias, layer, w_gate, w_up, w_down, wsg, wsu, wsd, fnorm, final, mixer, kv=None):
    bsz, s, d = x.shape
    t = bsz * s
    gt = mods[2]
    n_rows_max = _rows_max(t)
    hb, base, key, wkey, cnt, pe, po = _router(x, mods, nw, w_router, e_bias, wsg, wsu, wsd, mixer)
    pe = pe.reshape(-1)
    po = po.reshape(-1)
    layout_meta, block_meta = _sort_meta(cnt, n_rows_max)
    xs = _dispatch(layout_meta, pe, po, key, hb, n_rows_max)
    ys = _experts(block_meta, xs, layer, w_gate, w_up, w_down)
    out = _combine(layout_meta, pe, po, key, wkey, ys, base, gt, fnorm, s, final, kv)
    if kv is not None:
        return (out[0].reshape(bsz, s, d),) + tuple(out[1:])
    return out.reshape(bsz, s, d)


def _kv_body(x, sh_ref, sc_ref, nw_ref, wk_ref, wvt_ref, k_ref, vt_ref, km_ref):
    hb = (_rms(x, nw_ref[...]) * (1.0 + sc_ref[0]) + sh_ref[0]).astype(BF16)
    k = _bdot(hb, wk_ref[...])
    vt = _bdot_nt(wvt_ref[...], hb)
    for blk in range(x.shape[0] // B_BLOCK):
        rs = slice(blk * B_BLOCK, (blk + 1) * B_BLOCK)
        for p in range(k.shape[1] // LANES):
            k_ref[0, p, blk] = k[rs, p * LANES:(p + 1) * LANES].astype(BF16)
            vt_ref[0, p, blk] = vt[p * LANES:(p + 1) * LANES, rs].astype(BF16)
        km_ref[0, blk] = jnp.mean(k[rs], axis=0, keepdims=True)


def _kv_kernel(x_ref, *refs):
    _kv_body(x_ref[0], *refs)


def _kv(x, sh, sc, nw, w_k, w_v):
    bsz, s, d = x.shape
    nb = s // B_BLOCK
    npair = d // LANES
    vec = pl.BlockSpec((1, 1, d), lambda b, j: (b, 0, 0))
    full2 = lambda shape: pl.BlockSpec(shape, lambda b, j: (0, 0))
    return pl.pallas_call(
        _kv_kernel,
        grid=(bsz, nb),
        in_specs=[pl.BlockSpec((1, B_BLOCK, d), lambda b, j: (b, j, 0)), vec, vec,
                  full2((1, d)), full2((d, d)), full2((d, d))],
        out_specs=[pl.BlockSpec((1, npair, 1, B_BLOCK, LANES), lambda b, j: (b, 0, j, 0, 0)),
                   pl.BlockSpec((1, npair, 1, LANES, B_BLOCK), lambda b, j: (b, 0, j, 0, 0)),
                   pl.BlockSpec((1, 1, 1, d), lambda b, j: (b, j, 0, 0))],
        out_shape=[jax.ShapeDtypeStruct((bsz, npair, nb, B_BLOCK, LANES), BF16),
                   jax.ShapeDtypeStruct((bsz, npair, nb, LANES, B_BLOCK), BF16),
                   jax.ShapeDtypeStruct((bsz, nb, 1, d), F32)],
        compiler_params=_cparams(("arbitrary", "arbitrary")),
    )(x, sh, sc, nw.reshape(1, d), w_k.astype(BF16), w_v.T.astype(BF16))


def _attn_body(x, qb, sh_ref, sc_ref, g_ref, nw_ref, wqt_ref, wo_ref, k_ref, vt_ref, km_ref,
               qt_sc, qs_sc, acc_sc, sel_sc, m_sc, l_sc, *, nb, n_sel):
    bq = x.shape[0]
    npair = qt_sc.shape[0]
    nbp = km_ref.shape[2]
    hd = LANES // 2
    scale = float(hd) ** -0.5 * 1.4426950408889634
    h = _rms(x, nw_ref[...]) * (1.0 + sc_ref[0]) + sh_ref[0]
    qt = _bdot_nt(wqt_ref[...], h.astype(BF16))
    for p in range(npair):
        qt_sc[p] = qt[p * LANES:(p + 1) * LANES, :]

    subn = lax.broadcasted_iota(I32, (nbp, bq), 0)
    past = subn < qb
    krow = lax.broadcasted_iota(I32, (B_BLOCK, bq), 0)
    qcol = lax.broadcasted_iota(I32, (B_BLOCK, bq), 1)
    causal = krow <= qcol
    rowh = lax.broadcasted_iota(I32, (LANES, 1), 0)

    grp = ATTN_PAIR_UNROLL
    heads = [(u, e) for u in range(grp) for e in range(2)]
    ones_rows = jnp.ones((SUM_ROWS, B_BLOCK), BF16)

    def vsum(vt2, e):
        return jnp.concatenate([vt2[e * hd:(e + 1) * hd, :], ones_rows], axis=0)

    def own_body(gi, carry):
        ps = [gi * grp + u for u in range(grp)]
        q2ts = [qt_sc[p] for p in ps]
        kms = [km_ref[0, p] for p in ps]
        kown = [k_ref[0, p, qb] for p in ps]
        vown = [vt_ref[0, p, qb] for p in ps]
        qets = [jnp.where((rowh >= hd) if e == 1 else (rowh < hd), q2ts[u], 0.0) for u, e in heads]
        qsts = [(q * scale).astype(BF16) for q in qets]
        ss = [jnp.where(causal, _bdot(kown[u], qsts[i]), NEG_INF) for i, (u, e) in enumerate(heads)]
        gates = [_dot3(kms[u], qets[i]) for i, (u, e) in enumerate(heads)]
        ms = [jnp.max(s, axis=0, keepdims=True) for s in ss]
        pes = [jnp.exp2(s - m) for s, m in zip(ss, ms)]
        pvs = [_bdot(vsum(vown[u], e), pes[i].astype(BF16)) for i, (u, e) in enumerate(heads)]
        accs = [pv[:hd] for pv in pvs]
        ls = [pv[hd:hd + 1] for pv in pvs]
        sels = []
        for gate in gates:
            selt = jnp.zeros((nbp, bq), F32)
            for n in range(nb):
                gn = gate[n:n + 1, :]
                beats = jnp.logical_or(gate > gn, jnp.logical_and(gate == gn, subn < n))
                beats = jnp.logical_and(beats, past)
                cnt = jnp.sum(beats.astype(F32), axis=0, keepdims=True)
                selt = jnp.where(subn == n, (cnt < n_sel).astype(F32), selt)
            sels.append(selt)
        for i, (u, e) in enumerate(heads):
            p = ps[u]
            sel_sc[p, e] = sels[i]
            qs_sc[p, e] = qsts[i]
            m_sc[p, e] = ms[i]
            l_sc[p, e] = ls[i]
            acc_sc[p, e * hd:(e + 1) * hd, :] = accs[i]
        return carry

    lax.fori_loop(0, npair // grp, own_body, 0)

    grp = ATTN_PAIR_UNROLL_PAST
    heads = [(u, e) for u in range(grp) for e in range(2)]

    def kb_body(kb, carry):
        def group_body(gi, c2):
            ps = [gi * grp + u for u in range(grp)]
            kbl = [k_ref[0, p, kb] for p in ps]
            vbl = [vt_ref[0, p, kb] for p in ps]
            qsts = [qs_sc[ps[u], e] for u, e in heads]
            rows = [sel_sc[ps[u], e, pl.ds(kb, 1), :] for u, e in heads]
            m_old = [m_sc[ps[u], e] for u, e in heads]
            l_old = [l_sc[ps[u], e] for u, e in heads]
            a_old = [acc_sc[ps[u], e * hd:(e + 1) * hd, :] for u, e in heads]
            ss = [jnp.where(rows[i] > 0.5, _bdot(kbl[u], qsts[i]), NEG_INF)
                  for i, (u, e) in enumerate(heads)]
            m_new = [jnp.maximum(m, jnp.max(s, axis=0, keepdims=True)) for m, s in zip(m_old, ss)]
            alphas = [jnp.exp2(m - mn) for m, mn in zip(m_old, m_new)]
            pes = [jnp.exp2(s - mn) for s, mn in zip(ss, m_new)]
            pvs = [_bdot(vsum(vbl[u], e), pes[i].astype(BF16)) for i, (u, e) in enumerate(heads)]
            l_new = [a * l + pv[hd:hd + 1] for a, l, pv in zip(alphas, l_old, pvs)]
            a_new = [a * ao + pv[:hd] for a, ao, pv in zip(alphas, a_old, pvs)]
            for i, (u, e) in enumerate(heads):
                p = ps[u]
                m_sc[p, e] = m_new[i]
                l_sc[p, e] = l_new[i]
                acc_sc[p, e * hd:(e + 1) * hd, :] = a_new[i]
            return c2

        lax.fori_loop(0, npair // grp, group_body, 0)
        return carry

    lax.fori_loop(0, qb, kb_body, 0)

    parts = []
    for p in range(npair):
        for e in range(2):
            parts.append(acc_sc[p, e * hd:(e + 1) * hd, :] / l_sc[p, e])
    ot = jnp.concatenate(parts, axis=0)
    return x + g_ref[0] * _bdot(ot.T.astype(BF16), wo_ref[...])


def _attn_mixer(mods, nw, w_q, w_o, k5, vt5, km2):
    sh, sc, gt = mods
    d = w_q.shape[0]
    npair, nb = k5.shape[1], k5.shape[2]
    nbp = km2.shape[2]
    n_sel = min(B_TOPK, nb - 1)
    per = TM_SORT // B_BLOCK
    vec = pl.BlockSpec((1, 1, d), lambda b, j: (b, 0, 0))
    full2 = lambda shape: pl.BlockSpec(shape, lambda b, j: (0, 0))
    specs = [vec, vec, vec, full2((1, d)), full2((d, d)), full2((d, d)),
             pl.BlockSpec((1, npair, nb, B_BLOCK, LANES), lambda b, j: (b, 0, 0, 0, 0)),
             pl.BlockSpec((1, npair, nb, LANES, B_BLOCK), lambda b, j: (b, 0, 0, 0, 0)),
             pl.BlockSpec((1, npair, nbp, LANES), lambda b, j: (b, 0, 0, 0))]
    args = [sh, sc, gt, nw.reshape(1, d), w_q.T.astype(BF16), w_o.astype(BF16), k5, vt5, km2]
    scratch = [pltpu.VMEM((npair, LANES, B_BLOCK), F32),
               pltpu.VMEM((npair, 2, LANES, B_BLOCK), BF16),
               pltpu.VMEM((npair, LANES, B_BLOCK), F32),
               pltpu.VMEM((npair, 2, nbp, B_BLOCK), F32),
               pltpu.VMEM((npair, 2, 1, B_BLOCK), F32),
               pltpu.VMEM((npair, 2, 1, B_BLOCK), F32)]

    def body(x_ref, refs, scr):
        j = pl.program_id(1)
        outs = [_attn_body(x_ref[0, h * B_BLOCK:(h + 1) * B_BLOCK, :], j * per + h, *refs, *scr,
                           nb=nb, n_sel=n_sel) for h in range(per)]
        return jnp.concatenate(outs, axis=0)

    return body, specs, args, scratch


def kernel(x, c, ada_w, ada_b, norm_mix, norm_ffn, a_w_in, a_b_in, a_ln_g, a_ln_b, a_w_s, a_b_s,
           a_w_out, kv_norm, kv_ada_w, kv_ada_b, kv_w_k, kv_w_v, b_w_q, b_w_o, moe_router, moe_bias,
           moe_w_gate, moe_w_up, moe_w_down, sh_w_gate, sh_w_up, sh_w_down, final_norm):
    bsz, s, d = x.shape
    depth = ada_w.shape[0]
    n_a = a_w_in.shape[0]
    assert s % B_BLOCK == 0 and s % TM_SORT == 0 and TM_SORT % A_CHUNK == 0 and d % LANES == 0
    assert d == B_HEADS * (LANES // 2), "attention packs two 64-wide heads per 128 lanes"
    nb = s // B_BLOCK
    npair = d // LANES
    nbp = -(-nb // 8) * 8

    def split(m, n):
        return [m[:, i * d:(i + 1) * d].reshape(bsz, 1, d) for i in range(n)]

    def pair_major(km):
        km = km.reshape(bsz, nb, npair, LANES).transpose(0, 2, 1, 3)
        return jnp.pad(km, ((0, 0), (0, 0), (0, nbp - nb), (0, 0)))

    layer_mods = _ada(c, ada_w, ada_b)
    k5 = vt5 = km2 = None
    for i in range(depth):
        sh1, sc1, g1, sh2, sc2, g2 = split(layer_mods[i], 6)
        if i < n_a:
            mixer = _gmlp_mixer((sh1, sc1, g1), norm_mix[i], a_w_in[i], a_b_in[i], a_ln_g[i], a_ln_b[i],
                                a_w_s[i], a_b_s[i], a_w_out[i])
        else:
            if k5 is None:
                ksh, ksc = split(_ada(c, kv_ada_w[None], kv_ada_b[None])[0], 2)
                k5, vt5, km = _kv(x, ksh, ksc, kv_norm, kv_w_k, kv_w_v)
                km2 = pair_major(km)
            j = i - n_a
            mixer = _attn_mixer((sh1, sc1, g1), norm_mix[i], b_w_q[j], b_w_o[j], k5, vt5, km2)
        kv = None
        if i == n_a - 1 and i + 1 < depth:
            ksh, ksc = split(_ada(c, kv_ada_w[None], kv_ada_b[None])[0], 2)
            kv = (ksh, ksc, kv_norm, kv_w_k, kv_w_v)
        res = _moe(x, (sh2, sc2, g2), norm_ffn[i], moe_router[i], moe_bias[i], i, moe_w_gate,
                   moe_w_up, moe_w_down, sh_w_gate[i], sh_w_up[i], sh_w_down[i],
                   final_norm, i == depth - 1, mixer, kv)
        if kv is not None:
            x, k5, vt5, km = res
            km2 = pair_major(km)
        else:
            x = res
    return x
```
